```python
import math
import jax, jax.numpy as jnp
from jax import lax
import numpy as np

D_MODEL = 1024
BATCH = 8
SEQ = 16384
DEPTH = 4

CHUNK = 64
MIX_A = D_MODEL // 2
MIX_B = D_MODEL // 2
MIX_C = D_MODEL // 2
MIX_D = D_MODEL // 2
S5_GROUP = 16
S5_GROUPS = MIX_A // S5_GROUP
S5_STATE = 64
CONV_W = 3
ATT_HEADS = 8
HEAD_DIM = MIX_C // ATT_HEADS
LEFT_CHUNKS = 8
BAND = (LEFT_CHUNKS + 1) * CHUNK
MAX_REL = 128
POOL_WINDOWS = (2, 4, 8, 16)
POOL_GROUP = MIX_D // len(POOL_WINDOWS)
D_FF = ((math.ceil(8 * D_MODEL / 3) + 255) // 256) * 256
D_PLE = 256
EV_IN = MIX_A + 3 * MIX_B
OD_IN = 3 * MIX_C + MIX_D
N_EVEN = (DEPTH + 1) // 2
N_ODD = DEPTH // 2
ALPHA = (2 * DEPTH) ** 0.25
BETA = (8 * DEPTH) ** -0.25
LN_EPS = 1e-5
NEG_INF = -1e30

kernel_name = "hybrid_s5_conv_chunkattn_pool_deepnorm"


def layer_norm(x, g, b):
    xf = x.astype(jnp.float32)
    mu = jnp.mean(xf, axis=-1, keepdims=True)
    var = jnp.mean(jnp.square(xf - mu), axis=-1, keepdims=True)
    y = (xf - mu) * lax.rsqrt(var + LN_EPS)
    return (y * g.astype(jnp.float32) + b.astype(jnp.float32)).astype(x.dtype)


def _complex_affine_combine(e1, e2):
    a1r, a1i, b1r, b1i = e1
    a2r, a2i, b2r, b2i = e2
    ar = a2r * a1r - a2i * a1i
    ai = a2r * a1i + a2i * a1r
    br = a2r * b1r - a2i * b1i + b2r
    bi = a2r * b1i + a2i * b1r + b2i
    return (ar, ai, br, bi)


def s5_mixer(u, lam_re, lam_im, log_dt, b_re, b_im, c_re, c_im, d_skip, w_glu, b_glu):
    f32 = jnp.float32
    bsz, L, _ = u.shape
    lre = lam_re.astype(f32)
    lim = lam_im.astype(f32)
    dt = jnp.exp(log_dt.astype(f32))[:, None]
    mag = jnp.exp(lre * dt)
    ang = lim * dt
    lb_re = mag * jnp.cos(ang)
    lb_im = mag * jnp.sin(ang)
    den = lre * lre + lim * lim
    nr = lb_re - 1.0
    ni = lb_im
    r_re = (nr * lre + ni * lim) / den
    r_im = (ni * lre - nr * lim) / den
    br = b_re.astype(f32)
    bi = b_im.astype(f32)
    bb_re = r_re[..., None] * br - r_im[..., None] * bi
    bb_im = r_re[..., None] * bi + r_im[..., None] * br
    uf = u.astype(f32).reshape(bsz, L, S5_GROUPS, S5_GROUP)
    bu_re = jnp.einsum('blgh,gph->blgp', uf, bb_re)
    bu_im = jnp.einsum('blgh,gph->blgp', uf, bb_im)
    a_re = jnp.broadcast_to(lb_re, bu_re.shape)
    a_im = jnp.broadcast_to(lb_im, bu_im.shape)
    _, _, xr, xi = lax.associative_scan(_complex_affine_combine, (a_re, a_im, bu_re, bu_im), axis=1)
    y = (jnp.einsum('ghp,blgp->blgh', c_re.astype(f32), xr)
         - jnp.einsum('ghp,blgp->blgh', c_im.astype(f32), xi)
         + d_skip.astype(f32).reshape(S5_GROUPS, S5_GROUP) * uf)
    y = y.reshape(bsz, L, MIX_A)
    g = jax.nn.gelu(y)
    out = g * jax.nn.sigmoid(g @ w_glu.astype(f32) + b_glu.astype(f32))
    return out.astype(u.dtype)


def short_conv_mixer(b_gate, c_gate, x_in, conv_w):
    L = x_in.shape[1]
    z = c_gate * x_in
    zp = jnp.pad(z, ((0, 0), (CONV_W - 1, 0), (0, 0)))
    y = conv_w[0] * zp[:, 0:L]
    for k in range(1, CONV_W):
        y = y + conv_w[k] * zp[:, k:k + L]
    return b_gate * y


def chunk_attention(q, k, v, rel_bias):
    bsz, L, _ = q.shape
    nc = L // CHUNK
    q = q.reshape(bsz, nc, CHUNK, ATT_HEADS, HEAD_DIM) * (HEAD_DIM ** -0.5)
    k = k.reshape(bsz, nc, CHUNK, ATT_HEADS, HEAD_DIM)
    v = v.reshape(bsz, nc, CHUNK, ATT_HEADS, HEAD_DIM)
    pad = ((0, 0), (LEFT_CHUNKS, 0), (0, 0), (0, 0), (0, 0))
    kp = jnp.pad(k, pad)
    vp = jnp.pad(v, pad)
    kb = jnp.concatenate([kp[:, j:j + nc] for j in range(LEFT_CHUNKS + 1)], axis=2)
    vb = jnp.concatenate([vp[:, j:j + nc] for j in range(LEFT_CHUNKS + 1)], axis=2)
    s = jnp.einsum('bcqhd,bckhd->bchqk', q, kb).astype(jnp.float32)
    qi = jnp.arange(CHUNK)[:, None]
    kj = jnp.arange(BAND)[None, :]
    rel = jnp.clip(qi + LEFT_CHUNKS * CHUNK - kj, -MAX_REL, MAX_REL) + MAX_REL
    bias = rel_bias.astype(jnp.float32)[:, rel]
    key_chunk = (jnp.arange(BAND) // CHUNK)[None, :] - LEFT_CHUNKS
    valid = (jnp.arange(nc)[:, None] + key_chunk) >= 0
    s = jnp.where(valid[None, :, None, None, :], s + bias[None, None], NEG_INF)
    pr = jax.nn.softmax(s, axis=-1).astype(vb.dtype)
    o = jnp.einsum('bchqk,bckhd->bcqhd', pr, vb)
    return o.reshape(bsz, L, MIX_C)


def pool_mixer(z, pool_w, pool_scale):
    bsz, L, _ = z.shape
    zf = z.astype(jnp.float32)
    cs = jnp.cumsum(zf, axis=1)
    t = jnp.arange(L)
    outs = []
    for gi, w in enumerate(POOL_WINDOWS):
        lo, hi = gi * POOL_GROUP, (gi + 1) * POOL_GROUP
        csg = cs[..., lo:hi]
        lagged = jnp.pad(csg[:, :L - w], ((0, 0), (w, 0), (0, 0)))
        count = jnp.minimum(t + 1, w).astype(jnp.float32)[None, :, None]
        outs.append((csg - lagged) / count - zf[..., lo:hi])
    pooled = jnp.stack(outs, axis=2)
    mixed = jnp.einsum('blgc,gcd->blgd', pooled, pool_w.astype(jnp.float32)).reshape(bsz, L, MIX_D)
    return (mixed * pool_scale.astype(jnp.float32)).astype(z.dtype)


def swiglu(x, w_up, w_down):
    h = x @ w_up
    g, u = jnp.split(h, 2, axis=-1)
    return (jax.nn.silu(g) * u) @ w_down


def _fwd_setup_inputs(seed: int = 0) -> dict:
    key = jax.random.key(seed)
    ks = jax.random.split(key, 32)
    f32 = jnp.float32

    def nrm(k, shape, scale):
        return scale * jax.random.normal(k, shape, f32)

    x = nrm(ks[0], (BATCH, SEQ, D_MODEL), 1.0)
    p = nrm(ks[1], (DEPTH, BATCH, SEQ, D_PLE), 1.0)
    ev_w_in = nrm(ks[2], (N_EVEN, D_MODEL, EV_IN), D_MODEL ** -0.5)
    n_idx = jnp.arange(S5_STATE, dtype=f32)
    ev_lambda_re = -0.5 + nrm(ks[3], (N_EVEN, S5_GROUPS, S5_STATE), 0.01)
    ev_lambda_im = math.pi * n_idx + nrm(ks[4], (N_EVEN, S5_GROUPS, S5_STATE), 0.01)
    ev_log_dt = jax.random.uniform(ks[5], (N_EVEN, S5_GROUPS), f32, math.log(1e-3), math.log(1e-1))
    ev_b_re = nrm(ks[6], (N_EVEN, S5_GROUPS, S5_STATE, S5_GROUP), (2 * S5_GROUP) ** -0.5)
    ev_b_im = nrm(ks[7], (N_EVEN, S5_GROUPS, S5_STATE, S5_GROUP), (2 * S5_GROUP) ** -0.5)
    ev_c_re = nrm(ks[8], (N_EVEN, S5_GROUPS, S5_GROUP, S5_STATE), S5_STATE ** -0.5)
    ev_c_im = nrm(ks[9], (N_EVEN, S5_GROUPS, S5_GROUP, S5_STATE), S5_STATE ** -0.5)
    ev_d = nrm(ks[10], (N_EVEN, MIX_A), 1.0)
    ev_w_glu = nrm(ks[11], (N_EVEN, MIX_A, MIX_A), MIX_A ** -0.5)
    ev_b_glu = nrm(ks[12], (N_EVEN, MIX_A), 0.02)
    ev_conv_w = nrm(ks[13], (N_EVEN, CONV_W, MIX_B), CONV_W ** -0.5)
    ev_w_out = nrm(ks[14], (N_EVEN, D_MODEL, D_MODEL), BETA * D_MODEL ** -0.5)
    od_w_in = nrm(ks[15], (N_ODD, D_MODEL, OD_IN), D_MODEL ** -0.5)
    od_rel_bias = nrm(ks[16], (N_ODD, ATT_HEADS, 2 * MAX_REL + 1), 0.1)
    od_pool_w = nrm(ks[17], (N_ODD, len(POOL_WINDOWS), POOL_GROUP, POOL_GROUP), POOL_GROUP ** -0.5)
    od_pool_scale = 1.0 + nrm(ks[18], (N_ODD, MIX_D), 0.1)
    od_w_out = nrm(ks[19], (N_ODD, D_MODEL, D_MODEL), BETA * D_MODEL ** -0.5)
    ln_mix_g = 1.0 + nrm(ks[20], (DEPTH, D_MODEL), 0.01)
    ln_mix_b = nrm(ks[21], (DEPTH, D_MODEL), 0.01)
    ln_ffn_g = 1.0 + nrm(ks[22], (DEPTH, D_MODEL), 0.01)
    ln_ffn_b = nrm(ks[23], (DEPTH, D_MODEL), 0.01)
    ffn_w_up = nrm(ks[24], (DEPTH, D_MODEL, 2 * D_FF), D_MODEL ** -0.5)
    ffn_w_down = nrm(ks[25], (DEPTH, D_FF, D_MODEL), BETA * D_FF ** -0.5)
    ple_w_proj = nrm(ks[26], (DEPTH, D_PLE, D_MODEL), D_PLE ** -0.5)
    ple_w_gate = nrm(ks[27], (DEPTH, D_MODEL, D_MODEL), D_MODEL ** -0.5)
    ple_b_gate = nrm(ks[28], (DEPTH, D_MODEL), 0.01)
    return {
        "x": x, "p": p,
        "ev_w_in": ev_w_in, "ev_lambda_re": ev_lambda_re, "ev_lambda_im": ev_lambda_im,
        "ev_log_dt": ev_log_dt, "ev_b_re": ev_b_re, "ev_b_im": ev_b_im,
        "ev_c_re": ev_c_re, "ev_c_im": ev_c_im, "ev_d": ev_d,
        "ev_w_glu": ev_w_glu, "ev_b_glu": ev_b_glu, "ev_conv_w": ev_conv_w, "ev_w_out": ev_w_out,
        "od_w_in": od_w_in, "od_rel_bias": od_rel_bias, "od_pool_w": od_pool_w,
        "od_pool_scale": od_pool_scale, "od_w_out": od_w_out,
        "ln_mix_g": ln_mix_g, "ln_mix_b": ln_mix_b, "ln_ffn_g": ln_ffn_g, "ln_ffn_b": ln_ffn_b,
        "ffn_w_up": ffn_w_up, "ffn_w_down": ffn_w_down,
        "ple_w_proj": ple_w_proj, "ple_w_gate": ple_w_gate, "ple_b_gate": ple_b_gate,
    }


def _fwd_reference(x, p, ev_w_in, ev_lambda_re, ev_lambda_im, ev_log_dt, ev_b_re, ev_b_im,
              ev_c_re, ev_c_im, ev_d, ev_w_glu, ev_b_glu, ev_conv_w, ev_w_out,
              od_w_in, od_rel_bias, od_pool_w, od_pool_scale, od_w_out,
              ln_mix_g, ln_mix_b, ln_ffn_g, ln_ffn_b, ffn_w_up, ffn_w_down,
              ple_w_proj, ple_w_gate, ple_b_gate):
    for i in range(DEPTH):
        if i % 2 == 0:
            e = i // 2
            h = x @ ev_w_in[e]
            u_a, b_g, c_g, x_b = jnp.split(h, [MIX_A, MIX_A + MIX_B, MIX_A + 2 * MIX_B], axis=-1)
            y_a = s5_mixer(u_a, ev_lambda_re[e], ev_lambda_im[e], ev_log_dt[e], ev_b_re[e], ev_b_im[e],
                           ev_c_re[e], ev_c_im[e], ev_d[e], ev_w_glu[e], ev_b_glu[e])
            y_b = short_conv_mixer(b_g, c_g, x_b, ev_conv_w[e])
            mix = jnp.concatenate([y_a, y_b], axis=-1) @ ev_w_out[e]
        else:
            o = i // 2
            h = x @ od_w_in[o]
            q, k, v, z = jnp.split(h, [MIX_C, 2 * MIX_C, 3 * MIX_C], axis=-1)
            y_c = chunk_attention(q, k, v, od_rel_bias[o])
            y_d = pool_mixer(z, od_pool_w[o], od_pool_scale[o])
            mix = jnp.concatenate([y_c, y_d], axis=-1) @ od_w_out[o]
        x = layer_norm(ALPHA * x + mix, ln_mix_g[i], ln_mix_b[i])
        x = layer_norm(ALPHA * x + swiglu(x, ffn_w_up[i], ffn_w_down[i]), ln_ffn_g[i], ln_ffn_b[i])
        x = x + jax.nn.sigmoid(x @ ple_w_gate[i] + ple_b_gate[i]) * (p[i] @ ple_w_proj[i])
    return x


import jax as _jax
import jax.numpy as _jnp

TWIN_FORMAT = 'train_step'
FWD_PARAMS = ['x', 'p', 'ev_w_in', 'ev_lambda_re', 'ev_lambda_im', 'ev_log_dt', 'ev_b_re', 'ev_b_im', 'ev_c_re', 'ev_c_im', 'ev_d', 'ev_w_glu', 'ev_b_glu', 'ev_conv_w', 'ev_w_out', 'od_w_in', 'od_rel_bias', 'od_pool_w', 'od_pool_scale', 'od_w_out', 'ln_mix_g', 'ln_mix_b', 'ln_ffn_g', 'ln_ffn_b', 'ffn_w_up', 'ffn_w_down', 'ple_w_proj', 'ple_w_gate', 'ple_b_gate']
TWIN_WEIGHTS = ['ev_w_in', 'ev_lambda_re', 'ev_lambda_im', 'ev_log_dt', 'ev_b_re', 'ev_b_im', 'ev_c_re', 'ev_c_im', 'ev_d', 'ev_w_glu', 'ev_b_glu', 'ev_conv_w', 'ev_w_out', 'od_w_in', 'od_rel_bias', 'od_pool_w', 'od_pool_scale', 'od_w_out', 'ln_mix_g', 'ln_mix_b', 'ln_ffn_g', 'ln_ffn_b', 'ffn_w_up', 'ffn_w_down', 'ple_w_proj', 'ple_w_gate', 'ple_b_gate']
TWIN_DIFF_INPUT = 'x'
TWIN_INPUTS = ['x', 'p', 'ev_w_in', 'ev_lambda_re', 'ev_lambda_im', 'ev_log_dt', 'ev_b_re', 'ev_b_im', 'ev_c_re', 'ev_c_im', 'ev_d', 'ev_w_glu', 'ev_b_glu', 'ev_conv_w', 'ev_w_out', 'od_w_in', 'od_rel_bias', 'od_pool_w', 'od_pool_scale', 'od_w_out', 'ln_mix_g', 'ln_mix_b', 'ln_ffn_g', 'ln_ffn_b', 'ffn_w_up', 'ffn_w_down', 'ple_w_proj', 'ple_w_gate', 'ple_b_gate', 'loss_target', 'm_ev_w_in', 'm_ev_lambda_re', 'm_ev_lambda_im', 'm_ev_log_dt', 'm_ev_b_re', 'm_ev_b_im', 'm_ev_c_re', 'm_ev_c_im', 'm_ev_d', 'm_ev_w_glu', 'm_ev_b_glu', 'm_ev_conv_w', 'm_ev_w_out', 'm_od_w_in', 'm_od_rel_bias', 'm_od_pool_w', 'm_od_pool_scale', 'm_od_w_out', 'm_ln_mix_g', 'm_ln_mix_b', 'm_ln_ffn_g', 'm_ln_ffn_b', 'm_ffn_w_up', 'm_ffn_w_down', 'm_ple_w_proj', 'm_ple_w_gate', 'm_ple_b_gate', 'v_ev_w_in', 'v_ev_lambda_re', 'v_ev_lambda_im', 'v_ev_log_dt', 'v_ev_b_re', 'v_ev_b_im', 'v_ev_c_re', 'v_ev_c_im', 'v_ev_d', 'v_ev_w_glu', 'v_ev_b_glu', 'v_ev_conv_w', 'v_ev_w_out', 'v_od_w_in', 'v_od_rel_bias', 'v_od_pool_w', 'v_od_pool_scale', 'v_od_w_out', 'v_ln_mix_g', 'v_ln_mix_b', 'v_ln_ffn_g', 'v_ln_ffn_b', 'v_ffn_w_up', 'v_ffn_w_down', 'v_ple_w_proj', 'v_ple_w_gate', 'v_ple_b_gate']
TWIN_OUTPUTS = ['loss', 'grad_x', 'grad_ev_w_in', 'grad_ev_lambda_re', 'grad_ev_lambda_im', 'grad_ev_log_dt', 'grad_ev_b_re', 'grad_ev_b_im', 'grad_ev_c_re', 'grad_ev_c_im', 'grad_ev_d', 'grad_ev_w_glu', 'grad_ev_b_glu', 'grad_ev_conv_w', 'grad_ev_w_out', 'grad_od_w_in', 'grad_od_rel_bias', 'grad_od_pool_w', 'grad_od_pool_scale', 'grad_od_w_out', 'grad_ln_mix_g', 'grad_ln_mix_b', 'grad_ln_ffn_g', 'grad_ln_ffn_b', 'grad_ffn_w_up', 'grad_ffn_w_down', 'grad_ple_w_proj', 'grad_ple_w_gate', 'grad_ple_b_gate', 'delta_ev_w_in', 'delta_ev_lambda_re', 'delta_ev_lambda_im', 'delta_ev_log_dt', 'delta_ev_b_re', 'delta_ev_b_im', 'delta_ev_c_re', 'delta_ev_c_im', 'delta_ev_d', 'delta_ev_w_glu', 'delta_ev_b_glu', 'delta_ev_conv_w', 'delta_ev_w_out', 'delta_od_w_in', 'delta_od_rel_bias', 'delta_od_pool_w', 'delta_od_pool_scale', 'delta_od_w_out', 'delta_ln_mix_g', 'delta_ln_mix_b', 'delta_ln_ffn_g', 'delta_ln_ffn_b', 'delta_ffn_w_up', 'delta_ffn_w_down', 'delta_ple_w_proj', 'delta_ple_w_gate', 'delta_ple_b_gate', 'new_m_ev_w_in', 'new_m_ev_lambda_re', 'new_m_ev_lambda_im', 'new_m_ev_log_dt', 'new_m_ev_b_re', 'new_m_ev_b_im', 'new_m_ev_c_re', 'new_m_ev_c_im', 'new_m_ev_d', 'new_m_ev_w_glu', 'new_m_ev_b_glu', 'new_m_ev_conv_w', 'new_m_ev_w_out', 'new_m_od_w_in', 'new_m_od_rel_bias', 'new_m_od_pool_w', 'new_m_od_pool_scale', 'new_m_od_w_out', 'new_m_ln_mix_g', 'new_m_ln_mix_b', 'new_m_ln_ffn_g', 'new_m_ln_ffn_b', 'new_m_ffn_w_up', 'new_m_ffn_w_down', 'new_m_ple_w_proj', 'new_m_ple_w_gate', 'new_m_ple_b_gate', 'new_v_ev_w_in', 'new_v_ev_lambda_re', 'new_v_ev_lambda_im', 'new_v_ev_log_dt', 'new_v_ev_b_re', 'new_v_ev_b_im', 'new_v_ev_c_re', 'new_v_ev_c_im', 'new_v_ev_d', 'new_v_ev_w_glu', 'new_v_ev_b_glu', 'new_v_ev_conv_w', 'new_v_ev_w_out', 'new_v_od_w_in', 'new_v_od_rel_bias', 'new_v_od_pool_w', 'new_v_od_pool_scale', 'new_v_od_w_out', 'new_v_ln_mix_g', 'new_v_ln_mix_b', 'new_v_ln_ffn_g', 'new_v_ln_ffn_b', 'new_v_ffn_w_up', 'new_v_ffn_w_down', 'new_v_ple_w_proj', 'new_v_ple_w_gate', 'new_v_ple_b_gate']
TWIN_LEAF_KINDS = {'loss': 'loss', 'grad_x': 'grad_x', 'grad_ev_w_in': 'grad_w', 'grad_ev_lambda_re': 'grad_w', 'grad_ev_lambda_im': 'grad_w', 'grad_ev_log_dt': 'grad_w', 'grad_ev_b_re': 'grad_w', 'grad_ev_b_im': 'grad_w', 'grad_ev_c_re': 'grad_w', 'grad_ev_c_im': 'grad_w', 'grad_ev_d': 'grad_w', 'grad_ev_w_glu': 'grad_w', 'grad_ev_b_glu': 'grad_w', 'grad_ev_conv_w': 'grad_w', 'grad_ev_w_out': 'grad_w', 'grad_od_w_in': 'grad_w', 'grad_od_rel_bias': 'grad_w', 'grad_od_pool_w': 'grad_w', 'grad_od_pool_scale': 'grad_w', 'grad_od_w_out': 'grad_w', 'grad_ln_mix_g': 'grad_w', 'grad_ln_mix_b': 'grad_w', 'grad_ln_ffn_g': 'grad_w', 'grad_ln_ffn_b': 'grad_w', 'grad_ffn_w_up': 'grad_w', 'grad_ffn_w_down': 'grad_w', 'grad_ple_w_proj': 'grad_w', 'grad_ple_w_gate': 'grad_w', 'grad_ple_b_gate': 'grad_w', 'delta_ev_w_in': 'delta_w', 'delta_ev_lambda_re': 'delta_w', 'delta_ev_lambda_im': 'delta_w', 'delta_ev_log_dt': 'delta_w', 'delta_ev_b_re': 'delta_w', 'delta_ev_b_im': 'delta_w', 'delta_ev_c_re': 'delta_w', 'delta_ev_c_im': 'delta_w', 'delta_ev_d': 'delta_w', 'delta_ev_w_glu': 'delta_w', 'delta_ev_b_glu': 'delta_w', 'delta_ev_conv_w': 'delta_w', 'delta_ev_w_out': 'delta_w', 'delta_od_w_in': 'delta_w', 'delta_od_rel_bias': 'delta_w', 'delta_od_pool_w': 'delta_w', 'delta_od_pool_scale': 'delta_w', 'delta_od_w_out': 'delta_w', 'delta_ln_mix_g': 'delta_w', 'delta_ln_mix_b': 'delta_w', 'delta_ln_ffn_g': 'delta_w', 'delta_ln_ffn_b': 'delta_w', 'delta_ffn_w_up': 'delta_w', 'delta_ffn_w_down': 'delta_w', 'delta_ple_w_proj': 'delta_w', 'delta_ple_w_gate': 'delta_w', 'delta_ple_b_gate': 'delta_w', 'new_m_ev_w_in': 'new_m', 'new_m_ev_lambda_re': 'new_m', 'new_m_ev_lambda_im': 'new_m', 'new_m_ev_log_dt': 'new_m', 'new_m_ev_b_re': 'new_m', 'new_m_ev_b_im': 'new_m', 'new_m_ev_c_re': 'new_m', 'new_m_ev_c_im': 'new_m', 'new_m_ev_d': 'new_m', 'new_m_ev_w_glu': 'new_m', 'new_m_ev_b_glu': 'new_m', 'new_m_ev_conv_w': 'new_m', 'new_m_ev_w_out': 'new_m', 'new_m_od_w_in': 'new_m', 'new_m_od_rel_bias': 'new_m', 'new_m_od_pool_w': 'new_m', 'new_m_od_pool_scale': 'new_m', 'new_m_od_w_out': 'new_m', 'new_m_ln_mix_g': 'new_m', 'new_m_ln_mix_b': 'new_m', 'new_m_ln_ffn_g': 'new_m', 'new_m_ln_ffn_b': 'new_m', 'new_m_ffn_w_up': 'new_m', 'new_m_ffn_w_down': 'new_m', 'new_m_ple_w_proj': 'new_m', 'new_m_ple_w_gate': 'new_m', 'new_m_ple_b_gate': 'new_m', 'new_v_ev_w_in': 'new_v', 'new_v_ev_lambda_re': 'new_v', 'new_v_ev_lambda_im': 'new_v', 'new_v_ev_log_dt': 'new_v', 'new_v_ev_b_re': 'new_v', 'new_v_ev_b_im': 'new_v', 'new_v_ev_c_re': 'new_v', 'new_v_ev_c_im': 'new_v', 'new_v_ev_d': 'new_v', 'new_v_ev_w_glu': 'new_v', 'new_v_ev_b_glu': 'new_v', 'new_v_ev_conv_w': 'new_v', 'new_v_ev_w_out': 'new_v', 'new_v_od_w_in': 'new_v', 'new_v_od_rel_bias': 'new_v', 'new_v_od_pool_w': 'new_v', 'new_v_od_pool_scale': 'new_v', 'new_v_od_w_out': 'new_v', 'new_v_ln_mix_g': 'new_v', 'new_v_ln_mix_b': 'new_v', 'new_v_ln_ffn_g': 'new_v', 'new_v_ln_ffn_b': 'new_v', 'new_v_ffn_w_up': 'new_v', 'new_v_ffn_w_down': 'new_v', 'new_v_ple_w_proj': 'new_v', 'new_v_ple_w_gate': 'new_v', 'new_v_ple_b_gate': 'new_v'}


def _forward(args):
    return _fwd_reference(*[args[k] for k in FWD_PARAMS])


def _output_shape():
    def fwd():
        inp = _fwd_setup_inputs(0)
        return _fwd_reference(*[inp[k] for k in FWD_PARAMS])
    out = _jax.eval_shape(fwd)
    return out.shape, out.dtype

N_MICROBATCH = 1
ADAM_LR = 0.001
ADAM_B1 = 0.9
ADAM_B2 = 0.999
ADAM_EPS = 1e-08
ADAM_WD = 0.01
ADAM_STEP = 10
PER_EXAMPLE_BATCH_AXIS = {'x': 0, 'p': 1, 'loss_target': 0}
SHARED_INPUTS = []
_WEIGHT_DTYPES = {'ev_w_in': _jnp.float32, 'ev_lambda_re': _jnp.float32, 'ev_lambda_im': _jnp.float32, 'ev_log_dt': _jnp.float32, 'ev_b_re': _jnp.float32, 'ev_b_im': _jnp.float32, 'ev_c_re': _jnp.float32, 'ev_c_im': _jnp.float32, 'ev_d': _jnp.float32, 'ev_w_glu': _jnp.float32, 'ev_b_glu': _jnp.float32, 'ev_conv_w': _jnp.float32, 'ev_w_out': _jnp.float32, 'od_w_in': _jnp.float32, 'od_rel_bias': _jnp.float32, 'od_pool_w': _jnp.float32, 'od_pool_scale': _jnp.float32, 'od_w_out': _jnp.float32, 'ln_mix_g': _jnp.float32, 'ln_mix_b': _jnp.float32, 'ln_ffn_g': _jnp.float32, 'ln_ffn_b': _jnp.float32, 'ffn_w_up': _jnp.float32, 'ffn_w_down': _jnp.float32, 'ple_w_proj': _jnp.float32, 'ple_w_gate': _jnp.float32, 'ple_b_gate': _jnp.float32}
MOMENT_SCALE = {'ev_w_in': 9.220044e-02, 'ev_lambda_re': 4.482884e-03, 'ev_lambda_im': 3.291285e-03, 'ev_log_dt': 1.942772e+00, 'ev_b_re': 1.931142e-03, 'ev_b_im': 1.964821e-03, 'ev_c_re': 2.690280e-03, 'ev_c_im': 2.977412e-03, 'ev_d': 3.747523e-01, 'ev_w_glu': 4.617960e-02, 'ev_b_glu': 1.642637e-01, 'ev_conv_w': 1.046140e-01, 'ev_w_out': 3.856175e-01, 'od_w_in': 4.758792e-02, 'od_rel_bias': 8.241222e-03, 'od_pool_w': 8.721284e-02, 'od_pool_scale': 8.618505e-02, 'od_w_out': 1.602685e-01, 'ln_mix_g': 9.818609e-01, 'ln_mix_b': 9.982069e+00, 'ln_ffn_g': 6.572548e+01, 'ln_ffn_b': 1.020824e+01, 'ffn_w_up': 3.593754e-02, 'ffn_w_down': 1.407332e-01, 'ple_w_proj': 6.975619e-01, 'ple_w_gate': 1.538678e-01, 'ple_b_gate': 6.509741e+00}


def _to_microbatches(a, axis):
    t = _jnp.moveaxis(a, axis, 0)
    t = t.reshape((N_MICROBATCH, t.shape[0] // N_MICROBATCH) + t.shape[1:])
    return _jnp.moveaxis(t, 1, axis + 1)


def setup_inputs(seed: int = 0) -> dict:
    inp = _fwd_setup_inputs(seed)
    key = _jax.random.fold_in(_jax.random.key(seed), 7919)
    shape, _ = _output_shape()
    out = dict(inp)
    out["loss_target"] = _jax.random.normal(_jax.random.fold_in(key, 0), shape, _jnp.float32)
    for i, name in enumerate(TWIN_WEIGHTS):
        w = inp[name].astype(_jnp.float32)
        if MOMENT_SCALE is None:
            s = _jnp.sqrt(_jnp.mean(_jnp.square(w)) + 1e-30)
        else:
            s = MOMENT_SCALE[name]
        km, kv = _jax.random.split(_jax.random.fold_in(key, i + 1))
        out[name] = w
        out["m_" + name] = s * _jax.random.normal(km, w.shape, _jnp.float32)
        out["v_" + name] = (s * s) * _jax.random.uniform(kv, w.shape, _jnp.float32, 0.5, 1.5)
    if N_MICROBATCH > 1:
        for name, axis in PER_EXAMPLE_BATCH_AXIS.items():
            out[name] = _to_microbatches(out[name], axis)
    return {'x': out['x'], 'p': out['p'], 'ev_w_in': out['ev_w_in'], 'ev_lambda_re': out['ev_lambda_re'], 'ev_lambda_im': out['ev_lambda_im'], 'ev_log_dt': out['ev_log_dt'], 'ev_b_re': out['ev_b_re'], 'ev_b_im': out['ev_b_im'], 'ev_c_re': out['ev_c_re'], 'ev_c_im': out['ev_c_im'], 'ev_d': out['ev_d'], 'ev_w_glu': out['ev_w_glu'], 'ev_b_glu': out['ev_b_glu'], 'ev_conv_w': out['ev_conv_w'], 'ev_w_out': out['ev_w_out'], 'od_w_in': out['od_w_in'], 'od_rel_bias': out['od_rel_bias'], 'od_pool_w': out['od_pool_w'], 'od_pool_scale': out['od_pool_scale'], 'od_w_out': out['od_w_out'], 'ln_mix_g': out['ln_mix_g'], 'ln_mix_b': out['ln_mix_b'], 'ln_ffn_g': out['ln_ffn_g'], 'ln_ffn_b': out['ln_ffn_b'], 'ffn_w_up': out['ffn_w_up'], 'ffn_w_down': out['ffn_w_down'], 'ple_w_proj': out['ple_w_proj'], 'ple_w_gate': out['ple_w_gate'], 'ple_b_gate': out['ple_b_gate'], 'loss_target': out['loss_target'], 'm_ev_w_in': out['m_ev_w_in'], 'm_ev_lambda_re': out['m_ev_lambda_re'], 'm_ev_lambda_im': out['m_ev_lambda_im'], 'm_ev_log_dt': out['m_ev_log_dt'], 'm_ev_b_re': out['m_ev_b_re'], 'm_ev_b_im': out['m_ev_b_im'], 'm_ev_c_re': out['m_ev_c_re'], 'm_ev_c_im': out['m_ev_c_im'], 'm_ev_d': out['m_ev_d'], 'm_ev_w_glu': out['m_ev_w_glu'], 'm_ev_b_glu': out['m_ev_b_glu'], 'm_ev_conv_w': out['m_ev_conv_w'], 'm_ev_w_out': out['m_ev_w_out'], 'm_od_w_in': out['m_od_w_in'], 'm_od_rel_bias': out['m_od_rel_bias'], 'm_od_pool_w': out['m_od_pool_w'], 'm_od_pool_scale': out['m_od_pool_scale'], 'm_od_w_out': out['m_od_w_out'], 'm_ln_mix_g': out['m_ln_mix_g'], 'm_ln_mix_b': out['m_ln_mix_b'], 'm_ln_ffn_g': out['m_ln_ffn_g'], 'm_ln_ffn_b': out['m_ln_ffn_b'], 'm_ffn_w_up': out['m_ffn_w_up'], 'm_ffn_w_down': out['m_ffn_w_down'], 'm_ple_w_proj': out['m_ple_w_proj'], 'm_ple_w_gate': out['m_ple_w_gate'], 'm_ple_b_gate': out['m_ple_b_gate'], 'v_ev_w_in': out['v_ev_w_in'], 'v_ev_lambda_re': out['v_ev_lambda_re'], 'v_ev_lambda_im': out['v_ev_lambda_im'], 'v_ev_log_dt': out['v_ev_log_dt'], 'v_ev_b_re': out['v_ev_b_re'], 'v_ev_b_im': out['v_ev_b_im'], 'v_ev_c_re': out['v_ev_c_re'], 'v_ev_c_im': out['v_ev_c_im'], 'v_ev_d': out['v_ev_d'], 'v_ev_w_glu': out['v_ev_w_glu'], 'v_ev_b_glu': out['v_ev_b_glu'], 'v_ev_conv_w': out['v_ev_conv_w'], 'v_ev_w_out': out['v_ev_w_out'], 'v_od_w_in': out['v_od_w_in'], 'v_od_rel_bias': out['v_od_rel_bias'], 'v_od_pool_w': out['v_od_pool_w'], 'v_od_pool_scale': out['v_od_pool_scale'], 'v_od_w_out': out['v_od_w_out'], 'v_ln_mix_g': out['v_ln_mix_g'], 'v_ln_mix_b': out['v_ln_mix_b'], 'v_ln_ffn_g': out['v_ln_ffn_g'], 'v_ln_ffn_b': out['v_ln_ffn_b'], 'v_ffn_w_up': out['v_ffn_w_up'], 'v_ffn_w_down': out['v_ffn_w_down'], 'v_ple_w_proj': out['v_ple_w_proj'], 'v_ple_w_gate': out['v_ple_w_gate'], 'v_ple_b_gate': out['v_ple_b_gate']}


def _loss(weights, diff, rest, loss_target):
    with _jax.named_scope("forward"):
        args = {**rest, TWIN_DIFF_INPUT: diff, **{k: w.astype(_WEIGHT_DTYPES[k]) for k, w in weights.items()}}
        y = _forward(args)
    with _jax.named_scope("loss_head"):
        err = _jnp.square(y.astype(_jnp.float32) - loss_target)
        return 0.5 * _jnp.sum(_jnp.mean(err, axis=-1)) if err.ndim else 0.5 * err


def _adamw(w, g, m, v):
    m = ADAM_B1 * m + (1.0 - ADAM_B1) * g
    v = ADAM_B2 * v + (1.0 - ADAM_B2) * _jnp.square(g)
    m_hat = m / (1.0 - ADAM_B1 ** ADAM_STEP)
    v_hat = v / (1.0 - ADAM_B2 ** ADAM_STEP)
    delta = -ADAM_LR * (m_hat / (_jnp.sqrt(v_hat) + ADAM_EPS) + ADAM_WD * w)
    return delta, m, v


def reference(x, p, ev_w_in, ev_lambda_re, ev_lambda_im, ev_log_dt, ev_b_re, ev_b_im, ev_c_re, ev_c_im, ev_d, ev_w_glu, ev_b_glu, ev_conv_w, ev_w_out, od_w_in, od_rel_bias, od_pool_w, od_pool_scale, od_w_out, ln_mix_g, ln_mix_b, ln_ffn_g, ln_ffn_b, ffn_w_up, ffn_w_down, ple_w_proj, ple_w_gate, ple_b_gate, loss_target, m_ev_w_in, m_ev_lambda_re, m_ev_lambda_im, m_ev_log_dt, m_ev_b_re, m_ev_b_im, m_ev_c_re, m_ev_c_im, m_ev_d, m_ev_w_glu, m_ev_b_glu, m_ev_conv_w, m_ev_w_out, m_od_w_in, m_od_rel_bias, m_od_pool_w, m_od_pool_scale, m_od_w_out, m_ln_mix_g, m_ln_mix_b, m_ln_ffn_g, m_ln_ffn_b, m_ffn_w_up, m_ffn_w_down, m_ple_w_proj, m_ple_w_gate, m_ple_b_gate, v_ev_w_in, v_ev_lambda_re, v_ev_lambda_im, v_ev_log_dt, v_ev_b_re, v_ev_b_im, v_ev_c_re, v_ev_c_im, v_ev_d, v_ev_w_glu, v_ev_b_glu, v_ev_conv_w, v_ev_w_out, v_od_w_in, v_od_rel_bias, v_od_pool_w, v_od_pool_scale, v_od_w_out, v_ln_mix_g, v_ln_mix_b, v_ln_ffn_g, v_ln_ffn_b, v_ffn_w_up, v_ffn_w_down, v_ple_w_proj, v_ple_w_gate, v_ple_b_gate):
    given = dict(x=x, p=p, ev_w_in=ev_w_in, ev_lambda_re=ev_lambda_re, ev_lambda_im=ev_lambda_im, ev_log_dt=ev_log_dt, ev_b_re=ev_b_re, ev_b_im=ev_b_im, ev_c_re=ev_c_re, ev_c_im=ev_c_im, ev_d=ev_d, ev_w_glu=ev_w_glu, ev_b_glu=ev_b_glu, ev_conv_w=ev_conv_w, ev_w_out=ev_w_out, od_w_in=od_w_in, od_rel_bias=od_rel_bias, od_pool_w=od_pool_w, od_pool_scale=od_pool_scale, od_w_out=od_w_out, ln_mix_g=ln_mix_g, ln_mix_b=ln_mix_b, ln_ffn_g=ln_ffn_g, ln_ffn_b=ln_ffn_b, ffn_w_up=ffn_w_up, ffn_w_down=ffn_w_down, ple_w_proj=ple_w_proj, ple_w_gate=ple_w_gate, ple_b_gate=ple_b_gate, loss_target=loss_target, m_ev_w_in=m_ev_w_in, m_ev_lambda_re=m_ev_lambda_re, m_ev_lambda_im=m_ev_lambda_im, m_ev_log_dt=m_ev_log_dt, m_ev_b_re=m_ev_b_re, m_ev_b_im=m_ev_b_im, m_ev_c_re=m_ev_c_re, m_ev_c_im=m_ev_c_im, m_ev_d=m_ev_d, m_ev_w_glu=m_ev_w_glu, m_ev_b_glu=m_ev_b_glu, m_ev_conv_w=m_ev_conv_w, m_ev_w_out=m_ev_w_out, m_od_w_in=m_od_w_in, m_od_rel_bias=m_od_rel_bias, m_od_pool_w=m_od_pool_w, m_od_pool_scale=m_od_pool_scale, m_od_w_out=m_od_w_out, m_ln_mix_g=m_ln_mix_g, m_ln_mix_b=m_ln_mix_b, m_ln_ffn_g=m_ln_ffn_g, m_ln_ffn_b=m_ln_ffn_b, m_ffn_w_up=m_ffn_w_up, m_ffn_w_down=m_ffn_w_down, m_ple_w_proj=m_ple_w_proj, m_ple_w_gate=m_ple_w_gate, m_ple_b_gate=m_ple_b_gate, v_ev_w_in=v_ev_w_in, v_ev_lambda_re=v_ev_lambda_re, v_ev_lambda_im=v_ev_lambda_im, v_ev_log_dt=v_ev_log_dt, v_ev_b_re=v_ev_b_re, v_ev_b_im=v_ev_b_im, v_ev_c_re=v_ev_c_re, v_ev_c_im=v_ev_c_im, v_ev_d=v_ev_d, v_ev_w_glu=v_ev_w_glu, v_ev_b_glu=v_ev_b_glu, v_ev_conv_w=v_ev_conv_w, v_ev_w_out=v_ev_w_out, v_od_w_in=v_od_w_in, v_od_rel_bias=v_od_rel_bias, v_od_pool_w=v_od_pool_w, v_od_pool_scale=v_od_pool_scale, v_od_w_out=v_od_w_out, v_ln_mix_g=v_ln_mix_g, v_ln_mix_b=v_ln_mix_b, v_ln_ffn_g=v_ln_ffn_g, v_ln_ffn_b=v_ln_ffn_b, v_ffn_w_up=v_ffn_w_up, v_ffn_w_down=v_ffn_w_down, v_ple_w_proj=v_ple_w_proj, v_ple_w_gate=v_ple_w_gate, v_ple_b_gate=v_ple_b_gate)
    weights = {n: given[n] for n in TWIN_WEIGHTS}
    shared = {n: given[n] for n in SHARED_INPUTS}
    per_example = {n: given[n] for n in ['x', 'p']}
    grad_fn = _jax.value_and_grad(_loss, argnums=(0, 1))

    def one_microbatch(ex, loss_target):
        ex = dict(ex)
        diff = ex.pop(TWIN_DIFF_INPUT)
        return grad_fn(weights, diff, {**shared, **ex}, loss_target)

    if N_MICROBATCH == 1:
        loss, (grad_w, grad_x) = one_microbatch(per_example, given["loss_target"])
    else:
        def body(carry, xs):
            loss_sum, grad_sum = carry
            l_k, (gw_k, gx_k) = one_microbatch(xs[0], xs[1])
            with _jax.named_scope("update"):
                return (loss_sum + l_k, _jax.tree.map(_jnp.add, grad_sum, gw_k)), gx_k

        init = (_jnp.zeros((), _jnp.float32), _jax.tree.map(_jnp.zeros_like, weights))
        (loss, grad_w), grad_x = _jax.lax.scan(body, init, (per_example, given["loss_target"]))
    with _jax.named_scope("update"):
        delta_w, new_m, new_v = {}, {}, {}
        for n in TWIN_WEIGHTS:
            delta_w[n], new_m[n], new_v[n] = _adamw(weights[n], grad_w[n], given["m_" + n], given["v_" + n])
    return (loss, grad_x, *[grad_w[n] for n in TWIN_WEIGHTS], *[delta_w[n] for n in TWIN_WEIGHTS],
            *[new_m[n] for n in TWIN_WEIGHTS], *[new_v[n] for n in TWIN_WEIGHTS])
```

```python
import math

import jax
import jax.numpy as jnp
from jax import lax
from jax.experimental import pallas as pl
from jax.experimental.pallas import tpu as pltpu

F32 = jnp.float32
BF16 = jnp.bfloat16
HI = lax.Precision.HIGHEST

D_MODEL = 1024
DEPTH = 4
CHUNK = 64
MIX = 512
S5_GROUP = 16
S5_GROUPS = 32
S5_STATE = 64
HEADS = 8
HEAD_DIM = 64
LEFT_CHUNKS = 8
MAX_REL = 128
POOL_WINDOWS = (2, 4, 8, 16)
POOL_GROUP = 128
D_FF = 2816
D_PLE = 256
ALPHA = (2 * DEPTH) ** 0.25
LN_EPS = 1e-5
NEG_INF = -1e30
ADAM_LR = 0.001
ADAM_B1 = 0.9
ADAM_B2 = 0.999
ADAM_EPS = 1e-08
ADAM_WD = 0.01
ADAM_STEP = 10
N_DEV = 8

WEIGHT_NAMES = ['ev_w_in', 'ev_lambda_re', 'ev_lambda_im', 'ev_log_dt', 'ev_b_re', 'ev_b_im', 'ev_c_re', 'ev_c_im',
                'ev_d', 'ev_w_glu', 'ev_b_glu', 'ev_conv_w', 'ev_w_out', 'od_w_in', 'od_rel_bias', 'od_pool_w',
                'od_pool_scale', 'od_w_out', 'ln_mix_g', 'ln_mix_b', 'ln_ffn_g', 'ln_ffn_b', 'ffn_w_up', 'ffn_w_down',
                'ple_w_proj', 'ple_w_gate', 'ple_b_gate']
SHARD_AXIS = {'ev_w_in': 2, 'ev_w_glu': 1, 'ev_conv_w': 2, 'ev_w_out': 1, 'od_w_in': 2, 'od_pool_scale': 1,
              'od_w_out': 1, 'ffn_w_up': 2, 'ffn_w_down': 1, 'ple_w_proj': 2, 'ple_w_gate': 1}
SHARDED = [n for n in WEIGHT_NAMES if n in SHARD_AXIS]
REPLICATED = [n for n in WEIGHT_NAMES if n not in SHARD_AXIS]

VMEM_LIMIT = 48 * 1024 * 1024
TM = 512
FF_TN = 256
S5_LC = 32
S5_LW = S5_LC * S5_GROUP
ATT_TQ = 128
ATT_NV = LEFT_CHUNKS * CHUNK // ATT_TQ + 1
ATT_W = ATT_NV * ATT_TQ
FLAT_COLS = 1024
FLAT_ROWS = 256
PIECE = 16 * FLAT_COLS


def _sds(shape, dt):
    return jax.ShapeDtypeStruct(shape, dt)


def _pcall(body, *, name, grid, in_specs, out_specs, out_shape, scratch=(), sem=None):
    sem = sem or ("arbitrary",) * len(grid)
    return pl.pallas_call(
        body, name=name, grid=grid, in_specs=in_specs, out_specs=out_specs, out_shape=out_shape,
        scratch_shapes=scratch,
        compiler_params=pltpu.CompilerParams(dimension_semantics=sem, vmem_limit_bytes=VMEM_LIMIT))


def _rows(tm, n):
    return pl.BlockSpec((tm, n), lambda i: (i, 0))


def _full(shape):
    nd = len(shape)
    return pl.BlockSpec(shape, lambda *_: (0,) * nd)


def _dot(a, b, precision=None):
    return jnp.dot(a, b, preferred_element_type=F32, precision=precision)


def _dot_nt(a, b, precision=None):
    return lax.dot_general(a, b, (((1,), (1,)), ((), ())), preferred_element_type=F32, precision=precision)


def _dot_tn(a, b, precision=None):
    return lax.dot_general(a, b, (((0,), (0,)), ((), ())), preferred_element_type=F32, precision=precision)


def _sigmoid(x):
    return 1.0 / (1.0 + jnp.exp(-x))


_GELU_C = math.sqrt(2.0 / math.pi)


def _gelu(x):
    return 0.5 * x * (1.0 + jnp.tanh(_GELU_C * (x + 0.044715 * x * x * x)))


def _gelu_grad(x):
    t = jnp.tanh(_GELU_C * (x + 0.044715 * x * x * x))
    return 0.5 * (1.0 + t) + 0.5 * x * (1.0 - t * t) * _GELU_C * (1.0 + 3.0 * 0.044715 * x * x)


def _ln_fwd(r, g, b):
    mu = jnp.mean(r, axis=-1, keepdims=True)
    xc = r - mu
    var = jnp.mean(xc * xc, axis=-1, keepdims=True)
    rstd = lax.rsqrt(var + LN_EPS)
    xhat = xc * rstd
    return xhat, rstd, xhat * g + b


def _ln_bwd(dx, xhat, rstd, g):
    dxh = dx * g
    m1 = jnp.mean(dxh, axis=-1, keepdims=True)
    m2 = jnp.mean(dxh * xhat, axis=-1, keepdims=True)
    return rstd * (dxh - m1 - xhat * m2)


def _colsum(x):
    return jnp.sum(x, axis=0, keepdims=True)


def _mm_in(xh, w, name):
    T = xh.shape[0]

    def body(x_ref, w_ref, *outs):
        x = x_ref[...]
        for s, o in enumerate(outs):
            o[...] = _dot(x, w_ref[:, s * MIX:(s + 1) * MIX])

    return _pcall(body, name=name, grid=(T // TM,),
                  in_specs=[_rows(TM, D_MODEL), _full((D_MODEL, 4 * MIX))],
                  out_specs=[_rows(TM, MIX)] * 4, out_shape=[_sds((T, MIX), F32)] * 4,
                  sem=("parallel",))(xh, w)


def _tile_of(n, cap):
    best = None
    for t in range(128, min(n, cap) + 1, 128):
        if n % t == 0:
            best = t
    assert best is not None, n
    return best


def _mm_tn(a, b, name):
    T, M = a.shape
    N = b.shape[1]
    tm = _tile_of(M, 1408)
    tn = _tile_of(N, 512)
    tk = min(T, 1024)
    nk = T // tk

    def body(a_ref, b_ref, o_ref):
        k = pl.program_id(2)

        @pl.when(k == 0)
        def _():
            o_ref[...] = jnp.zeros_like(o_ref)

        o_ref[...] += _dot_tn(a_ref[...].astype(BF16), b_ref[...].astype(BF16))

    return _pcall(body, name=name, grid=(M // tm, N // tn, nk),
                  in_specs=[pl.BlockSpec((tk, tm), lambda i, j, k: (k, i)),
                            pl.BlockSpec((tk, tn), lambda i, j, k: (k, j))],
                  out_specs=pl.BlockSpec((tm, tn), lambda i, j, k: (i, j)),
                  out_shape=_sds((M, N), F32),
                  sem=("parallel", "parallel", "arbitrary"))(a, b)


def _mm_out_ln(ya, yb, wo, x0, g, b, name):
    T = x0.shape[0]

    def body(ya_ref, yb_ref, w_ref, x0_ref, g_ref, b_ref, xh_ref, rs_ref, x1_ref):
        r = ALPHA * x0_ref[...] + _dot(ya_ref[...], w_ref[0:MIX, :]) + _dot(yb_ref[...], w_ref[MIX:, :])
        xhat, rstd, x1 = _ln_fwd(r, g_ref[...], b_ref[...])
        xh_ref[...] = xhat
        rs_ref[...] = rstd
        x1_ref[...] = x1.astype(BF16)

    return _pcall(body, name=name, grid=(T // TM,),
                  in_specs=[_rows(TM, MIX), _rows(TM, MIX), _full((D_MODEL, D_MODEL)), _rows(TM, D_MODEL),
                            _full((1, D_MODEL)), _full((1, D_MODEL))],
                  out_specs=[_rows(TM, D_MODEL), _rows(TM, 1), _rows(TM, D_MODEL)],
                  out_shape=[_sds((T, D_MODEL), F32), _sds((T, 1), F32), _sds((T, D_MODEL), BF16)],
                  sem=("parallel",))(ya, yb, wo, x0, g, b)


def _mm_up(x1h, wup, name):
    T = x1h.shape[0]
    nj = D_FF // FF_TN

    def body(x_ref, w_ref, hf_ref, a_ref):
        h = _dot(x_ref[...], w_ref[...])
        g = h[:, :FF_TN]
        u = h[:, FF_TN:]
        hf_ref[...] = h.astype(BF16)
        a_ref[...] = (g * _sigmoid(g) * u).astype(BF16)

    return _pcall(body, name=name, grid=(T // TM, nj),
                  in_specs=[pl.BlockSpec((TM, D_MODEL), lambda i, j: (i, 0)),
                            pl.BlockSpec((D_MODEL, 2 * FF_TN), lambda i, j: (0, j))],
                  out_specs=[pl.BlockSpec((TM, 2 * FF_TN), lambda i, j: (i, j)),
                             pl.BlockSpec((TM, FF_TN), lambda i, j: (i, j))],
                  out_shape=[_sds((T, 2 * D_FF), BF16), _sds((T, D_FF), BF16)],
                  sem=("parallel", "parallel"))(x1h, wup)


def _mm_down_ln(ah, wd, xhat1, g1, b1, g2, b2, name):
    T = ah.shape[0]

    def body(a_ref, w_ref, xh1_ref, g1_ref, b1_ref, g2_ref, b2_ref, xh_ref, rs_ref, x2_ref):
        x1 = xh1_ref[...] * g1_ref[...] + b1_ref[...]
        r = ALPHA * x1 + _dot(a_ref[...], w_ref[...])
        xhat, rstd, x2 = _ln_fwd(r, g2_ref[...], b2_ref[...])
        xh_ref[...] = xhat
        rs_ref[...] = rstd
        x2_ref[...] = x2.astype(BF16)

    vec = _full((1, D_MODEL))
    return _pcall(body, name=name, grid=(T // TM,),
                  in_specs=[_rows(TM, D_FF), _full((D_FF, D_MODEL)), _rows(TM, D_MODEL), vec, vec, vec, vec],
                  out_specs=[_rows(TM, D_MODEL), _rows(TM, 1), _rows(TM, D_MODEL)],
                  out_shape=[_sds((T, D_MODEL), F32), _sds((T, 1), F32), _sds((T, D_MODEL), BF16)],
                  sem=("parallel",))(ah, wd, xhat1, g1, b1, g2, b2)


def _mm_ple(x2h, xhat2, g2, b2, p, wg, bg, wp, name):
    T = x2h.shape[0]

    def body(x2h_ref, xh_ref, g2_ref, b2_ref, p_ref, wg_ref, bg_ref, wp_ref, o_ref, oh_ref):
        x2 = xh_ref[...] * g2_ref[...] + b2_ref[...]
        gate = _sigmoid(_dot(x2h_ref[...], wg_ref[...]) + bg_ref[...])
        pp = _dot(p_ref[...].astype(BF16), wp_ref[...])
        x3 = x2 + gate * pp
        o_ref[...] = x3
        oh_ref[...] = x3.astype(BF16)

    vec = _full((1, D_MODEL))
    return _pcall(body, name=name, grid=(T // TM,),
                  in_specs=[_rows(TM, D_MODEL), _rows(TM, D_MODEL), vec, vec, _rows(TM, D_PLE),
                            _full((D_MODEL, D_MODEL)), vec, _full((D_PLE, D_MODEL))],
                  out_specs=[_rows(TM, D_MODEL), _rows(TM, D_MODEL)],
                  out_shape=[_sds((T, D_MODEL), F32), _sds((T, D_MODEL), BF16)],
                  sem=("parallel",))(x2h, xhat2, g2, b2, p, wg, bg, wp)


def _loss_head(x3, tgt, name):
    T = x3.shape[0]

    def body(x_ref, t_ref, dx_ref, l_ref):
        e = x_ref[...] - t_ref[...]
        dx_ref[...] = e * (1.0 / D_MODEL)

        @pl.when(pl.program_id(0) == 0)
        def _():
            l_ref[...] = jnp.zeros_like(l_ref)

        l_ref[...] += (0.5 / D_MODEL) * jnp.sum(e * e).reshape(1, 1)

    return _pcall(body, name=name, grid=(T // TM,),
                  in_specs=[_rows(TM, D_MODEL), _rows(TM, D_MODEL)],
                  out_specs=[_rows(TM, D_MODEL), _full((1, 1))],
                  out_shape=[_sds((T, D_MODEL), F32), _sds((1, 1), F32)])(x3, tgt)


def _ple_bwd(dx3, x2h, p, wg, bg, wp, xhat2, rstd2, g2, name):
    T = dx3.shape[0]

    def body(dx3_ref, x2h_ref, p_ref, wg_ref, bg_ref, wp_ref, xh_ref, rs_ref, g2_ref,
             dr_ref, drh_ref, dpre_ref, dpp_ref, dbg_ref, dg_ref, db_ref):
        dx3 = dx3_ref[...]
        gate = _sigmoid(_dot(x2h_ref[...], wg_ref[...]) + bg_ref[...])
        pp = _dot(p_ref[...].astype(BF16), wp_ref[...])
        dpre = dx3 * pp * gate * (1.0 - gate)
        dpreh = dpre.astype(BF16)
        dpre_ref[...] = dpreh
        dpp_ref[...] = (dx3 * gate).astype(BF16)
        dx2 = dx3 + _dot_nt(dpreh, wg_ref[...])
        xhat = xh_ref[...]
        dr = _ln_bwd(dx2, xhat, rs_ref[...], g2_ref[...])
        dr_ref[...] = dr
        drh_ref[...] = dr.astype(BF16)

        @pl.when(pl.program_id(0) == 0)
        def _():
            dbg_ref[...] = jnp.zeros_like(dbg_ref)
            dg_ref[...] = jnp.zeros_like(dg_ref)
            db_ref[...] = jnp.zeros_like(db_ref)

        dbg_ref[...] += _colsum(dpre)
        dg_ref[...] += _colsum(dx2 * xhat)
        db_ref[...] += _colsum(dx2)

    vec = _full((1, D_MODEL))
    big = _rows(TM, D_MODEL)
    return _pcall(body, name=name, grid=(T // TM,),
                  in_specs=[big, big, _rows(TM, D_PLE), _full((D_MODEL, D_MODEL)), vec, _full((D_PLE, D_MODEL)),
                            big, _rows(TM, 1), vec],
                  out_specs=[big, big, big, big, vec, vec, vec],
                  out_shape=[_sds((T, D_MODEL), F32), _sds((T, D_MODEL), BF16), _sds((T, D_MODEL), BF16),
                             _sds((T, D_MODEL), BF16), _sds((1, D_MODEL), F32), _sds((1, D_MODEL), F32),
                             _sds((1, D_MODEL), F32)])(dx3, x2h, p, wg, bg, wp, xhat2, rstd2, g2)


def _ffn_bwd1(dr2h, wd, hf, name):
    T = dr2h.shape[0]
    nj = D_FF // FF_TN

    def body(d_ref, w_ref, hf_ref, o_ref):
        da = _dot_nt(d_ref[...], w_ref[...])
        h = hf_ref[...].astype(F32)
        g = h[:, :FF_TN]
        u = h[:, FF_TN:]
        sg = _sigmoid(g)
        o_ref[:, :FF_TN] = (da * u * (sg * (1.0 + g * (1.0 - sg)))).astype(BF16)
        o_ref[:, FF_TN:] = (da * (g * sg)).astype(BF16)

    return _pcall(body, name=name, grid=(T // TM, nj),
                  in_specs=[pl.BlockSpec((TM, D_MODEL), lambda i, j: (i, 0)),
                            pl.BlockSpec((FF_TN, D_MODEL), lambda i, j: (j, 0)),
                            pl.BlockSpec((TM, 2 * FF_TN), lambda i, j: (i, j))],
                  out_specs=pl.BlockSpec((TM, 2 * FF_TN), lambda i, j: (i, j)),
                  out_shape=_sds((T, 2 * D_FF), BF16),
                  sem=("parallel", "parallel"))(dr2h, wd, hf)


def _ffn_bwd2(dhf, wup, dr2, xhat1, rstd1, g1, name):
    T = dhf.shape[0]
    tk = 2 * FF_TN
    nk = 2 * D_FF // tk

    def body(dh_ref, w_ref, dr2_ref, xh_ref, rs_ref, g_ref, dr_ref, drh_ref, dg_ref, db_ref, acc):
        i = pl.program_id(0)
        k = pl.program_id(1)

        @pl.when(k == 0)
        def _():
            acc[...] = ALPHA * dr2_ref[...]

        @pl.when((k == 0) & (i == 0))
        def _():
            dg_ref[...] = jnp.zeros_like(dg_ref)
            db_ref[...] = jnp.zeros_like(db_ref)

        acc[...] += _dot_nt(dh_ref[...], w_ref[...])

        @pl.when(k == nk - 1)
        def _():
            dx1 = acc[...]
            xhat = xh_ref[...]
            dr = _ln_bwd(dx1, xhat, rs_ref[...], g_ref[...])
            dr_ref[...] = dr
            drh_ref[...] = dr.astype(BF16)
            dg_ref[...] += _colsum(dx1 * xhat)
            db_ref[...] += _colsum(dx1)

    big = pl.BlockSpec((TM, D_MODEL), lambda i, k: (i, 0))
    vec = pl.BlockSpec((1, D_MODEL), lambda i, k: (0, 0))
    return _pcall(body, name=name, grid=(T // TM, nk),
                  in_specs=[pl.BlockSpec((TM, tk), lambda i, k: (i, k)),
                            pl.BlockSpec((D_MODEL, tk), lambda i, k: (0, k)),
                            big, big, pl.BlockSpec((TM, 1), lambda i, k: (i, 0)), vec],
                  out_specs=[big, big, vec, vec],
                  out_shape=[_sds((T, D_MODEL), F32), _sds((T, D_MODEL), BF16), _sds((1, D_MODEL), F32),
                             _sds((1, D_MODEL), F32)],
                  scratch=[pltpu.VMEM((TM, D_MODEL), F32)])(dhf, wup, dr2, xhat1, rstd1, g1)


def _out_bwd(dr1h, wo, name):
    T = dr1h.shape[0]

    def body(d_ref, w_ref, da_ref, db_ref):
        d = d_ref[...]
        da_ref[...] = _dot_nt(d, w_ref[0:MIX, :])
        db_ref[...] = _dot_nt(d, w_ref[MIX:, :])

    return _pcall(body, name=name, grid=(T // TM,),
                  in_specs=[_rows(TM, D_MODEL), _full((D_MODEL, D_MODEL))],
                  out_specs=[_rows(TM, MIX), _rows(TM, MIX)],
                  out_shape=[_sds((T, MIX), F32), _sds((T, MIX), F32)],
                  sem=("parallel",))(dr1h, wo)


def _in_bwd(pieces, win, dr1, name):
    T = dr1.shape[0]

    def body(p0, p1, p2, p3, w_ref, dr_ref, o_ref):
        acc = ALPHA * dr_ref[...]
        for s, pr in enumerate((p0, p1, p2, p3)):
            acc = acc + _dot_nt(pr[...], w_ref[:, s * MIX:(s + 1) * MIX])
        o_ref[...] = acc

    return _pcall(body, name=name, grid=(T // TM,),
                  in_specs=[_rows(TM, MIX)] * 4 + [_full((D_MODEL, 4 * MIX)), _rows(TM, D_MODEL)],
                  out_specs=_rows(TM, D_MODEL), out_shape=_sds((T, D_MODEL), F32),
                  sem=("parallel",))(*pieces, win, dr1)


def _s5_operators(lre, lim, log_dt, bre, bim, cre, cim, dskip):
    G, P, H, LC = S5_GROUPS, S5_STATE, S5_GROUP, S5_LC
    dt = jnp.exp(log_dt)[:, None]
    mag = jnp.exp(lre * dt)
    ang = lim * dt
    lb_re = mag * jnp.cos(ang)
    lb_im = mag * jnp.sin(ang)
    den = lre * lre + lim * lim
    nr = lb_re - 1.0
    ni = lb_im
    r_re = (nr * lre + ni * lim) / den
    r_im = (ni * lre - nr * lim) / den
    bb_re = r_re[..., None] * bre - r_im[..., None] * bim
    bb_im = r_re[..., None] * bim + r_im[..., None] * bre
    k = jnp.arange(LC + 1, dtype=F32)[:, None, None]
    pmag = jnp.exp(k * (lre * dt)[None])
    pang = k * ang[None]
    pw_re = pmag * jnp.cos(pang)
    pw_im = pmag * jnp.sin(pang)
    cp_re = cre[None] * pw_re[:, :, None, :] - cim[None] * pw_im[:, :, None, :]
    cp_im = cre[None] * pw_im[:, :, None, :] + cim[None] * pw_re[:, :, None, :]
    kk = (jnp.einsum('kghp,gpj->kghj', cp_re[:LC], bb_re, precision=HI)
          - jnp.einsum('kghp,gpj->kghj', cp_im[:LC], bb_im, precision=HI))
    dmat = dskip.reshape(G, H)[:, :, None] * jnp.eye(H, dtype=F32)[None]
    kk = jnp.concatenate([kk[:1] + dmat[None], kk[1:]], axis=0)
    s_idx = jnp.arange(LC)[:, None]
    t_idx = jnp.arange(LC)[None, :]
    onehot = (t_idx - s_idx == jnp.arange(LC)[:, None, None]).astype(F32)
    mt = jnp.einsum('kgab,kst->gsbta', kk, onehot, precision=HI).reshape(G, LC * H, LC * H)
    qt = jnp.stack([cp_re[1:], -cp_im[1:]], axis=0)
    qt = jnp.transpose(qt, (2, 0, 4, 1, 3)).reshape(G, 2 * P, LC * H)
    pb_re = pw_re[:LC, :, :, None] * bb_re[None] - pw_im[:LC, :, :, None] * bb_im[None]
    pb_im = pw_re[:LC, :, :, None] * bb_im[None] + pw_im[:LC, :, :, None] * bb_re[None]
    pm = jnp.stack([pb_re[::-1], pb_im[::-1]], axis=0)
    pm = jnp.transpose(pm, (2, 1, 4, 0, 3)).reshape(G, LC * H, 2 * P)
    a_re = pw_re[LC]
    a_im = pw_im[LC]
    a1 = jnp.concatenate([a_re, a_re], axis=-1)
    a2 = jnp.concatenate([-a_im, a_im], axis=-1)
    return mt, qt, pm, a1, a2


def _to_chunks(u):
    T = u.shape[0]
    return jnp.transpose(u.reshape(T // S5_LC, S5_LC, S5_GROUPS, S5_GROUP), (2, 0, 1, 3)).reshape(
        S5_GROUPS, T // S5_LC, S5_LW)


def _from_chunks(m):
    nc = m.shape[1]
    return jnp.transpose(m.reshape(S5_GROUPS, nc, S5_LC, S5_GROUP), (1, 2, 0, 3)).reshape(nc * S5_LC, MIX)


def _gspec(r, c):
    return pl.BlockSpec((None, r, c), lambda g: (g, 0, 0))


def _s5_chunk_state(umat, pm, name):
    G, nc, _ = umat.shape

    def body(u_ref, p_ref, o_ref):
        o_ref[...] = _dot(u_ref[...], p_ref[...], HI)

    return _pcall(body, name=name, grid=(G,), in_specs=[_gspec(nc, S5_LW), _gspec(S5_LW, 128)],
                  out_specs=_gspec(nc, 128), out_shape=_sds((G, nc, 128), F32), sem=("parallel",))(umat, pm)


_SCAN_G = 8


def _s5_scan_fwd(s_t, a1, a2, name):
    nc, G, _ = s_t.shape

    def body(s_ref, a1_ref, a2_ref, o_ref):
        a1v = a1_ref[...]
        a2v = a2_ref[...]

        def step(c, x):
            o_ref[c] = x
            return a1v * x + a2v * pltpu.roll(x, 64, 1) + s_ref[c]

        lax.fori_loop(0, nc, step, jnp.zeros((_SCAN_G, 128), F32))

    blk = pl.BlockSpec((nc, _SCAN_G, 128), lambda g: (0, g, 0))
    vec = pl.BlockSpec((_SCAN_G, 128), lambda g: (g, 0))
    return _pcall(body, name=name, grid=(G // _SCAN_G,), in_specs=[blk, vec, vec], out_specs=blk,
                  out_shape=_sds((nc, G, 128), F32), sem=("parallel",))(s_t, a1, a2)


def _s5_scan_bwd(dxp_t, xp_t, a1, a2, name):
    nc, G, _ = dxp_t.shape

    def body(dx_ref, x_ref, a1_ref, a2_ref, ds_ref, da1_ref, da2_ref):
        a1v = a1_ref[...]
        a2v = a2_ref[...]
        zero = jnp.zeros((_SCAN_G, 128), F32)

        def step(n, carry):
            gc, d1, d2 = carry
            c = nc - 1 - n
            ds_ref[c] = gc
            xp = x_ref[c]
            d1 = d1 + gc * xp
            d2 = d2 + gc * pltpu.roll(xp, 64, 1)
            gc = dx_ref[c] + a1v * gc + pltpu.roll(a2v * gc, 64, 1)
            return gc, d1, d2

        _, d1, d2 = lax.fori_loop(0, nc, step, (zero, zero, zero))
        da1_ref[...] = d1
        da2_ref[...] = d2

    blk = pl.BlockSpec((nc, _SCAN_G, 128), lambda g: (0, g, 0))
    vec = pl.BlockSpec((_SCAN_G, 128), lambda g: (g, 0))
    return _pcall(body, name=name, grid=(G // _SCAN_G,), in_specs=[blk, blk, vec, vec],
                  out_specs=[blk, vec, vec],
                  out_shape=[_sds((nc, G, 128), F32), _sds((G, 128), F32), _sds((G, 128), F32)],
                  sem=("parallel",))(dxp_t, xp_t, a1, a2)


def _s5_output(umat, xprev, mt, qt, name):
    G, nc, _ = umat.shape

    def body(u_ref, x_ref, m_ref, q_ref, o_ref):
        o_ref[...] = _dot(u_ref[...], m_ref[...], HI) + _dot(x_ref[...], q_ref[...], HI)

    return _pcall(body, name=name, grid=(G,),
                  in_specs=[_gspec(nc, S5_LW), _gspec(nc, 128), _gspec(S5_LW, S5_LW), _gspec(128, S5_LW)],
                  out_specs=_gspec(nc, S5_LW), out_shape=_sds((G, nc, S5_LW), F32),
                  sem=("parallel",))(umat, xprev, mt, qt)


def _s5_bwd_state(dymat, qt, name):
    G, nc, _ = dymat.shape

    def body(d_ref, q_ref, o_ref):
        o_ref[...] = _dot_nt(d_ref[...], q_ref[...], HI)

    return _pcall(body, name=name, grid=(G,), in_specs=[_gspec(nc, S5_LW), _gspec(128, S5_LW)],
                  out_specs=_gspec(nc, 128), out_shape=_sds((G, nc, 128), F32), sem=("parallel",))(dymat, qt)


def _s5_bwd_main(umat, xprev, dymat, ds, mt, pm, name):
    G, nc, _ = umat.shape

    def body(u_ref, x_ref, dy_ref, ds_ref, m_ref, p_ref, du_ref, dm_ref, dq_ref, dp_ref):
        u = u_ref[...]
        dy = dy_ref[...]
        dsv = ds_ref[...]
        du_ref[...] = _dot_nt(dy, m_ref[...], HI) + _dot_nt(dsv, p_ref[...], HI)
        dm_ref[...] = _dot_tn(u, dy, HI)
        dq_ref[...] = _dot_tn(x_ref[...], dy, HI)
        dp_ref[...] = _dot_tn(u, dsv, HI)

    return _pcall(body, name=name, grid=(G,),
                  in_specs=[_gspec(nc, S5_LW), _gspec(nc, 128), _gspec(nc, S5_LW), _gspec(nc, 128),
                            _gspec(S5_LW, S5_LW), _gspec(S5_LW, 128)],
                  out_specs=[_gspec(nc, S5_LW), _gspec(S5_LW, S5_LW), _gspec(128, S5_LW), _gspec(S5_LW, 128)],
                  out_shape=[_sds((G, nc, S5_LW), F32), _sds((G, S5_LW, S5_LW), F32), _sds((G, 128, S5_LW), F32),
                             _sds((G, S5_LW, 128), F32)],
                  sem=("parallel",))(umat, xprev, dymat, ds, mt, pm)


def _glu_fwd(y, wglu, bglu, name):
    T = y.shape[0]

    def body(y_ref, w_ref, b_ref, o_ref, g_ref):
        g = _gelu(y_ref[...])
        gh = g.astype(BF16)
        z = _dot(gh, w_ref[...]) + b_ref[...]
        o_ref[...] = (g * _sigmoid(z)).astype(BF16)
        g_ref[...] = gh

    return _pcall(body, name=name, grid=(T // TM,),
                  in_specs=[_rows(TM, MIX), _full((MIX, MIX)), _full((1, MIX))],
                  out_specs=[_rows(TM, MIX), _rows(TM, MIX)],
                  out_shape=[_sds((T, MIX), BF16), _sds((T, MIX), BF16)], sem=("parallel",))(y, wglu, bglu)


def _glu_bwd(y, dout, wglu, bglu, name):
    T = y.shape[0]

    def body(y_ref, do_ref, w_ref, b_ref, dy_ref, dz_ref, db_ref):
        yv = y_ref[...]
        do = do_ref[...]
        g = _gelu(yv)
        s = _sigmoid(_dot(g.astype(BF16), w_ref[...]) + b_ref[...])
        dz = do * g * s * (1.0 - s)
        dzh = dz.astype(BF16)
        dz_ref[...] = dzh
        dg = do * s + _dot_nt(dzh, w_ref[...])
        dy_ref[...] = dg * _gelu_grad(yv)

        @pl.when(pl.program_id(0) == 0)
        def _():
            db_ref[...] = jnp.zeros_like(db_ref)

        db_ref[...] += _colsum(dz)

    return _pcall(body, name=name, grid=(T // TM,),
                  in_specs=[_rows(TM, MIX), _rows(TM, MIX), _full((MIX, MIX)), _full((1, MIX))],
                  out_specs=[_rows(TM, MIX), _rows(TM, MIX), _full((1, MIX))],
                  out_shape=[_sds((T, MIX), F32), _sds((T, MIX), BF16), _sds((1, MIX), F32)])(y, dout, wglu, bglu)


def _prev_rows(T, h):
    return pl.BlockSpec((h, MIX), lambda i: (jnp.maximum(i * (TM // h) - 1, 0), 0))


def _next_rows(T, h):
    return pl.BlockSpec((h, MIX), lambda i: (jnp.minimum((i + 1) * (TM // h), T // h - 1), 0))


def _conv_fwd(bg, cg, xb, w, name):
    T = bg.shape[0]

    def body(b_ref, c_ref, x_ref, ch_ref, xh_ref, w_ref, o_ref, ext):
        i = pl.program_id(0)
        z = c_ref[...] * x_ref[...]
        ext[0:8, :] = jnp.where(i > 0, ch_ref[...] * xh_ref[...], 0.0)
        ext[8:, :] = z
        y = (w_ref[0:1, :] * ext[pl.ds(6, TM), :] + w_ref[1:2, :] * ext[pl.ds(7, TM), :] + w_ref[2:3, :] * z)
        o_ref[...] = (b_ref[...] * y).astype(BF16)

    blk = _rows(TM, MIX)
    return _pcall(body, name=name, grid=(T // TM,),
                  in_specs=[blk, blk, blk, _prev_rows(T, 8), _prev_rows(T, 8), _full((3, MIX))],
                  out_specs=blk, out_shape=_sds((T, MIX), BF16),
                  scratch=[pltpu.VMEM((TM + 8, MIX), F32)], sem=("parallel",))(bg, cg, xb, cg, xb, w)


def _conv_bwd(dout, bg, cg, xb, w, name):
    T = bg.shape[0]
    nb = T // TM

    def body(do_ref, b_ref, c_ref, x_ref, ch_ref, xh_ref, don_ref, bn_ref, w_ref,
             db_ref, dc_ref, dx_ref, dw_ref, ext, ext2):
        i = pl.program_id(0)
        c = c_ref[...]
        x = x_ref[...]
        z = c * x
        ext[0:8, :] = jnp.where(i > 0, ch_ref[...] * xh_ref[...], 0.0)
        ext[8:, :] = z
        zm2 = ext[pl.ds(6, TM), :]
        zm1 = ext[pl.ds(7, TM), :]
        w0 = w_ref[0:1, :]
        w1 = w_ref[1:2, :]
        w2 = w_ref[2:3, :]
        y = w0 * zm2 + w1 * zm1 + w2 * z
        do = do_ref[...]
        dy = do * b_ref[...]
        ext2[0:TM, :] = dy
        ext2[TM:, :] = jnp.where(i < nb - 1, don_ref[...] * bn_ref[...], 0.0)
        dz = w2 * dy + w1 * ext2[pl.ds(1, TM), :] + w0 * ext2[pl.ds(2, TM), :]
        db_ref[...] = (do * y).astype(BF16)
        dc_ref[...] = (dz * x).astype(BF16)
        dx_ref[...] = (dz * c).astype(BF16)

        @pl.when(i == 0)
        def _():
            dw_ref[...] = jnp.zeros_like(dw_ref)

        dw_ref[0:1, :] += _colsum(dy * zm2)
        dw_ref[1:2, :] += _colsum(dy * zm1)
        dw_ref[2:3, :] += _colsum(dy * z)

    blk = _rows(TM, MIX)
    return _pcall(body, name=name, grid=(nb,),
                  in_specs=[blk, blk, blk, blk, _prev_rows(T, 8), _prev_rows(T, 8), _next_rows(T, 8),
                            _next_rows(T, 8), _full((3, MIX))],
                  out_specs=[blk, blk, blk, _full((8, MIX))],
                  out_shape=[_sds((T, MIX), BF16)] * 3 + [_sds((8, MIX), F32)],
                  scratch=[pltpu.VMEM((TM + 8, MIX), F32), pltpu.VMEM((TM + 8, MIX), F32)])(
                      dout, bg, cg, xb, cg, xb, dout, bg, w)


_PH = 16


def _pooled(ext, t, gi, w):
    lo = gi * POOL_GROUP
    cur = ext[pl.ds(_PH, TM), lo:lo + POOL_GROUP]
    acc = cur
    for k in range(1, w):
        acc = acc + ext[pl.ds(_PH - k, TM), lo:lo + POOL_GROUP]
    cnt = jnp.minimum(t + 1, w).astype(F32)
    return acc / cnt - cur, cnt


def _pool_fwd(z, pw, scale, name):
    T = z.shape[0]

    def body(z_ref, zh_ref, pw_ref, sc_ref, o_ref, ext):
        i = pl.program_id(0)
        ext[0:_PH, :] = jnp.where(i > 0, zh_ref[...], 0.0)
        ext[_PH:, :] = z_ref[...]
        t = i * TM + lax.broadcasted_iota(jnp.int32, (TM, 1), 0)
        for gi, w in enumerate(POOL_WINDOWS):
            lo = gi * POOL_GROUP
            pooled, _ = _pooled(ext, t, gi, w)
            mixed = _dot(pooled.astype(BF16), pw_ref[gi].astype(BF16))
            o_ref[:, lo:lo + POOL_GROUP] = (mixed * sc_ref[:, lo:lo + POOL_GROUP]).astype(BF16)

    return _pcall(body, name=name, grid=(T // TM,),
                  in_specs=[_rows(TM, MIX), _prev_rows(T, _PH), _full((4, POOL_GROUP, POOL_GROUP)), _full((1, MIX))],
                  out_specs=_rows(TM, MIX), out_shape=_sds((T, MIX), BF16),
                  scratch=[pltpu.VMEM((TM + _PH, MIX), F32)], sem=("parallel",))(z, z, pw, scale)


def _pool_bwd(dout, z, pw, scale, name):
    T = z.shape[0]
    nb = T // TM

    def body(do_ref, don_ref, z_ref, zh_ref, pw_ref, sc_ref, dz_ref, dpw_ref, dsc_ref, ext, ext2):
        i = pl.program_id(0)
        ext[0:_PH, :] = jnp.where(i > 0, zh_ref[...], 0.0)
        ext[_PH:, :] = z_ref[...]
        t = i * TM + lax.broadcasted_iota(jnp.int32, (TM, 1), 0)

        @pl.when(i == 0)
        def _():
            dpw_ref[...] = jnp.zeros_like(dpw_ref)
            dsc_ref[...] = jnp.zeros_like(dsc_ref)

        for gi, w in enumerate(POOL_WINDOWS):
            lo = gi * POOL_GROUP
            pwb = pw_ref[gi].astype(BF16)
            sc = sc_ref[:, lo:lo + POOL_GROUP]
            pooled, cnt = _pooled(ext, t, gi, w)
            pb = pooled.astype(BF16)
            mixed = _dot(pb, pwb)
            dog = do_ref[:, lo:lo + POOL_GROUP]
            dsc_ref[:, lo:lo + POOL_GROUP] += _colsum(dog * mixed)
            dmix = (dog * sc).astype(BF16)
            dpw_ref[gi] += _dot_tn(pb, dmix)
            dpool = _dot_nt(dmix, pwb)
            dmix_n = (jnp.where(i < nb - 1, don_ref[:, lo:lo + POOL_GROUP], 0.0) * sc).astype(BF16)
            dpool_n = _dot_nt(dmix_n, pwb)
            e = dpool / cnt
            ext2[0:TM, lo:lo + POOL_GROUP] = e
            ext2[TM:, lo:lo + POOL_GROUP] = dpool_n * (1.0 / w)
            s = e
            for k in range(1, w):
                s = s + ext2[pl.ds(k, TM), lo:lo + POOL_GROUP]
            dz_ref[:, lo:lo + POOL_GROUP] = (s - dpool).astype(BF16)

    return _pcall(body, name=name, grid=(nb,),
                  in_specs=[_rows(TM, MIX), _next_rows(T, _PH), _rows(TM, MIX), _prev_rows(T, _PH),
                            _full((4, POOL_GROUP, POOL_GROUP)), _full((1, MIX))],
                  out_specs=[_rows(TM, MIX), _full((4, POOL_GROUP, POOL_GROUP)), _full((1, MIX))],
                  out_shape=[_sds((T, MIX), BF16), _sds((4, POOL_GROUP, POOL_GROUP), F32), _sds((1, MIX), F32)],
                  scratch=[pltpu.VMEM((TM + _PH, MIX), F32), pltpu.VMEM((TM + _PH, MIX), F32)])(
                      dout, dout, z, z, pw, scale)


def _att_bias_table(rel_bias):
    span = LEFT_CHUNKS * CHUNK
    assert ATT_TQ - 1 <= MAX_REL
    lo = MAX_REL - (ATT_TQ - 1)
    near = rel_bias[:, lo:2 * MAX_REL + 1]
    far = jnp.broadcast_to(rel_bias[:, 2 * MAX_REL:], (HEADS, span + ATT_TQ - 1 - MAX_REL))
    by_dist = jnp.concatenate([near, far], axis=1)
    rev = by_dist[:, ::-1]
    rows = [rev[:, ATT_TQ - 1 - r:ATT_TQ - 1 - r + ATT_W] for r in range(ATT_TQ)]
    bias = jnp.stack(rows, axis=1)
    r = jnp.arange(ATT_TQ)[:, None]
    col = jnp.arange(ATT_W)[None, :]
    dchunk = (LEFT_CHUNKS + r // CHUNK) - col // CHUNK
    visible = (dchunk >= 0) & (dchunk <= LEFT_CHUNKS)
    return jnp.where(visible[None], bias, NEG_INF)


def _att_views_back(d):
    return pl.BlockSpec((ATT_TQ, MIX), lambda i: (jnp.maximum(i - (ATT_NV - 1) + d, 0), 0))


def _att_scores(q_ref, k_refs, tb_ref, h, kvalid):
    sl = slice(h * HEAD_DIM, (h + 1) * HEAD_DIM)
    qh = (q_ref[:, sl] * (HEAD_DIM ** -0.5)).astype(BF16)
    kc = jnp.concatenate([r[:, sl] for r in k_refs], axis=0).astype(BF16)
    s = _dot_nt(qh, kc) + tb_ref[h]
    return jnp.where(kvalid, s, NEG_INF), kc


def _att_fwd(q, k, v, table, name):
    T = q.shape[0]

    def body(q_ref, *refs):
        k_refs = refs[:ATT_NV]
        v_refs = refs[ATT_NV:2 * ATT_NV]
        tb_ref = refs[2 * ATT_NV]
        o_ref, oh_ref, lse_ref = refs[2 * ATT_NV + 1:]
        i = pl.program_id(0)
        col = lax.broadcasted_iota(jnp.int32, (1, ATT_W), 1)
        kvalid = (col + (i - (ATT_NV - 1)) * ATT_TQ) >= 0
        for h in range(HEADS):
            sl = slice(h * HEAD_DIM, (h + 1) * HEAD_DIM)
            s, _ = _att_scores(q_ref, k_refs, tb_ref, h, kvalid)
            vc = jnp.concatenate([r[:, sl] for r in v_refs], axis=0).astype(BF16)
            m = jnp.max(s, axis=-1, keepdims=True)
            p = jnp.exp(s - m)
            l = jnp.sum(p, axis=-1, keepdims=True)
            o = _dot(p.astype(BF16), vc) / l
            o_ref[:, sl] = o
            oh_ref[:, sl] = o.astype(BF16)
            lse_ref[:, h:h + 1] = m + jnp.log(l)

    blk = _rows(ATT_TQ, MIX)
    views = [_att_views_back(d) for d in range(ATT_NV)]
    return _pcall(body, name=name, grid=(T // ATT_TQ,),
                  in_specs=[blk] + views + views + [_full((HEADS, ATT_TQ, ATT_W))],
                  out_specs=[blk, blk, _rows(ATT_TQ, HEADS)],
                  out_shape=[_sds((T, MIX), F32), _sds((T, MIX), BF16), _sds((T, HEADS), F32)],
                  sem=("parallel",))(q, *([k] * ATT_NV), *([v] * ATT_NV), table)


def _att_bwd_q(q, k, v, do, o, lse, table, name):
    T = q.shape[0]

    def body(q_ref, *refs):
        k_refs = refs[:ATT_NV]
        v_refs = refs[ATT_NV:2 * ATT_NV]
        do_ref, o_ref, lse_ref, tb_ref, dq_ref, dl_ref, dtb_ref = refs[2 * ATT_NV:]
        i = pl.program_id(0)
        col = lax.broadcasted_iota(jnp.int32, (1, ATT_W), 1)
        kvalid = (col + (i - (ATT_NV - 1)) * ATT_TQ) >= 0

        @pl.when(i == 0)
        def _():
            dtb_ref[...] = jnp.zeros_like(dtb_ref)

        for h in range(HEADS):
            sl = slice(h * HEAD_DIM, (h + 1) * HEAD_DIM)
            s, kc = _att_scores(q_ref, k_refs, tb_ref, h, kvalid)
            vc = jnp.concatenate([r[:, sl] for r in v_refs], axis=0).astype(BF16)
            p = jnp.exp(s - lse_ref[:, h:h + 1])
            doh = do_ref[:, sl]
            delta = jnp.sum(doh * o_ref[:, sl], axis=-1, keepdims=True)
            dp = _dot_nt(doh.astype(BF16), vc)
            ds = p * (dp - delta)
            dtb_ref[h] += ds
            dq_ref[:, sl] = (_dot(ds.astype(BF16), kc) * (HEAD_DIM ** -0.5)).astype(BF16)
            dl_ref[:, h:h + 1] = delta

    blk = _rows(ATT_TQ, MIX)
    views = [_att_views_back(d) for d in range(ATT_NV)]
    tb = _full((HEADS, ATT_TQ, ATT_W))
    return _pcall(body, name=name, grid=(T // ATT_TQ,),
                  in_specs=[blk] + views + views + [blk, blk, _rows(ATT_TQ, HEADS), tb],
                  out_specs=[blk, _rows(ATT_TQ, HEADS), tb],
                  out_shape=[_sds((T, MIX), BF16), _sds((T, HEADS), F32), _sds((HEADS, ATT_TQ, ATT_W), F32)])(
                      q, *([k] * ATT_NV), *([v] * ATT_NV), do, o, lse, table)


def _att_bwd_kv(q, k, v, do, lse, delta, table, name):
    T = q.shape[0]
    nb = T // ATT_TQ

    def fwd_view(d, n):
        return pl.BlockSpec((ATT_TQ, n), lambda j: (jnp.minimum(j + d, nb - 1), 0))

    def body(k_ref, v_ref, *refs):
        q_refs = refs[:ATT_NV]
        do_refs = refs[ATT_NV:2 * ATT_NV]
        lse_refs = refs[2 * ATT_NV:3 * ATT_NV]
        dl_refs = refs[3 * ATT_NV:4 * ATT_NV]
        tb_ref, dk_ref, dv_ref = refs[4 * ATT_NV:]
        j = pl.program_id(0)
        for h in range(HEADS):
            sl = slice(h * HEAD_DIM, (h + 1) * HEAD_DIM)
            kh = k_ref[:, sl].astype(BF16)
            vh = v_ref[:, sl].astype(BF16)
            dk = jnp.zeros((ATT_TQ, HEAD_DIM), F32)
            dv = jnp.zeros((ATT_TQ, HEAD_DIM), F32)
            for d in range(ATT_NV):
                c0 = (ATT_NV - 1 - d) * ATT_TQ
                qh = (q_refs[d][:, sl] * (HEAD_DIM ** -0.5)).astype(BF16)
                doh = do_refs[d][:, sl].astype(BF16)
                s = _dot_nt(qh, kh) + tb_ref[h, :, c0:c0 + ATT_TQ]
                p = jnp.exp(s - lse_refs[d][:, h:h + 1])
                p = jnp.where(j + d <= nb - 1, p, 0.0)
                dv = dv + _dot_tn(p.astype(BF16), doh)
                dp = _dot_nt(doh, vh)
                ds = p * (dp - dl_refs[d][:, h:h + 1])
                dk = dk + _dot_tn(ds.astype(BF16), qh)
            dk_ref[:, sl] = dk.astype(BF16)
            dv_ref[:, sl] = dv.astype(BF16)

    blk = _rows(ATT_TQ, MIX)
    wide = [fwd_view(d, MIX) for d in range(ATT_NV)]
    narrow = [fwd_view(d, HEADS) for d in range(ATT_NV)]
    return _pcall(body, name=name, grid=(nb,),
                  in_specs=[blk, blk] + wide + wide + narrow + narrow + [_full((HEADS, ATT_TQ, ATT_W))],
                  out_specs=[blk, blk], out_shape=[_sds((T, MIX), BF16), _sds((T, MIX), BF16)],
                  sem=("parallel",))(k, v, *([q] * ATT_NV), *([do] * ATT_NV), *([lse] * ATT_NV),
                                     *([delta] * ATT_NV), table)


_MESH = pl.DeviceIdType.MESH
_ANY = pl.BlockSpec(memory_space=pl.ANY)


def _all_gather(x, name):
    R, C = x.shape

    def body(x_ref, out_ref, send_sems, recv_sems, local_sem):
        xi, yi, ci = lax.axis_index("x"), lax.axis_index("y"), lax.axis_index("c")
        me, sibling = (xi, yi, ci), (xi, yi, 1 - ci)
        chips = [(1 - xi, yi), (xi, 1 - yi), (1 - xi, 1 - yi)]

        def slot(px, py, pc):
            return out_ref.at[4 * px + 2 * py + pc]

        def copy(k, block, to, src=None):
            return pltpu.make_async_remote_copy(
                src_ref=slot(*block) if src is None else src, dst_ref=slot(*block),
                send_sem=send_sems.at[k], recv_sem=recv_sems.at[k], device_id=to, device_id_type=_MESH)

        mine = pltpu.make_async_copy(x_ref, slot(*me), local_sem)
        mine.start()
        first = [copy(0, me, sibling, src=x_ref)]
        first += [copy(1 + j, me, (*chip, ci), src=x_ref) for j, chip in enumerate(chips)]
        for cp in first:
            cp.start()
        passed = [copy(4 + j, (*chip, ci), sibling) for j, chip in enumerate(chips)]
        for j, chip in enumerate(chips):
            copy(1 + j, (*chip, ci), me).wait_recv()
            passed[j].start()
        copy(0, sibling, me).wait_recv()
        for j, chip in enumerate(chips):
            copy(4 + j, (*chip, 1 - ci), me).wait_recv()
        for cp in first + passed:
            cp.wait_send()
        mine.wait()

    return pl.pallas_call(
        body, name=name, out_shape=_sds((N_DEV, R, C), x.dtype), in_specs=[_ANY], out_specs=_ANY,
        scratch_shapes=[pltpu.SemaphoreType.DMA((7,)), pltpu.SemaphoreType.DMA((7,)), pltpu.SemaphoreType.DMA(())],
    )(x)


def _all_to_all(s, name):
    _, R, C = s.shape

    def body(s_ref, r_ref, send_sems, recv_sems, local_sem):
        xi, yi, ci = lax.axis_index("x"), lax.axis_index("y"), lax.axis_index("c")
        me = 4 * xi + 2 * yi + ci
        mine = pltpu.make_async_copy(s_ref.at[me], r_ref.at[me], local_sem)
        mine.start()
        copies = []
        for m in range(1, N_DEV):
            px = 1 - xi if m & 4 else xi
            py = 1 - yi if m & 2 else yi
            pc = 1 - ci if m & 1 else ci
            peer = 4 * px + 2 * py + pc
            copies.append((
                pltpu.make_async_remote_copy(src_ref=s_ref.at[peer], dst_ref=r_ref.at[me], send_sem=send_sems.at[m - 1],
                                             recv_sem=recv_sems.at[m - 1], device_id=(px, py, pc), device_id_type=_MESH),
                pltpu.make_async_remote_copy(src_ref=s_ref.at[me], dst_ref=r_ref.at[peer], send_sem=send_sems.at[m - 1],
                                             recv_sem=recv_sems.at[m - 1], device_id=(px, py, pc), device_id_type=_MESH)))
        for send, _ in copies:
            send.start()
        for _, recv in copies:
            recv.wait_recv()
        for send, _ in copies:
            send.wait_send()
        mine.wait()

    return pl.pallas_call(
        body, name=name, out_shape=_sds(s.shape, s.dtype), in_specs=[_ANY], out_specs=_ANY,
        scratch_shapes=[pltpu.SemaphoreType.DMA((7,)), pltpu.SemaphoreType.DMA((7,)), pltpu.SemaphoreType.DMA(())],
    )(s)


def _adamw(parts, w, m, v, name):
    R = w.shape[0]
    c1 = 1.0 - ADAM_B1 ** ADAM_STEP
    c2 = 1.0 - ADAM_B2 ** ADAM_STEP

    def body(p_ref, w_ref, m_ref, v_ref, g_ref, d_ref, mo_ref, vo_ref):
        g = p_ref[0].astype(F32)
        for j in range(1, N_DEV):
            g = g + p_ref[j].astype(F32)
        mn = ADAM_B1 * m_ref[...] + (1.0 - ADAM_B1) * g
        vn = ADAM_B2 * v_ref[...] + (1.0 - ADAM_B2) * (g * g)
        m_hat = mn / c1
        v_hat = vn / c2
        g_ref[...] = g
        d_ref[...] = -ADAM_LR * (m_hat / (jnp.sqrt(v_hat) + ADAM_EPS) + ADAM_WD * w_ref[...])
        mo_ref[...] = mn
        vo_ref[...] = vn

    blk = _rows(FLAT_ROWS, FLAT_COLS)
    return _pcall(body, name=name, grid=(R // FLAT_ROWS,),
                  in_specs=[pl.BlockSpec((N_DEV, FLAT_ROWS, FLAT_COLS), lambda i: (0, i, 0)), blk, blk, blk],
                  out_specs=[blk] * 4, out_shape=[_sds((R, FLAT_COLS), F32)] * 4,
                  sem=("parallel",))(parts, w, m, v)


def _piece_rows(size):
    return -(-size // PIECE) * 16


def _pack_rows(arrays, lead=()):
    parts = []
    total = 0
    for a in arrays:
        flat = a.reshape(lead + (-1,))
        size = flat.shape[-1]
        rows = _piece_rows(size)
        pad = [(0, 0)] * len(lead) + [(0, rows * FLAT_COLS - size)]
        parts.append(jnp.pad(flat, pad).reshape(lead + (rows, FLAT_COLS)))
        total += rows
    tail = -total % FLAT_ROWS
    if tail:
        parts.append(jnp.zeros(lead + (tail, FLAT_COLS), parts[0].dtype))
    return jnp.concatenate(parts, axis=len(lead))


def _unpack_rows(buf, shapes, lead=()):
    out = []
    row = 0
    nl = len(lead)
    for shape in shapes:
        size = math.prod(shape)
        rows = _piece_rows(size)
        piece = lax.slice_in_dim(buf, row, row + rows, axis=nl).reshape(lead + (rows * FLAT_COLS,))
        out.append(lax.slice_in_dim(piece, 0, size, axis=nl).reshape(lead + tuple(shape)))
        row += rows
    return out


def _to_blocks(full, axis):
    shp = full.shape
    split = full.reshape(shp[:axis] + (N_DEV, shp[axis] // N_DEV) + shp[axis + 1:])
    return jnp.moveaxis(split, axis, 0)


def _from_blocks(blocks, axis):
    shp = blocks.shape[1:]
    moved = jnp.moveaxis(blocks, 0, axis)
    return moved.reshape(shp[:axis] + (N_DEV * shp[axis],) + shp[axis + 1:])


def _interleave_up(w):
    lead = w.shape[:-1]
    nj = D_FF // FF_TN
    return jnp.swapaxes(w.reshape(lead + (2, nj, FF_TN)), -3, -2).reshape(lead + (2 * D_FF,))


def _deinterleave_up(w):
    lead = w.shape[:-1]
    nj = D_FF // FF_TN
    return jnp.swapaxes(w.reshape(lead + (nj, 2, FF_TN)), -3, -2).reshape(lead + (2 * D_FF,))


def _local_step(x, p, tgt, W):
    vec = lambda a: a.reshape(1, -1)
    saved = []
    x0 = x
    x0h = x.astype(BF16)
    for i in range(DEPTH):
        L = f"L{i}_"
        j = i // 2
        s = dict(x0=x0, x0h=x0h)
        if i % 2 == 0:
            win = W['ev_w_in'][j]
            ua, bg, cg, xb = _mm_in(x0h, win, L + "mm_in")
            params = tuple(W[n][j] for n in ('ev_lambda_re', 'ev_lambda_im', 'ev_log_dt', 'ev_b_re', 'ev_b_im',
                                             'ev_c_re', 'ev_c_im', 'ev_d'))
            (mt, qt, pm, a1, a2), op_vjp = jax.vjp(_s5_operators, *params)
            umat = _to_chunks(ua)
            st = jnp.transpose(_s5_chunk_state(umat, pm, L + "s5_state"), (1, 0, 2))
            xp_t = _s5_scan_fwd(st, a1, a2, L + "s5_scan")
            xprev = jnp.transpose(xp_t, (1, 0, 2))
            y = _from_chunks(_s5_output(umat, xprev, mt, qt, L + "s5_out"))
            ya, gh = _glu_fwd(y, W['ev_w_glu'][j], vec(W['ev_b_glu'][j]), L + "glu")
            yb = _conv_fwd(bg, cg, xb, W['ev_conv_w'][j], L + "conv")
            s.update(bg=bg, cg=cg, xb=xb, op_vjp=op_vjp, mt=mt, qt=qt, pm=pm, a1=a1, a2=a2, umat=umat, xp_t=xp_t,
                     xprev=xprev, y=y, gh=gh)
            wo = W['ev_w_out'][j]
        else:
            win = W['od_w_in'][j]
            q, k, v, z = _mm_in(x0h, win, L + "mm_in")
            table, tb_vjp = jax.vjp(_att_bias_table, W['od_rel_bias'][j])
            of, ya, lse = _att_fwd(q, k, v, table, L + "att")
            yb = _pool_fwd(z, W['od_pool_w'][j], vec(W['od_pool_scale'][j]), L + "pool")
            s.update(q=q, k=k, v=v, z=z, table=table, tb_vjp=tb_vjp, of=of, lse=lse)
            wo = W['od_w_out'][j]
        g1, b1 = vec(W['ln_mix_g'][i]), vec(W['ln_mix_b'][i])
        g2, b2 = vec(W['ln_ffn_g'][i]), vec(W['ln_ffn_b'][i])
        xhat1, rstd1, x1h = _mm_out_ln(ya, yb, wo, x0, g1, b1, L + "mm_out_ln")
        hf, ah = _mm_up(x1h, W['ffn_w_up'][i], L + "mm_up")
        xhat2, rstd2, x2h = _mm_down_ln(ah, W['ffn_w_down'][i], xhat1, g1, b1, g2, b2, L + "mm_down_ln")
        x3, x3h = _mm_ple(x2h, xhat2, g2, b2, p[i], W['ple_w_gate'][i], vec(W['ple_b_gate'][i]),
                          W['ple_w_proj'][i], L + "mm_ple")
        s.update(win=win, wo=wo, ya=ya, yb=yb, xhat1=xhat1, rstd1=rstd1, x1h=x1h, hf=hf, ah=ah, xhat2=xhat2,
                 rstd2=rstd2, x2h=x2h, g1=g1, g2=g2)
        saved.append(s)
        x0, x0h = x3, x3h

    dx, loss = _loss_head(x0, tgt, "loss_head")

    G = {n: [None] * W[n].shape[0] for n in WEIGHT_NAMES}
    for i in reversed(range(DEPTH)):
        L = f"L{i}_"
        j = i // 2
        s = saved[i]
        dr2, dr2h, dpreh, dpph, dbg, dg2, db2 = _ple_bwd(
            dx, s['x2h'], p[i], W['ple_w_gate'][i], vec(W['ple_b_gate'][i]), W['ple_w_proj'][i], s['xhat2'],
            s['rstd2'], s['g2'], L + "ple_bwd")
        G['ple_w_gate'][i] = _mm_tn(s['x2h'], dpreh, L + "dw_gate")
        G['ple_w_proj'][i] = _mm_tn(p[i], dpph, L + "dw_proj")
        G['ple_b_gate'][i] = dbg[0]
        G['ln_ffn_g'][i] = dg2[0]
        G['ln_ffn_b'][i] = db2[0]
        dhf = _ffn_bwd1(dr2h, W['ffn_w_down'][i], s['hf'], L + "ffn_bwd1")
        G['ffn_w_down'][i] = _mm_tn(s['ah'], dr2h, L + "dw_down")
        G['ffn_w_up'][i] = _mm_tn(s['x1h'], dhf, L + "dw_up")
        dr1, dr1h, dg1, db1 = _ffn_bwd2(dhf, W['ffn_w_up'][i], dr2, s['xhat1'], s['rstd1'], s['g1'], L + "ffn_bwd2")
        G['ln_mix_g'][i] = dg1[0]
        G['ln_mix_b'][i] = db1[0]
        dya, dyb = _out_bwd(dr1h, s['wo'], L + "out_bwd")
        dwo = jnp.concatenate([_mm_tn(s['ya'], dr1h, L + "dw_out_a"), _mm_tn(s['yb'], dr1h, L + "dw_out_b")], axis=0)
        if i % 2 == 0:
            G['ev_w_out'][j] = dwo
            dy, dzh, dbglu = _glu_bwd(s['y'], dya, W['ev_w_glu'][j], vec(W['ev_b_glu'][j]), L + "glu_bwd")
            G['ev_w_glu'][j] = _mm_tn(s['gh'], dzh, L + "dw_glu")
            G['ev_b_glu'][j] = dbglu[0]
            dymat = _to_chunks(dy)
            dxp_t = jnp.transpose(_s5_bwd_state(dymat, s['qt'], L + "s5_bwd_state"), (1, 0, 2))
            ds_t, da1, da2 = _s5_scan_bwd(dxp_t, s['xp_t'], s['a1'], s['a2'], L + "s5_scan_bwd")
            dumat, dmt, dqt, dpm = _s5_bwd_main(s['umat'], s['xprev'], dymat, jnp.transpose(ds_t, (1, 0, 2)),
                                                s['mt'], s['pm'], L + "s5_bwd")
            dparams = s['op_vjp']((dmt, dqt, dpm, da1, da2))
            for n, dpar in zip(('ev_lambda_re', 'ev_lambda_im', 'ev_log_dt', 'ev_b_re', 'ev_b_im', 'ev_c_re',
                                'ev_c_im', 'ev_d'), dparams):
                G[n][j] = dpar
            dua = _from_chunks(dumat).astype(BF16)
            dbgh, dcgh, dxbh, dcw = _conv_bwd(dyb, s['bg'], s['cg'], s['xb'], W['ev_conv_w'][j], L + "conv_bwd")
            G['ev_conv_w'][j] = dcw[:3]
            pieces = [dua, dbgh, dcgh, dxbh]
            wname = 'ev_w_in'
        else:
            G['od_w_out'][j] = dwo
            dq, delta, dtable = _att_bwd_q(s['q'], s['k'], s['v'], dya, s['of'], s['lse'], s['table'], L + "att_bwd_q")
            dk, dv = _att_bwd_kv(s['q'], s['k'], s['v'], dya, s['lse'], delta, s['table'], L + "att_bwd_kv")
            G['od_rel_bias'][j] = s['tb_vjp'](dtable)[0]
            dzh, dpw, dsc = _pool_bwd(dyb, s['z'], W['od_pool_w'][j], vec(W['od_pool_scale'][j]), L + "pool_bwd")
            G['od_pool_w'][j] = dpw
            G['od_pool_scale'][j] = dsc[0]
            pieces = [dq, dk, dv, dzh]
            wname = 'od_w_in'
        G[wname][j] = jnp.concatenate([_mm_tn(s['x0h'], pc, L + f"dw_in{n}") for n, pc in enumerate(pieces)], axis=1)
        dx = _in_bwd(pieces, s['win'], dr1, L + "in_bwd")

    grads = {n: jnp.stack(G[n], axis=0) for n in WEIGHT_NAMES}
    return loss, dx, grads


def _step(x, p, tgt, W, M, V):
    shard_shapes = [W[n].shape for n in SHARDED]
    repl_shapes = [W[n].shape for n in REPLICATED]

    gathered = _all_gather(_pack_rows([W[n].astype(BF16) for n in SHARDED]), "gather_weights")
    full = {}
    for n, blocks in zip(SHARDED, _unpack_rows(gathered, shard_shapes, lead=(N_DEV,))):
        full[n] = _from_blocks(blocks, SHARD_AXIS[n])
    full['ffn_w_up'] = _interleave_up(full['ffn_w_up'])
    for n in ('ev_conv_w', 'od_pool_scale'):
        full[n] = full[n].astype(F32)
    for n in REPLICATED:
        full[n] = W[n]

    loss, dx, grads = _local_step(x[0], p[:, 0], tgt[0], full)
    grads['ffn_w_up'] = _deinterleave_up(grads['ffn_w_up'])

    send = _pack_rows([_to_blocks(grads[n], SHARD_AXIS[n]).astype(BF16) for n in SHARDED], lead=(N_DEV,))
    parts = _all_to_all(send, "scatter_grads")
    outs = _adamw(parts, _pack_rows([W[n] for n in SHARDED]), _pack_rows([M[n] for n in SHARDED]),
                  _pack_rows([V[n] for n in SHARDED]), "adamw_sharded")
    res = {}
    for kind, buf in zip(('grad', 'delta', 'm', 'v'), outs):
        for n, a in zip(SHARDED, _unpack_rows(buf, shard_shapes)):
            res[kind, n] = a

    small = _pack_rows([grads[n] for n in REPLICATED] + [loss])
    parts = _all_gather(small, "gather_small_grads")
    zero = jnp.zeros((1, 1), F32)
    outs = _adamw(parts, _pack_rows([W[n] for n in REPLICATED] + [zero]),
                  _pack_rows([M[n] for n in REPLICATED] + [zero]),
                  _pack_rows([V[n] for n in REPLICATED] + [zero]), "adamw_replicated")
    for kind, buf in zip(('grad', 'delta', 'm', 'v'), outs):
        arrays = _unpack_rows(buf, repl_shapes + [(1, 1)])
        for n, a in zip(REPLICATED, arrays):
            res[kind, n] = a
        if kind == 'grad':
            total_loss = arrays[-1].reshape(())

    out = [total_loss, dx[None]]
    for kind in ('grad', 'delta', 'm', 'v'):
        out += [res[kind, n] for n in WEIGHT_NAMES]
    return tuple(out)


def kernel(x, p, ev_w_in, ev_lambda_re, ev_lambda_im, ev_log_dt, ev_b_re, ev_b_im, ev_c_re, ev_c_im, ev_d, ev_w_glu, ev_b_glu, ev_conv_w, ev_w_out, od_w_in, od_rel_bias, od_pool_w, od_pool_scale, od_w_out, ln_mix_g, ln_mix_b, ln_ffn_g, ln_ffn_b, ffn_w_up, ffn_w_down, ple_w_proj, ple_w_gate, ple_b_gate, loss_target, m_ev_w_in, m_ev_lambda_re, m_ev_lambda_im, m_ev_log_dt, m_ev_b_re, m_ev_b_im, m_ev_c_re, m_ev_c_im, m_ev_d, m_ev_w_glu, m_ev_b_glu, m_ev_conv_w, m_ev_w_out, m_od_w_in, m_od_rel_bias, m_od_pool_w, m_od_pool_scale, m_od_w_out, m_ln_mix_g, m_ln_mix_b, m_ln_ffn_g, m_ln_ffn_b, m_ffn_w_up, m_ffn_w_down, m_ple_w_proj, m_ple_w_gate, m_ple_b_gate, v_ev_w_in, v_ev_lambda_re, v_ev_lambda_im, v_ev_log_dt, v_ev_b_re, v_ev_b_im, v_ev_c_re, v_ev_c_im, v_ev_d, v_ev_w_glu, v_ev_b_glu, v_ev_conv_w, v_ev_w_out, v_od_w_in, v_od_rel_bias, v_od_pool_w, v_od_pool_scale, v_od_w_out, v_ln_mix_g, v_ln_mix_b, v_ln_ffn_g, v_ln_ffn_b, v_ffn_w_up, v_ffn_w_down, v_ple_w_proj, v_ple_w_gate, v_ple_b_gate):
    given = dict(locals())
    W = {n: given[n] for n in WEIGHT_NAMES}
    M = {n: given["m_" + n] for n in WEIGHT_NAMES}
    V = {n: given["v_" + n] for n in WEIGHT_NAMES}
    return _step(x, p, loss_target, W, M, V)
```

```python
import math

import jax
import jax.numpy as jnp
from jax import lax
from jax.experimental import pallas as pl
from jax.experimental.pallas import tpu as pltpu

F32 = jnp.float32
BF16 = jnp.bfloat16
HI = lax.Precision.HIGHEST

D_MODEL = 1024
DEPTH = 4
CHUNK = 64
MIX = 512
S5_GROUP = 16
S5_GROUPS = 32
S5_STATE = 64
HEADS = 8
HEAD_DIM = 64
LEFT_CHUNKS = 8
MAX_REL = 128
POOL_WINDOWS = (2, 4, 8, 16)
POOL_GROUP = 128
D_FF = 2816
D_PLE = 256
ALPHA = (2 * DEPTH) ** 0.25
LN_EPS = 1e-5
NEG_INF = -1e30
ADAM_LR = 0.001
ADAM_B1 = 0.9
ADAM_B2 = 0.999
ADAM_EPS = 1e-08
ADAM_WD = 0.01
ADAM_STEP = 10
N_DEV = 8

WEIGHT_NAMES = ['ev_w_in', 'ev_lambda_re', 'ev_lambda_im', 'ev_log_dt', 'ev_b_re', 'ev_b_im', 'ev_c_re', 'ev_c_im',
                'ev_d', 'ev_w_glu', 'ev_b_glu', 'ev_conv_w', 'ev_w_out', 'od_w_in', 'od_rel_bias', 'od_pool_w',
                'od_pool_scale', 'od_w_out', 'ln_mix_g', 'ln_mix_b', 'ln_ffn_g', 'ln_ffn_b', 'ffn_w_up', 'ffn_w_down',
                'ple_w_proj', 'ple_w_gate', 'ple_b_gate']
SHARD_AXIS = {'ev_w_in': 2, 'ev_w_glu': 1, 'ev_conv_w': 2, 'ev_w_out': 1, 'od_w_in': 2, 'od_pool_scale': 1,
              'od_w_out': 1, 'ffn_w_up': 2, 'ffn_w_down': 1, 'ple_w_proj': 2, 'ple_w_gate': 1}
REPLICATED = [n for n in WEIGHT_NAMES if n not in SHARD_AXIS]
ROW_SHARDED = ['ev_w_out', 'od_w_out', 'ffn_w_down', 'ple_w_gate']
COL_IN = ['ev_w_in', 'od_w_in']
SMALL_SHARDED = ['ev_w_glu', 'ev_conv_w', 'od_pool_scale', 'ple_w_proj']

VMEM_LIMIT = 48 * 1024 * 1024
TM = 512
IN_NB = 4 * MIX // N_DEV
FF_NB = 2 * D_FF // N_DEV
FF_J = N_DEV // 2
S5_LC = 32
S5_LW = S5_LC * S5_GROUP
ATT_TQ = 128
ATT_NV = LEFT_CHUNKS * CHUNK // ATT_TQ + 1
ATT_W = ATT_NV * ATT_TQ
FLAT_COLS = 1024
FLAT_ROWS = 256
PIECE = 16 * FLAT_COLS


def _sds(shape, dt):
    return jax.ShapeDtypeStruct(shape, dt)


def _pcall(body, *, name, grid, in_specs, out_specs, out_shape, scratch=(), sem=None):
    sem = sem or ("arbitrary",) * len(grid)
    return pl.pallas_call(
        body, name=name, grid=grid, in_specs=in_specs, out_specs=out_specs, out_shape=out_shape,
        scratch_shapes=scratch,
        compiler_params=pltpu.CompilerParams(dimension_semantics=sem, vmem_limit_bytes=VMEM_LIMIT))


def _rows(tm, n):
    return pl.BlockSpec((tm, n), lambda i: (i, 0))


def _cols(tm, s):
    return pl.BlockSpec((tm, MIX), lambda i: (i, s))


def _full(shape):
    nd = len(shape)
    return pl.BlockSpec(shape, lambda *_: (0,) * nd)


def _layer(shape, l):
    nd = len(shape)
    return pl.BlockSpec((None,) + tuple(shape), lambda *_: (l,) + (0,) * nd)


def _vecs(a):
    return a.reshape(a.shape[0], 1, a.shape[1])


def _dot(a, b, precision=None):
    return jnp.dot(a, b, preferred_element_type=F32, precision=precision)


def _dot_nt(a, b, precision=None):
    return lax.dot_general(a, b, (((1,), (1,)), ((), ())), preferred_element_type=F32, precision=precision)


def _dot_tn(a, b, precision=None):
    return lax.dot_general(a, b, (((0,), (0,)), ((), ())), preferred_element_type=F32, precision=precision)


def _sigmoid(x):
    return 1.0 / (1.0 + jnp.exp(-x))


_GELU_C = math.sqrt(2.0 / math.pi)


def _gelu(x):
    return 0.5 * x * (1.0 + jnp.tanh(_GELU_C * (x + 0.044715 * x * x * x)))


def _gelu_grad(x):
    t = jnp.tanh(_GELU_C * (x + 0.044715 * x * x * x))
    return 0.5 * (1.0 + t) + 0.5 * x * (1.0 - t * t) * _GELU_C * (1.0 + 3.0 * 0.044715 * x * x)


def _ln_fwd(r, g, b):
    mu = jnp.mean(r, axis=-1, keepdims=True)
    xc = r - mu
    var = jnp.mean(xc * xc, axis=-1, keepdims=True)
    rstd = lax.rsqrt(var + LN_EPS)
    xhat = xc * rstd
    return xhat, rstd, xhat * g + b


def _ln_bwd(dx, xhat, rstd, g):
    dxh = dx * g
    m1 = jnp.mean(dxh, axis=-1, keepdims=True)
    m2 = jnp.mean(dxh * xhat, axis=-1, keepdims=True)
    return rstd * (dxh - m1 - xhat * m2)


def _colsum(x):
    return jnp.sum(x, axis=0, keepdims=True)


def _mm_in(xh, win8, l, name):
    T = xh.shape[0]

    def body(x_ref, w_ref, h_ref):
        x = x_ref[...]
        for b in range(N_DEV):
            h_ref[:, b * IN_NB:(b + 1) * IN_NB] = _dot(x, w_ref[b])

    return _pcall(body, name=name, grid=(T // TM,),
                  in_specs=[_rows(TM, D_MODEL),
                            pl.BlockSpec((N_DEV, None, D_MODEL, IN_NB), lambda i: (0, l, 0, 0))],
                  out_specs=_rows(TM, 4 * MIX), out_shape=_sds((T, 4 * MIX), F32),
                  sem=("parallel",))(xh, win8)


def _tile_of(n, cap):
    best = None
    for t in range(128, min(n, cap) + 1, 128):
        if n % t == 0:
            best = t
    assert best is not None, n
    return best


def _mm_tn(a, b, name, a_layer=None, blocked_n=None):
    T, M = a.shape[-2:]
    N = b.shape[1]
    tm = _tile_of(M, 1408)
    tn = blocked_n or _tile_of(N, 512)
    tk = min(T, 1024)
    nk = T // tk

    def body(a_ref, b_ref, o_ref, acc):
        k = pl.program_id(2)

        @pl.when(k == 0)
        def _():
            acc[...] = jnp.zeros_like(acc)

        acc[...] += _dot_tn(a_ref[...].astype(BF16), b_ref[...].astype(BF16))

        @pl.when(k == nk - 1)
        def _():
            o_ref[...] = acc[...].astype(BF16)

    if a_layer is None:
        a_spec = pl.BlockSpec((tk, tm), lambda i, j, k: (k, i))
    else:
        a_spec = pl.BlockSpec((None, tk, tm), lambda i, j, k: (a_layer, k, i))
    if blocked_n:
        o_spec = pl.BlockSpec((None, tm, tn), lambda i, j, k: (j, i, 0))
        o_shape = _sds((N // tn, M, tn), BF16)
    else:
        o_spec = pl.BlockSpec((tm, tn), lambda i, j, k: (i, j))
        o_shape = _sds((M, N), BF16)
    return _pcall(body, name=name, grid=(M // tm, N // tn, nk),
                  in_specs=[a_spec, pl.BlockSpec((tk, tn), lambda i, j, k: (k, j))],
                  out_specs=o_spec, out_shape=o_shape, scratch=[pltpu.VMEM((tm, tn), F32)],
                  sem=("parallel", "parallel", "arbitrary"))(a, b)


def _mm_tn_bblk(a, b3, name, per_step=2):
    T, M = a.shape
    NB, _, n = b3.shape
    tk = min(T, 1024)
    nk = T // tk

    def body(a_ref, b_ref, o_ref, acc):
        k = pl.program_id(1)

        @pl.when(k == 0)
        def _():
            acc[...] = jnp.zeros_like(acc)

        av = a_ref[...]
        for s in range(per_step):
            acc[s] += _dot_tn(av, b_ref[s])

        @pl.when(k == nk - 1)
        def _():
            o_ref[...] = acc[...].astype(BF16)

    return _pcall(body, name=name, grid=(NB // per_step, nk),
                  in_specs=[pl.BlockSpec((tk, M), lambda j, k: (k, 0)),
                            pl.BlockSpec((per_step, tk, n), lambda j, k: (j, k, 0))],
                  out_specs=pl.BlockSpec((per_step, M, n), lambda j, k: (j, 0, 0)),
                  out_shape=_sds((NB, M, n), BF16), scratch=[pltpu.VMEM((per_step, M, n), F32)],
                  sem=("parallel", "arbitrary"))(a, b3)


def _mm_tn_ablk(a3, b, name):
    NA, T, m = a3.shape
    N = b.shape[1]
    tn = _tile_of(N, 512)
    tk = min(T, 1024)
    nk = T // tk

    def body(a_ref, b_ref, o_ref, acc):
        k = pl.program_id(2)

        @pl.when(k == 0)
        def _():
            acc[...] = jnp.zeros_like(acc)

        acc[...] += _dot_tn(a_ref[...], b_ref[...])

        @pl.when(k == nk - 1)
        def _():
            o_ref[...] = acc[...].astype(BF16)

    return _pcall(body, name=name, grid=(NA, N // tn, nk),
                  in_specs=[pl.BlockSpec((None, tk, m), lambda j, n, k: (j, k, 0)),
                            pl.BlockSpec((tk, tn), lambda j, n, k: (k, n))],
                  out_specs=pl.BlockSpec((None, m, tn), lambda j, n, k: (j, 0, n)),
                  out_shape=_sds((NA, m, N), BF16), scratch=[pltpu.VMEM((m, tn), F32)],
                  sem=("parallel", "parallel", "arbitrary"))(a3, b)


def _mm_out_ln(ya, yb, wo, l, x0, g, b, name):
    T = x0.shape[0]

    def body(ya_ref, yb_ref, w_ref, x0_ref, g_ref, b_ref, xh_ref, rs_ref, x1_ref):
        r = ALPHA * x0_ref[...] + _dot(ya_ref[...], w_ref[0:MIX, :]) + _dot(yb_ref[...], w_ref[MIX:, :])
        xhat, rstd, x1 = _ln_fwd(r, g_ref[...], b_ref[...])
        xh_ref[...] = xhat
        rs_ref[...] = rstd
        x1_ref[...] = x1.astype(BF16)

    vec = _layer((1, D_MODEL), g[1])
    return _pcall(body, name=name, grid=(T // TM,),
                  in_specs=[_rows(TM, MIX), _rows(TM, MIX), _layer((D_MODEL, D_MODEL), l), _rows(TM, D_MODEL),
                            vec, vec],
                  out_specs=[_rows(TM, D_MODEL), _rows(TM, 1), _rows(TM, D_MODEL)],
                  out_shape=[_sds((T, D_MODEL), F32), _sds((T, 1), F32), _sds((T, D_MODEL), BF16)],
                  sem=("parallel",))(ya, yb, wo, x0, g[0], b[0])


def _mm_up(x1h, wup, l, name):
    T = x1h.shape[0]

    def body(x_ref, w_ref, hf_ref, a_ref):
        x = x_ref[...]
        g = _dot(x, w_ref[0])
        u = _dot(x, w_ref[1])
        hf_ref[0] = g.astype(BF16)
        hf_ref[1] = u.astype(BF16)
        a_ref[...] = (g * _sigmoid(g) * u).astype(BF16)

    return _pcall(body, name=name, grid=(FF_J, T // TM),
                  in_specs=[pl.BlockSpec((TM, D_MODEL), lambda j, i: (i, 0)),
                            pl.BlockSpec((2, None, None, D_MODEL, FF_NB), lambda j, i: (0, j, l, 0, 0))],
                  out_specs=[pl.BlockSpec((2, None, TM, FF_NB), lambda j, i: (0, j, i, 0)),
                             pl.BlockSpec((None, TM, FF_NB), lambda j, i: (j, i, 0))],
                  out_shape=[_sds((2, FF_J, T, FF_NB), BF16), _sds((FF_J, T, FF_NB), BF16)],
                  sem=("parallel", "parallel"))(x1h, wup)


def _mm_down_ln(a3, wd4, l, xhat1, g1, b1, g2, b2, name):
    T = a3.shape[1]

    def body(a_ref, w_ref, xh1_ref, g1_ref, b1_ref, g2_ref, b2_ref, xh_ref, rs_ref, x2_ref):
        x1 = xh1_ref[...] * g1_ref[...] + b1_ref[...]
        r = ALPHA * x1
        for j in range(FF_J):
            r = r + _dot(a_ref[j], w_ref[j])
        xhat, rstd, x2 = _ln_fwd(r, g2_ref[...], b2_ref[...])
        xh_ref[...] = xhat
        rs_ref[...] = rstd
        x2_ref[...] = x2.astype(BF16)

    vec = _layer((1, D_MODEL), g1[1])
    return _pcall(body, name=name, grid=(T // TM,),
                  in_specs=[pl.BlockSpec((FF_J, TM, FF_NB), lambda i: (0, i, 0)),
                            _layer((FF_J, FF_NB, D_MODEL), l), _rows(TM, D_MODEL), vec, vec, vec, vec],
                  out_specs=[_rows(TM, D_MODEL), _rows(TM, 1), _rows(TM, D_MODEL)],
                  out_shape=[_sds((T, D_MODEL), F32), _sds((T, 1), F32), _sds((T, D_MODEL), BF16)],
                  sem=("parallel",))(a3, wd4, xhat1, g1[0], b1[0], g2[0], b2[0])


def _mm_ple(x2h, xhat2, g2, b2, p, wg, bg, wp, l, name):
    T = x2h.shape[0]

    def body(x2h_ref, xh_ref, g2_ref, b2_ref, p_ref, wg_ref, bg_ref, wp_ref, o_ref, oh_ref):
        x2 = xh_ref[...] * g2_ref[...] + b2_ref[...]
        gate = _sigmoid(_dot(x2h_ref[...], wg_ref[...]) + bg_ref[...])
        pp = _dot(p_ref[...].astype(BF16), wp_ref[...])
        x3 = x2 + gate * pp
        o_ref[...] = x3
        oh_ref[...] = x3.astype(BF16)

    vec = _layer((1, D_MODEL), l)
    return _pcall(body, name=name, grid=(T // TM,),
                  in_specs=[_rows(TM, D_MODEL), _rows(TM, D_MODEL), vec, vec,
                            pl.BlockSpec((None, TM, D_PLE), lambda i: (l, i, 0)),
                            _layer((D_MODEL, D_MODEL), l), vec, _layer((D_PLE, D_MODEL), l)],
                  out_specs=[_rows(TM, D_MODEL), _rows(TM, D_MODEL)],
                  out_shape=[_sds((T, D_MODEL), F32), _sds((T, D_MODEL), BF16)],
                  sem=("parallel",))(x2h, xhat2, g2, b2, p, wg, bg, wp)


def _loss_head(x3, tgt, name):
    T = x3.shape[0]

    def body(x_ref, t_ref, dx_ref, l_ref):
        e = x_ref[...] - t_ref[...]
        dx_ref[...] = e * (1.0 / D_MODEL)

        @pl.when(pl.program_id(0) == 0)
        def _():
            l_ref[...] = jnp.zeros_like(l_ref)

        l_ref[...] += (0.5 / D_MODEL) * jnp.sum(e * e).reshape(1, 1)

    return _pcall(body, name=name, grid=(T // TM,),
                  in_specs=[_rows(TM, D_MODEL), _rows(TM, D_MODEL)],
                  out_specs=[_rows(TM, D_MODEL), _full((1, 1))],
                  out_shape=[_sds((T, D_MODEL), F32), _sds((1, 1), F32)])(x3, tgt)


def _ple_bwd(dx3, x2h, p, wg, bg, wp, l, xhat2, rstd2, g2, name):
    T = dx3.shape[0]

    def body(dx3_ref, x2h_ref, p_ref, wg_ref, bg_ref, wp_ref, xh_ref, rs_ref, g2_ref,
             dr_ref, drh_ref, dpre_ref, dpp_ref, dbg_ref, dg_ref, db_ref):
        dx3 = dx3_ref[...]
        gate = _sigmoid(_dot(x2h_ref[...], wg_ref[...]) + bg_ref[...])
        pp = _dot(p_ref[...].astype(BF16), wp_ref[...])
        dpre = dx3 * pp * gate * (1.0 - gate)
        dpreh = dpre.astype(BF16)
        dpre_ref[...] = dpreh
        dpp_ref[...] = (dx3 * gate).astype(BF16)
        dx2 = dx3 + _dot_nt(dpreh, wg_ref[...])
        xhat = xh_ref[...]
        dr = _ln_bwd(dx2, xhat, rs_ref[...], g2_ref[...])
        dr_ref[...] = dr
        drh_ref[...] = dr.astype(BF16)

        @pl.when(pl.program_id(0) == 0)
        def _():
            dbg_ref[...] = jnp.zeros_like(dbg_ref)
            dg_ref[...] = jnp.zeros_like(dg_ref)
            db_ref[...] = jnp.zeros_like(db_ref)

        dbg_ref[...] += _colsum(dpre)
        dg_ref[...] += _colsum(dx2 * xhat)
        db_ref[...] += _colsum(dx2)

    vec = _layer((1, D_MODEL), l)
    acc = _full((1, D_MODEL))
    big = _rows(TM, D_MODEL)
    return _pcall(body, name=name, grid=(T // TM,),
                  in_specs=[big, big, pl.BlockSpec((None, TM, D_PLE), lambda i: (l, i, 0)),
                            _layer((D_MODEL, D_MODEL), l), vec, _layer((D_PLE, D_MODEL), l),
                            big, _rows(TM, 1), vec],
                  out_specs=[big, big, big, big, acc, acc, acc],
                  out_shape=[_sds((T, D_MODEL), F32), _sds((T, D_MODEL), BF16), _sds((T, D_MODEL), BF16),
                             _sds((T, D_MODEL), BF16), _sds((1, D_MODEL), F32), _sds((1, D_MODEL), F32),
                             _sds((1, D_MODEL), F32)])(dx3, x2h, p, wg, bg, wp, xhat2, rstd2, g2)


def _ffn_bwd1(dr2h, wd4, l, hf, name):
    T = dr2h.shape[0]

    def body(d_ref, w_ref, hf_ref, o_ref):
        da = _dot_nt(d_ref[...], w_ref[...])
        g = hf_ref[0].astype(F32)
        u = hf_ref[1].astype(F32)
        sg = _sigmoid(g)
        o_ref[0] = (da * u * (sg * (1.0 + g * (1.0 - sg)))).astype(BF16)
        o_ref[1] = (da * (g * sg)).astype(BF16)

    blk = pl.BlockSpec((2, None, TM, FF_NB), lambda j, i: (0, j, i, 0))
    return _pcall(body, name=name, grid=(FF_J, T // TM),
                  in_specs=[pl.BlockSpec((TM, D_MODEL), lambda j, i: (i, 0)),
                            pl.BlockSpec((None, None, FF_NB, D_MODEL), lambda j, i: (l, j, 0, 0)), blk],
                  out_specs=blk, out_shape=_sds((2, FF_J, T, FF_NB), BF16),
                  sem=("parallel", "parallel"))(dr2h, wd4, hf)


_TM_B2 = 256


def _ffn_bwd2(dhf, wup, l, dr2, xhat1, rstd1, g1, name):
    T = dr2.shape[0]
    tm = min(T, _TM_B2)

    def body(dh_ref, w_ref, dr2_ref, xh_ref, rs_ref, g_ref, dr_ref, drh_ref, dg_ref, db_ref):
        dx1 = ALPHA * dr2_ref[...]
        for s in range(2):
            for j in range(FF_J):
                dx1 = dx1 + _dot_nt(dh_ref[s, j], w_ref[s, j])
        xhat = xh_ref[...]
        dr = _ln_bwd(dx1, xhat, rs_ref[...], g_ref[...])
        dr_ref[...] = dr
        drh_ref[...] = dr.astype(BF16)

        @pl.when(pl.program_id(0) == 0)
        def _():
            dg_ref[...] = jnp.zeros_like(dg_ref)
            db_ref[...] = jnp.zeros_like(db_ref)

        dg_ref[...] += _colsum(dx1 * xhat)
        db_ref[...] += _colsum(dx1)

    big = _rows(tm, D_MODEL)
    acc = _full((1, D_MODEL))
    return _pcall(body, name=name, grid=(T // tm,),
                  in_specs=[pl.BlockSpec((2, FF_J, tm, FF_NB), lambda i: (0, 0, i, 0)),
                            pl.BlockSpec((2, FF_J, None, D_MODEL, FF_NB), lambda i: (0, 0, l, 0, 0)),
                            big, big, _rows(tm, 1), _layer((1, D_MODEL), g1[1])],
                  out_specs=[big, big, acc, acc],
                  out_shape=[_sds((T, D_MODEL), F32), _sds((T, D_MODEL), BF16), _sds((1, D_MODEL), F32),
                             _sds((1, D_MODEL), F32)])(dhf, wup, dr2, xhat1, rstd1, g1[0])


def _out_bwd(dr1h, wo, l, name):
    T = dr1h.shape[0]

    def body(d_ref, w_ref, da_ref, db_ref):
        d = d_ref[...]
        da_ref[...] = _dot_nt(d, w_ref[0:MIX, :])
        db_ref[...] = _dot_nt(d, w_ref[MIX:, :])

    return _pcall(body, name=name, grid=(T // TM,),
                  in_specs=[_rows(TM, D_MODEL), _layer((D_MODEL, D_MODEL), l)],
                  out_specs=[_rows(TM, MIX), _rows(TM, MIX)],
                  out_shape=[_sds((T, MIX), F32), _sds((T, MIX), F32)],
                  sem=("parallel",))(dr1h, wo)


def _in_bwd(dh, win8, l, dr1, name):
    T = dr1.shape[0]

    def body(dh_ref, w_ref, dr_ref, o_ref):
        acc = ALPHA * dr_ref[...]
        for b in range(N_DEV):
            acc = acc + _dot_nt(dh_ref[:, b * IN_NB:(b + 1) * IN_NB], w_ref[b])
        o_ref[...] = acc

    return _pcall(body, name=name, grid=(T // TM,),
                  in_specs=[_rows(TM, 4 * MIX), pl.BlockSpec((N_DEV, None, D_MODEL, IN_NB), lambda i: (0, l, 0, 0)),
                            _rows(TM, D_MODEL)],
                  out_specs=_rows(TM, D_MODEL), out_shape=_sds((T, D_MODEL), F32),
                  sem=("parallel",))(dh, win8, dr1)


def _s5_operators(lre, lim, log_dt, bre, bim, cre, cim, dskip):
    G, P, H, LC = S5_GROUPS, S5_STATE, S5_GROUP, S5_LC
    dt = jnp.exp(log_dt)[:, None]
    mag = jnp.exp(lre * dt)
    ang = lim * dt
    lb_re = mag * jnp.cos(ang)
    lb_im = mag * jnp.sin(ang)
    den = lre * lre + lim * lim
    nr = lb_re - 1.0
    ni = lb_im
    r_re = (nr * lre + ni * lim) / den
    r_im = (ni * lre - nr * lim) / den
    bb_re = r_re[..., None] * bre - r_im[..., None] * bim
    bb_im = r_re[..., None] * bim + r_im[..., None] * bre
    k = jnp.arange(LC + 1, dtype=F32)[:, None, None]
    pmag = jnp.exp(k * (lre * dt)[None])
    pang = k * ang[None]
    pw_re = pmag * jnp.cos(pang)
    pw_im = pmag * jnp.sin(pang)
    cp_re = cre[None] * pw_re[:, :, None, :] - cim[None] * pw_im[:, :, None, :]
    cp_im = cre[None] * pw_im[:, :, None, :] + cim[None] * pw_re[:, :, None, :]
    kk = (jnp.einsum('kghp,gpj->kghj', cp_re[:LC], bb_re, precision=HI)
          - jnp.einsum('kghp,gpj->kghj', cp_im[:LC], bb_im, precision=HI))
    dmat = dskip.reshape(G, H)[:, :, None] * jnp.eye(H, dtype=F32)[None]
    kk = jnp.concatenate([kk[:1] + dmat[None], kk[1:]], axis=0)
    s_idx = jnp.arange(LC)[:, None]
    t_idx = jnp.arange(LC)[None, :]
    onehot = (t_idx - s_idx == jnp.arange(LC)[:, None, None]).astype(F32)
    mt = jnp.einsum('kgab,kst->gsbta', kk, onehot, precision=HI).reshape(G, LC * H, LC * H)
    qt = jnp.stack([cp_re[1:], -cp_im[1:]], axis=0)
    qt = jnp.transpose(qt, (2, 0, 4, 1, 3)).reshape(G, 2 * P, LC * H)
    pb_re = pw_re[:LC, :, :, None] * bb_re[None] - pw_im[:LC, :, :, None] * bb_im[None]
    pb_im = pw_re[:LC, :, :, None] * bb_im[None] + pw_im[:LC, :, :, None] * bb_re[None]
    pm = jnp.stack([pb_re[::-1], pb_im[::-1]], axis=0)
    pm = jnp.transpose(pm, (2, 1, 4, 0, 3)).reshape(G, LC * H, 2 * P)
    a_re = pw_re[LC]
    a_im = pw_im[LC]
    a1 = jnp.concatenate([a_re, a_re], axis=-1)
    a2 = jnp.concatenate([-a_im, a_im], axis=-1)
    return mt, qt, pm, a1, a2


def _to_chunks(u):
    T = u.shape[0]
    return jnp.transpose(u.reshape(T // S5_LC, S5_LC, S5_GROUPS, S5_GROUP), (2, 0, 1, 3)).reshape(
        S5_GROUPS, T // S5_LC, S5_LW)


def _from_chunks(m):
    nc = m.shape[1]
    return jnp.transpose(m.reshape(S5_GROUPS, nc, S5_LC, S5_GROUP), (1, 2, 0, 3)).reshape(nc * S5_LC, MIX)


def _gspec(r, c):
    return pl.BlockSpec((None, r, c), lambda g: (g, 0, 0))


def _s5_chunk_state(umat, pm, name):
    G, nc, _ = umat.shape

    def body(u_ref, p_ref, o_ref):
        o_ref[...] = _dot(u_ref[...], p_ref[...], HI)

    return _pcall(body, name=name, grid=(G,), in_specs=[_gspec(nc, S5_LW), _gspec(S5_LW, 128)],
                  out_specs=_gspec(nc, 128), out_shape=_sds((G, nc, 128), F32), sem=("parallel",))(umat, pm)


_SCAN_G = 8
_SCAN_UNROLL = 8


def _s5_scan_fwd(s_t, a1, a2, name):
    nc, G, _ = s_t.shape

    def body(s_ref, a1_ref, a2_ref, o_ref, sb_ref):
        sb_ref[...] = pltpu.roll(s_ref[...], 64, 2)
        a1v = a1_ref[...]
        a2v = a2_ref[...]

        def step(c, carry):
            x, xb = carry
            o_ref[c] = x
            return a1v * x + a2v * xb + s_ref[c], a1v * xb - a2v * x + sb_ref[c]

        zero = jnp.zeros((_SCAN_G, 128), F32)
        lax.fori_loop(0, nc, step, (zero, zero), unroll=_SCAN_UNROLL)

    blk = pl.BlockSpec((nc, _SCAN_G, 128), lambda g: (0, g, 0))
    vec = pl.BlockSpec((_SCAN_G, 128), lambda g: (g, 0))
    return _pcall(body, name=name, grid=(G // _SCAN_G,), in_specs=[blk, vec, vec], out_specs=blk,
                  out_shape=_sds((nc, G, 128), F32), scratch=[pltpu.VMEM((nc, _SCAN_G, 128), F32)],
                  sem=("parallel",))(s_t, a1, a2)


def _s5_scan_bwd(dxp_t, xp_t, a1, a2, name):
    nc, G, _ = dxp_t.shape

    def body(dx_ref, x_ref, a1_ref, a2_ref, ds_ref, da1_ref, da2_ref, dxb_ref):
        dxb_ref[...] = pltpu.roll(dx_ref[...], 64, 2)
        a1v = a1_ref[...]
        a2v = a2_ref[...]
        zero = jnp.zeros((_SCAN_G, 128), F32)

        def step(n, carry):
            gc, gb, d1, d2 = carry
            c = nc - 1 - n
            ds_ref[c] = gc
            xp = x_ref[c]
            d1 = d1 + gc * xp
            d2 = d2 + gc * pltpu.roll(xp, 64, 1)
            return dx_ref[c] + a1v * gc - a2v * gb, dxb_ref[c] + a1v * gb + a2v * gc, d1, d2

        _, _, d1, d2 = lax.fori_loop(0, nc, step, (zero, zero, zero, zero), unroll=_SCAN_UNROLL)
        da1_ref[...] = d1
        da2_ref[...] = d2

    blk = pl.BlockSpec((nc, _SCAN_G, 128), lambda g: (0, g, 0))
    vec = pl.BlockSpec((_SCAN_G, 128), lambda g: (g, 0))
    return _pcall(body, name=name, grid=(G // _SCAN_G,), in_specs=[blk, blk, vec, vec],
                  out_specs=[blk, vec, vec],
                  out_shape=[_sds((nc, G, 128), F32), _sds((G, 128), F32), _sds((G, 128), F32)],
                  scratch=[pltpu.VMEM((nc, _SCAN_G, 128), F32)], sem=("parallel",))(dxp_t, xp_t, a1, a2)


def _s5_output(umat, xprev, mt, qt, name):
    G, nc, _ = umat.shape

    def body(u_ref, x_ref, m_ref, q_ref, o_ref):
        o_ref[...] = _dot(u_ref[...], m_ref[...], HI) + _dot(x_ref[...], q_ref[...], HI)

    return _pcall(body, name=name, grid=(G,),
                  in_specs=[_gspec(nc, S5_LW), _gspec(nc, 128), _gspec(S5_LW, S5_LW), _gspec(128, S5_LW)],
                  out_specs=_gspec(nc, S5_LW), out_shape=_sds((G, nc, S5_LW), F32),
                  sem=("parallel",))(umat, xprev, mt, qt)


def _s5_bwd_state(dymat, qt, name):
    G, nc, _ = dymat.shape

    def body(d_ref, q_ref, o_ref):
        o_ref[...] = _dot_nt(d_ref[...], q_ref[...], HI)

    return _pcall(body, name=name, grid=(G,), in_specs=[_gspec(nc, S5_LW), _gspec(128, S5_LW)],
                  out_specs=_gspec(nc, 128), out_shape=_sds((G, nc, 128), F32), sem=("parallel",))(dymat, qt)


def _s5_bwd_main(umat, xprev, dymat, ds, mt, pm, name):
    G, nc, _ = umat.shape

    def body(u_ref, x_ref, dy_ref, ds_ref, m_ref, p_ref, du_ref, dm_ref, dq_ref, dp_ref):
        u = u_ref[...]
        dy = dy_ref[...]
        dsv = ds_ref[...]
        du_ref[...] = _dot_nt(dy, m_ref[...], HI) + _dot_nt(dsv, p_ref[...], HI)
        dm_ref[...] = _dot_tn(u, dy, HI)
        dq_ref[...] = _dot_tn(x_ref[...], dy, HI)
        dp_ref[...] = _dot_tn(u, dsv, HI)

    return _pcall(body, name=name, grid=(G,),
                  in_specs=[_gspec(nc, S5_LW), _gspec(nc, 128), _gspec(nc, S5_LW), _gspec(nc, 128),
                            _gspec(S5_LW, S5_LW), _gspec(S5_LW, 128)],
                  out_specs=[_gspec(nc, S5_LW), _gspec(S5_LW, S5_LW), _gspec(128, S5_LW), _gspec(S5_LW, 128)],
                  out_shape=[_sds((G, nc, S5_LW), F32), _sds((G, S5_LW, S5_LW), F32), _sds((G, 128, S5_LW), F32),
                             _sds((G, S5_LW, 128), F32)],
                  sem=("parallel",))(umat, xprev, dymat, ds, mt, pm)


def _glu_fwd(y, wglu, bglu, l, name):
    T = y.shape[0]

    def body(y_ref, w_ref, b_ref, o_ref, g_ref):
        g = _gelu(y_ref[...])
        gh = g.astype(BF16)
        z = _dot(gh, w_ref[...]) + b_ref[...]
        o_ref[...] = (g * _sigmoid(z)).astype(BF16)
        g_ref[...] = gh

    return _pcall(body, name=name, grid=(T // TM,),
                  in_specs=[_rows(TM, MIX), _layer((MIX, MIX), l), _layer((1, MIX), l)],
                  out_specs=[_rows(TM, MIX), _rows(TM, MIX)],
                  out_shape=[_sds((T, MIX), BF16), _sds((T, MIX), BF16)], sem=("parallel",))(y, wglu, bglu)


def _glu_bwd(y, dout, wglu, bglu, l, name):
    T = y.shape[0]

    def body(y_ref, do_ref, w_ref, b_ref, dy_ref, dz_ref, db_ref):
        yv = y_ref[...]
        do = do_ref[...]
        g = _gelu(yv)
        s = _sigmoid(_dot(g.astype(BF16), w_ref[...]) + b_ref[...])
        dz = do * g * s * (1.0 - s)
        dzh = dz.astype(BF16)
        dz_ref[...] = dzh
        dg = do * s + _dot_nt(dzh, w_ref[...])
        dy_ref[...] = dg * _gelu_grad(yv)

        @pl.when(pl.program_id(0) == 0)
        def _():
            db_ref[...] = jnp.zeros_like(db_ref)

        db_ref[...] += _colsum(dz)

    return _pcall(body, name=name, grid=(T // TM,),
                  in_specs=[_rows(TM, MIX), _rows(TM, MIX), _layer((MIX, MIX), l), _layer((1, MIX), l)],
                  out_specs=[_rows(TM, MIX), _rows(TM, MIX), _full((1, MIX))],
                  out_shape=[_sds((T, MIX), F32), _sds((T, MIX), BF16), _sds((1, MIX), F32)])(y, dout, wglu, bglu)


def _prev_rows(T, h, s=0):
    return pl.BlockSpec((h, MIX), lambda i: (jnp.maximum(i * (TM // h) - 1, 0), s))


def _next_rows(T, h, s=0):
    return pl.BlockSpec((h, MIX), lambda i: (jnp.minimum((i + 1) * (TM // h), T // h - 1), s))


def _conv_fwd(h, w, l, name):
    T = h.shape[0]

    def body(b_ref, c_ref, x_ref, ch_ref, xh_ref, w_ref, o_ref, ext):
        i = pl.program_id(0)
        z = c_ref[...] * x_ref[...]
        ext[0:8, :] = jnp.where(i > 0, ch_ref[...] * xh_ref[...], 0.0)
        ext[8:, :] = z
        y = (w_ref[0:1, :] * ext[pl.ds(6, TM), :] + w_ref[1:2, :] * ext[pl.ds(7, TM), :] + w_ref[2:3, :] * z)
        o_ref[...] = (b_ref[...] * y).astype(BF16)

    return _pcall(body, name=name, grid=(T // TM,),
                  in_specs=[_cols(TM, 1), _cols(TM, 2), _cols(TM, 3), _prev_rows(T, 8, 2), _prev_rows(T, 8, 3),
                            _layer((3, MIX), l)],
                  out_specs=_rows(TM, MIX), out_shape=_sds((T, MIX), BF16),
                  scratch=[pltpu.VMEM((TM + 8, MIX), F32)], sem=("parallel",))(h, h, h, h, h, w)


def _conv_bwd(dout, h, dua, w, l, name):
    T = h.shape[0]
    nb = T // TM

    def body(do_ref, b_ref, c_ref, x_ref, ch_ref, xh_ref, don_ref, bn_ref, du_ref, w_ref,
             dh_ref, dw_ref, ext, ext2):
        i = pl.program_id(0)
        c = c_ref[...]
        x = x_ref[...]
        z = c * x
        ext[0:8, :] = jnp.where(i > 0, ch_ref[...] * xh_ref[...], 0.0)
        ext[8:, :] = z
        zm2 = ext[pl.ds(6, TM), :]
        zm1 = ext[pl.ds(7, TM), :]
        w0 = w_ref[0:1, :]
        w1 = w_ref[1:2, :]
        w2 = w_ref[2:3, :]
        y = w0 * zm2 + w1 * zm1 + w2 * z
        do = do_ref[...]
        dy = do * b_ref[...]
        ext2[0:TM, :] = dy
        ext2[TM:, :] = jnp.where(i < nb - 1, don_ref[...] * bn_ref[...], 0.0)
        dz = w2 * dy + w1 * ext2[pl.ds(1, TM), :] + w0 * ext2[pl.ds(2, TM), :]
        dh_ref[:, 0:MIX] = du_ref[...]
        dh_ref[:, MIX:2 * MIX] = (do * y).astype(BF16)
        dh_ref[:, 2 * MIX:3 * MIX] = (dz * x).astype(BF16)
        dh_ref[:, 3 * MIX:] = (dz * c).astype(BF16)

        @pl.when(i == 0)
        def _():
            dw_ref[...] = jnp.zeros_like(dw_ref)

        dw_ref[0:1, :] += _colsum(dy * zm2)
        dw_ref[1:2, :] += _colsum(dy * zm1)
        dw_ref[2:3, :] += _colsum(dy * z)

    return _pcall(body, name=name, grid=(nb,),
                  in_specs=[_rows(TM, MIX), _cols(TM, 1), _cols(TM, 2), _cols(TM, 3), _prev_rows(T, 8, 2),
                            _prev_rows(T, 8, 3), _next_rows(T, 8), _next_rows(T, 8, 1), _rows(TM, MIX),
                            _layer((3, MIX), l)],
                  out_specs=[_rows(TM, 4 * MIX), _full((8, MIX))],
                  out_shape=[_sds((T, 4 * MIX), BF16), _sds((8, MIX), F32)],
                  scratch=[pltpu.VMEM((TM + 8, MIX), F32), pltpu.VMEM((TM + 8, MIX), F32)])(
                      dout, h, h, h, h, h, dout, h, dua, w)


_PH = 16


def _pooled(ext, t, gi, w):
    lo = gi * POOL_GROUP
    cur = ext[pl.ds(_PH, TM), lo:lo + POOL_GROUP]
    acc = cur
    for k in range(1, w):
        acc = acc + ext[pl.ds(_PH - k, TM), lo:lo + POOL_GROUP]
    cnt = jnp.minimum(t + 1, w).astype(F32)
    return acc / cnt - cur, cnt


def _pool_fwd(h, pw, scale, l, name):
    T = h.shape[0]

    def body(z_ref, zh_ref, pw_ref, sc_ref, o_ref, ext):
        i = pl.program_id(0)
        ext[0:_PH, :] = jnp.where(i > 0, zh_ref[...], 0.0)
        ext[_PH:, :] = z_ref[...]
        t = i * TM + lax.broadcasted_iota(jnp.int32, (TM, 1), 0)
        for gi, w in enumerate(POOL_WINDOWS):
            lo = gi * POOL_GROUP
            pooled, _ = _pooled(ext, t, gi, w)
            mixed = _dot(pooled.astype(BF16), pw_ref[gi].astype(BF16))
            o_ref[:, lo:lo + POOL_GROUP] = (mixed * sc_ref[:, lo:lo + POOL_GROUP]).astype(BF16)

    return _pcall(body, name=name, grid=(T // TM,),
                  in_specs=[_cols(TM, 3), _prev_rows(T, _PH, 3), _layer((4, POOL_GROUP, POOL_GROUP), l),
                            _layer((1, MIX), l)],
                  out_specs=_rows(TM, MIX), out_shape=_sds((T, MIX), BF16),
                  scratch=[pltpu.VMEM((TM + _PH, MIX), F32)], sem=("parallel",))(h, h, pw, scale)


def _pool_bwd(dout, h, dq, dk, dv, pw, scale, l, name):
    T = h.shape[0]
    nb = T // TM

    def body(do_ref, don_ref, z_ref, zh_ref, dq_ref, dk_ref, dv_ref, pw_ref, sc_ref,
             dh_ref, dpw_ref, dsc_ref, ext, ext2):
        i = pl.program_id(0)
        ext[0:_PH, :] = jnp.where(i > 0, zh_ref[...], 0.0)
        ext[_PH:, :] = z_ref[...]
        t = i * TM + lax.broadcasted_iota(jnp.int32, (TM, 1), 0)
        dh_ref[:, 0:MIX] = dq_ref[...]
        dh_ref[:, MIX:2 * MIX] = dk_ref[...]
        dh_ref[:, 2 * MIX:3 * MIX] = dv_ref[...]

        @pl.when(i == 0)
        def _():
            dpw_ref[...] = jnp.zeros_like(dpw_ref)
            dsc_ref[...] = jnp.zeros_like(dsc_ref)

        for gi, w in enumerate(POOL_WINDOWS):
            lo = gi * POOL_GROUP
            pwb = pw_ref[gi].astype(BF16)
            sc = sc_ref[:, lo:lo + POOL_GROUP]
            pooled, cnt = _pooled(ext, t, gi, w)
            pb = pooled.astype(BF16)
            mixed = _dot(pb, pwb)
            dog = do_ref[:, lo:lo + POOL_GROUP]
            dsc_ref[:, lo:lo + POOL_GROUP] += _colsum(dog * mixed)
            dmix = (dog * sc).astype(BF16)
            dpw_ref[gi] += _dot_tn(pb, dmix)
            dpool = _dot_nt(dmix, pwb)
            dmix_n = (jnp.where(i < nb - 1, don_ref[:, lo:lo + POOL_GROUP], 0.0) * sc).astype(BF16)
            dpool_n = _dot_nt(dmix_n, pwb)
            e = dpool / cnt
            ext2[0:TM, lo:lo + POOL_GROUP] = e
            ext2[TM:, lo:lo + POOL_GROUP] = dpool_n * (1.0 / w)
            s = e
            for k in range(1, w):
                s = s + ext2[pl.ds(k, TM), lo:lo + POOL_GROUP]
            dh_ref[:, 3 * MIX + lo:3 * MIX + lo + POOL_GROUP] = (s - dpool).astype(BF16)

    blk = _rows(TM, MIX)
    return _pcall(body, name=name, grid=(nb,),
                  in_specs=[blk, _next_rows(T, _PH), _cols(TM, 3), _prev_rows(T, _PH, 3), blk, blk, blk,
                            _layer((4, POOL_GROUP, POOL_GROUP), l), _layer((1, MIX), l)],
                  out_specs=[_rows(TM, 4 * MIX), _full((4, POOL_GROUP, POOL_GROUP)), _full((1, MIX))],
                  out_shape=[_sds((T, 4 * MIX), BF16), _sds((4, POOL_GROUP, POOL_GROUP), F32), _sds((1, MIX), F32)],
                  scratch=[pltpu.VMEM((TM + _PH, MIX), F32), pltpu.VMEM((TM + _PH, MIX), F32)])(
                      dout, dout, h, h, dq, dk, dv, pw, scale)


def _att_bias_table(rel_bias):
    span = LEFT_CHUNKS * CHUNK
    assert ATT_TQ - 1 <= MAX_REL
    lo = MAX_REL - (ATT_TQ - 1)
    near = rel_bias[:, lo:2 * MAX_REL + 1]
    far = jnp.broadcast_to(rel_bias[:, 2 * MAX_REL:], (HEADS, span + ATT_TQ - 1 - MAX_REL))
    by_dist = jnp.concatenate([near, far], axis=1)
    rev = by_dist[:, ::-1]
    rows = [rev[:, ATT_TQ - 1 - r:ATT_TQ - 1 - r + ATT_W] for r in range(ATT_TQ)]
    bias = jnp.stack(rows, axis=1)
    r = jnp.arange(ATT_TQ)[:, None]
    col = jnp.arange(ATT_W)[None, :]
    dchunk = (LEFT_CHUNKS + r // CHUNK) - col // CHUNK
    visible = (dchunk >= 0) & (dchunk <= LEFT_CHUNKS)
    return jnp.where(visible[None], bias, NEG_INF)


def _qrows(n, s=0):
    return pl.BlockSpec((ATT_TQ, n), lambda i: (i, s))


def _att_views_back(d, s):
    return pl.BlockSpec((ATT_TQ, MIX), lambda i: (jnp.maximum(i - (ATT_NV - 1) + d, 0), s))


def _att_scores(q_ref, k_refs, tb_ref, h, kvalid):
    sl = slice(h * HEAD_DIM, (h + 1) * HEAD_DIM)
    qh = (q_ref[:, sl] * (HEAD_DIM ** -0.5)).astype(BF16)
    kc = jnp.concatenate([r[:, sl] for r in k_refs], axis=0).astype(BF16)
    s = _dot_nt(qh, kc) + tb_ref[h]
    return jnp.where(kvalid, s, NEG_INF), kc


def _att_fwd(h, table, name):
    T = h.shape[0]

    def body(q_ref, *refs):
        k_refs = refs[:ATT_NV]
        v_refs = refs[ATT_NV:2 * ATT_NV]
        tb_ref = refs[2 * ATT_NV]
        o_ref, oh_ref, lse_ref = refs[2 * ATT_NV + 1:]
        i = pl.program_id(0)
        col = lax.broadcasted_iota(jnp.int32, (1, ATT_W), 1)
        kvalid = (col + (i - (ATT_NV - 1)) * ATT_TQ) >= 0
        for hd in range(HEADS):
            sl = slice(hd * HEAD_DIM, (hd + 1) * HEAD_DIM)
            s, _ = _att_scores(q_ref, k_refs, tb_ref, hd, kvalid)
            vc = jnp.concatenate([r[:, sl] for r in v_refs], axis=0).astype(BF16)
            m = jnp.max(s, axis=-1, keepdims=True)
            p = jnp.exp(s - m)
            l = jnp.sum(p, axis=-1, keepdims=True)
            o = _dot(p.astype(BF16), vc) / l
            o_ref[:, sl] = o
            oh_ref[:, sl] = o.astype(BF16)
            lse_ref[:, hd:hd + 1] = m + jnp.log(l)

    kviews = [_att_views_back(d, 1) for d in range(ATT_NV)]
    vviews = [_att_views_back(d, 2) for d in range(ATT_NV)]
    return _pcall(body, name=name, grid=(T // ATT_TQ,),
                  in_specs=[_qrows(MIX)] + kviews + vviews + [_full((HEADS, ATT_TQ, ATT_W))],
                  out_specs=[_qrows(MIX), _qrows(MIX), _qrows(HEADS)],
                  out_shape=[_sds((T, MIX), F32), _sds((T, MIX), BF16), _sds((T, HEADS), F32)],
                  sem=("parallel",))(h, *([h] * (2 * ATT_NV)), table)


def _att_bwd_q(h, do, o, lse, table, name):
    T = h.shape[0]

    def body(q_ref, *refs):
        k_refs = refs[:ATT_NV]
        v_refs = refs[ATT_NV:2 * ATT_NV]
        do_ref, o_ref, lse_ref, tb_ref, dq_ref, dl_ref, dtb_ref = refs[2 * ATT_NV:]
        i = pl.program_id(0)
        col = lax.broadcasted_iota(jnp.int32, (1, ATT_W), 1)
        kvalid = (col + (i - (ATT_NV - 1)) * ATT_TQ) >= 0

        @pl.when(i == 0)
        def _():
            dtb_ref[...] = jnp.zeros_like(dtb_ref)

        for hd in range(HEADS):
            sl = slice(hd * HEAD_DIM, (hd + 1) * HEAD_DIM)
            s, kc = _att_scores(q_ref, k_refs, tb_ref, hd, kvalid)
            vc = jnp.concatenate([r[:, sl] for r in v_refs], axis=0).astype(BF16)
            p = jnp.exp(s - lse_ref[:, hd:hd + 1])
            doh = do_ref[:, sl]
            delta = jnp.sum(doh * o_ref[:, sl], axis=-1, keepdims=True)
            dp = _dot_nt(doh.astype(BF16), vc)
            ds = p * (dp - delta)
            dtb_ref[hd] += ds
            dq_ref[:, sl] = (_dot(ds.astype(BF16), kc) * (HEAD_DIM ** -0.5)).astype(BF16)
            dl_ref[:, hd:hd + 1] = delta

    kviews = [_att_views_back(d, 1) for d in range(ATT_NV)]
    vviews = [_att_views_back(d, 2) for d in range(ATT_NV)]
    tb = _full((HEADS, ATT_TQ, ATT_W))
    return _pcall(body, name=name, grid=(T // ATT_TQ,),
                  in_specs=[_qrows(MIX)] + kviews + vviews + [_qrows(MIX), _qrows(MIX), _qrows(HEADS), tb],
                  out_specs=[_qrows(MIX), _qrows(HEADS), tb],
                  out_shape=[_sds((T, MIX), BF16), _sds((T, HEADS), F32), _sds((HEADS, ATT_TQ, ATT_W), F32)])(
                      h, *([h] * (2 * ATT_NV)), do, o, lse, table)


def _att_bwd_kv(h, do, lse, delta, table, name):
    T = h.shape[0]
    nb = T // ATT_TQ

    def fwd_view(d, n, s=0):
        return pl.BlockSpec((ATT_TQ, n), lambda j: (jnp.minimum(j + d, nb - 1), s))

    def body(k_ref, v_ref, *refs):
        q_refs = refs[:ATT_NV]
        do_refs = refs[ATT_NV:2 * ATT_NV]
        lse_refs = refs[2 * ATT_NV:3 * ATT_NV]
        dl_refs = refs[3 * ATT_NV:4 * ATT_NV]
        tb_ref, dk_ref, dv_ref = refs[4 * ATT_NV:]
        j = pl.program_id(0)
        for hd in range(HEADS):
            sl = slice(hd * HEAD_DIM, (hd + 1) * HEAD_DIM)
            kh = k_ref[:, sl].astype(BF16)
            vh = v_ref[:, sl].astype(BF16)
            dk = jnp.zeros((ATT_TQ, HEAD_DIM), F32)
            dv = jnp.zeros((ATT_TQ, HEAD_DIM), F32)
            for d in range(ATT_NV):
                c0 = (ATT_NV - 1 - d) * ATT_TQ
                qh = (q_refs[d][:, sl] * (HEAD_DIM ** -0.5)).astype(BF16)
                doh = do_refs[d][:, sl].astype(BF16)
                s = _dot_nt(qh, kh) + tb_ref[hd, :, c0:c0 + ATT_TQ]
                p = jnp.exp(s - lse_refs[d][:, hd:hd + 1])
                p = jnp.where(j + d <= nb - 1, p, 0.0)
                dv = dv + _dot_tn(p.astype(BF16), doh)
                dp = _dot_nt(doh, vh)
                ds = p * (dp - dl_refs[d][:, hd:hd + 1])
                dk = dk + _dot_tn(ds.astype(BF16), qh)
            dk_ref[:, sl] = dk.astype(BF16)
            dv_ref[:, sl] = dv.astype(BF16)

    qv = [fwd_view(d, MIX, 0) for d in range(ATT_NV)]
    dov = [fwd_view(d, MIX) for d in range(ATT_NV)]
    narrow = [fwd_view(d, HEADS) for d in range(ATT_NV)]
    return _pcall(body, name=name, grid=(nb,),
                  in_specs=[_qrows(MIX, 1), _qrows(MIX, 2)] + qv + dov + narrow + narrow
                  + [_full((HEADS, ATT_TQ, ATT_W))],
                  out_specs=[_qrows(MIX), _qrows(MIX)], out_shape=[_sds((T, MIX), BF16), _sds((T, MIX), BF16)],
                  sem=("parallel",))(h, h, *([h] * ATT_NV), *([do] * ATT_NV), *([lse] * ATT_NV),
                                     *([delta] * ATT_NV), table)


_MESH = pl.DeviceIdType.MESH
_ANY = pl.BlockSpec(memory_space=pl.ANY)


def _all_gather(x, name):
    R, C = x.shape

    def body(x_ref, out_ref, send_sems, recv_sems, local_sem):
        xi, yi, ci = lax.axis_index("x"), lax.axis_index("y"), lax.axis_index("c")
        me, sibling = (xi, yi, ci), (xi, yi, 1 - ci)
        chips = [(1 - xi, yi), (xi, 1 - yi), (1 - xi, 1 - yi)]

        def slot(px, py, pc):
            return out_ref.at[4 * px + 2 * py + pc]

        def copy(k, block, to, src=None):
            return pltpu.make_async_remote_copy(
                src_ref=slot(*block) if src is None else src, dst_ref=slot(*block),
                send_sem=send_sems.at[k], recv_sem=recv_sems.at[k], device_id=to, device_id_type=_MESH)

        mine = pltpu.make_async_copy(x_ref, slot(*me), local_sem)
        mine.start()
        first = [copy(0, me, sibling, src=x_ref)]
        first += [copy(1 + j, me, (*chip, ci), src=x_ref) for j, chip in enumerate(chips)]
        for cp in first:
            cp.start()
        passed = [copy(4 + j, (*chip, ci), sibling) for j, chip in enumerate(chips)]
        for j, chip in enumerate(chips):
            copy(1 + j, (*chip, ci), me).wait_recv()
            passed[j].start()
        copy(0, sibling, me).wait_recv()
        for j, chip in enumerate(chips):
            copy(4 + j, (*chip, 1 - ci), me).wait_recv()
        for cp in first + passed:
            cp.wait_send()
        mine.wait()

    return pl.pallas_call(
        body, name=name, out_shape=_sds((N_DEV, R, C), x.dtype), in_specs=[_ANY], out_specs=_ANY,
        scratch_shapes=[pltpu.SemaphoreType.DMA((7,)), pltpu.SemaphoreType.DMA((7,)), pltpu.SemaphoreType.DMA(())],
    )(x)


def _all_to_all(s, name):
    _, R, C = s.shape

    def body(s_ref, r_ref, send_sems, recv_sems, local_sem):
        xi, yi, ci = lax.axis_index("x"), lax.axis_index("y"), lax.axis_index("c")
        me = 4 * xi + 2 * yi + ci
        mine = pltpu.make_async_copy(s_ref.at[me], r_ref.at[me], local_sem)
        mine.start()
        copies = []
        for m in range(1, N_DEV):
            px = 1 - xi if m & 4 else xi
            py = 1 - yi if m & 2 else yi
            pc = 1 - ci if m & 1 else ci
            peer = 4 * px + 2 * py + pc
            copies.append((
                pltpu.make_async_remote_copy(src_ref=s_ref.at[peer], dst_ref=r_ref.at[me], send_sem=send_sems.at[m - 1],
                                             recv_sem=recv_sems.at[m - 1], device_id=(px, py, pc), device_id_type=_MESH),
                pltpu.make_async_remote_copy(src_ref=s_ref.at[me], dst_ref=r_ref.at[peer], send_sem=send_sems.at[m - 1],
                                             recv_sem=recv_sems.at[m - 1], device_id=(px, py, pc), device_id_type=_MESH)))
        for send, _ in copies:
            send.start()
        for _, recv in copies:
            recv.wait_recv()
        for send, _ in copies:
            send.wait_send()
        mine.wait()

    return pl.pallas_call(
        body, name=name, out_shape=_sds(s.shape, s.dtype), in_specs=[_ANY], out_specs=_ANY,
        scratch_shapes=[pltpu.SemaphoreType.DMA((7,)), pltpu.SemaphoreType.DMA((7,)), pltpu.SemaphoreType.DMA(())],
    )(s)


_ADAMW_BLOCK_BYTES = 6 * 1024 * 1024


def _adamw(parts, row_off, w, m, v, name):
    R, C = w.shape
    tr = None
    for cand in (512, 256, 128, 64, 32, 16):
        step_bytes = cand * C * (N_DEV * parts.dtype.itemsize + 7 * 4)
        if R % cand == 0 and row_off % cand == 0 and step_bytes <= _ADAMW_BLOCK_BYTES:
            tr = cand
            break
    assert tr is not None, (R, C, row_off)
    off = row_off // tr
    c1 = 1.0 - ADAM_B1 ** ADAM_STEP
    c2 = 1.0 - ADAM_B2 ** ADAM_STEP

    def body(p_ref, w_ref, m_ref, v_ref, g_ref, d_ref, mo_ref, vo_ref):
        g = p_ref[0].astype(F32)
        for j in range(1, N_DEV):
            g = g + p_ref[j].astype(F32)
        mn = ADAM_B1 * m_ref[...] + (1.0 - ADAM_B1) * g
        vn = ADAM_B2 * v_ref[...] + (1.0 - ADAM_B2) * (g * g)
        m_hat = mn / c1
        v_hat = vn / c2
        g_ref[...] = g
        d_ref[...] = -ADAM_LR * (m_hat / (jnp.sqrt(v_hat) + ADAM_EPS) + ADAM_WD * w_ref[...])
        mo_ref[...] = mn
        vo_ref[...] = vn

    blk = _rows(tr, C)
    return _pcall(body, name=name, grid=(R // tr,),
                  in_specs=[pl.BlockSpec((N_DEV, tr, C), lambda i: (0, off + i, 0)), blk, blk, blk],
                  out_specs=[blk] * 4, out_shape=[_sds((R, C), F32)] * 4,
                  sem=("parallel",))(parts, w, m, v)


def _piece_rows(size):
    return -(-size // PIECE) * 16


def _pack_rows(arrays, lead=()):
    parts = []
    total = 0
    for a in arrays:
        flat = a.reshape(lead + (-1,))
        size = flat.shape[-1]
        rows = _piece_rows(size)
        pad = [(0, 0)] * len(lead) + [(0, rows * FLAT_COLS - size)]
        parts.append(jnp.pad(flat, pad).reshape(lead + (rows, FLAT_COLS)))
        total += rows
    tail = -total % FLAT_ROWS
    if tail:
        parts.append(jnp.zeros(lead + (tail, FLAT_COLS), parts[0].dtype))
    return jnp.concatenate(parts, axis=len(lead))


def _unpack_rows(buf, shapes, lead=()):
    out = []
    row = 0
    nl = len(lead)
    for shape in shapes:
        size = math.prod(shape)
        rows = _piece_rows(size)
        piece = lax.slice_in_dim(buf, row, row + rows, axis=nl).reshape(lead + (rows * FLAT_COLS,))
        out.append(lax.slice_in_dim(piece, 0, size, axis=nl).reshape(lead + tuple(shape)))
        row += rows
    return out


def _to_blocks(full, axis):
    shp = full.shape
    split = full.reshape(shp[:axis] + (N_DEV, shp[axis] // N_DEV) + shp[axis + 1:])
    return jnp.moveaxis(split, axis, 0)


def _from_blocks(blocks, axis):
    shp = blocks.shape[1:]
    moved = jnp.moveaxis(blocks, 0, axis)
    return moved.reshape(shp[:axis] + (N_DEV * shp[axis],) + shp[axis + 1:])


def _local_step(x, p, tgt, W):
    wup = W['ffn_w_up'].reshape(2, FF_J, DEPTH, D_MODEL, FF_NB)
    wd4 = W['ffn_w_down'].reshape(DEPTH, FF_J, FF_NB, D_MODEL)
    ln = {n: _vecs(W[n]) for n in ('ln_mix_g', 'ln_mix_b', 'ln_ffn_g', 'ln_ffn_b', 'ple_b_gate')}
    bglu = _vecs(W['ev_b_glu'])
    pscale = _vecs(W['od_pool_scale'])
    saved = []
    x0 = x
    x0h = x.astype(BF16)
    for i in range(DEPTH):
        L = f"L{i}_"
        j = i // 2
        s = dict(x0=x0, x0h=x0h)
        if i % 2 == 0:
            h = _mm_in(x0h, W['ev_w_in'], j, L + "mm_in")
            params = tuple(W[n][j] for n in ('ev_lambda_re', 'ev_lambda_im', 'ev_log_dt', 'ev_b_re', 'ev_b_im',
                                             'ev_c_re', 'ev_c_im', 'ev_d'))
            (mt, qt, pm, a1, a2), op_vjp = jax.vjp(_s5_operators, *params)
            umat = _to_chunks(h[:, :MIX])
            st = jnp.transpose(_s5_chunk_state(umat, pm, L + "s5_state"), (1, 0, 2))
            xp_t = _s5_scan_fwd(st, a1, a2, L + "s5_scan")
            xprev = jnp.transpose(xp_t, (1, 0, 2))
            y = _from_chunks(_s5_output(umat, xprev, mt, qt, L + "s5_out"))
            ya, gh = _glu_fwd(y, W['ev_w_glu'], bglu, j, L + "glu")
            yb = _conv_fwd(h, W['ev_conv_w'], j, L + "conv")
            s.update(op_vjp=op_vjp, mt=mt, qt=qt, pm=pm, a1=a1, a2=a2, umat=umat, xp_t=xp_t, xprev=xprev, y=y, gh=gh)
            wo, win = W['ev_w_out'], W['ev_w_in']
        else:
            h = _mm_in(x0h, W['od_w_in'], j, L + "mm_in")
            table, tb_vjp = jax.vjp(_att_bias_table, W['od_rel_bias'][j])
            of, ya, lse = _att_fwd(h, table, L + "att")
            yb = _pool_fwd(h, W['od_pool_w'], pscale, j, L + "pool")
            s.update(table=table, tb_vjp=tb_vjp, of=of, lse=lse)
            wo, win = W['od_w_out'], W['od_w_in']
        g1, b1 = (ln['ln_mix_g'], i), (ln['ln_mix_b'], i)
        g2, b2 = (ln['ln_ffn_g'], i), (ln['ln_ffn_b'], i)
        xhat1, rstd1, x1h = _mm_out_ln(ya, yb, wo, j, x0, g1, b1, L + "mm_out_ln")
        hf, a3 = _mm_up(x1h, wup, i, L + "mm_up")
        xhat2, rstd2, x2h = _mm_down_ln(a3, wd4, i, xhat1, g1, b1, g2, b2, L + "mm_down_ln")
        x3, x3h = _mm_ple(x2h, xhat2, ln['ln_ffn_g'], ln['ln_ffn_b'], p, W['ple_w_gate'], ln['ple_b_gate'],
                          W['ple_w_proj'], i, L + "mm_ple")
        s.update(h=h, win=win, wo=wo, ya=ya, yb=yb, xhat1=xhat1, rstd1=rstd1, x1h=x1h, hf=hf, a3=a3, xhat2=xhat2,
                 rstd2=rstd2, x2h=x2h, g1=g1)
        saved.append(s)
        x0, x0h = x3, x3h

    dx, loss = _loss_head(x0, tgt, "loss_head")

    G = {n: [None] * W[n].shape[0 if n not in ('ev_w_in', 'od_w_in', 'ffn_w_up') else 1] for n in WEIGHT_NAMES}
    for i in reversed(range(DEPTH)):
        L = f"L{i}_"
        j = i // 2
        s = saved[i]
        dr2, dr2h, dpreh, dpph, dbg, dg2, db2 = _ple_bwd(
            dx, s['x2h'], p, W['ple_w_gate'], ln['ple_b_gate'], W['ple_w_proj'], i, s['xhat2'], s['rstd2'],
            ln['ln_ffn_g'], L + "ple_bwd")
        G['ple_w_gate'][i] = _mm_tn(s['x2h'], dpreh, L + "dw_gate")
        G['ple_w_proj'][i] = _mm_tn(p, dpph, L + "dw_proj", a_layer=i)
        G['ple_b_gate'][i] = dbg[0]
        G['ln_ffn_g'][i] = dg2[0]
        G['ln_ffn_b'][i] = db2[0]
        dhf = _ffn_bwd1(dr2h, wd4, i, s['hf'], L + "ffn_bwd1")
        G['ffn_w_down'][i] = _mm_tn_ablk(s['a3'], dr2h, L + "dw_down").reshape(D_FF, D_MODEL)
        T = dhf.shape[2]
        G['ffn_w_up'][i] = _mm_tn_bblk(s['x1h'], dhf.reshape(N_DEV, T, FF_NB), L + "dw_up")
        dr1, dr1h, dg1, db1 = _ffn_bwd2(dhf, wup, i, dr2, s['xhat1'], s['rstd1'], s['g1'], L + "ffn_bwd2")
        G['ln_mix_g'][i] = dg1[0]
        G['ln_mix_b'][i] = db1[0]
        dya, dyb = _out_bwd(dr1h, s['wo'], j, L + "out_bwd")
        dwo = jnp.concatenate([_mm_tn(s['ya'], dr1h, L + "dw_out_a"), _mm_tn(s['yb'], dr1h, L + "dw_out_b")], axis=0)
        if i % 2 == 0:
            G['ev_w_out'][j] = dwo
            dy, dzh, dbglu = _glu_bwd(s['y'], dya, W['ev_w_glu'], bglu, j, L + "glu_bwd")
            G['ev_w_glu'][j] = _mm_tn(s['gh'], dzh, L + "dw_glu")
            G['ev_b_glu'][j] = dbglu[0]
            dymat = _to_chunks(dy)
            dxp_t = jnp.transpose(_s5_bwd_state(dymat, s['qt'], L + "s5_bwd_state"), (1, 0, 2))
            ds_t, da1, da2 = _s5_scan_bwd(dxp_t, s['xp_t'], s['a1'], s['a2'], L + "s5_scan_bwd")
            dumat, dmt, dqt, dpm = _s5_bwd_main(s['umat'], s['xprev'], dymat, jnp.transpose(ds_t, (1, 0, 2)),
                                                s['mt'], s['pm'], L + "s5_bwd")
            dparams = s['op_vjp']((dmt, dqt, dpm, da1, da2))
            for n, dpar in zip(('ev_lambda_re', 'ev_lambda_im', 'ev_log_dt', 'ev_b_re', 'ev_b_im', 'ev_c_re',
                                'ev_c_im', 'ev_d'), dparams):
                G[n][j] = dpar
            dua = _from_chunks(dumat).astype(BF16)
            dh, dcw = _conv_bwd(dyb, s['h'], dua, W['ev_conv_w'], j, L + "conv_bwd")
            G['ev_conv_w'][j] = dcw[:3]
            wname = 'ev_w_in'
        else:
            G['od_w_out'][j] = dwo
            dq, delta, dtable = _att_bwd_q(s['h'], dya, s['of'], s['lse'], s['table'], L + "att_bwd_q")
            dk, dv = _att_bwd_kv(s['h'], dya, s['lse'], delta, s['table'], L + "att_bwd_kv")
            G['od_rel_bias'][j] = s['tb_vjp'](dtable)[0]
            dh, dpw, dsc = _pool_bwd(dyb, s['h'], dq, dk, dv, W['od_pool_w'], pscale, j, L + "pool_bwd")
            G['od_pool_w'][j] = dpw
            G['od_pool_scale'][j] = dsc[0]
            wname = 'od_w_in'
        G[wname][j] = _mm_tn(s['x0h'], dh, L + "dw_in", blocked_n=IN_NB)
        dx = _in_bwd(dh, s['win'], j, dr1, L + "in_bwd")

    return loss, dx, G


def _slab(a):
    return a.reshape(-1, a.shape[-1])


def _gather_weights(W):
    full = {n: W[n] for n in REPLICATED}
    slabs = [_slab(W[n].astype(BF16)) for n in ROW_SHARDED]
    got = _all_gather(jnp.concatenate(slabs, axis=0), "gather_w_rows")
    row = 0
    for n, sl in zip(ROW_SHARDED, slabs):
        L, k8, d = W[n].shape
        piece = lax.slice_in_dim(got, row, row + sl.shape[0], axis=1).reshape(N_DEV, L, k8, d)
        full[n] = jnp.swapaxes(piece, 0, 1).reshape(L, N_DEV * k8, d)
        row += sl.shape[0]
    got = _all_gather(_slab(W['ffn_w_up'].astype(BF16)), "gather_w_up")
    full['ffn_w_up'] = got.reshape((N_DEV,) + W['ffn_w_up'].shape)
    slabs = [_slab(W[n].astype(BF16)) for n in COL_IN]
    got = _all_gather(jnp.concatenate(slabs, axis=0), "gather_w_in")
    row = 0
    for n, sl in zip(COL_IN, slabs):
        full[n] = lax.slice_in_dim(got, row, row + sl.shape[0], axis=1).reshape((N_DEV,) + W[n].shape)
        row += sl.shape[0]
    got = _all_gather(_pack_rows([W[n].astype(BF16) for n in SMALL_SHARDED]), "gather_w_small")
    shapes = [W[n].shape for n in SMALL_SHARDED]
    for n, blocks in zip(SMALL_SHARDED, _unpack_rows(got, shapes, lead=(N_DEV,))):
        full[n] = _from_blocks(blocks, SHARD_AXIS[n])
    for n in ('ev_conv_w', 'od_pool_scale'):
        full[n] = full[n].astype(F32)
    return full


def _step(x, p, tgt, W, M, V):
    full = _gather_weights(W)
    loss, dx, G = _local_step(x[0], p[:, 0], tgt[0], full)
    res = {}

    def update(parts, row, n, tag):
        shape = W[n].shape
        outs = _adamw(parts, row, _slab(W[n]), _slab(M[n]), _slab(V[n]), "adamw_" + tag)
        for kind, a in zip(('grad', 'delta', 'm', 'v'), outs):
            res[kind, n] = a.reshape(shape)
        return row + math.prod(shape[:-1])

    send = []
    for n in ROW_SHARDED:
        g = jnp.stack(G[n], axis=0)
        L, K, d = g.shape
        send.append(jnp.swapaxes(g.reshape(L, N_DEV, K // N_DEV, d), 0, 1).reshape(N_DEV, L * K // N_DEV, d))
    parts = _all_to_all(jnp.concatenate(send, axis=1), "scatter_g_rows")
    row = 0
    for n in ROW_SHARDED:
        row = update(parts, row, n, n)
    g = jnp.stack(G['ffn_w_up'], axis=1)
    parts = _all_to_all(g.reshape(N_DEV, -1, FF_NB), "scatter_g_up")
    update(parts, 0, 'ffn_w_up', 'ffn_w_up')
    send = [jnp.stack(G[n], axis=1).reshape(N_DEV, -1, IN_NB) for n in COL_IN]
    parts = _all_to_all(jnp.concatenate(send, axis=1), "scatter_g_in")
    row = 0
    for n in COL_IN:
        row = update(parts, row, n, n)

    shapes = [W[n].shape for n in SMALL_SHARDED]
    send = _pack_rows([_to_blocks(jnp.stack(G[n], axis=0), SHARD_AXIS[n]).astype(BF16) for n in SMALL_SHARDED],
                      lead=(N_DEV,))
    parts = _all_to_all(send, "scatter_g_small")
    outs = _adamw(parts, 0, _pack_rows([W[n] for n in SMALL_SHARDED]), _pack_rows([M[n] for n in SMALL_SHARDED]),
                  _pack_rows([V[n] for n in SMALL_SHARDED]), "adamw_small")
    for kind, buf in zip(('grad', 'delta', 'm', 'v'), outs):
        for n, a in zip(SMALL_SHARDED, _unpack_rows(buf, shapes)):
            res[kind, n] = a

    repl_shapes = [W[n].shape for n in REPLICATED]
    small = _pack_rows([jnp.stack(G[n], axis=0) for n in REPLICATED] + [loss])
    parts = _all_gather(small, "gather_g_replicated")
    zero = jnp.zeros((1, 1), F32)
    outs = _adamw(parts, 0, _pack_rows([W[n] for n in REPLICATED] + [zero]),
                  _pack_rows([M[n] for n in REPLICATED] + [zero]),
                  _pack_rows([V[n] for n in REPLICATED] + [zero]), "adamw_replicated")
    for kind, buf in zip(('grad', 'delta', 'm', 'v'), outs):
        arrays = _unpack_rows(buf, repl_shapes + [(1, 1)])
        for n, a in zip(REPLICATED, arrays):
            res[kind, n] = a
        if kind == 'grad':
            total_loss = arrays[-1].reshape(())

    out = [total_loss, dx[None]]
    for kind in ('grad', 'delta', 'm', 'v'):
        out += [res[kind, n] for n in WEIGHT_NAMES]
    return tuple(out)


def kernel(x, p, ev_w_in, ev_lambda_re, ev_lambda_im, ev_log_dt, ev_b_re, ev_b_im, ev_c_re, ev_c_im, ev_d, ev_w_glu, ev_b_glu, ev_conv_w, ev_w_out, od_w_in, od_rel_bias, od_pool_w, od_pool_scale, od_w_out, ln_mix_g, ln_mix_b, ln_ffn_g, ln_ffn_b, ffn_w_up, ffn_w_down, ple_w_proj, ple_w_gate, ple_b_gate, loss_target, m_ev_w_in, m_ev_lambda_re, m_ev_lambda_im, m_ev_log_dt, m_ev_b_re, m_ev_b_im, m_ev_c_re, m_ev_c_im, m_ev_d, m_ev_w_glu, m_ev_b_glu, m_ev_conv_w, m_ev_w_out, m_od_w_in, m_od_rel_bias, m_od_pool_w, m_od_pool_scale, m_od_w_out, m_ln_mix_g, m_ln_mix_b, m_ln_ffn_g, m_ln_ffn_b, m_ffn_w_up, m_ffn_w_down, m_ple_w_proj, m_ple_w_gate, m_ple_b_gate, v_ev_w_in, v_ev_lambda_re, v_ev_lambda_im, v_ev_log_dt, v_ev_b_re, v_ev_b_im, v_ev_c_re, v_ev_c_im, v_ev_d, v_ev_w_glu, v_ev_b_glu, v_ev_conv_w, v_ev_w_out, v_od_w_in, v_od_rel_bias, v_od_pool_w, v_od_pool_scale, v_od_w_out, v_ln_mix_g, v_ln_mix_b, v_ln_ffn_g, v_ln_ffn_b, v_ffn_w_up, v_ffn_w_down, v_ple_w_proj, v_ple_w_gate, v_ple_b_gate):
    given = dict(locals())
    W = {n: given[n] for n in WEIGHT_NAMES}
    M = {n: given["m_" + n] for n in WEIGHT_NAMES}
    V = {n: given["v_" + n] for n in WEIGHT_NAMES}
    return _step(x, p, loss_target, W, M, V)
```

```python
import math

import jax
import jax.numpy as jnp
from jax import lax
from jax.experimental import pallas as pl
from jax.experimental.pallas import tpu as pltpu

F32 = jnp.float32
BF16 = jnp.bfloat16
HI = lax.Precision.HIGHEST

D_MODEL = 1024
DEPTH = 4
CHUNK = 64
MIX = 512
S5_GROUP = 16
S5_GROUPS = 32
S5_STATE = 64
HEADS = 8
HEAD_DIM = 64
LEFT_CHUNKS = 8
MAX_REL = 128
POOL_WINDOWS = (2, 4, 8, 16)
POOL_GROUP = 128
D_FF = 2816
D_PLE = 256
ALPHA = (2 * DEPTH) ** 0.25
LN_EPS = 1e-5
NEG_INF = -1e30
ADAM_LR = 0.001
ADAM_B1 = 0.9
ADAM_B2 = 0.999
ADAM_EPS = 1e-08
ADAM_WD = 0.01
ADAM_STEP = 10
N_DEV = 8

WEIGHT_NAMES = ['ev_w_in', 'ev_lambda_re', 'ev_lambda_im', 'ev_log_dt', 'ev_b_re', 'ev_b_im', 'ev_c_re', 'ev_c_im',
                'ev_d', 'ev_w_glu', 'ev_b_glu', 'ev_conv_w', 'ev_w_out', 'od_w_in', 'od_rel_bias', 'od_pool_w',
                'od_pool_scale', 'od_w_out', 'ln_mix_g', 'ln_mix_b', 'ln_ffn_g', 'ln_ffn_b', 'ffn_w_up', 'ffn_w_down',
                'ple_w_proj', 'ple_w_gate', 'ple_b_gate']
SHARD_AXIS = {'ev_w_in': 2, 'ev_w_glu': 1, 'ev_conv_w': 2, 'ev_w_out': 1, 'od_w_in': 2, 'od_pool_scale': 1,
              'od_w_out': 1, 'ffn_w_up': 2, 'ffn_w_down': 1, 'ple_w_proj': 2, 'ple_w_gate': 1}
REPLICATED = [n for n in WEIGHT_NAMES if n not in SHARD_AXIS]
ROW_SHARDED = ['ev_w_out', 'od_w_out', 'ffn_w_down', 'ple_w_gate']
COL_IN = ['ev_w_in', 'od_w_in']
SMALL_SHARDED = ['ev_w_glu', 'ev_conv_w', 'od_pool_scale', 'ple_w_proj']

VMEM_LIMIT = 48 * 1024 * 1024
TM = 512
IN_NB = 4 * MIX // N_DEV
FF_NB = 2 * D_FF // N_DEV
FF_J = N_DEV // 2
S5_LC = 32
S5_LW = S5_LC * S5_GROUP
ATT_TQ = 128
ATT_NV = LEFT_CHUNKS * CHUNK // ATT_TQ + 1
ATT_W = ATT_NV * ATT_TQ
FLAT_COLS = 1024
FLAT_ROWS = 256
PIECE = 16 * FLAT_COLS


def _sds(shape, dt):
    return jax.ShapeDtypeStruct(shape, dt)


def _pcall(body, *, name, grid, in_specs, out_specs, out_shape, scratch=(), sem=None):
    sem = sem or ("arbitrary",) * len(grid)
    return pl.pallas_call(
        body, name=name, grid=grid, in_specs=in_specs, out_specs=out_specs, out_shape=out_shape,
        scratch_shapes=scratch,
        compiler_params=pltpu.CompilerParams(dimension_semantics=sem, vmem_limit_bytes=VMEM_LIMIT))


def _rows(tm, n):
    return pl.BlockSpec((tm, n), lambda i: (i, 0))


def _cols(tm, s):
    return pl.BlockSpec((tm, MIX), lambda i: (i, s))


def _full(shape):
    nd = len(shape)
    return pl.BlockSpec(shape, lambda *_: (0,) * nd)


def _layer(shape, l):
    nd = len(shape)
    return pl.BlockSpec((None,) + tuple(shape), lambda *_: (l,) + (0,) * nd)


def _vecs(a):
    return a.reshape(a.shape[0], 1, a.shape[1])


def _dot(a, b, precision=None):
    return jnp.dot(a, b, preferred_element_type=F32, precision=precision)


def _dot_nt(a, b, precision=None):
    return lax.dot_general(a, b, (((1,), (1,)), ((), ())), preferred_element_type=F32, precision=precision)


def _dot_tn(a, b, precision=None):
    return lax.dot_general(a, b, (((0,), (0,)), ((), ())), preferred_element_type=F32, precision=precision)


def _sigmoid(x):
    return 1.0 / (1.0 + jnp.exp(-x))


_GELU_C = math.sqrt(2.0 / math.pi)


def _gelu(x):
    return 0.5 * x * (1.0 + jnp.tanh(_GELU_C * (x + 0.044715 * x * x * x)))


def _gelu_grad(x):
    t = jnp.tanh(_GELU_C * (x + 0.044715 * x * x * x))
    return 0.5 * (1.0 + t) + 0.5 * x * (1.0 - t * t) * _GELU_C * (1.0 + 3.0 * 0.044715 * x * x)


def _ln_fwd(r, g, b):
    mu = jnp.mean(r, axis=-1, keepdims=True)
    xc = r - mu
    var = jnp.mean(xc * xc, axis=-1, keepdims=True)
    rstd = lax.rsqrt(var + LN_EPS)
    xhat = xc * rstd
    return xhat, rstd, xhat * g + b


def _ln_bwd(dx, xhat, rstd, g):
    dxh = dx * g
    m1 = jnp.mean(dxh, axis=-1, keepdims=True)
    m2 = jnp.mean(dxh * xhat, axis=-1, keepdims=True)
    return rstd * (dxh - m1 - xhat * m2)


def _colsum(x):
    return jnp.sum(x, axis=0, keepdims=True)


def _mm_in(xh, win8, l, name):
    T = xh.shape[0]

    def body(x_ref, w_ref, h_ref):
        x = x_ref[...]
        for b in range(N_DEV):
            h_ref[:, b * IN_NB:(b + 1) * IN_NB] = _dot(x, w_ref[b])

    return _pcall(body, name=name, grid=(T // TM,),
                  in_specs=[_rows(TM, D_MODEL),
                            pl.BlockSpec((N_DEV, None, D_MODEL, IN_NB), lambda i: (0, l, 0, 0))],
                  out_specs=_rows(TM, 4 * MIX), out_shape=_sds((T, 4 * MIX), F32),
                  sem=("parallel",))(xh, win8)


def _tile_of(n, cap):
    best = None
    for t in range(128, min(n, cap) + 1, 128):
        if n % t == 0:
            best = t
    assert best is not None, n
    return best


def _mm_tn(a, b, name, a_layer=None, blocked_n=None):
    T, M = a.shape[-2:]
    N = b.shape[1]
    tm = _tile_of(M, 1408)
    nbs = max(1, 512 // blocked_n) if blocked_n else 1
    tn = blocked_n * nbs if blocked_n else _tile_of(N, 1024 if tm <= 512 else 512)
    tk = min(T, 1024)
    nk = T // tk

    def body(a_ref, b_ref, o_ref, acc):
        k = pl.program_id(2)

        @pl.when(k == 0)
        def _():
            acc[...] = jnp.zeros_like(acc)

        acc[...] += _dot_tn(a_ref[...].astype(BF16), b_ref[...].astype(BF16))

        @pl.when(k == nk - 1)
        def _():
            if blocked_n:
                for sb in range(nbs):
                    o_ref[sb] = acc[:, sb * blocked_n:(sb + 1) * blocked_n].astype(BF16)
            else:
                o_ref[...] = acc[...].astype(BF16)

    if a_layer is None:
        a_spec = pl.BlockSpec((tk, tm), lambda i, j, k: (k, i))
    else:
        a_spec = pl.BlockSpec((None, tk, tm), lambda i, j, k: (a_layer, k, i))
    if blocked_n:
        o_spec = pl.BlockSpec((nbs, tm, blocked_n), lambda i, j, k: (j, i, 0))
        o_shape = _sds((N // blocked_n, M, blocked_n), BF16)
    else:
        o_spec = pl.BlockSpec((tm, tn), lambda i, j, k: (i, j))
        o_shape = _sds((M, N), BF16)
    return _pcall(body, name=name, grid=(M // tm, N // tn, nk),
                  in_specs=[a_spec, pl.BlockSpec((tk, tn), lambda i, j, k: (k, j))],
                  out_specs=o_spec, out_shape=o_shape, scratch=[pltpu.VMEM((tm, tn), F32)],
                  sem=("parallel", "parallel", "arbitrary"))(a, b)


def _mm_tn_bblk(a, b3, name, per_step=2):
    T, M = a.shape
    NB, _, n = b3.shape
    tk = min(T, 1024)
    nk = T // tk

    def body(a_ref, b_ref, o_ref, acc):
        k = pl.program_id(1)

        @pl.when(k == 0)
        def _():
            acc[...] = jnp.zeros_like(acc)

        av = a_ref[...]
        for s in range(per_step):
            acc[s] += _dot_tn(av, b_ref[s])

        @pl.when(k == nk - 1)
        def _():
            o_ref[...] = acc[...].astype(BF16)

    return _pcall(body, name=name, grid=(NB // per_step, nk),
                  in_specs=[pl.BlockSpec((tk, M), lambda j, k: (k, 0)),
                            pl.BlockSpec((per_step, tk, n), lambda j, k: (j, k, 0))],
                  out_specs=pl.BlockSpec((per_step, M, n), lambda j, k: (j, 0, 0)),
                  out_shape=_sds((NB, M, n), BF16), scratch=[pltpu.VMEM((per_step, M, n), F32)],
                  sem=("parallel", "arbitrary"))(a, b3)


def _mm_tn_ablk(a3, b, name):
    NA, T, m = a3.shape
    N = b.shape[1]
    tn = _tile_of(N, 1024)
    tk = min(T, 1024)
    nk = T // tk

    def body(a_ref, b_ref, o_ref, acc):
        k = pl.program_id(2)

        @pl.when(k == 0)
        def _():
            acc[...] = jnp.zeros_like(acc)

        acc[...] += _dot_tn(a_ref[...], b_ref[...])

        @pl.when(k == nk - 1)
        def _():
            o_ref[...] = acc[...].astype(BF16)

    return _pcall(body, name=name, grid=(NA, N // tn, nk),
                  in_specs=[pl.BlockSpec((None, tk, m), lambda j, n, k: (j, k, 0)),
                            pl.BlockSpec((tk, tn), lambda j, n, k: (k, n))],
                  out_specs=pl.BlockSpec((None, m, tn), lambda j, n, k: (j, 0, n)),
                  out_shape=_sds((NA, m, N), BF16), scratch=[pltpu.VMEM((m, tn), F32)],
                  sem=("parallel", "parallel", "arbitrary"))(a3, b)


def _mm_out_ln(ya, yb, wo, l, x0, g, b, name):
    T = x0.shape[0]

    def body(ya_ref, yb_ref, w_ref, x0_ref, g_ref, b_ref, xh_ref, rs_ref, x1_ref):
        r = ALPHA * x0_ref[...] + _dot(ya_ref[...], w_ref[0:MIX, :]) + _dot(yb_ref[...], w_ref[MIX:, :])
        xhat, rstd, x1 = _ln_fwd(r, g_ref[...], b_ref[...])
        xh_ref[...] = xhat
        rs_ref[...] = rstd
        x1_ref[...] = x1.astype(BF16)

    vec = _layer((1, D_MODEL), g[1])
    return _pcall(body, name=name, grid=(T // TM,),
                  in_specs=[_rows(TM, MIX), _rows(TM, MIX), _layer((D_MODEL, D_MODEL), l), _rows(TM, D_MODEL),
                            vec, vec],
                  out_specs=[_rows(TM, D_MODEL), _rows(TM, 1), _rows(TM, D_MODEL)],
                  out_shape=[_sds((T, D_MODEL), F32), _sds((T, 1), F32), _sds((T, D_MODEL), BF16)],
                  sem=("parallel",))(ya, yb, wo, x0, g[0], b[0])


def _mm_up(x1h, wup, l, name):
    T = x1h.shape[0]

    def body(x_ref, w_ref, hf_ref, a_ref):
        x = x_ref[...]
        g = _dot(x, w_ref[0])
        u = _dot(x, w_ref[1])
        hf_ref[0] = g.astype(BF16)
        hf_ref[1] = u.astype(BF16)
        a_ref[...] = (g * _sigmoid(g) * u).astype(BF16)

    return _pcall(body, name=name, grid=(FF_J, T // TM),
                  in_specs=[pl.BlockSpec((TM, D_MODEL), lambda j, i: (i, 0)),
                            pl.BlockSpec((2, None, None, D_MODEL, FF_NB), lambda j, i: (0, j, l, 0, 0))],
                  out_specs=[pl.BlockSpec((2, None, TM, FF_NB), lambda j, i: (0, j, i, 0)),
                             pl.BlockSpec((None, TM, FF_NB), lambda j, i: (j, i, 0))],
                  out_shape=[_sds((2, FF_J, T, FF_NB), BF16), _sds((FF_J, T, FF_NB), BF16)],
                  sem=("parallel", "parallel"))(x1h, wup)


def _mm_down_ln(a3, wd4, l, xhat1, g1, b1, g2, b2, name):
    T = a3.shape[1]

    def body(a_ref, w_ref, xh1_ref, g1_ref, b1_ref, g2_ref, b2_ref, xh_ref, rs_ref, x2_ref):
        x1 = xh1_ref[...] * g1_ref[...] + b1_ref[...]
        r = ALPHA * x1
        for j in range(FF_J):
            r = r + _dot(a_ref[j], w_ref[j])
        xhat, rstd, x2 = _ln_fwd(r, g2_ref[...], b2_ref[...])
        xh_ref[...] = xhat
        rs_ref[...] = rstd
        x2_ref[...] = x2.astype(BF16)

    vec = _layer((1, D_MODEL), g1[1])
    return _pcall(body, name=name, grid=(T // TM,),
                  in_specs=[pl.BlockSpec((FF_J, TM, FF_NB), lambda i: (0, i, 0)),
                            _layer((FF_J, FF_NB, D_MODEL), l), _rows(TM, D_MODEL), vec, vec, vec, vec],
                  out_specs=[_rows(TM, D_MODEL), _rows(TM, 1), _rows(TM, D_MODEL)],
                  out_shape=[_sds((T, D_MODEL), F32), _sds((T, 1), F32), _sds((T, D_MODEL), BF16)],
                  sem=("parallel",))(a3, wd4, xhat1, g1[0], b1[0], g2[0], b2[0])


def _mm_ple(x2h, xhat2, g2, b2, p, wg, bg, wp, l, name):
    T = x2h.shape[0]

    def body(x2h_ref, xh_ref, g2_ref, b2_ref, p_ref, wg_ref, bg_ref, wp_ref, o_ref, oh_ref):
        x2 = xh_ref[...] * g2_ref[...] + b2_ref[...]
        gate = _sigmoid(_dot(x2h_ref[...], wg_ref[...]) + bg_ref[...])
        pp = _dot(p_ref[...].astype(BF16), wp_ref[...])
        x3 = x2 + gate * pp
        o_ref[...] = x3
        oh_ref[...] = x3.astype(BF16)

    vec = _layer((1, D_MODEL), l)
    return _pcall(body, name=name, grid=(T // TM,),
                  in_specs=[_rows(TM, D_MODEL), _rows(TM, D_MODEL), vec, vec,
                            pl.BlockSpec((None, TM, D_PLE), lambda i: (l, i, 0)),
                            _layer((D_MODEL, D_MODEL), l), vec, _layer((D_PLE, D_MODEL), l)],
                  out_specs=[_rows(TM, D_MODEL), _rows(TM, D_MODEL)],
                  out_shape=[_sds((T, D_MODEL), F32), _sds((T, D_MODEL), BF16)],
                  sem=("parallel",))(x2h, xhat2, g2, b2, p, wg, bg, wp)


def _loss_head(x3, tgt, name):
    T = x3.shape[0]

    def body(x_ref, t_ref, dx_ref, l_ref):
        e = x_ref[...] - t_ref[...]
        dx_ref[...] = e * (1.0 / D_MODEL)

        @pl.when(pl.program_id(0) == 0)
        def _():
            l_ref[...] = jnp.zeros_like(l_ref)

        l_ref[...] += (0.5 / D_MODEL) * jnp.sum(e * e).reshape(1, 1)

    return _pcall(body, name=name, grid=(T // TM,),
                  in_specs=[_rows(TM, D_MODEL), _rows(TM, D_MODEL)],
                  out_specs=[_rows(TM, D_MODEL), _full((1, 1))],
                  out_shape=[_sds((T, D_MODEL), F32), _sds((1, 1), F32)])(x3, tgt)


def _ple_bwd(dx3, x2h, p, wg, bg, wp, l, xhat2, rstd2, g2, name):
    T = dx3.shape[0]

    def body(dx3_ref, x2h_ref, p_ref, wg_ref, bg_ref, wp_ref, xh_ref, rs_ref, g2_ref,
             dr_ref, drh_ref, dpre_ref, dpp_ref, dbg_ref, dg_ref, db_ref):
        dx3 = dx3_ref[...]
        gate = _sigmoid(_dot(x2h_ref[...], wg_ref[...]) + bg_ref[...])
        pp = _dot(p_ref[...].astype(BF16), wp_ref[...])
        dpre = dx3 * pp * gate * (1.0 - gate)
        dpreh = dpre.astype(BF16)
        dpre_ref[...] = dpreh
        dpp_ref[...] = (dx3 * gate).astype(BF16)
        dx2 = dx3 + _dot_nt(dpreh, wg_ref[...])
        xhat = xh_ref[...]
        dr = _ln_bwd(dx2, xhat, rs_ref[...], g2_ref[...])
        dr_ref[...] = dr
        drh_ref[...] = dr.astype(BF16)

        @pl.when(pl.program_id(0) == 0)
        def _():
            dbg_ref[...] = jnp.zeros_like(dbg_ref)
            dg_ref[...] = jnp.zeros_like(dg_ref)
            db_ref[...] = jnp.zeros_like(db_ref)

        dbg_ref[...] += _colsum(dpre)
        dg_ref[...] += _colsum(dx2 * xhat)
        db_ref[...] += _colsum(dx2)

    vec = _layer((1, D_MODEL), l)
    acc = _full((1, D_MODEL))
    big = _rows(TM, D_MODEL)
    return _pcall(body, name=name, grid=(T // TM,),
                  in_specs=[big, big, pl.BlockSpec((None, TM, D_PLE), lambda i: (l, i, 0)),
                            _layer((D_MODEL, D_MODEL), l), vec, _layer((D_PLE, D_MODEL), l),
                            big, _rows(TM, 1), vec],
                  out_specs=[big, big, big, big, acc, acc, acc],
                  out_shape=[_sds((T, D_MODEL), F32), _sds((T, D_MODEL), BF16), _sds((T, D_MODEL), BF16),
                             _sds((T, D_MODEL), BF16), _sds((1, D_MODEL), F32), _sds((1, D_MODEL), F32),
                             _sds((1, D_MODEL), F32)])(dx3, x2h, p, wg, bg, wp, xhat2, rstd2, g2)


def _ffn_bwd1(dr2h, wd4, l, hf, name):
    T = dr2h.shape[0]

    def body(d_ref, w_ref, hf_ref, o_ref):
        da = _dot_nt(d_ref[...], w_ref[...])
        g = hf_ref[0].astype(F32)
        u = hf_ref[1].astype(F32)
        sg = _sigmoid(g)
        o_ref[0] = (da * u * (sg * (1.0 + g * (1.0 - sg)))).astype(BF16)
        o_ref[1] = (da * (g * sg)).astype(BF16)

    blk = pl.BlockSpec((2, None, TM, FF_NB), lambda j, i: (0, j, i, 0))
    return _pcall(body, name=name, grid=(FF_J, T // TM),
                  in_specs=[pl.BlockSpec((TM, D_MODEL), lambda j, i: (i, 0)),
                            pl.BlockSpec((None, None, FF_NB, D_MODEL), lambda j, i: (l, j, 0, 0)), blk],
                  out_specs=blk, out_shape=_sds((2, FF_J, T, FF_NB), BF16),
                  sem=("parallel", "parallel"))(dr2h, wd4, hf)


_TM_B2 = 256


def _ffn_bwd2(dhf, wup, l, dr2, xhat1, rstd1, g1, name):
    T = dr2.shape[0]
    tm = min(T, _TM_B2)

    def body(dh_ref, w_ref, dr2_ref, xh_ref, rs_ref, g_ref, dr_ref, drh_ref, dg_ref, db_ref):
        dx1 = ALPHA * dr2_ref[...]
        for s in range(2):
            for j in range(FF_J):
                dx1 = dx1 + _dot_nt(dh_ref[s, j], w_ref[s, j])
        xhat = xh_ref[...]
        dr = _ln_bwd(dx1, xhat, rs_ref[...], g_ref[...])
        dr_ref[...] = dr
        drh_ref[...] = dr.astype(BF16)

        @pl.when(pl.program_id(0) == 0)
        def _():
            dg_ref[...] = jnp.zeros_like(dg_ref)
            db_ref[...] = jnp.zeros_like(db_ref)

        dg_ref[...] += _colsum(dx1 * xhat)
        db_ref[...] += _colsum(dx1)

    big = _rows(tm, D_MODEL)
    acc = _full((1, D_MODEL))
    return _pcall(body, name=name, grid=(T // tm,),
                  in_specs=[pl.BlockSpec((2, FF_J, tm, FF_NB), lambda i: (0, 0, i, 0)),
                            pl.BlockSpec((2, FF_J, None, D_MODEL, FF_NB), lambda i: (0, 0, l, 0, 0)),
                            big, big, _rows(tm, 1), _layer((1, D_MODEL), g1[1])],
                  out_specs=[big, big, acc, acc],
                  out_shape=[_sds((T, D_MODEL), F32), _sds((T, D_MODEL), BF16), _sds((1, D_MODEL), F32),
                             _sds((1, D_MODEL), F32)])(dhf, wup, dr2, xhat1, rstd1, g1[0])


def _out_bwd(dr1h, wo, l, name):
    T = dr1h.shape[0]

    def body(d_ref, w_ref, da_ref, db_ref):
        d = d_ref[...]
        da_ref[...] = _dot_nt(d, w_ref[0:MIX, :])
        db_ref[...] = _dot_nt(d, w_ref[MIX:, :])

    return _pcall(body, name=name, grid=(T // TM,),
                  in_specs=[_rows(TM, D_MODEL), _layer((D_MODEL, D_MODEL), l)],
                  out_specs=[_rows(TM, MIX), _rows(TM, MIX)],
                  out_shape=[_sds((T, MIX), F32), _sds((T, MIX), F32)],
                  sem=("parallel",))(dr1h, wo)


def _in_bwd(dh, win8, l, dr1, name):
    T = dr1.shape[0]

    def body(dh_ref, w_ref, dr_ref, o_ref):
        acc = ALPHA * dr_ref[...]
        for b in range(N_DEV):
            acc = acc + _dot_nt(dh_ref[:, b * IN_NB:(b + 1) * IN_NB], w_ref[b])
        o_ref[...] = acc

    return _pcall(body, name=name, grid=(T // TM,),
                  in_specs=[_rows(TM, 4 * MIX), pl.BlockSpec((N_DEV, None, D_MODEL, IN_NB), lambda i: (0, l, 0, 0)),
                            _rows(TM, D_MODEL)],
                  out_specs=_rows(TM, D_MODEL), out_shape=_sds((T, D_MODEL), F32),
                  sem=("parallel",))(dh, win8, dr1)


def _s5_operators(lre, lim, log_dt, bre, bim, cre, cim, dskip):
    G, P, H, LC = S5_GROUPS, S5_STATE, S5_GROUP, S5_LC
    dt = jnp.exp(log_dt)[:, None]
    mag = jnp.exp(lre * dt)
    ang = lim * dt
    lb_re = mag * jnp.cos(ang)
    lb_im = mag * jnp.sin(ang)
    den = lre * lre + lim * lim
    nr = lb_re - 1.0
    ni = lb_im
    r_re = (nr * lre + ni * lim) / den
    r_im = (ni * lre - nr * lim) / den
    bb_re = r_re[..., None] * bre - r_im[..., None] * bim
    bb_im = r_re[..., None] * bim + r_im[..., None] * bre
    k = jnp.arange(LC + 1, dtype=F32)[:, None, None]
    pmag = jnp.exp(k * (lre * dt)[None])
    pang = k * ang[None]
    pw_re = pmag * jnp.cos(pang)
    pw_im = pmag * jnp.sin(pang)
    cp_re = cre[None] * pw_re[:, :, None, :] - cim[None] * pw_im[:, :, None, :]
    cp_im = cre[None] * pw_im[:, :, None, :] + cim[None] * pw_re[:, :, None, :]
    kk = (jnp.einsum('kghp,gpj->kghj', cp_re[:LC], bb_re, precision=HI)
          - jnp.einsum('kghp,gpj->kghj', cp_im[:LC], bb_im, precision=HI))
    dmat = dskip.reshape(G, H)[:, :, None] * jnp.eye(H, dtype=F32)[None]
    kk = jnp.concatenate([kk[:1] + dmat[None], kk[1:]], axis=0)
    r0 = jnp.transpose(kk, (1, 3, 0, 2)).reshape(G, H, LC * H)
    qt = jnp.stack([cp_re[1:], -cp_im[1:]], axis=0)
    qt = jnp.transpose(qt, (2, 0, 4, 1, 3)).reshape(G, 2 * P, LC * H)
    pb_re = pw_re[:LC, :, :, None] * bb_re[None] - pw_im[:LC, :, :, None] * bb_im[None]
    pb_im = pw_re[:LC, :, :, None] * bb_im[None] + pw_im[:LC, :, :, None] * bb_re[None]
    pm = jnp.stack([pb_re[::-1], pb_im[::-1]], axis=0)
    pm = jnp.transpose(pm, (2, 1, 4, 0, 3)).reshape(G, LC * H, 2 * P)
    a_re = pw_re[LC]
    a_im = pw_im[LC]
    a1 = jnp.concatenate([a_re, a_re], axis=-1)
    a2 = jnp.concatenate([-a_im, a_im], axis=-1)
    return r0, qt, pm, a1, a2


def _to_chunks(u):
    T = u.shape[0]
    return jnp.transpose(u.reshape(T // S5_LC, S5_LC, S5_GROUPS, S5_GROUP), (2, 0, 1, 3)).reshape(
        S5_GROUPS, T // S5_LC, S5_LW)


def _from_chunks(m):
    nc = m.shape[1]
    return jnp.transpose(m.reshape(S5_GROUPS, nc, S5_LC, S5_GROUP), (1, 2, 0, 3)).reshape(nc * S5_LC, MIX)


def _gspec(r, c):
    return pl.BlockSpec((None, r, c), lambda g: (g, 0, 0))


def _s5_chunk_state(umat, pm, name):
    G, nc, _ = umat.shape

    def body(u_ref, p_ref, o_ref):
        o_ref[...] = _dot(u_ref[...], p_ref[...], HI)

    return _pcall(body, name=name, grid=(G,), in_specs=[_gspec(nc, S5_LW), _gspec(S5_LW, 128)],
                  out_specs=_gspec(nc, 128), out_shape=_sds((G, nc, 128), F32), sem=("parallel",))(umat, pm)


_SCAN_G = 8
_SCAN_UNROLL = 8


def _s5_scan_fwd(s_t, a1, a2, name):
    nc, G, _ = s_t.shape

    def body(s_ref, a1_ref, a2_ref, o_ref, sb_ref):
        sb_ref[...] = pltpu.roll(s_ref[...], 64, 2)
        a1v = a1_ref[...]
        a2v = a2_ref[...]

        def step(c, carry):
            x, xb = carry
            o_ref[c] = x
            return a1v * x + a2v * xb + s_ref[c], a1v * xb - a2v * x + sb_ref[c]

        zero = jnp.zeros((_SCAN_G, 128), F32)
        lax.fori_loop(0, nc, step, (zero, zero), unroll=_SCAN_UNROLL)

    blk = pl.BlockSpec((nc, _SCAN_G, 128), lambda g: (0, g, 0))
    vec = pl.BlockSpec((_SCAN_G, 128), lambda g: (g, 0))
    return _pcall(body, name=name, grid=(G // _SCAN_G,), in_specs=[blk, vec, vec], out_specs=blk,
                  out_shape=_sds((nc, G, 128), F32), scratch=[pltpu.VMEM((nc, _SCAN_G, 128), F32)],
                  sem=("parallel",))(s_t, a1, a2)


def _s5_scan_bwd(dxp_t, xp_t, a1, a2, name):
    nc, G, _ = dxp_t.shape

    def body(dx_ref, x_ref, a1_ref, a2_ref, ds_ref, da1_ref, da2_ref, dxb_ref, xb_ref):
        dxb_ref[...] = pltpu.roll(dx_ref[...], 64, 2)
        xb_ref[...] = pltpu.roll(x_ref[...], 64, 2)
        a1v = a1_ref[...]
        a2v = a2_ref[...]
        zero = jnp.zeros((_SCAN_G, 128), F32)

        def step(n, carry):
            gc, gb, d1, d2 = carry
            c = nc - 1 - n
            ds_ref[c] = gc
            d1 = d1 + gc * x_ref[c]
            d2 = d2 + gc * xb_ref[c]
            return dx_ref[c] + a1v * gc - a2v * gb, dxb_ref[c] + a1v * gb + a2v * gc, d1, d2

        _, _, d1, d2 = lax.fori_loop(0, nc, step, (zero, zero, zero, zero), unroll=_SCAN_UNROLL)
        da1_ref[...] = d1
        da2_ref[...] = d2

    blk = pl.BlockSpec((nc, _SCAN_G, 128), lambda g: (0, g, 0))
    vec = pl.BlockSpec((_SCAN_G, 128), lambda g: (g, 0))
    return _pcall(body, name=name, grid=(G // _SCAN_G,), in_specs=[blk, blk, vec, vec],
                  out_specs=[blk, vec, vec],
                  out_shape=[_sds((nc, G, 128), F32), _sds((G, 128), F32), _sds((G, 128), F32)],
                  scratch=[pltpu.VMEM((nc, _SCAN_G, 128), F32)] * 2, sem=("parallel",))(dxp_t, xp_t, a1, a2)


def _toeplitz_rows(r0, mt):
    lane = lax.broadcasted_iota(jnp.int32, (S5_GROUP, S5_LW), 1)
    mt[0:S5_GROUP, :] = r0
    for s in range(1, S5_LC):
        mt[s * S5_GROUP:(s + 1) * S5_GROUP, :] = jnp.where(lane >= s * S5_GROUP, pltpu.roll(r0, s * S5_GROUP, 1), 0.0)


def _toeplitz_rows_t(dmt):
    lane = lax.broadcasted_iota(jnp.int32, (S5_GROUP, S5_LW), 1)
    acc = dmt[0:S5_GROUP, :]
    for s in range(1, S5_LC):
        blk = dmt[s * S5_GROUP:(s + 1) * S5_GROUP, :]
        acc = acc + jnp.where(lane < S5_LW - s * S5_GROUP, pltpu.roll(blk, S5_LW - s * S5_GROUP, 1), 0.0)
    return acc


def _s5_output(umat, xprev, r0, qt, name):
    G, nc, _ = umat.shape

    def body(u_ref, x_ref, r_ref, q_ref, o_ref, mt):
        _toeplitz_rows(r_ref[...], mt)
        o_ref[...] = _dot(u_ref[...], mt[...], HI) + _dot(x_ref[...], q_ref[...], HI)

    return _pcall(body, name=name, grid=(G,),
                  in_specs=[_gspec(nc, S5_LW), _gspec(nc, 128), _gspec(S5_GROUP, S5_LW), _gspec(128, S5_LW)],
                  out_specs=_gspec(nc, S5_LW), out_shape=_sds((G, nc, S5_LW), F32),
                  scratch=[pltpu.VMEM((S5_LW, S5_LW), F32)], sem=("parallel",))(umat, xprev, r0, qt)


def _s5_bwd_state(dymat, qt, name):
    G, nc, _ = dymat.shape

    def body(d_ref, q_ref, o_ref):
        o_ref[...] = _dot_nt(d_ref[...], q_ref[...], HI)

    return _pcall(body, name=name, grid=(G,), in_specs=[_gspec(nc, S5_LW), _gspec(128, S5_LW)],
                  out_specs=_gspec(nc, 128), out_shape=_sds((G, nc, 128), F32), sem=("parallel",))(dymat, qt)


def _s5_bwd_main(umat, xprev, dymat, ds, r0, pm, name):
    G, nc, _ = umat.shape

    def body(u_ref, x_ref, dy_ref, ds_ref, r_ref, p_ref, du_ref, dr_ref, dq_ref, dp_ref, mt):
        u = u_ref[...]
        dy = dy_ref[...]
        dsv = ds_ref[...]
        _toeplitz_rows(r_ref[...], mt)
        du_ref[...] = (_dot_nt(dy, mt[...], HI) + _dot_nt(dsv, p_ref[...], HI)).astype(BF16)
        dq_ref[...] = _dot_tn(x_ref[...], dy, HI)
        dp_ref[...] = _dot_tn(u, dsv, HI)
        mt[...] = _dot_tn(u, dy, HI)
        dr_ref[...] = _toeplitz_rows_t(mt)

    return _pcall(body, name=name, grid=(G,),
                  in_specs=[_gspec(nc, S5_LW), _gspec(nc, 128), _gspec(nc, S5_LW), _gspec(nc, 128),
                            _gspec(S5_GROUP, S5_LW), _gspec(S5_LW, 128)],
                  out_specs=[_gspec(nc, S5_LW), _gspec(S5_GROUP, S5_LW), _gspec(128, S5_LW), _gspec(S5_LW, 128)],
                  out_shape=[_sds((G, nc, S5_LW), BF16), _sds((G, S5_GROUP, S5_LW), F32), _sds((G, 128, S5_LW), F32),
                             _sds((G, S5_LW, 128), F32)],
                  scratch=[pltpu.VMEM((S5_LW, S5_LW), F32)], sem=("parallel",))(umat, xprev, dymat, ds, r0, pm)


def _glu_fwd(y, wglu, bglu, l, name):
    T = y.shape[0]

    def body(y_ref, w_ref, b_ref, o_ref, g_ref):
        g = _gelu(y_ref[...])
        gh = g.astype(BF16)
        z = _dot(gh, w_ref[...]) + b_ref[...]
        o_ref[...] = (g * _sigmoid(z)).astype(BF16)
        g_ref[...] = gh

    return _pcall(body, name=name, grid=(T // TM,),
                  in_specs=[_rows(TM, MIX), _layer((MIX, MIX), l), _layer((1, MIX), l)],
                  out_specs=[_rows(TM, MIX), _rows(TM, MIX)],
                  out_shape=[_sds((T, MIX), BF16), _sds((T, MIX), BF16)], sem=("parallel",))(y, wglu, bglu)


def _glu_bwd(y, dout, wglu, bglu, l, name):
    T = y.shape[0]

    def body(y_ref, do_ref, w_ref, b_ref, dy_ref, dz_ref, db_ref):
        yv = y_ref[...]
        do = do_ref[...]
        g = _gelu(yv)
        s = _sigmoid(_dot(g.astype(BF16), w_ref[...]) + b_ref[...])
        dz = do * g * s * (1.0 - s)
        dzh = dz.astype(BF16)
        dz_ref[...] = dzh
        dg = do * s + _dot_nt(dzh, w_ref[...])
        dy_ref[...] = dg * _gelu_grad(yv)

        @pl.when(pl.program_id(0) == 0)
        def _():
            db_ref[...] = jnp.zeros_like(db_ref)

        db_ref[...] += _colsum(dz)

    return _pcall(body, name=name, grid=(T // TM,),
                  in_specs=[_rows(TM, MIX), _rows(TM, MIX), _layer((MIX, MIX), l), _layer((1, MIX), l)],
                  out_specs=[_rows(TM, MIX), _rows(TM, MIX), _full((1, MIX))],
                  out_shape=[_sds((T, MIX), F32), _sds((T, MIX), BF16), _sds((1, MIX), F32)])(y, dout, wglu, bglu)


def _prev_rows(T, h, s=0):
    return pl.BlockSpec((h, MIX), lambda i: (jnp.maximum(i * (TM // h) - 1, 0), s))


def _next_rows(T, h, s=0):
    return pl.BlockSpec((h, MIX), lambda i: (jnp.minimum((i + 1) * (TM // h), T // h - 1), s))


def _conv_fwd(h, w, l, name):
    T = h.shape[0]

    def body(b_ref, c_ref, x_ref, ch_ref, xh_ref, w_ref, o_ref, ext):
        i = pl.program_id(0)
        z = c_ref[...] * x_ref[...]
        ext[0:8, :] = jnp.where(i > 0, ch_ref[...] * xh_ref[...], 0.0)
        ext[8:, :] = z
        y = (w_ref[0:1, :] * ext[pl.ds(6, TM), :] + w_ref[1:2, :] * ext[pl.ds(7, TM), :] + w_ref[2:3, :] * z)
        o_ref[...] = (b_ref[...] * y).astype(BF16)

    return _pcall(body, name=name, grid=(T // TM,),
                  in_specs=[_cols(TM, 1), _cols(TM, 2), _cols(TM, 3), _prev_rows(T, 8, 2), _prev_rows(T, 8, 3),
                            _layer((3, MIX), l)],
                  out_specs=_rows(TM, MIX), out_shape=_sds((T, MIX), BF16),
                  scratch=[pltpu.VMEM((TM + 8, MIX), F32)], sem=("parallel",))(h, h, h, h, h, w)


def _conv_bwd(dout, h, dua, w, l, name):
    T = h.shape[0]
    nb = T // TM

    def body(do_ref, b_ref, c_ref, x_ref, ch_ref, xh_ref, don_ref, bn_ref, du_ref, w_ref,
             dh_ref, dw_ref, ext, ext2):
        i = pl.program_id(0)
        c = c_ref[...]
        x = x_ref[...]
        z = c * x
        ext[0:8, :] = jnp.where(i > 0, ch_ref[...] * xh_ref[...], 0.0)
        ext[8:, :] = z
        zm2 = ext[pl.ds(6, TM), :]
        zm1 = ext[pl.ds(7, TM), :]
        w0 = w_ref[0:1, :]
        w1 = w_ref[1:2, :]
        w2 = w_ref[2:3, :]
        y = w0 * zm2 + w1 * zm1 + w2 * z
        do = do_ref[...]
        dy = do * b_ref[...]
        ext2[0:TM, :] = dy
        ext2[TM:, :] = jnp.where(i < nb - 1, don_ref[...] * bn_ref[...], 0.0)
        dz = w2 * dy + w1 * ext2[pl.ds(1, TM), :] + w0 * ext2[pl.ds(2, TM), :]
        dh_ref[:, 0:MIX] = du_ref[...]
        dh_ref[:, MIX:2 * MIX] = (do * y).astype(BF16)
        dh_ref[:, 2 * MIX:3 * MIX] = (dz * x).astype(BF16)
        dh_ref[:, 3 * MIX:] = (dz * c).astype(BF16)

        @pl.when(i == 0)
        def _():
            dw_ref[...] = jnp.zeros_like(dw_ref)

        dw_ref[0:1, :] += _colsum(dy * zm2)
        dw_ref[1:2, :] += _colsum(dy * zm1)
        dw_ref[2:3, :] += _colsum(dy * z)

    return _pcall(body, name=name, grid=(nb,),
                  in_specs=[_rows(TM, MIX), _cols(TM, 1), _cols(TM, 2), _cols(TM, 3), _prev_rows(T, 8, 2),
                            _prev_rows(T, 8, 3), _next_rows(T, 8), _next_rows(T, 8, 1), _rows(TM, MIX),
                            _layer((3, MIX), l)],
                  out_specs=[_rows(TM, 4 * MIX), _full((8, MIX))],
                  out_shape=[_sds((T, 4 * MIX), BF16), _sds((8, MIX), F32)],
                  scratch=[pltpu.VMEM((TM + 8, MIX), F32), pltpu.VMEM((TM + 8, MIX), F32)])(
                      dout, h, h, h, h, h, dout, h, dua, w)


_PH = 16


def _pooled(ext, t, gi, w):
    lo = gi * POOL_GROUP
    cur = ext[pl.ds(_PH, TM), lo:lo + POOL_GROUP]
    acc = cur
    for k in range(1, w):
        acc = acc + ext[pl.ds(_PH - k, TM), lo:lo + POOL_GROUP]
    cnt = jnp.minimum(t + 1, w).astype(F32)
    return acc / cnt - cur, cnt


def _pool_fwd(h, pw, scale, l, name):
    T = h.shape[0]

    def body(z_ref, zh_ref, pw_ref, sc_ref, o_ref, ext):
        i = pl.program_id(0)
        ext[0:_PH, :] = jnp.where(i > 0, zh_ref[...], 0.0)
        ext[_PH:, :] = z_ref[...]
        t = i * TM + lax.broadcasted_iota(jnp.int32, (TM, 1), 0)
        for gi, w in enumerate(POOL_WINDOWS):
            lo = gi * POOL_GROUP
            pooled, _ = _pooled(ext, t, gi, w)
            mixed = _dot(pooled.astype(BF16), pw_ref[gi].astype(BF16))
            o_ref[:, lo:lo + POOL_GROUP] = (mixed * sc_ref[:, lo:lo + POOL_GROUP]).astype(BF16)

    return _pcall(body, name=name, grid=(T // TM,),
                  in_specs=[_cols(TM, 3), _prev_rows(T, _PH, 3), _layer((4, POOL_GROUP, POOL_GROUP), l),
                            _layer((1, MIX), l)],
                  out_specs=_rows(TM, MIX), out_shape=_sds((T, MIX), BF16),
                  scratch=[pltpu.VMEM((TM + _PH, MIX), F32)], sem=("parallel",))(h, h, pw, scale)


def _pool_bwd(dout, h, dq, dk, dv, pw, scale, l, name):
    T = h.shape[0]
    nb = T // TM

    def body(do_ref, don_ref, z_ref, zh_ref, dq_ref, dk_ref, dv_ref, pw_ref, sc_ref,
             dh_ref, dpw_ref, dsc_ref, ext, ext2):
        i = pl.program_id(0)
        ext[0:_PH, :] = jnp.where(i > 0, zh_ref[...], 0.0)
        ext[_PH:, :] = z_ref[...]
        t = i * TM + lax.broadcasted_iota(jnp.int32, (TM, 1), 0)
        dh_ref[:, 0:MIX] = dq_ref[...]
        dh_ref[:, MIX:2 * MIX] = dk_ref[...]
        dh_ref[:, 2 * MIX:3 * MIX] = dv_ref[...]

        @pl.when(i == 0)
        def _():
            dpw_ref[...] = jnp.zeros_like(dpw_ref)
            dsc_ref[...] = jnp.zeros_like(dsc_ref)

        for gi, w in enumerate(POOL_WINDOWS):
            lo = gi * POOL_GROUP
            pwb = pw_ref[gi].astype(BF16)
            sc = sc_ref[:, lo:lo + POOL_GROUP]
            pooled, cnt = _pooled(ext, t, gi, w)
            pb = pooled.astype(BF16)
            mixed = _dot(pb, pwb)
            dog = do_ref[:, lo:lo + POOL_GROUP]
            dsc_ref[:, lo:lo + POOL_GROUP] += _colsum(dog * mixed)
            dmix = (dog * sc).astype(BF16)
            dpw_ref[gi] += _dot_tn(pb, dmix)
            dpool = _dot_nt(dmix, pwb)
            dmix_n = (jnp.where(i < nb - 1, don_ref[:, lo:lo + POOL_GROUP], 0.0) * sc).astype(BF16)
            dpool_n = _dot_nt(dmix_n, pwb)
            e = dpool / cnt
            ext2[0:TM, lo:lo + POOL_GROUP] = e
            ext2[TM:, lo:lo + POOL_GROUP] = dpool_n * (1.0 / w)
            s = e
            for k in range(1, w):
                s = s + ext2[pl.ds(k, TM), lo:lo + POOL_GROUP]
            dh_ref[:, 3 * MIX + lo:3 * MIX + lo + POOL_GROUP] = (s - dpool).astype(BF16)

    blk = _rows(TM, MIX)
    return _pcall(body, name=name, grid=(nb,),
                  in_specs=[blk, _next_rows(T, _PH), _cols(TM, 3), _prev_rows(T, _PH, 3), blk, blk, blk,
                            _layer((4, POOL_GROUP, POOL_GROUP), l), _layer((1, MIX), l)],
                  out_specs=[_rows(TM, 4 * MIX), _full((4, POOL_GROUP, POOL_GROUP)), _full((1, MIX))],
                  out_shape=[_sds((T, 4 * MIX), BF16), _sds((4, POOL_GROUP, POOL_GROUP), F32), _sds((1, MIX), F32)],
                  scratch=[pltpu.VMEM((TM + _PH, MIX), F32), pltpu.VMEM((TM + _PH, MIX), F32)])(
                      dout, dout, h, h, dq, dk, dv, pw, scale)


def _att_bias_table(rel_bias):
    span = LEFT_CHUNKS * CHUNK
    assert ATT_TQ - 1 <= MAX_REL
    lo = MAX_REL - (ATT_TQ - 1)
    near = rel_bias[:, lo:2 * MAX_REL + 1]
    far = jnp.broadcast_to(rel_bias[:, 2 * MAX_REL:], (HEADS, span + ATT_TQ - 1 - MAX_REL))
    by_dist = jnp.concatenate([near, far], axis=1)
    rev = by_dist[:, ::-1]
    lv = rev.shape[1]
    skew = jnp.tile(rev, (1, ATT_TQ + 1))[:, :ATT_TQ * (lv + 1)].reshape(HEADS, ATT_TQ, lv + 1)[:, :, :ATT_W]
    bias = skew[:, ::-1, :]
    r = jnp.arange(ATT_TQ)[:, None]
    col = jnp.arange(ATT_W)[None, :]
    dchunk = (LEFT_CHUNKS + r // CHUNK) - col // CHUNK
    visible = (dchunk >= 0) & (dchunk <= LEFT_CHUNKS)
    return jnp.where(visible[None], bias, NEG_INF)


def _qrows(n, s=0):
    return pl.BlockSpec((ATT_TQ, n), lambda i: (i, s))


def _att_views_back(d, s):
    return pl.BlockSpec((ATT_TQ, MIX), lambda i: (jnp.maximum(i - (ATT_NV - 1) + d, 0), s))


def _att_scores(q_ref, k_refs, tb_ref, h, kvalid):
    sl = slice(h * HEAD_DIM, (h + 1) * HEAD_DIM)
    qh = (q_ref[:, sl] * (HEAD_DIM ** -0.5)).astype(BF16)
    kc = jnp.concatenate([r[:, sl] for r in k_refs], axis=0).astype(BF16)
    s = _dot_nt(qh, kc) + tb_ref[h]
    return jnp.where(kvalid, s, NEG_INF), kc


def _att_fwd(h, table, name):
    T = h.shape[0]

    def body(q_ref, *refs):
        k_refs = refs[:ATT_NV]
        v_refs = refs[ATT_NV:2 * ATT_NV]
        tb_ref = refs[2 * ATT_NV]
        o_ref, oh_ref, lse_ref = refs[2 * ATT_NV + 1:]
        i = pl.program_id(0)
        col = lax.broadcasted_iota(jnp.int32, (1, ATT_W), 1)
        kvalid = (col + (i - (ATT_NV - 1)) * ATT_TQ) >= 0
        for hd in range(HEADS):
            sl = slice(hd * HEAD_DIM, (hd + 1) * HEAD_DIM)
            s, _ = _att_scores(q_ref, k_refs, tb_ref, hd, kvalid)
            vc = jnp.concatenate([r[:, sl] for r in v_refs], axis=0).astype(BF16)
            m = jnp.max(s, axis=-1, keepdims=True)
            p = jnp.exp(s - m)
            l = jnp.sum(p, axis=-1, keepdims=True)
            o = _dot(p.astype(BF16), vc) / l
            o_ref[:, sl] = o
            oh_ref[:, sl] = o.astype(BF16)
            lse_ref[:, hd:hd + 1] = m + jnp.log(l)

    kviews = [_att_views_back(d, 1) for d in range(ATT_NV)]
    vviews = [_att_views_back(d, 2) for d in range(ATT_NV)]
    return _pcall(body, name=name, grid=(T // ATT_TQ,),
                  in_specs=[_qrows(MIX)] + kviews + vviews + [_full((HEADS, ATT_TQ, ATT_W))],
                  out_specs=[_qrows(MIX), _qrows(MIX), _qrows(HEADS)],
                  out_shape=[_sds((T, MIX), F32), _sds((T, MIX), BF16), _sds((T, HEADS), F32)],
                  sem=("parallel",))(h, *([h] * (2 * ATT_NV)), table)


def _att_bwd_q(h, do, o, lse, table, name):
    T = h.shape[0]

    def body(q_ref, *refs):
        k_refs = refs[:ATT_NV]
        v_refs = refs[ATT_NV:2 * ATT_NV]
        do_ref, o_ref, lse_ref, tb_ref, dq_ref, dl_ref, dtb_ref = refs[2 * ATT_NV:]
        i = pl.program_id(0)
        col = lax.broadcasted_iota(jnp.int32, (1, ATT_W), 1)
        kvalid = (col + (i - (ATT_NV - 1)) * ATT_TQ) >= 0

        @pl.when(i == 0)
        def _():
            dtb_ref[...] = jnp.zeros_like(dtb_ref)

        for hd in range(HEADS):
            sl = slice(hd * HEAD_DIM, (hd + 1) * HEAD_DIM)
            s, kc = _att_scores(q_ref, k_refs, tb_ref, hd, kvalid)
            vc = jnp.concatenate([r[:, sl] for r in v_refs], axis=0).astype(BF16)
            p = jnp.exp(s - lse_ref[:, hd:hd + 1])
            doh = do_ref[:, sl]
            delta = jnp.sum(doh * o_ref[:, sl], axis=-1, keepdims=True)
            dp = _dot_nt(doh.astype(BF16), vc)
            ds = p * (dp - delta)
            dtb_ref[hd] += ds
            dq_ref[:, sl] = (_dot(ds.astype(BF16), kc) * (HEAD_DIM ** -0.5)).astype(BF16)
            dl_ref[:, hd:hd + 1] = delta

    kviews = [_att_views_back(d, 1) for d in range(ATT_NV)]
    vviews = [_att_views_back(d, 2) for d in range(ATT_NV)]
    tb = _full((HEADS, ATT_TQ, ATT_W))
    return _pcall(body, name=name, grid=(T // ATT_TQ,),
                  in_specs=[_qrows(MIX)] + kviews + vviews + [_qrows(MIX), _qrows(MIX), _qrows(HEADS), tb],
                  out_specs=[_qrows(MIX), _qrows(HEADS), tb],
                  out_shape=[_sds((T, MIX), BF16), _sds((T, HEADS), F32), _sds((HEADS, ATT_TQ, ATT_W), F32)])(
                      h, *([h] * (2 * ATT_NV)), do, o, lse, table)


def _att_table_by_view(table):
    t = table.reshape(HEADS, ATT_TQ, ATT_NV, ATT_TQ)[:, :, ::-1, :]
    return jnp.transpose(t, (0, 2, 1, 3)).reshape(HEADS, ATT_W, ATT_TQ)


def _att_bwd_kv(h, do, lse, delta, table_v, name):
    T = h.shape[0]
    nb = T // ATT_TQ

    def fwd_view(d, n, s=0):
        return pl.BlockSpec((ATT_TQ, n), lambda j: (jnp.minimum(j + d, nb - 1), s))

    def body(k_ref, v_ref, *refs):
        q_refs = refs[:ATT_NV]
        do_refs = refs[ATT_NV:2 * ATT_NV]
        lse_refs = refs[2 * ATT_NV:3 * ATT_NV]
        dl_refs = refs[3 * ATT_NV:4 * ATT_NV]
        tb_ref, dk_ref, dv_ref = refs[4 * ATT_NV:]
        j = pl.program_id(0)
        view = lax.broadcasted_iota(jnp.int32, (ATT_W, 1), 0) // ATT_TQ
        valid = (j + view) <= nb - 1
        for hd in range(HEADS):
            sl = slice(hd * HEAD_DIM, (hd + 1) * HEAD_DIM)
            kh = k_ref[:, sl].astype(BF16)
            vh = v_ref[:, sl].astype(BF16)
            qs = (jnp.concatenate([r[:, sl] for r in q_refs], axis=0) * (HEAD_DIM ** -0.5)).astype(BF16)
            dos = jnp.concatenate([r[:, sl] for r in do_refs], axis=0).astype(BF16)
            lses = jnp.concatenate([r[:, hd:hd + 1] for r in lse_refs], axis=0)
            dls = jnp.concatenate([r[:, hd:hd + 1] for r in dl_refs], axis=0)
            sc = _dot_nt(qs, kh) + tb_ref[hd]
            p = jnp.where(valid, jnp.exp(sc - lses), 0.0)
            dv_ref[:, sl] = _dot_tn(p.astype(BF16), dos).astype(BF16)
            ds = p * (_dot_nt(dos, vh) - dls)
            dk_ref[:, sl] = _dot_tn(ds.astype(BF16), qs).astype(BF16)

    qv = [fwd_view(d, MIX, 0) for d in range(ATT_NV)]
    dov = [fwd_view(d, MIX) for d in range(ATT_NV)]
    narrow = [fwd_view(d, HEADS) for d in range(ATT_NV)]
    return _pcall(body, name=name, grid=(nb,),
                  in_specs=[_qrows(MIX, 1), _qrows(MIX, 2)] + qv + dov + narrow + narrow
                  + [_full((HEADS, ATT_W, ATT_TQ))],
                  out_specs=[_qrows(MIX), _qrows(MIX)], out_shape=[_sds((T, MIX), BF16), _sds((T, MIX), BF16)],
                  sem=("parallel",))(h, h, *([h] * ATT_NV), *([do] * ATT_NV), *([lse] * ATT_NV),
                                     *([delta] * ATT_NV), table_v)


_MESH = pl.DeviceIdType.MESH
_ANY = pl.BlockSpec(memory_space=pl.ANY)


def _all_gather(x, name):
    R, C = x.shape

    def body(x_ref, out_ref, send_sems, recv_sems, local_sem):
        xi, yi, ci = lax.axis_index("x"), lax.axis_index("y"), lax.axis_index("c")
        me, sibling = (xi, yi, ci), (xi, yi, 1 - ci)
        chips = [(1 - xi, yi), (xi, 1 - yi), (1 - xi, 1 - yi)]

        def slot(px, py, pc):
            return out_ref.at[4 * px + 2 * py + pc]

        def copy(k, block, to, src=None):
            return pltpu.make_async_remote_copy(
                src_ref=slot(*block) if src is None else src, dst_ref=slot(*block),
                send_sem=send_sems.at[k], recv_sem=recv_sems.at[k], device_id=to, device_id_type=_MESH)

        mine = pltpu.make_async_copy(x_ref, slot(*me), local_sem)
        mine.start()
        first = [copy(0, me, sibling, src=x_ref)]
        first += [copy(1 + j, me, (*chip, ci), src=x_ref) for j, chip in enumerate(chips)]
        for cp in first:
            cp.start()
        passed = [copy(4 + j, (*chip, ci), sibling) for j, chip in enumerate(chips)]
        for j, chip in enumerate(chips):
            copy(1 + j, (*chip, ci), me).wait_recv()
            passed[j].start()
        copy(0, sibling, me).wait_recv()
        for j, chip in enumerate(chips):
            copy(4 + j, (*chip, 1 - ci), me).wait_recv()
        for cp in first + passed:
            cp.wait_send()
        mine.wait()

    return pl.pallas_call(
        body, name=name, out_shape=_sds((N_DEV, R, C), x.dtype), in_specs=[_ANY], out_specs=_ANY,
        scratch_shapes=[pltpu.SemaphoreType.DMA((7,)), pltpu.SemaphoreType.DMA((7,)), pltpu.SemaphoreType.DMA(())],
    )(x)


def _all_to_all(s, name):
    _, R, C = s.shape

    def body(s_ref, r_ref, send_sems, recv_sems, local_sem):
        xi, yi, ci = lax.axis_index("x"), lax.axis_index("y"), lax.axis_index("c")
        me = 4 * xi + 2 * yi + ci
        mine = pltpu.make_async_copy(s_ref.at[me], r_ref.at[me], local_sem)
        mine.start()
        copies = []
        for m in range(1, N_DEV):
            px = 1 - xi if m & 4 else xi
            py = 1 - yi if m & 2 else yi
            pc = 1 - ci if m & 1 else ci
            peer = 4 * px + 2 * py + pc
            copies.append((
                pltpu.make_async_remote_copy(src_ref=s_ref.at[peer], dst_ref=r_ref.at[me], send_sem=send_sems.at[m - 1],
                                             recv_sem=recv_sems.at[m - 1], device_id=(px, py, pc), device_id_type=_MESH),
                pltpu.make_async_remote_copy(src_ref=s_ref.at[me], dst_ref=r_ref.at[peer], send_sem=send_sems.at[m - 1],
                                             recv_sem=recv_sems.at[m - 1], device_id=(px, py, pc), device_id_type=_MESH)))
        for send, _ in copies:
            send.start()
        for _, recv in copies:
            recv.wait_recv()
        for send, _ in copies:
            send.wait_send()
        mine.wait()

    return pl.pallas_call(
        body, name=name, out_shape=_sds(s.shape, s.dtype), in_specs=[_ANY], out_specs=_ANY,
        scratch_shapes=[pltpu.SemaphoreType.DMA((7,)), pltpu.SemaphoreType.DMA((7,)), pltpu.SemaphoreType.DMA(())],
    )(s)


_ADAMW_BLOCK_BYTES = 6 * 1024 * 1024


def _adamw(parts, row_off, w, m, v, name):
    R, C = w.shape
    tr = None
    for cand in (512, 256, 128, 64, 32, 16):
        step_bytes = cand * C * (N_DEV * parts.dtype.itemsize + 7 * 4)
        if R % cand == 0 and row_off % cand == 0 and step_bytes <= _ADAMW_BLOCK_BYTES:
            tr = cand
            break
    assert tr is not None, (R, C, row_off)
    off = row_off // tr
    c1 = 1.0 - ADAM_B1 ** ADAM_STEP
    c2 = 1.0 - ADAM_B2 ** ADAM_STEP

    def body(p_ref, w_ref, m_ref, v_ref, g_ref, d_ref, mo_ref, vo_ref):
        g = p_ref[0].astype(F32)
        for j in range(1, N_DEV):
            g = g + p_ref[j].astype(F32)
        mn = ADAM_B1 * m_ref[...] + (1.0 - ADAM_B1) * g
        vn = ADAM_B2 * v_ref[...] + (1.0 - ADAM_B2) * (g * g)
        m_hat = mn / c1
        v_hat = vn / c2
        g_ref[...] = g
        d_ref[...] = -ADAM_LR * (m_hat / (jnp.sqrt(v_hat) + ADAM_EPS) + ADAM_WD * w_ref[...])
        mo_ref[...] = mn
        vo_ref[...] = vn

    blk = _rows(tr, C)
    return _pcall(body, name=name, grid=(R // tr,),
                  in_specs=[pl.BlockSpec((N_DEV, tr, C), lambda i: (0, off + i, 0)), blk, blk, blk],
                  out_specs=[blk] * 4, out_shape=[_sds((R, C), F32)] * 4,
                  sem=("parallel",))(parts, w, m, v)


def _piece_rows(size):
    return -(-size // PIECE) * 16


def _pack_rows(arrays, lead=()):
    parts = []
    total = 0
    for a in arrays:
        flat = a.reshape(lead + (-1,))
        size = flat.shape[-1]
        rows = _piece_rows(size)
        pad = [(0, 0)] * len(lead) + [(0, rows * FLAT_COLS - size)]
        parts.append(jnp.pad(flat, pad).reshape(lead + (rows, FLAT_COLS)))
        total += rows
    tail = -total % FLAT_ROWS
    if tail:
        parts.append(jnp.zeros(lead + (tail, FLAT_COLS), parts[0].dtype))
    return jnp.concatenate(parts, axis=len(lead))


def _unpack_rows(buf, shapes, lead=()):
    out = []
    row = 0
    nl = len(lead)
    for shape in shapes:
        size = math.prod(shape)
        rows = _piece_rows(size)
        piece = lax.slice_in_dim(buf, row, row + rows, axis=nl).reshape(lead + (rows * FLAT_COLS,))
        out.append(lax.slice_in_dim(piece, 0, size, axis=nl).reshape(lead + tuple(shape)))
        row += rows
    return out


def _to_blocks(full, axis):
    shp = full.shape
    split = full.reshape(shp[:axis] + (N_DEV, shp[axis] // N_DEV) + shp[axis + 1:])
    return jnp.moveaxis(split, axis, 0)


def _from_blocks(blocks, axis):
    shp = blocks.shape[1:]
    moved = jnp.moveaxis(blocks, 0, axis)
    return moved.reshape(shp[:axis] + (N_DEV * shp[axis],) + shp[axis + 1:])


def _local_step(x, p, tgt, W):
    wup = W['ffn_w_up'].reshape(2, FF_J, DEPTH, D_MODEL, FF_NB)
    wd4 = W['ffn_w_down'].reshape(DEPTH, FF_J, FF_NB, D_MODEL)
    ln = {n: _vecs(W[n]) for n in ('ln_mix_g', 'ln_mix_b', 'ln_ffn_g', 'ln_ffn_b', 'ple_b_gate')}
    bglu = _vecs(W['ev_b_glu'])
    pscale = _vecs(W['od_pool_scale'])
    saved = []
    x0 = x
    x0h = x.astype(BF16)
    for i in range(DEPTH):
        L = f"L{i}_"
        j = i // 2
        s = dict(x0=x0, x0h=x0h)
        if i % 2 == 0:
            h = _mm_in(x0h, W['ev_w_in'], j, L + "mm_in")
            params = tuple(W[n][j] for n in ('ev_lambda_re', 'ev_lambda_im', 'ev_log_dt', 'ev_b_re', 'ev_b_im',
                                             'ev_c_re', 'ev_c_im', 'ev_d'))
            (r0, qt, pm, a1, a2), op_vjp = jax.vjp(_s5_operators, *params)
            umat = _to_chunks(h[:, :MIX])
            st = jnp.transpose(_s5_chunk_state(umat, pm, L + "s5_state"), (1, 0, 2))
            xp_t = _s5_scan_fwd(st, a1, a2, L + "s5_scan")
            xprev = jnp.transpose(xp_t, (1, 0, 2))
            y = _from_chunks(_s5_output(umat, xprev, r0, qt, L + "s5_out"))
            ya, gh = _glu_fwd(y, W['ev_w_glu'], bglu, j, L + "glu")
            yb = _conv_fwd(h, W['ev_conv_w'], j, L + "conv")
            s.update(op_vjp=op_vjp, r0=r0, qt=qt, pm=pm, a1=a1, a2=a2, umat=umat, xp_t=xp_t, xprev=xprev, y=y, gh=gh)
            wo, win = W['ev_w_out'], W['ev_w_in']
        else:
            h = _mm_in(x0h, W['od_w_in'], j, L + "mm_in")
            table, tb_vjp = jax.vjp(_att_bias_table, W['od_rel_bias'][j])
            of, ya, lse = _att_fwd(h, table, L + "att")
            yb = _pool_fwd(h, W['od_pool_w'], pscale, j, L + "pool")
            s.update(table=table, tb_vjp=tb_vjp, of=of, lse=lse)
            wo, win = W['od_w_out'], W['od_w_in']
        g1, b1 = (ln['ln_mix_g'], i), (ln['ln_mix_b'], i)
        g2, b2 = (ln['ln_ffn_g'], i), (ln['ln_ffn_b'], i)
        xhat1, rstd1, x1h = _mm_out_ln(ya, yb, wo, j, x0, g1, b1, L + "mm_out_ln")
        hf, a3 = _mm_up(x1h, wup, i, L + "mm_up")
        xhat2, rstd2, x2h = _mm_down_ln(a3, wd4, i, xhat1, g1, b1, g2, b2, L + "mm_down_ln")
        x3, x3h = _mm_ple(x2h, xhat2, ln['ln_ffn_g'], ln['ln_ffn_b'], p, W['ple_w_gate'], ln['ple_b_gate'],
                          W['ple_w_proj'], i, L + "mm_ple")
        s.update(h=h, win=win, wo=wo, ya=ya, yb=yb, xhat1=xhat1, rstd1=rstd1, x1h=x1h, hf=hf, a3=a3, xhat2=xhat2,
                 rstd2=rstd2, x2h=x2h, g1=g1)
        saved.append(s)
        x0, x0h = x3, x3h

    dx, loss = _loss_head(x0, tgt, "loss_head")

    G = {n: [None] * W[n].shape[0 if n not in ('ev_w_in', 'od_w_in', 'ffn_w_up') else 1] for n in WEIGHT_NAMES}
    for i in reversed(range(DEPTH)):
        L = f"L{i}_"
        j = i // 2
        s = saved[i]
        dr2, dr2h, dpreh, dpph, dbg, dg2, db2 = _ple_bwd(
            dx, s['x2h'], p, W['ple_w_gate'], ln['ple_b_gate'], W['ple_w_proj'], i, s['xhat2'], s['rstd2'],
            ln['ln_ffn_g'], L + "ple_bwd")
        G['ple_w_gate'][i] = _mm_tn(s['x2h'], dpreh, L + "dw_gate")
        G['ple_w_proj'][i] = _mm_tn(p, dpph, L + "dw_proj", a_layer=i)
        G['ple_b_gate'][i] = dbg[0]
        G['ln_ffn_g'][i] = dg2[0]
        G['ln_ffn_b'][i] = db2[0]
        dhf = _ffn_bwd1(dr2h, wd4, i, s['hf'], L + "ffn_bwd1")
        G['ffn_w_down'][i] = _mm_tn_ablk(s['a3'], dr2h, L + "dw_down").reshape(D_FF, D_MODEL)
        T = dhf.shape[2]
        G['ffn_w_up'][i] = _mm_tn_bblk(s['x1h'], dhf.reshape(N_DEV, T, FF_NB), L + "dw_up")
        dr1, dr1h, dg1, db1 = _ffn_bwd2(dhf, wup, i, dr2, s['xhat1'], s['rstd1'], s['g1'], L + "ffn_bwd2")
        G['ln_mix_g'][i] = dg1[0]
        G['ln_mix_b'][i] = db1[0]
        dya, dyb = _out_bwd(dr1h, s['wo'], j, L + "out_bwd")
        dwo = jnp.concatenate([_mm_tn(s['ya'], dr1h, L + "dw_out_a"), _mm_tn(s['yb'], dr1h, L + "dw_out_b")], axis=0)
        if i % 2 == 0:
            G['ev_w_out'][j] = dwo
            dy, dzh, dbglu = _glu_bwd(s['y'], dya, W['ev_w_glu'], bglu, j, L + "glu_bwd")
            G['ev_w_glu'][j] = _mm_tn(s['gh'], dzh, L + "dw_glu")
            G['ev_b_glu'][j] = dbglu[0]
            dymat = _to_chunks(dy)
            dxp_t = jnp.transpose(_s5_bwd_state(dymat, s['qt'], L + "s5_bwd_state"), (1, 0, 2))
            ds_t, da1, da2 = _s5_scan_bwd(dxp_t, s['xp_t'], s['a1'], s['a2'], L + "s5_scan_bwd")
            dumat, dr0, dqt, dpm = _s5_bwd_main(s['umat'], s['xprev'], dymat, jnp.transpose(ds_t, (1, 0, 2)),
                                                s['r0'], s['pm'], L + "s5_bwd")
            dparams = s['op_vjp']((dr0, dqt, dpm, da1, da2))
            for n, dpar in zip(('ev_lambda_re', 'ev_lambda_im', 'ev_log_dt', 'ev_b_re', 'ev_b_im', 'ev_c_re',
                                'ev_c_im', 'ev_d'), dparams):
                G[n][j] = dpar
            dua = _from_chunks(dumat)
            dh, dcw = _conv_bwd(dyb, s['h'], dua, W['ev_conv_w'], j, L + "conv_bwd")
            G['ev_conv_w'][j] = dcw[:3]
            wname = 'ev_w_in'
        else:
            G['od_w_out'][j] = dwo
            dq, delta, dtable = _att_bwd_q(s['h'], dya, s['of'], s['lse'], s['table'], L + "att_bwd_q")
            dk, dv = _att_bwd_kv(s['h'], dya, s['lse'], delta, _att_table_by_view(s['table']), L + "att_bwd_kv")
            G['od_rel_bias'][j] = s['tb_vjp'](dtable)[0]
            dh, dpw, dsc = _pool_bwd(dyb, s['h'], dq, dk, dv, W['od_pool_w'], pscale, j, L + "pool_bwd")
            G['od_pool_w'][j] = dpw
            G['od_pool_scale'][j] = dsc[0]
            wname = 'od_w_in'
        G[wname][j] = _mm_tn(s['x0h'], dh, L + "dw_in", blocked_n=IN_NB)
        dx = _in_bwd(dh, s['win'], j, dr1, L + "in_bwd")

    return loss, dx, G


def _slab(a):
    return a.reshape(-1, a.shape[-1])


def _gather_weights(W):
    full = {n: W[n] for n in REPLICATED}
    slabs = [_slab(W[n].astype(BF16)) for n in ROW_SHARDED]
    got = _all_gather(jnp.concatenate(slabs, axis=0), "gather_w_rows")
    row = 0
    for n, sl in zip(ROW_SHARDED, slabs):
        L, k8, d = W[n].shape
        piece = lax.slice_in_dim(got, row, row + sl.shape[0], axis=1).reshape(N_DEV, L, k8, d)
        full[n] = jnp.swapaxes(piece, 0, 1).reshape(L, N_DEV * k8, d)
        row += sl.shape[0]
    got = _all_gather(_slab(W['ffn_w_up'].astype(BF16)), "gather_w_up")
    full['ffn_w_up'] = got.reshape((N_DEV,) + W['ffn_w_up'].shape)
    slabs = [_slab(W[n].astype(BF16)) for n in COL_IN]
    got = _all_gather(jnp.concatenate(slabs, axis=0), "gather_w_in")
    row = 0
    for n, sl in zip(COL_IN, slabs):
        full[n] = lax.slice_in_dim(got, row, row + sl.shape[0], axis=1).reshape((N_DEV,) + W[n].shape)
        row += sl.shape[0]
    got = _all_gather(_pack_rows([W[n].astype(BF16) for n in SMALL_SHARDED]), "gather_w_small")
    shapes = [W[n].shape for n in SMALL_SHARDED]
    for n, blocks in zip(SMALL_SHARDED, _unpack_rows(got, shapes, lead=(N_DEV,))):
        full[n] = _from_blocks(blocks, SHARD_AXIS[n])
    for n in ('ev_conv_w', 'od_pool_scale'):
        full[n] = full[n].astype(F32)
    return full


def _step(x, p, tgt, W, M, V):
    full = _gather_weights(W)
    loss, dx, G = _local_step(x[0], p[:, 0], tgt[0], full)
    res = {}

    def update(parts, row, n, tag):
        shape = W[n].shape
        outs = _adamw(parts, row, _slab(W[n]), _slab(M[n]), _slab(V[n]), "adamw_" + tag)
        for kind, a in zip(('grad', 'delta', 'm', 'v'), outs):
            res[kind, n] = a.reshape(shape)
        return row + math.prod(shape[:-1])

    send = []
    for n in ROW_SHARDED:
        g = jnp.stack(G[n], axis=0)
        L, K, d = g.shape
        send.append(jnp.swapaxes(g.reshape(L, N_DEV, K // N_DEV, d), 0, 1).reshape(N_DEV, L * K // N_DEV, d))
    parts = _all_to_all(jnp.concatenate(send, axis=1), "scatter_g_rows")
    row = 0
    for n in ROW_SHARDED:
        row = update(parts, row, n, n)
    g = jnp.stack(G['ffn_w_up'], axis=1)
    parts = _all_to_all(g.reshape(N_DEV, -1, FF_NB), "scatter_g_up")
    update(parts, 0, 'ffn_w_up', 'ffn_w_up')
    send = [jnp.stack(G[n], axis=1).reshape(N_DEV, -1, IN_NB) for n in COL_IN]
    parts = _all_to_all(jnp.concatenate(send, axis=1), "scatter_g_in")
    row = 0
    for n in COL_IN:
        row = update(parts, row, n, n)

    shapes = [W[n].shape for n in SMALL_SHARDED]
    send = _pack_rows([_to_blocks(jnp.stack(G[n], axis=0), SHARD_AXIS[n]).astype(BF16) for n in SMALL_SHARDED],
                      lead=(N_DEV,))
    parts = _all_to_all(send, "scatter_g_small")
    outs = _adamw(parts, 0, _pack_rows([W[n] for n in SMALL_SHARDED]), _pack_rows([M[n] for n in SMALL_SHARDED]),
                  _pack_rows([V[n] for n in SMALL_SHARDED]), "adamw_small")
    for kind, buf in zip(('grad', 'delta', 'm', 'v'), outs):
        for n, a in zip(SMALL_SHARDED, _unpack_rows(buf, shapes)):
            res[kind, n] = a

    repl_shapes = [W[n].shape for n in REPLICATED]
    small = _pack_rows([jnp.stack(G[n], axis=0) for n in REPLICATED] + [loss])
    parts = _all_gather(small, "gather_g_replicated")
    zero = jnp.zeros((1, 1), F32)
    outs = _adamw(parts, 0, _pack_rows([W[n] for n in REPLICATED] + [zero]),
                  _pack_rows([M[n] for n in REPLICATED] + [zero]),
                  _pack_rows([V[n] for n in REPLICATED] + [zero]), "adamw_replicated")
    for kind, buf in zip(('grad', 'delta', 'm', 'v'), outs):
        arrays = _unpack_rows(buf, repl_shapes + [(1, 1)])
        for n, a in zip(REPLICATED, arrays):
            res[kind, n] = a
        if kind == 'grad':
            total_loss = arrays[-1].reshape(())

    out = [total_loss, dx[None]]
    for kind in ('grad', 'delta', 'm', 'v'):
        out += [res[kind, n] for n in WEIGHT_NAMES]
    return tuple(out)


def kernel(x, p, ev_w_in, ev_lambda_re, ev_lambda_im, ev_log_dt, ev_b_re, ev_b_im, ev_c_re, ev_c_im, ev_d, ev_w_glu, ev_b_glu, ev_conv_w, ev_w_out, od_w_in, od_rel_bias, od_pool_w, od_pool_scale, od_w_out, ln_mix_g, ln_mix_b, ln_ffn_g, ln_ffn_b, ffn_w_up, ffn_w_down, ple_w_proj, ple_w_gate, ple_b_gate, loss_target, m_ev_w_in, m_ev_lambda_re, m_ev_lambda_im, m_ev_log_dt, m_ev_b_re, m_ev_b_im, m_ev_c_re, m_ev_c_im, m_ev_d, m_ev_w_glu, m_ev_b_glu, m_ev_conv_w, m_ev_w_out, m_od_w_in, m_od_rel_bias, m_od_pool_w, m_od_pool_scale, m_od_w_out, m_ln_mix_g, m_ln_mix_b, m_ln_ffn_g, m_ln_ffn_b, m_ffn_w_up, m_ffn_w_down, m_ple_w_proj, m_ple_w_gate, m_ple_b_gate, v_ev_w_in, v_ev_lambda_re, v_ev_lambda_im, v_ev_log_dt, v_ev_b_re, v_ev_b_im, v_ev_c_re, v_ev_c_im, v_ev_d, v_ev_w_glu, v_ev_b_glu, v_ev_conv_w, v_ev_w_out, v_od_w_in, v_od_rel_bias, v_od_pool_w, v_od_pool_scale, v_od_w_out, v_ln_mix_g, v_ln_mix_b, v_ln_ffn_g, v_ln_ffn_b, v_ffn_w_up, v_ffn_w_down, v_ple_w_proj, v_ple_w_gate, v_ple_b_gate):
    given = dict(locals())
    W = {n: given[n] for n in WEIGHT_NAMES}
    M = {n: given["m_" + n] for n in WEIGHT_NAMES}
    V = {n: given["v_" + n] for n in WEIGHT_NAMES}
    return _step(x, p, loss_target, W, M, V)
```

```python
import math

import jax
import jax.numpy as jnp
from jax import lax
from jax.experimental import pallas as pl
from jax.experimental.pallas import tpu as pltpu

F32 = jnp.float32
BF16 = jnp.bfloat16
HI = lax.Precision.HIGHEST

D_MODEL = 1024
DEPTH = 4
CHUNK = 64
MIX = 512
S5_GROUP = 16
S5_GROUPS = 32
S5_STATE = 64
HEADS = 8
HEAD_DIM = 64
LEFT_CHUNKS = 8
MAX_REL = 128
POOL_WINDOWS = (2, 4, 8, 16)
POOL_GROUP = 128
D_FF = 2816
D_PLE = 256
ALPHA = (2 * DEPTH) ** 0.25
LN_EPS = 1e-5
NEG_INF = -1e30
ADAM_LR = 0.001
ADAM_B1 = 0.9
ADAM_B2 = 0.999
ADAM_EPS = 1e-08
ADAM_WD = 0.01
ADAM_STEP = 10
N_DEV = 8

WEIGHT_NAMES = ['ev_w_in', 'ev_lambda_re', 'ev_lambda_im', 'ev_log_dt', 'ev_b_re', 'ev_b_im', 'ev_c_re', 'ev_c_im',
                'ev_d', 'ev_w_glu', 'ev_b_glu', 'ev_conv_w', 'ev_w_out', 'od_w_in', 'od_rel_bias', 'od_pool_w',
                'od_pool_scale', 'od_w_out', 'ln_mix_g', 'ln_mix_b', 'ln_ffn_g', 'ln_ffn_b', 'ffn_w_up', 'ffn_w_down',
                'ple_w_proj', 'ple_w_gate', 'ple_b_gate']
SHARD_AXIS = {'ev_w_in': 2, 'ev_w_glu': 1, 'ev_conv_w': 2, 'ev_w_out': 1, 'od_w_in': 2, 'od_pool_scale': 1,
              'od_w_out': 1, 'ffn_w_up': 2, 'ffn_w_down': 1, 'ple_w_proj': 2, 'ple_w_gate': 1}
REPLICATED = [n for n in WEIGHT_NAMES if n not in SHARD_AXIS]
ROW_SHARDED = ['ev_w_out', 'od_w_out', 'ffn_w_down', 'ple_w_gate']
COL_IN = ['ev_w_in', 'od_w_in']
SMALL_SHARDED = ['ev_w_glu', 'ev_conv_w', 'od_pool_scale', 'ple_w_proj']

VMEM_LIMIT = 48 * 1024 * 1024
TM = 512
IN_NB = 4 * MIX // N_DEV
FF_NB = 2 * D_FF // N_DEV
FF_J = N_DEV // 2
S5_LC = 32
S5_LW = S5_LC * S5_GROUP
ATT_TQ = 128
ATT_NV = LEFT_CHUNKS * CHUNK // ATT_TQ + 1
ATT_W = ATT_NV * ATT_TQ
FLAT_COLS = 1024
FLAT_ROWS = 256
PIECE = 16 * FLAT_COLS


def _sds(shape, dt):
    return jax.ShapeDtypeStruct(shape, dt)


def _pcall(body, *, name, grid, in_specs, out_specs, out_shape, scratch=(), sem=None):
    sem = sem or ("arbitrary",) * len(grid)
    return pl.pallas_call(
        body, name=name, grid=grid, in_specs=in_specs, out_specs=out_specs, out_shape=out_shape,
        scratch_shapes=scratch,
        compiler_params=pltpu.CompilerParams(dimension_semantics=sem, vmem_limit_bytes=VMEM_LIMIT))


def _rows(tm, n):
    return pl.BlockSpec((tm, n), lambda i: (i, 0))


def _cols(tm, s):
    return pl.BlockSpec((tm, MIX), lambda i: (i, s))


def _full(shape):
    nd = len(shape)
    return pl.BlockSpec(shape, lambda *_: (0,) * nd)


def _layer(shape, l):
    nd = len(shape)
    return pl.BlockSpec((None,) + tuple(shape), lambda *_: (l,) + (0,) * nd)


def _vecs(a):
    return a.reshape(a.shape[0], 1, a.shape[1])


def _dot(a, b, precision=None):
    return jnp.dot(a, b, preferred_element_type=F32, precision=precision)


def _dot_nt(a, b, precision=None):
    return lax.dot_general(a, b, (((1,), (1,)), ((), ())), preferred_element_type=F32, precision=precision)


def _dot_tn(a, b, precision=None):
    return lax.dot_general(a, b, (((0,), (0,)), ((), ())), preferred_element_type=F32, precision=precision)


def _sigmoid(x):
    return 1.0 / (1.0 + jnp.exp(-x))


_GELU_C = math.sqrt(2.0 / math.pi)


def _gelu(x):
    return 0.5 * x * (1.0 + jnp.tanh(_GELU_C * (x + 0.044715 * x * x * x)))


def _gelu_grad(x):
    t = jnp.tanh(_GELU_C * (x + 0.044715 * x * x * x))
    return 0.5 * (1.0 + t) + 0.5 * x * (1.0 - t * t) * _GELU_C * (1.0 + 3.0 * 0.044715 * x * x)


def _ln_fwd(r, g, b):
    mu = jnp.mean(r, axis=-1, keepdims=True)
    xc = r - mu
    var = jnp.mean(xc * xc, axis=-1, keepdims=True)
    rstd = lax.rsqrt(var + LN_EPS)
    xhat = xc * rstd
    return xhat, rstd, xhat * g + b


def _ln_bwd(dx, xhat, rstd, g):
    dxh = dx * g
    m1 = jnp.mean(dxh, axis=-1, keepdims=True)
    m2 = jnp.mean(dxh * xhat, axis=-1, keepdims=True)
    return rstd * (dxh - m1 - xhat * m2)


def _colsum(x):
    return jnp.sum(x, axis=0, keepdims=True)


def _mm_in(xh, win8, l, name):
    T = xh.shape[0]

    def body(x_ref, w_ref, h_ref):
        x = x_ref[...]
        for b in range(N_DEV):
            h_ref[:, b * IN_NB:(b + 1) * IN_NB] = _dot(x, w_ref[b])

    return _pcall(body, name=name, grid=(T // TM,),
                  in_specs=[_rows(TM, D_MODEL),
                            pl.BlockSpec((N_DEV, None, D_MODEL, IN_NB), lambda i: (0, l, 0, 0))],
                  out_specs=_rows(TM, 4 * MIX), out_shape=_sds((T, 4 * MIX), F32),
                  sem=("parallel",))(xh, win8)


def _tile_of(n, cap):
    best = None
    for t in range(128, min(n, cap) + 1, 128):
        if n % t == 0:
            best = t
    assert best is not None, n
    return best


def _mm_tn(a, b, name, a_layer=None, blocked_n=None):
    T, M = a.shape[-2:]
    N = b.shape[1]
    tm = _tile_of(M, 1408)
    nbs = max(1, 512 // blocked_n) if blocked_n else 1
    tn = blocked_n * nbs if blocked_n else _tile_of(N, 1024 if tm <= 512 else 512)
    tk = min(T, 1024)
    nk = T // tk

    def body(a_ref, b_ref, o_ref, acc):
        k = pl.program_id(2)

        @pl.when(k == 0)
        def _():
            acc[...] = jnp.zeros_like(acc)

        acc[...] += _dot_tn(a_ref[...].astype(BF16), b_ref[...].astype(BF16))

        @pl.when(k == nk - 1)
        def _():
            if blocked_n:
                for sb in range(nbs):
                    o_ref[sb] = acc[:, sb * blocked_n:(sb + 1) * blocked_n].astype(BF16)
            else:
                o_ref[...] = acc[...].astype(BF16)

    if a_layer is None:
        a_spec = pl.BlockSpec((tk, tm), lambda i, j, k: (k, i))
    else:
        a_spec = pl.BlockSpec((None, tk, tm), lambda i, j, k: (a_layer, k, i))
    if blocked_n:
        o_spec = pl.BlockSpec((nbs, tm, blocked_n), lambda i, j, k: (j, i, 0))
        o_shape = _sds((N // blocked_n, M, blocked_n), BF16)
    else:
        o_spec = pl.BlockSpec((tm, tn), lambda i, j, k: (i, j))
        o_shape = _sds((M, N), BF16)
    return _pcall(body, name=name, grid=(M // tm, N // tn, nk),
                  in_specs=[a_spec, pl.BlockSpec((tk, tn), lambda i, j, k: (k, j))],
                  out_specs=o_spec, out_shape=o_shape, scratch=[pltpu.VMEM((tm, tn), F32)],
                  sem=("parallel", "parallel", "arbitrary"))(a, b)


def _mm_tn_bblk(a, b3, name, per_step=2):
    T, M = a.shape
    NB, _, n = b3.shape
    tk = min(T, 1024)
    nk = T // tk

    def body(a_ref, b_ref, o_ref, acc):
        k = pl.program_id(1)

        @pl.when(k == 0)
        def _():
            acc[...] = jnp.zeros_like(acc)

        av = a_ref[...]
        for s in range(per_step):
            acc[s] += _dot_tn(av, b_ref[s])

        @pl.when(k == nk - 1)
        def _():
            o_ref[...] = acc[...].astype(BF16)

    return _pcall(body, name=name, grid=(NB // per_step, nk),
                  in_specs=[pl.BlockSpec((tk, M), lambda j, k: (k, 0)),
                            pl.BlockSpec((per_step, tk, n), lambda j, k: (j, k, 0))],
                  out_specs=pl.BlockSpec((per_step, M, n), lambda j, k: (j, 0, 0)),
                  out_shape=_sds((NB, M, n), BF16), scratch=[pltpu.VMEM((per_step, M, n), F32)],
                  sem=("parallel", "arbitrary"))(a, b3)


def _mm_tn_ablk(a3, b, name):
    NA, T, m = a3.shape
    N = b.shape[1]
    tn = _tile_of(N, 1024)
    tk = min(T, 1024)
    nk = T // tk

    def body(a_ref, b_ref, o_ref, acc):
        k = pl.program_id(2)

        @pl.when(k == 0)
        def _():
            acc[...] = jnp.zeros_like(acc)

        acc[...] += _dot_tn(a_ref[...], b_ref[...])

        @pl.when(k == nk - 1)
        def _():
            o_ref[...] = acc[...].astype(BF16)

    return _pcall(body, name=name, grid=(NA, N // tn, nk),
                  in_specs=[pl.BlockSpec((None, tk, m), lambda j, n, k: (j, k, 0)),
                            pl.BlockSpec((tk, tn), lambda j, n, k: (k, n))],
                  out_specs=pl.BlockSpec((None, m, tn), lambda j, n, k: (j, 0, n)),
                  out_shape=_sds((NA, m, N), BF16), scratch=[pltpu.VMEM((m, tn), F32)],
                  sem=("parallel", "parallel", "arbitrary"))(a3, b)


def _mm_out_ln(ya, yb, wo, l, x0, g, b, name):
    T = x0.shape[0]

    def body(ya_ref, yb_ref, w_ref, x0_ref, g_ref, b_ref, xh_ref, rs_ref, x1_ref):
        r = ALPHA * x0_ref[...] + _dot(ya_ref[...], w_ref[0:MIX, :]) + _dot(yb_ref[...], w_ref[MIX:, :])
        xhat, rstd, x1 = _ln_fwd(r, g_ref[...], b_ref[...])
        xh_ref[...] = xhat
        rs_ref[...] = rstd
        x1_ref[...] = x1.astype(BF16)

    vec = _layer((1, D_MODEL), g[1])
    return _pcall(body, name=name, grid=(T // TM,),
                  in_specs=[_rows(TM, MIX), _rows(TM, MIX), _layer((D_MODEL, D_MODEL), l), _rows(TM, D_MODEL),
                            vec, vec],
                  out_specs=[_rows(TM, D_MODEL), _rows(TM, 1), _rows(TM, D_MODEL)],
                  out_shape=[_sds((T, D_MODEL), F32), _sds((T, 1), F32), _sds((T, D_MODEL), BF16)],
                  sem=("parallel",))(ya, yb, wo, x0, g[0], b[0])


def _mm_up(x1h, wup, l, name):
    T = x1h.shape[0]

    def body(x_ref, w_ref, hf_ref, a_ref):
        x = x_ref[...]
        g = _dot(x, w_ref[0])
        u = _dot(x, w_ref[1])
        hf_ref[0] = g.astype(BF16)
        hf_ref[1] = u.astype(BF16)
        a_ref[...] = (g * _sigmoid(g) * u).astype(BF16)

    return _pcall(body, name=name, grid=(FF_J, T // TM),
                  in_specs=[pl.BlockSpec((TM, D_MODEL), lambda j, i: (i, 0)),
                            pl.BlockSpec((2, None, None, D_MODEL, FF_NB), lambda j, i: (0, j, l, 0, 0))],
                  out_specs=[pl.BlockSpec((2, None, TM, FF_NB), lambda j, i: (0, j, i, 0)),
                             pl.BlockSpec((None, TM, FF_NB), lambda j, i: (j, i, 0))],
                  out_shape=[_sds((2, FF_J, T, FF_NB), BF16), _sds((FF_J, T, FF_NB), BF16)],
                  sem=("parallel", "parallel"))(x1h, wup)


def _mm_down_ln(a3, wd4, l, xhat1, g1, b1, g2, b2, name):
    T = a3.shape[1]

    def body(a_ref, w_ref, xh1_ref, g1_ref, b1_ref, g2_ref, b2_ref, xh_ref, rs_ref, x2_ref):
        x1 = xh1_ref[...] * g1_ref[...] + b1_ref[...]
        r = ALPHA * x1
        for j in range(FF_J):
            r = r + _dot(a_ref[j], w_ref[j])
        xhat, rstd, x2 = _ln_fwd(r, g2_ref[...], b2_ref[...])
        xh_ref[...] = xhat
        rs_ref[...] = rstd
        x2_ref[...] = x2.astype(BF16)

    vec = _layer((1, D_MODEL), g1[1])
    return _pcall(body, name=name, grid=(T // TM,),
                  in_specs=[pl.BlockSpec((FF_J, TM, FF_NB), lambda i: (0, i, 0)),
                            _layer((FF_J, FF_NB, D_MODEL), l), _rows(TM, D_MODEL), vec, vec, vec, vec],
                  out_specs=[_rows(TM, D_MODEL), _rows(TM, 1), _rows(TM, D_MODEL)],
                  out_shape=[_sds((T, D_MODEL), F32), _sds((T, 1), F32), _sds((T, D_MODEL), BF16)],
                  sem=("parallel",))(a3, wd4, xhat1, g1[0], b1[0], g2[0], b2[0])


def _mm_ple(x2h, xhat2, g2, b2, p, wg, bg, wp, l, name):
    T = x2h.shape[0]

    def body(x2h_ref, xh_ref, g2_ref, b2_ref, p_ref, wg_ref, bg_ref, wp_ref, o_ref, oh_ref):
        x2 = xh_ref[...] * g2_ref[...] + b2_ref[...]
        gate = _sigmoid(_dot(x2h_ref[...], wg_ref[...]) + bg_ref[...])
        pp = _dot(p_ref[...].astype(BF16), wp_ref[...])
        x3 = x2 + gate * pp
        o_ref[...] = x3
        oh_ref[...] = x3.astype(BF16)

    vec = _layer((1, D_MODEL), l)
    return _pcall(body, name=name, grid=(T // TM,),
                  in_specs=[_rows(TM, D_MODEL), _rows(TM, D_MODEL), vec, vec,
                            pl.BlockSpec((None, TM, D_PLE), lambda i: (l, i, 0)),
                            _layer((D_MODEL, D_MODEL), l), vec, _layer((D_PLE, D_MODEL), l)],
                  out_specs=[_rows(TM, D_MODEL), _rows(TM, D_MODEL)],
                  out_shape=[_sds((T, D_MODEL), F32), _sds((T, D_MODEL), BF16)],
                  sem=("parallel",))(x2h, xhat2, g2, b2, p, wg, bg, wp)


def _loss_head(x3, tgt, name):
    T = x3.shape[0]

    def body(x_ref, t_ref, dx_ref, l_ref):
        e = x_ref[...] - t_ref[...]
        dx_ref[...] = e * (1.0 / D_MODEL)

        @pl.when(pl.program_id(0) == 0)
        def _():
            l_ref[...] = jnp.zeros_like(l_ref)

        l_ref[...] += (0.5 / D_MODEL) * jnp.sum(e * e).reshape(1, 1)

    return _pcall(body, name=name, grid=(T // TM,),
                  in_specs=[_rows(TM, D_MODEL), _rows(TM, D_MODEL)],
                  out_specs=[_rows(TM, D_MODEL), _full((1, 1))],
                  out_shape=[_sds((T, D_MODEL), F32), _sds((1, 1), F32)])(x3, tgt)


def _ple_bwd(dx3, x2h, p, wg, bg, wp, l, xhat2, rstd2, g2, name):
    T = dx3.shape[0]

    def body(dx3_ref, x2h_ref, p_ref, wg_ref, bg_ref, wp_ref, xh_ref, rs_ref, g2_ref,
             dr_ref, drh_ref, dpre_ref, dpp_ref, dbg_ref, dg_ref, db_ref):
        dx3 = dx3_ref[...]
        gate = _sigmoid(_dot(x2h_ref[...], wg_ref[...]) + bg_ref[...])
        pp = _dot(p_ref[...].astype(BF16), wp_ref[...])
        dpre = dx3 * pp * gate * (1.0 - gate)
        dpreh = dpre.astype(BF16)
        dpre_ref[...] = dpreh
        dpp_ref[...] = (dx3 * gate).astype(BF16)
        dx2 = dx3 + _dot_nt(dpreh, wg_ref[...])
        xhat = xh_ref[...]
        dr = _ln_bwd(dx2, xhat, rs_ref[...], g2_ref[...])
        dr_ref[...] = dr
        drh_ref[...] = dr.astype(BF16)

        @pl.when(pl.program_id(0) == 0)
        def _():
            dbg_ref[...] = jnp.zeros_like(dbg_ref)
            dg_ref[...] = jnp.zeros_like(dg_ref)
            db_ref[...] = jnp.zeros_like(db_ref)

        dbg_ref[...] += _colsum(dpre)
        dg_ref[...] += _colsum(dx2 * xhat)
        db_ref[...] += _colsum(dx2)

    vec = _layer((1, D_MODEL), l)
    acc = _full((1, D_MODEL))
    big = _rows(TM, D_MODEL)
    return _pcall(body, name=name, grid=(T // TM,),
                  in_specs=[big, big, pl.BlockSpec((None, TM, D_PLE), lambda i: (l, i, 0)),
                            _layer((D_MODEL, D_MODEL), l), vec, _layer((D_PLE, D_MODEL), l),
                            big, _rows(TM, 1), vec],
                  out_specs=[big, big, big, big, acc, acc, acc],
                  out_shape=[_sds((T, D_MODEL), F32), _sds((T, D_MODEL), BF16), _sds((T, D_MODEL), BF16),
                             _sds((T, D_MODEL), BF16), _sds((1, D_MODEL), F32), _sds((1, D_MODEL), F32),
                             _sds((1, D_MODEL), F32)])(dx3, x2h, p, wg, bg, wp, xhat2, rstd2, g2)


def _ffn_bwd1(dr2h, wd4, l, hf, name):
    T = dr2h.shape[0]

    def body(d_ref, w_ref, hf_ref, o_ref):
        da = _dot_nt(d_ref[...], w_ref[...])
        g = hf_ref[0].astype(F32)
        u = hf_ref[1].astype(F32)
        sg = _sigmoid(g)
        o_ref[0] = (da * u * (sg * (1.0 + g * (1.0 - sg)))).astype(BF16)
        o_ref[1] = (da * (g * sg)).astype(BF16)

    blk = pl.BlockSpec((2, None, TM, FF_NB), lambda j, i: (0, j, i, 0))
    return _pcall(body, name=name, grid=(FF_J, T // TM),
                  in_specs=[pl.BlockSpec((TM, D_MODEL), lambda j, i: (i, 0)),
                            pl.BlockSpec((None, None, FF_NB, D_MODEL), lambda j, i: (l, j, 0, 0)), blk],
                  out_specs=blk, out_shape=_sds((2, FF_J, T, FF_NB), BF16),
                  sem=("parallel", "parallel"))(dr2h, wd4, hf)


_TM_B2 = 256


def _ffn_bwd2(dhf, wup, l, dr2, xhat1, rstd1, g1, name):
    T = dr2.shape[0]
    tm = min(T, _TM_B2)

    def body(dh_ref, w_ref, dr2_ref, xh_ref, rs_ref, g_ref, dr_ref, drh_ref, dg_ref, db_ref):
        dx1 = ALPHA * dr2_ref[...]
        for s in range(2):
            for j in range(FF_J):
                dx1 = dx1 + _dot_nt(dh_ref[s, j], w_ref[s, j])
        xhat = xh_ref[...]
        dr = _ln_bwd(dx1, xhat, rs_ref[...], g_ref[...])
        dr_ref[...] = dr
        drh_ref[...] = dr.astype(BF16)

        @pl.when(pl.program_id(0) == 0)
        def _():
            dg_ref[...] = jnp.zeros_like(dg_ref)
            db_ref[...] = jnp.zeros_like(db_ref)

        dg_ref[...] += _colsum(dx1 * xhat)
        db_ref[...] += _colsum(dx1)

    big = _rows(tm, D_MODEL)
    acc = _full((1, D_MODEL))
    return _pcall(body, name=name, grid=(T // tm,),
                  in_specs=[pl.BlockSpec((2, FF_J, tm, FF_NB), lambda i: (0, 0, i, 0)),
                            pl.BlockSpec((2, FF_J, None, D_MODEL, FF_NB), lambda i: (0, 0, l, 0, 0)),
                            big, big, _rows(tm, 1), _layer((1, D_MODEL), g1[1])],
                  out_specs=[big, big, acc, acc],
                  out_shape=[_sds((T, D_MODEL), F32), _sds((T, D_MODEL), BF16), _sds((1, D_MODEL), F32),
                             _sds((1, D_MODEL), F32)])(dhf, wup, dr2, xhat1, rstd1, g1[0])


def _out_bwd(dr1h, wo, l, name):
    T = dr1h.shape[0]

    def body(d_ref, w_ref, da_ref, db_ref):
        d = d_ref[...]
        da_ref[...] = _dot_nt(d, w_ref[0:MIX, :])
        db_ref[...] = _dot_nt(d, w_ref[MIX:, :])

    return _pcall(body, name=name, grid=(T // TM,),
                  in_specs=[_rows(TM, D_MODEL), _layer((D_MODEL, D_MODEL), l)],
                  out_specs=[_rows(TM, MIX), _rows(TM, MIX)],
                  out_shape=[_sds((T, MIX), F32), _sds((T, MIX), F32)],
                  sem=("parallel",))(dr1h, wo)


def _in_bwd(dh, win8, l, dr1, name):
    T = dr1.shape[0]

    def body(dh_ref, w_ref, dr_ref, o_ref):
        acc = ALPHA * dr_ref[...]
        for b in range(N_DEV):
            acc = acc + _dot_nt(dh_ref[:, b * IN_NB:(b + 1) * IN_NB], w_ref[b])
        o_ref[...] = acc

    return _pcall(body, name=name, grid=(T // TM,),
                  in_specs=[_rows(TM, 4 * MIX), pl.BlockSpec((N_DEV, None, D_MODEL, IN_NB), lambda i: (0, l, 0, 0)),
                            _rows(TM, D_MODEL)],
                  out_specs=_rows(TM, D_MODEL), out_shape=_sds((T, D_MODEL), F32),
                  sem=("parallel",))(dh, win8, dr1)


def _s5_operators(lre, lim, log_dt, bre, bim, cre, cim, dskip):
    G, P, H, LC = S5_GROUPS, S5_STATE, S5_GROUP, S5_LC
    dt = jnp.exp(log_dt)[:, None]
    mag = jnp.exp(lre * dt)
    ang = lim * dt
    lb_re = mag * jnp.cos(ang)
    lb_im = mag * jnp.sin(ang)
    den = lre * lre + lim * lim
    nr = lb_re - 1.0
    ni = lb_im
    r_re = (nr * lre + ni * lim) / den
    r_im = (ni * lre - nr * lim) / den
    bb_re = r_re[..., None] * bre - r_im[..., None] * bim
    bb_im = r_re[..., None] * bim + r_im[..., None] * bre
    k = jnp.arange(LC + 1, dtype=F32)[:, None, None]
    pmag = jnp.exp(k * (lre * dt)[None])
    pang = k * ang[None]
    pw_re = pmag * jnp.cos(pang)
    pw_im = pmag * jnp.sin(pang)
    cp_re = cre[None] * pw_re[:, :, None, :] - cim[None] * pw_im[:, :, None, :]
    cp_im = cre[None] * pw_im[:, :, None, :] + cim[None] * pw_re[:, :, None, :]
    kk = (jnp.einsum('kghp,gpj->kghj', cp_re[:LC], bb_re, precision=HI)
          - jnp.einsum('kghp,gpj->kghj', cp_im[:LC], bb_im, precision=HI))
    dmat = dskip.reshape(G, H)[:, :, None] * jnp.eye(H, dtype=F32)[None]
    kk = jnp.concatenate([kk[:1] + dmat[None], kk[1:]], axis=0)
    r0 = jnp.transpose(kk, (1, 3, 0, 2)).reshape(G, H, LC * H)
    qt = jnp.stack([cp_re[1:], -cp_im[1:]], axis=0)
    qt = jnp.transpose(qt, (2, 0, 4, 1, 3)).reshape(G, 2 * P, LC * H)
    pb_re = pw_re[:LC, :, :, None] * bb_re[None] - pw_im[:LC, :, :, None] * bb_im[None]
    pb_im = pw_re[:LC, :, :, None] * bb_im[None] + pw_im[:LC, :, :, None] * bb_re[None]
    pm = jnp.stack([pb_re[::-1], pb_im[::-1]], axis=0)
    pm = jnp.transpose(pm, (2, 1, 4, 0, 3)).reshape(G, LC * H, 2 * P)
    a_re = pw_re[LC]
    a_im = pw_im[LC]
    a1 = jnp.concatenate([a_re, a_re], axis=-1)
    a2 = jnp.concatenate([-a_im, a_im], axis=-1)
    return r0, qt, pm, a1, a2


def _to_chunks(u):
    T = u.shape[0]
    return jnp.transpose(u.reshape(T // S5_LC, S5_LC, S5_GROUPS, S5_GROUP), (2, 0, 1, 3)).reshape(
        S5_GROUPS, T // S5_LC, S5_LW)


def _from_chunks(m):
    nc = m.shape[1]
    return jnp.transpose(m.reshape(S5_GROUPS, nc, S5_LC, S5_GROUP), (1, 2, 0, 3)).reshape(nc * S5_LC, MIX)


def _gspec(r, c):
    return pl.BlockSpec((None, r, c), lambda g: (g, 0, 0))


def _s5_chunk_state(umat, pm, name):
    G, nc, _ = umat.shape

    def body(u_ref, p_ref, o_ref):
        o_ref[...] = _dot(u_ref[...], p_ref[...], HI)

    return _pcall(body, name=name, grid=(G,), in_specs=[_gspec(nc, S5_LW), _gspec(S5_LW, 128)],
                  out_specs=_gspec(nc, 128), out_shape=_sds((G, nc, 128), F32), sem=("parallel",))(umat, pm)


_SCAN_G = 8
_SCAN_UNROLL = 8


def _s5_scan_fwd(s_t, a1, a2, name):
    nc, G, _ = s_t.shape

    def body(s_ref, a1_ref, a2_ref, o_ref, sb_ref):
        sb_ref[...] = pltpu.roll(s_ref[...], 64, 2)
        a1v = a1_ref[...]
        a2v = a2_ref[...]

        def step(c, carry):
            x, xb = carry
            o_ref[c] = x
            return a1v * x + a2v * xb + s_ref[c], a1v * xb - a2v * x + sb_ref[c]

        zero = jnp.zeros((_SCAN_G, 128), F32)
        lax.fori_loop(0, nc, step, (zero, zero), unroll=_SCAN_UNROLL)

    blk = pl.BlockSpec((nc, _SCAN_G, 128), lambda g: (0, g, 0))
    vec = pl.BlockSpec((_SCAN_G, 128), lambda g: (g, 0))
    return _pcall(body, name=name, grid=(G // _SCAN_G,), in_specs=[blk, vec, vec], out_specs=blk,
                  out_shape=_sds((nc, G, 128), F32), scratch=[pltpu.VMEM((nc, _SCAN_G, 128), F32)],
                  sem=("parallel",))(s_t, a1, a2)


def _s5_scan_bwd(dxp_t, xp_t, a1, a2, name):
    nc, G, _ = dxp_t.shape

    def body(dx_ref, x_ref, a1_ref, a2_ref, ds_ref, da1_ref, da2_ref, dxb_ref, xb_ref):
        dxb_ref[...] = pltpu.roll(dx_ref[...], 64, 2)
        xb_ref[...] = pltpu.roll(x_ref[...], 64, 2)
        a1v = a1_ref[...]
        a2v = a2_ref[...]
        zero = jnp.zeros((_SCAN_G, 128), F32)

        def step(n, carry):
            gc, gb, d1, d2 = carry
            c = nc - 1 - n
            ds_ref[c] = gc
            d1 = d1 + gc * x_ref[c]
            d2 = d2 + gc * xb_ref[c]
            return dx_ref[c] + a1v * gc - a2v * gb, dxb_ref[c] + a1v * gb + a2v * gc, d1, d2

        _, _, d1, d2 = lax.fori_loop(0, nc, step, (zero, zero, zero, zero), unroll=_SCAN_UNROLL)
        da1_ref[...] = d1
        da2_ref[...] = d2

    blk = pl.BlockSpec((nc, _SCAN_G, 128), lambda g: (0, g, 0))
    vec = pl.BlockSpec((_SCAN_G, 128), lambda g: (g, 0))
    return _pcall(body, name=name, grid=(G // _SCAN_G,), in_specs=[blk, blk, vec, vec],
                  out_specs=[blk, vec, vec],
                  out_shape=[_sds((nc, G, 128), F32), _sds((G, 128), F32), _sds((G, 128), F32)],
                  scratch=[pltpu.VMEM((nc, _SCAN_G, 128), F32)] * 2, sem=("parallel",))(dxp_t, xp_t, a1, a2)


def _toeplitz_rows(r0, mt):
    lane = lax.broadcasted_iota(jnp.int32, (S5_GROUP, S5_LW), 1)
    mt[0:S5_GROUP, :] = r0
    for s in range(1, S5_LC):
        mt[s * S5_GROUP:(s + 1) * S5_GROUP, :] = jnp.where(lane >= s * S5_GROUP, pltpu.roll(r0, s * S5_GROUP, 1), 0.0)


def _toeplitz_rows_t(dmt):
    lane = lax.broadcasted_iota(jnp.int32, (S5_GROUP, S5_LW), 1)
    acc = dmt[0:S5_GROUP, :]
    for s in range(1, S5_LC):
        blk = dmt[s * S5_GROUP:(s + 1) * S5_GROUP, :]
        acc = acc + jnp.where(lane < S5_LW - s * S5_GROUP, pltpu.roll(blk, S5_LW - s * S5_GROUP, 1), 0.0)
    return acc


def _s5_output(umat, xprev, r0, qt, name):
    G, nc, _ = umat.shape

    def body(u_ref, x_ref, r_ref, q_ref, o_ref, mt):
        _toeplitz_rows(r_ref[...], mt)
        o_ref[...] = _dot(u_ref[...], mt[...], HI) + _dot(x_ref[...], q_ref[...], HI)

    return _pcall(body, name=name, grid=(G,),
                  in_specs=[_gspec(nc, S5_LW), _gspec(nc, 128), _gspec(S5_GROUP, S5_LW), _gspec(128, S5_LW)],
                  out_specs=_gspec(nc, S5_LW), out_shape=_sds((G, nc, S5_LW), F32),
                  scratch=[pltpu.VMEM((S5_LW, S5_LW), F32)], sem=("parallel",))(umat, xprev, r0, qt)


def _s5_bwd_state(dymat, qt, name):
    G, nc, _ = dymat.shape

    def body(d_ref, q_ref, o_ref):
        o_ref[...] = _dot_nt(d_ref[...], q_ref[...], HI)

    return _pcall(body, name=name, grid=(G,), in_specs=[_gspec(nc, S5_LW), _gspec(128, S5_LW)],
                  out_specs=_gspec(nc, 128), out_shape=_sds((G, nc, 128), F32), sem=("parallel",))(dymat, qt)


def _s5_bwd_main(umat, xprev, dymat, ds, r0, pm, name):
    G, nc, _ = umat.shape

    def body(u_ref, x_ref, dy_ref, ds_ref, r_ref, p_ref, du_ref, dr_ref, dq_ref, dp_ref, mt):
        u = u_ref[...]
        dy = dy_ref[...]
        dsv = ds_ref[...]
        _toeplitz_rows(r_ref[...], mt)
        du_ref[...] = (_dot_nt(dy, mt[...], HI) + _dot_nt(dsv, p_ref[...], HI)).astype(BF16)
        dq_ref[...] = _dot_tn(x_ref[...], dy, HI)
        dp_ref[...] = _dot_tn(u, dsv, HI)
        mt[...] = _dot_tn(u, dy, HI)
        dr_ref[...] = _toeplitz_rows_t(mt)

    return _pcall(body, name=name, grid=(G,),
                  in_specs=[_gspec(nc, S5_LW), _gspec(nc, 128), _gspec(nc, S5_LW), _gspec(nc, 128),
                            _gspec(S5_GROUP, S5_LW), _gspec(S5_LW, 128)],
                  out_specs=[_gspec(nc, S5_LW), _gspec(S5_GROUP, S5_LW), _gspec(128, S5_LW), _gspec(S5_LW, 128)],
                  out_shape=[_sds((G, nc, S5_LW), BF16), _sds((G, S5_GROUP, S5_LW), F32), _sds((G, 128, S5_LW), F32),
                             _sds((G, S5_LW, 128), F32)],
                  scratch=[pltpu.VMEM((S5_LW, S5_LW), F32)], sem=("parallel",))(umat, xprev, dymat, ds, r0, pm)


def _glu_fwd(y, wglu, bglu, l, name):
    T = y.shape[0]

    def body(y_ref, w_ref, b_ref, o_ref, g_ref):
        g = _gelu(y_ref[...])
        gh = g.astype(BF16)
        z = _dot(gh, w_ref[...]) + b_ref[...]
        o_ref[...] = (g * _sigmoid(z)).astype(BF16)
        g_ref[...] = gh

    return _pcall(body, name=name, grid=(T // TM,),
                  in_specs=[_rows(TM, MIX), _layer((MIX, MIX), l), _layer((1, MIX), l)],
                  out_specs=[_rows(TM, MIX), _rows(TM, MIX)],
                  out_shape=[_sds((T, MIX), BF16), _sds((T, MIX), BF16)], sem=("parallel",))(y, wglu, bglu)


def _glu_bwd(y, dout, wglu, bglu, l, name):
    T = y.shape[0]

    def body(y_ref, do_ref, w_ref, b_ref, dy_ref, dz_ref, db_ref):
        yv = y_ref[...]
        do = do_ref[...]
        g = _gelu(yv)
        s = _sigmoid(_dot(g.astype(BF16), w_ref[...]) + b_ref[...])
        dz = do * g * s * (1.0 - s)
        dzh = dz.astype(BF16)
        dz_ref[...] = dzh
        dg = do * s + _dot_nt(dzh, w_ref[...])
        dy_ref[...] = dg * _gelu_grad(yv)

        @pl.when(pl.program_id(0) == 0)
        def _():
            db_ref[...] = jnp.zeros_like(db_ref)

        db_ref[...] += _colsum(dz)

    return _pcall(body, name=name, grid=(T // TM,),
                  in_specs=[_rows(TM, MIX), _rows(TM, MIX), _layer((MIX, MIX), l), _layer((1, MIX), l)],
                  out_specs=[_rows(TM, MIX), _rows(TM, MIX), _full((1, MIX))],
                  out_shape=[_sds((T, MIX), F32), _sds((T, MIX), BF16), _sds((1, MIX), F32)])(y, dout, wglu, bglu)


def _prev_rows(T, h, s=0):
    return pl.BlockSpec((h, MIX), lambda i: (jnp.maximum(i * (TM // h) - 1, 0), s))


def _next_rows(T, h, s=0):
    return pl.BlockSpec((h, MIX), lambda i: (jnp.minimum((i + 1) * (TM // h), T // h - 1), s))


def _conv_fwd(h, w, l, name):
    T = h.shape[0]

    def body(b_ref, c_ref, x_ref, ch_ref, xh_ref, w_ref, o_ref, ext):
        i = pl.program_id(0)
        z = c_ref[...] * x_ref[...]
        ext[0:8, :] = jnp.where(i > 0, ch_ref[...] * xh_ref[...], 0.0)
        ext[8:, :] = z
        y = (w_ref[0:1, :] * ext[pl.ds(6, TM), :] + w_ref[1:2, :] * ext[pl.ds(7, TM), :] + w_ref[2:3, :] * z)
        o_ref[...] = (b_ref[...] * y).astype(BF16)

    return _pcall(body, name=name, grid=(T // TM,),
                  in_specs=[_cols(TM, 1), _cols(TM, 2), _cols(TM, 3), _prev_rows(T, 8, 2), _prev_rows(T, 8, 3),
                            _layer((3, MIX), l)],
                  out_specs=_rows(TM, MIX), out_shape=_sds((T, MIX), BF16),
                  scratch=[pltpu.VMEM((TM + 8, MIX), F32)], sem=("parallel",))(h, h, h, h, h, w)


def _conv_bwd(dout, h, dua, w, l, name):
    T = h.shape[0]
    nb = T // TM

    def body(do_ref, b_ref, c_ref, x_ref, ch_ref, xh_ref, don_ref, bn_ref, du_ref, w_ref,
             dh_ref, dw_ref, ext, ext2):
        i = pl.program_id(0)
        c = c_ref[...]
        x = x_ref[...]
        z = c * x
        ext[0:8, :] = jnp.where(i > 0, ch_ref[...] * xh_ref[...], 0.0)
        ext[8:, :] = z
        zm2 = ext[pl.ds(6, TM), :]
        zm1 = ext[pl.ds(7, TM), :]
        w0 = w_ref[0:1, :]
        w1 = w_ref[1:2, :]
        w2 = w_ref[2:3, :]
        y = w0 * zm2 + w1 * zm1 + w2 * z
        do = do_ref[...]
        dy = do * b_ref[...]
        ext2[0:TM, :] = dy
        ext2[TM:, :] = jnp.where(i < nb - 1, don_ref[...] * bn_ref[...], 0.0)
        dz = w2 * dy + w1 * ext2[pl.ds(1, TM), :] + w0 * ext2[pl.ds(2, TM), :]
        dh_ref[:, 0:MIX] = du_ref[...]
        dh_ref[:, MIX:2 * MIX] = (do * y).astype(BF16)
        dh_ref[:, 2 * MIX:3 * MIX] = (dz * x).astype(BF16)
        dh_ref[:, 3 * MIX:] = (dz * c).astype(BF16)

        @pl.when(i == 0)
        def _():
            dw_ref[...] = jnp.zeros_like(dw_ref)

        dw_ref[0:1, :] += _colsum(dy * zm2)
        dw_ref[1:2, :] += _colsum(dy * zm1)
        dw_ref[2:3, :] += _colsum(dy * z)

    return _pcall(body, name=name, grid=(nb,),
                  in_specs=[_rows(TM, MIX), _cols(TM, 1), _cols(TM, 2), _cols(TM, 3), _prev_rows(T, 8, 2),
                            _prev_rows(T, 8, 3), _next_rows(T, 8), _next_rows(T, 8, 1), _rows(TM, MIX),
                            _layer((3, MIX), l)],
                  out_specs=[_rows(TM, 4 * MIX), _full((8, MIX))],
                  out_shape=[_sds((T, 4 * MIX), BF16), _sds((8, MIX), F32)],
                  scratch=[pltpu.VMEM((TM + 8, MIX), F32), pltpu.VMEM((TM + 8, MIX), F32)])(
                      dout, h, h, h, h, h, dout, h, dua, w)


_PH = 16


def _pooled(ext, t, gi, w):
    lo = gi * POOL_GROUP
    cur = ext[pl.ds(_PH, TM), lo:lo + POOL_GROUP]
    acc = cur
    for k in range(1, w):
        acc = acc + ext[pl.ds(_PH - k, TM), lo:lo + POOL_GROUP]
    cnt = jnp.minimum(t + 1, w).astype(F32)
    return acc / cnt - cur, cnt


def _pool_fwd(h, pw, scale, l, name):
    T = h.shape[0]

    def body(z_ref, zh_ref, pw_ref, sc_ref, o_ref, ext):
        i = pl.program_id(0)
        ext[0:_PH, :] = jnp.where(i > 0, zh_ref[...], 0.0)
        ext[_PH:, :] = z_ref[...]
        t = i * TM + lax.broadcasted_iota(jnp.int32, (TM, 1), 0)
        for gi, w in enumerate(POOL_WINDOWS):
            lo = gi * POOL_GROUP
            pooled, _ = _pooled(ext, t, gi, w)
            mixed = _dot(pooled.astype(BF16), pw_ref[gi].astype(BF16))
            o_ref[:, lo:lo + POOL_GROUP] = (mixed * sc_ref[:, lo:lo + POOL_GROUP]).astype(BF16)

    return _pcall(body, name=name, grid=(T // TM,),
                  in_specs=[_cols(TM, 3), _prev_rows(T, _PH, 3), _layer((4, POOL_GROUP, POOL_GROUP), l),
                            _layer((1, MIX), l)],
                  out_specs=_rows(TM, MIX), out_shape=_sds((T, MIX), BF16),
                  scratch=[pltpu.VMEM((TM + _PH, MIX), F32)], sem=("parallel",))(h, h, pw, scale)


def _pool_bwd(dout, h, dq, dk, dv, pw, scale, l, name):
    T = h.shape[0]
    nb = T // TM

    def body(do_ref, don_ref, z_ref, zh_ref, dq_ref, dk_ref, dv_ref, pw_ref, sc_ref,
             dh_ref, dpw_ref, dsc_ref, ext, ext2):
        i = pl.program_id(0)
        ext[0:_PH, :] = jnp.where(i > 0, zh_ref[...], 0.0)
        ext[_PH:, :] = z_ref[...]
        t = i * TM + lax.broadcasted_iota(jnp.int32, (TM, 1), 0)
        dh_ref[:, 0:MIX] = dq_ref[...]
        dh_ref[:, MIX:2 * MIX] = dk_ref[...]
        dh_ref[:, 2 * MIX:3 * MIX] = dv_ref[...]

        @pl.when(i == 0)
        def _():
            dpw_ref[...] = jnp.zeros_like(dpw_ref)
            dsc_ref[...] = jnp.zeros_like(dsc_ref)

        for gi, w in enumerate(POOL_WINDOWS):
            lo = gi * POOL_GROUP
            pwb = pw_ref[gi].astype(BF16)
            sc = sc_ref[:, lo:lo + POOL_GROUP]
            pooled, cnt = _pooled(ext, t, gi, w)
            pb = pooled.astype(BF16)
            mixed = _dot(pb, pwb)
            dog = do_ref[:, lo:lo + POOL_GROUP]
            dsc_ref[:, lo:lo + POOL_GROUP] += _colsum(dog * mixed)
            dmix = (dog * sc).astype(BF16)
            dpw_ref[gi] += _dot_tn(pb, dmix)
            dpool = _dot_nt(dmix, pwb)
            dmix_n = (jnp.where(i < nb - 1, don_ref[:, lo:lo + POOL_GROUP], 0.0) * sc).astype(BF16)
            dpool_n = _dot_nt(dmix_n, pwb)
            e = dpool / cnt
            ext2[0:TM, lo:lo + POOL_GROUP] = e
            ext2[TM:, lo:lo + POOL_GROUP] = dpool_n * (1.0 / w)
            s = e
            for k in range(1, w):
                s = s + ext2[pl.ds(k, TM), lo:lo + POOL_GROUP]
            dh_ref[:, 3 * MIX + lo:3 * MIX + lo + POOL_GROUP] = (s - dpool).astype(BF16)

    blk = _rows(TM, MIX)
    return _pcall(body, name=name, grid=(nb,),
                  in_specs=[blk, _next_rows(T, _PH), _cols(TM, 3), _prev_rows(T, _PH, 3), blk, blk, blk,
                            _layer((4, POOL_GROUP, POOL_GROUP), l), _layer((1, MIX), l)],
                  out_specs=[_rows(TM, 4 * MIX), _full((4, POOL_GROUP, POOL_GROUP)), _full((1, MIX))],
                  out_shape=[_sds((T, 4 * MIX), BF16), _sds((4, POOL_GROUP, POOL_GROUP), F32), _sds((1, MIX), F32)],
                  scratch=[pltpu.VMEM((TM + _PH, MIX), F32), pltpu.VMEM((TM + _PH, MIX), F32)])(
                      dout, dout, h, h, dq, dk, dv, pw, scale)


def _att_bias_table(rel_bias):
    span = LEFT_CHUNKS * CHUNK
    assert ATT_TQ - 1 <= MAX_REL
    lo = MAX_REL - (ATT_TQ - 1)
    near = rel_bias[:, lo:2 * MAX_REL + 1]
    far = jnp.broadcast_to(rel_bias[:, 2 * MAX_REL:], (HEADS, span + ATT_TQ - 1 - MAX_REL))
    by_dist = jnp.concatenate([near, far], axis=1)
    rev = by_dist[:, ::-1]
    lv = rev.shape[1]
    skew = jnp.tile(rev, (1, ATT_TQ + 1))[:, :ATT_TQ * (lv + 1)].reshape(HEADS, ATT_TQ, lv + 1)[:, :, :ATT_W]
    bias = skew[:, ::-1, :]
    r = jnp.arange(ATT_TQ)[:, None]
    col = jnp.arange(ATT_W)[None, :]
    dchunk = (LEFT_CHUNKS + r // CHUNK) - col // CHUNK
    visible = (dchunk >= 0) & (dchunk <= LEFT_CHUNKS)
    return jnp.where(visible[None], bias, NEG_INF)


def _qrows(n, s=0):
    return pl.BlockSpec((ATT_TQ, n), lambda i: (i, s))


def _att_views_back(d, s):
    return pl.BlockSpec((ATT_TQ, MIX), lambda i: (jnp.maximum(i - (ATT_NV - 1) + d, 0), s))


_PAIR = 2 * HEAD_DIM


def _att_pair(refs, pp):
    sl = slice(pp * _PAIR, (pp + 1) * _PAIR)
    if isinstance(refs, (tuple, list)):
        return jnp.concatenate([r[:, sl] for r in refs], axis=0)
    return refs[:, sl]


def _att_fwd(h, table, name):
    T = h.shape[0]

    def body(q_ref, *refs):
        k_refs = refs[:ATT_NV]
        v_refs = refs[ATT_NV:2 * ATT_NV]
        tb_ref = refs[2 * ATT_NV]
        o_ref, oh_ref, lse_ref = refs[2 * ATT_NV + 1:]
        i = pl.program_id(0)
        col = lax.broadcasted_iota(jnp.int32, (1, ATT_W), 1)
        kvalid = (col + (i - (ATT_NV - 1)) * ATT_TQ) >= 0
        first = lax.broadcasted_iota(jnp.int32, (1, _PAIR), 1) < HEAD_DIM
        lses = []
        for pp in range(HEADS // 2):
            qp = _att_pair(q_ref, pp) * (HEAD_DIM ** -0.5)
            kp = _att_pair(k_refs, pp).astype(BF16)
            vp = _att_pair(v_refs, pp).astype(BF16)
            outs = []
            for e in range(2):
                half = first if e == 0 else jnp.logical_not(first)
                s = _dot_nt(jnp.where(half, qp, 0.0).astype(BF16), kp) + tb_ref[2 * pp + e]
                s = jnp.where(kvalid, s, NEG_INF)
                m = jnp.max(s, axis=-1, keepdims=True)
                p = jnp.exp(s - m)
                l = jnp.sum(p, axis=-1, keepdims=True)
                outs.append(_dot(p.astype(BF16), vp) / l)
                lses.append(m + jnp.log(l))
            o = jnp.where(first, outs[0], outs[1])
            o_ref[:, pp * _PAIR:(pp + 1) * _PAIR] = o
            oh_ref[:, pp * _PAIR:(pp + 1) * _PAIR] = o.astype(BF16)
        lse_ref[...] = jnp.concatenate(lses, axis=1)

    kviews = [_att_views_back(d, 1) for d in range(ATT_NV)]
    vviews = [_att_views_back(d, 2) for d in range(ATT_NV)]
    return _pcall(body, name=name, grid=(T // ATT_TQ,),
                  in_specs=[_qrows(MIX)] + kviews + vviews + [_full((HEADS, ATT_TQ, ATT_W))],
                  out_specs=[_qrows(MIX), _qrows(MIX), _qrows(HEADS)],
                  out_shape=[_sds((T, MIX), F32), _sds((T, MIX), BF16), _sds((T, HEADS), F32)],
                  sem=("parallel",))(h, *([h] * (2 * ATT_NV)), table)


def _att_bwd_q(h, do, o, lse, table, name):
    T = h.shape[0]

    def body(q_ref, *refs):
        k_refs = refs[:ATT_NV]
        v_refs = refs[ATT_NV:2 * ATT_NV]
        do_ref, o_ref, lse_ref, tb_ref, dq_ref, dl_ref, dtb_ref = refs[2 * ATT_NV:]
        i = pl.program_id(0)
        col = lax.broadcasted_iota(jnp.int32, (1, ATT_W), 1)
        kvalid = (col + (i - (ATT_NV - 1)) * ATT_TQ) >= 0

        first = lax.broadcasted_iota(jnp.int32, (1, _PAIR), 1) < HEAD_DIM

        @pl.when(i == 0)
        def _():
            dtb_ref[...] = jnp.zeros_like(dtb_ref)

        deltas = []
        for pp in range(HEADS // 2):
            qp = _att_pair(q_ref, pp) * (HEAD_DIM ** -0.5)
            kp = _att_pair(k_refs, pp).astype(BF16)
            vp = _att_pair(v_refs, pp).astype(BF16)
            dop = _att_pair(do_ref, pp)
            doo = dop * _att_pair(o_ref, pp)
            outs = []
            for e in range(2):
                hd = 2 * pp + e
                half = first if e == 0 else jnp.logical_not(first)
                s = _dot_nt(jnp.where(half, qp, 0.0).astype(BF16), kp) + tb_ref[hd]
                s = jnp.where(kvalid, s, NEG_INF)
                p = jnp.exp(s - lse_ref[:, hd:hd + 1])
                delta = jnp.sum(jnp.where(half, doo, 0.0), axis=-1, keepdims=True)
                dp = _dot_nt(jnp.where(half, dop, 0.0).astype(BF16), vp)
                ds = p * (dp - delta)
                dtb_ref[hd] += ds
                outs.append(_dot(ds.astype(BF16), kp))
                deltas.append(delta)
            dq = jnp.where(first, outs[0], outs[1]) * (HEAD_DIM ** -0.5)
            dq_ref[:, pp * _PAIR:(pp + 1) * _PAIR] = dq.astype(BF16)
        dl_ref[...] = jnp.concatenate(deltas, axis=1)

    kviews = [_att_views_back(d, 1) for d in range(ATT_NV)]
    vviews = [_att_views_back(d, 2) for d in range(ATT_NV)]
    tb = _full((HEADS, ATT_TQ, ATT_W))
    return _pcall(body, name=name, grid=(T // ATT_TQ,),
                  in_specs=[_qrows(MIX)] + kviews + vviews + [_qrows(MIX), _qrows(MIX), _qrows(HEADS), tb],
                  out_specs=[_qrows(MIX), _qrows(HEADS), tb],
                  out_shape=[_sds((T, MIX), BF16), _sds((T, HEADS), F32), _sds((HEADS, ATT_TQ, ATT_W), F32)])(
                      h, *([h] * (2 * ATT_NV)), do, o, lse, table)


def _att_table_by_key(table):
    t = table.reshape(HEADS, ATT_TQ, ATT_NV, ATT_TQ)[:, :, ::-1, :]
    return jnp.transpose(t, (0, 3, 2, 1)).reshape(HEADS, ATT_TQ, ATT_W)


def _att_bwd_kv(h, do, lse_t, delta_t, table_k, name):
    T = h.shape[0]
    nb = T // ATT_TQ

    def fwd_view(d, s=0):
        return pl.BlockSpec((ATT_TQ, MIX), lambda j: (jnp.minimum(j + d, nb - 1), s))

    def row_view(d):
        return pl.BlockSpec((HEADS, ATT_TQ), lambda j: (0, jnp.minimum(j + d, nb - 1)))

    def body(k_ref, v_ref, *refs):
        q_refs = refs[:ATT_NV]
        do_refs = refs[ATT_NV:2 * ATT_NV]
        lse_refs = refs[2 * ATT_NV:3 * ATT_NV]
        dl_refs = refs[3 * ATT_NV:4 * ATT_NV]
        tb_ref, dk_ref, dv_ref = refs[4 * ATT_NV:]
        j = pl.program_id(0)
        view = lax.broadcasted_iota(jnp.int32, (1, ATT_W), 1) // ATT_TQ
        valid = (j + view) <= nb - 1
        first = lax.broadcasted_iota(jnp.int32, (1, _PAIR), 1) < HEAD_DIM
        for pp in range(HEADS // 2):
            kp = _att_pair(k_ref, pp)
            vp = _att_pair(v_ref, pp)
            qs = (_att_pair(q_refs, pp) * (HEAD_DIM ** -0.5)).astype(BF16)
            dos = _att_pair(do_refs, pp).astype(BF16)
            dks, dvs = [], []
            for e in range(2):
                hd = 2 * pp + e
                half = first if e == 0 else jnp.logical_not(first)
                lses = jnp.concatenate([r[hd:hd + 1, :] for r in lse_refs], axis=1)
                dls = jnp.concatenate([r[hd:hd + 1, :] for r in dl_refs], axis=1)
                st = _dot_nt(jnp.where(half, kp, 0.0).astype(BF16), qs) + tb_ref[hd]
                pt = jnp.where(valid, jnp.exp(st - lses), 0.0)
                dvs.append(_dot(pt.astype(BF16), dos))
                dst = pt * (_dot_nt(jnp.where(half, vp, 0.0).astype(BF16), dos) - dls)
                dks.append(_dot(dst.astype(BF16), qs))
            dk_ref[:, pp * _PAIR:(pp + 1) * _PAIR] = jnp.where(first, dks[0], dks[1]).astype(BF16)
            dv_ref[:, pp * _PAIR:(pp + 1) * _PAIR] = jnp.where(first, dvs[0], dvs[1]).astype(BF16)

    qv = [fwd_view(d, 0) for d in range(ATT_NV)]
    dov = [fwd_view(d) for d in range(ATT_NV)]
    rows = [row_view(d) for d in range(ATT_NV)]
    return _pcall(body, name=name, grid=(nb,),
                  in_specs=[_qrows(MIX, 1), _qrows(MIX, 2)] + qv + dov + rows + rows
                  + [_full((HEADS, ATT_TQ, ATT_W))],
                  out_specs=[_qrows(MIX), _qrows(MIX)], out_shape=[_sds((T, MIX), BF16), _sds((T, MIX), BF16)],
                  sem=("parallel",))(h, h, *([h] * ATT_NV), *([do] * ATT_NV), *([lse_t] * ATT_NV),
                                     *([delta_t] * ATT_NV), table_k)


_MESH = pl.DeviceIdType.MESH
_ANY = pl.BlockSpec(memory_space=pl.ANY)


def _all_gather(x, name):
    R, C = x.shape

    def body(x_ref, out_ref, send_sems, recv_sems, local_sem):
        xi, yi, ci = lax.axis_index("x"), lax.axis_index("y"), lax.axis_index("c")
        me, sibling = (xi, yi, ci), (xi, yi, 1 - ci)
        chips = [(1 - xi, yi), (xi, 1 - yi), (1 - xi, 1 - yi)]

        def slot(px, py, pc):
            return out_ref.at[4 * px + 2 * py + pc]

        def copy(k, block, to, src=None):
            return pltpu.make_async_remote_copy(
                src_ref=slot(*block) if src is None else src, dst_ref=slot(*block),
                send_sem=send_sems.at[k], recv_sem=recv_sems.at[k], device_id=to, device_id_type=_MESH)

        mine = pltpu.make_async_copy(x_ref, slot(*me), local_sem)
        mine.start()
        first = [copy(0, me, sibling, src=x_ref)]
        first += [copy(1 + j, me, (*chip, ci), src=x_ref) for j, chip in enumerate(chips)]
        for cp in first:
            cp.start()
        passed = [copy(4 + j, (*chip, ci), sibling) for j, chip in enumerate(chips)]
        for j, chip in enumerate(chips):
            copy(1 + j, (*chip, ci), me).wait_recv()
            passed[j].start()
        copy(0, sibling, me).wait_recv()
        for j, chip in enumerate(chips):
            copy(4 + j, (*chip, 1 - ci), me).wait_recv()
        for cp in first + passed:
            cp.wait_send()
        mine.wait()

    return pl.pallas_call(
        body, name=name, out_shape=_sds((N_DEV, R, C), x.dtype), in_specs=[_ANY], out_specs=_ANY,
        scratch_shapes=[pltpu.SemaphoreType.DMA((7,)), pltpu.SemaphoreType.DMA((7,)), pltpu.SemaphoreType.DMA(())],
    )(x)


def _all_to_all(s, name):
    _, R, C = s.shape

    def body(s_ref, r_ref, send_sems, recv_sems, local_sem):
        xi, yi, ci = lax.axis_index("x"), lax.axis_index("y"), lax.axis_index("c")
        me = 4 * xi + 2 * yi + ci
        mine = pltpu.make_async_copy(s_ref.at[me], r_ref.at[me], local_sem)
        mine.start()
        copies = []
        for m in range(1, N_DEV):
            px = 1 - xi if m & 4 else xi
            py = 1 - yi if m & 2 else yi
            pc = 1 - ci if m & 1 else ci
            peer = 4 * px + 2 * py + pc
            copies.append((
                pltpu.make_async_remote_copy(src_ref=s_ref.at[peer], dst_ref=r_ref.at[me], send_sem=send_sems.at[m - 1],
                                             recv_sem=recv_sems.at[m - 1], device_id=(px, py, pc), device_id_type=_MESH),
                pltpu.make_async_remote_copy(src_ref=s_ref.at[me], dst_ref=r_ref.at[peer], send_sem=send_sems.at[m - 1],
                                             recv_sem=recv_sems.at[m - 1], device_id=(px, py, pc), device_id_type=_MESH)))
        for send, _ in copies:
            send.start()
        for _, recv in copies:
            recv.wait_recv()
        for send, _ in copies:
            send.wait_send()
        mine.wait()

    return pl.pallas_call(
        body, name=name, out_shape=_sds(s.shape, s.dtype), in_specs=[_ANY], out_specs=_ANY,
        scratch_shapes=[pltpu.SemaphoreType.DMA((7,)), pltpu.SemaphoreType.DMA((7,)), pltpu.SemaphoreType.DMA(())],
    )(s)


_ADAMW_BLOCK_BYTES = 6 * 1024 * 1024


def _adamw(parts, row_off, w, m, v, name):
    R, C = w.shape
    tr = None
    for cand in (512, 256, 128, 64, 32, 16):
        step_bytes = cand * C * (N_DEV * parts.dtype.itemsize + 7 * 4)
        if R % cand == 0 and row_off % cand == 0 and step_bytes <= _ADAMW_BLOCK_BYTES:
            tr = cand
            break
    assert tr is not None, (R, C, row_off)
    off = row_off // tr
    c1 = 1.0 - ADAM_B1 ** ADAM_STEP
    c2 = 1.0 - ADAM_B2 ** ADAM_STEP

    def body(p_ref, w_ref, m_ref, v_ref, g_ref, d_ref, mo_ref, vo_ref):
        g = p_ref[0].astype(F32)
        for j in range(1, N_DEV):
            g = g + p_ref[j].astype(F32)
        mn = ADAM_B1 * m_ref[...] + (1.0 - ADAM_B1) * g
        vn = ADAM_B2 * v_ref[...] + (1.0 - ADAM_B2) * (g * g)
        m_hat = mn / c1
        v_hat = vn / c2
        g_ref[...] = g
        d_ref[...] = -ADAM_LR * (m_hat / (jnp.sqrt(v_hat) + ADAM_EPS) + ADAM_WD * w_ref[...])
        mo_ref[...] = mn
        vo_ref[...] = vn

    blk = _rows(tr, C)
    return _pcall(body, name=name, grid=(R // tr,),
                  in_specs=[pl.BlockSpec((N_DEV, tr, C), lambda i: (0, off + i, 0)), blk, blk, blk],
                  out_specs=[blk] * 4, out_shape=[_sds((R, C), F32)] * 4,
                  sem=("parallel",))(parts, w, m, v)


def _piece_rows(size):
    return -(-size // PIECE) * 16


def _pack_rows(arrays, lead=()):
    parts = []
    total = 0
    for a in arrays:
        flat = a.reshape(lead + (-1,))
        size = flat.shape[-1]
        rows = _piece_rows(size)
        pad = [(0, 0)] * len(lead) + [(0, rows * FLAT_COLS - size)]
        parts.append(jnp.pad(flat, pad).reshape(lead + (rows, FLAT_COLS)))
        total += rows
    tail = -total % FLAT_ROWS
    if tail:
        parts.append(jnp.zeros(lead + (tail, FLAT_COLS), parts[0].dtype))
    return jnp.concatenate(parts, axis=len(lead))


def _unpack_rows(buf, shapes, lead=()):
    out = []
    row = 0
    nl = len(lead)
    for shape in shapes:
        size = math.prod(shape)
        rows = _piece_rows(size)
        piece = lax.slice_in_dim(buf, row, row + rows, axis=nl).reshape(lead + (rows * FLAT_COLS,))
        out.append(lax.slice_in_dim(piece, 0, size, axis=nl).reshape(lead + tuple(shape)))
        row += rows
    return out


def _to_blocks(full, axis):
    shp = full.shape
    split = full.reshape(shp[:axis] + (N_DEV, shp[axis] // N_DEV) + shp[axis + 1:])
    return jnp.moveaxis(split, axis, 0)


def _from_blocks(blocks, axis):
    shp = blocks.shape[1:]
    moved = jnp.moveaxis(blocks, 0, axis)
    return moved.reshape(shp[:axis] + (N_DEV * shp[axis],) + shp[axis + 1:])


def _local_step(x, p, tgt, W):
    wup = W['ffn_w_up'].reshape(2, FF_J, DEPTH, D_MODEL, FF_NB)
    wd4 = W['ffn_w_down'].reshape(DEPTH, FF_J, FF_NB, D_MODEL)
    ln = {n: _vecs(W[n]) for n in ('ln_mix_g', 'ln_mix_b', 'ln_ffn_g', 'ln_ffn_b', 'ple_b_gate')}
    bglu = _vecs(W['ev_b_glu'])
    pscale = _vecs(W['od_pool_scale'])
    saved = []
    x0 = x
    x0h = x.astype(BF16)
    for i in range(DEPTH):
        L = f"L{i}_"
        j = i // 2
        s = dict(x0=x0, x0h=x0h)
        if i % 2 == 0:
            h = _mm_in(x0h, W['ev_w_in'], j, L + "mm_in")
            params = tuple(W[n][j] for n in ('ev_lambda_re', 'ev_lambda_im', 'ev_log_dt', 'ev_b_re', 'ev_b_im',
                                             'ev_c_re', 'ev_c_im', 'ev_d'))
            (r0, qt, pm, a1, a2), op_vjp = jax.vjp(_s5_operators, *params)
            umat = _to_chunks(h[:, :MIX])
            st = jnp.transpose(_s5_chunk_state(umat, pm, L + "s5_state"), (1, 0, 2))
            xp_t = _s5_scan_fwd(st, a1, a2, L + "s5_scan")
            xprev = jnp.transpose(xp_t, (1, 0, 2))
            y = _from_chunks(_s5_output(umat, xprev, r0, qt, L + "s5_out"))
            ya, gh = _glu_fwd(y, W['ev_w_glu'], bglu, j, L + "glu")
            yb = _conv_fwd(h, W['ev_conv_w'], j, L + "conv")
            s.update(op_vjp=op_vjp, r0=r0, qt=qt, pm=pm, a1=a1, a2=a2, umat=umat, xp_t=xp_t, xprev=xprev, y=y, gh=gh)
            wo, win = W['ev_w_out'], W['ev_w_in']
        else:
            h = _mm_in(x0h, W['od_w_in'], j, L + "mm_in")
            table, tb_vjp = jax.vjp(_att_bias_table, W['od_rel_bias'][j])
            of, ya, lse = _att_fwd(h, table, L + "att")
            yb = _pool_fwd(h, W['od_pool_w'], pscale, j, L + "pool")
            s.update(table=table, tb_vjp=tb_vjp, of=of, lse=lse)
            wo, win = W['od_w_out'], W['od_w_in']
        g1, b1 = (ln['ln_mix_g'], i), (ln['ln_mix_b'], i)
        g2, b2 = (ln['ln_ffn_g'], i), (ln['ln_ffn_b'], i)
        xhat1, rstd1, x1h = _mm_out_ln(ya, yb, wo, j, x0, g1, b1, L + "mm_out_ln")
        hf, a3 = _mm_up(x1h, wup, i, L + "mm_up")
        xhat2, rstd2, x2h = _mm_down_ln(a3, wd4, i, xhat1, g1, b1, g2, b2, L + "mm_down_ln")
        x3, x3h = _mm_ple(x2h, xhat2, ln['ln_ffn_g'], ln['ln_ffn_b'], p, W['ple_w_gate'], ln['ple_b_gate'],
                          W['ple_w_proj'], i, L + "mm_ple")
        s.update(h=h, win=win, wo=wo, ya=ya, yb=yb, xhat1=xhat1, rstd1=rstd1, x1h=x1h, hf=hf, a3=a3, xhat2=xhat2,
                 rstd2=rstd2, x2h=x2h, g1=g1)
        saved.append(s)
        x0, x0h = x3, x3h

    dx, loss = _loss_head(x0, tgt, "loss_head")

    G = {n: [None] * W[n].shape[0 if n not in ('ev_w_in', 'od_w_in', 'ffn_w_up') else 1] for n in WEIGHT_NAMES}
    for i in reversed(range(DEPTH)):
        L = f"L{i}_"
        j = i // 2
        s = saved[i]
        dr2, dr2h, dpreh, dpph, dbg, dg2, db2 = _ple_bwd(
            dx, s['x2h'], p, W['ple_w_gate'], ln['ple_b_gate'], W['ple_w_proj'], i, s['xhat2'], s['rstd2'],
            ln['ln_ffn_g'], L + "ple_bwd")
        G['ple_w_gate'][i] = _mm_tn(s['x2h'], dpreh, L + "dw_gate")
        G['ple_w_proj'][i] = _mm_tn(p, dpph, L + "dw_proj", a_layer=i)
        G['ple_b_gate'][i] = dbg[0]
        G['ln_ffn_g'][i] = dg2[0]
        G['ln_ffn_b'][i] = db2[0]
        dhf = _ffn_bwd1(dr2h, wd4, i, s['hf'], L + "ffn_bwd1")
        G['ffn_w_down'][i] = _mm_tn_ablk(s['a3'], dr2h, L + "dw_down").reshape(D_FF, D_MODEL)
        T = dhf.shape[2]
        G['ffn_w_up'][i] = _mm_tn_bblk(s['x1h'], dhf.reshape(N_DEV, T, FF_NB), L + "dw_up")
        dr1, dr1h, dg1, db1 = _ffn_bwd2(dhf, wup, i, dr2, s['xhat1'], s['rstd1'], s['g1'], L + "ffn_bwd2")
        G['ln_mix_g'][i] = dg1[0]
        G['ln_mix_b'][i] = db1[0]
        dya, dyb = _out_bwd(dr1h, s['wo'], j, L + "out_bwd")
        dwo = jnp.concatenate([_mm_tn(s['ya'], dr1h, L + "dw_out_a"), _mm_tn(s['yb'], dr1h, L + "dw_out_b")], axis=0)
        if i % 2 == 0:
            G['ev_w_out'][j] = dwo
            dy, dzh, dbglu = _glu_bwd(s['y'], dya, W['ev_w_glu'], bglu, j, L + "glu_bwd")
            G['ev_w_glu'][j] = _mm_tn(s['gh'], dzh, L + "dw_glu")
            G['ev_b_glu'][j] = dbglu[0]
            dymat = _to_chunks(dy)
            dxp_t = jnp.transpose(_s5_bwd_state(dymat, s['qt'], L + "s5_bwd_state"), (1, 0, 2))
            ds_t, da1, da2 = _s5_scan_bwd(dxp_t, s['xp_t'], s['a1'], s['a2'], L + "s5_scan_bwd")
            dumat, dr0, dqt, dpm = _s5_bwd_main(s['umat'], s['xprev'], dymat, jnp.transpose(ds_t, (1, 0, 2)),
                                                s['r0'], s['pm'], L + "s5_bwd")
            dparams = s['op_vjp']((dr0, dqt, dpm, da1, da2))
            for n, dpar in zip(('ev_lambda_re', 'ev_lambda_im', 'ev_log_dt', 'ev_b_re', 'ev_b_im', 'ev_c_re',
                                'ev_c_im', 'ev_d'), dparams):
                G[n][j] = dpar
            dua = _from_chunks(dumat)
            dh, dcw = _conv_bwd(dyb, s['h'], dua, W['ev_conv_w'], j, L + "conv_bwd")
            G['ev_conv_w'][j] = dcw[:3]
            wname = 'ev_w_in'
        else:
            G['od_w_out'][j] = dwo
            dq, delta, dtable = _att_bwd_q(s['h'], dya, s['of'], s['lse'], s['table'], L + "att_bwd_q")
            dk, dv = _att_bwd_kv(s['h'], dya, s['lse'].T, delta.T, _att_table_by_key(s['table']), L + "att_bwd_kv")
            G['od_rel_bias'][j] = s['tb_vjp'](dtable)[0]
            dh, dpw, dsc = _pool_bwd(dyb, s['h'], dq, dk, dv, W['od_pool_w'], pscale, j, L + "pool_bwd")
            G['od_pool_w'][j] = dpw
            G['od_pool_scale'][j] = dsc[0]
            wname = 'od_w_in'
        G[wname][j] = _mm_tn(s['x0h'], dh, L + "dw_in", blocked_n=IN_NB)
        dx = _in_bwd(dh, s['win'], j, dr1, L + "in_bwd")

    return loss, dx, G


def _slab(a):
    return a.reshape(-1, a.shape[-1])


def _gather_weights(W):
    full = {n: W[n] for n in REPLICATED}
    slabs = [_slab(W[n].astype(BF16)) for n in ROW_SHARDED]
    got = _all_gather(jnp.concatenate(slabs, axis=0), "gather_w_rows")
    row = 0
    for n, sl in zip(ROW_SHARDED, slabs):
        L, k8, d = W[n].shape
        piece = lax.slice_in_dim(got, row, row + sl.shape[0], axis=1).reshape(N_DEV, L, k8, d)
        full[n] = jnp.swapaxes(piece, 0, 1).reshape(L, N_DEV * k8, d)
        row += sl.shape[0]
    got = _all_gather(_slab(W['ffn_w_up'].astype(BF16)), "gather_w_up")
    full['ffn_w_up'] = got.reshape((N_DEV,) + W['ffn_w_up'].shape)
    slabs = [_slab(W[n].astype(BF16)) for n in COL_IN]
    got = _all_gather(jnp.concatenate(slabs, axis=0), "gather_w_in")
    row = 0
    for n, sl in zip(COL_IN, slabs):
        full[n] = lax.slice_in_dim(got, row, row + sl.shape[0], axis=1).reshape((N_DEV,) + W[n].shape)
        row += sl.shape[0]
    got = _all_gather(_pack_rows([W[n].astype(BF16) for n in SMALL_SHARDED]), "gather_w_small")
    shapes = [W[n].shape for n in SMALL_SHARDED]
    for n, blocks in zip(SMALL_SHARDED, _unpack_rows(got, shapes, lead=(N_DEV,))):
        full[n] = _from_blocks(blocks, SHARD_AXIS[n])
    for n in ('ev_conv_w', 'od_pool_scale'):
        full[n] = full[n].astype(F32)
    return full


def _step(x, p, tgt, W, M, V):
    full = _gather_weights(W)
    loss, dx, G = _local_step(x[0], p[:, 0], tgt[0], full)
    res = {}

    def update(parts, row, n, tag):
        shape = W[n].shape
        outs = _adamw(parts, row, _slab(W[n]), _slab(M[n]), _slab(V[n]), "adamw_" + tag)
        for kind, a in zip(('grad', 'delta', 'm', 'v'), outs):
            res[kind, n] = a.reshape(shape)
        return row + math.prod(shape[:-1])

    send = []
    for n in ROW_SHARDED:
        g = jnp.stack(G[n], axis=0)
        L, K, d = g.shape
        send.append(jnp.swapaxes(g.reshape(L, N_DEV, K // N_DEV, d), 0, 1).reshape(N_DEV, L * K // N_DEV, d))
    parts = _all_to_all(jnp.concatenate(send, axis=1), "scatter_g_rows")
    row = 0
    for n in ROW_SHARDED:
        row = update(parts, row, n, n)
    g = jnp.stack(G['ffn_w_up'], axis=1)
    parts = _all_to_all(g.reshape(N_DEV, -1, FF_NB), "scatter_g_up")
    update(parts, 0, 'ffn_w_up', 'ffn_w_up')
    send = [jnp.stack(G[n], axis=1).reshape(N_DEV, -1, IN_NB) for n in COL_IN]
    parts = _all_to_all(jnp.concatenate(send, axis=1), "scatter_g_in")
    row = 0
    for n in COL_IN:
        row = update(parts, row, n, n)

    shapes = [W[n].shape for n in SMALL_SHARDED]
    send = _pack_rows([_to_blocks(jnp.stack(G[n], axis=0), SHARD_AXIS[n]).astype(BF16) for n in SMALL_SHARDED],
                      lead=(N_DEV,))
    parts = _all_to_all(send, "scatter_g_small")
    outs = _adamw(parts, 0, _pack_rows([W[n] for n in SMALL_SHARDED]), _pack_rows([M[n] for n in SMALL_SHARDED]),
                  _pack_rows([V[n] for n in SMALL_SHARDED]), "adamw_small")
    for kind, buf in zip(('grad', 'delta', 'm', 'v'), outs):
        for n, a in zip(SMALL_SHARDED, _unpack_rows(buf, shapes)):
            res[kind, n] = a

    repl_shapes = [W[n].shape for n in REPLICATED]
    small = _pack_rows([jnp.stack(G[n], axis=0) for n in REPLICATED] + [loss])
    parts = _all_gather(small, "gather_g_replicated")
    zero = jnp.zeros((1, 1), F32)
    outs = _adamw(parts, 0, _pack_rows([W[n] for n in REPLICATED] + [zero]),
                  _pack_rows([M[n] for n in REPLICATED] + [zero]),
                  _pack_rows([V[n] for n in REPLICATED] + [zero]), "adamw_replicated")
    for kind, buf in zip(('grad', 'delta', 'm', 'v'), outs):
        arrays = _unpack_rows(buf, repl_shapes + [(1, 1)])
        for n, a in zip(REPLICATED, arrays):
            res[kind, n] = a
        if kind == 'grad':
            total_loss = arrays[-1].reshape(())

    out = [total_loss, dx[None]]
    for kind in ('grad', 'delta', 'm', 'v'):
        out += [res[kind, n] for n in WEIGHT_NAMES]
    return tuple(out)


def kernel(x, p, ev_w_in, ev_lambda_re, ev_lambda_im, ev_log_dt, ev_b_re, ev_b_im, ev_c_re, ev_c_im, ev_d, ev_w_glu, ev_b_glu, ev_conv_w, ev_w_out, od_w_in, od_rel_bias, od_pool_w, od_pool_scale, od_w_out, ln_mix_g, ln_mix_b, ln_ffn_g, ln_ffn_b, ffn_w_up, ffn_w_down, ple_w_proj, ple_w_gate, ple_b_gate, loss_target, m_ev_w_in, m_ev_lambda_re, m_ev_lambda_im, m_ev_log_dt, m_ev_b_re, m_ev_b_im, m_ev_c_re, m_ev_c_im, m_ev_d, m_ev_w_glu, m_ev_b_glu, m_ev_conv_w, m_ev_w_out, m_od_w_in, m_od_rel_bias, m_od_pool_w, m_od_pool_scale, m_od_w_out, m_ln_mix_g, m_ln_mix_b, m_ln_ffn_g, m_ln_ffn_b, m_ffn_w_up, m_ffn_w_down, m_ple_w_proj, m_ple_w_gate, m_ple_b_gate, v_ev_w_in, v_ev_lambda_re, v_ev_lambda_im, v_ev_log_dt, v_ev_b_re, v_ev_b_im, v_ev_c_re, v_ev_c_im, v_ev_d, v_ev_w_glu, v_ev_b_glu, v_ev_conv_w, v_ev_w_out, v_od_w_in, v_od_rel_bias, v_od_pool_w, v_od_pool_scale, v_od_w_out, v_ln_mix_g, v_ln_mix_b, v_ln_ffn_g, v_ln_ffn_b, v_ffn_w_up, v_ffn_w_down, v_ple_w_proj, v_ple_w_gate, v_ple_b_gate):
    given = dict(locals())
    W = {n: given[n] for n in WEIGHT_NAMES}
    M = {n: given["m_" + n] for n in WEIGHT_NAMES}
    V = {n: given["v_" + n] for n in WEIGHT_NAMES}
    return _step(x, p, loss_target, W, M, V)
```

```python
import math

import jax
import jax.numpy as jnp
from jax import lax
from jax.experimental import pallas as pl
from jax.experimental.pallas import tpu as pltpu

F32 = jnp.float32
BF16 = jnp.bfloat16
HI = lax.Precision.HIGHEST

D_MODEL = 1024
DEPTH = 4
CHUNK = 64
MIX = 512
S5_GROUP = 16
S5_GROUPS = 32
S5_STATE = 64
HEADS = 8
HEAD_DIM = 64
LEFT_CHUNKS = 8
MAX_REL = 128
POOL_WINDOWS = (2, 4, 8, 16)
POOL_GROUP = 128
D_FF = 2816
D_PLE = 256
ALPHA = (2 * DEPTH) ** 0.25
LN_EPS = 1e-5
NEG_INF = -1e30
ADAM_LR = 0.001
ADAM_B1 = 0.9
ADAM_B2 = 0.999
ADAM_EPS = 1e-08
ADAM_WD = 0.01
ADAM_STEP = 10
N_DEV = 8

WEIGHT_NAMES = ['ev_w_in', 'ev_lambda_re', 'ev_lambda_im', 'ev_log_dt', 'ev_b_re', 'ev_b_im', 'ev_c_re', 'ev_c_im',
                'ev_d', 'ev_w_glu', 'ev_b_glu', 'ev_conv_w', 'ev_w_out', 'od_w_in', 'od_rel_bias', 'od_pool_w',
                'od_pool_scale', 'od_w_out', 'ln_mix_g', 'ln_mix_b', 'ln_ffn_g', 'ln_ffn_b', 'ffn_w_up', 'ffn_w_down',
                'ple_w_proj', 'ple_w_gate', 'ple_b_gate']
SHARD_AXIS = {'ev_w_in': 2, 'ev_w_glu': 1, 'ev_conv_w': 2, 'ev_w_out': 1, 'od_w_in': 2, 'od_pool_scale': 1,
              'od_w_out': 1, 'ffn_w_up': 2, 'ffn_w_down': 1, 'ple_w_proj': 2, 'ple_w_gate': 1}
REPLICATED = [n for n in WEIGHT_NAMES if n not in SHARD_AXIS]
ROW_SHARDED = ['ev_w_out', 'od_w_out', 'ffn_w_down', 'ple_w_gate']
COL_IN = ['ev_w_in', 'od_w_in']
SMALL_SHARDED = ['ev_w_glu', 'ev_conv_w', 'od_pool_scale', 'ple_w_proj']

VMEM_LIMIT = 48 * 1024 * 1024
TM = 512
IN_NB = 4 * MIX // N_DEV
FF_NB = 2 * D_FF // N_DEV
FF_J = N_DEV // 2
S5_LC = 32
S5_LW = S5_LC * S5_GROUP
ATT_TQ = 128
ATT_NV = LEFT_CHUNKS * CHUNK // ATT_TQ + 1
ATT_W = ATT_NV * ATT_TQ
FLAT_COLS = 1024
FLAT_ROWS = 256
PIECE = 16 * FLAT_COLS


_MESH = pl.DeviceIdType.MESH
_ANY = pl.BlockSpec(memory_space=pl.ANY)


def _sds(shape, dt):
    return jax.ShapeDtypeStruct(shape, dt)


def _pcall(body, *, name, grid, in_specs, out_specs, out_shape, scratch=(), sem=None):
    sem = sem or ("arbitrary",) * len(grid)
    return pl.pallas_call(
        body, name=name, grid=grid, in_specs=in_specs, out_specs=out_specs, out_shape=out_shape,
        scratch_shapes=scratch,
        compiler_params=pltpu.CompilerParams(dimension_semantics=sem, vmem_limit_bytes=VMEM_LIMIT))


def _rows(tm, n):
    return pl.BlockSpec((tm, n), lambda i: (i, 0))


def _cols(tm, s):
    return pl.BlockSpec((tm, MIX), lambda i: (i, s))


def _full(shape):
    nd = len(shape)
    return pl.BlockSpec(shape, lambda *_: (0,) * nd)


def _layer(shape, l):
    nd = len(shape)
    return pl.BlockSpec((None,) + tuple(shape), lambda *_: (l,) + (0,) * nd)


def _vecs(a):
    return a.reshape(a.shape[0], 1, a.shape[1])


def _dot(a, b, precision=None):
    return jnp.dot(a, b, preferred_element_type=F32, precision=precision)


def _dot_nt(a, b, precision=None):
    return lax.dot_general(a, b, (((1,), (1,)), ((), ())), preferred_element_type=F32, precision=precision)


def _dot_tn(a, b, precision=None):
    return lax.dot_general(a, b, (((0,), (0,)), ((), ())), preferred_element_type=F32, precision=precision)


def _sigmoid(x):
    return 0.5 * jnp.tanh(0.5 * x) + 0.5


_GELU_C = math.sqrt(2.0 / math.pi)


def _gelu(x):
    return 0.5 * x * (1.0 + jnp.tanh(_GELU_C * (x + 0.044715 * x * x * x)))


def _gelu_grad(x):
    t = jnp.tanh(_GELU_C * (x + 0.044715 * x * x * x))
    return 0.5 * (1.0 + t) + 0.5 * x * (1.0 - t * t) * _GELU_C * (1.0 + 3.0 * 0.044715 * x * x)


def _ln_fwd(r, g, b):
    mu = jnp.mean(r, axis=-1, keepdims=True)
    xc = r - mu
    var = jnp.mean(xc * xc, axis=-1, keepdims=True)
    rstd = lax.rsqrt(var + LN_EPS)
    xhat = xc * rstd
    return xhat, rstd, xhat * g + b


def _ln_bwd(dx, xhat, rstd, g):
    dxh = dx * g
    m1 = jnp.mean(dxh, axis=-1, keepdims=True)
    m2 = jnp.mean(dxh * xhat, axis=-1, keepdims=True)
    return rstd * (dxh - m1 - xhat * m2)


def _colsum(x):
    return jnp.sum(x, axis=0, keepdims=True)


def _mm_in(xh, win8, l, name):
    T = xh.shape[0]

    def body(x_ref, w_ref, h_ref):
        x = x_ref[...]
        for b in range(N_DEV):
            h_ref[:, b * IN_NB:(b + 1) * IN_NB] = _dot(x, w_ref[b])

    return _pcall(body, name=name, grid=(T // TM,),
                  in_specs=[_rows(TM, D_MODEL),
                            pl.BlockSpec((N_DEV, None, D_MODEL, IN_NB), lambda i: (0, l, 0, 0))],
                  out_specs=_rows(TM, 4 * MIX), out_shape=_sds((T, 4 * MIX), F32),
                  sem=("parallel",))(xh, win8)


def _tile_of(n, cap):
    best = None
    for t in range(128, min(n, cap) + 1, 128):
        if n % t == 0:
            best = t
    assert best is not None, n
    return best


def _mm_tn(a, b, name, a_layer=None, blocked_n=None):
    T, M = a.shape[-2:]
    N = b.shape[1]
    tm = _tile_of(M, 1408)
    nbs = max(1, 512 // blocked_n) if blocked_n else 1
    tn = blocked_n * nbs if blocked_n else _tile_of(N, 1024 if tm <= 512 else 512)
    tk = min(T, 1024)
    nk = T // tk

    def body(a_ref, b_ref, o_ref, acc):
        k = pl.program_id(2)

        @pl.when(k == 0)
        def _():
            acc[...] = jnp.zeros_like(acc)

        acc[...] += _dot_tn(a_ref[...].astype(BF16), b_ref[...].astype(BF16))

        @pl.when(k == nk - 1)
        def _():
            if blocked_n:
                for sb in range(nbs):
                    o_ref[sb] = acc[:, sb * blocked_n:(sb + 1) * blocked_n].astype(BF16)
            else:
                o_ref[...] = acc[...].astype(BF16)

    if a_layer is None:
        a_spec = pl.BlockSpec((tk, tm), lambda i, j, k: (k, i))
    else:
        a_spec = pl.BlockSpec((None, tk, tm), lambda i, j, k: (a_layer, k, i))
    if blocked_n:
        o_spec = pl.BlockSpec((nbs, tm, blocked_n), lambda i, j, k: (j, i, 0))
        o_shape = _sds((N // blocked_n, M, blocked_n), BF16)
    else:
        o_spec = pl.BlockSpec((tm, tn), lambda i, j, k: (i, j))
        o_shape = _sds((M, N), BF16)
    return _pcall(body, name=name, grid=(M // tm, N // tn, nk),
                  in_specs=[a_spec, pl.BlockSpec((tk, tn), lambda i, j, k: (k, j))],
                  out_specs=o_spec, out_shape=o_shape, scratch=[pltpu.VMEM((tm, tn), F32)],
                  sem=("parallel", "parallel", "arbitrary"))(a, b)


def _mm_tn_bblk(a, b3, name, per_step=2):
    T, M = a.shape
    NB, _, n = b3.shape
    tk = min(T, 1024)
    nk = T // tk

    def body(a_ref, b_ref, o_ref, acc):
        k = pl.program_id(1)

        @pl.when(k == 0)
        def _():
            acc[...] = jnp.zeros_like(acc)

        av = a_ref[...]
        for s in range(per_step):
            acc[s] += _dot_tn(av, b_ref[s])

        @pl.when(k == nk - 1)
        def _():
            o_ref[...] = acc[...].astype(BF16)

    return _pcall(body, name=name, grid=(NB // per_step, nk),
                  in_specs=[pl.BlockSpec((tk, M), lambda j, k: (k, 0)),
                            pl.BlockSpec((per_step, tk, n), lambda j, k: (j, k, 0))],
                  out_specs=pl.BlockSpec((per_step, M, n), lambda j, k: (j, 0, 0)),
                  out_shape=_sds((NB, M, n), BF16), scratch=[pltpu.VMEM((per_step, M, n), F32)],
                  sem=("parallel", "arbitrary"))(a, b3)


def _mm_tn_ablk(a3, b, name):
    NA, T, m = a3.shape
    N = b.shape[1]
    tn = _tile_of(N, 1024)
    tk = min(T, 1024)
    nk = T // tk

    def body(a_ref, b_ref, o_ref, acc):
        k = pl.program_id(2)

        @pl.when(k == 0)
        def _():
            acc[...] = jnp.zeros_like(acc)

        acc[...] += _dot_tn(a_ref[...], b_ref[...])

        @pl.when(k == nk - 1)
        def _():
            o_ref[...] = acc[...].astype(BF16)

    return _pcall(body, name=name, grid=(NA, N // tn, nk),
                  in_specs=[pl.BlockSpec((None, tk, m), lambda j, n, k: (j, k, 0)),
                            pl.BlockSpec((tk, tn), lambda j, n, k: (k, n))],
                  out_specs=pl.BlockSpec((None, m, tn), lambda j, n, k: (j, 0, n)),
                  out_shape=_sds((NA, m, N), BF16), scratch=[pltpu.VMEM((m, tn), F32)],
                  sem=("parallel", "parallel", "arbitrary"))(a3, b)


def _mm_out_ln(ya, yb, wo, l, x0, g, b, name):
    T = x0.shape[0]

    def body(ya_ref, yb_ref, w_ref, x0_ref, g_ref, b_ref, xh_ref, rs_ref, x1_ref):
        r = ALPHA * x0_ref[...] + _dot(ya_ref[...], w_ref[0:MIX, :]) + _dot(yb_ref[...], w_ref[MIX:, :])
        xhat, rstd, x1 = _ln_fwd(r, g_ref[...], b_ref[...])
        xh_ref[...] = xhat
        rs_ref[...] = rstd
        x1_ref[...] = x1.astype(BF16)

    vec = _layer((1, D_MODEL), g[1])
    return _pcall(body, name=name, grid=(T // TM,),
                  in_specs=[_rows(TM, MIX), _rows(TM, MIX), _layer((D_MODEL, D_MODEL), l), _rows(TM, D_MODEL),
                            vec, vec],
                  out_specs=[_rows(TM, D_MODEL), _rows(TM, 1), _rows(TM, D_MODEL)],
                  out_shape=[_sds((T, D_MODEL), F32), _sds((T, 1), F32), _sds((T, D_MODEL), BF16)],
                  sem=("parallel",))(ya, yb, wo, x0, g[0], b[0])


def _mm_up(x1h, wup, l, name):
    T = x1h.shape[0]

    def body(x_ref, w_ref, hf_ref, a_ref):
        x = x_ref[...]
        g = _dot(x, w_ref[0])
        u = _dot(x, w_ref[1])
        hf_ref[0] = g.astype(BF16)
        hf_ref[1] = u.astype(BF16)
        a_ref[...] = (g * _sigmoid(g) * u).astype(BF16)

    return _pcall(body, name=name, grid=(FF_J, T // TM),
                  in_specs=[pl.BlockSpec((TM, D_MODEL), lambda j, i: (i, 0)),
                            pl.BlockSpec((2, None, None, D_MODEL, FF_NB), lambda j, i: (0, j, l, 0, 0))],
                  out_specs=[pl.BlockSpec((2, None, TM, FF_NB), lambda j, i: (0, j, i, 0)),
                             pl.BlockSpec((None, TM, FF_NB), lambda j, i: (j, i, 0))],
                  out_shape=[_sds((2, FF_J, T, FF_NB), BF16), _sds((FF_J, T, FF_NB), BF16)],
                  sem=("parallel", "parallel"))(x1h, wup)


def _mm_down_ln(a3, wd4, l, xhat1, g1, b1, g2, b2, name):
    T = a3.shape[1]

    def body(a_ref, w_ref, xh1_ref, g1_ref, b1_ref, g2_ref, b2_ref, xh_ref, rs_ref, x2_ref):
        x1 = xh1_ref[...] * g1_ref[...] + b1_ref[...]
        r = ALPHA * x1
        for j in range(FF_J):
            r = r + _dot(a_ref[j], w_ref[j])
        xhat, rstd, x2 = _ln_fwd(r, g2_ref[...], b2_ref[...])
        xh_ref[...] = xhat
        rs_ref[...] = rstd
        x2_ref[...] = x2.astype(BF16)

    vec = _layer((1, D_MODEL), g1[1])
    return _pcall(body, name=name, grid=(T // TM,),
                  in_specs=[pl.BlockSpec((FF_J, TM, FF_NB), lambda i: (0, i, 0)),
                            _layer((FF_J, FF_NB, D_MODEL), l), _rows(TM, D_MODEL), vec, vec, vec, vec],
                  out_specs=[_rows(TM, D_MODEL), _rows(TM, 1), _rows(TM, D_MODEL)],
                  out_shape=[_sds((T, D_MODEL), F32), _sds((T, 1), F32), _sds((T, D_MODEL), BF16)],
                  sem=("parallel",))(a3, wd4, xhat1, g1[0], b1[0], g2[0], b2[0])


def _mm_ple(x2h, xhat2, g2, b2, p, wg, bg, wp, l, name):
    T = x2h.shape[0]

    def body(x2h_ref, xh_ref, g2_ref, b2_ref, p_ref, wg_ref, bg_ref, wp_ref, o_ref, oh_ref):
        x2 = xh_ref[...] * g2_ref[...] + b2_ref[...]
        gate = _sigmoid(_dot(x2h_ref[...], wg_ref[...]) + bg_ref[...])
        pp = _dot(p_ref[...].astype(BF16), wp_ref[...])
        x3 = x2 + gate * pp
        o_ref[...] = x3
        oh_ref[...] = x3.astype(BF16)

    vec = _layer((1, D_MODEL), l)
    return _pcall(body, name=name, grid=(T // TM,),
                  in_specs=[_rows(TM, D_MODEL), _rows(TM, D_MODEL), vec, vec,
                            pl.BlockSpec((None, TM, D_PLE), lambda i: (l, i, 0)),
                            _layer((D_MODEL, D_MODEL), l), vec, _layer((D_PLE, D_MODEL), l)],
                  out_specs=[_rows(TM, D_MODEL), _rows(TM, D_MODEL)],
                  out_shape=[_sds((T, D_MODEL), F32), _sds((T, D_MODEL), BF16)],
                  sem=("parallel",))(x2h, xhat2, g2, b2, p, wg, bg, wp)


def _loss_head(x3, tgt, name):
    T = x3.shape[0]

    def body(x_ref, t_ref, dx_ref, l_ref):
        e = x_ref[...] - t_ref[...]
        dx_ref[...] = e * (1.0 / D_MODEL)

        @pl.when(pl.program_id(0) == 0)
        def _():
            l_ref[...] = jnp.zeros_like(l_ref)

        l_ref[...] += (0.5 / D_MODEL) * jnp.sum(e * e).reshape(1, 1)

    return _pcall(body, name=name, grid=(T // TM,),
                  in_specs=[_rows(TM, D_MODEL), _rows(TM, D_MODEL)],
                  out_specs=[_rows(TM, D_MODEL), _full((1, 1))],
                  out_shape=[_sds((T, D_MODEL), F32), _sds((1, 1), F32)])(x3, tgt)


def _ple_bwd(dx3, x2h, p, wg, bg, wp, l, xhat2, rstd2, g2, name):
    T = dx3.shape[0]

    def body(dx3_ref, x2h_ref, p_ref, wg_ref, bg_ref, wp_ref, xh_ref, rs_ref, g2_ref,
             dr_ref, drh_ref, dpre_ref, dpp_ref, dbg_ref, dg_ref, db_ref):
        dx3 = dx3_ref[...]
        gate = _sigmoid(_dot(x2h_ref[...], wg_ref[...]) + bg_ref[...])
        pp = _dot(p_ref[...].astype(BF16), wp_ref[...])
        dpre = dx3 * pp * gate * (1.0 - gate)
        dpreh = dpre.astype(BF16)
        dpre_ref[...] = dpreh
        dpp_ref[...] = (dx3 * gate).astype(BF16)
        dx2 = dx3 + _dot_nt(dpreh, wg_ref[...])
        xhat = xh_ref[...]
        dr = _ln_bwd(dx2, xhat, rs_ref[...], g2_ref[...])
        dr_ref[...] = dr
        drh_ref[...] = dr.astype(BF16)

        @pl.when(pl.program_id(0) == 0)
        def _():
            dbg_ref[...] = jnp.zeros_like(dbg_ref)
            dg_ref[...] = jnp.zeros_like(dg_ref)
            db_ref[...] = jnp.zeros_like(db_ref)

        dbg_ref[...] += _colsum(dpre)
        dg_ref[...] += _colsum(dx2 * xhat)
        db_ref[...] += _colsum(dx2)

    vec = _layer((1, D_MODEL), l)
    acc = _full((1, D_MODEL))
    big = _rows(TM, D_MODEL)
    return _pcall(body, name=name, grid=(T // TM,),
                  in_specs=[big, big, pl.BlockSpec((None, TM, D_PLE), lambda i: (l, i, 0)),
                            _layer((D_MODEL, D_MODEL), l), vec, _layer((D_PLE, D_MODEL), l),
                            big, _rows(TM, 1), vec],
                  out_specs=[big, big, big, big, acc, acc, acc],
                  out_shape=[_sds((T, D_MODEL), F32), _sds((T, D_MODEL), BF16), _sds((T, D_MODEL), BF16),
                             _sds((T, D_MODEL), BF16), _sds((1, D_MODEL), F32), _sds((1, D_MODEL), F32),
                             _sds((1, D_MODEL), F32)])(dx3, x2h, p, wg, bg, wp, xhat2, rstd2, g2)


def _ffn_bwd1(dr2h, wd4, l, hf, name, ride=()):
    T = dr2h.shape[0]
    ni = T // TM

    def body(d_ref, w_ref, hf_ref, o_ref):
        da = _dot_nt(d_ref[...], w_ref[...])
        g = hf_ref[0].astype(F32)
        u = hf_ref[1].astype(F32)
        sg = _sigmoid(g)
        o_ref[0] = (da * u * (sg * (1.0 + g * (1.0 - sg)))).astype(BF16)
        o_ref[1] = (da * (g * sg)).astype(BF16)

    blk = pl.BlockSpec((2, None, TM, FF_NB), lambda j, i: (0, j, i, 0))
    in_specs = [pl.BlockSpec((TM, D_MODEL), lambda j, i: (i, 0)),
                pl.BlockSpec((None, None, FF_NB, D_MODEL), lambda j, i: (l, j, 0, 0)), blk]
    out_shape = _sds((2, FF_J, T, FF_NB), BF16)
    if not ride:
        return _pcall(body, name=name, grid=(FF_J, ni), in_specs=in_specs, out_specs=blk, out_shape=out_shape,
                      sem=("parallel", "parallel"))(dr2h, wd4, hf), ()
    first = lambda: (pl.program_id(0) == 0) & (pl.program_id(1) == 0)
    last = lambda: (pl.program_id(0) == FF_J - 1) & (pl.program_id(1) == ni - 1)
    outs = _pcall(_riding(body, 3, 1, len(ride), first, last), name=name, grid=(FF_J, ni),
                  in_specs=in_specs + [_ANY] * len(ride), out_specs=[blk] + [_ANY] * len(ride),
                  out_shape=[out_shape] + [_sds(r.shape, r.dtype) for r in ride],
                  scratch=list(_A2A_SEMS) * len(ride))(dr2h, wd4, hf, *ride)
    return outs[0], outs[1:]


_TM_B2 = 256


def _ffn_bwd2(dhf, wup, l, dr2, xhat1, rstd1, g1, name, ride=()):
    T = dr2.shape[0]
    tm = min(T, _TM_B2)
    ni = T // tm

    def body(dh_ref, w_ref, dr2_ref, xh_ref, rs_ref, g_ref, dr_ref, drh_ref, dg_ref, db_ref):
        dx1 = ALPHA * dr2_ref[...]
        for s in range(2):
            for j in range(FF_J):
                dx1 = dx1 + _dot_nt(dh_ref[s, j], w_ref[s, j])
        xhat = xh_ref[...]
        dr = _ln_bwd(dx1, xhat, rs_ref[...], g_ref[...])
        dr_ref[...] = dr
        drh_ref[...] = dr.astype(BF16)

        @pl.when(pl.program_id(0) == 0)
        def _():
            dg_ref[...] = jnp.zeros_like(dg_ref)
            db_ref[...] = jnp.zeros_like(db_ref)

        dg_ref[...] += _colsum(dx1 * xhat)
        db_ref[...] += _colsum(dx1)

    big = _rows(tm, D_MODEL)
    acc = _full((1, D_MODEL))
    in_specs = [pl.BlockSpec((2, FF_J, tm, FF_NB), lambda i: (0, 0, i, 0)),
                pl.BlockSpec((2, FF_J, None, D_MODEL, FF_NB), lambda i: (0, 0, l, 0, 0)),
                big, big, _rows(tm, 1), _layer((1, D_MODEL), g1[1])]
    out_specs = [big, big, acc, acc]
    out_shape = [_sds((T, D_MODEL), F32), _sds((T, D_MODEL), BF16), _sds((1, D_MODEL), F32),
                 _sds((1, D_MODEL), F32)]
    operands = (dhf, wup, dr2, xhat1, rstd1, g1[0])
    if not ride:
        return _pcall(body, name=name, grid=(ni,), in_specs=in_specs, out_specs=out_specs,
                      out_shape=out_shape)(*operands), ()
    first = lambda: pl.program_id(0) == 0
    last = lambda: pl.program_id(0) == ni - 1
    outs = _pcall(_riding(body, 6, 4, len(ride), first, last), name=name, grid=(ni,),
                  in_specs=in_specs + [_ANY] * len(ride), out_specs=out_specs + [_ANY] * len(ride),
                  out_shape=out_shape + [_sds(r.shape, r.dtype) for r in ride],
                  scratch=list(_A2A_SEMS) * len(ride))(*operands, *ride)
    return outs[:4], outs[4:]


def _out_bwd(dr1h, wo, l, name):
    T = dr1h.shape[0]

    def body(d_ref, w_ref, da_ref, db_ref):
        d = d_ref[...]
        da_ref[...] = _dot_nt(d, w_ref[0:MIX, :])
        db_ref[...] = _dot_nt(d, w_ref[MIX:, :])

    return _pcall(body, name=name, grid=(T // TM,),
                  in_specs=[_rows(TM, D_MODEL), _layer((D_MODEL, D_MODEL), l)],
                  out_specs=[_rows(TM, MIX), _rows(TM, MIX)],
                  out_shape=[_sds((T, MIX), F32), _sds((T, MIX), F32)],
                  sem=("parallel",))(dr1h, wo)


def _in_bwd(dh, win8, l, dr1, name):
    T = dr1.shape[0]

    def body(dh_ref, w_ref, dr_ref, o_ref):
        acc = ALPHA * dr_ref[...]
        for b in range(N_DEV):
            acc = acc + _dot_nt(dh_ref[:, b * IN_NB:(b + 1) * IN_NB], w_ref[b])
        o_ref[...] = acc

    return _pcall(body, name=name, grid=(T // TM,),
                  in_specs=[_rows(TM, 4 * MIX), pl.BlockSpec((N_DEV, None, D_MODEL, IN_NB), lambda i: (0, l, 0, 0)),
                            _rows(TM, D_MODEL)],
                  out_specs=_rows(TM, D_MODEL), out_shape=_sds((T, D_MODEL), F32),
                  sem=("parallel",))(dh, win8, dr1)


def _s5_operators(lre, lim, log_dt, bre, bim, cre, cim, dskip):
    G, P, H, LC = S5_GROUPS, S5_STATE, S5_GROUP, S5_LC
    dt = jnp.exp(log_dt)[:, None]
    mag = jnp.exp(lre * dt)
    ang = lim * dt
    lb_re = mag * jnp.cos(ang)
    lb_im = mag * jnp.sin(ang)
    den = lre * lre + lim * lim
    nr = lb_re - 1.0
    ni = lb_im
    r_re = (nr * lre + ni * lim) / den
    r_im = (ni * lre - nr * lim) / den
    bb_re = r_re[..., None] * bre - r_im[..., None] * bim
    bb_im = r_re[..., None] * bim + r_im[..., None] * bre
    k = jnp.arange(LC + 1, dtype=F32)[:, None, None]
    pmag = jnp.exp(k * (lre * dt)[None])
    pang = k * ang[None]
    pw_re = pmag * jnp.cos(pang)
    pw_im = pmag * jnp.sin(pang)
    cp_re = cre[None] * pw_re[:, :, None, :] - cim[None] * pw_im[:, :, None, :]
    cp_im = cre[None] * pw_im[:, :, None, :] + cim[None] * pw_re[:, :, None, :]
    kk = (jnp.einsum('kghp,gpj->kghj', cp_re[:LC], bb_re, precision=HI)
          - jnp.einsum('kghp,gpj->kghj', cp_im[:LC], bb_im, precision=HI))
    dmat = dskip.reshape(G, H)[:, :, None] * jnp.eye(H, dtype=F32)[None]
    kk = jnp.concatenate([kk[:1] + dmat[None], kk[1:]], axis=0)
    r0 = jnp.transpose(kk, (1, 3, 0, 2)).reshape(G, H, LC * H)
    qt = jnp.stack([cp_re[1:], -cp_im[1:]], axis=0)
    qt = jnp.transpose(qt, (2, 0, 4, 1, 3)).reshape(G, 2 * P, LC * H)
    pb_re = pw_re[:LC, :, :, None] * bb_re[None] - pw_im[:LC, :, :, None] * bb_im[None]
    pb_im = pw_re[:LC, :, :, None] * bb_im[None] + pw_im[:LC, :, :, None] * bb_re[None]
    pm = jnp.stack([pb_re[::-1], pb_im[::-1]], axis=0)
    pm = jnp.transpose(pm, (2, 1, 4, 0, 3)).reshape(G, LC * H, 2 * P)
    a_re = pw_re[LC]
    a_im = pw_im[LC]
    a1 = jnp.concatenate([a_re, a_re], axis=-1)
    a2 = jnp.concatenate([-a_im, a_im], axis=-1)
    return r0, qt, pm, a1, a2


def _to_chunks(u):
    T = u.shape[0]
    return jnp.transpose(u.reshape(T // S5_LC, S5_LC, S5_GROUPS, S5_GROUP), (2, 0, 1, 3)).reshape(
        S5_GROUPS, T // S5_LC, S5_LW)


def _from_chunks(m):
    nc = m.shape[1]
    return jnp.transpose(m.reshape(S5_GROUPS, nc, S5_LC, S5_GROUP), (1, 2, 0, 3)).reshape(nc * S5_LC, MIX)


def _gspec(r, c):
    return pl.BlockSpec((None, r, c), lambda g: (g, 0, 0))


def _s5_chunk_state(umat, pm, name):
    G, nc, _ = umat.shape

    def body(u_ref, p_ref, o_ref):
        o_ref[...] = _dot(u_ref[...], p_ref[...], HI)

    return _pcall(body, name=name, grid=(G,), in_specs=[_gspec(nc, S5_LW), _gspec(S5_LW, 128)],
                  out_specs=_gspec(nc, 128), out_shape=_sds((G, nc, 128), F32), sem=("parallel",))(umat, pm)


_SCAN_G = 8
_SCAN_UNROLL = 8


def _s5_scan_fwd(s_t, a1, a2, name):
    nc, G, _ = s_t.shape

    def body(s_ref, a1_ref, a2_ref, o_ref, sb_ref):
        sb_ref[...] = pltpu.roll(s_ref[...], 64, 2)
        a1v = a1_ref[...]
        a2v = a2_ref[...]

        def step(c, carry):
            x, xb = carry
            o_ref[c] = x
            return a1v * x + a2v * xb + s_ref[c], a1v * xb - a2v * x + sb_ref[c]

        zero = jnp.zeros((_SCAN_G, 128), F32)
        lax.fori_loop(0, nc, step, (zero, zero), unroll=_SCAN_UNROLL)

    blk = pl.BlockSpec((nc, _SCAN_G, 128), lambda g: (0, g, 0))
    vec = pl.BlockSpec((_SCAN_G, 128), lambda g: (g, 0))
    return _pcall(body, name=name, grid=(G // _SCAN_G,), in_specs=[blk, vec, vec], out_specs=blk,
                  out_shape=_sds((nc, G, 128), F32), scratch=[pltpu.VMEM((nc, _SCAN_G, 128), F32)],
                  sem=("parallel",))(s_t, a1, a2)


def _s5_scan_bwd(dxp_t, xp_t, a1, a2, name):
    nc, G, _ = dxp_t.shape

    def body(dx_ref, x_ref, a1_ref, a2_ref, ds_ref, da1_ref, da2_ref, dxb_ref, xb_ref):
        dxb_ref[...] = pltpu.roll(dx_ref[...], 64, 2)
        xb_ref[...] = pltpu.roll(x_ref[...], 64, 2)
        a1v = a1_ref[...]
        a2v = a2_ref[...]
        zero = jnp.zeros((_SCAN_G, 128), F32)

        def step(n, carry):
            gc, gb, d1, d2 = carry
            c = nc - 1 - n
            ds_ref[c] = gc
            d1 = d1 + gc * x_ref[c]
            d2 = d2 + gc * xb_ref[c]
            return dx_ref[c] + a1v * gc - a2v * gb, dxb_ref[c] + a1v * gb + a2v * gc, d1, d2

        _, _, d1, d2 = lax.fori_loop(0, nc, step, (zero, zero, zero, zero), unroll=_SCAN_UNROLL)
        da1_ref[...] = d1
        da2_ref[...] = d2

    blk = pl.BlockSpec((nc, _SCAN_G, 128), lambda g: (0, g, 0))
    vec = pl.BlockSpec((_SCAN_G, 128), lambda g: (g, 0))
    return _pcall(body, name=name, grid=(G // _SCAN_G,), in_specs=[blk, blk, vec, vec],
                  out_specs=[blk, vec, vec],
                  out_shape=[_sds((nc, G, 128), F32), _sds((G, 128), F32), _sds((G, 128), F32)],
                  scratch=[pltpu.VMEM((nc, _SCAN_G, 128), F32)] * 2, sem=("parallel",))(dxp_t, xp_t, a1, a2)


def _toeplitz_rows(r0, mt):
    lane = lax.broadcasted_iota(jnp.int32, (S5_GROUP, S5_LW), 1)
    mt[0:S5_GROUP, :] = r0
    for s in range(1, S5_LC):
        mt[s * S5_GROUP:(s + 1) * S5_GROUP, :] = jnp.where(lane >= s * S5_GROUP, pltpu.roll(r0, s * S5_GROUP, 1), 0.0)


def _toeplitz_rows_t(dmt):
    lane = lax.broadcasted_iota(jnp.int32, (S5_GROUP, S5_LW), 1)
    acc = dmt[0:S5_GROUP, :]
    for s in range(1, S5_LC):
        blk = dmt[s * S5_GROUP:(s + 1) * S5_GROUP, :]
        acc = acc + jnp.where(lane < S5_LW - s * S5_GROUP, pltpu.roll(blk, S5_LW - s * S5_GROUP, 1), 0.0)
    return acc


def _s5_output(umat, xprev, r0, qt, name):
    G, nc, _ = umat.shape

    def body(u_ref, x_ref, r_ref, q_ref, o_ref, mt):
        _toeplitz_rows(r_ref[...], mt)
        o_ref[...] = _dot(u_ref[...], mt[...], HI) + _dot(x_ref[...], q_ref[...], HI)

    return _pcall(body, name=name, grid=(G,),
                  in_specs=[_gspec(nc, S5_LW), _gspec(nc, 128), _gspec(S5_GROUP, S5_LW), _gspec(128, S5_LW)],
                  out_specs=_gspec(nc, S5_LW), out_shape=_sds((G, nc, S5_LW), F32),
                  scratch=[pltpu.VMEM((S5_LW, S5_LW), F32)], sem=("parallel",))(umat, xprev, r0, qt)


def _s5_bwd_state(dymat, qt, name):
    G, nc, _ = dymat.shape

    def body(d_ref, q_ref, o_ref):
        o_ref[...] = _dot_nt(d_ref[...], q_ref[...], HI)

    return _pcall(body, name=name, grid=(G,), in_specs=[_gspec(nc, S5_LW), _gspec(128, S5_LW)],
                  out_specs=_gspec(nc, 128), out_shape=_sds((G, nc, 128), F32), sem=("parallel",))(dymat, qt)


def _s5_bwd_main(umat, xprev, dymat, ds, r0, pm, name):
    G, nc, _ = umat.shape

    def body(u_ref, x_ref, dy_ref, ds_ref, r_ref, p_ref, du_ref, dr_ref, dq_ref, dp_ref, mt):
        u = u_ref[...]
        dy = dy_ref[...]
        dsv = ds_ref[...]
        _toeplitz_rows(r_ref[...], mt)
        du_ref[...] = (_dot_nt(dy, mt[...], HI) + _dot_nt(dsv, p_ref[...], HI)).astype(BF16)
        dq_ref[...] = _dot_tn(x_ref[...], dy, HI)
        dp_ref[...] = _dot_tn(u, dsv, HI)
        mt[...] = _dot_tn(u, dy, HI)
        dr_ref[...] = _toeplitz_rows_t(mt)

    return _pcall(body, name=name, grid=(G,),
                  in_specs=[_gspec(nc, S5_LW), _gspec(nc, 128), _gspec(nc, S5_LW), _gspec(nc, 128),
                            _gspec(S5_GROUP, S5_LW), _gspec(S5_LW, 128)],
                  out_specs=[_gspec(nc, S5_LW), _gspec(S5_GROUP, S5_LW), _gspec(128, S5_LW), _gspec(S5_LW, 128)],
                  out_shape=[_sds((G, nc, S5_LW), BF16), _sds((G, S5_GROUP, S5_LW), F32), _sds((G, 128, S5_LW), F32),
                             _sds((G, S5_LW, 128), F32)],
                  scratch=[pltpu.VMEM((S5_LW, S5_LW), F32)], sem=("parallel",))(umat, xprev, dymat, ds, r0, pm)


def _glu_fwd(y, wglu, bglu, l, name):
    T = y.shape[0]

    def body(y_ref, w_ref, b_ref, o_ref, g_ref):
        g = _gelu(y_ref[...])
        gh = g.astype(BF16)
        z = _dot(gh, w_ref[...]) + b_ref[...]
        o_ref[...] = (g * _sigmoid(z)).astype(BF16)
        g_ref[...] = gh

    return _pcall(body, name=name, grid=(T // TM,),
                  in_specs=[_rows(TM, MIX), _layer((MIX, MIX), l), _layer((1, MIX), l)],
                  out_specs=[_rows(TM, MIX), _rows(TM, MIX)],
                  out_shape=[_sds((T, MIX), BF16), _sds((T, MIX), BF16)], sem=("parallel",))(y, wglu, bglu)


def _glu_bwd(y, dout, wglu, bglu, l, name):
    T = y.shape[0]

    def body(y_ref, do_ref, w_ref, b_ref, dy_ref, dz_ref, db_ref):
        yv = y_ref[...]
        do = do_ref[...]
        g = _gelu(yv)
        s = _sigmoid(_dot(g.astype(BF16), w_ref[...]) + b_ref[...])
        dz = do * g * s * (1.0 - s)
        dzh = dz.astype(BF16)
        dz_ref[...] = dzh
        dg = do * s + _dot_nt(dzh, w_ref[...])
        dy_ref[...] = dg * _gelu_grad(yv)

        @pl.when(pl.program_id(0) == 0)
        def _():
            db_ref[...] = jnp.zeros_like(db_ref)

        db_ref[...] += _colsum(dz)

    return _pcall(body, name=name, grid=(T // TM,),
                  in_specs=[_rows(TM, MIX), _rows(TM, MIX), _layer((MIX, MIX), l), _layer((1, MIX), l)],
                  out_specs=[_rows(TM, MIX), _rows(TM, MIX), _full((1, MIX))],
                  out_shape=[_sds((T, MIX), F32), _sds((T, MIX), BF16), _sds((1, MIX), F32)])(y, dout, wglu, bglu)


def _prev_rows(T, h, s=0):
    return pl.BlockSpec((h, MIX), lambda i: (jnp.maximum(i * (TM // h) - 1, 0), s))


def _next_rows(T, h, s=0):
    return pl.BlockSpec((h, MIX), lambda i: (jnp.minimum((i + 1) * (TM // h), T // h - 1), s))


def _conv_fwd(h, w, l, name):
    T = h.shape[0]

    def body(b_ref, c_ref, x_ref, ch_ref, xh_ref, w_ref, o_ref, ext):
        i = pl.program_id(0)
        z = c_ref[...] * x_ref[...]
        ext[0:8, :] = jnp.where(i > 0, ch_ref[...] * xh_ref[...], 0.0)
        ext[8:, :] = z
        y = (w_ref[0:1, :] * ext[pl.ds(6, TM), :] + w_ref[1:2, :] * ext[pl.ds(7, TM), :] + w_ref[2:3, :] * z)
        o_ref[...] = (b_ref[...] * y).astype(BF16)

    return _pcall(body, name=name, grid=(T // TM,),
                  in_specs=[_cols(TM, 1), _cols(TM, 2), _cols(TM, 3), _prev_rows(T, 8, 2), _prev_rows(T, 8, 3),
                            _layer((3, MIX), l)],
                  out_specs=_rows(TM, MIX), out_shape=_sds((T, MIX), BF16),
                  scratch=[pltpu.VMEM((TM + 8, MIX), F32)], sem=("parallel",))(h, h, h, h, h, w)


def _conv_bwd(dout, h, dua, w, l, name):
    T = h.shape[0]
    nb = T // TM

    def body(do_ref, b_ref, c_ref, x_ref, ch_ref, xh_ref, don_ref, bn_ref, du_ref, w_ref,
             dh_ref, dw_ref, ext, ext2):
        i = pl.program_id(0)
        c = c_ref[...]
        x = x_ref[...]
        z = c * x
        ext[0:8, :] = jnp.where(i > 0, ch_ref[...] * xh_ref[...], 0.0)
        ext[8:, :] = z
        zm2 = ext[pl.ds(6, TM), :]
        zm1 = ext[pl.ds(7, TM), :]
        w0 = w_ref[0:1, :]
        w1 = w_ref[1:2, :]
        w2 = w_ref[2:3, :]
        y = w0 * zm2 + w1 * zm1 + w2 * z
        do = do_ref[...]
        dy = do * b_ref[...]
        ext2[0:TM, :] = dy
        ext2[TM:, :] = jnp.where(i < nb - 1, don_ref[...] * bn_ref[...], 0.0)
        dz = w2 * dy + w1 * ext2[pl.ds(1, TM), :] + w0 * ext2[pl.ds(2, TM), :]
        dh_ref[:, 0:MIX] = du_ref[...]
        dh_ref[:, MIX:2 * MIX] = (do * y).astype(BF16)
        dh_ref[:, 2 * MIX:3 * MIX] = (dz * x).astype(BF16)
        dh_ref[:, 3 * MIX:] = (dz * c).astype(BF16)

        @pl.when(i == 0)
        def _():
            dw_ref[...] = jnp.zeros_like(dw_ref)

        dw_ref[0:1, :] += _colsum(dy * zm2)
        dw_ref[1:2, :] += _colsum(dy * zm1)
        dw_ref[2:3, :] += _colsum(dy * z)

    return _pcall(body, name=name, grid=(nb,),
                  in_specs=[_rows(TM, MIX), _cols(TM, 1), _cols(TM, 2), _cols(TM, 3), _prev_rows(T, 8, 2),
                            _prev_rows(T, 8, 3), _next_rows(T, 8), _next_rows(T, 8, 1), _rows(TM, MIX),
                            _layer((3, MIX), l)],
                  out_specs=[_rows(TM, 4 * MIX), _full((8, MIX))],
                  out_shape=[_sds((T, 4 * MIX), BF16), _sds((8, MIX), F32)],
                  scratch=[pltpu.VMEM((TM + 8, MIX), F32), pltpu.VMEM((TM + 8, MIX), F32)])(
                      dout, h, h, h, h, h, dout, h, dua, w)


_PH = 16


def _pooled(ext, t, gi, w):
    lo = gi * POOL_GROUP
    cur = ext[pl.ds(_PH, TM), lo:lo + POOL_GROUP]
    acc = cur
    for k in range(1, w):
        acc = acc + ext[pl.ds(_PH - k, TM), lo:lo + POOL_GROUP]
    cnt = jnp.minimum(t + 1, w).astype(F32)
    return acc / cnt - cur, cnt


def _pool_fwd(h, pw, scale, l, name):
    T = h.shape[0]

    def body(z_ref, zh_ref, pw_ref, sc_ref, o_ref, ext):
        i = pl.program_id(0)
        ext[0:_PH, :] = jnp.where(i > 0, zh_ref[...], 0.0)
        ext[_PH:, :] = z_ref[...]
        t = i * TM + lax.broadcasted_iota(jnp.int32, (TM, 1), 0)
        for gi, w in enumerate(POOL_WINDOWS):
            lo = gi * POOL_GROUP
            pooled, _ = _pooled(ext, t, gi, w)
            mixed = _dot(pooled.astype(BF16), pw_ref[gi].astype(BF16))
            o_ref[:, lo:lo + POOL_GROUP] = (mixed * sc_ref[:, lo:lo + POOL_GROUP]).astype(BF16)

    return _pcall(body, name=name, grid=(T // TM,),
                  in_specs=[_cols(TM, 3), _prev_rows(T, _PH, 3), _layer((4, POOL_GROUP, POOL_GROUP), l),
                            _layer((1, MIX), l)],
                  out_specs=_rows(TM, MIX), out_shape=_sds((T, MIX), BF16),
                  scratch=[pltpu.VMEM((TM + _PH, MIX), F32)], sem=("parallel",))(h, h, pw, scale)


def _pool_bwd(dout, h, dq, dk, dv, pw, scale, l, name):
    T = h.shape[0]
    nb = T // TM

    def body(do_ref, don_ref, z_ref, zh_ref, dq_ref, dk_ref, dv_ref, pw_ref, sc_ref,
             dh_ref, dpw_ref, dsc_ref, ext, ext2):
        i = pl.program_id(0)
        ext[0:_PH, :] = jnp.where(i > 0, zh_ref[...], 0.0)
        ext[_PH:, :] = z_ref[...]
        t = i * TM + lax.broadcasted_iota(jnp.int32, (TM, 1), 0)
        dh_ref[:, 0:MIX] = dq_ref[...]
        dh_ref[:, MIX:2 * MIX] = dk_ref[...]
        dh_ref[:, 2 * MIX:3 * MIX] = dv_ref[...]

        @pl.when(i == 0)
        def _():
            dpw_ref[...] = jnp.zeros_like(dpw_ref)
            dsc_ref[...] = jnp.zeros_like(dsc_ref)

        for gi, w in enumerate(POOL_WINDOWS):
            lo = gi * POOL_GROUP
            pwb = pw_ref[gi].astype(BF16)
            sc = sc_ref[:, lo:lo + POOL_GROUP]
            pooled, cnt = _pooled(ext, t, gi, w)
            pb = pooled.astype(BF16)
            mixed = _dot(pb, pwb)
            dog = do_ref[:, lo:lo + POOL_GROUP]
            dsc_ref[:, lo:lo + POOL_GROUP] += _colsum(dog * mixed)
            dmix = (dog * sc).astype(BF16)
            dpw_ref[gi] += _dot_tn(pb, dmix)
            dpool = _dot_nt(dmix, pwb)
            dmix_n = (jnp.where(i < nb - 1, don_ref[:, lo:lo + POOL_GROUP], 0.0) * sc).astype(BF16)
            dpool_n = _dot_nt(dmix_n, pwb)
            e = dpool / cnt
            ext2[0:TM, lo:lo + POOL_GROUP] = e
            ext2[TM:, lo:lo + POOL_GROUP] = dpool_n * (1.0 / w)
            s = e
            for k in range(1, w):
                s = s + ext2[pl.ds(k, TM), lo:lo + POOL_GROUP]
            dh_ref[:, 3 * MIX + lo:3 * MIX + lo + POOL_GROUP] = (s - dpool).astype(BF16)

    blk = _rows(TM, MIX)
    return _pcall(body, name=name, grid=(nb,),
                  in_specs=[blk, _next_rows(T, _PH), _cols(TM, 3), _prev_rows(T, _PH, 3), blk, blk, blk,
                            _layer((4, POOL_GROUP, POOL_GROUP), l), _layer((1, MIX), l)],
                  out_specs=[_rows(TM, 4 * MIX), _full((4, POOL_GROUP, POOL_GROUP)), _full((1, MIX))],
                  out_shape=[_sds((T, 4 * MIX), BF16), _sds((4, POOL_GROUP, POOL_GROUP), F32), _sds((1, MIX), F32)],
                  scratch=[pltpu.VMEM((TM + _PH, MIX), F32), pltpu.VMEM((TM + _PH, MIX), F32)])(
                      dout, dout, h, h, dq, dk, dv, pw, scale)


def _att_bias_table(rel_bias):
    span = LEFT_CHUNKS * CHUNK
    assert ATT_TQ - 1 <= MAX_REL
    lo = MAX_REL - (ATT_TQ - 1)
    near = rel_bias[:, lo:2 * MAX_REL + 1]
    far = jnp.broadcast_to(rel_bias[:, 2 * MAX_REL:], (HEADS, span + ATT_TQ - 1 - MAX_REL))
    by_dist = jnp.concatenate([near, far], axis=1)
    rev = by_dist[:, ::-1]
    lv = rev.shape[1]
    skew = jnp.tile(rev, (1, ATT_TQ + 1))[:, :ATT_TQ * (lv + 1)].reshape(HEADS, ATT_TQ, lv + 1)[:, :, :ATT_W]
    bias = skew[:, ::-1, :]
    r = jnp.arange(ATT_TQ)[:, None]
    col = jnp.arange(ATT_W)[None, :]
    dchunk = (LEFT_CHUNKS + r // CHUNK) - col // CHUNK
    visible = (dchunk >= 0) & (dchunk <= LEFT_CHUNKS)
    return jnp.where(visible[None], bias, NEG_INF)


def _qrows(n, s=0):
    return pl.BlockSpec((ATT_TQ, n), lambda i: (i, s))


def _att_views_back(d, s):
    return pl.BlockSpec((ATT_TQ, MIX), lambda i: (jnp.maximum(i - (ATT_NV - 1) + d, 0), s))


_PAIR = 2 * HEAD_DIM


def _att_pair(refs, pp):
    sl = slice(pp * _PAIR, (pp + 1) * _PAIR)
    if isinstance(refs, (tuple, list)):
        return jnp.concatenate([r[:, sl] for r in refs], axis=0)
    return refs[:, sl]


def _att_fwd(h, table, name):
    T = h.shape[0]

    def body(q_ref, *refs):
        k_refs = refs[:ATT_NV]
        v_refs = refs[ATT_NV:2 * ATT_NV]
        tb_ref = refs[2 * ATT_NV]
        o_ref, oh_ref, lse_ref = refs[2 * ATT_NV + 1:]
        i = pl.program_id(0)
        col = lax.broadcasted_iota(jnp.int32, (1, ATT_W), 1)
        kvalid = (col + (i - (ATT_NV - 1)) * ATT_TQ) >= 0
        first = lax.broadcasted_iota(jnp.int32, (1, _PAIR), 1) < HEAD_DIM
        lses = []
        for pp in range(HEADS // 2):
            qp = _att_pair(q_ref, pp) * (HEAD_DIM ** -0.5)
            kp = _att_pair(k_refs, pp).astype(BF16)
            vp = _att_pair(v_refs, pp).astype(BF16)
            outs = []
            for e in range(2):
                half = first if e == 0 else jnp.logical_not(first)
                s = _dot_nt(jnp.where(half, qp, 0.0).astype(BF16), kp) + tb_ref[2 * pp + e]
                s = jnp.where(kvalid, s, NEG_INF)
                m = jnp.max(s, axis=-1, keepdims=True)
                p = jnp.exp(s - m)
                l = jnp.sum(p, axis=-1, keepdims=True)
                outs.append(_dot(p.astype(BF16), vp) / l)
                lses.append(m + jnp.log(l))
            o = jnp.where(first, outs[0], outs[1])
            o_ref[:, pp * _PAIR:(pp + 1) * _PAIR] = o
            oh_ref[:, pp * _PAIR:(pp + 1) * _PAIR] = o.astype(BF16)
        lse_ref[...] = jnp.concatenate(lses, axis=1)

    kviews = [_att_views_back(d, 1) for d in range(ATT_NV)]
    vviews = [_att_views_back(d, 2) for d in range(ATT_NV)]
    return _pcall(body, name=name, grid=(T // ATT_TQ,),
                  in_specs=[_qrows(MIX)] + kviews + vviews + [_full((HEADS, ATT_TQ, ATT_W))],
                  out_specs=[_qrows(MIX), _qrows(MIX), _qrows(HEADS)],
                  out_shape=[_sds((T, MIX), F32), _sds((T, MIX), BF16), _sds((T, HEADS), F32)],
                  sem=("parallel",))(h, *([h] * (2 * ATT_NV)), table)


def _att_bwd_q(h, do, o, lse, table, name):
    T = h.shape[0]

    def body(q_ref, *refs):
        k_refs = refs[:ATT_NV]
        v_refs = refs[ATT_NV:2 * ATT_NV]
        do_ref, o_ref, lse_ref, tb_ref, dq_ref, dl_ref, dtb_ref = refs[2 * ATT_NV:]
        i = pl.program_id(0)
        col = lax.broadcasted_iota(jnp.int32, (1, ATT_W), 1)
        kvalid = (col + (i - (ATT_NV - 1)) * ATT_TQ) >= 0

        first = lax.broadcasted_iota(jnp.int32, (1, _PAIR), 1) < HEAD_DIM

        @pl.when(i == 0)
        def _():
            dtb_ref[...] = jnp.zeros_like(dtb_ref)

        deltas = []
        for pp in range(HEADS // 2):
            qp = _att_pair(q_ref, pp) * (HEAD_DIM ** -0.5)
            kp = _att_pair(k_refs, pp).astype(BF16)
            vp = _att_pair(v_refs, pp).astype(BF16)
            dop = _att_pair(do_ref, pp)
            doo = dop * _att_pair(o_ref, pp)
            outs = []
            for e in range(2):
                hd = 2 * pp + e
                half = first if e == 0 else jnp.logical_not(first)
                s = _dot_nt(jnp.where(half, qp, 0.0).astype(BF16), kp) + tb_ref[hd]
                s = jnp.where(kvalid, s, NEG_INF)
                p = jnp.exp(s - lse_ref[:, hd:hd + 1])
                delta = jnp.sum(jnp.where(half, doo, 0.0), axis=-1, keepdims=True)
                dp = _dot_nt(jnp.where(half, dop, 0.0).astype(BF16), vp)
                ds = p * (dp - delta)
                dtb_ref[hd] += ds
                outs.append(_dot(ds.astype(BF16), kp))
                deltas.append(delta)
            dq = jnp.where(first, outs[0], outs[1]) * (HEAD_DIM ** -0.5)
            dq_ref[:, pp * _PAIR:(pp + 1) * _PAIR] = dq.astype(BF16)
        dl_ref[...] = jnp.concatenate(deltas, axis=1)

    kviews = [_att_views_back(d, 1) for d in range(ATT_NV)]
    vviews = [_att_views_back(d, 2) for d in range(ATT_NV)]
    tb = _full((HEADS, ATT_TQ, ATT_W))
    return _pcall(body, name=name, grid=(T // ATT_TQ,),
                  in_specs=[_qrows(MIX)] + kviews + vviews + [_qrows(MIX), _qrows(MIX), _qrows(HEADS), tb],
                  out_specs=[_qrows(MIX), _qrows(HEADS), tb],
                  out_shape=[_sds((T, MIX), BF16), _sds((T, HEADS), F32), _sds((HEADS, ATT_TQ, ATT_W), F32)])(
                      h, *([h] * (2 * ATT_NV)), do, o, lse, table)


def _att_table_by_key(table):
    t = table.reshape(HEADS, ATT_TQ, ATT_NV, ATT_TQ)[:, :, ::-1, :]
    return jnp.transpose(t, (0, 3, 2, 1)).reshape(HEADS, ATT_TQ, ATT_W)


def _att_bwd_kv(h, do, lse_t, delta_t, table_k, name):
    T = h.shape[0]
    nb = T // ATT_TQ

    def fwd_view(d, s=0):
        return pl.BlockSpec((ATT_TQ, MIX), lambda j: (jnp.minimum(j + d, nb - 1), s))

    def row_view(d):
        return pl.BlockSpec((HEADS, ATT_TQ), lambda j: (0, jnp.minimum(j + d, nb - 1)))

    def body(k_ref, v_ref, *refs):
        q_refs = refs[:ATT_NV]
        do_refs = refs[ATT_NV:2 * ATT_NV]
        lse_refs = refs[2 * ATT_NV:3 * ATT_NV]
        dl_refs = refs[3 * ATT_NV:4 * ATT_NV]
        tb_ref, dk_ref, dv_ref = refs[4 * ATT_NV:]
        j = pl.program_id(0)
        view = lax.broadcasted_iota(jnp.int32, (1, ATT_W), 1) // ATT_TQ
        valid = (j + view) <= nb - 1
        first = lax.broadcasted_iota(jnp.int32, (1, _PAIR), 1) < HEAD_DIM
        for pp in range(HEADS // 2):
            kp = _att_pair(k_ref, pp)
            vp = _att_pair(v_ref, pp)
            qs = (_att_pair(q_refs, pp) * (HEAD_DIM ** -0.5)).astype(BF16)
            dos = _att_pair(do_refs, pp).astype(BF16)
            dks, dvs = [], []
            for e in range(2):
                hd = 2 * pp + e
                half = first if e == 0 else jnp.logical_not(first)
                lses = jnp.concatenate([r[hd:hd + 1, :] for r in lse_refs], axis=1)
                dls = jnp.concatenate([r[hd:hd + 1, :] for r in dl_refs], axis=1)
                st = _dot_nt(jnp.where(half, kp, 0.0).astype(BF16), qs) + tb_ref[hd]
                pt = jnp.where(valid, jnp.exp(st - lses), 0.0)
                dvs.append(_dot(pt.astype(BF16), dos))
                dst = pt * (_dot_nt(jnp.where(half, vp, 0.0).astype(BF16), dos) - dls)
                dks.append(_dot(dst.astype(BF16), qs))
            dk_ref[:, pp * _PAIR:(pp + 1) * _PAIR] = jnp.where(first, dks[0], dks[1]).astype(BF16)
            dv_ref[:, pp * _PAIR:(pp + 1) * _PAIR] = jnp.where(first, dvs[0], dvs[1]).astype(BF16)

    qv = [fwd_view(d, 0) for d in range(ATT_NV)]
    dov = [fwd_view(d) for d in range(ATT_NV)]
    rows = [row_view(d) for d in range(ATT_NV)]
    return _pcall(body, name=name, grid=(nb,),
                  in_specs=[_qrows(MIX, 1), _qrows(MIX, 2)] + qv + dov + rows + rows
                  + [_full((HEADS, ATT_TQ, ATT_W))],
                  out_specs=[_qrows(MIX), _qrows(MIX)], out_shape=[_sds((T, MIX), BF16), _sds((T, MIX), BF16)],
                  sem=("parallel",))(h, h, *([h] * ATT_NV), *([do] * ATT_NV), *([lse_t] * ATT_NV),
                                     *([delta_t] * ATT_NV), table_k)


def _all_gather(x, name):
    R, C = x.shape

    def body(x_ref, out_ref, send_sems, recv_sems, local_sem):
        xi, yi, ci = lax.axis_index("x"), lax.axis_index("y"), lax.axis_index("c")
        me, sibling = (xi, yi, ci), (xi, yi, 1 - ci)
        chips = [(1 - xi, yi), (xi, 1 - yi), (1 - xi, 1 - yi)]

        def slot(px, py, pc):
            return out_ref.at[4 * px + 2 * py + pc]

        def copy(k, block, to, src=None):
            return pltpu.make_async_remote_copy(
                src_ref=slot(*block) if src is None else src, dst_ref=slot(*block),
                send_sem=send_sems.at[k], recv_sem=recv_sems.at[k], device_id=to, device_id_type=_MESH)

        mine = pltpu.make_async_copy(x_ref, slot(*me), local_sem)
        mine.start()
        first = [copy(0, me, sibling, src=x_ref)]
        first += [copy(1 + j, me, (*chip, ci), src=x_ref) for j, chip in enumerate(chips)]
        for cp in first:
            cp.start()
        passed = [copy(4 + j, (*chip, ci), sibling) for j, chip in enumerate(chips)]
        for j, chip in enumerate(chips):
            copy(1 + j, (*chip, ci), me).wait_recv()
            passed[j].start()
        copy(0, sibling, me).wait_recv()
        for j, chip in enumerate(chips):
            copy(4 + j, (*chip, 1 - ci), me).wait_recv()
        for cp in first + passed:
            cp.wait_send()
        mine.wait()

    return pl.pallas_call(
        body, name=name, out_shape=_sds((N_DEV, R, C), x.dtype), in_specs=[_ANY], out_specs=_ANY,
        scratch_shapes=[pltpu.SemaphoreType.DMA((7,)), pltpu.SemaphoreType.DMA((7,)), pltpu.SemaphoreType.DMA(())],
    )(x)


def _a2a_copies(s_ref, r_ref, send_sems, recv_sems, local_sem):
    xi, yi, ci = lax.axis_index("x"), lax.axis_index("y"), lax.axis_index("c")
    me = 4 * xi + 2 * yi + ci

    def mine():
        return pltpu.make_async_copy(s_ref.at[me], r_ref.at[me], local_sem)

    def remote(m, sending):
        px = 1 - xi if m & 4 else xi
        py = 1 - yi if m & 2 else yi
        pc = 1 - ci if m & 1 else ci
        peer = 4 * px + 2 * py + pc
        src, dst = (s_ref.at[peer], r_ref.at[me]) if sending else (s_ref.at[me], r_ref.at[peer])
        return pltpu.make_async_remote_copy(src_ref=src, dst_ref=dst, send_sem=send_sems.at[m - 1],
                                            recv_sem=recv_sems.at[m - 1], device_id=(px, py, pc), device_id_type=_MESH)

    def start():
        mine().start()
        for m in range(1, N_DEV):
            remote(m, True).start()

    def wait():
        for m in range(1, N_DEV):
            remote(m, False).wait_recv()
        for m in range(1, N_DEV):
            remote(m, True).wait_send()
        mine().wait()

    return start, wait


_A2A_SEMS = [pltpu.SemaphoreType.DMA((7,)), pltpu.SemaphoreType.DMA((7,)), pltpu.SemaphoreType.DMA(())]


def _all_to_all(s, name):
    def body(s_ref, r_ref, send_sems, recv_sems, local_sem):
        start, wait = _a2a_copies(s_ref, r_ref, send_sems, recv_sems, local_sem)
        start()
        wait()

    return pl.pallas_call(
        body, name=name, out_shape=_sds(s.shape, s.dtype), in_specs=[_ANY], out_specs=_ANY,
        scratch_shapes=list(_A2A_SEMS))(s)


def _riding(body, n_in, n_out, n_ex, first, last):
    def wrapped(*refs):
        ins = refs[:n_in]
        sends = refs[n_in:n_in + n_ex]
        outs = refs[n_in + n_ex:n_in + n_ex + n_out]
        recvs = refs[n_in + n_ex + n_out:n_in + 2 * n_ex + n_out]
        rest = refs[n_in + 2 * n_ex + n_out:]
        scratch, sems = rest[:len(rest) - 3 * n_ex], rest[len(rest) - 3 * n_ex:]

        @pl.when(first())
        def _():
            for k in range(n_ex):
                _a2a_copies(sends[k], recvs[k], *sems[3 * k:3 * k + 3])[0]()

        body(*ins, *outs, *scratch)

        @pl.when(last())
        def _():
            for k in range(n_ex):
                _a2a_copies(sends[k], recvs[k], *sems[3 * k:3 * k + 3])[1]()

    return wrapped


_ADAMW_BLOCK_BYTES = 6 * 1024 * 1024


def _adamw(parts, row_off, w, m, v, name):
    R, C = w.shape
    tr = None
    for cand in (512, 256, 128, 64, 32, 16):
        step_bytes = cand * C * (N_DEV * parts.dtype.itemsize + 7 * 4)
        if R % cand == 0 and row_off % cand == 0 and step_bytes <= _ADAMW_BLOCK_BYTES:
            tr = cand
            break
    assert tr is not None, (R, C, row_off)
    off = row_off // tr
    c1 = 1.0 - ADAM_B1 ** ADAM_STEP
    c2 = 1.0 - ADAM_B2 ** ADAM_STEP

    def body(p_ref, w_ref, m_ref, v_ref, g_ref, d_ref, mo_ref, vo_ref):
        g = p_ref[0].astype(F32)
        for j in range(1, N_DEV):
            g = g + p_ref[j].astype(F32)
        mn = ADAM_B1 * m_ref[...] + (1.0 - ADAM_B1) * g
        vn = ADAM_B2 * v_ref[...] + (1.0 - ADAM_B2) * (g * g)
        m_hat = mn / c1
        v_hat = vn / c2
        g_ref[...] = g
        d_ref[...] = -ADAM_LR * (m_hat / (jnp.sqrt(v_hat) + ADAM_EPS) + ADAM_WD * w_ref[...])
        mo_ref[...] = mn
        vo_ref[...] = vn

    blk = _rows(tr, C)
    return _pcall(body, name=name, grid=(R // tr,),
                  in_specs=[pl.BlockSpec((N_DEV, tr, C), lambda i: (0, off + i, 0)), blk, blk, blk],
                  out_specs=[blk] * 4, out_shape=[_sds((R, C), F32)] * 4,
                  sem=("parallel",))(parts, w, m, v)


def _piece_rows(size):
    return -(-size // PIECE) * 16


def _pack_rows(arrays, lead=()):
    parts = []
    total = 0
    for a in arrays:
        flat = a.reshape(lead + (-1,))
        size = flat.shape[-1]
        rows = _piece_rows(size)
        pad = [(0, 0)] * len(lead) + [(0, rows * FLAT_COLS - size)]
        parts.append(jnp.pad(flat, pad).reshape(lead + (rows, FLAT_COLS)))
        total += rows
    tail = -total % FLAT_ROWS
    if tail:
        parts.append(jnp.zeros(lead + (tail, FLAT_COLS), parts[0].dtype))
    return jnp.concatenate(parts, axis=len(lead))


def _unpack_rows(buf, shapes, lead=()):
    out = []
    row = 0
    nl = len(lead)
    for shape in shapes:
        size = math.prod(shape)
        rows = _piece_rows(size)
        piece = lax.slice_in_dim(buf, row, row + rows, axis=nl).reshape(lead + (rows * FLAT_COLS,))
        out.append(lax.slice_in_dim(piece, 0, size, axis=nl).reshape(lead + tuple(shape)))
        row += rows
    return out


def _to_blocks(full, axis):
    shp = full.shape
    split = full.reshape(shp[:axis] + (N_DEV, shp[axis] // N_DEV) + shp[axis + 1:])
    return jnp.moveaxis(split, axis, 0)


def _from_blocks(blocks, axis):
    shp = blocks.shape[1:]
    moved = jnp.moveaxis(blocks, 0, axis)
    return moved.reshape(shp[:axis] + (N_DEV * shp[axis],) + shp[axis + 1:])


def _layer_sends(G, i):
    j = i // 2
    ev = i % 2 == 0
    rows = [G['ev_w_out' if ev else 'od_w_out'][j], G['ffn_w_down'][i], G['ple_w_gate'][i]]
    rows = jnp.concatenate([g.reshape(N_DEV, g.shape[0] // N_DEV, g.shape[1]) for g in rows], axis=1)
    return dict(rows=rows, up=G['ffn_w_up'][i], inn=G['ev_w_in' if ev else 'od_w_in'][j])


def _local_step(x, p, tgt, W, overlap=False):
    wup = W['ffn_w_up'].reshape(2, FF_J, DEPTH, D_MODEL, FF_NB)
    wd4 = W['ffn_w_down'].reshape(DEPTH, FF_J, FF_NB, D_MODEL)
    ln = {n: _vecs(W[n]) for n in ('ln_mix_g', 'ln_mix_b', 'ln_ffn_g', 'ln_ffn_b', 'ple_b_gate')}
    bglu = _vecs(W['ev_b_glu'])
    pscale = _vecs(W['od_pool_scale'])
    saved = []
    x0 = x
    x0h = x.astype(BF16)
    for i in range(DEPTH):
        L = f"L{i}_"
        j = i // 2
        s = dict(x0=x0, x0h=x0h)
        if i % 2 == 0:
            h = _mm_in(x0h, W['ev_w_in'], j, L + "mm_in")
            params = tuple(W[n][j] for n in ('ev_lambda_re', 'ev_lambda_im', 'ev_log_dt', 'ev_b_re', 'ev_b_im',
                                             'ev_c_re', 'ev_c_im', 'ev_d'))
            (r0, qt, pm, a1, a2), op_vjp = jax.vjp(_s5_operators, *params)
            umat = _to_chunks(h[:, :MIX])
            st = jnp.transpose(_s5_chunk_state(umat, pm, L + "s5_state"), (1, 0, 2))
            xp_t = _s5_scan_fwd(st, a1, a2, L + "s5_scan")
            xprev = jnp.transpose(xp_t, (1, 0, 2))
            y = _from_chunks(_s5_output(umat, xprev, r0, qt, L + "s5_out"))
            ya, gh = _glu_fwd(y, W['ev_w_glu'], bglu, j, L + "glu")
            yb = _conv_fwd(h, W['ev_conv_w'], j, L + "conv")
            s.update(op_vjp=op_vjp, r0=r0, qt=qt, pm=pm, a1=a1, a2=a2, umat=umat, xp_t=xp_t, xprev=xprev, y=y, gh=gh)
            wo, win = W['ev_w_out'], W['ev_w_in']
        else:
            h = _mm_in(x0h, W['od_w_in'], j, L + "mm_in")
            table, tb_vjp = jax.vjp(_att_bias_table, W['od_rel_bias'][j])
            of, ya, lse = _att_fwd(h, table, L + "att")
            yb = _pool_fwd(h, W['od_pool_w'], pscale, j, L + "pool")
            s.update(table=table, tb_vjp=tb_vjp, of=of, lse=lse)
            wo, win = W['od_w_out'], W['od_w_in']
        g1, b1 = (ln['ln_mix_g'], i), (ln['ln_mix_b'], i)
        g2, b2 = (ln['ln_ffn_g'], i), (ln['ln_ffn_b'], i)
        xhat1, rstd1, x1h = _mm_out_ln(ya, yb, wo, j, x0, g1, b1, L + "mm_out_ln")
        hf, a3 = _mm_up(x1h, wup, i, L + "mm_up")
        xhat2, rstd2, x2h = _mm_down_ln(a3, wd4, i, xhat1, g1, b1, g2, b2, L + "mm_down_ln")
        x3, x3h = _mm_ple(x2h, xhat2, ln['ln_ffn_g'], ln['ln_ffn_b'], p, W['ple_w_gate'], ln['ple_b_gate'],
                          W['ple_w_proj'], i, L + "mm_ple")
        s.update(h=h, win=win, wo=wo, ya=ya, yb=yb, xhat1=xhat1, rstd1=rstd1, x1h=x1h, hf=hf, a3=a3, xhat2=xhat2,
                 rstd2=rstd2, x2h=x2h, g1=g1)
        saved.append(s)
        x0, x0h = x3, x3h

    dx, loss = _loss_head(x0, tgt, "loss_head")

    G = {n: [None] * W[n].shape[0 if n not in ('ev_w_in', 'od_w_in', 'ffn_w_up') else 1] for n in WEIGHT_NAMES}
    landed = {}
    pending = None
    for i in reversed(range(DEPTH)):
        L = f"L{i}_"
        j = i // 2
        s = saved[i]
        dr2, dr2h, dpreh, dpph, dbg, dg2, db2 = _ple_bwd(
            dx, s['x2h'], p, W['ple_w_gate'], ln['ple_b_gate'], W['ple_w_proj'], i, s['xhat2'], s['rstd2'],
            ln['ln_ffn_g'], L + "ple_bwd")
        G['ple_w_gate'][i] = _mm_tn(s['x2h'], dpreh, L + "dw_gate")
        G['ple_w_proj'][i] = _mm_tn(p, dpph, L + "dw_proj", a_layer=i)
        G['ple_b_gate'][i] = dbg[0]
        G['ln_ffn_g'][i] = dg2[0]
        G['ln_ffn_b'][i] = db2[0]
        dhf, got1 = _ffn_bwd1(dr2h, wd4, i, s['hf'], L + "ffn_bwd1",
                              ride=(pending['rows'], pending['inn']) if pending else ())
        G['ffn_w_down'][i] = _mm_tn_ablk(s['a3'], dr2h, L + "dw_down").reshape(D_FF, D_MODEL)
        T = dhf.shape[2]
        G['ffn_w_up'][i] = _mm_tn_bblk(s['x1h'], dhf.reshape(N_DEV, T, FF_NB), L + "dw_up")
        (dr1, dr1h, dg1, db1), got2 = _ffn_bwd2(dhf, wup, i, dr2, s['xhat1'], s['rstd1'], s['g1'], L + "ffn_bwd2",
                                                ride=(pending['up'],) if pending else ())
        if pending:
            landed[i + 1] = dict(rows=got1[0], inn=got1[1], up=got2[0])
        G['ln_mix_g'][i] = dg1[0]
        G['ln_mix_b'][i] = db1[0]
        dya, dyb = _out_bwd(dr1h, s['wo'], j, L + "out_bwd")
        dwo = jnp.concatenate([_mm_tn(s['ya'], dr1h, L + "dw_out_a"), _mm_tn(s['yb'], dr1h, L + "dw_out_b")], axis=0)
        if i % 2 == 0:
            G['ev_w_out'][j] = dwo
            dy, dzh, dbglu = _glu_bwd(s['y'], dya, W['ev_w_glu'], bglu, j, L + "glu_bwd")
            G['ev_w_glu'][j] = _mm_tn(s['gh'], dzh, L + "dw_glu")
            G['ev_b_glu'][j] = dbglu[0]
            dymat = _to_chunks(dy)
            dxp_t = jnp.transpose(_s5_bwd_state(dymat, s['qt'], L + "s5_bwd_state"), (1, 0, 2))
            ds_t, da1, da2 = _s5_scan_bwd(dxp_t, s['xp_t'], s['a1'], s['a2'], L + "s5_scan_bwd")
            dumat, dr0, dqt, dpm = _s5_bwd_main(s['umat'], s['xprev'], dymat, jnp.transpose(ds_t, (1, 0, 2)),
                                                s['r0'], s['pm'], L + "s5_bwd")
            dparams = s['op_vjp']((dr0, dqt, dpm, da1, da2))
            for n, dpar in zip(('ev_lambda_re', 'ev_lambda_im', 'ev_log_dt', 'ev_b_re', 'ev_b_im', 'ev_c_re',
                                'ev_c_im', 'ev_d'), dparams):
                G[n][j] = dpar
            dua = _from_chunks(dumat)
            dh, dcw = _conv_bwd(dyb, s['h'], dua, W['ev_conv_w'], j, L + "conv_bwd")
            G['ev_conv_w'][j] = dcw[:3]
            wname = 'ev_w_in'
        else:
            G['od_w_out'][j] = dwo
            dq, delta, dtable = _att_bwd_q(s['h'], dya, s['of'], s['lse'], s['table'], L + "att_bwd_q")
            dk, dv = _att_bwd_kv(s['h'], dya, s['lse'].T, delta.T, _att_table_by_key(s['table']), L + "att_bwd_kv")
            G['od_rel_bias'][j] = s['tb_vjp'](dtable)[0]
            dh, dpw, dsc = _pool_bwd(dyb, s['h'], dq, dk, dv, W['od_pool_w'], pscale, j, L + "pool_bwd")
            G['od_pool_w'][j] = dpw
            G['od_pool_scale'][j] = dsc[0]
            wname = 'od_w_in'
        G[wname][j] = _mm_tn(s['x0h'], dh, L + "dw_in", blocked_n=IN_NB)
        dx = _in_bwd(dh, s['win'], j, dr1, L + "in_bwd")
        pending = _layer_sends(G, i) if overlap else None

    if overlap:
        return loss, dx, G, landed, pending
    return loss, dx, G


def _slab(a):
    return a.reshape(-1, a.shape[-1])


def _gather_weights(W):
    full = {n: W[n] for n in REPLICATED}
    slabs = [_slab(W[n].astype(BF16)) for n in ROW_SHARDED]
    got = _all_gather(jnp.concatenate(slabs, axis=0), "gather_w_rows")
    row = 0
    for n, sl in zip(ROW_SHARDED, slabs):
        L, k8, d = W[n].shape
        piece = lax.slice_in_dim(got, row, row + sl.shape[0], axis=1).reshape(N_DEV, L, k8, d)
        full[n] = jnp.swapaxes(piece, 0, 1).reshape(L, N_DEV * k8, d)
        row += sl.shape[0]
    got = _all_gather(_slab(W['ffn_w_up'].astype(BF16)), "gather_w_up")
    full['ffn_w_up'] = got.reshape((N_DEV,) + W['ffn_w_up'].shape)
    slabs = [_slab(W[n].astype(BF16)) for n in COL_IN]
    got = _all_gather(jnp.concatenate(slabs, axis=0), "gather_w_in")
    row = 0
    for n, sl in zip(COL_IN, slabs):
        full[n] = lax.slice_in_dim(got, row, row + sl.shape[0], axis=1).reshape((N_DEV,) + W[n].shape)
        row += sl.shape[0]
    got = _all_gather(_pack_rows([W[n].astype(BF16) for n in SMALL_SHARDED]), "gather_w_small")
    shapes = [W[n].shape for n in SMALL_SHARDED]
    for n, blocks in zip(SMALL_SHARDED, _unpack_rows(got, shapes, lead=(N_DEV,))):
        full[n] = _from_blocks(blocks, SHARD_AXIS[n])
    for n in ('ev_conv_w', 'od_pool_scale'):
        full[n] = full[n].astype(F32)
    return full


def _step(x, p, tgt, W, M, V):
    full = _gather_weights(W)
    loss, dx, G, landed, tail = _local_step(x[0], p[:, 0], tgt[0], full, overlap=True)
    landed[0] = {k: _all_to_all(v, "scatter_g_" + k) for k, v in tail.items()}
    res = {}

    def update(parts, n):
        shape = W[n].shape
        outs = _adamw(parts, 0, _slab(W[n]), _slab(M[n]), _slab(V[n]), "adamw_" + n)
        for kind, a in zip(('grad', 'delta', 'm', 'v'), outs):
            res[kind, n] = a.reshape(shape)

    def over_layers(layers, key, lo=None, hi=None):
        got = [landed[i][key] for i in layers]
        if lo is not None:
            got = [lax.slice_in_dim(g, lo, hi, axis=1) for g in got]
        return jnp.concatenate(got, axis=1)

    even, odd, every = (0, 2), (1, 3), (0, 1, 2, 3)
    r_out = D_MODEL // N_DEV
    r_down = D_FF // N_DEV
    update(over_layers(even, 'rows', 0, r_out), 'ev_w_out')
    update(over_layers(odd, 'rows', 0, r_out), 'od_w_out')
    update(over_layers(every, 'rows', r_out, r_out + r_down), 'ffn_w_down')
    update(over_layers(every, 'rows', r_out + r_down, 2 * r_out + r_down), 'ple_w_gate')
    update(over_layers(every, 'up'), 'ffn_w_up')
    update(over_layers(even, 'inn'), 'ev_w_in')
    update(over_layers(odd, 'inn'), 'od_w_in')

    shapes = [W[n].shape for n in SMALL_SHARDED]
    send = _pack_rows([_to_blocks(jnp.stack(G[n], axis=0), SHARD_AXIS[n]).astype(BF16) for n in SMALL_SHARDED],
                      lead=(N_DEV,))
    parts = _all_to_all(send, "scatter_g_small")
    outs = _adamw(parts, 0, _pack_rows([W[n] for n in SMALL_SHARDED]), _pack_rows([M[n] for n in SMALL_SHARDED]),
                  _pack_rows([V[n] for n in SMALL_SHARDED]), "adamw_small")
    for kind, buf in zip(('grad', 'delta', 'm', 'v'), outs):
        for n, a in zip(SMALL_SHARDED, _unpack_rows(buf, shapes)):
            res[kind, n] = a

    repl_shapes = [W[n].shape for n in REPLICATED]
    small = _pack_rows([jnp.stack(G[n], axis=0) for n in REPLICATED] + [loss])
    parts = _all_gather(small, "gather_g_replicated")
    zero = jnp.zeros((1, 1), F32)
    outs = _adamw(parts, 0, _pack_rows([W[n] for n in REPLICATED] + [zero]),
                  _pack_rows([M[n] for n in REPLICATED] + [zero]),
                  _pack_rows([V[n] for n in REPLICATED] + [zero]), "adamw_replicated")
    for kind, buf in zip(('grad', 'delta', 'm', 'v'), outs):
        arrays = _unpack_rows(buf, repl_shapes + [(1, 1)])
        for n, a in zip(REPLICATED, arrays):
            res[kind, n] = a
        if kind == 'grad':
            total_loss = arrays[-1].reshape(())

    out = [total_loss, dx[None]]
    for kind in ('grad', 'delta', 'm', 'v'):
        out += [res[kind, n] for n in WEIGHT_NAMES]
    return tuple(out)


def kernel(x, p, ev_w_in, ev_lambda_re, ev_lambda_im, ev_log_dt, ev_b_re, ev_b_im, ev_c_re, ev_c_im, ev_d, ev_w_glu, ev_b_glu, ev_conv_w, ev_w_out, od_w_in, od_rel_bias, od_pool_w, od_pool_scale, od_w_out, ln_mix_g, ln_mix_b, ln_ffn_g, ln_ffn_b, ffn_w_up, ffn_w_down, ple_w_proj, ple_w_gate, ple_b_gate, loss_target, m_ev_w_in, m_ev_lambda_re, m_ev_lambda_im, m_ev_log_dt, m_ev_b_re, m_ev_b_im, m_ev_c_re, m_ev_c_im, m_ev_d, m_ev_w_glu, m_ev_b_glu, m_ev_conv_w, m_ev_w_out, m_od_w_in, m_od_rel_bias, m_od_pool_w, m_od_pool_scale, m_od_w_out, m_ln_mix_g, m_ln_mix_b, m_ln_ffn_g, m_ln_ffn_b, m_ffn_w_up, m_ffn_w_down, m_ple_w_proj, m_ple_w_gate, m_ple_b_gate, v_ev_w_in, v_ev_lambda_re, v_ev_lambda_im, v_ev_log_dt, v_ev_b_re, v_ev_b_im, v_ev_c_re, v_ev_c_im, v_ev_d, v_ev_w_glu, v_ev_b_glu, v_ev_conv_w, v_ev_w_out, v_od_w_in, v_od_rel_bias, v_od_pool_w, v_od_pool_scale, v_od_w_out, v_ln_mix_g, v_ln_mix_b, v_ln_ffn_g, v_ln_ffn_b, v_ffn_w_up, v_ffn_w_down, v_ple_w_proj, v_ple_w_gate, v_ple_b_gate):
    given = dict(locals())
    W = {n: given[n] for n in WEIGHT_NAMES}
    M = {n: given["m_" + n] for n in WEIGHT_NAMES}
    V = {n: given["v_" + n] for n in WEIGHT_NAMES}
    return _step(x, p, loss_target, W, M, V)
```

```python
import math

import jax
import jax.numpy as jnp
from jax import lax
from jax.experimental import pallas as pl
from jax.experimental.pallas import tpu as pltpu

F32 = jnp.float32
BF16 = jnp.bfloat16
HI = lax.Precision.HIGHEST

D_MODEL = 1024
DEPTH = 4
CHUNK = 64
MIX = 512
S5_GROUP = 16
S5_GROUPS = 32
S5_STATE = 64
HEADS = 8
HEAD_DIM = 64
LEFT_CHUNKS = 8
MAX_REL = 128
POOL_WINDOWS = (2, 4, 8, 16)
POOL_GROUP = 128
D_FF = 2816
D_PLE = 256
ALPHA = (2 * DEPTH) ** 0.25
LN_EPS = 1e-5
NEG_INF = -1e30
ADAM_LR = 0.001
ADAM_B1 = 0.9
ADAM_B2 = 0.999
ADAM_EPS = 1e-08
ADAM_WD = 0.01
ADAM_STEP = 10
N_DEV = 8

WEIGHT_NAMES = ['ev_w_in', 'ev_lambda_re', 'ev_lambda_im', 'ev_log_dt', 'ev_b_re', 'ev_b_im', 'ev_c_re', 'ev_c_im',
                'ev_d', 'ev_w_glu', 'ev_b_glu', 'ev_conv_w', 'ev_w_out', 'od_w_in', 'od_rel_bias', 'od_pool_w',
                'od_pool_scale', 'od_w_out', 'ln_mix_g', 'ln_mix_b', 'ln_ffn_g', 'ln_ffn_b', 'ffn_w_up', 'ffn_w_down',
                'ple_w_proj', 'ple_w_gate', 'ple_b_gate']
SHARD_AXIS = {'ev_w_in': 2, 'ev_w_glu': 1, 'ev_conv_w': 2, 'ev_w_out': 1, 'od_w_in': 2, 'od_pool_scale': 1,
              'od_w_out': 1, 'ffn_w_up': 2, 'ffn_w_down': 1, 'ple_w_proj': 2, 'ple_w_gate': 1}
REPLICATED = [n for n in WEIGHT_NAMES if n not in SHARD_AXIS]
ROW_SHARDED = ['ev_w_out', 'od_w_out', 'ffn_w_down', 'ple_w_gate']
COL_IN = ['ev_w_in', 'od_w_in']
SMALL_SHARDED = ['ev_w_glu', 'ev_conv_w', 'od_pool_scale', 'ple_w_proj']

VMEM_LIMIT = 48 * 1024 * 1024
TM = 512
IN_NB = 4 * MIX // N_DEV
FF_NB = 2 * D_FF // N_DEV
FF_J = N_DEV // 2
S5_LC = 32
S5_LW = S5_LC * S5_GROUP
ATT_TQ = 128
ATT_NV = LEFT_CHUNKS * CHUNK // ATT_TQ + 1
ATT_W = ATT_NV * ATT_TQ
FLAT_COLS = 1024
FLAT_ROWS = 256
PIECE = 16 * FLAT_COLS


_MESH = pl.DeviceIdType.MESH
_ANY = pl.BlockSpec(memory_space=pl.ANY)


def _sds(shape, dt):
    return jax.ShapeDtypeStruct(shape, dt)


def _pcall(body, *, name, grid, in_specs, out_specs, out_shape, scratch=(), sem=None):
    sem = sem or ("arbitrary",) * len(grid)
    return pl.pallas_call(
        body, name=name, grid=grid, in_specs=in_specs, out_specs=out_specs, out_shape=out_shape,
        scratch_shapes=scratch,
        compiler_params=pltpu.CompilerParams(dimension_semantics=sem, vmem_limit_bytes=VMEM_LIMIT))


def _rows(tm, n):
    return pl.BlockSpec((tm, n), lambda i: (i, 0))


def _cols(tm, s):
    return pl.BlockSpec((tm, MIX), lambda i: (i, s))


def _full(shape):
    nd = len(shape)
    return pl.BlockSpec(shape, lambda *_: (0,) * nd)


def _layer(shape, l):
    nd = len(shape)
    return pl.BlockSpec((None,) + tuple(shape), lambda *_: (l,) + (0,) * nd)


def _vecs(a):
    return a.reshape(a.shape[0], 1, a.shape[1])


def _dot(a, b, precision=None):
    return jnp.dot(a, b, preferred_element_type=F32, precision=precision)


def _dot_nt(a, b, precision=None):
    return lax.dot_general(a, b, (((1,), (1,)), ((), ())), preferred_element_type=F32, precision=precision)


def _dot_tn(a, b, precision=None):
    return lax.dot_general(a, b, (((0,), (0,)), ((), ())), preferred_element_type=F32, precision=precision)


def _split(a):
    hi = a.astype(BF16)
    return hi, (a - hi.astype(F32)).astype(BF16)


def _mm3(dot, a2, b2):
    return dot(a2[0], b2[0]) + (dot(a2[0], b2[1]) + dot(a2[1], b2[0]))


def _sigmoid(x):
    return 0.5 * jnp.tanh(0.5 * x) + 0.5


_GELU_C = math.sqrt(2.0 / math.pi)


def _gelu(x):
    return 0.5 * x * (1.0 + jnp.tanh(_GELU_C * (x + 0.044715 * x * x * x)))


def _gelu_grad(x):
    t = jnp.tanh(_GELU_C * (x + 0.044715 * x * x * x))
    return 0.5 * (1.0 + t) + 0.5 * x * (1.0 - t * t) * _GELU_C * (1.0 + 3.0 * 0.044715 * x * x)


def _ln_fwd(r, g, b):
    mu = jnp.mean(r, axis=-1, keepdims=True)
    xc = r - mu
    var = jnp.mean(xc * xc, axis=-1, keepdims=True)
    rstd = lax.rsqrt(var + LN_EPS)
    xhat = xc * rstd
    return xhat, rstd, xhat * g + b


def _ln_bwd(dx, xhat, rstd, g):
    dxh = dx * g
    m1 = jnp.mean(dxh, axis=-1, keepdims=True)
    m2 = jnp.mean(dxh * xhat, axis=-1, keepdims=True)
    return rstd * (dxh - m1 - xhat * m2)


def _colsum(x):
    return jnp.sum(x, axis=0, keepdims=True)


def _mm_in(xh, win8, l, name):
    T = xh.shape[0]

    def body(x_ref, w_ref, h_ref):
        x = x_ref[...]
        for b in range(N_DEV):
            h_ref[:, b * IN_NB:(b + 1) * IN_NB] = _dot(x, w_ref[b])

    return _pcall(body, name=name, grid=(T // TM,),
                  in_specs=[_rows(TM, D_MODEL),
                            pl.BlockSpec((N_DEV, None, D_MODEL, IN_NB), lambda i: (0, l, 0, 0))],
                  out_specs=_rows(TM, 4 * MIX), out_shape=_sds((T, 4 * MIX), F32),
                  sem=("parallel",))(xh, win8)


def _tile_of(n, cap):
    best = None
    for t in range(128, min(n, cap) + 1, 128):
        if n % t == 0:
            best = t
    assert best is not None, n
    return best


def _mm_tn(a, b, name, a_layer=None, blocked_n=None):
    T, M = a.shape[-2:]
    N = b.shape[1]
    tm = _tile_of(M, 1408)
    nbs = max(1, 512 // blocked_n) if blocked_n else 1
    tn = blocked_n * nbs if blocked_n else _tile_of(N, 1024 if tm <= 512 else 512)
    tk = min(T, 1024)
    nk = T // tk

    def body(a_ref, b_ref, o_ref, acc):
        k = pl.program_id(2)

        @pl.when(k == 0)
        def _():
            acc[...] = jnp.zeros_like(acc)

        acc[...] += _dot_tn(a_ref[...].astype(BF16), b_ref[...].astype(BF16))

        @pl.when(k == nk - 1)
        def _():
            if blocked_n:
                for sb in range(nbs):
                    o_ref[sb] = acc[:, sb * blocked_n:(sb + 1) * blocked_n].astype(BF16)
            else:
                o_ref[...] = acc[...].astype(BF16)

    if a_layer is None:
        a_spec = pl.BlockSpec((tk, tm), lambda i, j, k: (k, i))
    else:
        a_spec = pl.BlockSpec((None, tk, tm), lambda i, j, k: (a_layer, k, i))
    if blocked_n:
        o_spec = pl.BlockSpec((nbs, tm, blocked_n), lambda i, j, k: (j, i, 0))
        o_shape = _sds((N // blocked_n, M, blocked_n), BF16)
    else:
        o_spec = pl.BlockSpec((tm, tn), lambda i, j, k: (i, j))
        o_shape = _sds((M, N), BF16)
    return _pcall(body, name=name, grid=(M // tm, N // tn, nk),
                  in_specs=[a_spec, pl.BlockSpec((tk, tn), lambda i, j, k: (k, j))],
                  out_specs=o_spec, out_shape=o_shape, scratch=[pltpu.VMEM((tm, tn), F32)],
                  sem=("parallel", "parallel", "arbitrary"))(a, b)


def _mm_tn_bblk(a, b3, name, per_step=2):
    T, M = a.shape
    NB, _, n = b3.shape
    tk = min(T, 1024)
    nk = T // tk

    def body(a_ref, b_ref, o_ref, acc):
        k = pl.program_id(1)

        @pl.when(k == 0)
        def _():
            acc[...] = jnp.zeros_like(acc)

        av = a_ref[...]
        for s in range(per_step):
            acc[s] += _dot_tn(av, b_ref[s])

        @pl.when(k == nk - 1)
        def _():
            o_ref[...] = acc[...].astype(BF16)

    return _pcall(body, name=name, grid=(NB // per_step, nk),
                  in_specs=[pl.BlockSpec((tk, M), lambda j, k: (k, 0)),
                            pl.BlockSpec((per_step, tk, n), lambda j, k: (j, k, 0))],
                  out_specs=pl.BlockSpec((per_step, M, n), lambda j, k: (j, 0, 0)),
                  out_shape=_sds((NB, M, n), BF16), scratch=[pltpu.VMEM((per_step, M, n), F32)],
                  sem=("parallel", "arbitrary"))(a, b3)


def _mm_tn_ablk(a3, b, name):
    NA, T, m = a3.shape
    N = b.shape[1]
    tn = _tile_of(N, 1024)
    tk = min(T, 1024)
    nk = T // tk

    def body(a_ref, b_ref, o_ref, acc):
        k = pl.program_id(2)

        @pl.when(k == 0)
        def _():
            acc[...] = jnp.zeros_like(acc)

        acc[...] += _dot_tn(a_ref[...], b_ref[...])

        @pl.when(k == nk - 1)
        def _():
            o_ref[...] = acc[...].astype(BF16)

    return _pcall(body, name=name, grid=(NA, N // tn, nk),
                  in_specs=[pl.BlockSpec((None, tk, m), lambda j, n, k: (j, k, 0)),
                            pl.BlockSpec((tk, tn), lambda j, n, k: (k, n))],
                  out_specs=pl.BlockSpec((None, m, tn), lambda j, n, k: (j, 0, n)),
                  out_shape=_sds((NA, m, N), BF16), scratch=[pltpu.VMEM((m, tn), F32)],
                  sem=("parallel", "parallel", "arbitrary"))(a3, b)


def _mm_out_ln(ya, yb, wo, l, x0, g, b, name):
    T = x0.shape[0]

    def body(ya_ref, yb_ref, w_ref, x0_ref, g_ref, b_ref, xh_ref, rs_ref, x1_ref):
        r = ALPHA * x0_ref[...] + _dot(ya_ref[...], w_ref[0:MIX, :]) + _dot(yb_ref[...], w_ref[MIX:, :])
        xhat, rstd, x1 = _ln_fwd(r, g_ref[...], b_ref[...])
        xh_ref[...] = xhat
        rs_ref[...] = rstd
        x1_ref[...] = x1.astype(BF16)

    vec = _layer((1, D_MODEL), g[1])
    return _pcall(body, name=name, grid=(T // TM,),
                  in_specs=[_rows(TM, MIX), _rows(TM, MIX), _layer((D_MODEL, D_MODEL), l), _rows(TM, D_MODEL),
                            vec, vec],
                  out_specs=[_rows(TM, D_MODEL), _rows(TM, 1), _rows(TM, D_MODEL)],
                  out_shape=[_sds((T, D_MODEL), F32), _sds((T, 1), F32), _sds((T, D_MODEL), BF16)],
                  sem=("parallel",))(ya, yb, wo, x0, g[0], b[0])


def _mm_up(x1h, wup, l, name):
    T = x1h.shape[0]

    def body(x_ref, w_ref, hf_ref, a_ref):
        x = x_ref[...]
        g = _dot(x, w_ref[0])
        u = _dot(x, w_ref[1])
        hf_ref[0] = g.astype(BF16)
        hf_ref[1] = u.astype(BF16)
        a_ref[...] = (g * _sigmoid(g) * u).astype(BF16)

    return _pcall(body, name=name, grid=(FF_J, T // TM),
                  in_specs=[pl.BlockSpec((TM, D_MODEL), lambda j, i: (i, 0)),
                            pl.BlockSpec((2, None, None, D_MODEL, FF_NB), lambda j, i: (0, j, l, 0, 0))],
                  out_specs=[pl.BlockSpec((2, None, TM, FF_NB), lambda j, i: (0, j, i, 0)),
                             pl.BlockSpec((None, TM, FF_NB), lambda j, i: (j, i, 0))],
                  out_shape=[_sds((2, FF_J, T, FF_NB), BF16), _sds((FF_J, T, FF_NB), BF16)],
                  sem=("parallel", "parallel"))(x1h, wup)


def _mm_down_ln(a3, wd4, l, xhat1, g1, b1, g2, b2, name):
    T = a3.shape[1]

    def body(a_ref, w_ref, xh1_ref, g1_ref, b1_ref, g2_ref, b2_ref, xh_ref, rs_ref, x2_ref):
        x1 = xh1_ref[...] * g1_ref[...] + b1_ref[...]
        r = ALPHA * x1
        for j in range(FF_J):
            r = r + _dot(a_ref[j], w_ref[j])
        xhat, rstd, x2 = _ln_fwd(r, g2_ref[...], b2_ref[...])
        xh_ref[...] = xhat
        rs_ref[...] = rstd
        x2_ref[...] = x2.astype(BF16)

    vec = _layer((1, D_MODEL), g1[1])
    return _pcall(body, name=name, grid=(T // TM,),
                  in_specs=[pl.BlockSpec((FF_J, TM, FF_NB), lambda i: (0, i, 0)),
                            _layer((FF_J, FF_NB, D_MODEL), l), _rows(TM, D_MODEL), vec, vec, vec, vec],
                  out_specs=[_rows(TM, D_MODEL), _rows(TM, 1), _rows(TM, D_MODEL)],
                  out_shape=[_sds((T, D_MODEL), F32), _sds((T, 1), F32), _sds((T, D_MODEL), BF16)],
                  sem=("parallel",))(a3, wd4, xhat1, g1[0], b1[0], g2[0], b2[0])


def _mm_ple(x2h, xhat2, g2, b2, p, wg, bg, wp, l, name):
    T = x2h.shape[0]

    def body(x2h_ref, xh_ref, g2_ref, b2_ref, p_ref, wg_ref, bg_ref, wp_ref, o_ref, oh_ref):
        x2 = xh_ref[...] * g2_ref[...] + b2_ref[...]
        gate = _sigmoid(_dot(x2h_ref[...], wg_ref[...]) + bg_ref[...])
        pp = _dot(p_ref[...].astype(BF16), wp_ref[...])
        x3 = x2 + gate * pp
        o_ref[...] = x3
        oh_ref[...] = x3.astype(BF16)

    vec = _layer((1, D_MODEL), l)
    return _pcall(body, name=name, grid=(T // TM,),
                  in_specs=[_rows(TM, D_MODEL), _rows(TM, D_MODEL), vec, vec,
                            pl.BlockSpec((None, TM, D_PLE), lambda i: (l, i, 0)),
                            _layer((D_MODEL, D_MODEL), l), vec, _layer((D_PLE, D_MODEL), l)],
                  out_specs=[_rows(TM, D_MODEL), _rows(TM, D_MODEL)],
                  out_shape=[_sds((T, D_MODEL), F32), _sds((T, D_MODEL), BF16)],
                  sem=("parallel",))(x2h, xhat2, g2, b2, p, wg, bg, wp)


def _loss_head(x3, tgt, name):
    T = x3.shape[0]

    def body(x_ref, t_ref, dx_ref, l_ref):
        e = x_ref[...] - t_ref[...]
        dx_ref[...] = e * (1.0 / D_MODEL)

        @pl.when(pl.program_id(0) == 0)
        def _():
            l_ref[...] = jnp.zeros_like(l_ref)

        l_ref[...] += (0.5 / D_MODEL) * jnp.sum(e * e).reshape(1, 1)

    return _pcall(body, name=name, grid=(T // TM,),
                  in_specs=[_rows(TM, D_MODEL), _rows(TM, D_MODEL)],
                  out_specs=[_rows(TM, D_MODEL), _full((1, 1))],
                  out_shape=[_sds((T, D_MODEL), F32), _sds((1, 1), F32)])(x3, tgt)


def _ple_bwd(dx3, x2h, p, wg, bg, wp, l, xhat2, rstd2, g2, name):
    T = dx3.shape[0]

    def body(dx3_ref, x2h_ref, p_ref, wg_ref, bg_ref, wp_ref, xh_ref, rs_ref, g2_ref,
             dr_ref, drh_ref, dpre_ref, dpp_ref, dbg_ref, dg_ref, db_ref):
        dx3 = dx3_ref[...]
        gate = _sigmoid(_dot(x2h_ref[...], wg_ref[...]) + bg_ref[...])
        pp = _dot(p_ref[...].astype(BF16), wp_ref[...])
        dpre = dx3 * pp * gate * (1.0 - gate)
        dpreh = dpre.astype(BF16)
        dpre_ref[...] = dpreh
        dpp_ref[...] = (dx3 * gate).astype(BF16)
        dx2 = dx3 + _dot_nt(dpreh, wg_ref[...])
        xhat = xh_ref[...]
        dr = _ln_bwd(dx2, xhat, rs_ref[...], g2_ref[...])
        dr_ref[...] = dr
        drh_ref[...] = dr.astype(BF16)

        @pl.when(pl.program_id(0) == 0)
        def _():
            dbg_ref[...] = jnp.zeros_like(dbg_ref)
            dg_ref[...] = jnp.zeros_like(dg_ref)
            db_ref[...] = jnp.zeros_like(db_ref)

        dbg_ref[...] += _colsum(dpre)
        dg_ref[...] += _colsum(dx2 * xhat)
        db_ref[...] += _colsum(dx2)

    vec = _layer((1, D_MODEL), l)
    acc = _full((1, D_MODEL))
    big = _rows(TM, D_MODEL)
    return _pcall(body, name=name, grid=(T // TM,),
                  in_specs=[big, big, pl.BlockSpec((None, TM, D_PLE), lambda i: (l, i, 0)),
                            _layer((D_MODEL, D_MODEL), l), vec, _layer((D_PLE, D_MODEL), l),
                            big, _rows(TM, 1), vec],
                  out_specs=[big, big, big, big, acc, acc, acc],
                  out_shape=[_sds((T, D_MODEL), F32), _sds((T, D_MODEL), BF16), _sds((T, D_MODEL), BF16),
                             _sds((T, D_MODEL), BF16), _sds((1, D_MODEL), F32), _sds((1, D_MODEL), F32),
                             _sds((1, D_MODEL), F32)])(dx3, x2h, p, wg, bg, wp, xhat2, rstd2, g2)


def _ffn_bwd1(dr2h, wd4, l, hf, name, ride=()):
    T = dr2h.shape[0]
    ni = T // TM

    def body(d_ref, w_ref, hf_ref, o_ref):
        da = _dot_nt(d_ref[...], w_ref[...])
        g = hf_ref[0].astype(F32)
        u = hf_ref[1].astype(F32)
        sg = _sigmoid(g)
        o_ref[0] = (da * u * (sg * (1.0 + g * (1.0 - sg)))).astype(BF16)
        o_ref[1] = (da * (g * sg)).astype(BF16)

    blk = pl.BlockSpec((2, None, TM, FF_NB), lambda j, i: (0, j, i, 0))
    in_specs = [pl.BlockSpec((TM, D_MODEL), lambda j, i: (i, 0)),
                pl.BlockSpec((None, None, FF_NB, D_MODEL), lambda j, i: (l, j, 0, 0)), blk]
    out_shape = _sds((2, FF_J, T, FF_NB), BF16)
    if not ride:
        return _pcall(body, name=name, grid=(FF_J, ni), in_specs=in_specs, out_specs=blk, out_shape=out_shape,
                      sem=("parallel", "parallel"))(dr2h, wd4, hf), ()
    first = lambda: (pl.program_id(0) == 0) & (pl.program_id(1) == 0)
    last = lambda: (pl.program_id(0) == FF_J - 1) & (pl.program_id(1) == ni - 1)
    outs = _pcall(_riding(body, 3, 1, len(ride), first, last), name=name, grid=(FF_J, ni),
                  in_specs=in_specs + [_ANY] * len(ride), out_specs=[blk] + [_ANY] * len(ride),
                  out_shape=[out_shape] + [_sds(r.shape, r.dtype) for r in ride],
                  scratch=list(_A2A_SEMS) * len(ride))(dr2h, wd4, hf, *ride)
    return outs[0], outs[1:]


_TM_B2 = 256


def _ffn_bwd2(dhf, wup, l, dr2, xhat1, rstd1, g1, name, ride=()):
    T = dr2.shape[0]
    tm = min(T, _TM_B2)
    ni = T // tm

    def body(dh_ref, w_ref, dr2_ref, xh_ref, rs_ref, g_ref, dr_ref, drh_ref, dg_ref, db_ref):
        dx1 = ALPHA * dr2_ref[...]
        for s in range(2):
            for j in range(FF_J):
                dx1 = dx1 + _dot_nt(dh_ref[s, j], w_ref[s, j])
        xhat = xh_ref[...]
        dr = _ln_bwd(dx1, xhat, rs_ref[...], g_ref[...])
        dr_ref[...] = dr
        drh_ref[...] = dr.astype(BF16)

        @pl.when(pl.program_id(0) == 0)
        def _():
            dg_ref[...] = jnp.zeros_like(dg_ref)
            db_ref[...] = jnp.zeros_like(db_ref)

        dg_ref[...] += _colsum(dx1 * xhat)
        db_ref[...] += _colsum(dx1)

    big = _rows(tm, D_MODEL)
    acc = _full((1, D_MODEL))
    in_specs = [pl.BlockSpec((2, FF_J, tm, FF_NB), lambda i: (0, 0, i, 0)),
                pl.BlockSpec((2, FF_J, None, D_MODEL, FF_NB), lambda i: (0, 0, l, 0, 0)),
                big, big, _rows(tm, 1), _layer((1, D_MODEL), g1[1])]
    out_specs = [big, big, acc, acc]
    out_shape = [_sds((T, D_MODEL), F32), _sds((T, D_MODEL), BF16), _sds((1, D_MODEL), F32),
                 _sds((1, D_MODEL), F32)]
    operands = (dhf, wup, dr2, xhat1, rstd1, g1[0])
    if not ride:
        return _pcall(body, name=name, grid=(ni,), in_specs=in_specs, out_specs=out_specs,
                      out_shape=out_shape)(*operands), ()
    first = lambda: pl.program_id(0) == 0
    last = lambda: pl.program_id(0) == ni - 1
    outs = _pcall(_riding(body, 6, 4, len(ride), first, last), name=name, grid=(ni,),
                  in_specs=in_specs + [_ANY] * len(ride), out_specs=out_specs + [_ANY] * len(ride),
                  out_shape=out_shape + [_sds(r.shape, r.dtype) for r in ride],
                  scratch=list(_A2A_SEMS) * len(ride))(*operands, *ride)
    return outs[:4], outs[4:]


def _out_bwd(dr1h, wo, l, name):
    T = dr1h.shape[0]

    def body(d_ref, w_ref, da_ref, db_ref):
        d = d_ref[...]
        da_ref[...] = _dot_nt(d, w_ref[0:MIX, :])
        db_ref[...] = _dot_nt(d, w_ref[MIX:, :])

    return _pcall(body, name=name, grid=(T // TM,),
                  in_specs=[_rows(TM, D_MODEL), _layer((D_MODEL, D_MODEL), l)],
                  out_specs=[_rows(TM, MIX), _rows(TM, MIX)],
                  out_shape=[_sds((T, MIX), F32), _sds((T, MIX), F32)],
                  sem=("parallel",))(dr1h, wo)


def _in_bwd(dh, win8, l, dr1, name):
    T = dr1.shape[0]

    def body(dh_ref, w_ref, dr_ref, o_ref):
        acc = ALPHA * dr_ref[...]
        for b in range(N_DEV):
            acc = acc + _dot_nt(dh_ref[:, b * IN_NB:(b + 1) * IN_NB], w_ref[b])
        o_ref[...] = acc

    return _pcall(body, name=name, grid=(T // TM,),
                  in_specs=[_rows(TM, 4 * MIX), pl.BlockSpec((N_DEV, None, D_MODEL, IN_NB), lambda i: (0, l, 0, 0)),
                            _rows(TM, D_MODEL)],
                  out_specs=_rows(TM, D_MODEL), out_shape=_sds((T, D_MODEL), F32),
                  sem=("parallel",))(dh, win8, dr1)


def _s5_operators(lre, lim, log_dt, bre, bim, cre, cim, dskip):
    G, P, H, LC = S5_GROUPS, S5_STATE, S5_GROUP, S5_LC
    dt = jnp.exp(log_dt)[:, None]
    mag = jnp.exp(lre * dt)
    ang = lim * dt
    lb_re = mag * jnp.cos(ang)
    lb_im = mag * jnp.sin(ang)
    den = lre * lre + lim * lim
    nr = lb_re - 1.0
    ni = lb_im
    r_re = (nr * lre + ni * lim) / den
    r_im = (ni * lre - nr * lim) / den
    bb_re = r_re[..., None] * bre - r_im[..., None] * bim
    bb_im = r_re[..., None] * bim + r_im[..., None] * bre
    k = jnp.arange(LC + 1, dtype=F32)[:, None, None]
    pmag = jnp.exp(k * (lre * dt)[None])
    pang = k * ang[None]
    pw_re = pmag * jnp.cos(pang)
    pw_im = pmag * jnp.sin(pang)
    cp_re = cre[None] * pw_re[:, :, None, :] - cim[None] * pw_im[:, :, None, :]
    cp_im = cre[None] * pw_im[:, :, None, :] + cim[None] * pw_re[:, :, None, :]
    kk = (jnp.einsum('kghp,gpj->kghj', cp_re[:LC], bb_re, precision=HI)
          - jnp.einsum('kghp,gpj->kghj', cp_im[:LC], bb_im, precision=HI))
    dmat = dskip.reshape(G, H)[:, :, None] * jnp.eye(H, dtype=F32)[None]
    kk = jnp.concatenate([kk[:1] + dmat[None], kk[1:]], axis=0)
    r0 = jnp.transpose(kk, (1, 3, 0, 2)).reshape(G, H, LC * H)
    qt = jnp.stack([cp_re[1:], -cp_im[1:]], axis=0)
    qt = jnp.transpose(qt, (2, 0, 4, 1, 3)).reshape(G, 2 * P, LC * H)
    pb_re = pw_re[:LC, :, :, None] * bb_re[None] - pw_im[:LC, :, :, None] * bb_im[None]
    pb_im = pw_re[:LC, :, :, None] * bb_im[None] + pw_im[:LC, :, :, None] * bb_re[None]
    pm = jnp.stack([pb_re[::-1], pb_im[::-1]], axis=0)
    pm = jnp.transpose(pm, (2, 1, 4, 0, 3)).reshape(G, LC * H, 2 * P)
    a_re = pw_re[LC]
    a_im = pw_im[LC]
    a1 = jnp.concatenate([a_re, a_re], axis=-1)
    a2 = jnp.concatenate([-a_im, a_im], axis=-1)
    return r0, qt, pm, a1, a2


def _to_chunks(u):
    T = u.shape[0]
    return jnp.transpose(u.reshape(T // S5_LC, S5_LC, S5_GROUPS, S5_GROUP), (2, 0, 1, 3)).reshape(
        S5_GROUPS, T // S5_LC, S5_LW)


def _from_chunks(m):
    nc = m.shape[1]
    return jnp.transpose(m.reshape(S5_GROUPS, nc, S5_LC, S5_GROUP), (1, 2, 0, 3)).reshape(nc * S5_LC, MIX)


def _gspec(r, c):
    return pl.BlockSpec((None, r, c), lambda g: (g, 0, 0))


def _s5_chunk_state(umat, pm, name):
    G, nc, _ = umat.shape

    def body(u_ref, p_ref, o_ref):
        o_ref[...] = _mm3(_dot, _split(u_ref[...]), _split(p_ref[...]))

    return _pcall(body, name=name, grid=(G,), in_specs=[_gspec(nc, S5_LW), _gspec(S5_LW, 128)],
                  out_specs=_gspec(nc, 128), out_shape=_sds((G, nc, 128), F32), sem=("parallel",))(umat, pm)


_SCAN_G = 8
_SCAN_UNROLL = 8


def _s5_scan_fwd(s_t, a1, a2, name):
    nc, G, _ = s_t.shape

    def body(s_ref, a1_ref, a2_ref, o_ref, sb_ref):
        sb_ref[...] = pltpu.roll(s_ref[...], 64, 2)
        a1v = a1_ref[...]
        a2v = a2_ref[...]

        def step(c, carry):
            x, xb = carry
            o_ref[c] = x
            return a1v * x + a2v * xb + s_ref[c], a1v * xb - a2v * x + sb_ref[c]

        zero = jnp.zeros((_SCAN_G, 128), F32)
        lax.fori_loop(0, nc, step, (zero, zero), unroll=_SCAN_UNROLL)

    blk = pl.BlockSpec((nc, _SCAN_G, 128), lambda g: (0, g, 0))
    vec = pl.BlockSpec((_SCAN_G, 128), lambda g: (g, 0))
    return _pcall(body, name=name, grid=(G // _SCAN_G,), in_specs=[blk, vec, vec], out_specs=blk,
                  out_shape=_sds((nc, G, 128), F32), scratch=[pltpu.VMEM((nc, _SCAN_G, 128), F32)],
                  sem=("parallel",))(s_t, a1, a2)


def _s5_scan_bwd(dxp_t, xp_t, a1, a2, name):
    nc, G, _ = dxp_t.shape

    def body(dx_ref, x_ref, a1_ref, a2_ref, ds_ref, da1_ref, da2_ref, dxb_ref, xb_ref):
        dxb_ref[...] = pltpu.roll(dx_ref[...], 64, 2)
        xb_ref[...] = pltpu.roll(x_ref[...], 64, 2)
        a1v = a1_ref[...]
        a2v = a2_ref[...]
        zero = jnp.zeros((_SCAN_G, 128), F32)

        def step(n, carry):
            gc, gb, d1, d2 = carry
            c = nc - 1 - n
            ds_ref[c] = gc
            d1 = d1 + gc * x_ref[c]
            d2 = d2 + gc * xb_ref[c]
            return dx_ref[c] + a1v * gc - a2v * gb, dxb_ref[c] + a1v * gb + a2v * gc, d1, d2

        _, _, d1, d2 = lax.fori_loop(0, nc, step, (zero, zero, zero, zero), unroll=_SCAN_UNROLL)
        da1_ref[...] = d1
        da2_ref[...] = d2

    blk = pl.BlockSpec((nc, _SCAN_G, 128), lambda g: (0, g, 0))
    vec = pl.BlockSpec((_SCAN_G, 128), lambda g: (g, 0))
    return _pcall(body, name=name, grid=(G // _SCAN_G,), in_specs=[blk, blk, vec, vec],
                  out_specs=[blk, vec, vec],
                  out_shape=[_sds((nc, G, 128), F32), _sds((G, 128), F32), _sds((G, 128), F32)],
                  scratch=[pltpu.VMEM((nc, _SCAN_G, 128), F32)] * 2, sem=("parallel",))(dxp_t, xp_t, a1, a2)


def _toeplitz_rows(r0, mt):
    lane = lax.broadcasted_iota(jnp.int32, (S5_GROUP, S5_LW), 1)
    mt[0:S5_GROUP, :] = r0
    for s in range(1, S5_LC):
        mt[s * S5_GROUP:(s + 1) * S5_GROUP, :] = jnp.where(lane >= s * S5_GROUP, pltpu.roll(r0, s * S5_GROUP, 1), 0.0)


def _toeplitz_rows_t(dmt):
    lane = lax.broadcasted_iota(jnp.int32, (S5_GROUP, S5_LW), 1)
    acc = dmt[0:S5_GROUP, :]
    for s in range(1, S5_LC):
        blk = dmt[s * S5_GROUP:(s + 1) * S5_GROUP, :]
        acc = acc + jnp.where(lane < S5_LW - s * S5_GROUP, pltpu.roll(blk, S5_LW - s * S5_GROUP, 1), 0.0)
    return acc


def _s5_output(umat, xprev, r0, qt, name):
    G, nc, _ = umat.shape

    def body(u_ref, x_ref, r_ref, q_ref, o_ref, mt):
        _toeplitz_rows(r_ref[...], mt)
        o_ref[...] = (_mm3(_dot, _split(u_ref[...]), _split(mt[...]))
                      + _mm3(_dot, _split(x_ref[...]), _split(q_ref[...])))

    return _pcall(body, name=name, grid=(G,),
                  in_specs=[_gspec(nc, S5_LW), _gspec(nc, 128), _gspec(S5_GROUP, S5_LW), _gspec(128, S5_LW)],
                  out_specs=_gspec(nc, S5_LW), out_shape=_sds((G, nc, S5_LW), F32),
                  scratch=[pltpu.VMEM((S5_LW, S5_LW), F32)], sem=("parallel",))(umat, xprev, r0, qt)


def _s5_bwd_state(dymat, qt, name):
    G, nc, _ = dymat.shape

    def body(d_ref, q_ref, o_ref):
        o_ref[...] = _mm3(_dot_nt, _split(d_ref[...]), _split(q_ref[...]))

    return _pcall(body, name=name, grid=(G,), in_specs=[_gspec(nc, S5_LW), _gspec(128, S5_LW)],
                  out_specs=_gspec(nc, 128), out_shape=_sds((G, nc, 128), F32), sem=("parallel",))(dymat, qt)


def _s5_bwd_main(umat, xprev, dymat, ds, r0, pm, name):
    G, nc, _ = umat.shape

    def body(u_ref, x_ref, dy_ref, ds_ref, r_ref, p_ref, du_ref, dr_ref, dq_ref, dp_ref, mt):
        u = _split(u_ref[...])
        dy = _split(dy_ref[...])
        dsv = _split(ds_ref[...])
        _toeplitz_rows(r_ref[...], mt)
        du_ref[...] = (_mm3(_dot_nt, dy, _split(mt[...])) + _mm3(_dot_nt, dsv, _split(p_ref[...]))).astype(BF16)
        dq_ref[...] = _mm3(_dot_tn, _split(x_ref[...]), dy)
        dp_ref[...] = _mm3(_dot_tn, u, dsv)
        mt[...] = _mm3(_dot_tn, u, dy)
        dr_ref[...] = _toeplitz_rows_t(mt)

    return _pcall(body, name=name, grid=(G,),
                  in_specs=[_gspec(nc, S5_LW), _gspec(nc, 128), _gspec(nc, S5_LW), _gspec(nc, 128),
                            _gspec(S5_GROUP, S5_LW), _gspec(S5_LW, 128)],
                  out_specs=[_gspec(nc, S5_LW), _gspec(S5_GROUP, S5_LW), _gspec(128, S5_LW), _gspec(S5_LW, 128)],
                  out_shape=[_sds((G, nc, S5_LW), BF16), _sds((G, S5_GROUP, S5_LW), F32), _sds((G, 128, S5_LW), F32),
                             _sds((G, S5_LW, 128), F32)],
                  scratch=[pltpu.VMEM((S5_LW, S5_LW), F32)], sem=("parallel",))(umat, xprev, dymat, ds, r0, pm)


def _glu_fwd(y, wglu, bglu, l, name):
    T = y.shape[0]

    def body(y_ref, w_ref, b_ref, o_ref, g_ref):
        g = _gelu(y_ref[...])
        gh = g.astype(BF16)
        z = _dot(gh, w_ref[...]) + b_ref[...]
        o_ref[...] = (g * _sigmoid(z)).astype(BF16)
        g_ref[...] = gh

    return _pcall(body, name=name, grid=(T // TM,),
                  in_specs=[_rows(TM, MIX), _layer((MIX, MIX), l), _layer((1, MIX), l)],
                  out_specs=[_rows(TM, MIX), _rows(TM, MIX)],
                  out_shape=[_sds((T, MIX), BF16), _sds((T, MIX), BF16)], sem=("parallel",))(y, wglu, bglu)


def _glu_bwd(y, dout, wglu, bglu, l, name):
    T = y.shape[0]

    def body(y_ref, do_ref, w_ref, b_ref, dy_ref, dz_ref, db_ref):
        yv = y_ref[...]
        do = do_ref[...]
        g = _gelu(yv)
        s = _sigmoid(_dot(g.astype(BF16), w_ref[...]) + b_ref[...])
        dz = do * g * s * (1.0 - s)
        dzh = dz.astype(BF16)
        dz_ref[...] = dzh
        dg = do * s + _dot_nt(dzh, w_ref[...])
        dy_ref[...] = dg * _gelu_grad(yv)

        @pl.when(pl.program_id(0) == 0)
        def _():
            db_ref[...] = jnp.zeros_like(db_ref)

        db_ref[...] += _colsum(dz)

    return _pcall(body, name=name, grid=(T // TM,),
                  in_specs=[_rows(TM, MIX), _rows(TM, MIX), _layer((MIX, MIX), l), _layer((1, MIX), l)],
                  out_specs=[_rows(TM, MIX), _rows(TM, MIX), _full((1, MIX))],
                  out_shape=[_sds((T, MIX), F32), _sds((T, MIX), BF16), _sds((1, MIX), F32)])(y, dout, wglu, bglu)


def _prev_rows(T, h, s=0):
    return pl.BlockSpec((h, MIX), lambda i: (jnp.maximum(i * (TM // h) - 1, 0), s))


def _next_rows(T, h, s=0):
    return pl.BlockSpec((h, MIX), lambda i: (jnp.minimum((i + 1) * (TM // h), T // h - 1), s))


def _conv_fwd(h, w, l, name):
    T = h.shape[0]

    def body(b_ref, c_ref, x_ref, ch_ref, xh_ref, w_ref, o_ref, ext):
        i = pl.program_id(0)
        z = c_ref[...] * x_ref[...]
        ext[0:8, :] = jnp.where(i > 0, ch_ref[...] * xh_ref[...], 0.0)
        ext[8:, :] = z
        y = (w_ref[0:1, :] * ext[pl.ds(6, TM), :] + w_ref[1:2, :] * ext[pl.ds(7, TM), :] + w_ref[2:3, :] * z)
        o_ref[...] = (b_ref[...] * y).astype(BF16)

    return _pcall(body, name=name, grid=(T // TM,),
                  in_specs=[_cols(TM, 1), _cols(TM, 2), _cols(TM, 3), _prev_rows(T, 8, 2), _prev_rows(T, 8, 3),
                            _layer((3, MIX), l)],
                  out_specs=_rows(TM, MIX), out_shape=_sds((T, MIX), BF16),
                  scratch=[pltpu.VMEM((TM + 8, MIX), F32)], sem=("parallel",))(h, h, h, h, h, w)


def _conv_bwd(dout, h, dua, w, l, name):
    T = h.shape[0]
    nb = T // TM

    def body(do_ref, b_ref, c_ref, x_ref, ch_ref, xh_ref, don_ref, bn_ref, du_ref, w_ref,
             dh_ref, dw_ref, ext, ext2):
        i = pl.program_id(0)
        c = c_ref[...]
        x = x_ref[...]
        z = c * x
        ext[0:8, :] = jnp.where(i > 0, ch_ref[...] * xh_ref[...], 0.0)
        ext[8:, :] = z
        zm2 = ext[pl.ds(6, TM), :]
        zm1 = ext[pl.ds(7, TM), :]
        w0 = w_ref[0:1, :]
        w1 = w_ref[1:2, :]
        w2 = w_ref[2:3, :]
        y = w0 * zm2 + w1 * zm1 + w2 * z
        do = do_ref[...]
        dy = do * b_ref[...]
        ext2[0:TM, :] = dy
        ext2[TM:, :] = jnp.where(i < nb - 1, don_ref[...] * bn_ref[...], 0.0)
        dz = w2 * dy + w1 * ext2[pl.ds(1, TM), :] + w0 * ext2[pl.ds(2, TM), :]
        dh_ref[:, 0:MIX] = du_ref[...]
        dh_ref[:, MIX:2 * MIX] = (do * y).astype(BF16)
        dh_ref[:, 2 * MIX:3 * MIX] = (dz * x).astype(BF16)
        dh_ref[:, 3 * MIX:] = (dz * c).astype(BF16)

        @pl.when(i == 0)
        def _():
            dw_ref[...] = jnp.zeros_like(dw_ref)

        dw_ref[0:1, :] += _colsum(dy * zm2)
        dw_ref[1:2, :] += _colsum(dy * zm1)
        dw_ref[2:3, :] += _colsum(dy * z)

    return _pcall(body, name=name, grid=(nb,),
                  in_specs=[_rows(TM, MIX), _cols(TM, 1), _cols(TM, 2), _cols(TM, 3), _prev_rows(T, 8, 2),
                            _prev_rows(T, 8, 3), _next_rows(T, 8), _next_rows(T, 8, 1), _rows(TM, MIX),
                            _layer((3, MIX), l)],
                  out_specs=[_rows(TM, 4 * MIX), _full((8, MIX))],
                  out_shape=[_sds((T, 4 * MIX), BF16), _sds((8, MIX), F32)],
                  scratch=[pltpu.VMEM((TM + 8, MIX), F32), pltpu.VMEM((TM + 8, MIX), F32)])(
                      dout, h, h, h, h, h, dout, h, dua, w)


_PH = 16


def _pooled(ext, t, gi, w):
    lo = gi * POOL_GROUP
    cur = ext[pl.ds(_PH, TM), lo:lo + POOL_GROUP]
    acc = cur
    for k in range(1, w):
        acc = acc + ext[pl.ds(_PH - k, TM), lo:lo + POOL_GROUP]
    cnt = jnp.minimum(t + 1, w).astype(F32)
    return acc / cnt - cur, cnt


def _pool_fwd(h, pw, scale, l, name):
    T = h.shape[0]

    def body(z_ref, zh_ref, pw_ref, sc_ref, o_ref, ext):
        i = pl.program_id(0)
        ext[0:_PH, :] = jnp.where(i > 0, zh_ref[...], 0.0)
        ext[_PH:, :] = z_ref[...]
        t = i * TM + lax.broadcasted_iota(jnp.int32, (TM, 1), 0)
        for gi, w in enumerate(POOL_WINDOWS):
            lo = gi * POOL_GROUP
            pooled, _ = _pooled(ext, t, gi, w)
            mixed = _dot(pooled.astype(BF16), pw_ref[gi].astype(BF16))
            o_ref[:, lo:lo + POOL_GROUP] = (mixed * sc_ref[:, lo:lo + POOL_GROUP]).astype(BF16)

    return _pcall(body, name=name, grid=(T // TM,),
                  in_specs=[_cols(TM, 3), _prev_rows(T, _PH, 3), _layer((4, POOL_GROUP, POOL_GROUP), l),
                            _layer((1, MIX), l)],
                  out_specs=_rows(TM, MIX), out_shape=_sds((T, MIX), BF16),
                  scratch=[pltpu.VMEM((TM + _PH, MIX), F32)], sem=("parallel",))(h, h, pw, scale)


def _pool_bwd(dout, h, dq, dk, dv, pw, scale, l, name):
    T = h.shape[0]
    nb = T // TM

    def body(do_ref, don_ref, z_ref, zh_ref, dq_ref, dk_ref, dv_ref, pw_ref, sc_ref,
             dh_ref, dpw_ref, dsc_ref, ext, ext2):
        i = pl.program_id(0)
        ext[0:_PH, :] = jnp.where(i > 0, zh_ref[...], 0.0)
        ext[_PH:, :] = z_ref[...]
        t = i * TM + lax.broadcasted_iota(jnp.int32, (TM, 1), 0)
        dh_ref[:, 0:MIX] = dq_ref[...]
        dh_ref[:, MIX:2 * MIX] = dk_ref[...]
        dh_ref[:, 2 * MIX:3 * MIX] = dv_ref[...]

        @pl.when(i == 0)
        def _():
            dpw_ref[...] = jnp.zeros_like(dpw_ref)
            dsc_ref[...] = jnp.zeros_like(dsc_ref)

        for gi, w in enumerate(POOL_WINDOWS):
            lo = gi * POOL_GROUP
            pwb = pw_ref[gi].astype(BF16)
            sc = sc_ref[:, lo:lo + POOL_GROUP]
            pooled, cnt = _pooled(ext, t, gi, w)
            pb = pooled.astype(BF16)
            mixed = _dot(pb, pwb)
            dog = do_ref[:, lo:lo + POOL_GROUP]
            dsc_ref[:, lo:lo + POOL_GROUP] += _colsum(dog * mixed)
            dmix = (dog * sc).astype(BF16)
            dpw_ref[gi] += _dot_tn(pb, dmix)
            dpool = _dot_nt(dmix, pwb)
            dmix_n = (jnp.where(i < nb - 1, don_ref[:, lo:lo + POOL_GROUP], 0.0) * sc).astype(BF16)
            dpool_n = _dot_nt(dmix_n, pwb)
            e = dpool / cnt
            ext2[0:TM, lo:lo + POOL_GROUP] = e
            ext2[TM:, lo:lo + POOL_GROUP] = dpool_n * (1.0 / w)
            s = e
            for k in range(1, w):
                s = s + ext2[pl.ds(k, TM), lo:lo + POOL_GROUP]
            dh_ref[:, 3 * MIX + lo:3 * MIX + lo + POOL_GROUP] = (s - dpool).astype(BF16)

    blk = _rows(TM, MIX)
    return _pcall(body, name=name, grid=(nb,),
                  in_specs=[blk, _next_rows(T, _PH), _cols(TM, 3), _prev_rows(T, _PH, 3), blk, blk, blk,
                            _layer((4, POOL_GROUP, POOL_GROUP), l), _layer((1, MIX), l)],
                  out_specs=[_rows(TM, 4 * MIX), _full((4, POOL_GROUP, POOL_GROUP)), _full((1, MIX))],
                  out_shape=[_sds((T, 4 * MIX), BF16), _sds((4, POOL_GROUP, POOL_GROUP), F32), _sds((1, MIX), F32)],
                  scratch=[pltpu.VMEM((TM + _PH, MIX), F32), pltpu.VMEM((TM + _PH, MIX), F32)])(
                      dout, dout, h, h, dq, dk, dv, pw, scale)


def _att_bias_table(rel_bias):
    span = LEFT_CHUNKS * CHUNK
    assert ATT_TQ - 1 <= MAX_REL
    lo = MAX_REL - (ATT_TQ - 1)
    near = rel_bias[:, lo:2 * MAX_REL + 1]
    far = jnp.broadcast_to(rel_bias[:, 2 * MAX_REL:], (HEADS, span + ATT_TQ - 1 - MAX_REL))
    by_dist = jnp.concatenate([near, far], axis=1)
    rev = by_dist[:, ::-1]
    lv = rev.shape[1]
    skew = jnp.tile(rev, (1, ATT_TQ + 1))[:, :ATT_TQ * (lv + 1)].reshape(HEADS, ATT_TQ, lv + 1)[:, :, :ATT_W]
    bias = skew[:, ::-1, :]
    r = jnp.arange(ATT_TQ)[:, None]
    col = jnp.arange(ATT_W)[None, :]
    dchunk = (LEFT_CHUNKS + r // CHUNK) - col // CHUNK
    visible = (dchunk >= 0) & (dchunk <= LEFT_CHUNKS)
    return jnp.where(visible[None], bias, NEG_INF)


def _qrows(n, s=0):
    return pl.BlockSpec((ATT_TQ, n), lambda i: (i, s))


def _att_views_back(d, s):
    return pl.BlockSpec((ATT_TQ, MIX), lambda i: (jnp.maximum(i - (ATT_NV - 1) + d, 0), s))


_PAIR = 2 * HEAD_DIM


def _att_pair(refs, pp):
    sl = slice(pp * _PAIR, (pp + 1) * _PAIR)
    if isinstance(refs, (tuple, list)):
        return jnp.concatenate([r[:, sl] for r in refs], axis=0)
    return refs[:, sl]


def _att_fwd(h, table, name):
    T = h.shape[0]

    def body(q_ref, *refs):
        k_refs = refs[:ATT_NV]
        v_refs = refs[ATT_NV:2 * ATT_NV]
        tb_ref = refs[2 * ATT_NV]
        o_ref, oh_ref, lse_ref = refs[2 * ATT_NV + 1:]
        i = pl.program_id(0)
        col = lax.broadcasted_iota(jnp.int32, (1, ATT_W), 1)
        kvalid = (col + (i - (ATT_NV - 1)) * ATT_TQ) >= 0
        first = lax.broadcasted_iota(jnp.int32, (1, _PAIR), 1) < HEAD_DIM
        lses = []
        for pp in range(HEADS // 2):
            qp = _att_pair(q_ref, pp) * (HEAD_DIM ** -0.5)
            kp = _att_pair(k_refs, pp).astype(BF16)
            vp = _att_pair(v_refs, pp).astype(BF16)
            outs = []
            for e in range(2):
                half = first if e == 0 else jnp.logical_not(first)
                s = _dot_nt(jnp.where(half, qp, 0.0).astype(BF16), kp) + tb_ref[2 * pp + e]
                s = jnp.where(kvalid, s, NEG_INF)
                m = jnp.max(s, axis=-1, keepdims=True)
                p = jnp.exp(s - m)
                l = jnp.sum(p, axis=-1, keepdims=True)
                outs.append(_dot(p.astype(BF16), vp) / l)
                lses.append(m + jnp.log(l))
            o = jnp.where(first, outs[0], outs[1])
            o_ref[:, pp * _PAIR:(pp + 1) * _PAIR] = o
            oh_ref[:, pp * _PAIR:(pp + 1) * _PAIR] = o.astype(BF16)
        lse_ref[...] = jnp.concatenate(lses, axis=1)

    kviews = [_att_views_back(d, 1) for d in range(ATT_NV)]
    vviews = [_att_views_back(d, 2) for d in range(ATT_NV)]
    return _pcall(body, name=name, grid=(T // ATT_TQ,),
                  in_specs=[_qrows(MIX)] + kviews + vviews + [_full((HEADS, ATT_TQ, ATT_W))],
                  out_specs=[_qrows(MIX), _qrows(MIX), _qrows(HEADS)],
                  out_shape=[_sds((T, MIX), F32), _sds((T, MIX), BF16), _sds((T, HEADS), F32)],
                  sem=("parallel",))(h, *([h] * (2 * ATT_NV)), table)


def _att_bwd_q(h, do, o, lse, table, name):
    T = h.shape[0]

    def body(q_ref, *refs):
        k_refs = refs[:ATT_NV]
        v_refs = refs[ATT_NV:2 * ATT_NV]
        do_ref, o_ref, lse_ref, tb_ref, dq_ref, dl_ref, dtb_ref = refs[2 * ATT_NV:]
        i = pl.program_id(0)
        col = lax.broadcasted_iota(jnp.int32, (1, ATT_W), 1)
        kvalid = (col + (i - (ATT_NV - 1)) * ATT_TQ) >= 0

        first = lax.broadcasted_iota(jnp.int32, (1, _PAIR), 1) < HEAD_DIM

        @pl.when(i == 0)
        def _():
            dtb_ref[...] = jnp.zeros_like(dtb_ref)

        deltas = []
        for pp in range(HEADS // 2):
            qp = _att_pair(q_ref, pp) * (HEAD_DIM ** -0.5)
            kp = _att_pair(k_refs, pp).astype(BF16)
            vp = _att_pair(v_refs, pp).astype(BF16)
            dop = _att_pair(do_ref, pp)
            doo = dop * _att_pair(o_ref, pp)
            outs = []
            for e in range(2):
                hd = 2 * pp + e
                half = first if e == 0 else jnp.logical_not(first)
                s = _dot_nt(jnp.where(half, qp, 0.0).astype(BF16), kp) + tb_ref[hd]
                s = jnp.where(kvalid, s, NEG_INF)
                p = jnp.exp(s - lse_ref[:, hd:hd + 1])
                delta = jnp.sum(jnp.where(half, doo, 0.0), axis=-1, keepdims=True)
                dp = _dot_nt(jnp.where(half, dop, 0.0).astype(BF16), vp)
                ds = p * (dp - delta)
                dtb_ref[hd] += ds
                outs.append(_dot(ds.astype(BF16), kp))
                deltas.append(delta)
            dq = jnp.where(first, outs[0], outs[1]) * (HEAD_DIM ** -0.5)
            dq_ref[:, pp * _PAIR:(pp + 1) * _PAIR] = dq.astype(BF16)
        dl_ref[...] = jnp.concatenate(deltas, axis=1)

    kviews = [_att_views_back(d, 1) for d in range(ATT_NV)]
    vviews = [_att_views_back(d, 2) for d in range(ATT_NV)]
    tb = _full((HEADS, ATT_TQ, ATT_W))
    return _pcall(body, name=name, grid=(T // ATT_TQ,),
                  in_specs=[_qrows(MIX)] + kviews + vviews + [_qrows(MIX), _qrows(MIX), _qrows(HEADS), tb],
                  out_specs=[_qrows(MIX), _qrows(HEADS), tb],
                  out_shape=[_sds((T, MIX), BF16), _sds((T, HEADS), F32), _sds((HEADS, ATT_TQ, ATT_W), F32)])(
                      h, *([h] * (2 * ATT_NV)), do, o, lse, table)


def _att_table_by_key(table):
    t = table.reshape(HEADS, ATT_TQ, ATT_NV, ATT_TQ)[:, :, ::-1, :]
    return jnp.transpose(t, (0, 3, 2, 1)).reshape(HEADS, ATT_TQ, ATT_W)


def _att_bwd_kv(h, do, lse_t, delta_t, table_k, name):
    T = h.shape[0]
    nb = T // ATT_TQ

    def fwd_view(d, s=0):
        return pl.BlockSpec((ATT_TQ, MIX), lambda j: (jnp.minimum(j + d, nb - 1), s))

    def row_view(d):
        return pl.BlockSpec((HEADS, ATT_TQ), lambda j: (0, jnp.minimum(j + d, nb - 1)))

    def body(k_ref, v_ref, *refs):
        q_refs = refs[:ATT_NV]
        do_refs = refs[ATT_NV:2 * ATT_NV]
        lse_refs = refs[2 * ATT_NV:3 * ATT_NV]
        dl_refs = refs[3 * ATT_NV:4 * ATT_NV]
        tb_ref, dk_ref, dv_ref = refs[4 * ATT_NV:]
        j = pl.program_id(0)
        view = lax.broadcasted_iota(jnp.int32, (1, ATT_W), 1) // ATT_TQ
        valid = (j + view) <= nb - 1
        first = lax.broadcasted_iota(jnp.int32, (1, _PAIR), 1) < HEAD_DIM
        for pp in range(HEADS // 2):
            kp = _att_pair(k_ref, pp)
            vp = _att_pair(v_ref, pp)
            qs = (_att_pair(q_refs, pp) * (HEAD_DIM ** -0.5)).astype(BF16)
            dos = _att_pair(do_refs, pp).astype(BF16)
            dks, dvs = [], []
            for e in range(2):
                hd = 2 * pp + e
                half = first if e == 0 else jnp.logical_not(first)
                lses = jnp.concatenate([r[hd:hd + 1, :] for r in lse_refs], axis=1)
                dls = jnp.concatenate([r[hd:hd + 1, :] for r in dl_refs], axis=1)
                st = _dot_nt(jnp.where(half, kp, 0.0).astype(BF16), qs) + tb_ref[hd]
                pt = jnp.where(valid, jnp.exp(st - lses), 0.0)
                dvs.append(_dot(pt.astype(BF16), dos))
                dst = pt * (_dot_nt(jnp.where(half, vp, 0.0).astype(BF16), dos) - dls)
                dks.append(_dot(dst.astype(BF16), qs))
            dk_ref[:, pp * _PAIR:(pp + 1) * _PAIR] = jnp.where(first, dks[0], dks[1]).astype(BF16)
            dv_ref[:, pp * _PAIR:(pp + 1) * _PAIR] = jnp.where(first, dvs[0], dvs[1]).astype(BF16)

    qv = [fwd_view(d, 0) for d in range(ATT_NV)]
    dov = [fwd_view(d) for d in range(ATT_NV)]
    rows = [row_view(d) for d in range(ATT_NV)]
    return _pcall(body, name=name, grid=(nb,),
                  in_specs=[_qrows(MIX, 1), _qrows(MIX, 2)] + qv + dov + rows + rows
                  + [_full((HEADS, ATT_TQ, ATT_W))],
                  out_specs=[_qrows(MIX), _qrows(MIX)], out_shape=[_sds((T, MIX), BF16), _sds((T, MIX), BF16)],
                  sem=("parallel",))(h, h, *([h] * ATT_NV), *([do] * ATT_NV), *([lse_t] * ATT_NV),
                                     *([delta_t] * ATT_NV), table_k)


def _all_gather(x, name):
    R, C = x.shape

    def body(x_ref, out_ref, send_sems, recv_sems, local_sem):
        xi, yi, ci = lax.axis_index("x"), lax.axis_index("y"), lax.axis_index("c")
        me, sibling = (xi, yi, ci), (xi, yi, 1 - ci)
        chips = [(1 - xi, yi), (xi, 1 - yi), (1 - xi, 1 - yi)]

        def slot(px, py, pc):
            return out_ref.at[4 * px + 2 * py + pc]

        def copy(k, block, to, src=None):
            return pltpu.make_async_remote_copy(
                src_ref=slot(*block) if src is None else src, dst_ref=slot(*block),
                send_sem=send_sems.at[k], recv_sem=recv_sems.at[k], device_id=to, device_id_type=_MESH)

        mine = pltpu.make_async_copy(x_ref, slot(*me), local_sem)
        mine.start()
        first = [copy(0, me, sibling, src=x_ref)]
        first += [copy(1 + j, me, (*chip, ci), src=x_ref) for j, chip in enumerate(chips)]
        for cp in first:
            cp.start()
        passed = [copy(4 + j, (*chip, ci), sibling) for j, chip in enumerate(chips)]
        for j, chip in enumerate(chips):
            copy(1 + j, (*chip, ci), me).wait_recv()
            passed[j].start()
        copy(0, sibling, me).wait_recv()
        for j, chip in enumerate(chips):
            copy(4 + j, (*chip, 1 - ci), me).wait_recv()
        for cp in first + passed:
            cp.wait_send()
        mine.wait()

    return pl.pallas_call(
        body, name=name, out_shape=_sds((N_DEV, R, C), x.dtype), in_specs=[_ANY], out_specs=_ANY,
        scratch_shapes=[pltpu.SemaphoreType.DMA((7,)), pltpu.SemaphoreType.DMA((7,)), pltpu.SemaphoreType.DMA(())],
    )(x)


def _a2a_copies(s_ref, r_ref, send_sems, recv_sems, local_sem):
    xi, yi, ci = lax.axis_index("x"), lax.axis_index("y"), lax.axis_index("c")
    me = 4 * xi + 2 * yi + ci

    def mine():
        return pltpu.make_async_copy(s_ref.at[me], r_ref.at[me], local_sem)

    def remote(m, sending):
        px = 1 - xi if m & 4 else xi
        py = 1 - yi if m & 2 else yi
        pc = 1 - ci if m & 1 else ci
        peer = 4 * px + 2 * py + pc
        src, dst = (s_ref.at[peer], r_ref.at[me]) if sending else (s_ref.at[me], r_ref.at[peer])
        return pltpu.make_async_remote_copy(src_ref=src, dst_ref=dst, send_sem=send_sems.at[m - 1],
                                            recv_sem=recv_sems.at[m - 1], device_id=(px, py, pc), device_id_type=_MESH)

    def start():
        mine().start()
        for m in range(1, N_DEV):
            remote(m, True).start()

    def wait():
        for m in range(1, N_DEV):
            remote(m, False).wait_recv()
        for m in range(1, N_DEV):
            remote(m, True).wait_send()
        mine().wait()

    return start, wait


_A2A_SEMS = [pltpu.SemaphoreType.DMA((7,)), pltpu.SemaphoreType.DMA((7,)), pltpu.SemaphoreType.DMA(())]


def _all_to_all(s, name):
    def body(s_ref, r_ref, send_sems, recv_sems, local_sem):
        start, wait = _a2a_copies(s_ref, r_ref, send_sems, recv_sems, local_sem)
        start()
        wait()

    return pl.pallas_call(
        body, name=name, out_shape=_sds(s.shape, s.dtype), in_specs=[_ANY], out_specs=_ANY,
        scratch_shapes=list(_A2A_SEMS))(s)


def _riding(body, n_in, n_out, n_ex, first, last):
    def wrapped(*refs):
        ins = refs[:n_in]
        sends = refs[n_in:n_in + n_ex]
        outs = refs[n_in + n_ex:n_in + n_ex + n_out]
        recvs = refs[n_in + n_ex + n_out:n_in + 2 * n_ex + n_out]
        rest = refs[n_in + 2 * n_ex + n_out:]
        scratch, sems = rest[:len(rest) - 3 * n_ex], rest[len(rest) - 3 * n_ex:]

        @pl.when(first())
        def _():
            for k in range(n_ex):
                _a2a_copies(sends[k], recvs[k], *sems[3 * k:3 * k + 3])[0]()

        body(*ins, *outs, *scratch)

        @pl.when(last())
        def _():
            for k in range(n_ex):
                _a2a_copies(sends[k], recvs[k], *sems[3 * k:3 * k + 3])[1]()

    return wrapped


_ADAMW_BLOCK_BYTES = 6 * 1024 * 1024


def _adamw(parts, row_off, w, m, v, name):
    R, C = w.shape
    tr = None
    for cand in (512, 256, 128, 64, 32, 16):
        step_bytes = cand * C * (N_DEV * parts.dtype.itemsize + 7 * 4)
        if R % cand == 0 and row_off % cand == 0 and step_bytes <= _ADAMW_BLOCK_BYTES:
            tr = cand
            break
    assert tr is not None, (R, C, row_off)
    off = row_off // tr
    c1 = 1.0 - ADAM_B1 ** ADAM_STEP
    c2 = 1.0 - ADAM_B2 ** ADAM_STEP

    def body(p_ref, w_ref, m_ref, v_ref, g_ref, d_ref, mo_ref, vo_ref):
        g = p_ref[0].astype(F32)
        for j in range(1, N_DEV):
            g = g + p_ref[j].astype(F32)
        mn = ADAM_B1 * m_ref[...] + (1.0 - ADAM_B1) * g
        vn = ADAM_B2 * v_ref[...] + (1.0 - ADAM_B2) * (g * g)
        m_hat = mn / c1
        v_hat = vn / c2
        g_ref[...] = g
        d_ref[...] = -ADAM_LR * (m_hat / (jnp.sqrt(v_hat) + ADAM_EPS) + ADAM_WD * w_ref[...])
        mo_ref[...] = mn
        vo_ref[...] = vn

    blk = _rows(tr, C)
    return _pcall(body, name=name, grid=(R // tr,),
                  in_specs=[pl.BlockSpec((N_DEV, tr, C), lambda i: (0, off + i, 0)), blk, blk, blk],
                  out_specs=[blk] * 4, out_shape=[_sds((R, C), F32)] * 4,
                  sem=("parallel",))(parts, w, m, v)


def _piece_rows(size):
    return -(-size // PIECE) * 16


def _pack_rows(arrays, lead=()):
    parts = []
    total = 0
    for a in arrays:
        flat = a.reshape(lead + (-1,))
        size = flat.shape[-1]
        rows = _piece_rows(size)
        pad = [(0, 0)] * len(lead) + [(0, rows * FLAT_COLS - size)]
        parts.append(jnp.pad(flat, pad).reshape(lead + (rows, FLAT_COLS)))
        total += rows
    tail = -total % FLAT_ROWS
    if tail:
        parts.append(jnp.zeros(lead + (tail, FLAT_COLS), parts[0].dtype))
    return jnp.concatenate(parts, axis=len(lead))


def _unpack_rows(buf, shapes, lead=()):
    out = []
    row = 0
    nl = len(lead)
    for shape in shapes:
        size = math.prod(shape)
        rows = _piece_rows(size)
        piece = lax.slice_in_dim(buf, row, row + rows, axis=nl).reshape(lead + (rows * FLAT_COLS,))
        out.append(lax.slice_in_dim(piece, 0, size, axis=nl).reshape(lead + tuple(shape)))
        row += rows
    return out


def _to_blocks(full, axis):
    shp = full.shape
    split = full.reshape(shp[:axis] + (N_DEV, shp[axis] // N_DEV) + shp[axis + 1:])
    return jnp.moveaxis(split, axis, 0)


def _from_blocks(blocks, axis):
    shp = blocks.shape[1:]
    moved = jnp.moveaxis(blocks, 0, axis)
    return moved.reshape(shp[:axis] + (N_DEV * shp[axis],) + shp[axis + 1:])


def _row_blocks(g):
    return g.reshape(N_DEV, g.shape[0] // N_DEV, g.shape[1])


def _early_sends(G, i):
    ffn = jnp.concatenate([_row_blocks(G['ffn_w_down'][i]), _row_blocks(G['ple_w_gate'][i])], axis=1)
    return dict(up=G['ffn_w_up'][i], ffn=ffn)


def _late_sends(G, i):
    j = i // 2
    ev = i % 2 == 0
    return dict(out=_row_blocks(G['ev_w_out' if ev else 'od_w_out'][j]), inn=G['ev_w_in' if ev else 'od_w_in'][j])


def _local_step(x, p, tgt, W, overlap=False):
    wup = W['ffn_w_up'].reshape(2, FF_J, DEPTH, D_MODEL, FF_NB)
    wd4 = W['ffn_w_down'].reshape(DEPTH, FF_J, FF_NB, D_MODEL)
    ln = {n: _vecs(W[n]) for n in ('ln_mix_g', 'ln_mix_b', 'ln_ffn_g', 'ln_ffn_b', 'ple_b_gate')}
    bglu = _vecs(W['ev_b_glu'])
    pscale = _vecs(W['od_pool_scale'])
    saved = []
    x0 = x
    x0h = x.astype(BF16)
    for i in range(DEPTH):
        L = f"L{i}_"
        j = i // 2
        s = dict(x0=x0, x0h=x0h)
        if i % 2 == 0:
            h = _mm_in(x0h, W['ev_w_in'], j, L + "mm_in")
            params = tuple(W[n][j] for n in ('ev_lambda_re', 'ev_lambda_im', 'ev_log_dt', 'ev_b_re', 'ev_b_im',
                                             'ev_c_re', 'ev_c_im', 'ev_d'))
            (r0, qt, pm, a1, a2), op_vjp = jax.vjp(_s5_operators, *params)
            umat = _to_chunks(h[:, :MIX])
            st = jnp.transpose(_s5_chunk_state(umat, pm, L + "s5_state"), (1, 0, 2))
            xp_t = _s5_scan_fwd(st, a1, a2, L + "s5_scan")
            xprev = jnp.transpose(xp_t, (1, 0, 2))
            y = _from_chunks(_s5_output(umat, xprev, r0, qt, L + "s5_out"))
            ya, gh = _glu_fwd(y, W['ev_w_glu'], bglu, j, L + "glu")
            yb = _conv_fwd(h, W['ev_conv_w'], j, L + "conv")
            s.update(op_vjp=op_vjp, r0=r0, qt=qt, pm=pm, a1=a1, a2=a2, umat=umat, xp_t=xp_t, xprev=xprev, y=y, gh=gh)
            wo, win = W['ev_w_out'], W['ev_w_in']
        else:
            h = _mm_in(x0h, W['od_w_in'], j, L + "mm_in")
            table, tb_vjp = jax.vjp(_att_bias_table, W['od_rel_bias'][j])
            of, ya, lse = _att_fwd(h, table, L + "att")
            yb = _pool_fwd(h, W['od_pool_w'], pscale, j, L + "pool")
            s.update(table=table, tb_vjp=tb_vjp, of=of, lse=lse)
            wo, win = W['od_w_out'], W['od_w_in']
        g1, b1 = (ln['ln_mix_g'], i), (ln['ln_mix_b'], i)
        g2, b2 = (ln['ln_ffn_g'], i), (ln['ln_ffn_b'], i)
        xhat1, rstd1, x1h = _mm_out_ln(ya, yb, wo, j, x0, g1, b1, L + "mm_out_ln")
        hf, a3 = _mm_up(x1h, wup, i, L + "mm_up")
        xhat2, rstd2, x2h = _mm_down_ln(a3, wd4, i, xhat1, g1, b1, g2, b2, L + "mm_down_ln")
        x3, x3h = _mm_ple(x2h, xhat2, ln['ln_ffn_g'], ln['ln_ffn_b'], p, W['ple_w_gate'], ln['ple_b_gate'],
                          W['ple_w_proj'], i, L + "mm_ple")
        s.update(h=h, win=win, wo=wo, ya=ya, yb=yb, xhat1=xhat1, rstd1=rstd1, x1h=x1h, hf=hf, a3=a3, xhat2=xhat2,
                 rstd2=rstd2, x2h=x2h, g1=g1)
        saved.append(s)
        x0, x0h = x3, x3h

    dx, loss = _loss_head(x0, tgt, "loss_head")

    G = {n: [None] * W[n].shape[0 if n not in ('ev_w_in', 'od_w_in', 'ffn_w_up') else 1] for n in WEIGHT_NAMES}
    landed = {i: {} for i in range(DEPTH)}
    pending = None
    for i in reversed(range(DEPTH)):
        L = f"L{i}_"
        j = i // 2
        s = saved[i]
        dr2, dr2h, dpreh, dpph, dbg, dg2, db2 = _ple_bwd(
            dx, s['x2h'], p, W['ple_w_gate'], ln['ple_b_gate'], W['ple_w_proj'], i, s['xhat2'], s['rstd2'],
            ln['ln_ffn_g'], L + "ple_bwd")
        G['ple_w_gate'][i] = _mm_tn(s['x2h'], dpreh, L + "dw_gate")
        G['ple_w_proj'][i] = _mm_tn(p, dpph, L + "dw_proj", a_layer=i)
        G['ple_b_gate'][i] = dbg[0]
        G['ln_ffn_g'][i] = dg2[0]
        G['ln_ffn_b'][i] = db2[0]
        dhf, got = _ffn_bwd1(dr2h, wd4, i, s['hf'], L + "ffn_bwd1",
                             ride=(pending['out'], pending['inn']) if pending else ())
        if pending:
            landed[i + 1].update(out=got[0], inn=got[1])
        G['ffn_w_down'][i] = _mm_tn_ablk(s['a3'], dr2h, L + "dw_down").reshape(D_FF, D_MODEL)
        T = dhf.shape[2]
        G['ffn_w_up'][i] = _mm_tn_bblk(s['x1h'], dhf.reshape(N_DEV, T, FF_NB), L + "dw_up")
        early = _early_sends(G, i) if overlap else None
        (dr1, dr1h, dg1, db1), got = _ffn_bwd2(dhf, wup, i, dr2, s['xhat1'], s['rstd1'], s['g1'], L + "ffn_bwd2",
                                               ride=(early['up'], early['ffn']) if overlap else ())
        if overlap:
            landed[i].update(up=got[0], ffn=got[1])
        G['ln_mix_g'][i] = dg1[0]
        G['ln_mix_b'][i] = db1[0]
        dya, dyb = _out_bwd(dr1h, s['wo'], j, L + "out_bwd")
        dwo = jnp.concatenate([_mm_tn(s['ya'], dr1h, L + "dw_out_a"), _mm_tn(s['yb'], dr1h, L + "dw_out_b")], axis=0)
        if i % 2 == 0:
            G['ev_w_out'][j] = dwo
            dy, dzh, dbglu = _glu_bwd(s['y'], dya, W['ev_w_glu'], bglu, j, L + "glu_bwd")
            G['ev_w_glu'][j] = _mm_tn(s['gh'], dzh, L + "dw_glu")
            G['ev_b_glu'][j] = dbglu[0]
            dymat = _to_chunks(dy)
            dxp_t = jnp.transpose(_s5_bwd_state(dymat, s['qt'], L + "s5_bwd_state"), (1, 0, 2))
            ds_t, da1, da2 = _s5_scan_bwd(dxp_t, s['xp_t'], s['a1'], s['a2'], L + "s5_scan_bwd")
            dumat, dr0, dqt, dpm = _s5_bwd_main(s['umat'], s['xprev'], dymat, jnp.transpose(ds_t, (1, 0, 2)),
                                                s['r0'], s['pm'], L + "s5_bwd")
            dparams = s['op_vjp']((dr0, dqt, dpm, da1, da2))
            for n, dpar in zip(('ev_lambda_re', 'ev_lambda_im', 'ev_log_dt', 'ev_b_re', 'ev_b_im', 'ev_c_re',
                                'ev_c_im', 'ev_d'), dparams):
                G[n][j] = dpar
            dua = _from_chunks(dumat)
            dh, dcw = _conv_bwd(dyb, s['h'], dua, W['ev_conv_w'], j, L + "conv_bwd")
            G['ev_conv_w'][j] = dcw[:3]
            wname = 'ev_w_in'
        else:
            G['od_w_out'][j] = dwo
            dq, delta, dtable = _att_bwd_q(s['h'], dya, s['of'], s['lse'], s['table'], L + "att_bwd_q")
            dk, dv = _att_bwd_kv(s['h'], dya, s['lse'].T, delta.T, _att_table_by_key(s['table']), L + "att_bwd_kv")
            G['od_rel_bias'][j] = s['tb_vjp'](dtable)[0]
            dh, dpw, dsc = _pool_bwd(dyb, s['h'], dq, dk, dv, W['od_pool_w'], pscale, j, L + "pool_bwd")
            G['od_pool_w'][j] = dpw
            G['od_pool_scale'][j] = dsc[0]
            wname = 'od_w_in'
        G[wname][j] = _mm_tn(s['x0h'], dh, L + "dw_in", blocked_n=IN_NB)
        dx = _in_bwd(dh, s['win'], j, dr1, L + "in_bwd")
        pending = _late_sends(G, i) if overlap else None

    if overlap:
        return loss, dx, G, landed, pending
    return loss, dx, G


def _slab(a):
    return a.reshape(-1, a.shape[-1])


def _gather_weights(W):
    full = {n: W[n] for n in REPLICATED}
    slabs = [_slab(W[n].astype(BF16)) for n in ROW_SHARDED]
    got = _all_gather(jnp.concatenate(slabs, axis=0), "gather_w_rows")
    row = 0
    for n, sl in zip(ROW_SHARDED, slabs):
        L, k8, d = W[n].shape
        piece = lax.slice_in_dim(got, row, row + sl.shape[0], axis=1).reshape(N_DEV, L, k8, d)
        full[n] = jnp.swapaxes(piece, 0, 1).reshape(L, N_DEV * k8, d)
        row += sl.shape[0]
    got = _all_gather(_slab(W['ffn_w_up'].astype(BF16)), "gather_w_up")
    full['ffn_w_up'] = got.reshape((N_DEV,) + W['ffn_w_up'].shape)
    slabs = [_slab(W[n].astype(BF16)) for n in COL_IN]
    got = _all_gather(jnp.concatenate(slabs, axis=0), "gather_w_in")
    row = 0
    for n, sl in zip(COL_IN, slabs):
        full[n] = lax.slice_in_dim(got, row, row + sl.shape[0], axis=1).reshape((N_DEV,) + W[n].shape)
        row += sl.shape[0]
    got = _all_gather(_pack_rows([W[n].astype(BF16) for n in SMALL_SHARDED]), "gather_w_small")
    shapes = [W[n].shape for n in SMALL_SHARDED]
    for n, blocks in zip(SMALL_SHARDED, _unpack_rows(got, shapes, lead=(N_DEV,))):
        full[n] = _from_blocks(blocks, SHARD_AXIS[n])
    for n in ('ev_conv_w', 'od_pool_scale'):
        full[n] = full[n].astype(F32)
    return full


def _step(x, p, tgt, W, M, V):
    full = _gather_weights(W)
    loss, dx, G, landed, tail = _local_step(x[0], p[:, 0], tgt[0], full, overlap=True)
    landed[0].update({k: _all_to_all(v, "scatter_g_" + k) for k, v in tail.items()})
    res = {}

    def update(parts, n):
        shape = W[n].shape
        outs = _adamw(parts, 0, _slab(W[n]), _slab(M[n]), _slab(V[n]), "adamw_" + n)
        for kind, a in zip(('grad', 'delta', 'm', 'v'), outs):
            res[kind, n] = a.reshape(shape)

    def over_layers(layers, key, lo=None, hi=None):
        got = [landed[i][key] for i in layers]
        if lo is not None:
            got = [lax.slice_in_dim(g, lo, hi, axis=1) for g in got]
        return jnp.concatenate(got, axis=1)

    even, odd, every = (0, 2), (1, 3), (0, 1, 2, 3)
    r_out = D_MODEL // N_DEV
    r_down = D_FF // N_DEV
    update(over_layers(even, 'out'), 'ev_w_out')
    update(over_layers(odd, 'out'), 'od_w_out')
    update(over_layers(every, 'ffn', 0, r_down), 'ffn_w_down')
    update(over_layers(every, 'ffn', r_down, r_down + r_out), 'ple_w_gate')
    update(over_layers(every, 'up'), 'ffn_w_up')
    update(over_layers(even, 'inn'), 'ev_w_in')
    update(over_layers(odd, 'inn'), 'od_w_in')

    shapes = [W[n].shape for n in SMALL_SHARDED]
    send = _pack_rows([_to_blocks(jnp.stack(G[n], axis=0), SHARD_AXIS[n]).astype(BF16) for n in SMALL_SHARDED],
                      lead=(N_DEV,))
    parts = _all_to_all(send, "scatter_g_small")
    outs = _adamw(parts, 0, _pack_rows([W[n] for n in SMALL_SHARDED]), _pack_rows([M[n] for n in SMALL_SHARDED]),
                  _pack_rows([V[n] for n in SMALL_SHARDED]), "adamw_small")
    for kind, buf in zip(('grad', 'delta', 'm', 'v'), outs):
        for n, a in zip(SMALL_SHARDED, _unpack_rows(buf, shapes)):
            res[kind, n] = a

    repl_shapes = [W[n].shape for n in REPLICATED]
    small = _pack_rows([jnp.stack(G[n], axis=0) for n in REPLICATED] + [loss])
    parts = _all_gather(small, "gather_g_replicated")
    zero = jnp.zeros((1, 1), F32)
    outs = _adamw(parts, 0, _pack_rows([W[n] for n in REPLICATED] + [zero]),
                  _pack_rows([M[n] for n in REPLICATED] + [zero]),
                  _pack_rows([V[n] for n in REPLICATED] + [zero]), "adamw_replicated")
    for kind, buf in zip(('grad', 'delta', 'm', 'v'), outs):
        arrays = _unpack_rows(buf, repl_shapes + [(1, 1)])
        for n, a in zip(REPLICATED, arrays):
            res[kind, n] = a
        if kind == 'grad':
            total_loss = arrays[-1].reshape(())

    out = [total_loss, dx[None]]
    for kind in ('grad', 'delta', 'm', 'v'):
        out += [res[kind, n] for n in WEIGHT_NAMES]
    return tuple(out)


def kernel(x, p, ev_w_in, ev_lambda_re, ev_lambda_im, ev_log_dt, ev_b_re, ev_b_im, ev_c_re, ev_c_im, ev_d, ev_w_glu, ev_b_glu, ev_conv_w, ev_w_out, od_w_in, od_rel_bias, od_pool_w, od_pool_scale, od_w_out, ln_mix_g, ln_mix_b, ln_ffn_g, ln_ffn_b, ffn_w_up, ffn_w_down, ple_w_proj, ple_w_gate, ple_b_gate, loss_target, m_ev_w_in, m_ev_lambda_re, m_ev_lambda_im, m_ev_log_dt, m_ev_b_re, m_ev_b_im, m_ev_c_re, m_ev_c_im, m_ev_d, m_ev_w_glu, m_ev_b_glu, m_ev_conv_w, m_ev_w_out, m_od_w_in, m_od_rel_bias, m_od_pool_w, m_od_pool_scale, m_od_w_out, m_ln_mix_g, m_ln_mix_b, m_ln_ffn_g, m_ln_ffn_b, m_ffn_w_up, m_ffn_w_down, m_ple_w_proj, m_ple_w_gate, m_ple_b_gate, v_ev_w_in, v_ev_lambda_re, v_ev_lambda_im, v_ev_log_dt, v_ev_b_re, v_ev_b_im, v_ev_c_re, v_ev_c_im, v_ev_d, v_ev_w_glu, v_ev_b_glu, v_ev_conv_w, v_ev_w_out, v_od_w_in, v_od_rel_bias, v_od_pool_w, v_od_pool_scale, v_od_w_out, v_ln_mix_g, v_ln_mix_b, v_ln_ffn_g, v_ln_ffn_b, v_ffn_w_up, v_ffn_w_down, v_ple_w_proj, v_ple_w_gate, v_ple_b_gate):
    given = dict(locals())
    W = {n: given[n] for n in WEIGHT_NAMES}
    M = {n: given["m_" + n] for n in WEIGHT_NAMES}
    V = {n: given["v_" + n] for n in WEIGHT_NAMES}
    return _step(x, p, loss_target, W, M, V)
```

```python
import math

import jax
import jax.numpy as jnp
from jax import lax
from jax.experimental import pallas as pl
from jax.experimental.pallas import tpu as pltpu

F32 = jnp.float32
BF16 = jnp.bfloat16
HI = lax.Precision.HIGHEST

D_MODEL = 1024
DEPTH = 4
CHUNK = 64
MIX = 512
S5_GROUP = 16
S5_GROUPS = 32
S5_STATE = 64
HEADS = 8
HEAD_DIM = 64
LEFT_CHUNKS = 8
MAX_REL = 128
POOL_WINDOWS = (2, 4, 8, 16)
POOL_GROUP = 128
D_FF = 2816
D_PLE = 256
ALPHA = (2 * DEPTH) ** 0.25
LN_EPS = 1e-5
NEG_INF = -1e30
ADAM_LR = 0.001
ADAM_B1 = 0.9
ADAM_B2 = 0.999
ADAM_EPS = 1e-08
ADAM_WD = 0.01
ADAM_STEP = 10
N_DEV = 8

WEIGHT_NAMES = ['ev_w_in', 'ev_lambda_re', 'ev_lambda_im', 'ev_log_dt', 'ev_b_re', 'ev_b_im', 'ev_c_re', 'ev_c_im',
                'ev_d', 'ev_w_glu', 'ev_b_glu', 'ev_conv_w', 'ev_w_out', 'od_w_in', 'od_rel_bias', 'od_pool_w',
                'od_pool_scale', 'od_w_out', 'ln_mix_g', 'ln_mix_b', 'ln_ffn_g', 'ln_ffn_b', 'ffn_w_up', 'ffn_w_down',
                'ple_w_proj', 'ple_w_gate', 'ple_b_gate']
SHARD_AXIS = {'ev_w_in': 2, 'ev_w_glu': 1, 'ev_conv_w': 2, 'ev_w_out': 1, 'od_w_in': 2, 'od_pool_scale': 1,
              'od_w_out': 1, 'ffn_w_up': 2, 'ffn_w_down': 1, 'ple_w_proj': 2, 'ple_w_gate': 1}
REPLICATED = [n for n in WEIGHT_NAMES if n not in SHARD_AXIS]
ROW_SHARDED = ['ev_w_out', 'od_w_out', 'ffn_w_down', 'ple_w_gate']
COL_IN = ['ev_w_in', 'od_w_in']
SMALL_SHARDED = ['ev_w_glu', 'ev_conv_w', 'od_pool_scale', 'ple_w_proj']

VMEM_LIMIT = 48 * 1024 * 1024
TM = 512
IN_NB = 4 * MIX // N_DEV
FF_NB = 2 * D_FF // N_DEV
FF_J = N_DEV // 2
S5_LC = 32
S5_LW = S5_LC * S5_GROUP
ATT_TQ = 128
ATT_NV = LEFT_CHUNKS * CHUNK // ATT_TQ + 1
ATT_W = ATT_NV * ATT_TQ
FLAT_COLS = 1024
FLAT_ROWS = 256
PIECE = 16 * FLAT_COLS


_MESH = pl.DeviceIdType.MESH
_ANY = pl.BlockSpec(memory_space=pl.ANY)


def _sds(shape, dt):
    return jax.ShapeDtypeStruct(shape, dt)


def _pcall(body, *, name, grid, in_specs, out_specs, out_shape, scratch=(), sem=None):
    sem = sem or ("arbitrary",) * len(grid)
    return pl.pallas_call(
        body, name=name, grid=grid, in_specs=in_specs, out_specs=out_specs, out_shape=out_shape,
        scratch_shapes=scratch,
        compiler_params=pltpu.CompilerParams(dimension_semantics=sem, vmem_limit_bytes=VMEM_LIMIT))


def _rows(tm, n):
    return pl.BlockSpec((tm, n), lambda i: (i, 0))


def _cols(tm, s):
    return pl.BlockSpec((tm, MIX), lambda i: (i, s))


def _full(shape):
    nd = len(shape)
    return pl.BlockSpec(shape, lambda *_: (0,) * nd)


def _layer(shape, l):
    nd = len(shape)
    return pl.BlockSpec((None,) + tuple(shape), lambda *_: (l,) + (0,) * nd)


def _vecs(a):
    return a.reshape(a.shape[0], 1, a.shape[1])


def _dot(a, b, precision=None):
    return jnp.dot(a, b, preferred_element_type=F32, precision=precision)


def _dot_nt(a, b, precision=None):
    return lax.dot_general(a, b, (((1,), (1,)), ((), ())), preferred_element_type=F32, precision=precision)


def _dot_tn(a, b, precision=None):
    return lax.dot_general(a, b, (((0,), (0,)), ((), ())), preferred_element_type=F32, precision=precision)


def _split(a):
    hi = a.astype(BF16)
    return hi, (a - hi.astype(F32)).astype(BF16)


def _mm3(dot, a2, b2):
    return dot(a2[0], b2[0]) + (dot(a2[0], b2[1]) + dot(a2[1], b2[0]))


def _sigmoid(x):
    return 0.5 * jnp.tanh(0.5 * x) + 0.5


_GELU_C = math.sqrt(2.0 / math.pi)


def _gelu(x):
    return 0.5 * x * (1.0 + jnp.tanh(_GELU_C * (x + 0.044715 * x * x * x)))


def _gelu_grad(x):
    t = jnp.tanh(_GELU_C * (x + 0.044715 * x * x * x))
    return 0.5 * (1.0 + t) + 0.5 * x * (1.0 - t * t) * _GELU_C * (1.0 + 3.0 * 0.044715 * x * x)


def _ln_fwd(r, g, b):
    mu = jnp.mean(r, axis=-1, keepdims=True)
    xc = r - mu
    var = jnp.mean(xc * xc, axis=-1, keepdims=True)
    rstd = lax.rsqrt(var + LN_EPS)
    xhat = xc * rstd
    return xhat, rstd, xhat * g + b


def _ln_bwd(dx, xhat, rstd, g):
    dxh = dx * g
    m1 = jnp.mean(dxh, axis=-1, keepdims=True)
    m2 = jnp.mean(dxh * xhat, axis=-1, keepdims=True)
    return rstd * (dxh - m1 - xhat * m2)


def _colsum(x):
    return jnp.sum(x, axis=0, keepdims=True)


def _mm_in(xh, win8, l, name):
    T = xh.shape[0]

    def body(x_ref, w_ref, h_ref):
        x = x_ref[...]
        for b in range(N_DEV):
            h_ref[:, b * IN_NB:(b + 1) * IN_NB] = _dot(x, w_ref[b])

    return _pcall(body, name=name, grid=(T // TM,),
                  in_specs=[_rows(TM, D_MODEL),
                            pl.BlockSpec((N_DEV, None, D_MODEL, IN_NB), lambda i: (0, l, 0, 0))],
                  out_specs=_rows(TM, 4 * MIX), out_shape=_sds((T, 4 * MIX), F32),
                  sem=("parallel",))(xh, win8)


def _tile_of(n, cap):
    best = None
    for t in range(128, min(n, cap) + 1, 128):
        if n % t == 0:
            best = t
    assert best is not None, n
    return best


def _mm_tn(a, b, name, a_layer=None, blocked_n=None):
    T, M = a.shape[-2:]
    N = b.shape[1]
    tm = _tile_of(M, 1408)
    nbs = max(1, 512 // blocked_n) if blocked_n else 1
    tn = blocked_n * nbs if blocked_n else _tile_of(N, 1024 if tm <= 512 else 512)
    tk = min(T, 1024)
    nk = T // tk

    def body(a_ref, b_ref, o_ref, acc):
        k = pl.program_id(2)

        @pl.when(k == 0)
        def _():
            acc[...] = jnp.zeros_like(acc)

        acc[...] += _dot_tn(a_ref[...].astype(BF16), b_ref[...].astype(BF16))

        @pl.when(k == nk - 1)
        def _():
            if blocked_n:
                for sb in range(nbs):
                    o_ref[sb] = acc[:, sb * blocked_n:(sb + 1) * blocked_n].astype(BF16)
            else:
                o_ref[...] = acc[...].astype(BF16)

    if a_layer is None:
        a_spec = pl.BlockSpec((tk, tm), lambda i, j, k: (k, i))
    else:
        a_spec = pl.BlockSpec((None, tk, tm), lambda i, j, k: (a_layer, k, i))
    if blocked_n:
        o_spec = pl.BlockSpec((nbs, tm, blocked_n), lambda i, j, k: (j, i, 0))
        o_shape = _sds((N // blocked_n, M, blocked_n), BF16)
    else:
        o_spec = pl.BlockSpec((tm, tn), lambda i, j, k: (i, j))
        o_shape = _sds((M, N), BF16)
    return _pcall(body, name=name, grid=(M // tm, N // tn, nk),
                  in_specs=[a_spec, pl.BlockSpec((tk, tn), lambda i, j, k: (k, j))],
                  out_specs=o_spec, out_shape=o_shape, scratch=[pltpu.VMEM((tm, tn), F32)],
                  sem=("parallel", "parallel", "arbitrary"))(a, b)


def _mm_tn_bblk(a, b3, name, per_step=2):
    T, M = a.shape
    NB, _, n = b3.shape
    tk = min(T, 1024)
    nk = T // tk

    def body(a_ref, b_ref, o_ref, acc):
        k = pl.program_id(1)

        @pl.when(k == 0)
        def _():
            acc[...] = jnp.zeros_like(acc)

        av = a_ref[...]
        for s in range(per_step):
            acc[s] += _dot_tn(av, b_ref[s])

        @pl.when(k == nk - 1)
        def _():
            o_ref[...] = acc[...].astype(BF16)

    return _pcall(body, name=name, grid=(NB // per_step, nk),
                  in_specs=[pl.BlockSpec((tk, M), lambda j, k: (k, 0)),
                            pl.BlockSpec((per_step, tk, n), lambda j, k: (j, k, 0))],
                  out_specs=pl.BlockSpec((per_step, M, n), lambda j, k: (j, 0, 0)),
                  out_shape=_sds((NB, M, n), BF16), scratch=[pltpu.VMEM((per_step, M, n), F32)],
                  sem=("parallel", "arbitrary"))(a, b3)


def _mm_tn_ablk(a3, b, name):
    NA, T, m = a3.shape
    N = b.shape[1]
    tn = _tile_of(N, 1024)
    tk = min(T, 1024)
    nk = T // tk

    def body(a_ref, b_ref, o_ref, acc):
        k = pl.program_id(2)

        @pl.when(k == 0)
        def _():
            acc[...] = jnp.zeros_like(acc)

        acc[...] += _dot_tn(a_ref[...], b_ref[...])

        @pl.when(k == nk - 1)
        def _():
            o_ref[...] = acc[...].astype(BF16)

    return _pcall(body, name=name, grid=(NA, N // tn, nk),
                  in_specs=[pl.BlockSpec((None, tk, m), lambda j, n, k: (j, k, 0)),
                            pl.BlockSpec((tk, tn), lambda j, n, k: (k, n))],
                  out_specs=pl.BlockSpec((None, m, tn), lambda j, n, k: (j, 0, n)),
                  out_shape=_sds((NA, m, N), BF16), scratch=[pltpu.VMEM((m, tn), F32)],
                  sem=("parallel", "parallel", "arbitrary"))(a3, b)


def _mm_out_ln(ya, yb, wo, l, x0, g, b, name):
    T = x0.shape[0]

    def body(ya_ref, yb_ref, w_ref, x0_ref, g_ref, b_ref, xh_ref, rs_ref, x1_ref):
        r = ALPHA * x0_ref[...] + _dot(ya_ref[...], w_ref[0:MIX, :]) + _dot(yb_ref[...], w_ref[MIX:, :])
        xhat, rstd, x1 = _ln_fwd(r, g_ref[...], b_ref[...])
        xh_ref[...] = xhat
        rs_ref[...] = rstd
        x1_ref[...] = x1.astype(BF16)

    vec = _layer((1, D_MODEL), g[1])
    return _pcall(body, name=name, grid=(T // TM,),
                  in_specs=[_rows(TM, MIX), _rows(TM, MIX), _layer((D_MODEL, D_MODEL), l), _rows(TM, D_MODEL),
                            vec, vec],
                  out_specs=[_rows(TM, D_MODEL), _rows(TM, 1), _rows(TM, D_MODEL)],
                  out_shape=[_sds((T, D_MODEL), F32), _sds((T, 1), F32), _sds((T, D_MODEL), BF16)],
                  sem=("parallel",))(ya, yb, wo, x0, g[0], b[0])


def _mm_up(x1h, wup, l, name, ride=()):
    T = x1h.shape[0]
    ni = T // TM

    def body(x_ref, w_ref, hf_ref, a_ref):
        x = x_ref[...]
        g = _dot(x, w_ref[0])
        u = _dot(x, w_ref[1])
        hf_ref[0] = g.astype(BF16)
        hf_ref[1] = u.astype(BF16)
        a_ref[...] = (g * _sigmoid(g) * u).astype(BF16)

    in_specs = [pl.BlockSpec((TM, D_MODEL), lambda j, i: (i, 0)),
                pl.BlockSpec((2, None, None, D_MODEL, FF_NB), lambda j, i: (0, j, l, 0, 0))]
    out_specs = [pl.BlockSpec((2, None, TM, FF_NB), lambda j, i: (0, j, i, 0)),
                 pl.BlockSpec((None, TM, FF_NB), lambda j, i: (j, i, 0))]
    out_shape = [_sds((2, FF_J, T, FF_NB), BF16), _sds((FF_J, T, FF_NB), BF16)]
    if not ride:
        return _pcall(body, name=name, grid=(FF_J, ni), in_specs=in_specs, out_specs=out_specs,
                      out_shape=out_shape, sem=("parallel", "parallel"))(x1h, wup), ()
    first = lambda: (pl.program_id(0) == 0) & (pl.program_id(1) == 0)
    last = lambda: (pl.program_id(0) == FF_J - 1) & (pl.program_id(1) == ni - 1)
    outs = _pcall(_riding(body, 2, 2, len(ride), first, last, gather=True), name=name, grid=(FF_J, ni),
                  in_specs=in_specs + [_ANY] * len(ride), out_specs=out_specs + [_ANY] * len(ride),
                  out_shape=out_shape + [_sds((N_DEV,) + r.shape, r.dtype) for r in ride],
                  scratch=list(_A2A_SEMS) * len(ride))(x1h, wup, *ride)
    return outs[:2], outs[2:]


def _mm_down_ln(a3, wd4, l, xhat1, g1, b1, g2, b2, name, ride=()):
    T = a3.shape[1]
    ni = T // TM

    def body(a_ref, w_ref, xh1_ref, g1_ref, b1_ref, g2_ref, b2_ref, xh_ref, rs_ref, x2_ref):
        x1 = xh1_ref[...] * g1_ref[...] + b1_ref[...]
        r = ALPHA * x1
        for j in range(FF_J):
            r = r + _dot(a_ref[j], w_ref[j])
        xhat, rstd, x2 = _ln_fwd(r, g2_ref[...], b2_ref[...])
        xh_ref[...] = xhat
        rs_ref[...] = rstd
        x2_ref[...] = x2.astype(BF16)

    vec = _layer((1, D_MODEL), g1[1])
    in_specs = [pl.BlockSpec((FF_J, TM, FF_NB), lambda i: (0, i, 0)),
                _layer((FF_J, FF_NB, D_MODEL), l), _rows(TM, D_MODEL), vec, vec, vec, vec]
    out_specs = [_rows(TM, D_MODEL), _rows(TM, 1), _rows(TM, D_MODEL)]
    out_shape = [_sds((T, D_MODEL), F32), _sds((T, 1), F32), _sds((T, D_MODEL), BF16)]
    operands = (a3, wd4, xhat1, g1[0], b1[0], g2[0], b2[0])
    if not ride:
        return _pcall(body, name=name, grid=(ni,), in_specs=in_specs, out_specs=out_specs, out_shape=out_shape,
                      sem=("parallel",))(*operands), ()
    first = lambda: pl.program_id(0) == 0
    last = lambda: pl.program_id(0) == ni - 1
    outs = _pcall(_riding(body, 7, 3, len(ride), first, last, gather=True), name=name, grid=(ni,),
                  in_specs=in_specs + [_ANY] * len(ride), out_specs=out_specs + [_ANY] * len(ride),
                  out_shape=out_shape + [_sds((N_DEV,) + r.shape, r.dtype) for r in ride],
                  scratch=list(_A2A_SEMS) * len(ride))(*operands, *ride)
    return outs[:3], outs[3:]


def _mm_ple(x2h, xhat2, g2, b2, p, wg, bg, wp, l, name, lw=None):
    lw = l if lw is None else lw
    T = x2h.shape[0]

    def body(x2h_ref, xh_ref, g2_ref, b2_ref, p_ref, wg_ref, bg_ref, wp_ref, o_ref, oh_ref):
        x2 = xh_ref[...] * g2_ref[...] + b2_ref[...]
        gate = _sigmoid(_dot(x2h_ref[...], wg_ref[...]) + bg_ref[...])
        pp = _dot(p_ref[...].astype(BF16), wp_ref[...])
        x3 = x2 + gate * pp
        o_ref[...] = x3
        oh_ref[...] = x3.astype(BF16)

    vec = _layer((1, D_MODEL), l)
    return _pcall(body, name=name, grid=(T // TM,),
                  in_specs=[_rows(TM, D_MODEL), _rows(TM, D_MODEL), vec, vec,
                            pl.BlockSpec((None, TM, D_PLE), lambda i: (l, i, 0)),
                            _layer((D_MODEL, D_MODEL), lw), vec, _layer((D_PLE, D_MODEL), l)],
                  out_specs=[_rows(TM, D_MODEL), _rows(TM, D_MODEL)],
                  out_shape=[_sds((T, D_MODEL), F32), _sds((T, D_MODEL), BF16)],
                  sem=("parallel",))(x2h, xhat2, g2, b2, p, wg, bg, wp)


def _loss_head(x3, tgt, name):
    T = x3.shape[0]

    def body(x_ref, t_ref, dx_ref, l_ref):
        e = x_ref[...] - t_ref[...]
        dx_ref[...] = e * (1.0 / D_MODEL)

        @pl.when(pl.program_id(0) == 0)
        def _():
            l_ref[...] = jnp.zeros_like(l_ref)

        l_ref[...] += (0.5 / D_MODEL) * jnp.sum(e * e).reshape(1, 1)

    return _pcall(body, name=name, grid=(T // TM,),
                  in_specs=[_rows(TM, D_MODEL), _rows(TM, D_MODEL)],
                  out_specs=[_rows(TM, D_MODEL), _full((1, 1))],
                  out_shape=[_sds((T, D_MODEL), F32), _sds((1, 1), F32)])(x3, tgt)


def _ple_bwd(dx3, x2h, p, wg, bg, wp, l, xhat2, rstd2, g2, name, lw=None):
    T = dx3.shape[0]
    lw = l if lw is None else lw

    def body(dx3_ref, x2h_ref, p_ref, wg_ref, bg_ref, wp_ref, xh_ref, rs_ref, g2_ref,
             dr_ref, drh_ref, dpre_ref, dpp_ref, dbg_ref, dg_ref, db_ref):
        dx3 = dx3_ref[...]
        gate = _sigmoid(_dot(x2h_ref[...], wg_ref[...]) + bg_ref[...])
        pp = _dot(p_ref[...].astype(BF16), wp_ref[...])
        dpre = dx3 * pp * gate * (1.0 - gate)
        dpreh = dpre.astype(BF16)
        dpre_ref[...] = dpreh
        dpp_ref[...] = (dx3 * gate).astype(BF16)
        dx2 = dx3 + _dot_nt(dpreh, wg_ref[...])
        xhat = xh_ref[...]
        dr = _ln_bwd(dx2, xhat, rs_ref[...], g2_ref[...])
        dr_ref[...] = dr
        drh_ref[...] = dr.astype(BF16)

        @pl.when(pl.program_id(0) == 0)
        def _():
            dbg_ref[...] = jnp.zeros_like(dbg_ref)
            dg_ref[...] = jnp.zeros_like(dg_ref)
            db_ref[...] = jnp.zeros_like(db_ref)

        dbg_ref[...] += _colsum(dpre)
        dg_ref[...] += _colsum(dx2 * xhat)
        db_ref[...] += _colsum(dx2)

    vec = _layer((1, D_MODEL), l)
    acc = _full((1, D_MODEL))
    big = _rows(TM, D_MODEL)
    return _pcall(body, name=name, grid=(T // TM,),
                  in_specs=[big, big, pl.BlockSpec((None, TM, D_PLE), lambda i: (l, i, 0)),
                            _layer((D_MODEL, D_MODEL), lw), vec, _layer((D_PLE, D_MODEL), l),
                            big, _rows(TM, 1), vec],
                  out_specs=[big, big, big, big, acc, acc, acc],
                  out_shape=[_sds((T, D_MODEL), F32), _sds((T, D_MODEL), BF16), _sds((T, D_MODEL), BF16),
                             _sds((T, D_MODEL), BF16), _sds((1, D_MODEL), F32), _sds((1, D_MODEL), F32),
                             _sds((1, D_MODEL), F32)])(dx3, x2h, p, wg, bg, wp, xhat2, rstd2, g2)


def _ffn_bwd1(dr2h, wd4, l, hf, name, ride=()):
    T = dr2h.shape[0]
    ni = T // TM

    def body(d_ref, w_ref, hf_ref, o_ref):
        da = _dot_nt(d_ref[...], w_ref[...])
        g = hf_ref[0].astype(F32)
        u = hf_ref[1].astype(F32)
        sg = _sigmoid(g)
        o_ref[0] = (da * u * (sg * (1.0 + g * (1.0 - sg)))).astype(BF16)
        o_ref[1] = (da * (g * sg)).astype(BF16)

    blk = pl.BlockSpec((2, None, TM, FF_NB), lambda j, i: (0, j, i, 0))
    in_specs = [pl.BlockSpec((TM, D_MODEL), lambda j, i: (i, 0)),
                pl.BlockSpec((None, None, FF_NB, D_MODEL), lambda j, i: (l, j, 0, 0)), blk]
    out_shape = _sds((2, FF_J, T, FF_NB), BF16)
    if not ride:
        return _pcall(body, name=name, grid=(FF_J, ni), in_specs=in_specs, out_specs=blk, out_shape=out_shape,
                      sem=("parallel", "parallel"))(dr2h, wd4, hf), ()
    first = lambda: (pl.program_id(0) == 0) & (pl.program_id(1) == 0)
    last = lambda: (pl.program_id(0) == FF_J - 1) & (pl.program_id(1) == ni - 1)
    outs = _pcall(_riding(body, 3, 1, len(ride), first, last), name=name, grid=(FF_J, ni),
                  in_specs=in_specs + [_ANY] * len(ride), out_specs=[blk] + [_ANY] * len(ride),
                  out_shape=[out_shape] + [_sds(r.shape, r.dtype) for r in ride],
                  scratch=list(_A2A_SEMS) * len(ride))(dr2h, wd4, hf, *ride)
    return outs[0], outs[1:]


_TM_B2 = 256


def _ffn_bwd2(dhf, wup, l, dr2, xhat1, rstd1, g1, name, ride=()):
    T = dr2.shape[0]
    tm = min(T, _TM_B2)
    ni = T // tm

    def body(dh_ref, w_ref, dr2_ref, xh_ref, rs_ref, g_ref, dr_ref, drh_ref, dg_ref, db_ref):
        dx1 = ALPHA * dr2_ref[...]
        for s in range(2):
            for j in range(FF_J):
                dx1 = dx1 + _dot_nt(dh_ref[s, j], w_ref[s, j])
        xhat = xh_ref[...]
        dr = _ln_bwd(dx1, xhat, rs_ref[...], g_ref[...])
        dr_ref[...] = dr
        drh_ref[...] = dr.astype(BF16)

        @pl.when(pl.program_id(0) == 0)
        def _():
            dg_ref[...] = jnp.zeros_like(dg_ref)
            db_ref[...] = jnp.zeros_like(db_ref)

        dg_ref[...] += _colsum(dx1 * xhat)
        db_ref[...] += _colsum(dx1)

    big = _rows(tm, D_MODEL)
    acc = _full((1, D_MODEL))
    in_specs = [pl.BlockSpec((2, FF_J, tm, FF_NB), lambda i: (0, 0, i, 0)),
                pl.BlockSpec((2, FF_J, None, D_MODEL, FF_NB), lambda i: (0, 0, l, 0, 0)),
                big, big, _rows(tm, 1), _layer((1, D_MODEL), g1[1])]
    out_specs = [big, big, acc, acc]
    out_shape = [_sds((T, D_MODEL), F32), _sds((T, D_MODEL), BF16), _sds((1, D_MODEL), F32),
                 _sds((1, D_MODEL), F32)]
    operands = (dhf, wup, dr2, xhat1, rstd1, g1[0])
    if not ride:
        return _pcall(body, name=name, grid=(ni,), in_specs=in_specs, out_specs=out_specs,
                      out_shape=out_shape)(*operands), ()
    first = lambda: pl.program_id(0) == 0
    last = lambda: pl.program_id(0) == ni - 1
    outs = _pcall(_riding(body, 6, 4, len(ride), first, last), name=name, grid=(ni,),
                  in_specs=in_specs + [_ANY] * len(ride), out_specs=out_specs + [_ANY] * len(ride),
                  out_shape=out_shape + [_sds(r.shape, r.dtype) for r in ride],
                  scratch=list(_A2A_SEMS) * len(ride))(*operands, *ride)
    return outs[:4], outs[4:]


def _out_bwd(dr1h, wo, l, name):
    T = dr1h.shape[0]

    def body(d_ref, w_ref, da_ref, db_ref):
        d = d_ref[...]
        da_ref[...] = _dot_nt(d, w_ref[0:MIX, :])
        db_ref[...] = _dot_nt(d, w_ref[MIX:, :])

    return _pcall(body, name=name, grid=(T // TM,),
                  in_specs=[_rows(TM, D_MODEL), _layer((D_MODEL, D_MODEL), l)],
                  out_specs=[_rows(TM, MIX), _rows(TM, MIX)],
                  out_shape=[_sds((T, MIX), F32), _sds((T, MIX), F32)],
                  sem=("parallel",))(dr1h, wo)


def _in_bwd(dh, win8, l, dr1, name):
    T = dr1.shape[0]

    def body(dh_ref, w_ref, dr_ref, o_ref):
        acc = ALPHA * dr_ref[...]
        for b in range(N_DEV):
            acc = acc + _dot_nt(dh_ref[:, b * IN_NB:(b + 1) * IN_NB], w_ref[b])
        o_ref[...] = acc

    return _pcall(body, name=name, grid=(T // TM,),
                  in_specs=[_rows(TM, 4 * MIX), pl.BlockSpec((N_DEV, None, D_MODEL, IN_NB), lambda i: (0, l, 0, 0)),
                            _rows(TM, D_MODEL)],
                  out_specs=_rows(TM, D_MODEL), out_shape=_sds((T, D_MODEL), F32),
                  sem=("parallel",))(dh, win8, dr1)


def _s5_operators(lre, lim, log_dt, bre, bim, cre, cim, dskip):
    G, P, H, LC = S5_GROUPS, S5_STATE, S5_GROUP, S5_LC
    dt = jnp.exp(log_dt)[:, None]
    mag = jnp.exp(lre * dt)
    ang = lim * dt
    lb_re = mag * jnp.cos(ang)
    lb_im = mag * jnp.sin(ang)
    den = lre * lre + lim * lim
    nr = lb_re - 1.0
    ni = lb_im
    r_re = (nr * lre + ni * lim) / den
    r_im = (ni * lre - nr * lim) / den
    bb_re = r_re[..., None] * bre - r_im[..., None] * bim
    bb_im = r_re[..., None] * bim + r_im[..., None] * bre
    k = jnp.arange(LC + 1, dtype=F32)[:, None, None]
    pmag = jnp.exp(k * (lre * dt)[None])
    pang = k * ang[None]
    pw_re = pmag * jnp.cos(pang)
    pw_im = pmag * jnp.sin(pang)
    cp_re = cre[None] * pw_re[:, :, None, :] - cim[None] * pw_im[:, :, None, :]
    cp_im = cre[None] * pw_im[:, :, None, :] + cim[None] * pw_re[:, :, None, :]
    kk = (jnp.einsum('kghp,gpj->kghj', cp_re[:LC], bb_re, precision=HI)
          - jnp.einsum('kghp,gpj->kghj', cp_im[:LC], bb_im, precision=HI))
    dmat = dskip.reshape(G, H)[:, :, None] * jnp.eye(H, dtype=F32)[None]
    kk = jnp.concatenate([kk[:1] + dmat[None], kk[1:]], axis=0)
    r0 = jnp.transpose(kk, (1, 3, 0, 2)).reshape(G, H, LC * H)
    qt = jnp.stack([cp_re[1:], -cp_im[1:]], axis=0)
    qt = jnp.transpose(qt, (2, 0, 4, 1, 3)).reshape(G, 2 * P, LC * H)
    pb_re = pw_re[:LC, :, :, None] * bb_re[None] - pw_im[:LC, :, :, None] * bb_im[None]
    pb_im = pw_re[:LC, :, :, None] * bb_im[None] + pw_im[:LC, :, :, None] * bb_re[None]
    pm = jnp.stack([pb_re[::-1], pb_im[::-1]], axis=0)
    pm = jnp.transpose(pm, (2, 1, 4, 0, 3)).reshape(G, LC * H, 2 * P)
    a_re = pw_re[LC]
    a_im = pw_im[LC]
    a1 = jnp.concatenate([a_re, a_re], axis=-1)
    a2 = jnp.concatenate([-a_im, a_im], axis=-1)
    return r0, qt, pm, a1, a2


def _to_chunks(u):
    T = u.shape[0]
    return jnp.transpose(u.reshape(T // S5_LC, S5_LC, S5_GROUPS, S5_GROUP), (2, 0, 1, 3)).reshape(
        S5_GROUPS, T // S5_LC, S5_LW)


def _from_chunks(m):
    nc = m.shape[1]
    return jnp.transpose(m.reshape(S5_GROUPS, nc, S5_LC, S5_GROUP), (1, 2, 0, 3)).reshape(nc * S5_LC, MIX)


def _gspec(r, c):
    return pl.BlockSpec((None, r, c), lambda g: (g, 0, 0))


def _s5_chunk_state(umat, pm, name):
    G, nc, _ = umat.shape

    def body(u_ref, p_ref, o_ref):
        o_ref[...] = _mm3(_dot, _split(u_ref[...]), _split(p_ref[...]))

    return _pcall(body, name=name, grid=(G,), in_specs=[_gspec(nc, S5_LW), _gspec(S5_LW, 128)],
                  out_specs=_gspec(nc, 128), out_shape=_sds((G, nc, 128), F32), sem=("parallel",))(umat, pm)


_SCAN_G = 8
_SCAN_UNROLL = 8


def _s5_scan_fwd(s_t, a1, a2, name):
    nc, G, _ = s_t.shape

    def body(s_ref, a1_ref, a2_ref, o_ref, sb_ref):
        sb_ref[...] = pltpu.roll(s_ref[...], 64, 2)
        a1v = a1_ref[...]
        a2v = a2_ref[...]

        def step(c, carry):
            x, xb = carry
            o_ref[c] = x
            return a1v * x + a2v * xb + s_ref[c], a1v * xb - a2v * x + sb_ref[c]

        zero = jnp.zeros((_SCAN_G, 128), F32)
        lax.fori_loop(0, nc, step, (zero, zero), unroll=_SCAN_UNROLL)

    blk = pl.BlockSpec((nc, _SCAN_G, 128), lambda g: (0, g, 0))
    vec = pl.BlockSpec((_SCAN_G, 128), lambda g: (g, 0))
    return _pcall(body, name=name, grid=(G // _SCAN_G,), in_specs=[blk, vec, vec], out_specs=blk,
                  out_shape=_sds((nc, G, 128), F32), scratch=[pltpu.VMEM((nc, _SCAN_G, 128), F32)],
                  sem=("parallel",))(s_t, a1, a2)


def _s5_scan_bwd(dxp_t, xp_t, a1, a2, name):
    nc, G, _ = dxp_t.shape

    def body(dx_ref, x_ref, a1_ref, a2_ref, ds_ref, da1_ref, da2_ref, dxb_ref, xb_ref):
        dxb_ref[...] = pltpu.roll(dx_ref[...], 64, 2)
        xb_ref[...] = pltpu.roll(x_ref[...], 64, 2)
        a1v = a1_ref[...]
        a2v = a2_ref[...]
        zero = jnp.zeros((_SCAN_G, 128), F32)

        def step(n, carry):
            gc, gb, d1, d2 = carry
            c = nc - 1 - n
            ds_ref[c] = gc
            d1 = d1 + gc * x_ref[c]
            d2 = d2 + gc * xb_ref[c]
            return dx_ref[c] + a1v * gc - a2v * gb, dxb_ref[c] + a1v * gb + a2v * gc, d1, d2

        _, _, d1, d2 = lax.fori_loop(0, nc, step, (zero, zero, zero, zero), unroll=_SCAN_UNROLL)
        da1_ref[...] = d1
        da2_ref[...] = d2

    blk = pl.BlockSpec((nc, _SCAN_G, 128), lambda g: (0, g, 0))
    vec = pl.BlockSpec((_SCAN_G, 128), lambda g: (g, 0))
    return _pcall(body, name=name, grid=(G // _SCAN_G,), in_specs=[blk, blk, vec, vec],
                  out_specs=[blk, vec, vec],
                  out_shape=[_sds((nc, G, 128), F32), _sds((G, 128), F32), _sds((G, 128), F32)],
                  scratch=[pltpu.VMEM((nc, _SCAN_G, 128), F32)] * 2, sem=("parallel",))(dxp_t, xp_t, a1, a2)


def _toeplitz_rows(r0, mt):
    lane = lax.broadcasted_iota(jnp.int32, (S5_GROUP, S5_LW), 1)
    mt[0:S5_GROUP, :] = r0
    for s in range(1, S5_LC):
        mt[s * S5_GROUP:(s + 1) * S5_GROUP, :] = jnp.where(lane >= s * S5_GROUP, pltpu.roll(r0, s * S5_GROUP, 1), 0.0)


def _toeplitz_rows_t(dmt):
    lane = lax.broadcasted_iota(jnp.int32, (S5_GROUP, S5_LW), 1)
    acc = dmt[0:S5_GROUP, :]
    for s in range(1, S5_LC):
        blk = dmt[s * S5_GROUP:(s + 1) * S5_GROUP, :]
        acc = acc + jnp.where(lane < S5_LW - s * S5_GROUP, pltpu.roll(blk, S5_LW - s * S5_GROUP, 1), 0.0)
    return acc


def _s5_output(umat, xprev, r0, qt, name):
    G, nc, _ = umat.shape

    def body(u_ref, x_ref, r_ref, q_ref, o_ref, mt):
        _toeplitz_rows(r_ref[...], mt)
        o_ref[...] = (_mm3(_dot, _split(u_ref[...]), _split(mt[...]))
                      + _mm3(_dot, _split(x_ref[...]), _split(q_ref[...])))

    return _pcall(body, name=name, grid=(G,),
                  in_specs=[_gspec(nc, S5_LW), _gspec(nc, 128), _gspec(S5_GROUP, S5_LW), _gspec(128, S5_LW)],
                  out_specs=_gspec(nc, S5_LW), out_shape=_sds((G, nc, S5_LW), F32),
                  scratch=[pltpu.VMEM((S5_LW, S5_LW), F32)], sem=("parallel",))(umat, xprev, r0, qt)


def _s5_bwd_state(dymat, qt, name):
    G, nc, _ = dymat.shape

    def body(d_ref, q_ref, o_ref):
        o_ref[...] = _mm3(_dot_nt, _split(d_ref[...]), _split(q_ref[...]))

    return _pcall(body, name=name, grid=(G,), in_specs=[_gspec(nc, S5_LW), _gspec(128, S5_LW)],
                  out_specs=_gspec(nc, 128), out_shape=_sds((G, nc, 128), F32), sem=("parallel",))(dymat, qt)


def _s5_bwd_main(umat, xprev, dymat, ds, r0, pm, name):
    G, nc, _ = umat.shape

    def body(u_ref, x_ref, dy_ref, ds_ref, r_ref, p_ref, du_ref, dr_ref, dq_ref, dp_ref, mt):
        u = _split(u_ref[...])
        dy = _split(dy_ref[...])
        dsv = _split(ds_ref[...])
        _toeplitz_rows(r_ref[...], mt)
        du_ref[...] = (_mm3(_dot_nt, dy, _split(mt[...])) + _mm3(_dot_nt, dsv, _split(p_ref[...]))).astype(BF16)
        dq_ref[...] = _mm3(_dot_tn, _split(x_ref[...]), dy)
        dp_ref[...] = _mm3(_dot_tn, u, dsv)
        mt[...] = _mm3(_dot_tn, u, dy)
        dr_ref[...] = _toeplitz_rows_t(mt)

    return _pcall(body, name=name, grid=(G,),
                  in_specs=[_gspec(nc, S5_LW), _gspec(nc, 128), _gspec(nc, S5_LW), _gspec(nc, 128),
                            _gspec(S5_GROUP, S5_LW), _gspec(S5_LW, 128)],
                  out_specs=[_gspec(nc, S5_LW), _gspec(S5_GROUP, S5_LW), _gspec(128, S5_LW), _gspec(S5_LW, 128)],
                  out_shape=[_sds((G, nc, S5_LW), BF16), _sds((G, S5_GROUP, S5_LW), F32), _sds((G, 128, S5_LW), F32),
                             _sds((G, S5_LW, 128), F32)],
                  scratch=[pltpu.VMEM((S5_LW, S5_LW), F32)], sem=("parallel",))(umat, xprev, dymat, ds, r0, pm)


def _glu_fwd(y, wglu, bglu, l, name):
    T = y.shape[0]

    def body(y_ref, w_ref, b_ref, o_ref, g_ref):
        g = _gelu(y_ref[...])
        gh = g.astype(BF16)
        z = _dot(gh, w_ref[...]) + b_ref[...]
        o_ref[...] = (g * _sigmoid(z)).astype(BF16)
        g_ref[...] = gh

    return _pcall(body, name=name, grid=(T // TM,),
                  in_specs=[_rows(TM, MIX), _layer((MIX, MIX), l), _layer((1, MIX), l)],
                  out_specs=[_rows(TM, MIX), _rows(TM, MIX)],
                  out_shape=[_sds((T, MIX), BF16), _sds((T, MIX), BF16)], sem=("parallel",))(y, wglu, bglu)


def _glu_bwd(y, dout, wglu, bglu, l, name):
    T = y.shape[0]

    def body(y_ref, do_ref, w_ref, b_ref, dy_ref, dz_ref, db_ref):
        yv = y_ref[...]
        do = do_ref[...]
        g = _gelu(yv)
        s = _sigmoid(_dot(g.astype(BF16), w_ref[...]) + b_ref[...])
        dz = do * g * s * (1.0 - s)
        dzh = dz.astype(BF16)
        dz_ref[...] = dzh
        dg = do * s + _dot_nt(dzh, w_ref[...])
        dy_ref[...] = dg * _gelu_grad(yv)

        @pl.when(pl.program_id(0) == 0)
        def _():
            db_ref[...] = jnp.zeros_like(db_ref)

        db_ref[...] += _colsum(dz)

    return _pcall(body, name=name, grid=(T // TM,),
                  in_specs=[_rows(TM, MIX), _rows(TM, MIX), _layer((MIX, MIX), l), _layer((1, MIX), l)],
                  out_specs=[_rows(TM, MIX), _rows(TM, MIX), _full((1, MIX))],
                  out_shape=[_sds((T, MIX), F32), _sds((T, MIX), BF16), _sds((1, MIX), F32)])(y, dout, wglu, bglu)


def _prev_rows(T, h, s=0):
    return pl.BlockSpec((h, MIX), lambda i: (jnp.maximum(i * (TM // h) - 1, 0), s))


def _next_rows(T, h, s=0):
    return pl.BlockSpec((h, MIX), lambda i: (jnp.minimum((i + 1) * (TM // h), T // h - 1), s))


def _conv_fwd(h, w, l, name):
    T = h.shape[0]

    def body(b_ref, c_ref, x_ref, ch_ref, xh_ref, w_ref, o_ref, ext):
        i = pl.program_id(0)
        z = c_ref[...] * x_ref[...]
        ext[0:8, :] = jnp.where(i > 0, ch_ref[...] * xh_ref[...], 0.0)
        ext[8:, :] = z
        y = (w_ref[0:1, :] * ext[pl.ds(6, TM), :] + w_ref[1:2, :] * ext[pl.ds(7, TM), :] + w_ref[2:3, :] * z)
        o_ref[...] = (b_ref[...] * y).astype(BF16)

    return _pcall(body, name=name, grid=(T // TM,),
                  in_specs=[_cols(TM, 1), _cols(TM, 2), _cols(TM, 3), _prev_rows(T, 8, 2), _prev_rows(T, 8, 3),
                            _layer((3, MIX), l)],
                  out_specs=_rows(TM, MIX), out_shape=_sds((T, MIX), BF16),
                  scratch=[pltpu.VMEM((TM + 8, MIX), F32)], sem=("parallel",))(h, h, h, h, h, w)


def _conv_bwd(dout, h, dua, w, l, name):
    T = h.shape[0]
    nb = T // TM

    def body(do_ref, b_ref, c_ref, x_ref, ch_ref, xh_ref, don_ref, bn_ref, du_ref, w_ref,
             dh_ref, dw_ref, ext, ext2):
        i = pl.program_id(0)
        c = c_ref[...]
        x = x_ref[...]
        z = c * x
        ext[0:8, :] = jnp.where(i > 0, ch_ref[...] * xh_ref[...], 0.0)
        ext[8:, :] = z
        zm2 = ext[pl.ds(6, TM), :]
        zm1 = ext[pl.ds(7, TM), :]
        w0 = w_ref[0:1, :]
        w1 = w_ref[1:2, :]
        w2 = w_ref[2:3, :]
        y = w0 * zm2 + w1 * zm1 + w2 * z
        do = do_ref[...]
        dy = do * b_ref[...]
        ext2[0:TM, :] = dy
        ext2[TM:, :] = jnp.where(i < nb - 1, don_ref[...] * bn_ref[...], 0.0)
        dz = w2 * dy + w1 * ext2[pl.ds(1, TM), :] + w0 * ext2[pl.ds(2, TM), :]
        dh_ref[:, 0:MIX] = du_ref[...]
        dh_ref[:, MIX:2 * MIX] = (do * y).astype(BF16)
        dh_ref[:, 2 * MIX:3 * MIX] = (dz * x).astype(BF16)
        dh_ref[:, 3 * MIX:] = (dz * c).astype(BF16)

        @pl.when(i == 0)
        def _():
            dw_ref[...] = jnp.zeros_like(dw_ref)

        dw_ref[0:1, :] += _colsum(dy * zm2)
        dw_ref[1:2, :] += _colsum(dy * zm1)
        dw_ref[2:3, :] += _colsum(dy * z)

    return _pcall(body, name=name, grid=(nb,),
                  in_specs=[_rows(TM, MIX), _cols(TM, 1), _cols(TM, 2), _cols(TM, 3), _prev_rows(T, 8, 2),
                            _prev_rows(T, 8, 3), _next_rows(T, 8), _next_rows(T, 8, 1), _rows(TM, MIX),
                            _layer((3, MIX), l)],
                  out_specs=[_rows(TM, 4 * MIX), _full((8, MIX))],
                  out_shape=[_sds((T, 4 * MIX), BF16), _sds((8, MIX), F32)],
                  scratch=[pltpu.VMEM((TM + 8, MIX), F32), pltpu.VMEM((TM + 8, MIX), F32)])(
                      dout, h, h, h, h, h, dout, h, dua, w)


_PH = 16


def _pooled(ext, t, gi, w):
    lo = gi * POOL_GROUP
    cur = ext[pl.ds(_PH, TM), lo:lo + POOL_GROUP]
    acc = cur
    for k in range(1, w):
        acc = acc + ext[pl.ds(_PH - k, TM), lo:lo + POOL_GROUP]
    cnt = jnp.minimum(t + 1, w).astype(F32)
    return acc / cnt - cur, cnt


def _pool_fwd(h, pw, scale, l, name):
    T = h.shape[0]

    def body(z_ref, zh_ref, pw_ref, sc_ref, o_ref, ext):
        i = pl.program_id(0)
        ext[0:_PH, :] = jnp.where(i > 0, zh_ref[...], 0.0)
        ext[_PH:, :] = z_ref[...]
        t = i * TM + lax.broadcasted_iota(jnp.int32, (TM, 1), 0)
        for gi, w in enumerate(POOL_WINDOWS):
            lo = gi * POOL_GROUP
            pooled, _ = _pooled(ext, t, gi, w)
            mixed = _dot(pooled.astype(BF16), pw_ref[gi].astype(BF16))
            o_ref[:, lo:lo + POOL_GROUP] = (mixed * sc_ref[:, lo:lo + POOL_GROUP]).astype(BF16)

    return _pcall(body, name=name, grid=(T // TM,),
                  in_specs=[_cols(TM, 3), _prev_rows(T, _PH, 3), _layer((4, POOL_GROUP, POOL_GROUP), l),
                            _layer((1, MIX), l)],
                  out_specs=_rows(TM, MIX), out_shape=_sds((T, MIX), BF16),
                  scratch=[pltpu.VMEM((TM + _PH, MIX), F32)], sem=("parallel",))(h, h, pw, scale)


def _pool_bwd(dout, h, dq, dk, dv, pw, scale, l, name):
    T = h.shape[0]
    nb = T // TM

    def body(do_ref, don_ref, z_ref, zh_ref, dq_ref, dk_ref, dv_ref, pw_ref, sc_ref,
             dh_ref, dpw_ref, dsc_ref, ext, ext2):
        i = pl.program_id(0)
        ext[0:_PH, :] = jnp.where(i > 0, zh_ref[...], 0.0)
        ext[_PH:, :] = z_ref[...]
        t = i * TM + lax.broadcasted_iota(jnp.int32, (TM, 1), 0)
        dh_ref[:, 0:MIX] = dq_ref[...]
        dh_ref[:, MIX:2 * MIX] = dk_ref[...]
        dh_ref[:, 2 * MIX:3 * MIX] = dv_ref[...]

        @pl.when(i == 0)
        def _():
            dpw_ref[...] = jnp.zeros_like(dpw_ref)
            dsc_ref[...] = jnp.zeros_like(dsc_ref)

        for gi, w in enumerate(POOL_WINDOWS):
            lo = gi * POOL_GROUP
            pwb = pw_ref[gi].astype(BF16)
            sc = sc_ref[:, lo:lo + POOL_GROUP]
            pooled, cnt = _pooled(ext, t, gi, w)
            pb = pooled.astype(BF16)
            mixed = _dot(pb, pwb)
            dog = do_ref[:, lo:lo + POOL_GROUP]
            dsc_ref[:, lo:lo + POOL_GROUP] += _colsum(dog * mixed)
            dmix = (dog * sc).astype(BF16)
            dpw_ref[gi] += _dot_tn(pb, dmix)
            dpool = _dot_nt(dmix, pwb)
            dmix_n = (jnp.where(i < nb - 1, don_ref[:, lo:lo + POOL_GROUP], 0.0) * sc).astype(BF16)
            dpool_n = _dot_nt(dmix_n, pwb)
            e = dpool / cnt
            ext2[0:TM, lo:lo + POOL_GROUP] = e
            ext2[TM:, lo:lo + POOL_GROUP] = dpool_n * (1.0 / w)
            s = e
            for k in range(1, w):
                s = s + ext2[pl.ds(k, TM), lo:lo + POOL_GROUP]
            dh_ref[:, 3 * MIX + lo:3 * MIX + lo + POOL_GROUP] = (s - dpool).astype(BF16)

    blk = _rows(TM, MIX)
    return _pcall(body, name=name, grid=(nb,),
                  in_specs=[blk, _next_rows(T, _PH), _cols(TM, 3), _prev_rows(T, _PH, 3), blk, blk, blk,
                            _layer((4, POOL_GROUP, POOL_GROUP), l), _layer((1, MIX), l)],
                  out_specs=[_rows(TM, 4 * MIX), _full((4, POOL_GROUP, POOL_GROUP)), _full((1, MIX))],
                  out_shape=[_sds((T, 4 * MIX), BF16), _sds((4, POOL_GROUP, POOL_GROUP), F32), _sds((1, MIX), F32)],
                  scratch=[pltpu.VMEM((TM + _PH, MIX), F32), pltpu.VMEM((TM + _PH, MIX), F32)])(
                      dout, dout, h, h, dq, dk, dv, pw, scale)


def _att_bias_table(rel_bias):
    span = LEFT_CHUNKS * CHUNK
    assert ATT_TQ - 1 <= MAX_REL
    lo = MAX_REL - (ATT_TQ - 1)
    near = rel_bias[:, lo:2 * MAX_REL + 1]
    far = jnp.broadcast_to(rel_bias[:, 2 * MAX_REL:], (HEADS, span + ATT_TQ - 1 - MAX_REL))
    by_dist = jnp.concatenate([near, far], axis=1)
    rev = by_dist[:, ::-1]
    lv = rev.shape[1]
    skew = jnp.tile(rev, (1, ATT_TQ + 1))[:, :ATT_TQ * (lv + 1)].reshape(HEADS, ATT_TQ, lv + 1)[:, :, :ATT_W]
    bias = skew[:, ::-1, :]
    r = jnp.arange(ATT_TQ)[:, None]
    col = jnp.arange(ATT_W)[None, :]
    dchunk = (LEFT_CHUNKS + r // CHUNK) - col // CHUNK
    visible = (dchunk >= 0) & (dchunk <= LEFT_CHUNKS)
    return jnp.where(visible[None], bias, NEG_INF)


def _qrows(n, s=0):
    return pl.BlockSpec((ATT_TQ, n), lambda i: (i, s))


def _att_views_back(d, s):
    return pl.BlockSpec((ATT_TQ, MIX), lambda i: (jnp.maximum(i - (ATT_NV - 1) + d, 0), s))


_PAIR = 2 * HEAD_DIM


def _att_pair(refs, pp):
    sl = slice(pp * _PAIR, (pp + 1) * _PAIR)
    if isinstance(refs, (tuple, list)):
        return jnp.concatenate([r[:, sl] for r in refs], axis=0)
    return refs[:, sl]


def _att_fwd(h, table, name):
    T = h.shape[0]

    def body(q_ref, *refs):
        k_refs = refs[:ATT_NV]
        v_refs = refs[ATT_NV:2 * ATT_NV]
        tb_ref = refs[2 * ATT_NV]
        o_ref, oh_ref, lse_ref = refs[2 * ATT_NV + 1:]
        i = pl.program_id(0)
        col = lax.broadcasted_iota(jnp.int32, (1, ATT_W), 1)
        kvalid = (col + (i - (ATT_NV - 1)) * ATT_TQ) >= 0
        first = lax.broadcasted_iota(jnp.int32, (1, _PAIR), 1) < HEAD_DIM
        lses = []
        for pp in range(HEADS // 2):
            qp = _att_pair(q_ref, pp) * (HEAD_DIM ** -0.5)
            kp = _att_pair(k_refs, pp).astype(BF16)
            vp = _att_pair(v_refs, pp).astype(BF16)
            outs = []
            for e in range(2):
                half = first if e == 0 else jnp.logical_not(first)
                s = _dot_nt(jnp.where(half, qp, 0.0).astype(BF16), kp) + tb_ref[2 * pp + e]
                s = jnp.where(kvalid, s, NEG_INF)
                m = jnp.max(s, axis=-1, keepdims=True)
                p = jnp.exp(s - m)
                l = jnp.sum(p, axis=-1, keepdims=True)
                outs.append(_dot(p.astype(BF16), vp) / l)
                lses.append(m + jnp.log(l))
            o = jnp.where(first, outs[0], outs[1])
            o_ref[:, pp * _PAIR:(pp + 1) * _PAIR] = o
            oh_ref[:, pp * _PAIR:(pp + 1) * _PAIR] = o.astype(BF16)
        lse_ref[...] = jnp.concatenate(lses, axis=1)

    kviews = [_att_views_back(d, 1) for d in range(ATT_NV)]
    vviews = [_att_views_back(d, 2) for d in range(ATT_NV)]
    return _pcall(body, name=name, grid=(T // ATT_TQ,),
                  in_specs=[_qrows(MIX)] + kviews + vviews + [_full((HEADS, ATT_TQ, ATT_W))],
                  out_specs=[_qrows(MIX), _qrows(MIX), _qrows(HEADS)],
                  out_shape=[_sds((T, MIX), F32), _sds((T, MIX), BF16), _sds((T, HEADS), F32)],
                  sem=("parallel",))(h, *([h] * (2 * ATT_NV)), table)


def _att_bwd_q(h, do, o, lse, table, name):
    T = h.shape[0]

    def body(q_ref, *refs):
        k_refs = refs[:ATT_NV]
        v_refs = refs[ATT_NV:2 * ATT_NV]
        do_ref, o_ref, lse_ref, tb_ref, dq_ref, dl_ref, dtb_ref = refs[2 * ATT_NV:]
        i = pl.program_id(0)
        col = lax.broadcasted_iota(jnp.int32, (1, ATT_W), 1)
        kvalid = (col + (i - (ATT_NV - 1)) * ATT_TQ) >= 0

        first = lax.broadcasted_iota(jnp.int32, (1, _PAIR), 1) < HEAD_DIM

        @pl.when(i == 0)
        def _():
            dtb_ref[...] = jnp.zeros_like(dtb_ref)

        deltas = []
        for pp in range(HEADS // 2):
            qp = _att_pair(q_ref, pp) * (HEAD_DIM ** -0.5)
            kp = _att_pair(k_refs, pp).astype(BF16)
            vp = _att_pair(v_refs, pp).astype(BF16)
            dop = _att_pair(do_ref, pp)
            doo = dop * _att_pair(o_ref, pp)
            outs = []
            for e in range(2):
                hd = 2 * pp + e
                half = first if e == 0 else jnp.logical_not(first)
                s = _dot_nt(jnp.where(half, qp, 0.0).astype(BF16), kp) + tb_ref[hd]
                s = jnp.where(kvalid, s, NEG_INF)
                p = jnp.exp(s - lse_ref[:, hd:hd + 1])
                delta = jnp.sum(jnp.where(half, doo, 0.0), axis=-1, keepdims=True)
                dp = _dot_nt(jnp.where(half, dop, 0.0).astype(BF16), vp)
                ds = p * (dp - delta)
                dtb_ref[hd] += ds
                outs.append(_dot(ds.astype(BF16), kp))
                deltas.append(delta)
            dq = jnp.where(first, outs[0], outs[1]) * (HEAD_DIM ** -0.5)
            dq_ref[:, pp * _PAIR:(pp + 1) * _PAIR] = dq.astype(BF16)
        dl_ref[...] = jnp.concatenate(deltas, axis=1)

    kviews = [_att_views_back(d, 1) for d in range(ATT_NV)]
    vviews = [_att_views_back(d, 2) for d in range(ATT_NV)]
    tb = _full((HEADS, ATT_TQ, ATT_W))
    return _pcall(body, name=name, grid=(T // ATT_TQ,),
                  in_specs=[_qrows(MIX)] + kviews + vviews + [_qrows(MIX), _qrows(MIX), _qrows(HEADS), tb],
                  out_specs=[_qrows(MIX), _qrows(HEADS), tb],
                  out_shape=[_sds((T, MIX), BF16), _sds((T, HEADS), F32), _sds((HEADS, ATT_TQ, ATT_W), F32)])(
                      h, *([h] * (2 * ATT_NV)), do, o, lse, table)


def _att_table_by_key(table):
    t = table.reshape(HEADS, ATT_TQ, ATT_NV, ATT_TQ)[:, :, ::-1, :]
    return jnp.transpose(t, (0, 3, 2, 1)).reshape(HEADS, ATT_TQ, ATT_W)


def _att_bwd_kv(h, do, lse_t, delta_t, table_k, name):
    T = h.shape[0]
    nb = T // ATT_TQ

    def fwd_view(d, s=0):
        return pl.BlockSpec((ATT_TQ, MIX), lambda j: (jnp.minimum(j + d, nb - 1), s))

    def row_view(d):
        return pl.BlockSpec((HEADS, ATT_TQ), lambda j: (0, jnp.minimum(j + d, nb - 1)))

    def body(k_ref, v_ref, *refs):
        q_refs = refs[:ATT_NV]
        do_refs = refs[ATT_NV:2 * ATT_NV]
        lse_refs = refs[2 * ATT_NV:3 * ATT_NV]
        dl_refs = refs[3 * ATT_NV:4 * ATT_NV]
        tb_ref, dk_ref, dv_ref = refs[4 * ATT_NV:]
        j = pl.program_id(0)
        view = lax.broadcasted_iota(jnp.int32, (1, ATT_W), 1) // ATT_TQ
        valid = (j + view) <= nb - 1
        first = lax.broadcasted_iota(jnp.int32, (1, _PAIR), 1) < HEAD_DIM
        for pp in range(HEADS // 2):
            kp = _att_pair(k_ref, pp)
            vp = _att_pair(v_ref, pp)
            qs = (_att_pair(q_refs, pp) * (HEAD_DIM ** -0.5)).astype(BF16)
            dos = _att_pair(do_refs, pp).astype(BF16)
            dks, dvs = [], []
            for e in range(2):
                hd = 2 * pp + e
                half = first if e == 0 else jnp.logical_not(first)
                lses = jnp.concatenate([r[hd:hd + 1, :] for r in lse_refs], axis=1)
                dls = jnp.concatenate([r[hd:hd + 1, :] for r in dl_refs], axis=1)
                st = _dot_nt(jnp.where(half, kp, 0.0).astype(BF16), qs) + tb_ref[hd]
                pt = jnp.where(valid, jnp.exp(st - lses), 0.0)
                dvs.append(_dot(pt.astype(BF16), dos))
                dst = pt * (_dot_nt(jnp.where(half, vp, 0.0).astype(BF16), dos) - dls)
                dks.append(_dot(dst.astype(BF16), qs))
            dk_ref[:, pp * _PAIR:(pp + 1) * _PAIR] = jnp.where(first, dks[0], dks[1]).astype(BF16)
            dv_ref[:, pp * _PAIR:(pp + 1) * _PAIR] = jnp.where(first, dvs[0], dvs[1]).astype(BF16)

    qv = [fwd_view(d, 0) for d in range(ATT_NV)]
    dov = [fwd_view(d) for d in range(ATT_NV)]
    rows = [row_view(d) for d in range(ATT_NV)]
    return _pcall(body, name=name, grid=(nb,),
                  in_specs=[_qrows(MIX, 1), _qrows(MIX, 2)] + qv + dov + rows + rows
                  + [_full((HEADS, ATT_TQ, ATT_W))],
                  out_specs=[_qrows(MIX), _qrows(MIX)], out_shape=[_sds((T, MIX), BF16), _sds((T, MIX), BF16)],
                  sem=("parallel",))(h, h, *([h] * ATT_NV), *([do] * ATT_NV), *([lse_t] * ATT_NV),
                                     *([delta_t] * ATT_NV), table_k)


def _all_gather(x, name):
    R, C = x.shape

    def body(x_ref, out_ref, send_sems, recv_sems, local_sem):
        xi, yi, ci = lax.axis_index("x"), lax.axis_index("y"), lax.axis_index("c")
        me, sibling = (xi, yi, ci), (xi, yi, 1 - ci)
        chips = [(1 - xi, yi), (xi, 1 - yi), (1 - xi, 1 - yi)]

        def slot(px, py, pc):
            return out_ref.at[4 * px + 2 * py + pc]

        def copy(k, block, to, src=None):
            return pltpu.make_async_remote_copy(
                src_ref=slot(*block) if src is None else src, dst_ref=slot(*block),
                send_sem=send_sems.at[k], recv_sem=recv_sems.at[k], device_id=to, device_id_type=_MESH)

        mine = pltpu.make_async_copy(x_ref, slot(*me), local_sem)
        mine.start()
        first = [copy(0, me, sibling, src=x_ref)]
        first += [copy(1 + j, me, (*chip, ci), src=x_ref) for j, chip in enumerate(chips)]
        for cp in first:
            cp.start()
        passed = [copy(4 + j, (*chip, ci), sibling) for j, chip in enumerate(chips)]
        for j, chip in enumerate(chips):
            copy(1 + j, (*chip, ci), me).wait_recv()
            passed[j].start()
        copy(0, sibling, me).wait_recv()
        for j, chip in enumerate(chips):
            copy(4 + j, (*chip, 1 - ci), me).wait_recv()
        for cp in first + passed:
            cp.wait_send()
        mine.wait()

    return pl.pallas_call(
        body, name=name, out_shape=_sds((N_DEV, R, C), x.dtype), in_specs=[_ANY], out_specs=_ANY,
        scratch_shapes=[pltpu.SemaphoreType.DMA((7,)), pltpu.SemaphoreType.DMA((7,)), pltpu.SemaphoreType.DMA(())],
    )(x)


def _a2a_copies(s_ref, r_ref, send_sems, recv_sems, local_sem, gather=False):
    xi, yi, ci = lax.axis_index("x"), lax.axis_index("y"), lax.axis_index("c")
    me = 4 * xi + 2 * yi + ci

    def mine():
        return pltpu.make_async_copy(s_ref if gather else s_ref.at[me], r_ref.at[me], local_sem)

    def remote(m, sending):
        px = 1 - xi if m & 4 else xi
        py = 1 - yi if m & 2 else yi
        pc = 1 - ci if m & 1 else ci
        peer = 4 * px + 2 * py + pc
        src, dst = (s_ref.at[peer], r_ref.at[me]) if sending else (s_ref.at[me], r_ref.at[peer])
        if gather:
            src = s_ref
        return pltpu.make_async_remote_copy(src_ref=src, dst_ref=dst, send_sem=send_sems.at[m - 1],
                                            recv_sem=recv_sems.at[m - 1], device_id=(px, py, pc), device_id_type=_MESH)

    def start():
        mine().start()
        for m in range(1, N_DEV):
            remote(m, True).start()

    def wait():
        for m in range(1, N_DEV):
            remote(m, False).wait_recv()
        for m in range(1, N_DEV):
            remote(m, True).wait_send()
        mine().wait()

    return start, wait


_A2A_SEMS = [pltpu.SemaphoreType.DMA((7,)), pltpu.SemaphoreType.DMA((7,)), pltpu.SemaphoreType.DMA(())]


def _all_to_all(s, name):
    def body(s_ref, r_ref, send_sems, recv_sems, local_sem):
        start, wait = _a2a_copies(s_ref, r_ref, send_sems, recv_sems, local_sem)
        start()
        wait()

    return pl.pallas_call(
        body, name=name, out_shape=_sds(s.shape, s.dtype), in_specs=[_ANY], out_specs=_ANY,
        scratch_shapes=list(_A2A_SEMS))(s)


def _riding(body, n_in, n_out, n_ex, first, last, gather=False):
    def wrapped(*refs):
        ins = refs[:n_in]
        sends = refs[n_in:n_in + n_ex]
        outs = refs[n_in + n_ex:n_in + n_ex + n_out]
        recvs = refs[n_in + n_ex + n_out:n_in + 2 * n_ex + n_out]
        rest = refs[n_in + 2 * n_ex + n_out:]
        scratch, sems = rest[:len(rest) - 3 * n_ex], rest[len(rest) - 3 * n_ex:]

        @pl.when(first())
        def _():
            for k in range(n_ex):
                _a2a_copies(sends[k], recvs[k], *sems[3 * k:3 * k + 3], gather=gather)[0]()

        body(*ins, *outs, *scratch)

        @pl.when(last())
        def _():
            for k in range(n_ex):
                _a2a_copies(sends[k], recvs[k], *sems[3 * k:3 * k + 3], gather=gather)[1]()

    return wrapped


_ADAMW_BLOCK_BYTES = 6 * 1024 * 1024


def _adamw(parts, row_off, w, m, v, name):
    R, C = w.shape
    tr = None
    for cand in (512, 256, 128, 64, 32, 16):
        step_bytes = cand * C * (N_DEV * parts.dtype.itemsize + 7 * 4)
        if R % cand == 0 and row_off % cand == 0 and step_bytes <= _ADAMW_BLOCK_BYTES:
            tr = cand
            break
    assert tr is not None, (R, C, row_off)
    off = row_off // tr
    c1 = 1.0 - ADAM_B1 ** ADAM_STEP
    c2 = 1.0 - ADAM_B2 ** ADAM_STEP

    def body(p_ref, w_ref, m_ref, v_ref, g_ref, d_ref, mo_ref, vo_ref):
        g = p_ref[0].astype(F32)
        for j in range(1, N_DEV):
            g = g + p_ref[j].astype(F32)
        mn = ADAM_B1 * m_ref[...] + (1.0 - ADAM_B1) * g
        vn = ADAM_B2 * v_ref[...] + (1.0 - ADAM_B2) * (g * g)
        m_hat = mn / c1
        v_hat = vn / c2
        g_ref[...] = g
        d_ref[...] = -ADAM_LR * (m_hat / (jnp.sqrt(v_hat) + ADAM_EPS) + ADAM_WD * w_ref[...])
        mo_ref[...] = mn
        vo_ref[...] = vn

    blk = _rows(tr, C)
    return _pcall(body, name=name, grid=(R // tr,),
                  in_specs=[pl.BlockSpec((N_DEV, tr, C), lambda i: (0, off + i, 0)), blk, blk, blk],
                  out_specs=[blk] * 4, out_shape=[_sds((R, C), F32)] * 4,
                  sem=("parallel",))(parts, w, m, v)


def _piece_rows(size):
    return -(-size // PIECE) * 16


def _pack_rows(arrays, lead=()):
    parts = []
    total = 0
    for a in arrays:
        flat = a.reshape(lead + (-1,))
        size = flat.shape[-1]
        rows = _piece_rows(size)
        pad = [(0, 0)] * len(lead) + [(0, rows * FLAT_COLS - size)]
        parts.append(jnp.pad(flat, pad).reshape(lead + (rows, FLAT_COLS)))
        total += rows
    tail = -total % FLAT_ROWS
    if tail:
        parts.append(jnp.zeros(lead + (tail, FLAT_COLS), parts[0].dtype))
    return jnp.concatenate(parts, axis=len(lead))


def _unpack_rows(buf, shapes, lead=()):
    out = []
    row = 0
    nl = len(lead)
    for shape in shapes:
        size = math.prod(shape)
        rows = _piece_rows(size)
        piece = lax.slice_in_dim(buf, row, row + rows, axis=nl).reshape(lead + (rows * FLAT_COLS,))
        out.append(lax.slice_in_dim(piece, 0, size, axis=nl).reshape(lead + tuple(shape)))
        row += rows
    return out


def _to_blocks(full, axis):
    shp = full.shape
    split = full.reshape(shp[:axis] + (N_DEV, shp[axis] // N_DEV) + shp[axis + 1:])
    return jnp.moveaxis(split, axis, 0)


def _from_blocks(blocks, axis):
    shp = blocks.shape[1:]
    moved = jnp.moveaxis(blocks, 0, axis)
    return moved.reshape(shp[:axis] + (N_DEV * shp[axis],) + shp[axis + 1:])


N_LAYERS = {n: (DEPTH // 2 if n.startswith(('ev_', 'od_')) else DEPTH) for n in WEIGHT_NAMES}
R_OUT = D_MODEL // N_DEV
R_DOWN = D_FF // N_DEV


def _layer_shards(W, i):
    j = i // 2
    ev = i % 2 == 0
    rows = [W['ev_w_out' if ev else 'od_w_out'][j], W['ffn_w_down'][i], W['ple_w_gate'][i]]
    return dict(up=W['ffn_w_up'][i].astype(BF16), inn=W['ev_w_in' if ev else 'od_w_in'][j].astype(BF16),
                rows=jnp.concatenate(rows, axis=0).astype(BF16))


def _layer_weights(up, inn, rows):
    return dict(up=up.reshape(2, FF_J, 1, D_MODEL, FF_NB), win=inn.reshape(N_DEV, 1, D_MODEL, IN_NB),
                wo=lax.slice_in_dim(rows, 0, R_OUT, axis=1).reshape(1, D_MODEL, D_MODEL),
                wd4=lax.slice_in_dim(rows, R_OUT, R_OUT + R_DOWN, axis=1).reshape(1, FF_J, FF_NB, D_MODEL),
                gate=lax.slice_in_dim(rows, R_OUT + R_DOWN, 2 * R_OUT + R_DOWN, axis=1).reshape(1, D_MODEL, D_MODEL))


def _layer_weights_of(W, i):
    j = i // 2
    ev = i % 2 == 0
    return dict(up=W['ffn_w_up'][:, i:i + 1].reshape(2, FF_J, 1, D_MODEL, FF_NB),
                win=W['ev_w_in' if ev else 'od_w_in'][:, j:j + 1],
                wo=W['ev_w_out' if ev else 'od_w_out'][j:j + 1],
                wd4=W['ffn_w_down'][i:i + 1].reshape(1, FF_J, FF_NB, D_MODEL), gate=W['ple_w_gate'][i:i + 1])


def _row_blocks(g):
    return g.reshape(N_DEV, g.shape[0] // N_DEV, g.shape[1])


def _early_sends(G, i):
    ffn = jnp.concatenate([_row_blocks(G['ffn_w_down'][i]), _row_blocks(G['ple_w_gate'][i])], axis=1)
    return dict(up=G['ffn_w_up'][i], ffn=ffn)


def _late_sends(G, i):
    j = i // 2
    ev = i % 2 == 0
    return dict(out=_row_blocks(G['ev_w_out' if ev else 'od_w_out'][j]), inn=G['ev_w_in' if ev else 'od_w_in'][j])


def _local_step(x, p, tgt, W, big=None, shards=None):
    overlap = shards is not None
    big = list(big) if overlap else [_layer_weights_of(W, i) for i in range(DEPTH)]
    ln = {n: _vecs(W[n]) for n in ('ln_mix_g', 'ln_mix_b', 'ln_ffn_g', 'ln_ffn_b', 'ple_b_gate')}
    bglu = _vecs(W['ev_b_glu'])
    pscale = _vecs(W['od_pool_scale'])
    saved = []
    x0 = x
    x0h = x.astype(BF16)
    for i in range(DEPTH):
        L = f"L{i}_"
        j = i // 2
        B = big[i]
        s = dict(x0=x0, x0h=x0h, B=B)
        h = _mm_in(x0h, B['win'], 0, L + "mm_in")
        if i % 2 == 0:
            params = tuple(W[n][j] for n in ('ev_lambda_re', 'ev_lambda_im', 'ev_log_dt', 'ev_b_re', 'ev_b_im',
                                             'ev_c_re', 'ev_c_im', 'ev_d'))
            (r0, qt, pm, a1, a2), op_vjp = jax.vjp(_s5_operators, *params)
            umat = _to_chunks(h[:, :MIX])
            st = jnp.transpose(_s5_chunk_state(umat, pm, L + "s5_state"), (1, 0, 2))
            xp_t = _s5_scan_fwd(st, a1, a2, L + "s5_scan")
            xprev = jnp.transpose(xp_t, (1, 0, 2))
            y = _from_chunks(_s5_output(umat, xprev, r0, qt, L + "s5_out"))
            ya, gh = _glu_fwd(y, W['ev_w_glu'], bglu, j, L + "glu")
            yb = _conv_fwd(h, W['ev_conv_w'], j, L + "conv")
            s.update(op_vjp=op_vjp, r0=r0, qt=qt, pm=pm, a1=a1, a2=a2, umat=umat, xp_t=xp_t, xprev=xprev, y=y, gh=gh)
        else:
            table, tb_vjp = jax.vjp(_att_bias_table, W['od_rel_bias'][j])
            of, ya, lse = _att_fwd(h, table, L + "att")
            yb = _pool_fwd(h, W['od_pool_w'], pscale, j, L + "pool")
            s.update(table=table, tb_vjp=tb_vjp, of=of, lse=lse)
        g1, b1 = (ln['ln_mix_g'], i), (ln['ln_mix_b'], i)
        g2, b2 = (ln['ln_ffn_g'], i), (ln['ln_ffn_b'], i)
        xhat1, rstd1, x1h = _mm_out_ln(ya, yb, B['wo'], 0, x0, g1, b1, L + "mm_out_ln")
        nxt = shards[i + 1] if overlap and i + 1 < DEPTH else None
        (hf, a3), got_a = _mm_up(x1h, B['up'], 0, L + "mm_up", ride=(nxt['up'], nxt['inn']) if nxt else ())
        (xhat2, rstd2, x2h), got_b = _mm_down_ln(a3, B['wd4'], 0, xhat1, g1, b1, g2, b2, L + "mm_down_ln",
                                                 ride=(nxt['rows'],) if nxt else ())
        if nxt:
            big.append(_layer_weights(got_a[0], got_a[1], got_b[0]))
        x3, x3h = _mm_ple(x2h, xhat2, ln['ln_ffn_g'], ln['ln_ffn_b'], p, B['gate'], ln['ple_b_gate'],
                          W['ple_w_proj'], i, L + "mm_ple", lw=0)
        s.update(h=h, ya=ya, yb=yb, xhat1=xhat1, rstd1=rstd1, x1h=x1h, hf=hf, a3=a3, xhat2=xhat2,
                 rstd2=rstd2, x2h=x2h, g1=g1)
        saved.append(s)
        x0, x0h = x3, x3h

    dx, loss = _loss_head(x0, tgt, "loss_head")

    G = {n: [None] * N_LAYERS[n] for n in WEIGHT_NAMES}
    landed = {i: {} for i in range(DEPTH)}
    pending = None
    for i in reversed(range(DEPTH)):
        L = f"L{i}_"
        j = i // 2
        s = saved[i]
        B = s['B']
        dr2, dr2h, dpreh, dpph, dbg, dg2, db2 = _ple_bwd(
            dx, s['x2h'], p, B['gate'], ln['ple_b_gate'], W['ple_w_proj'], i, s['xhat2'], s['rstd2'],
            ln['ln_ffn_g'], L + "ple_bwd", lw=0)
        G['ple_w_gate'][i] = _mm_tn(s['x2h'], dpreh, L + "dw_gate")
        G['ple_w_proj'][i] = _mm_tn(p, dpph, L + "dw_proj", a_layer=i)
        G['ple_b_gate'][i] = dbg[0]
        G['ln_ffn_g'][i] = dg2[0]
        G['ln_ffn_b'][i] = db2[0]
        dhf, got = _ffn_bwd1(dr2h, B['wd4'], 0, s['hf'], L + "ffn_bwd1",
                             ride=(pending['out'], pending['inn']) if pending else ())
        if pending:
            landed[i + 1].update(out=got[0], inn=got[1])
        G['ffn_w_down'][i] = _mm_tn_ablk(s['a3'], dr2h, L + "dw_down").reshape(D_FF, D_MODEL)
        T = dhf.shape[2]
        G['ffn_w_up'][i] = _mm_tn_bblk(s['x1h'], dhf.reshape(N_DEV, T, FF_NB), L + "dw_up")
        early = _early_sends(G, i) if overlap else None
        (dr1, dr1h, dg1, db1), got = _ffn_bwd2(dhf, B['up'], 0, dr2, s['xhat1'], s['rstd1'], s['g1'], L + "ffn_bwd2",
                                               ride=(early['up'], early['ffn']) if overlap else ())
        if overlap:
            landed[i].update(up=got[0], ffn=got[1])
        G['ln_mix_g'][i] = dg1[0]
        G['ln_mix_b'][i] = db1[0]
        dya, dyb = _out_bwd(dr1h, B['wo'], 0, L + "out_bwd")
        dwo = jnp.concatenate([_mm_tn(s['ya'], dr1h, L + "dw_out_a"), _mm_tn(s['yb'], dr1h, L + "dw_out_b")], axis=0)
        if i % 2 == 0:
            G['ev_w_out'][j] = dwo
            dy, dzh, dbglu = _glu_bwd(s['y'], dya, W['ev_w_glu'], bglu, j, L + "glu_bwd")
            G['ev_w_glu'][j] = _mm_tn(s['gh'], dzh, L + "dw_glu")
            G['ev_b_glu'][j] = dbglu[0]
            dymat = _to_chunks(dy)
            dxp_t = jnp.transpose(_s5_bwd_state(dymat, s['qt'], L + "s5_bwd_state"), (1, 0, 2))
            ds_t, da1, da2 = _s5_scan_bwd(dxp_t, s['xp_t'], s['a1'], s['a2'], L + "s5_scan_bwd")
            dumat, dr0, dqt, dpm = _s5_bwd_main(s['umat'], s['xprev'], dymat, jnp.transpose(ds_t, (1, 0, 2)),
                                                s['r0'], s['pm'], L + "s5_bwd")
            dparams = s['op_vjp']((dr0, dqt, dpm, da1, da2))
            for n, dpar in zip(('ev_lambda_re', 'ev_lambda_im', 'ev_log_dt', 'ev_b_re', 'ev_b_im', 'ev_c_re',
                                'ev_c_im', 'ev_d'), dparams):
                G[n][j] = dpar
            dua = _from_chunks(dumat)
            dh, dcw = _conv_bwd(dyb, s['h'], dua, W['ev_conv_w'], j, L + "conv_bwd")
            G['ev_conv_w'][j] = dcw[:3]
            wname = 'ev_w_in'
        else:
            G['od_w_out'][j] = dwo
            dq, delta, dtable = _att_bwd_q(s['h'], dya, s['of'], s['lse'], s['table'], L + "att_bwd_q")
            dk, dv = _att_bwd_kv(s['h'], dya, s['lse'].T, delta.T, _att_table_by_key(s['table']), L + "att_bwd_kv")
            G['od_rel_bias'][j] = s['tb_vjp'](dtable)[0]
            dh, dpw, dsc = _pool_bwd(dyb, s['h'], dq, dk, dv, W['od_pool_w'], pscale, j, L + "pool_bwd")
            G['od_pool_w'][j] = dpw
            G['od_pool_scale'][j] = dsc[0]
            wname = 'od_w_in'
        G[wname][j] = _mm_tn(s['x0h'], dh, L + "dw_in", blocked_n=IN_NB)
        dx = _in_bwd(dh, B['win'], 0, dr1, L + "in_bwd")
        pending = _late_sends(G, i) if overlap else None

    if overlap:
        return loss, dx, G, landed, pending
    return loss, dx, G


def _slab(a):
    return a.reshape(-1, a.shape[-1])


def _gather_small(W):
    full = {n: W[n] for n in REPLICATED}
    got = _all_gather(_pack_rows([W[n].astype(BF16) for n in SMALL_SHARDED]), "gather_w_small")
    shapes = [W[n].shape for n in SMALL_SHARDED]
    for n, blocks in zip(SMALL_SHARDED, _unpack_rows(got, shapes, lead=(N_DEV,))):
        full[n] = _from_blocks(blocks, SHARD_AXIS[n])
    for n in ('ev_conv_w', 'od_pool_scale'):
        full[n] = full[n].astype(F32)
    return full


def _step(x, p, tgt, W, M, V):
    shards = [_layer_shards(W, i) for i in range(DEPTH)]
    first = _layer_weights(*[_all_gather(shards[0][k], "gather_w0_" + k) for k in ('up', 'inn', 'rows')])
    loss, dx, G, landed, tail = _local_step(x[0], p[:, 0], tgt[0], _gather_small(W), big=[first], shards=shards)
    landed[0].update({k: _all_to_all(v, "scatter_g_" + k) for k, v in tail.items()})
    res = {}

    def update(parts, n):
        shape = W[n].shape
        outs = _adamw(parts, 0, _slab(W[n]), _slab(M[n]), _slab(V[n]), "adamw_" + n)
        for kind, a in zip(('grad', 'delta', 'm', 'v'), outs):
            res[kind, n] = a.reshape(shape)

    def over_layers(layers, key, lo=None, hi=None):
        got = [landed[i][key] for i in layers]
        if lo is not None:
            got = [lax.slice_in_dim(g, lo, hi, axis=1) for g in got]
        return jnp.concatenate(got, axis=1)

    even, odd, every = (0, 2), (1, 3), (0, 1, 2, 3)
    update(over_layers(even, 'out'), 'ev_w_out')
    update(over_layers(odd, 'out'), 'od_w_out')
    update(over_layers(every, 'ffn', 0, R_DOWN), 'ffn_w_down')
    update(over_layers(every, 'ffn', R_DOWN, R_DOWN + R_OUT), 'ple_w_gate')
    update(over_layers(every, 'up'), 'ffn_w_up')
    update(over_layers(even, 'inn'), 'ev_w_in')
    update(over_layers(odd, 'inn'), 'od_w_in')

    shapes = [W[n].shape for n in SMALL_SHARDED]
    send = _pack_rows([_to_blocks(jnp.stack(G[n], axis=0), SHARD_AXIS[n]).astype(BF16) for n in SMALL_SHARDED],
                      lead=(N_DEV,))
    parts = _all_to_all(send, "scatter_g_small")
    outs = _adamw(parts, 0, _pack_rows([W[n] for n in SMALL_SHARDED]), _pack_rows([M[n] for n in SMALL_SHARDED]),
                  _pack_rows([V[n] for n in SMALL_SHARDED]), "adamw_small")
    for kind, buf in zip(('grad', 'delta', 'm', 'v'), outs):
        for n, a in zip(SMALL_SHARDED, _unpack_rows(buf, shapes)):
            res[kind, n] = a

    repl_shapes = [W[n].shape for n in REPLICATED]
    small = _pack_rows([jnp.stack(G[n], axis=0) for n in REPLICATED] + [loss])
    parts = _all_gather(small, "gather_g_replicated")
    zero = jnp.zeros((1, 1), F32)
    outs = _adamw(parts, 0, _pack_rows([W[n] for n in REPLICATED] + [zero]),
                  _pack_rows([M[n] for n in REPLICATED] + [zero]),
                  _pack_rows([V[n] for n in REPLICATED] + [zero]), "adamw_replicated")
    for kind, buf in zip(('grad', 'delta', 'm', 'v'), outs):
        arrays = _unpack_rows(buf, repl_shapes + [(1, 1)])
        for n, a in zip(REPLICATED, arrays):
            res[kind, n] = a
        if kind == 'grad':
            total_loss = arrays[-1].reshape(())

    out = [total_loss, dx[None]]
    for kind in ('grad', 'delta', 'm', 'v'):
        out += [res[kind, n] for n in WEIGHT_NAMES]
    return tuple(out)


def kernel(x, p, ev_w_in, ev_lambda_re, ev_lambda_im, ev_log_dt, ev_b_re, ev_b_im, ev_c_re, ev_c_im, ev_d, ev_w_glu, ev_b_glu, ev_conv_w, ev_w_out, od_w_in, od_rel_bias, od_pool_w, od_pool_scale, od_w_out, ln_mix_g, ln_mix_b, ln_ffn_g, ln_ffn_b, ffn_w_up, ffn_w_down, ple_w_proj, ple_w_gate, ple_b_gate, loss_target, m_ev_w_in, m_ev_lambda_re, m_ev_lambda_im, m_ev_log_dt, m_ev_b_re, m_ev_b_im, m_ev_c_re, m_ev_c_im, m_ev_d, m_ev_w_glu, m_ev_b_glu, m_ev_conv_w, m_ev_w_out, m_od_w_in, m_od_rel_bias, m_od_pool_w, m_od_pool_scale, m_od_w_out, m_ln_mix_g, m_ln_mix_b, m_ln_ffn_g, m_ln_ffn_b, m_ffn_w_up, m_ffn_w_down, m_ple_w_proj, m_ple_w_gate, m_ple_b_gate, v_ev_w_in, v_ev_lambda_re, v_ev_lambda_im, v_ev_log_dt, v_ev_b_re, v_ev_b_im, v_ev_c_re, v_ev_c_im, v_ev_d, v_ev_w_glu, v_ev_b_glu, v_ev_conv_w, v_ev_w_out, v_od_w_in, v_od_rel_bias, v_od_pool_w, v_od_pool_scale, v_od_w_out, v_ln_mix_g, v_ln_mix_b, v_ln_ffn_g, v_ln_ffn_b, v_ffn_w_up, v_ffn_w_down, v_ple_w_proj, v_ple_w_gate, v_ple_b_gate):
    given = dict(locals())
    W = {n: given[n] for n in WEIGHT_NAMES}
    M = {n: given["m_" + n] for n in WEIGHT_NAMES}
    V = {n: given["v_" + n] for n in WEIGHT_NAMES}
    return _step(x, p, loss_target, W, M, V)
```

```python
import math

import jax
import jax.numpy as jnp
from jax import lax
from jax.experimental import pallas as pl
from jax.experimental.pallas import tpu as pltpu

F32 = jnp.float32
BF16 = jnp.bfloat16
HI = lax.Precision.HIGHEST

D_MODEL = 1024
DEPTH = 4
CHUNK = 64
MIX = 512
S5_GROUP = 16
S5_GROUPS = 32
S5_STATE = 64
HEADS = 8
HEAD_DIM = 64
LEFT_CHUNKS = 8
MAX_REL = 128
POOL_WINDOWS = (2, 4, 8, 16)
POOL_GROUP = 128
D_FF = 2816
D_PLE = 256
ALPHA = (2 * DEPTH) ** 0.25
LN_EPS = 1e-5
NEG_INF = -1e30
ADAM_LR = 0.001
ADAM_B1 = 0.9
ADAM_B2 = 0.999
ADAM_EPS = 1e-08
ADAM_WD = 0.01
ADAM_STEP = 10
N_DEV = 8

WEIGHT_NAMES = ['ev_w_in', 'ev_lambda_re', 'ev_lambda_im', 'ev_log_dt', 'ev_b_re', 'ev_b_im', 'ev_c_re', 'ev_c_im',
                'ev_d', 'ev_w_glu', 'ev_b_glu', 'ev_conv_w', 'ev_w_out', 'od_w_in', 'od_rel_bias', 'od_pool_w',
                'od_pool_scale', 'od_w_out', 'ln_mix_g', 'ln_mix_b', 'ln_ffn_g', 'ln_ffn_b', 'ffn_w_up', 'ffn_w_down',
                'ple_w_proj', 'ple_w_gate', 'ple_b_gate']
SHARD_AXIS = {'ev_w_in': 2, 'ev_w_glu': 1, 'ev_conv_w': 2, 'ev_w_out': 1, 'od_w_in': 2, 'od_pool_scale': 1,
              'od_w_out': 1, 'ffn_w_up': 2, 'ffn_w_down': 1, 'ple_w_proj': 2, 'ple_w_gate': 1}
REPLICATED = [n for n in WEIGHT_NAMES if n not in SHARD_AXIS]
ROW_SHARDED = ['ev_w_out', 'od_w_out', 'ffn_w_down', 'ple_w_gate']
COL_IN = ['ev_w_in', 'od_w_in']
SMALL_SHARDED = ['ev_w_glu', 'ev_conv_w', 'od_pool_scale', 'ple_w_proj']

VMEM_LIMIT = 48 * 1024 * 1024
TM = 512
IN_NB = 4 * MIX // N_DEV
FF_NB = 2 * D_FF // N_DEV
FF_J = N_DEV // 2
S5_LC = 32
S5_LW = S5_LC * S5_GROUP
ATT_TQ = 128
ATT_NV = LEFT_CHUNKS * CHUNK // ATT_TQ + 1
ATT_W = ATT_NV * ATT_TQ
FLAT_COLS = 1024
FLAT_ROWS = 256
PIECE = 16 * FLAT_COLS


_MESH = pl.DeviceIdType.MESH
_ANY = pl.BlockSpec(memory_space=pl.ANY)


def _sds(shape, dt):
    return jax.ShapeDtypeStruct(shape, dt)


def _pcall(body, *, name, grid, in_specs, out_specs, out_shape, scratch=(), sem=None):
    sem = sem or ("arbitrary",) * len(grid)
    return pl.pallas_call(
        body, name=name, grid=grid, in_specs=in_specs, out_specs=out_specs, out_shape=out_shape,
        scratch_shapes=scratch,
        compiler_params=pltpu.CompilerParams(dimension_semantics=sem, vmem_limit_bytes=VMEM_LIMIT))


def _rows(tm, n):
    return pl.BlockSpec((tm, n), lambda i: (i, 0))


def _cols(tm, s):
    return pl.BlockSpec((tm, MIX), lambda i: (i, s))


def _full(shape):
    nd = len(shape)
    return pl.BlockSpec(shape, lambda *_: (0,) * nd)


def _layer(shape, l):
    nd = len(shape)
    return pl.BlockSpec((None,) + tuple(shape), lambda *_: (l,) + (0,) * nd)


def _vecs(a):
    return a.reshape(a.shape[0], 1, a.shape[1])


def _dot(a, b, precision=None):
    return jnp.dot(a, b, preferred_element_type=F32, precision=precision)


def _dot_nt(a, b, precision=None):
    return lax.dot_general(a, b, (((1,), (1,)), ((), ())), preferred_element_type=F32, precision=precision)


def _dot_tn(a, b, precision=None):
    return lax.dot_general(a, b, (((0,), (0,)), ((), ())), preferred_element_type=F32, precision=precision)


def _split(a):
    hi = a.astype(BF16)
    return hi, (a - hi.astype(F32)).astype(BF16)


def _mm3(dot, a2, b2):
    return dot(a2[0], b2[0]) + (dot(a2[0], b2[1]) + dot(a2[1], b2[0]))


def _sigmoid(x):
    return 0.5 * jnp.tanh(0.5 * x) + 0.5


_GELU_C = math.sqrt(2.0 / math.pi)


def _gelu(x):
    return 0.5 * x * (1.0 + jnp.tanh(_GELU_C * (x + 0.044715 * x * x * x)))


def _gelu_grad(x):
    t = jnp.tanh(_GELU_C * (x + 0.044715 * x * x * x))
    return 0.5 * (1.0 + t) + 0.5 * x * (1.0 - t * t) * _GELU_C * (1.0 + 3.0 * 0.044715 * x * x)


def _ln_fwd(r, g, b):
    mu = jnp.mean(r, axis=-1, keepdims=True)
    xc = r - mu
    var = jnp.mean(xc * xc, axis=-1, keepdims=True)
    rstd = lax.rsqrt(var + LN_EPS)
    xhat = xc * rstd
    return xhat, rstd, xhat * g + b


def _ln_bwd(dx, xhat, rstd, g):
    dxh = dx * g
    m1 = jnp.mean(dxh, axis=-1, keepdims=True)
    m2 = jnp.mean(dxh * xhat, axis=-1, keepdims=True)
    return rstd * (dxh - m1 - xhat * m2)


def _colsum(x):
    return jnp.sum(x, axis=0, keepdims=True)


def _mm_in(xh, win8, l, name):
    T = xh.shape[0]

    def body(x_ref, w_ref, h_ref):
        x = x_ref[...]
        for b in range(N_DEV):
            h_ref[:, b * IN_NB:(b + 1) * IN_NB] = _dot(x, w_ref[b])

    return _pcall(body, name=name, grid=(T // TM,),
                  in_specs=[_rows(TM, D_MODEL),
                            pl.BlockSpec((N_DEV, None, D_MODEL, IN_NB), lambda i: (0, l, 0, 0))],
                  out_specs=_rows(TM, 4 * MIX), out_shape=_sds((T, 4 * MIX), F32),
                  sem=("parallel",))(xh, win8)


def _tile_of(n, cap):
    best = None
    for t in range(128, min(n, cap) + 1, 128):
        if n % t == 0:
            best = t
    assert best is not None, n
    return best


def _mm_tn(a, b, name, a_layer=None, blocked_n=None):
    T, M = a.shape[-2:]
    N = b.shape[1]
    tm = _tile_of(M, 1408)
    nbs = max(1, 512 // blocked_n) if blocked_n else 1
    tn = blocked_n * nbs if blocked_n else _tile_of(N, 1024 if tm <= 512 else 512)
    tk = min(T, 1024)
    nk = T // tk

    def body(a_ref, b_ref, o_ref, acc):
        k = pl.program_id(2)

        @pl.when(k == 0)
        def _():
            acc[...] = jnp.zeros_like(acc)

        acc[...] += _dot_tn(a_ref[...].astype(BF16), b_ref[...].astype(BF16))

        @pl.when(k == nk - 1)
        def _():
            if blocked_n:
                for sb in range(nbs):
                    o_ref[sb] = acc[:, sb * blocked_n:(sb + 1) * blocked_n].astype(BF16)
            else:
                o_ref[...] = acc[...].astype(BF16)

    if a_layer is None:
        a_spec = pl.BlockSpec((tk, tm), lambda i, j, k: (k, i))
    else:
        a_spec = pl.BlockSpec((None, tk, tm), lambda i, j, k: (a_layer, k, i))
    if blocked_n:
        o_spec = pl.BlockSpec((nbs, tm, blocked_n), lambda i, j, k: (j, i, 0))
        o_shape = _sds((N // blocked_n, M, blocked_n), BF16)
    else:
        o_spec = pl.BlockSpec((tm, tn), lambda i, j, k: (i, j))
        o_shape = _sds((M, N), BF16)
    return _pcall(body, name=name, grid=(M // tm, N // tn, nk),
                  in_specs=[a_spec, pl.BlockSpec((tk, tn), lambda i, j, k: (k, j))],
                  out_specs=o_spec, out_shape=o_shape, scratch=[pltpu.VMEM((tm, tn), F32)],
                  sem=("parallel", "parallel", "arbitrary"))(a, b)


def _mm_tn_bblk(a, b3, name, per_step=2):
    T, M = a.shape
    NB, _, n = b3.shape
    tk = min(T, 1024)
    nk = T // tk

    def body(a_ref, b_ref, o_ref, acc):
        k = pl.program_id(1)

        @pl.when(k == 0)
        def _():
            acc[...] = jnp.zeros_like(acc)

        av = a_ref[...]
        for s in range(per_step):
            acc[s] += _dot_tn(av, b_ref[s])

        @pl.when(k == nk - 1)
        def _():
            o_ref[...] = acc[...].astype(BF16)

    return _pcall(body, name=name, grid=(NB // per_step, nk),
                  in_specs=[pl.BlockSpec((tk, M), lambda j, k: (k, 0)),
                            pl.BlockSpec((per_step, tk, n), lambda j, k: (j, k, 0))],
                  out_specs=pl.BlockSpec((per_step, M, n), lambda j, k: (j, 0, 0)),
                  out_shape=_sds((NB, M, n), BF16), scratch=[pltpu.VMEM((per_step, M, n), F32)],
                  sem=("parallel", "arbitrary"))(a, b3)


def _mm_tn_ablk(a3, b, name):
    NA, T, m = a3.shape
    N = b.shape[1]
    tn = _tile_of(N, 1024)
    tk = min(T, 1024)
    nk = T // tk

    def body(a_ref, b_ref, o_ref, acc):
        k = pl.program_id(2)

        @pl.when(k == 0)
        def _():
            acc[...] = jnp.zeros_like(acc)

        acc[...] += _dot_tn(a_ref[...], b_ref[...])

        @pl.when(k == nk - 1)
        def _():
            o_ref[...] = acc[...].astype(BF16)

    return _pcall(body, name=name, grid=(NA, N // tn, nk),
                  in_specs=[pl.BlockSpec((None, tk, m), lambda j, n, k: (j, k, 0)),
                            pl.BlockSpec((tk, tn), lambda j, n, k: (k, n))],
                  out_specs=pl.BlockSpec((None, m, tn), lambda j, n, k: (j, 0, n)),
                  out_shape=_sds((NA, m, N), BF16), scratch=[pltpu.VMEM((m, tn), F32)],
                  sem=("parallel", "parallel", "arbitrary"))(a3, b)


def _mm_out_ln(ya, yb, wo, l, x0, g, b, name):
    T = x0.shape[0]

    def body(ya_ref, yb_ref, w_ref, x0_ref, g_ref, b_ref, xh_ref, rs_ref, x1_ref):
        r = ALPHA * x0_ref[...] + _dot(ya_ref[...], w_ref[0:MIX, :]) + _dot(yb_ref[...], w_ref[MIX:, :])
        xhat, rstd, x1 = _ln_fwd(r, g_ref[...], b_ref[...])
        xh_ref[...] = xhat
        rs_ref[...] = rstd
        x1_ref[...] = x1.astype(BF16)

    vec = _layer((1, D_MODEL), g[1])
    return _pcall(body, name=name, grid=(T // TM,),
                  in_specs=[_rows(TM, MIX), _rows(TM, MIX), _layer((D_MODEL, D_MODEL), l), _rows(TM, D_MODEL),
                            vec, vec],
                  out_specs=[_rows(TM, D_MODEL), _rows(TM, 1), _rows(TM, D_MODEL)],
                  out_shape=[_sds((T, D_MODEL), F32), _sds((T, 1), F32), _sds((T, D_MODEL), BF16)],
                  sem=("parallel",))(ya, yb, wo, x0, g[0], b[0])


def _mm_up(x1h, wup, l, name, ride=()):
    T = x1h.shape[0]
    ni = T // TM

    def body(x_ref, w_ref, hf_ref, a_ref):
        x = x_ref[...]
        g = _dot(x, w_ref[0])
        u = _dot(x, w_ref[1])
        hf_ref[0] = g.astype(BF16)
        hf_ref[1] = u.astype(BF16)
        a_ref[...] = (g * _sigmoid(g) * u).astype(BF16)

    in_specs = [pl.BlockSpec((TM, D_MODEL), lambda j, i: (i, 0)),
                pl.BlockSpec((2, None, None, D_MODEL, FF_NB), lambda j, i: (0, j, l, 0, 0))]
    out_specs = [pl.BlockSpec((2, None, TM, FF_NB), lambda j, i: (0, j, i, 0)),
                 pl.BlockSpec((None, TM, FF_NB), lambda j, i: (j, i, 0))]
    out_shape = [_sds((2, FF_J, T, FF_NB), BF16), _sds((FF_J, T, FF_NB), BF16)]
    if not ride:
        return _pcall(body, name=name, grid=(FF_J, ni), in_specs=in_specs, out_specs=out_specs,
                      out_shape=out_shape, sem=("parallel", "parallel"))(x1h, wup), ()
    first = lambda: (pl.program_id(0) == 0) & (pl.program_id(1) == 0)
    last = lambda: (pl.program_id(0) == FF_J - 1) & (pl.program_id(1) == ni - 1)
    outs = _pcall(_riding(body, 2, 2, len(ride), first, last, gather=True), name=name, grid=(FF_J, ni),
                  in_specs=in_specs + [_ANY] * len(ride), out_specs=out_specs + [_ANY] * len(ride),
                  out_shape=out_shape + [_sds((N_DEV,) + r.shape, r.dtype) for r in ride],
                  scratch=list(_A2A_SEMS) * len(ride))(x1h, wup, *ride)
    return outs[:2], outs[2:]


def _mm_down_ln(a3, wd4, l, xhat1, g1, b1, g2, b2, name, ride=()):
    T = a3.shape[1]
    ni = T // TM

    def body(a_ref, w_ref, xh1_ref, g1_ref, b1_ref, g2_ref, b2_ref, xh_ref, rs_ref, x2_ref):
        x1 = xh1_ref[...] * g1_ref[...] + b1_ref[...]
        r = ALPHA * x1
        for j in range(FF_J):
            r = r + _dot(a_ref[j], w_ref[j])
        xhat, rstd, x2 = _ln_fwd(r, g2_ref[...], b2_ref[...])
        xh_ref[...] = xhat
        rs_ref[...] = rstd
        x2_ref[...] = x2.astype(BF16)

    vec = _layer((1, D_MODEL), g1[1])
    in_specs = [pl.BlockSpec((FF_J, TM, FF_NB), lambda i: (0, i, 0)),
                _layer((FF_J, FF_NB, D_MODEL), l), _rows(TM, D_MODEL), vec, vec, vec, vec]
    out_specs = [_rows(TM, D_MODEL), _rows(TM, 1), _rows(TM, D_MODEL)]
    out_shape = [_sds((T, D_MODEL), F32), _sds((T, 1), F32), _sds((T, D_MODEL), BF16)]
    operands = (a3, wd4, xhat1, g1[0], b1[0], g2[0], b2[0])
    if not ride:
        return _pcall(body, name=name, grid=(ni,), in_specs=in_specs, out_specs=out_specs, out_shape=out_shape,
                      sem=("parallel",))(*operands), ()
    first = lambda: pl.program_id(0) == 0
    last = lambda: pl.program_id(0) == ni - 1
    outs = _pcall(_riding(body, 7, 3, len(ride), first, last, gather=True), name=name, grid=(ni,),
                  in_specs=in_specs + [_ANY] * len(ride), out_specs=out_specs + [_ANY] * len(ride),
                  out_shape=out_shape + [_sds((N_DEV,) + r.shape, r.dtype) for r in ride],
                  scratch=list(_A2A_SEMS) * len(ride))(*operands, *ride)
    return outs[:3], outs[3:]


def _mm_ple(x2h, xhat2, g2, b2, p, wg, bg, wp, l, name, lw=None):
    lw = l if lw is None else lw
    T = x2h.shape[0]

    def body(x2h_ref, xh_ref, g2_ref, b2_ref, p_ref, wg_ref, bg_ref, wp_ref, o_ref, oh_ref):
        x2 = xh_ref[...] * g2_ref[...] + b2_ref[...]
        gate = _sigmoid(_dot(x2h_ref[...], wg_ref[...]) + bg_ref[...])
        pp = _dot(p_ref[...].astype(BF16), wp_ref[...])
        x3 = x2 + gate * pp
        o_ref[...] = x3
        oh_ref[...] = x3.astype(BF16)

    vec = _layer((1, D_MODEL), l)
    return _pcall(body, name=name, grid=(T // TM,),
                  in_specs=[_rows(TM, D_MODEL), _rows(TM, D_MODEL), vec, vec,
                            pl.BlockSpec((None, TM, D_PLE), lambda i: (l, i, 0)),
                            _layer((D_MODEL, D_MODEL), lw), vec, _layer((D_PLE, D_MODEL), l)],
                  out_specs=[_rows(TM, D_MODEL), _rows(TM, D_MODEL)],
                  out_shape=[_sds((T, D_MODEL), F32), _sds((T, D_MODEL), BF16)],
                  sem=("parallel",))(x2h, xhat2, g2, b2, p, wg, bg, wp)


def _loss_head(x3, tgt, name):
    T = x3.shape[0]

    def body(x_ref, t_ref, dx_ref, l_ref):
        e = x_ref[...] - t_ref[...]
        dx_ref[...] = e * (1.0 / D_MODEL)

        @pl.when(pl.program_id(0) == 0)
        def _():
            l_ref[...] = jnp.zeros_like(l_ref)

        l_ref[...] += (0.5 / D_MODEL) * jnp.sum(e * e).reshape(1, 1)

    return _pcall(body, name=name, grid=(T // TM,),
                  in_specs=[_rows(TM, D_MODEL), _rows(TM, D_MODEL)],
                  out_specs=[_rows(TM, D_MODEL), _full((1, 1))],
                  out_shape=[_sds((T, D_MODEL), F32), _sds((1, 1), F32)])(x3, tgt)


def _ple_bwd(dx3, x2h, p, wg, bg, wp, l, xhat2, rstd2, g2, name, lw=None):
    T = dx3.shape[0]
    lw = l if lw is None else lw

    def body(dx3_ref, x2h_ref, p_ref, wg_ref, bg_ref, wp_ref, xh_ref, rs_ref, g2_ref,
             dr_ref, drh_ref, dpre_ref, dpp_ref, dbg_ref, dg_ref, db_ref):
        dx3 = dx3_ref[...]
        gate = _sigmoid(_dot(x2h_ref[...], wg_ref[...]) + bg_ref[...])
        pp = _dot(p_ref[...].astype(BF16), wp_ref[...])
        dpre = dx3 * pp * gate * (1.0 - gate)
        dpreh = dpre.astype(BF16)
        dpre_ref[...] = dpreh
        dpp_ref[...] = (dx3 * gate).astype(BF16)
        dx2 = dx3 + _dot_nt(dpreh, wg_ref[...])
        xhat = xh_ref[...]
        dr = _ln_bwd(dx2, xhat, rs_ref[...], g2_ref[...])
        dr_ref[...] = dr
        drh_ref[...] = dr.astype(BF16)

        @pl.when(pl.program_id(0) == 0)
        def _():
            dbg_ref[...] = jnp.zeros_like(dbg_ref)
            dg_ref[...] = jnp.zeros_like(dg_ref)
            db_ref[...] = jnp.zeros_like(db_ref)

        dbg_ref[...] += _colsum(dpre)
        dg_ref[...] += _colsum(dx2 * xhat)
        db_ref[...] += _colsum(dx2)

    vec = _layer((1, D_MODEL), l)
    acc = _full((1, D_MODEL))
    big = _rows(TM, D_MODEL)
    return _pcall(body, name=name, grid=(T // TM,),
                  in_specs=[big, big, pl.BlockSpec((None, TM, D_PLE), lambda i: (l, i, 0)),
                            _layer((D_MODEL, D_MODEL), lw), vec, _layer((D_PLE, D_MODEL), l),
                            big, _rows(TM, 1), vec],
                  out_specs=[big, big, big, big, acc, acc, acc],
                  out_shape=[_sds((T, D_MODEL), F32), _sds((T, D_MODEL), BF16), _sds((T, D_MODEL), BF16),
                             _sds((T, D_MODEL), BF16), _sds((1, D_MODEL), F32), _sds((1, D_MODEL), F32),
                             _sds((1, D_MODEL), F32)])(dx3, x2h, p, wg, bg, wp, xhat2, rstd2, g2)


def _ffn_bwd1(dr2h, wd4, l, hf, name, ride=()):
    T = dr2h.shape[0]
    ni = T // TM

    def body(d_ref, w_ref, hf_ref, o_ref):
        da = _dot_nt(d_ref[...], w_ref[...])
        g = hf_ref[0].astype(F32)
        u = hf_ref[1].astype(F32)
        sg = _sigmoid(g)
        o_ref[0] = (da * u * (sg * (1.0 + g * (1.0 - sg)))).astype(BF16)
        o_ref[1] = (da * (g * sg)).astype(BF16)

    blk = pl.BlockSpec((2, None, TM, FF_NB), lambda j, i: (0, j, i, 0))
    in_specs = [pl.BlockSpec((TM, D_MODEL), lambda j, i: (i, 0)),
                pl.BlockSpec((None, None, FF_NB, D_MODEL), lambda j, i: (l, j, 0, 0)), blk]
    out_shape = _sds((2, FF_J, T, FF_NB), BF16)
    if not ride:
        return _pcall(body, name=name, grid=(FF_J, ni), in_specs=in_specs, out_specs=blk, out_shape=out_shape,
                      sem=("parallel", "parallel"))(dr2h, wd4, hf), ()
    first = lambda: (pl.program_id(0) == 0) & (pl.program_id(1) == 0)
    last = lambda: (pl.program_id(0) == FF_J - 1) & (pl.program_id(1) == ni - 1)
    outs = _pcall(_riding(body, 3, 1, len(ride), first, last), name=name, grid=(FF_J, ni),
                  in_specs=in_specs + [_ANY] * len(ride), out_specs=[blk] + [_ANY] * len(ride),
                  out_shape=[out_shape] + [_sds(r.shape, r.dtype) for r in ride],
                  scratch=list(_A2A_SEMS) * len(ride))(dr2h, wd4, hf, *ride)
    return outs[0], outs[1:]


_TM_B2 = 256


def _ffn_bwd2(dhf, wup, l, dr2, xhat1, rstd1, g1, name, ride=()):
    T = dr2.shape[0]
    tm = min(T, _TM_B2)
    ni = T // tm

    def body(dh_ref, w_ref, dr2_ref, xh_ref, rs_ref, g_ref, dr_ref, drh_ref, dg_ref, db_ref):
        dx1 = ALPHA * dr2_ref[...]
        for s in range(2):
            for j in range(FF_J):
                dx1 = dx1 + _dot_nt(dh_ref[s, j], w_ref[s, j])
        xhat = xh_ref[...]
        dr = _ln_bwd(dx1, xhat, rs_ref[...], g_ref[...])
        dr_ref[...] = dr
        drh_ref[...] = dr.astype(BF16)

        @pl.when(pl.program_id(0) == 0)
        def _():
            dg_ref[...] = jnp.zeros_like(dg_ref)
            db_ref[...] = jnp.zeros_like(db_ref)

        dg_ref[...] += _colsum(dx1 * xhat)
        db_ref[...] += _colsum(dx1)

    big = _rows(tm, D_MODEL)
    acc = _full((1, D_MODEL))
    in_specs = [pl.BlockSpec((2, FF_J, tm, FF_NB), lambda i: (0, 0, i, 0)),
                pl.BlockSpec((2, FF_J, None, D_MODEL, FF_NB), lambda i: (0, 0, l, 0, 0)),
                big, big, _rows(tm, 1), _layer((1, D_MODEL), g1[1])]
    out_specs = [big, big, acc, acc]
    out_shape = [_sds((T, D_MODEL), F32), _sds((T, D_MODEL), BF16), _sds((1, D_MODEL), F32),
                 _sds((1, D_MODEL), F32)]
    operands = (dhf, wup, dr2, xhat1, rstd1, g1[0])
    if not ride:
        return _pcall(body, name=name, grid=(ni,), in_specs=in_specs, out_specs=out_specs,
                      out_shape=out_shape)(*operands), ()
    first = lambda: pl.program_id(0) == 0
    last = lambda: pl.program_id(0) == ni - 1
    outs = _pcall(_riding(body, 6, 4, len(ride), first, last), name=name, grid=(ni,),
                  in_specs=in_specs + [_ANY] * len(ride), out_specs=out_specs + [_ANY] * len(ride),
                  out_shape=out_shape + [_sds(r.shape, r.dtype) for r in ride],
                  scratch=list(_A2A_SEMS) * len(ride))(*operands, *ride)
    return outs[:4], outs[4:]


def _out_bwd(dr1h, wo, l, name):
    T = dr1h.shape[0]

    def body(d_ref, w_ref, da_ref, db_ref):
        d = d_ref[...]
        da_ref[...] = _dot_nt(d, w_ref[0:MIX, :])
        db_ref[...] = _dot_nt(d, w_ref[MIX:, :])

    return _pcall(body, name=name, grid=(T // TM,),
                  in_specs=[_rows(TM, D_MODEL), _layer((D_MODEL, D_MODEL), l)],
                  out_specs=[_rows(TM, MIX), _rows(TM, MIX)],
                  out_shape=[_sds((T, MIX), F32), _sds((T, MIX), F32)],
                  sem=("parallel",))(dr1h, wo)


def _in_bwd(dh, win8, l, dr1, name):
    T = dr1.shape[0]

    def body(dh_ref, w_ref, dr_ref, o_ref):
        acc = ALPHA * dr_ref[...]
        for b in range(N_DEV):
            acc = acc + _dot_nt(dh_ref[:, b * IN_NB:(b + 1) * IN_NB], w_ref[b])
        o_ref[...] = acc

    return _pcall(body, name=name, grid=(T // TM,),
                  in_specs=[_rows(TM, 4 * MIX), pl.BlockSpec((N_DEV, None, D_MODEL, IN_NB), lambda i: (0, l, 0, 0)),
                            _rows(TM, D_MODEL)],
                  out_specs=_rows(TM, D_MODEL), out_shape=_sds((T, D_MODEL), F32),
                  sem=("parallel",))(dh, win8, dr1)


def _s5_operators(lre, lim, log_dt, bre, bim, cre, cim, dskip):
    G, P, H, LC = S5_GROUPS, S5_STATE, S5_GROUP, S5_LC
    dt = jnp.exp(log_dt)[:, None]
    mag = jnp.exp(lre * dt)
    ang = lim * dt
    lb_re = mag * jnp.cos(ang)
    lb_im = mag * jnp.sin(ang)
    den = lre * lre + lim * lim
    nr = lb_re - 1.0
    ni = lb_im
    r_re = (nr * lre + ni * lim) / den
    r_im = (ni * lre - nr * lim) / den
    bb_re = r_re[..., None] * bre - r_im[..., None] * bim
    bb_im = r_re[..., None] * bim + r_im[..., None] * bre
    k = jnp.arange(LC + 1, dtype=F32)[:, None, None]
    pmag = jnp.exp(k * (lre * dt)[None])
    pang = k * ang[None]
    pw_re = pmag * jnp.cos(pang)
    pw_im = pmag * jnp.sin(pang)
    cp_re = cre[None] * pw_re[:, :, None, :] - cim[None] * pw_im[:, :, None, :]
    cp_im = cre[None] * pw_im[:, :, None, :] + cim[None] * pw_re[:, :, None, :]
    kk = (jnp.einsum('kghp,gpj->kghj', cp_re[:LC], bb_re, precision=HI)
          - jnp.einsum('kghp,gpj->kghj', cp_im[:LC], bb_im, precision=HI))
    dmat = dskip.reshape(G, H)[:, :, None] * jnp.eye(H, dtype=F32)[None]
    kk = jnp.concatenate([kk[:1] + dmat[None], kk[1:]], axis=0)
    r0 = jnp.transpose(kk, (1, 3, 0, 2)).reshape(G, H, LC * H)
    qt = jnp.stack([cp_re[1:], -cp_im[1:]], axis=0)
    qt = jnp.transpose(qt, (2, 0, 4, 1, 3)).reshape(G, 2 * P, LC * H)
    pb_re = pw_re[:LC, :, :, None] * bb_re[None] - pw_im[:LC, :, :, None] * bb_im[None]
    pb_im = pw_re[:LC, :, :, None] * bb_im[None] + pw_im[:LC, :, :, None] * bb_re[None]
    pm = jnp.stack([pb_re[::-1], pb_im[::-1]], axis=0)
    pm = jnp.transpose(pm, (2, 1, 4, 0, 3)).reshape(G, LC * H, 2 * P)
    a_re = pw_re[LC]
    a_im = pw_im[LC]
    a1 = jnp.concatenate([a_re, a_re], axis=-1)
    a2 = jnp.concatenate([-a_im, a_im], axis=-1)
    return r0, qt, pm, a1, a2


_CHUNK_BLOCK = 128
_LANE_GROUPS = 128 // S5_GROUP


def _to_chunks(u, name):
    T = u.shape[0]
    nc = T // S5_LC
    cb = min(nc, _CHUNK_BLOCK)

    def body(x_ref, o_ref, scr):
        for s in range(S5_LC):
            xt = x_ref[pl.ds(s, cb, stride=S5_LC), :].T
            for g in range(_LANE_GROUPS):
                scr[g, s * S5_GROUP:(s + 1) * S5_GROUP, :] = xt[g * S5_GROUP:(g + 1) * S5_GROUP, :]
        for g in range(_LANE_GROUPS):
            o_ref[g] = scr[g].T

    return _pcall(body, name=name, grid=(nc // cb, S5_GROUPS // _LANE_GROUPS),
                  in_specs=[pl.BlockSpec((cb * S5_LC, 128), lambda i, k: (i, k))],
                  out_specs=pl.BlockSpec((_LANE_GROUPS, cb, S5_LW), lambda i, k: (k, i, 0)),
                  out_shape=_sds((S5_GROUPS, nc, S5_LW), F32),
                  scratch=[pltpu.VMEM((_LANE_GROUPS, S5_LW, cb), F32)], sem=("parallel", "parallel"))(u)


def _from_chunks(m, name):
    nc = m.shape[1]
    cb = min(nc, _CHUNK_BLOCK)

    def body(m_ref, y_ref, scr, rows):
        for g in range(_LANE_GROUPS):
            scr[g] = m_ref[g].T
        for s in range(S5_LC):
            for g in range(_LANE_GROUPS):
                rows[g * S5_GROUP:(g + 1) * S5_GROUP, :] = scr[g, s * S5_GROUP:(s + 1) * S5_GROUP, :]
            y_ref[pl.ds(s, cb, stride=S5_LC), :] = rows[...].T

    return _pcall(body, name=name, grid=(nc // cb, S5_GROUPS // _LANE_GROUPS),
                  in_specs=[pl.BlockSpec((_LANE_GROUPS, cb, S5_LW), lambda i, k: (k, i, 0))],
                  out_specs=pl.BlockSpec((cb * S5_LC, 128), lambda i, k: (i, k)),
                  out_shape=_sds((nc * S5_LC, MIX), F32),
                  scratch=[pltpu.VMEM((_LANE_GROUPS, S5_LW, cb), F32), pltpu.VMEM((128, cb), F32)],
                  sem=("parallel", "parallel"))(m)


def _gspec(r, c):
    return pl.BlockSpec((None, r, c), lambda g: (g, 0, 0))


def _s5_chunk_state(umat, pm, name):
    G, nc, _ = umat.shape

    def body(u_ref, p_ref, o_ref):
        o_ref[...] = _mm3(_dot, _split(u_ref[...]), _split(p_ref[...]))

    return _pcall(body, name=name, grid=(G,), in_specs=[_gspec(nc, S5_LW), _gspec(S5_LW, 128)],
                  out_specs=_gspec(nc, 128), out_shape=_sds((G, nc, 128), F32), sem=("parallel",))(umat, pm)


_SCAN_G = 8
_SCAN_UNROLL = 8


def _s5_scan_fwd(s_t, a1, a2, name):
    nc, G, _ = s_t.shape

    def body(s_ref, a1_ref, a2_ref, o_ref, sb_ref):
        sb_ref[...] = pltpu.roll(s_ref[...], 64, 2)
        a1v = a1_ref[...]
        a2v = a2_ref[...]

        def step(c, carry):
            x, xb = carry
            o_ref[c] = x
            return a1v * x + a2v * xb + s_ref[c], a1v * xb - a2v * x + sb_ref[c]

        zero = jnp.zeros((_SCAN_G, 128), F32)
        lax.fori_loop(0, nc, step, (zero, zero), unroll=_SCAN_UNROLL)

    blk = pl.BlockSpec((nc, _SCAN_G, 128), lambda g: (0, g, 0))
    vec = pl.BlockSpec((_SCAN_G, 128), lambda g: (g, 0))
    return _pcall(body, name=name, grid=(G // _SCAN_G,), in_specs=[blk, vec, vec], out_specs=blk,
                  out_shape=_sds((nc, G, 128), F32), scratch=[pltpu.VMEM((nc, _SCAN_G, 128), F32)],
                  sem=("parallel",))(s_t, a1, a2)


def _s5_scan_bwd(dxp_t, xp_t, a1, a2, name):
    nc, G, _ = dxp_t.shape

    def body(dx_ref, x_ref, a1_ref, a2_ref, ds_ref, da1_ref, da2_ref, dxb_ref, xb_ref):
        dxb_ref[...] = pltpu.roll(dx_ref[...], 64, 2)
        xb_ref[...] = pltpu.roll(x_ref[...], 64, 2)
        a1v = a1_ref[...]
        a2v = a2_ref[...]
        zero = jnp.zeros((_SCAN_G, 128), F32)

        def step(n, carry):
            gc, gb, d1, d2 = carry
            c = nc - 1 - n
            ds_ref[c] = gc
            d1 = d1 + gc * x_ref[c]
            d2 = d2 + gc * xb_ref[c]
            return dx_ref[c] + a1v * gc - a2v * gb, dxb_ref[c] + a1v * gb + a2v * gc, d1, d2

        _, _, d1, d2 = lax.fori_loop(0, nc, step, (zero, zero, zero, zero), unroll=_SCAN_UNROLL)
        da1_ref[...] = d1
        da2_ref[...] = d2

    blk = pl.BlockSpec((nc, _SCAN_G, 128), lambda g: (0, g, 0))
    vec = pl.BlockSpec((_SCAN_G, 128), lambda g: (g, 0))
    return _pcall(body, name=name, grid=(G // _SCAN_G,), in_specs=[blk, blk, vec, vec],
                  out_specs=[blk, vec, vec],
                  out_shape=[_sds((nc, G, 128), F32), _sds((G, 128), F32), _sds((G, 128), F32)],
                  scratch=[pltpu.VMEM((nc, _SCAN_G, 128), F32)] * 2, sem=("parallel",))(dxp_t, xp_t, a1, a2)


def _toeplitz_rows(r0, mt):
    lane = lax.broadcasted_iota(jnp.int32, (S5_GROUP, S5_LW), 1)
    mt[0:S5_GROUP, :] = r0
    for s in range(1, S5_LC):
        mt[s * S5_GROUP:(s + 1) * S5_GROUP, :] = jnp.where(lane >= s * S5_GROUP, pltpu.roll(r0, s * S5_GROUP, 1), 0.0)


def _toeplitz_rows_t(dmt):
    lane = lax.broadcasted_iota(jnp.int32, (S5_GROUP, S5_LW), 1)
    acc = dmt[0:S5_GROUP, :]
    for s in range(1, S5_LC):
        blk = dmt[s * S5_GROUP:(s + 1) * S5_GROUP, :]
        acc = acc + jnp.where(lane < S5_LW - s * S5_GROUP, pltpu.roll(blk, S5_LW - s * S5_GROUP, 1), 0.0)
    return acc


def _s5_output(umat, xprev, r0, qt, name):
    G, nc, _ = umat.shape

    def body(u_ref, x_ref, r_ref, q_ref, o_ref, mt):
        _toeplitz_rows(r_ref[...], mt)
        o_ref[...] = (_mm3(_dot, _split(u_ref[...]), _split(mt[...]))
                      + _mm3(_dot, _split(x_ref[...]), _split(q_ref[...])))

    return _pcall(body, name=name, grid=(G,),
                  in_specs=[_gspec(nc, S5_LW), _gspec(nc, 128), _gspec(S5_GROUP, S5_LW), _gspec(128, S5_LW)],
                  out_specs=_gspec(nc, S5_LW), out_shape=_sds((G, nc, S5_LW), F32),
                  scratch=[pltpu.VMEM((S5_LW, S5_LW), F32)], sem=("parallel",))(umat, xprev, r0, qt)


def _s5_bwd_state(dymat, qt, name):
    G, nc, _ = dymat.shape

    def body(d_ref, q_ref, o_ref):
        o_ref[...] = _mm3(_dot_nt, _split(d_ref[...]), _split(q_ref[...]))

    return _pcall(body, name=name, grid=(G,), in_specs=[_gspec(nc, S5_LW), _gspec(128, S5_LW)],
                  out_specs=_gspec(nc, 128), out_shape=_sds((G, nc, 128), F32), sem=("parallel",))(dymat, qt)


def _s5_bwd_main(umat, xprev, dymat, ds, r0, pm, name):
    G, nc, _ = umat.shape

    def body(u_ref, x_ref, dy_ref, ds_ref, r_ref, p_ref, du_ref, dr_ref, dq_ref, dp_ref, mt):
        u = _split(u_ref[...])
        dy = _split(dy_ref[...])
        dsv = _split(ds_ref[...])
        _toeplitz_rows(r_ref[...], mt)
        du_ref[...] = _mm3(_dot_nt, dy, _split(mt[...])) + _mm3(_dot_nt, dsv, _split(p_ref[...]))
        dq_ref[...] = _mm3(_dot_tn, _split(x_ref[...]), dy)
        dp_ref[...] = _mm3(_dot_tn, u, dsv)
        mt[...] = _mm3(_dot_tn, u, dy)
        dr_ref[...] = _toeplitz_rows_t(mt)

    return _pcall(body, name=name, grid=(G,),
                  in_specs=[_gspec(nc, S5_LW), _gspec(nc, 128), _gspec(nc, S5_LW), _gspec(nc, 128),
                            _gspec(S5_GROUP, S5_LW), _gspec(S5_LW, 128)],
                  out_specs=[_gspec(nc, S5_LW), _gspec(S5_GROUP, S5_LW), _gspec(128, S5_LW), _gspec(S5_LW, 128)],
                  out_shape=[_sds((G, nc, S5_LW), F32), _sds((G, S5_GROUP, S5_LW), F32), _sds((G, 128, S5_LW), F32),
                             _sds((G, S5_LW, 128), F32)],
                  scratch=[pltpu.VMEM((S5_LW, S5_LW), F32)], sem=("parallel",))(umat, xprev, dymat, ds, r0, pm)


def _glu_fwd(y, wglu, bglu, l, name):
    T = y.shape[0]

    def body(y_ref, w_ref, b_ref, o_ref, g_ref):
        g = _gelu(y_ref[...])
        gh = g.astype(BF16)
        z = _dot(gh, w_ref[...]) + b_ref[...]
        o_ref[...] = (g * _sigmoid(z)).astype(BF16)
        g_ref[...] = gh

    return _pcall(body, name=name, grid=(T // TM,),
                  in_specs=[_rows(TM, MIX), _layer((MIX, MIX), l), _layer((1, MIX), l)],
                  out_specs=[_rows(TM, MIX), _rows(TM, MIX)],
                  out_shape=[_sds((T, MIX), BF16), _sds((T, MIX), BF16)], sem=("parallel",))(y, wglu, bglu)


def _glu_bwd(y, dout, wglu, bglu, l, name):
    T = y.shape[0]

    def body(y_ref, do_ref, w_ref, b_ref, dy_ref, dz_ref, db_ref):
        yv = y_ref[...]
        do = do_ref[...]
        g = _gelu(yv)
        s = _sigmoid(_dot(g.astype(BF16), w_ref[...]) + b_ref[...])
        dz = do * g * s * (1.0 - s)
        dzh = dz.astype(BF16)
        dz_ref[...] = dzh
        dg = do * s + _dot_nt(dzh, w_ref[...])
        dy_ref[...] = dg * _gelu_grad(yv)

        @pl.when(pl.program_id(0) == 0)
        def _():
            db_ref[...] = jnp.zeros_like(db_ref)

        db_ref[...] += _colsum(dz)

    return _pcall(body, name=name, grid=(T // TM,),
                  in_specs=[_rows(TM, MIX), _rows(TM, MIX), _layer((MIX, MIX), l), _layer((1, MIX), l)],
                  out_specs=[_rows(TM, MIX), _rows(TM, MIX), _full((1, MIX))],
                  out_shape=[_sds((T, MIX), F32), _sds((T, MIX), BF16), _sds((1, MIX), F32)])(y, dout, wglu, bglu)


def _prev_rows(T, h, s=0):
    return pl.BlockSpec((h, MIX), lambda i: (jnp.maximum(i * (TM // h) - 1, 0), s))


def _next_rows(T, h, s=0):
    return pl.BlockSpec((h, MIX), lambda i: (jnp.minimum((i + 1) * (TM // h), T // h - 1), s))


def _conv_fwd(h, w, l, name):
    T = h.shape[0]

    def body(b_ref, c_ref, x_ref, ch_ref, xh_ref, w_ref, o_ref, ext):
        i = pl.program_id(0)
        z = c_ref[...] * x_ref[...]
        ext[0:8, :] = jnp.where(i > 0, ch_ref[...] * xh_ref[...], 0.0)
        ext[8:, :] = z
        y = (w_ref[0:1, :] * ext[pl.ds(6, TM), :] + w_ref[1:2, :] * ext[pl.ds(7, TM), :] + w_ref[2:3, :] * z)
        o_ref[...] = (b_ref[...] * y).astype(BF16)

    return _pcall(body, name=name, grid=(T // TM,),
                  in_specs=[_cols(TM, 1), _cols(TM, 2), _cols(TM, 3), _prev_rows(T, 8, 2), _prev_rows(T, 8, 3),
                            _layer((3, MIX), l)],
                  out_specs=_rows(TM, MIX), out_shape=_sds((T, MIX), BF16),
                  scratch=[pltpu.VMEM((TM + 8, MIX), F32)], sem=("parallel",))(h, h, h, h, h, w)


def _conv_bwd(dout, h, dua, w, l, name):
    T = h.shape[0]
    nb = T // TM

    def body(do_ref, b_ref, c_ref, x_ref, ch_ref, xh_ref, don_ref, bn_ref, du_ref, w_ref,
             dh_ref, dw_ref, ext, ext2):
        i = pl.program_id(0)
        c = c_ref[...]
        x = x_ref[...]
        z = c * x
        ext[0:8, :] = jnp.where(i > 0, ch_ref[...] * xh_ref[...], 0.0)
        ext[8:, :] = z
        zm2 = ext[pl.ds(6, TM), :]
        zm1 = ext[pl.ds(7, TM), :]
        w0 = w_ref[0:1, :]
        w1 = w_ref[1:2, :]
        w2 = w_ref[2:3, :]
        y = w0 * zm2 + w1 * zm1 + w2 * z
        do = do_ref[...]
        dy = do * b_ref[...]
        ext2[0:TM, :] = dy
        ext2[TM:, :] = jnp.where(i < nb - 1, don_ref[...] * bn_ref[...], 0.0)
        dz = w2 * dy + w1 * ext2[pl.ds(1, TM), :] + w0 * ext2[pl.ds(2, TM), :]
        dh_ref[:, 0:MIX] = du_ref[...].astype(BF16)
        dh_ref[:, MIX:2 * MIX] = (do * y).astype(BF16)
        dh_ref[:, 2 * MIX:3 * MIX] = (dz * x).astype(BF16)
        dh_ref[:, 3 * MIX:] = (dz * c).astype(BF16)

        @pl.when(i == 0)
        def _():
            dw_ref[...] = jnp.zeros_like(dw_ref)

        dw_ref[0:1, :] += _colsum(dy * zm2)
        dw_ref[1:2, :] += _colsum(dy * zm1)
        dw_ref[2:3, :] += _colsum(dy * z)

    return _pcall(body, name=name, grid=(nb,),
                  in_specs=[_rows(TM, MIX), _cols(TM, 1), _cols(TM, 2), _cols(TM, 3), _prev_rows(T, 8, 2),
                            _prev_rows(T, 8, 3), _next_rows(T, 8), _next_rows(T, 8, 1), _rows(TM, MIX),
                            _layer((3, MIX), l)],
                  out_specs=[_rows(TM, 4 * MIX), _full((8, MIX))],
                  out_shape=[_sds((T, 4 * MIX), BF16), _sds((8, MIX), F32)],
                  scratch=[pltpu.VMEM((TM + 8, MIX), F32), pltpu.VMEM((TM + 8, MIX), F32)])(
                      dout, h, h, h, h, h, dout, h, dua, w)


_PH = 16


def _pooled(ext, t, gi, w):
    lo = gi * POOL_GROUP
    cur = ext[pl.ds(_PH, TM), lo:lo + POOL_GROUP]
    acc = cur
    for k in range(1, w):
        acc = acc + ext[pl.ds(_PH - k, TM), lo:lo + POOL_GROUP]
    cnt = jnp.minimum(t + 1, w).astype(F32)
    return acc / cnt - cur, cnt


def _pool_fwd(h, pw, scale, l, name):
    T = h.shape[0]

    def body(z_ref, zh_ref, pw_ref, sc_ref, o_ref, ext):
        i = pl.program_id(0)
        ext[0:_PH, :] = jnp.where(i > 0, zh_ref[...], 0.0)
        ext[_PH:, :] = z_ref[...]
        t = i * TM + lax.broadcasted_iota(jnp.int32, (TM, 1), 0)
        for gi, w in enumerate(POOL_WINDOWS):
            lo = gi * POOL_GROUP
            pooled, _ = _pooled(ext, t, gi, w)
            mixed = _dot(pooled.astype(BF16), pw_ref[gi].astype(BF16))
            o_ref[:, lo:lo + POOL_GROUP] = (mixed * sc_ref[:, lo:lo + POOL_GROUP]).astype(BF16)

    return _pcall(body, name=name, grid=(T // TM,),
                  in_specs=[_cols(TM, 3), _prev_rows(T, _PH, 3), _layer((4, POOL_GROUP, POOL_GROUP), l),
                            _layer((1, MIX), l)],
                  out_specs=_rows(TM, MIX), out_shape=_sds((T, MIX), BF16),
                  scratch=[pltpu.VMEM((TM + _PH, MIX), F32)], sem=("parallel",))(h, h, pw, scale)


def _pool_bwd(dout, h, dq, dk, dv, pw, scale, l, name):
    T = h.shape[0]
    nb = T // TM

    def body(do_ref, don_ref, z_ref, zh_ref, dq_ref, dk_ref, dv_ref, pw_ref, sc_ref,
             dh_ref, dpw_ref, dsc_ref, ext, ext2):
        i = pl.program_id(0)
        ext[0:_PH, :] = jnp.where(i > 0, zh_ref[...], 0.0)
        ext[_PH:, :] = z_ref[...]
        t = i * TM + lax.broadcasted_iota(jnp.int32, (TM, 1), 0)
        dh_ref[:, 0:MIX] = dq_ref[...]
        dh_ref[:, MIX:2 * MIX] = dk_ref[...]
        dh_ref[:, 2 * MIX:3 * MIX] = dv_ref[...]

        @pl.when(i == 0)
        def _():
            dpw_ref[...] = jnp.zeros_like(dpw_ref)
            dsc_ref[...] = jnp.zeros_like(dsc_ref)

        for gi, w in enumerate(POOL_WINDOWS):
            lo = gi * POOL_GROUP
            pwb = pw_ref[gi].astype(BF16)
            sc = sc_ref[:, lo:lo + POOL_GROUP]
            pooled, cnt = _pooled(ext, t, gi, w)
            pb = pooled.astype(BF16)
            mixed = _dot(pb, pwb)
            dog = do_ref[:, lo:lo + POOL_GROUP]
            dsc_ref[:, lo:lo + POOL_GROUP] += _colsum(dog * mixed)
            dmix = (dog * sc).astype(BF16)
            dpw_ref[gi] += _dot_tn(pb, dmix)
            dpool = _dot_nt(dmix, pwb)
            dmix_n = (jnp.where(i < nb - 1, don_ref[:, lo:lo + POOL_GROUP], 0.0) * sc).astype(BF16)
            dpool_n = _dot_nt(dmix_n, pwb)
            e = dpool / cnt
            ext2[0:TM, lo:lo + POOL_GROUP] = e
            ext2[TM:, lo:lo + POOL_GROUP] = dpool_n * (1.0 / w)
            s = e
            for k in range(1, w):
                s = s + ext2[pl.ds(k, TM), lo:lo + POOL_GROUP]
            dh_ref[:, 3 * MIX + lo:3 * MIX + lo + POOL_GROUP] = (s - dpool).astype(BF16)

    blk = _rows(TM, MIX)
    return _pcall(body, name=name, grid=(nb,),
                  in_specs=[blk, _next_rows(T, _PH), _cols(TM, 3), _prev_rows(T, _PH, 3), blk, blk, blk,
                            _layer((4, POOL_GROUP, POOL_GROUP), l), _layer((1, MIX), l)],
                  out_specs=[_rows(TM, 4 * MIX), _full((4, POOL_GROUP, POOL_GROUP)), _full((1, MIX))],
                  out_shape=[_sds((T, 4 * MIX), BF16), _sds((4, POOL_GROUP, POOL_GROUP), F32), _sds((1, MIX), F32)],
                  scratch=[pltpu.VMEM((TM + _PH, MIX), F32), pltpu.VMEM((TM + _PH, MIX), F32)])(
                      dout, dout, h, h, dq, dk, dv, pw, scale)


def _att_bias_table(rel_bias):
    span = LEFT_CHUNKS * CHUNK
    assert ATT_TQ - 1 <= MAX_REL
    lo = MAX_REL - (ATT_TQ - 1)
    near = rel_bias[:, lo:2 * MAX_REL + 1]
    far = jnp.broadcast_to(rel_bias[:, 2 * MAX_REL:], (HEADS, span + ATT_TQ - 1 - MAX_REL))
    by_dist = jnp.concatenate([near, far], axis=1)
    rev = by_dist[:, ::-1]
    lv = rev.shape[1]
    skew = jnp.tile(rev, (1, ATT_TQ + 1))[:, :ATT_TQ * (lv + 1)].reshape(HEADS, ATT_TQ, lv + 1)[:, :, :ATT_W]
    bias = skew[:, ::-1, :]
    r = jnp.arange(ATT_TQ)[:, None]
    col = jnp.arange(ATT_W)[None, :]
    dchunk = (LEFT_CHUNKS + r // CHUNK) - col // CHUNK
    visible = (dchunk >= 0) & (dchunk <= LEFT_CHUNKS)
    return jnp.where(visible[None], bias, NEG_INF)


def _qrows(n, s=0):
    return pl.BlockSpec((ATT_TQ, n), lambda i: (i, s))


def _att_views_back(d, s):
    return pl.BlockSpec((ATT_TQ, MIX), lambda i: (jnp.maximum(i - (ATT_NV - 1) + d, 0), s))


_PAIR = 2 * HEAD_DIM


def _att_pair(refs, pp):
    sl = slice(pp * _PAIR, (pp + 1) * _PAIR)
    if isinstance(refs, (tuple, list)):
        return jnp.concatenate([r[:, sl] for r in refs], axis=0)
    return refs[:, sl]


def _att_fwd(h, table, name):
    T = h.shape[0]

    def body(q_ref, *refs):
        k_refs = refs[:ATT_NV]
        v_refs = refs[ATT_NV:2 * ATT_NV]
        tb_ref = refs[2 * ATT_NV]
        o_ref, oh_ref, lse_ref = refs[2 * ATT_NV + 1:]
        i = pl.program_id(0)
        col = lax.broadcasted_iota(jnp.int32, (1, ATT_W), 1)
        kvalid = (col + (i - (ATT_NV - 1)) * ATT_TQ) >= 0
        first = lax.broadcasted_iota(jnp.int32, (1, _PAIR), 1) < HEAD_DIM
        lses = []
        for pp in range(HEADS // 2):
            qp = _att_pair(q_ref, pp) * (HEAD_DIM ** -0.5)
            kp = _att_pair(k_refs, pp).astype(BF16)
            vp = _att_pair(v_refs, pp).astype(BF16)
            outs = []
            for e in range(2):
                half = first if e == 0 else jnp.logical_not(first)
                s = _dot_nt(jnp.where(half, qp, 0.0).astype(BF16), kp) + tb_ref[2 * pp + e]
                s = jnp.where(kvalid, s, NEG_INF)
                m = jnp.max(s, axis=-1, keepdims=True)
                p = jnp.exp(s - m)
                l = jnp.sum(p, axis=-1, keepdims=True)
                outs.append(_dot(p.astype(BF16), vp) / l)
                lses.append(m + jnp.log(l))
            o = jnp.where(first, outs[0], outs[1])
            o_ref[:, pp * _PAIR:(pp + 1) * _PAIR] = o
            oh_ref[:, pp * _PAIR:(pp + 1) * _PAIR] = o.astype(BF16)
        lse_ref[...] = jnp.concatenate(lses, axis=1)

    kviews = [_att_views_back(d, 1) for d in range(ATT_NV)]
    vviews = [_att_views_back(d, 2) for d in range(ATT_NV)]
    return _pcall(body, name=name, grid=(T // ATT_TQ,),
                  in_specs=[_qrows(MIX)] + kviews + vviews + [_full((HEADS, ATT_TQ, ATT_W))],
                  out_specs=[_qrows(MIX), _qrows(MIX), _qrows(HEADS)],
                  out_shape=[_sds((T, MIX), F32), _sds((T, MIX), BF16), _sds((T, HEADS), F32)],
                  sem=("parallel",))(h, *([h] * (2 * ATT_NV)), table)


def _att_bwd_q(h, do, o, lse, table, name):
    T = h.shape[0]

    def body(q_ref, *refs):
        k_refs = refs[:ATT_NV]
        v_refs = refs[ATT_NV:2 * ATT_NV]
        do_ref, o_ref, lse_ref, tb_ref, dq_ref, dl_ref, dtb_ref = refs[2 * ATT_NV:]
        i = pl.program_id(0)
        col = lax.broadcasted_iota(jnp.int32, (1, ATT_W), 1)
        kvalid = (col + (i - (ATT_NV - 1)) * ATT_TQ) >= 0

        first = lax.broadcasted_iota(jnp.int32, (1, _PAIR), 1) < HEAD_DIM

        @pl.when(i == 0)
        def _():
            dtb_ref[...] = jnp.zeros_like(dtb_ref)

        deltas = []
        for pp in range(HEADS // 2):
            qp = _att_pair(q_ref, pp) * (HEAD_DIM ** -0.5)
            kp = _att_pair(k_refs, pp).astype(BF16)
            vp = _att_pair(v_refs, pp).astype(BF16)
            dop = _att_pair(do_ref, pp)
            doo = dop * _att_pair(o_ref, pp)
            outs = []
            for e in range(2):
                hd = 2 * pp + e
                half = first if e == 0 else jnp.logical_not(first)
                s = _dot_nt(jnp.where(half, qp, 0.0).astype(BF16), kp) + tb_ref[hd]
                s = jnp.where(kvalid, s, NEG_INF)
                p = jnp.exp(s - lse_ref[:, hd:hd + 1])
                delta = jnp.sum(jnp.where(half, doo, 0.0), axis=-1, keepdims=True)
                dp = _dot_nt(jnp.where(half, dop, 0.0).astype(BF16), vp)
                ds = p * (dp - delta)
                dtb_ref[hd] += ds
                outs.append(_dot(ds.astype(BF16), kp))
                deltas.append(delta)
            dq = jnp.where(first, outs[0], outs[1]) * (HEAD_DIM ** -0.5)
            dq_ref[:, pp * _PAIR:(pp + 1) * _PAIR] = dq.astype(BF16)
        dl_ref[...] = jnp.concatenate(deltas, axis=1)

    kviews = [_att_views_back(d, 1) for d in range(ATT_NV)]
    vviews = [_att_views_back(d, 2) for d in range(ATT_NV)]
    tb = _full((HEADS, ATT_TQ, ATT_W))
    return _pcall(body, name=name, grid=(T // ATT_TQ,),
                  in_specs=[_qrows(MIX)] + kviews + vviews + [_qrows(MIX), _qrows(MIX), _qrows(HEADS), tb],
                  out_specs=[_qrows(MIX), _qrows(HEADS), tb],
                  out_shape=[_sds((T, MIX), BF16), _sds((T, HEADS), F32), _sds((HEADS, ATT_TQ, ATT_W), F32)])(
                      h, *([h] * (2 * ATT_NV)), do, o, lse, table)


def _att_table_by_key(table):
    t = table.reshape(HEADS, ATT_TQ, ATT_NV, ATT_TQ)[:, :, ::-1, :]
    return jnp.transpose(t, (0, 3, 2, 1)).reshape(HEADS, ATT_TQ, ATT_W)


def _att_bwd_kv(h, do, lse_t, delta_t, table_k, name):
    T = h.shape[0]
    nb = T // ATT_TQ

    def fwd_view(d, s=0):
        return pl.BlockSpec((ATT_TQ, MIX), lambda j: (jnp.minimum(j + d, nb - 1), s))

    def row_view(d):
        return pl.BlockSpec((HEADS, ATT_TQ), lambda j: (0, jnp.minimum(j + d, nb - 1)))

    def body(k_ref, v_ref, *refs):
        q_refs = refs[:ATT_NV]
        do_refs = refs[ATT_NV:2 * ATT_NV]
        lse_refs = refs[2 * ATT_NV:3 * ATT_NV]
        dl_refs = refs[3 * ATT_NV:4 * ATT_NV]
        tb_ref, dk_ref, dv_ref = refs[4 * ATT_NV:]
        j = pl.program_id(0)
        view = lax.broadcasted_iota(jnp.int32, (1, ATT_W), 1) // ATT_TQ
        valid = (j + view) <= nb - 1
        first = lax.broadcasted_iota(jnp.int32, (1, _PAIR), 1) < HEAD_DIM
        for pp in range(HEADS // 2):
            kp = _att_pair(k_ref, pp)
            vp = _att_pair(v_ref, pp)
            qs = (_att_pair(q_refs, pp) * (HEAD_DIM ** -0.5)).astype(BF16)
            dos = _att_pair(do_refs, pp).astype(BF16)
            dks, dvs = [], []
            for e in range(2):
                hd = 2 * pp + e
                half = first if e == 0 else jnp.logical_not(first)
                lses = jnp.concatenate([r[hd:hd + 1, :] for r in lse_refs], axis=1)
                dls = jnp.concatenate([r[hd:hd + 1, :] for r in dl_refs], axis=1)
                st = _dot_nt(jnp.where(half, kp, 0.0).astype(BF16), qs) + tb_ref[hd]
                pt = jnp.where(valid, jnp.exp(st - lses), 0.0)
                dvs.append(_dot(pt.astype(BF16), dos))
                dst = pt * (_dot_nt(jnp.where(half, vp, 0.0).astype(BF16), dos) - dls)
                dks.append(_dot(dst.astype(BF16), qs))
            dk_ref[:, pp * _PAIR:(pp + 1) * _PAIR] = jnp.where(first, dks[0], dks[1]).astype(BF16)
            dv_ref[:, pp * _PAIR:(pp + 1) * _PAIR] = jnp.where(first, dvs[0], dvs[1]).astype(BF16)

    qv = [fwd_view(d, 0) for d in range(ATT_NV)]
    dov = [fwd_view(d) for d in range(ATT_NV)]
    rows = [row_view(d) for d in range(ATT_NV)]
    return _pcall(body, name=name, grid=(nb,),
                  in_specs=[_qrows(MIX, 1), _qrows(MIX, 2)] + qv + dov + rows + rows
                  + [_full((HEADS, ATT_TQ, ATT_W))],
                  out_specs=[_qrows(MIX), _qrows(MIX)], out_shape=[_sds((T, MIX), BF16), _sds((T, MIX), BF16)],
                  sem=("parallel",))(h, h, *([h] * ATT_NV), *([do] * ATT_NV), *([lse_t] * ATT_NV),
                                     *([delta_t] * ATT_NV), table_k)


def _all_gather(x, name):
    R, C = x.shape

    def body(x_ref, out_ref, send_sems, recv_sems, local_sem):
        xi, yi, ci = lax.axis_index("x"), lax.axis_index("y"), lax.axis_index("c")
        me, sibling = (xi, yi, ci), (xi, yi, 1 - ci)
        chips = [(1 - xi, yi), (xi, 1 - yi), (1 - xi, 1 - yi)]

        def slot(px, py, pc):
            return out_ref.at[4 * px + 2 * py + pc]

        def copy(k, block, to, src=None):
            return pltpu.make_async_remote_copy(
                src_ref=slot(*block) if src is None else src, dst_ref=slot(*block),
                send_sem=send_sems.at[k], recv_sem=recv_sems.at[k], device_id=to, device_id_type=_MESH)

        mine = pltpu.make_async_copy(x_ref, slot(*me), local_sem)
        mine.start()
        first = [copy(0, me, sibling, src=x_ref)]
        first += [copy(1 + j, me, (*chip, ci), src=x_ref) for j, chip in enumerate(chips)]
        for cp in first:
            cp.start()
        passed = [copy(4 + j, (*chip, ci), sibling) for j, chip in enumerate(chips)]
        for j, chip in enumerate(chips):
            copy(1 + j, (*chip, ci), me).wait_recv()
            passed[j].start()
        copy(0, sibling, me).wait_recv()
        for j, chip in enumerate(chips):
            copy(4 + j, (*chip, 1 - ci), me).wait_recv()
        for cp in first + passed:
            cp.wait_send()
        mine.wait()

    return pl.pallas_call(
        body, name=name, out_shape=_sds((N_DEV, R, C), x.dtype), in_specs=[_ANY], out_specs=_ANY,
        scratch_shapes=[pltpu.SemaphoreType.DMA((7,)), pltpu.SemaphoreType.DMA((7,)), pltpu.SemaphoreType.DMA(())],
    )(x)


def _a2a_copies(s_ref, r_ref, send_sems, recv_sems, local_sem, gather=False):
    xi, yi, ci = lax.axis_index("x"), lax.axis_index("y"), lax.axis_index("c")
    me = 4 * xi + 2 * yi + ci

    def mine():
        return pltpu.make_async_copy(s_ref if gather else s_ref.at[me], r_ref.at[me], local_sem)

    def remote(m, sending):
        px = 1 - xi if m & 4 else xi
        py = 1 - yi if m & 2 else yi
        pc = 1 - ci if m & 1 else ci
        peer = 4 * px + 2 * py + pc
        src, dst = (s_ref.at[peer], r_ref.at[me]) if sending else (s_ref.at[me], r_ref.at[peer])
        if gather:
            src = s_ref
        return pltpu.make_async_remote_copy(src_ref=src, dst_ref=dst, send_sem=send_sems.at[m - 1],
                                            recv_sem=recv_sems.at[m - 1], device_id=(px, py, pc), device_id_type=_MESH)

    def start():
        mine().start()
        for m in range(1, N_DEV):
            remote(m, True).start()

    def wait():
        for m in range(1, N_DEV):
            remote(m, False).wait_recv()
        for m in range(1, N_DEV):
            remote(m, True).wait_send()
        mine().wait()

    return start, wait


_A2A_SEMS = [pltpu.SemaphoreType.DMA((7,)), pltpu.SemaphoreType.DMA((7,)), pltpu.SemaphoreType.DMA(())]


def _all_to_all(s, name):
    def body(s_ref, r_ref, send_sems, recv_sems, local_sem):
        start, wait = _a2a_copies(s_ref, r_ref, send_sems, recv_sems, local_sem)
        start()
        wait()

    return pl.pallas_call(
        body, name=name, out_shape=_sds(s.shape, s.dtype), in_specs=[_ANY], out_specs=_ANY,
        scratch_shapes=list(_A2A_SEMS))(s)


def _riding(body, n_in, n_out, n_ex, first, last, gather=False):
    def wrapped(*refs):
        ins = refs[:n_in]
        sends = refs[n_in:n_in + n_ex]
        outs = refs[n_in + n_ex:n_in + n_ex + n_out]
        recvs = refs[n_in + n_ex + n_out:n_in + 2 * n_ex + n_out]
        rest = refs[n_in + 2 * n_ex + n_out:]
        scratch, sems = rest[:len(rest) - 3 * n_ex], rest[len(rest) - 3 * n_ex:]

        @pl.when(first())
        def _():
            for k in range(n_ex):
                _a2a_copies(sends[k], recvs[k], *sems[3 * k:3 * k + 3], gather=gather)[0]()

        body(*ins, *outs, *scratch)

        @pl.when(last())
        def _():
            for k in range(n_ex):
                _a2a_copies(sends[k], recvs[k], *sems[3 * k:3 * k + 3], gather=gather)[1]()

    return wrapped


_ADAMW_BLOCK_BYTES = 6 * 1024 * 1024


def _adamw(parts, row_off, w, m, v, name):
    R, C = w.shape
    tr = None
    for cand in (512, 256, 128, 64, 32, 16):
        step_bytes = cand * C * (N_DEV * parts.dtype.itemsize + 7 * 4)
        if R % cand == 0 and row_off % cand == 0 and step_bytes <= _ADAMW_BLOCK_BYTES:
            tr = cand
            break
    assert tr is not None, (R, C, row_off)
    off = row_off // tr
    c1 = 1.0 - ADAM_B1 ** ADAM_STEP
    c2 = 1.0 - ADAM_B2 ** ADAM_STEP

    def body(p_ref, w_ref, m_ref, v_ref, g_ref, d_ref, mo_ref, vo_ref):
        g = p_ref[0].astype(F32)
        for j in range(1, N_DEV):
            g = g + p_ref[j].astype(F32)
        mn = ADAM_B1 * m_ref[...] + (1.0 - ADAM_B1) * g
        vn = ADAM_B2 * v_ref[...] + (1.0 - ADAM_B2) * (g * g)
        m_hat = mn / c1
        v_hat = vn / c2
        g_ref[...] = g
        d_ref[...] = -ADAM_LR * (m_hat / (jnp.sqrt(v_hat) + ADAM_EPS) + ADAM_WD * w_ref[...])
        mo_ref[...] = mn
        vo_ref[...] = vn

    blk = _rows(tr, C)
    return _pcall(body, name=name, grid=(R // tr,),
                  in_specs=[pl.BlockSpec((N_DEV, tr, C), lambda i: (0, off + i, 0)), blk, blk, blk],
                  out_specs=[blk] * 4, out_shape=[_sds((R, C), F32)] * 4,
                  sem=("parallel",))(parts, w, m, v)


def _piece_rows(size):
    return -(-size // PIECE) * 16


def _pack_rows(arrays, lead=()):
    parts = []
    total = 0
    for a in arrays:
        flat = a.reshape(lead + (-1,))
        size = flat.shape[-1]
        rows = _piece_rows(size)
        pad = [(0, 0)] * len(lead) + [(0, rows * FLAT_COLS - size)]
        parts.append(jnp.pad(flat, pad).reshape(lead + (rows, FLAT_COLS)))
        total += rows
    tail = -total % FLAT_ROWS
    if tail:
        parts.append(jnp.zeros(lead + (tail, FLAT_COLS), parts[0].dtype))
    return jnp.concatenate(parts, axis=len(lead))


def _unpack_rows(buf, shapes, lead=()):
    out = []
    row = 0
    nl = len(lead)
    for shape in shapes:
        size = math.prod(shape)
        rows = _piece_rows(size)
        piece = lax.slice_in_dim(buf, row, row + rows, axis=nl).reshape(lead + (rows * FLAT_COLS,))
        out.append(lax.slice_in_dim(piece, 0, size, axis=nl).reshape(lead + tuple(shape)))
        row += rows
    return out


def _to_blocks(full, axis):
    shp = full.shape
    split = full.reshape(shp[:axis] + (N_DEV, shp[axis] // N_DEV) + shp[axis + 1:])
    return jnp.moveaxis(split, axis, 0)


def _from_blocks(blocks, axis):
    shp = blocks.shape[1:]
    moved = jnp.moveaxis(blocks, 0, axis)
    return moved.reshape(shp[:axis] + (N_DEV * shp[axis],) + shp[axis + 1:])


N_LAYERS = {n: (DEPTH // 2 if n.startswith(('ev_', 'od_')) else DEPTH) for n in WEIGHT_NAMES}
R_OUT = D_MODEL // N_DEV
R_DOWN = D_FF // N_DEV


def _layer_shards(W, i):
    j = i // 2
    ev = i % 2 == 0
    rows = [W['ev_w_out' if ev else 'od_w_out'][j], W['ffn_w_down'][i], W['ple_w_gate'][i]]
    return dict(up=W['ffn_w_up'][i].astype(BF16), inn=W['ev_w_in' if ev else 'od_w_in'][j].astype(BF16),
                rows=jnp.concatenate(rows, axis=0).astype(BF16))


def _layer_weights(up, inn, rows):
    return dict(up=up.reshape(2, FF_J, 1, D_MODEL, FF_NB), win=inn.reshape(N_DEV, 1, D_MODEL, IN_NB),
                wo=lax.slice_in_dim(rows, 0, R_OUT, axis=1).reshape(1, D_MODEL, D_MODEL),
                wd4=lax.slice_in_dim(rows, R_OUT, R_OUT + R_DOWN, axis=1).reshape(1, FF_J, FF_NB, D_MODEL),
                gate=lax.slice_in_dim(rows, R_OUT + R_DOWN, 2 * R_OUT + R_DOWN, axis=1).reshape(1, D_MODEL, D_MODEL))


def _layer_weights_of(W, i):
    j = i // 2
    ev = i % 2 == 0
    return dict(up=W['ffn_w_up'][:, i:i + 1].reshape(2, FF_J, 1, D_MODEL, FF_NB),
                win=W['ev_w_in' if ev else 'od_w_in'][:, j:j + 1],
                wo=W['ev_w_out' if ev else 'od_w_out'][j:j + 1],
                wd4=W['ffn_w_down'][i:i + 1].reshape(1, FF_J, FF_NB, D_MODEL), gate=W['ple_w_gate'][i:i + 1])


def _row_blocks(g):
    return g.reshape(N_DEV, g.shape[0] // N_DEV, g.shape[1])


def _early_sends(G, i):
    ffn = jnp.concatenate([_row_blocks(G['ffn_w_down'][i]), _row_blocks(G['ple_w_gate'][i])], axis=1)
    return dict(up=G['ffn_w_up'][i], ffn=ffn)


def _late_sends(G, i):
    j = i // 2
    ev = i % 2 == 0
    return dict(out=_row_blocks(G['ev_w_out' if ev else 'od_w_out'][j]), inn=G['ev_w_in' if ev else 'od_w_in'][j])


def _local_step(x, p, tgt, W, big=None, shards=None):
    overlap = shards is not None
    big = list(big) if overlap else [_layer_weights_of(W, i) for i in range(DEPTH)]
    ln = {n: _vecs(W[n]) for n in ('ln_mix_g', 'ln_mix_b', 'ln_ffn_g', 'ln_ffn_b', 'ple_b_gate')}
    bglu = _vecs(W['ev_b_glu'])
    pscale = _vecs(W['od_pool_scale'])
    saved = []
    x0 = x
    x0h = x.astype(BF16)
    for i in range(DEPTH):
        L = f"L{i}_"
        j = i // 2
        B = big[i]
        s = dict(x0=x0, x0h=x0h, B=B)
        h = _mm_in(x0h, B['win'], 0, L + "mm_in")
        if i % 2 == 0:
            params = tuple(W[n][j] for n in ('ev_lambda_re', 'ev_lambda_im', 'ev_log_dt', 'ev_b_re', 'ev_b_im',
                                             'ev_c_re', 'ev_c_im', 'ev_d'))
            (r0, qt, pm, a1, a2), op_vjp = jax.vjp(_s5_operators, *params)
            umat = _to_chunks(h, L + "to_chunks")
            st = jnp.transpose(_s5_chunk_state(umat, pm, L + "s5_state"), (1, 0, 2))
            xp_t = _s5_scan_fwd(st, a1, a2, L + "s5_scan")
            xprev = jnp.transpose(xp_t, (1, 0, 2))
            y = _from_chunks(_s5_output(umat, xprev, r0, qt, L + "s5_out"), L + "from_chunks")
            ya, gh = _glu_fwd(y, W['ev_w_glu'], bglu, j, L + "glu")
            yb = _conv_fwd(h, W['ev_conv_w'], j, L + "conv")
            s.update(op_vjp=op_vjp, r0=r0, qt=qt, pm=pm, a1=a1, a2=a2, umat=umat, xp_t=xp_t, xprev=xprev, y=y, gh=gh)
        else:
            table, tb_vjp = jax.vjp(_att_bias_table, W['od_rel_bias'][j])
            of, ya, lse = _att_fwd(h, table, L + "att")
            yb = _pool_fwd(h, W['od_pool_w'], pscale, j, L + "pool")
            s.update(table=table, tb_vjp=tb_vjp, of=of, lse=lse)
        g1, b1 = (ln['ln_mix_g'], i), (ln['ln_mix_b'], i)
        g2, b2 = (ln['ln_ffn_g'], i), (ln['ln_ffn_b'], i)
        xhat1, rstd1, x1h = _mm_out_ln(ya, yb, B['wo'], 0, x0, g1, b1, L + "mm_out_ln")
        nxt = shards[i + 1] if overlap and i + 1 < DEPTH else None
        (hf, a3), got_a = _mm_up(x1h, B['up'], 0, L + "mm_up", ride=(nxt['up'], nxt['inn']) if nxt else ())
        (xhat2, rstd2, x2h), got_b = _mm_down_ln(a3, B['wd4'], 0, xhat1, g1, b1, g2, b2, L + "mm_down_ln",
                                                 ride=(nxt['rows'],) if nxt else ())
        if nxt:
            big.append(_layer_weights(got_a[0], got_a[1], got_b[0]))
        x3, x3h = _mm_ple(x2h, xhat2, ln['ln_ffn_g'], ln['ln_ffn_b'], p, B['gate'], ln['ple_b_gate'],
                          W['ple_w_proj'], i, L + "mm_ple", lw=0)
        s.update(h=h, ya=ya, yb=yb, xhat1=xhat1, rstd1=rstd1, x1h=x1h, hf=hf, a3=a3, xhat2=xhat2,
                 rstd2=rstd2, x2h=x2h, g1=g1)
        saved.append(s)
        x0, x0h = x3, x3h

    dx, loss = _loss_head(x0, tgt, "loss_head")

    G = {n: [None] * N_LAYERS[n] for n in WEIGHT_NAMES}
    landed = {i: {} for i in range(DEPTH)}
    pending = None
    for i in reversed(range(DEPTH)):
        L = f"L{i}_"
        j = i // 2
        s = saved[i]
        B = s['B']
        dr2, dr2h, dpreh, dpph, dbg, dg2, db2 = _ple_bwd(
            dx, s['x2h'], p, B['gate'], ln['ple_b_gate'], W['ple_w_proj'], i, s['xhat2'], s['rstd2'],
            ln['ln_ffn_g'], L + "ple_bwd", lw=0)
        G['ple_w_gate'][i] = _mm_tn(s['x2h'], dpreh, L + "dw_gate")
        G['ple_w_proj'][i] = _mm_tn(p, dpph, L + "dw_proj", a_layer=i)
        G['ple_b_gate'][i] = dbg[0]
        G['ln_ffn_g'][i] = dg2[0]
        G['ln_ffn_b'][i] = db2[0]
        dhf, got = _ffn_bwd1(dr2h, B['wd4'], 0, s['hf'], L + "ffn_bwd1",
                             ride=(pending['out'], pending['inn']) if pending else ())
        if pending:
            landed[i + 1].update(out=got[0], inn=got[1])
        G['ffn_w_down'][i] = _mm_tn_ablk(s['a3'], dr2h, L + "dw_down").reshape(D_FF, D_MODEL)
        T = dhf.shape[2]
        G['ffn_w_up'][i] = _mm_tn_bblk(s['x1h'], dhf.reshape(N_DEV, T, FF_NB), L + "dw_up")
        early = _early_sends(G, i) if overlap else None
        (dr1, dr1h, dg1, db1), got = _ffn_bwd2(dhf, B['up'], 0, dr2, s['xhat1'], s['rstd1'], s['g1'], L + "ffn_bwd2",
                                               ride=(early['up'], early['ffn']) if overlap else ())
        if overlap:
            landed[i].update(up=got[0], ffn=got[1])
        G['ln_mix_g'][i] = dg1[0]
        G['ln_mix_b'][i] = db1[0]
        dya, dyb = _out_bwd(dr1h, B['wo'], 0, L + "out_bwd")
        dwo = jnp.concatenate([_mm_tn(s['ya'], dr1h, L + "dw_out_a"), _mm_tn(s['yb'], dr1h, L + "dw_out_b")], axis=0)
        if i % 2 == 0:
            G['ev_w_out'][j] = dwo
            dy, dzh, dbglu = _glu_bwd(s['y'], dya, W['ev_w_glu'], bglu, j, L + "glu_bwd")
            G['ev_w_glu'][j] = _mm_tn(s['gh'], dzh, L + "dw_glu")
            G['ev_b_glu'][j] = dbglu[0]
            dymat = _to_chunks(dy, L + "to_chunks_dy")
            dxp_t = jnp.transpose(_s5_bwd_state(dymat, s['qt'], L + "s5_bwd_state"), (1, 0, 2))
            ds_t, da1, da2 = _s5_scan_bwd(dxp_t, s['xp_t'], s['a1'], s['a2'], L + "s5_scan_bwd")
            dumat, dr0, dqt, dpm = _s5_bwd_main(s['umat'], s['xprev'], dymat, jnp.transpose(ds_t, (1, 0, 2)),
                                                s['r0'], s['pm'], L + "s5_bwd")
            dparams = s['op_vjp']((dr0, dqt, dpm, da1, da2))
            for n, dpar in zip(('ev_lambda_re', 'ev_lambda_im', 'ev_log_dt', 'ev_b_re', 'ev_b_im', 'ev_c_re',
                                'ev_c_im', 'ev_d'), dparams):
                G[n][j] = dpar
            dua = _from_chunks(dumat, L + "from_chunks_du")
            dh, dcw = _conv_bwd(dyb, s['h'], dua, W['ev_conv_w'], j, L + "conv_bwd")
            G['ev_conv_w'][j] = dcw[:3]
            wname = 'ev_w_in'
        else:
            G['od_w_out'][j] = dwo
            dq, delta, dtable = _att_bwd_q(s['h'], dya, s['of'], s['lse'], s['table'], L + "att_bwd_q")
            dk, dv = _att_bwd_kv(s['h'], dya, s['lse'].T, delta.T, _att_table_by_key(s['table']), L + "att_bwd_kv")
            G['od_rel_bias'][j] = s['tb_vjp'](dtable)[0]
            dh, dpw, dsc = _pool_bwd(dyb, s['h'], dq, dk, dv, W['od_pool_w'], pscale, j, L + "pool_bwd")
            G['od_pool_w'][j] = dpw
            G['od_pool_scale'][j] = dsc[0]
            wname = 'od_w_in'
        G[wname][j] = _mm_tn(s['x0h'], dh, L + "dw_in", blocked_n=IN_NB)
        dx = _in_bwd(dh, B['win'], 0, dr1, L + "in_bwd")
        pending = _late_sends(G, i) if overlap else None

    if overlap:
        return loss, dx, G, landed, pending
    return loss, dx, G


def _slab(a):
    return a.reshape(-1, a.shape[-1])


def _gather_small(W):
    full = {n: W[n] for n in REPLICATED}
    got = _all_gather(_pack_rows([W[n].astype(BF16) for n in SMALL_SHARDED]), "gather_w_small")
    shapes = [W[n].shape for n in SMALL_SHARDED]
    for n, blocks in zip(SMALL_SHARDED, _unpack_rows(got, shapes, lead=(N_DEV,))):
        full[n] = _from_blocks(blocks, SHARD_AXIS[n])
    for n in ('ev_conv_w', 'od_pool_scale'):
        full[n] = full[n].astype(F32)
    return full


def _step(x, p, tgt, W, M, V):
    shards = [_layer_shards(W, i) for i in range(DEPTH)]
    first = _layer_weights(*[_all_gather(shards[0][k], "gather_w0_" + k) for k in ('up', 'inn', 'rows')])
    loss, dx, G, landed, tail = _local_step(x[0], p[:, 0], tgt[0], _gather_small(W), big=[first], shards=shards)
    landed[0].update({k: _all_to_all(v, "scatter_g_" + k) for k, v in tail.items()})
    res = {}

    def update(parts, n):
        shape = W[n].shape
        outs = _adamw(parts, 0, _slab(W[n]), _slab(M[n]), _slab(V[n]), "adamw_" + n)
        for kind, a in zip(('grad', 'delta', 'm', 'v'), outs):
            res[kind, n] = a.reshape(shape)

    def over_layers(layers, key, lo=None, hi=None):
        got = [landed[i][key] for i in layers]
        if lo is not None:
            got = [lax.slice_in_dim(g, lo, hi, axis=1) for g in got]
        return jnp.concatenate(got, axis=1)

    even, odd, every = (0, 2), (1, 3), (0, 1, 2, 3)
    update(over_layers(even, 'out'), 'ev_w_out')
    update(over_layers(odd, 'out'), 'od_w_out')
    update(over_layers(every, 'ffn', 0, R_DOWN), 'ffn_w_down')
    update(over_layers(every, 'ffn', R_DOWN, R_DOWN + R_OUT), 'ple_w_gate')
    update(over_layers(every, 'up'), 'ffn_w_up')
    update(over_layers(even, 'inn'), 'ev_w_in')
    update(over_layers(odd, 'inn'), 'od_w_in')

    shapes = [W[n].shape for n in SMALL_SHARDED]
    send = _pack_rows([_to_blocks(jnp.stack(G[n], axis=0), SHARD_AXIS[n]).astype(BF16) for n in SMALL_SHARDED],
                      lead=(N_DEV,))
    parts = _all_to_all(send, "scatter_g_small")
    outs = _adamw(parts, 0, _pack_rows([W[n] for n in SMALL_SHARDED]), _pack_rows([M[n] for n in SMALL_SHARDED]),
                  _pack_rows([V[n] for n in SMALL_SHARDED]), "adamw_small")
    for kind, buf in zip(('grad', 'delta', 'm', 'v'), outs):
        for n, a in zip(SMALL_SHARDED, _unpack_rows(buf, shapes)):
            res[kind, n] = a

    repl_shapes = [W[n].shape for n in REPLICATED]
    small = _pack_rows([jnp.stack(G[n], axis=0) for n in REPLICATED] + [loss])
    parts = _all_gather(small, "gather_g_replicated")
    zero = jnp.zeros((1, 1), F32)
    outs = _adamw(parts, 0, _pack_rows([W[n] for n in REPLICATED] + [zero]),
                  _pack_rows([M[n] for n in REPLICATED] + [zero]),
                  _pack_rows([V[n] for n in REPLICATED] + [zero]), "adamw_replicated")
    for kind, buf in zip(('grad', 'delta', 'm', 'v'), outs):
        arrays = _unpack_rows(buf, repl_shapes + [(1, 1)])
        for n, a in zip(REPLICATED, arrays):
            res[kind, n] = a
        if kind == 'grad':
            total_loss = arrays[-1].reshape(())

    out = [total_loss, dx[None]]
    for kind in ('grad', 'delta', 'm', 'v'):
        out += [res[kind, n] for n in WEIGHT_NAMES]
    return tuple(out)


def kernel(x, p, ev_w_in, ev_lambda_re, ev_lambda_im, ev_log_dt, ev_b_re, ev_b_im, ev_c_re, ev_c_im, ev_d, ev_w_glu, ev_b_glu, ev_conv_w, ev_w_out, od_w_in, od_rel_bias, od_pool_w, od_pool_scale, od_w_out, ln_mix_g, ln_mix_b, ln_ffn_g, ln_ffn_b, ffn_w_up, ffn_w_down, ple_w_proj, ple_w_gate, ple_b_gate, loss_target, m_ev_w_in, m_ev_lambda_re, m_ev_lambda_im, m_ev_log_dt, m_ev_b_re, m_ev_b_im, m_ev_c_re, m_ev_c_im, m_ev_d, m_ev_w_glu, m_ev_b_glu, m_ev_conv_w, m_ev_w_out, m_od_w_in, m_od_rel_bias, m_od_pool_w, m_od_pool_scale, m_od_w_out, m_ln_mix_g, m_ln_mix_b, m_ln_ffn_g, m_ln_ffn_b, m_ffn_w_up, m_ffn_w_down, m_ple_w_proj, m_ple_w_gate, m_ple_b_gate, v_ev_w_in, v_ev_lambda_re, v_ev_lambda_im, v_ev_log_dt, v_ev_b_re, v_ev_b_im, v_ev_c_re, v_ev_c_im, v_ev_d, v_ev_w_glu, v_ev_b_glu, v_ev_conv_w, v_ev_w_out, v_od_w_in, v_od_rel_bias, v_od_pool_w, v_od_pool_scale, v_od_w_out, v_ln_mix_g, v_ln_mix_b, v_ln_ffn_g, v_ln_ffn_b, v_ffn_w_up, v_ffn_w_down, v_ple_w_proj, v_ple_w_gate, v_ple_b_gate):
    given = dict(locals())
    W = {n: given[n] for n in WEIGHT_NAMES}
    M = {n: given["m_" + n] for n in WEIGHT_NAMES}
    V = {n: given["v_" + n] for n in WEIGHT_NAMES}
    return _step(x, p, loss_target, W, M, V)
```

```python
import math

import jax
import jax.numpy as jnp
from jax import lax
from jax.experimental import pallas as pl
from jax.experimental.pallas import tpu as pltpu

F32 = jnp.float32
BF16 = jnp.bfloat16
HI = lax.Precision.HIGHEST

D_MODEL = 1024
DEPTH = 4
CHUNK = 64
MIX = 512
S5_GROUP = 16
S5_GROUPS = 32
S5_STATE = 64
HEADS = 8
HEAD_DIM = 64
LEFT_CHUNKS = 8
MAX_REL = 128
POOL_WINDOWS = (2, 4, 8, 16)
POOL_GROUP = 128
D_FF = 2816
D_PLE = 256
ALPHA = (2 * DEPTH) ** 0.25
LN_EPS = 1e-5
NEG_INF = -1e30
ADAM_LR = 0.001
ADAM_B1 = 0.9
ADAM_B2 = 0.999
ADAM_EPS = 1e-08
ADAM_WD = 0.01
ADAM_STEP = 10
N_DEV = 8

WEIGHT_NAMES = ['ev_w_in', 'ev_lambda_re', 'ev_lambda_im', 'ev_log_dt', 'ev_b_re', 'ev_b_im', 'ev_c_re', 'ev_c_im',
                'ev_d', 'ev_w_glu', 'ev_b_glu', 'ev_conv_w', 'ev_w_out', 'od_w_in', 'od_rel_bias', 'od_pool_w',
                'od_pool_scale', 'od_w_out', 'ln_mix_g', 'ln_mix_b', 'ln_ffn_g', 'ln_ffn_b', 'ffn_w_up', 'ffn_w_down',
                'ple_w_proj', 'ple_w_gate', 'ple_b_gate']
SHARD_AXIS = {'ev_w_in': 2, 'ev_w_glu': 1, 'ev_conv_w': 2, 'ev_w_out': 1, 'od_w_in': 2, 'od_pool_scale': 1,
              'od_w_out': 1, 'ffn_w_up': 2, 'ffn_w_down': 1, 'ple_w_proj': 2, 'ple_w_gate': 1}
REPLICATED = [n for n in WEIGHT_NAMES if n not in SHARD_AXIS]
SMALL_SHARDED = ['ev_w_glu', 'ev_conv_w', 'od_pool_scale', 'ple_w_proj']

VMEM_LIMIT = 48 * 1024 * 1024
TM = 512
TM_FF = 1024
IN_NB = 4 * MIX // N_DEV
FF_NB = 2 * D_FF // N_DEV
FF_J = N_DEV // 2
S5_LC = 32
S5_LW = S5_LC * S5_GROUP
ATT_TQ = 128
ATT_NV = LEFT_CHUNKS * CHUNK // ATT_TQ + 1
ATT_W = ATT_NV * ATT_TQ
FLAT_COLS = 1024
FLAT_ROWS = 256
PIECE = 16 * FLAT_COLS


_MESH = pl.DeviceIdType.MESH
_ANY = pl.BlockSpec(memory_space=pl.ANY)


def _sds(shape, dt):
    return jax.ShapeDtypeStruct(shape, dt)


def _pcall(body, *, name, grid, in_specs, out_specs, out_shape, scratch=(), sem=None):
    sem = sem or ("arbitrary",) * len(grid)
    return pl.pallas_call(
        body, name=name, grid=grid, in_specs=in_specs, out_specs=out_specs, out_shape=out_shape,
        scratch_shapes=scratch,
        compiler_params=pltpu.CompilerParams(dimension_semantics=sem, vmem_limit_bytes=VMEM_LIMIT))


def _rows(tm, n):
    return pl.BlockSpec((tm, n), lambda i: (i, 0))


def _cols(tm, s):
    return pl.BlockSpec((tm, MIX), lambda i: (i, s))


def _full(shape):
    nd = len(shape)
    return pl.BlockSpec(shape, lambda *_: (0,) * nd)


def _layer(shape, l):
    nd = len(shape)
    return pl.BlockSpec((None,) + tuple(shape), lambda *_: (l,) + (0,) * nd)


def _vecs(a):
    return a.reshape(a.shape[0], 1, a.shape[1])


def _dot(a, b, precision=None):
    return jnp.dot(a, b, preferred_element_type=F32, precision=precision)


def _dot_nt(a, b, precision=None):
    return lax.dot_general(a, b, (((1,), (1,)), ((), ())), preferred_element_type=F32, precision=precision)


def _dot_tn(a, b, precision=None):
    return lax.dot_general(a, b, (((0,), (0,)), ((), ())), preferred_element_type=F32, precision=precision)


def _split(a):
    hi = a.astype(BF16)
    return hi, (a - hi.astype(F32)).astype(BF16)


def _mm3(dot, a2, b2):
    return dot(a2[0], b2[0]) + (dot(a2[0], b2[1]) + dot(a2[1], b2[0]))


def _sigmoid(x):
    return 0.5 * jnp.tanh(0.5 * x) + 0.5


_GELU_C = math.sqrt(2.0 / math.pi)


def _gelu(x):
    return 0.5 * x * (1.0 + jnp.tanh(_GELU_C * (x + 0.044715 * x * x * x)))


def _gelu_grad(x):
    t = jnp.tanh(_GELU_C * (x + 0.044715 * x * x * x))
    return 0.5 * (1.0 + t) + 0.5 * x * (1.0 - t * t) * _GELU_C * (1.0 + 3.0 * 0.044715 * x * x)


def _ln_fwd(r, g, b):
    mu = jnp.mean(r, axis=-1, keepdims=True)
    xc = r - mu
    var = jnp.mean(xc * xc, axis=-1, keepdims=True)
    rstd = lax.rsqrt(var + LN_EPS)
    xhat = xc * rstd
    return xhat, rstd, xhat * g + b


def _ln_bwd(dx, xhat, rstd, g):
    dxh = dx * g
    m1 = jnp.mean(dxh, axis=-1, keepdims=True)
    m2 = jnp.mean(dxh * xhat, axis=-1, keepdims=True)
    return rstd * (dxh - m1 - xhat * m2)


def _colsum(x):
    return jnp.sum(x, axis=0, keepdims=True)


def _mm_in(xh, win8, l, name, ride=()):
    T = xh.shape[0]
    ni = T // TM

    def body(x_ref, w_ref, h_ref):
        x = x_ref[...]
        for b in range(N_DEV):
            h_ref[:, b * IN_NB:(b + 1) * IN_NB] = _dot(x, w_ref[b])

    in_specs = [_rows(TM, D_MODEL), pl.BlockSpec((N_DEV, None, D_MODEL, IN_NB), lambda i: (0, l, 0, 0))]
    out_shape = _sds((T, 4 * MIX), F32)
    if not ride:
        return _pcall(body, name=name, grid=(ni,), in_specs=in_specs, out_specs=_rows(TM, 4 * MIX),
                      out_shape=out_shape, sem=("parallel",))(xh, win8)
    first = lambda: pl.program_id(0) == 0
    last = lambda: pl.program_id(0) == ni - 1
    outs = _pcall(_riding(body, 2, 1, len(ride), first, last, gather=True), name=name, grid=(ni,),
                  in_specs=in_specs + [_ANY] * len(ride), out_specs=[_rows(TM, 4 * MIX)] + [_ANY] * len(ride),
                  out_shape=[out_shape] + [_sds((N_DEV,) + r.shape, r.dtype) for r in ride],
                  scratch=list(_A2A_SEMS) * len(ride))(xh, win8, *ride)
    return outs[0], outs[1:]


def _tile_of(n, cap):
    best = None
    for t in range(128, min(n, cap) + 1, 128):
        if n % t == 0:
            best = t
    assert best is not None, n
    return best


def _mm_tn(a, b, name, a_layer=None, blocked_n=None):
    T, M = a.shape[-2:]
    N = b.shape[1]
    tm = _tile_of(M, 1408)
    nbs = max(1, 512 // blocked_n) if blocked_n else 1
    tn = blocked_n * nbs if blocked_n else _tile_of(N, 1024 if tm <= 512 else 512)
    tk = min(T, 1024)
    nk = T // tk

    def body(a_ref, b_ref, o_ref, acc):
        k = pl.program_id(2)

        @pl.when(k == 0)
        def _():
            acc[...] = jnp.zeros_like(acc)

        acc[...] += _dot_tn(a_ref[...].astype(BF16), b_ref[...].astype(BF16))

        @pl.when(k == nk - 1)
        def _():
            if blocked_n:
                for sb in range(nbs):
                    o_ref[sb] = acc[:, sb * blocked_n:(sb + 1) * blocked_n].astype(BF16)
            else:
                o_ref[...] = acc[...].astype(BF16)

    if a_layer is None:
        a_spec = pl.BlockSpec((tk, tm), lambda i, j, k: (k, i))
    else:
        a_spec = pl.BlockSpec((None, tk, tm), lambda i, j, k: (a_layer, k, i))
    if blocked_n:
        o_spec = pl.BlockSpec((nbs, tm, blocked_n), lambda i, j, k: (j, i, 0))
        o_shape = _sds((N // blocked_n, M, blocked_n), BF16)
    else:
        o_spec = pl.BlockSpec((tm, tn), lambda i, j, k: (i, j))
        o_shape = _sds((M, N), BF16)
    return _pcall(body, name=name, grid=(M // tm, N // tn, nk),
                  in_specs=[a_spec, pl.BlockSpec((tk, tn), lambda i, j, k: (k, j))],
                  out_specs=o_spec, out_shape=o_shape, scratch=[pltpu.VMEM((tm, tn), F32)],
                  sem=("parallel", "parallel", "arbitrary"))(a, b)


def _mm_tn_bblk(a, b3, name, per_step=2):
    T, M = a.shape
    NB, _, n = b3.shape
    tk = min(T, 1024)
    nk = T // tk

    def body(a_ref, b_ref, o_ref, acc):
        k = pl.program_id(1)

        @pl.when(k == 0)
        def _():
            acc[...] = jnp.zeros_like(acc)

        av = a_ref[...]
        for s in range(per_step):
            acc[s] += _dot_tn(av, b_ref[s])

        @pl.when(k == nk - 1)
        def _():
            o_ref[...] = acc[...].astype(BF16)

    return _pcall(body, name=name, grid=(NB // per_step, nk),
                  in_specs=[pl.BlockSpec((tk, M), lambda j, k: (k, 0)),
                            pl.BlockSpec((per_step, tk, n), lambda j, k: (j, k, 0))],
                  out_specs=pl.BlockSpec((per_step, M, n), lambda j, k: (j, 0, 0)),
                  out_shape=_sds((NB, M, n), BF16), scratch=[pltpu.VMEM((per_step, M, n), F32)],
                  sem=("parallel", "arbitrary"))(a, b3)


def _mm_tn_ablk(a3, b, name):
    NA, T, m = a3.shape
    N = b.shape[1]
    tn = _tile_of(N, 1024)
    tk = min(T, 1024)
    nk = T // tk

    def body(a_ref, b_ref, o_ref, acc):
        k = pl.program_id(2)

        @pl.when(k == 0)
        def _():
            acc[...] = jnp.zeros_like(acc)

        acc[...] += _dot_tn(a_ref[...], b_ref[...])

        @pl.when(k == nk - 1)
        def _():
            o_ref[...] = acc[...].astype(BF16)

    return _pcall(body, name=name, grid=(NA, N // tn, nk),
                  in_specs=[pl.BlockSpec((None, tk, m), lambda j, n, k: (j, k, 0)),
                            pl.BlockSpec((tk, tn), lambda j, n, k: (k, n))],
                  out_specs=pl.BlockSpec((None, m, tn), lambda j, n, k: (j, 0, n)),
                  out_shape=_sds((NA, m, N), BF16), scratch=[pltpu.VMEM((m, tn), F32)],
                  sem=("parallel", "parallel", "arbitrary"))(a3, b)


def _mm_out_ln(ya, yb, wo, l, x0, g, b, name):
    T = x0.shape[0]

    def body(ya_ref, yb_ref, w_ref, x0_ref, g_ref, b_ref, xh_ref, rs_ref, x1_ref):
        r = ALPHA * x0_ref[...] + _dot(ya_ref[...], w_ref[0:MIX, :]) + _dot(yb_ref[...], w_ref[MIX:, :])
        xhat, rstd, x1 = _ln_fwd(r, g_ref[...], b_ref[...])
        xh_ref[...] = xhat
        rs_ref[...] = rstd
        x1_ref[...] = x1.astype(BF16)

    vec = _layer((1, D_MODEL), g[1])
    return _pcall(body, name=name, grid=(T // TM,),
                  in_specs=[_rows(TM, MIX), _rows(TM, MIX), _layer((D_MODEL, D_MODEL), l), _rows(TM, D_MODEL),
                            vec, vec],
                  out_specs=[_rows(TM, D_MODEL), _rows(TM, 1), _rows(TM, D_MODEL)],
                  out_shape=[_sds((T, D_MODEL), F32), _sds((T, 1), F32), _sds((T, D_MODEL), BF16)],
                  sem=("parallel",))(ya, yb, wo, x0, g[0], b[0])


def _mm_up(x1h, wup, l, name, ride=()):
    T = x1h.shape[0]
    tm = min(T, TM_FF)
    ni = T // tm

    def body(x_ref, w_ref, hf_ref, a_ref):
        x = x_ref[...]
        g = _dot(x, w_ref[0])
        u = _dot(x, w_ref[1])
        hf_ref[0] = g.astype(BF16)
        hf_ref[1] = u.astype(BF16)
        a_ref[...] = (g * _sigmoid(g) * u).astype(BF16)

    in_specs = [pl.BlockSpec((tm, D_MODEL), lambda j, i: (i, 0)),
                pl.BlockSpec((2, None, None, D_MODEL, FF_NB), lambda j, i: (0, j, l, 0, 0))]
    out_specs = [pl.BlockSpec((2, None, tm, FF_NB), lambda j, i: (0, j, i, 0)),
                 pl.BlockSpec((None, tm, FF_NB), lambda j, i: (j, i, 0))]
    out_shape = [_sds((2, FF_J, T, FF_NB), BF16), _sds((FF_J, T, FF_NB), BF16)]
    if not ride:
        return _pcall(body, name=name, grid=(FF_J, ni), in_specs=in_specs, out_specs=out_specs,
                      out_shape=out_shape, sem=("parallel", "parallel"))(x1h, wup), ()
    first = lambda: (pl.program_id(0) == 0) & (pl.program_id(1) == 0)
    last = lambda: (pl.program_id(0) == FF_J - 1) & (pl.program_id(1) == ni - 1)
    outs = _pcall(_riding(body, 2, 2, len(ride), first, last, gather=True), name=name, grid=(FF_J, ni),
                  in_specs=in_specs + [_ANY] * len(ride), out_specs=out_specs + [_ANY] * len(ride),
                  out_shape=out_shape + [_sds((N_DEV,) + r.shape, r.dtype) for r in ride],
                  scratch=list(_A2A_SEMS) * len(ride))(x1h, wup, *ride)
    return outs[:2], outs[2:]


def _mm_down_ln(a3, wd4, l, xhat1, g1, b1, g2, b2, name, ride=()):
    T = a3.shape[1]
    ni = T // TM

    def body(a_ref, w_ref, xh1_ref, g1_ref, b1_ref, g2_ref, b2_ref, xh_ref, rs_ref, x2_ref):
        x1 = xh1_ref[...] * g1_ref[...] + b1_ref[...]
        r = ALPHA * x1
        for j in range(FF_J):
            r = r + _dot(a_ref[j], w_ref[j])
        xhat, rstd, x2 = _ln_fwd(r, g2_ref[...], b2_ref[...])
        xh_ref[...] = xhat
        rs_ref[...] = rstd
        x2_ref[...] = x2.astype(BF16)

    vec = _layer((1, D_MODEL), g1[1])
    in_specs = [pl.BlockSpec((FF_J, TM, FF_NB), lambda i: (0, i, 0)),
                _layer((FF_J, FF_NB, D_MODEL), l), _rows(TM, D_MODEL), vec, vec, vec, vec]
    out_specs = [_rows(TM, D_MODEL), _rows(TM, 1), _rows(TM, D_MODEL)]
    out_shape = [_sds((T, D_MODEL), F32), _sds((T, 1), F32), _sds((T, D_MODEL), BF16)]
    operands = (a3, wd4, xhat1, g1[0], b1[0], g2[0], b2[0])
    if not ride:
        return _pcall(body, name=name, grid=(ni,), in_specs=in_specs, out_specs=out_specs, out_shape=out_shape,
                      sem=("parallel",))(*operands), ()
    first = lambda: pl.program_id(0) == 0
    last = lambda: pl.program_id(0) == ni - 1
    outs = _pcall(_riding(body, 7, 3, len(ride), first, last, gather=True), name=name, grid=(ni,),
                  in_specs=in_specs + [_ANY] * len(ride), out_specs=out_specs + [_ANY] * len(ride),
                  out_shape=out_shape + [_sds((N_DEV,) + r.shape, r.dtype) for r in ride],
                  scratch=list(_A2A_SEMS) * len(ride))(*operands, *ride)
    return outs[:3], outs[3:]


def _mm_ple(x2h, xhat2, g2, b2, p, wg, bg, wp, l, name, lw=None):
    lw = l if lw is None else lw
    T = x2h.shape[0]

    def body(x2h_ref, xh_ref, g2_ref, b2_ref, p_ref, wg_ref, bg_ref, wp_ref, o_ref, oh_ref):
        x2 = xh_ref[...] * g2_ref[...] + b2_ref[...]
        gate = _sigmoid(_dot(x2h_ref[...], wg_ref[...]) + bg_ref[...])
        pp = _dot(p_ref[...].astype(BF16), wp_ref[...])
        x3 = x2 + gate * pp
        o_ref[...] = x3
        oh_ref[...] = x3.astype(BF16)

    vec = _layer((1, D_MODEL), l)
    return _pcall(body, name=name, grid=(T // TM,),
                  in_specs=[_rows(TM, D_MODEL), _rows(TM, D_MODEL), vec, vec,
                            pl.BlockSpec((None, TM, D_PLE), lambda i: (l, i, 0)),
                            _layer((D_MODEL, D_MODEL), lw), vec, _layer((D_PLE, D_MODEL), l)],
                  out_specs=[_rows(TM, D_MODEL), _rows(TM, D_MODEL)],
                  out_shape=[_sds((T, D_MODEL), F32), _sds((T, D_MODEL), BF16)],
                  sem=("parallel",))(x2h, xhat2, g2, b2, p, wg, bg, wp)


def _loss_head(x3, tgt, name):
    T = x3.shape[0]

    def body(x_ref, t_ref, dx_ref, l_ref):
        e = x_ref[...] - t_ref[...]
        dx_ref[...] = e * (1.0 / D_MODEL)

        @pl.when(pl.program_id(0) == 0)
        def _():
            l_ref[...] = jnp.zeros_like(l_ref)

        l_ref[...] += (0.5 / D_MODEL) * jnp.sum(e * e).reshape(1, 1)

    return _pcall(body, name=name, grid=(T // TM,),
                  in_specs=[_rows(TM, D_MODEL), _rows(TM, D_MODEL)],
                  out_specs=[_rows(TM, D_MODEL), _full((1, 1))],
                  out_shape=[_sds((T, D_MODEL), F32), _sds((1, 1), F32)])(x3, tgt)


def _ple_bwd(dx3, x2h, p, wg, bg, wp, l, xhat2, rstd2, g2, name, lw=None):
    T = dx3.shape[0]
    lw = l if lw is None else lw

    def body(dx3_ref, x2h_ref, p_ref, wg_ref, bg_ref, wp_ref, xh_ref, rs_ref, g2_ref,
             dr_ref, drh_ref, dpre_ref, dpp_ref, dbg_ref, dg_ref, db_ref):
        dx3 = dx3_ref[...]
        gate = _sigmoid(_dot(x2h_ref[...], wg_ref[...]) + bg_ref[...])
        pp = _dot(p_ref[...].astype(BF16), wp_ref[...])
        dpre = dx3 * pp * gate * (1.0 - gate)
        dpreh = dpre.astype(BF16)
        dpre_ref[...] = dpreh
        dpp_ref[...] = (dx3 * gate).astype(BF16)
        dx2 = dx3 + _dot_nt(dpreh, wg_ref[...])
        xhat = xh_ref[...]
        dr = _ln_bwd(dx2, xhat, rs_ref[...], g2_ref[...])
        dr_ref[...] = dr
        drh_ref[...] = dr.astype(BF16)

        @pl.when(pl.program_id(0) == 0)
        def _():
            dbg_ref[...] = jnp.zeros_like(dbg_ref)
            dg_ref[...] = jnp.zeros_like(dg_ref)
            db_ref[...] = jnp.zeros_like(db_ref)

        dbg_ref[...] += _colsum(dpre)
        dg_ref[...] += _colsum(dx2 * xhat)
        db_ref[...] += _colsum(dx2)

    vec = _layer((1, D_MODEL), l)
    acc = _full((1, D_MODEL))
    big = _rows(TM, D_MODEL)
    return _pcall(body, name=name, grid=(T // TM,),
                  in_specs=[big, big, pl.BlockSpec((None, TM, D_PLE), lambda i: (l, i, 0)),
                            _layer((D_MODEL, D_MODEL), lw), vec, _layer((D_PLE, D_MODEL), l),
                            big, _rows(TM, 1), vec],
                  out_specs=[big, big, big, big, acc, acc, acc],
                  out_shape=[_sds((T, D_MODEL), F32), _sds((T, D_MODEL), BF16), _sds((T, D_MODEL), BF16),
                             _sds((T, D_MODEL), BF16), _sds((1, D_MODEL), F32), _sds((1, D_MODEL), F32),
                             _sds((1, D_MODEL), F32)])(dx3, x2h, p, wg, bg, wp, xhat2, rstd2, g2)


def _ffn_bwd1(dr2h, wd4, l, hf, name, ride=()):
    T = dr2h.shape[0]
    tm = min(T, TM_FF)
    ni = T // tm

    def body(d_ref, w_ref, hf_ref, o_ref):
        da = _dot_nt(d_ref[...], w_ref[...])
        g = hf_ref[0].astype(F32)
        u = hf_ref[1].astype(F32)
        sg = _sigmoid(g)
        o_ref[0] = (da * u * (sg * (1.0 + g * (1.0 - sg)))).astype(BF16)
        o_ref[1] = (da * (g * sg)).astype(BF16)

    blk = pl.BlockSpec((2, None, tm, FF_NB), lambda j, i: (0, j, i, 0))
    in_specs = [pl.BlockSpec((tm, D_MODEL), lambda j, i: (i, 0)),
                pl.BlockSpec((None, None, FF_NB, D_MODEL), lambda j, i: (l, j, 0, 0)), blk]
    out_shape = _sds((2, FF_J, T, FF_NB), BF16)
    if not ride:
        return _pcall(body, name=name, grid=(FF_J, ni), in_specs=in_specs, out_specs=blk, out_shape=out_shape,
                      sem=("parallel", "parallel"))(dr2h, wd4, hf), ()
    first = lambda: (pl.program_id(0) == 0) & (pl.program_id(1) == 0)
    last = lambda: (pl.program_id(0) == FF_J - 1) & (pl.program_id(1) == ni - 1)
    outs = _pcall(_riding(body, 3, 1, len(ride), first, last), name=name, grid=(FF_J, ni),
                  in_specs=in_specs + [_ANY] * len(ride), out_specs=[blk] + [_ANY] * len(ride),
                  out_shape=[out_shape] + [_sds(r.shape, r.dtype) for r in ride],
                  scratch=list(_A2A_SEMS) * len(ride))(dr2h, wd4, hf, *ride)
    return outs[0], outs[1:]


_TM_B2 = 256


def _ffn_bwd2(dhf, wup, l, dr2, xhat1, rstd1, g1, name, ride=()):
    T = dr2.shape[0]
    tm = min(T, _TM_B2)
    ni = T // tm

    def body(dh_ref, w_ref, dr2_ref, xh_ref, rs_ref, g_ref, dr_ref, drh_ref, dg_ref, db_ref):
        dx1 = ALPHA * dr2_ref[...]
        for s in range(2):
            for j in range(FF_J):
                dx1 = dx1 + _dot_nt(dh_ref[s, j], w_ref[s, j])
        xhat = xh_ref[...]
        dr = _ln_bwd(dx1, xhat, rs_ref[...], g_ref[...])
        dr_ref[...] = dr
        drh_ref[...] = dr.astype(BF16)

        @pl.when(pl.program_id(0) == 0)
        def _():
            dg_ref[...] = jnp.zeros_like(dg_ref)
            db_ref[...] = jnp.zeros_like(db_ref)

        dg_ref[...] += _colsum(dx1 * xhat)
        db_ref[...] += _colsum(dx1)

    big = _rows(tm, D_MODEL)
    acc = _full((1, D_MODEL))
    in_specs = [pl.BlockSpec((2, FF_J, tm, FF_NB), lambda i: (0, 0, i, 0)),
                pl.BlockSpec((2, FF_J, None, D_MODEL, FF_NB), lambda i: (0, 0, l, 0, 0)),
                big, big, _rows(tm, 1), _layer((1, D_MODEL), g1[1])]
    out_specs = [big, big, acc, acc]
    out_shape = [_sds((T, D_MODEL), F32), _sds((T, D_MODEL), BF16), _sds((1, D_MODEL), F32),
                 _sds((1, D_MODEL), F32)]
    operands = (dhf, wup, dr2, xhat1, rstd1, g1[0])
    if not ride:
        return _pcall(body, name=name, grid=(ni,), in_specs=in_specs, out_specs=out_specs,
                      out_shape=out_shape)(*operands), ()
    first = lambda: pl.program_id(0) == 0
    last = lambda: pl.program_id(0) == ni - 1
    outs = _pcall(_riding(body, 6, 4, len(ride), first, last), name=name, grid=(ni,),
                  in_specs=in_specs + [_ANY] * len(ride), out_specs=out_specs + [_ANY] * len(ride),
                  out_shape=out_shape + [_sds(r.shape, r.dtype) for r in ride],
                  scratch=list(_A2A_SEMS) * len(ride))(*operands, *ride)
    return outs[:4], outs[4:]


def _out_bwd(dr1h, wo, l, name):
    T = dr1h.shape[0]

    def body(d_ref, w_ref, da_ref, db_ref):
        d = d_ref[...]
        da_ref[...] = _dot_nt(d, w_ref[0:MIX, :])
        db_ref[...] = _dot_nt(d, w_ref[MIX:, :])

    return _pcall(body, name=name, grid=(T // TM,),
                  in_specs=[_rows(TM, D_MODEL), _layer((D_MODEL, D_MODEL), l)],
                  out_specs=[_rows(TM, MIX), _rows(TM, MIX)],
                  out_shape=[_sds((T, MIX), F32), _sds((T, MIX), F32)],
                  sem=("parallel",))(dr1h, wo)


def _in_bwd(dh, win8, l, dr1, name):
    T = dr1.shape[0]

    def body(dh_ref, w_ref, dr_ref, o_ref):
        acc = ALPHA * dr_ref[...]
        for b in range(N_DEV):
            acc = acc + _dot_nt(dh_ref[:, b * IN_NB:(b + 1) * IN_NB], w_ref[b])
        o_ref[...] = acc

    return _pcall(body, name=name, grid=(T // TM,),
                  in_specs=[_rows(TM, 4 * MIX), pl.BlockSpec((N_DEV, None, D_MODEL, IN_NB), lambda i: (0, l, 0, 0)),
                            _rows(TM, D_MODEL)],
                  out_specs=_rows(TM, D_MODEL), out_shape=_sds((T, D_MODEL), F32),
                  sem=("parallel",))(dh, win8, dr1)


def _s5_operators(lre, lim, log_dt, bre, bim, cre, cim, dskip):
    G, P, H, LC = S5_GROUPS, S5_STATE, S5_GROUP, S5_LC
    dt = jnp.exp(log_dt)[:, None]
    mag = jnp.exp(lre * dt)
    ang = lim * dt
    lb_re = mag * jnp.cos(ang)
    lb_im = mag * jnp.sin(ang)
    den = lre * lre + lim * lim
    nr = lb_re - 1.0
    ni = lb_im
    r_re = (nr * lre + ni * lim) / den
    r_im = (ni * lre - nr * lim) / den
    bb_re = r_re[..., None] * bre - r_im[..., None] * bim
    bb_im = r_re[..., None] * bim + r_im[..., None] * bre
    k = jnp.arange(LC + 1, dtype=F32)[:, None, None]
    pmag = jnp.exp(k * (lre * dt)[None])
    pang = k * ang[None]
    pw_re = pmag * jnp.cos(pang)
    pw_im = pmag * jnp.sin(pang)
    cp_re = cre[None] * pw_re[:, :, None, :] - cim[None] * pw_im[:, :, None, :]
    cp_im = cre[None] * pw_im[:, :, None, :] + cim[None] * pw_re[:, :, None, :]
    kk = (jnp.einsum('kghp,gpj->kghj', cp_re[:LC], bb_re, precision=HI)
          - jnp.einsum('kghp,gpj->kghj', cp_im[:LC], bb_im, precision=HI))
    dmat = dskip.reshape(G, H)[:, :, None] * jnp.eye(H, dtype=F32)[None]
    kk = jnp.concatenate([kk[:1] + dmat[None], kk[1:]], axis=0)
    r0 = jnp.transpose(kk, (1, 3, 0, 2)).reshape(G, H, LC * H)
    qt = jnp.stack([cp_re[1:], -cp_im[1:]], axis=0)
    qt = jnp.transpose(qt, (2, 0, 4, 1, 3)).reshape(G, 2 * P, LC * H)
    pb_re = pw_re[:LC, :, :, None] * bb_re[None] - pw_im[:LC, :, :, None] * bb_im[None]
    pb_im = pw_re[:LC, :, :, None] * bb_im[None] + pw_im[:LC, :, :, None] * bb_re[None]
    pm = jnp.stack([pb_re[::-1], pb_im[::-1]], axis=0)
    pm = jnp.transpose(pm, (2, 1, 4, 0, 3)).reshape(G, LC * H, 2 * P)
    a_re = pw_re[LC]
    a_im = pw_im[LC]
    a1 = jnp.concatenate([a_re, a_re], axis=-1)
    a2 = jnp.concatenate([-a_im, a_im], axis=-1)
    return r0, qt, pm, a1, a2


_CHUNK_BLOCK = 128
_LANE_GROUPS = 128 // S5_GROUP


def _to_chunks(u, name):
    T = u.shape[0]
    nc = T // S5_LC
    cb = min(nc, _CHUNK_BLOCK)

    def body(x_ref, o_ref, scr):
        for s in range(S5_LC):
            xt = x_ref[pl.ds(s, cb, stride=S5_LC), :].T
            for g in range(_LANE_GROUPS):
                scr[g, s * S5_GROUP:(s + 1) * S5_GROUP, :] = xt[g * S5_GROUP:(g + 1) * S5_GROUP, :]
        for g in range(_LANE_GROUPS):
            o_ref[g] = scr[g].T

    return _pcall(body, name=name, grid=(nc // cb, S5_GROUPS // _LANE_GROUPS),
                  in_specs=[pl.BlockSpec((cb * S5_LC, 128), lambda i, k: (i, k))],
                  out_specs=pl.BlockSpec((_LANE_GROUPS, cb, S5_LW), lambda i, k: (k, i, 0)),
                  out_shape=_sds((S5_GROUPS, nc, S5_LW), F32),
                  scratch=[pltpu.VMEM((_LANE_GROUPS, S5_LW, cb), F32)], sem=("parallel", "parallel"))(u)


def _from_chunks(m, name):
    nc = m.shape[1]
    cb = min(nc, _CHUNK_BLOCK)

    def body(m_ref, y_ref, scr, rows):
        for g in range(_LANE_GROUPS):
            scr[g] = m_ref[g].T
        for s in range(S5_LC):
            for g in range(_LANE_GROUPS):
                rows[g * S5_GROUP:(g + 1) * S5_GROUP, :] = scr[g, s * S5_GROUP:(s + 1) * S5_GROUP, :]
            y_ref[pl.ds(s, cb, stride=S5_LC), :] = rows[...].T

    return _pcall(body, name=name, grid=(nc // cb, S5_GROUPS // _LANE_GROUPS),
                  in_specs=[pl.BlockSpec((_LANE_GROUPS, cb, S5_LW), lambda i, k: (k, i, 0))],
                  out_specs=pl.BlockSpec((cb * S5_LC, 128), lambda i, k: (i, k)),
                  out_shape=_sds((nc * S5_LC, MIX), F32),
                  scratch=[pltpu.VMEM((_LANE_GROUPS, S5_LW, cb), F32), pltpu.VMEM((128, cb), F32)],
                  sem=("parallel", "parallel"))(m)


def _gspec(r, c):
    return pl.BlockSpec((None, r, c), lambda g: (g, 0, 0))


def _s5_chunk_state(umat, pm, name):
    G, nc, _ = umat.shape

    def body(u_ref, p_ref, o_ref):
        o_ref[...] = _mm3(_dot, _split(u_ref[...]), _split(p_ref[...]))

    return _pcall(body, name=name, grid=(G,), in_specs=[_gspec(nc, S5_LW), _gspec(S5_LW, 128)],
                  out_specs=_gspec(nc, 128), out_shape=_sds((G, nc, 128), F32), sem=("parallel",))(umat, pm)


_SCAN_G = 8
_SCAN_UNROLL = 8


def _s5_scan_fwd(s_t, a1, a2, name):
    nc, G, _ = s_t.shape

    def body(s_ref, a1_ref, a2_ref, o_ref, sb_ref):
        sb_ref[...] = pltpu.roll(s_ref[...], 64, 2)
        a1v = a1_ref[...]
        a2v = a2_ref[...]

        def step(c, carry):
            x, xb = carry
            o_ref[c] = x
            return a1v * x + a2v * xb + s_ref[c], a1v * xb - a2v * x + sb_ref[c]

        zero = jnp.zeros((_SCAN_G, 128), F32)
        lax.fori_loop(0, nc, step, (zero, zero), unroll=_SCAN_UNROLL)

    blk = pl.BlockSpec((nc, _SCAN_G, 128), lambda g: (0, g, 0))
    vec = pl.BlockSpec((_SCAN_G, 128), lambda g: (g, 0))
    return _pcall(body, name=name, grid=(G // _SCAN_G,), in_specs=[blk, vec, vec], out_specs=blk,
                  out_shape=_sds((nc, G, 128), F32), scratch=[pltpu.VMEM((nc, _SCAN_G, 128), F32)],
                  sem=("parallel",))(s_t, a1, a2)


def _s5_scan_bwd(dxp_t, xp_t, a1, a2, name):
    nc, G, _ = dxp_t.shape

    def body(dx_ref, x_ref, a1_ref, a2_ref, ds_ref, da1_ref, da2_ref, dxb_ref, xb_ref):
        dxb_ref[...] = pltpu.roll(dx_ref[...], 64, 2)
        xb_ref[...] = pltpu.roll(x_ref[...], 64, 2)
        a1v = a1_ref[...]
        a2v = a2_ref[...]
        zero = jnp.zeros((_SCAN_G, 128), F32)

        def step(n, carry):
            gc, gb, d1, d2 = carry
            c = nc - 1 - n
            ds_ref[c] = gc
            d1 = d1 + gc * x_ref[c]
            d2 = d2 + gc * xb_ref[c]
            return dx_ref[c] + a1v * gc - a2v * gb, dxb_ref[c] + a1v * gb + a2v * gc, d1, d2

        _, _, d1, d2 = lax.fori_loop(0, nc, step, (zero, zero, zero, zero), unroll=_SCAN_UNROLL)
        da1_ref[...] = d1
        da2_ref[...] = d2

    blk = pl.BlockSpec((nc, _SCAN_G, 128), lambda g: (0, g, 0))
    vec = pl.BlockSpec((_SCAN_G, 128), lambda g: (g, 0))
    return _pcall(body, name=name, grid=(G // _SCAN_G,), in_specs=[blk, blk, vec, vec],
                  out_specs=[blk, vec, vec],
                  out_shape=[_sds((nc, G, 128), F32), _sds((G, 128), F32), _sds((G, 128), F32)],
                  scratch=[pltpu.VMEM((nc, _SCAN_G, 128), F32)] * 2, sem=("parallel",))(dxp_t, xp_t, a1, a2)


def _toeplitz_rows(r0, mt):
    lane = lax.broadcasted_iota(jnp.int32, (S5_GROUP, S5_LW), 1)
    mt[0:S5_GROUP, :] = r0
    for s in range(1, S5_LC):
        mt[s * S5_GROUP:(s + 1) * S5_GROUP, :] = jnp.where(lane >= s * S5_GROUP, pltpu.roll(r0, s * S5_GROUP, 1), 0.0)


def _toeplitz_rows_t(dmt):
    lane = lax.broadcasted_iota(jnp.int32, (S5_GROUP, S5_LW), 1)
    acc = dmt[0:S5_GROUP, :]
    for s in range(1, S5_LC):
        blk = dmt[s * S5_GROUP:(s + 1) * S5_GROUP, :]
        acc = acc + jnp.where(lane < S5_LW - s * S5_GROUP, pltpu.roll(blk, S5_LW - s * S5_GROUP, 1), 0.0)
    return acc


def _s5_output(umat, xprev, r0, qt, name):
    G, nc, _ = umat.shape

    def body(u_ref, x_ref, r_ref, q_ref, o_ref, mt):
        _toeplitz_rows(r_ref[...], mt)
        o_ref[...] = (_mm3(_dot, _split(u_ref[...]), _split(mt[...]))
                      + _mm3(_dot, _split(x_ref[...]), _split(q_ref[...])))

    return _pcall(body, name=name, grid=(G,),
                  in_specs=[_gspec(nc, S5_LW), _gspec(nc, 128), _gspec(S5_GROUP, S5_LW), _gspec(128, S5_LW)],
                  out_specs=_gspec(nc, S5_LW), out_shape=_sds((G, nc, S5_LW), F32),
                  scratch=[pltpu.VMEM((S5_LW, S5_LW), F32)], sem=("parallel",))(umat, xprev, r0, qt)


def _s5_bwd_state(dymat, qt, name):
    G, nc, _ = dymat.shape

    def body(d_ref, q_ref, o_ref):
        o_ref[...] = _mm3(_dot_nt, _split(d_ref[...]), _split(q_ref[...]))

    return _pcall(body, name=name, grid=(G,), in_specs=[_gspec(nc, S5_LW), _gspec(128, S5_LW)],
                  out_specs=_gspec(nc, 128), out_shape=_sds((G, nc, 128), F32), sem=("parallel",))(dymat, qt)


def _s5_bwd_main(umat, xprev, dymat, ds, r0, pm, name):
    G, nc, _ = umat.shape

    def body(u_ref, x_ref, dy_ref, ds_ref, r_ref, p_ref, du_ref, dr_ref, dq_ref, dp_ref, mt):
        u = _split(u_ref[...])
        dy = _split(dy_ref[...])
        dsv = _split(ds_ref[...])
        _toeplitz_rows(r_ref[...], mt)
        du_ref[...] = _mm3(_dot_nt, dy, _split(mt[...])) + _mm3(_dot_nt, dsv, _split(p_ref[...]))
        dq_ref[...] = _mm3(_dot_tn, _split(x_ref[...]), dy)
        dp_ref[...] = _mm3(_dot_tn, u, dsv)
        mt[...] = _mm3(_dot_tn, u, dy)
        dr_ref[...] = _toeplitz_rows_t(mt)

    return _pcall(body, name=name, grid=(G,),
                  in_specs=[_gspec(nc, S5_LW), _gspec(nc, 128), _gspec(nc, S5_LW), _gspec(nc, 128),
                            _gspec(S5_GROUP, S5_LW), _gspec(S5_LW, 128)],
                  out_specs=[_gspec(nc, S5_LW), _gspec(S5_GROUP, S5_LW), _gspec(128, S5_LW), _gspec(S5_LW, 128)],
                  out_shape=[_sds((G, nc, S5_LW), F32), _sds((G, S5_GROUP, S5_LW), F32), _sds((G, 128, S5_LW), F32),
                             _sds((G, S5_LW, 128), F32)],
                  scratch=[pltpu.VMEM((S5_LW, S5_LW), F32)], sem=("parallel",))(umat, xprev, dymat, ds, r0, pm)


def _glu_fwd(y, wglu, bglu, l, name):
    T = y.shape[0]

    def body(y_ref, w_ref, b_ref, o_ref, g_ref):
        g = _gelu(y_ref[...])
        gh = g.astype(BF16)
        z = _dot(gh, w_ref[...]) + b_ref[...]
        o_ref[...] = (g * _sigmoid(z)).astype(BF16)
        g_ref[...] = gh

    return _pcall(body, name=name, grid=(T // TM,),
                  in_specs=[_rows(TM, MIX), _layer((MIX, MIX), l), _layer((1, MIX), l)],
                  out_specs=[_rows(TM, MIX), _rows(TM, MIX)],
                  out_shape=[_sds((T, MIX), BF16), _sds((T, MIX), BF16)], sem=("parallel",))(y, wglu, bglu)


def _glu_bwd(y, dout, wglu, bglu, l, name):
    T = y.shape[0]

    def body(y_ref, do_ref, w_ref, b_ref, dy_ref, dz_ref, db_ref):
        yv = y_ref[...]
        do = do_ref[...]
        g = _gelu(yv)
        s = _sigmoid(_dot(g.astype(BF16), w_ref[...]) + b_ref[...])
        dz = do * g * s * (1.0 - s)
        dzh = dz.astype(BF16)
        dz_ref[...] = dzh
        dg = do * s + _dot_nt(dzh, w_ref[...])
        dy_ref[...] = dg * _gelu_grad(yv)

        @pl.when(pl.program_id(0) == 0)
        def _():
            db_ref[...] = jnp.zeros_like(db_ref)

        db_ref[...] += _colsum(dz)

    return _pcall(body, name=name, grid=(T // TM,),
                  in_specs=[_rows(TM, MIX), _rows(TM, MIX), _layer((MIX, MIX), l), _layer((1, MIX), l)],
                  out_specs=[_rows(TM, MIX), _rows(TM, MIX), _full((1, MIX))],
                  out_shape=[_sds((T, MIX), F32), _sds((T, MIX), BF16), _sds((1, MIX), F32)])(y, dout, wglu, bglu)


def _prev_rows(T, h, s=0):
    return pl.BlockSpec((h, MIX), lambda i: (jnp.maximum(i * (TM // h) - 1, 0), s))


def _next_rows(T, h, s=0):
    return pl.BlockSpec((h, MIX), lambda i: (jnp.minimum((i + 1) * (TM // h), T // h - 1), s))


def _conv_fwd(h, w, l, name):
    T = h.shape[0]

    def body(b_ref, c_ref, x_ref, ch_ref, xh_ref, w_ref, o_ref, ext):
        i = pl.program_id(0)
        z = c_ref[...] * x_ref[...]
        ext[0:8, :] = jnp.where(i > 0, ch_ref[...] * xh_ref[...], 0.0)
        ext[8:, :] = z
        y = (w_ref[0:1, :] * ext[pl.ds(6, TM), :] + w_ref[1:2, :] * ext[pl.ds(7, TM), :] + w_ref[2:3, :] * z)
        o_ref[...] = (b_ref[...] * y).astype(BF16)

    return _pcall(body, name=name, grid=(T // TM,),
                  in_specs=[_cols(TM, 1), _cols(TM, 2), _cols(TM, 3), _prev_rows(T, 8, 2), _prev_rows(T, 8, 3),
                            _layer((3, MIX), l)],
                  out_specs=_rows(TM, MIX), out_shape=_sds((T, MIX), BF16),
                  scratch=[pltpu.VMEM((TM + 8, MIX), F32)], sem=("parallel",))(h, h, h, h, h, w)


def _conv_bwd(dout, h, dua, w, l, name):
    T = h.shape[0]
    nb = T // TM

    def body(do_ref, b_ref, c_ref, x_ref, ch_ref, xh_ref, don_ref, bn_ref, du_ref, w_ref,
             dh_ref, dw_ref, ext, ext2):
        i = pl.program_id(0)
        c = c_ref[...]
        x = x_ref[...]
        z = c * x
        ext[0:8, :] = jnp.where(i > 0, ch_ref[...] * xh_ref[...], 0.0)
        ext[8:, :] = z
        zm2 = ext[pl.ds(6, TM), :]
        zm1 = ext[pl.ds(7, TM), :]
        w0 = w_ref[0:1, :]
        w1 = w_ref[1:2, :]
        w2 = w_ref[2:3, :]
        y = w0 * zm2 + w1 * zm1 + w2 * z
        do = do_ref[...]
        dy = do * b_ref[...]
        ext2[0:TM, :] = dy
        ext2[TM:, :] = jnp.where(i < nb - 1, don_ref[...] * bn_ref[...], 0.0)
        dz = w2 * dy + w1 * ext2[pl.ds(1, TM), :] + w0 * ext2[pl.ds(2, TM), :]
        dh_ref[:, 0:MIX] = du_ref[...].astype(BF16)
        dh_ref[:, MIX:2 * MIX] = (do * y).astype(BF16)
        dh_ref[:, 2 * MIX:3 * MIX] = (dz * x).astype(BF16)
        dh_ref[:, 3 * MIX:] = (dz * c).astype(BF16)

        @pl.when(i == 0)
        def _():
            dw_ref[...] = jnp.zeros_like(dw_ref)

        dw_ref[0:1, :] += _colsum(dy * zm2)
        dw_ref[1:2, :] += _colsum(dy * zm1)
        dw_ref[2:3, :] += _colsum(dy * z)

    return _pcall(body, name=name, grid=(nb,),
                  in_specs=[_rows(TM, MIX), _cols(TM, 1), _cols(TM, 2), _cols(TM, 3), _prev_rows(T, 8, 2),
                            _prev_rows(T, 8, 3), _next_rows(T, 8), _next_rows(T, 8, 1), _rows(TM, MIX),
                            _layer((3, MIX), l)],
                  out_specs=[_rows(TM, 4 * MIX), _full((8, MIX))],
                  out_shape=[_sds((T, 4 * MIX), BF16), _sds((8, MIX), F32)],
                  scratch=[pltpu.VMEM((TM + 8, MIX), F32), pltpu.VMEM((TM + 8, MIX), F32)])(
                      dout, h, h, h, h, h, dout, h, dua, w)


_PH = 16


def _pooled(ext, t, gi, w):
    lo = gi * POOL_GROUP
    cur = ext[pl.ds(_PH, TM), lo:lo + POOL_GROUP]
    acc = cur
    for k in range(1, w):
        acc = acc + ext[pl.ds(_PH - k, TM), lo:lo + POOL_GROUP]
    cnt = jnp.minimum(t + 1, w).astype(F32)
    return acc / cnt - cur, cnt


def _pool_fwd(h, pw, scale, l, name):
    T = h.shape[0]

    def body(z_ref, zh_ref, pw_ref, sc_ref, o_ref, ext):
        i = pl.program_id(0)
        ext[0:_PH, :] = jnp.where(i > 0, zh_ref[...], 0.0)
        ext[_PH:, :] = z_ref[...]
        t = i * TM + lax.broadcasted_iota(jnp.int32, (TM, 1), 0)
        for gi, w in enumerate(POOL_WINDOWS):
            lo = gi * POOL_GROUP
            pooled, _ = _pooled(ext, t, gi, w)
            mixed = _dot(pooled.astype(BF16), pw_ref[gi].astype(BF16))
            o_ref[:, lo:lo + POOL_GROUP] = (mixed * sc_ref[:, lo:lo + POOL_GROUP]).astype(BF16)

    return _pcall(body, name=name, grid=(T // TM,),
                  in_specs=[_cols(TM, 3), _prev_rows(T, _PH, 3), _layer((4, POOL_GROUP, POOL_GROUP), l),
                            _layer((1, MIX), l)],
                  out_specs=_rows(TM, MIX), out_shape=_sds((T, MIX), BF16),
                  scratch=[pltpu.VMEM((TM + _PH, MIX), F32)], sem=("parallel",))(h, h, pw, scale)


def _pool_bwd(dout, h, dq, dk, dv, pw, scale, l, name):
    T = h.shape[0]
    nb = T // TM

    def body(do_ref, don_ref, z_ref, zh_ref, dq_ref, dk_ref, dv_ref, pw_ref, sc_ref,
             dh_ref, dpw_ref, dsc_ref, ext, ext2):
        i = pl.program_id(0)
        ext[0:_PH, :] = jnp.where(i > 0, zh_ref[...], 0.0)
        ext[_PH:, :] = z_ref[...]
        t = i * TM + lax.broadcasted_iota(jnp.int32, (TM, 1), 0)
        dh_ref[:, 0:MIX] = dq_ref[...]
        dh_ref[:, MIX:2 * MIX] = dk_ref[...]
        dh_ref[:, 2 * MIX:3 * MIX] = dv_ref[...]

        @pl.when(i == 0)
        def _():
            dpw_ref[...] = jnp.zeros_like(dpw_ref)
            dsc_ref[...] = jnp.zeros_like(dsc_ref)

        for gi, w in enumerate(POOL_WINDOWS):
            lo = gi * POOL_GROUP
            pwb = pw_ref[gi].astype(BF16)
            sc = sc_ref[:, lo:lo + POOL_GROUP]
            pooled, cnt = _pooled(ext, t, gi, w)
            pb = pooled.astype(BF16)
            mixed = _dot(pb, pwb)
            dog = do_ref[:, lo:lo + POOL_GROUP]
            dsc_ref[:, lo:lo + POOL_GROUP] += _colsum(dog * mixed)
            dmix = (dog * sc).astype(BF16)
            dpw_ref[gi] += _dot_tn(pb, dmix)
            dpool = _dot_nt(dmix, pwb)
            dmix_n = (jnp.where(i < nb - 1, don_ref[:, lo:lo + POOL_GROUP], 0.0) * sc).astype(BF16)
            dpool_n = _dot_nt(dmix_n, pwb)
            e = dpool / cnt
            ext2[0:TM, lo:lo + POOL_GROUP] = e
            ext2[TM:, lo:lo + POOL_GROUP] = dpool_n * (1.0 / w)
            s = e
            for k in range(1, w):
                s = s + ext2[pl.ds(k, TM), lo:lo + POOL_GROUP]
            dh_ref[:, 3 * MIX + lo:3 * MIX + lo + POOL_GROUP] = (s - dpool).astype(BF16)

    blk = _rows(TM, MIX)
    return _pcall(body, name=name, grid=(nb,),
                  in_specs=[blk, _next_rows(T, _PH), _cols(TM, 3), _prev_rows(T, _PH, 3), blk, blk, blk,
                            _layer((4, POOL_GROUP, POOL_GROUP), l), _layer((1, MIX), l)],
                  out_specs=[_rows(TM, 4 * MIX), _full((4, POOL_GROUP, POOL_GROUP)), _full((1, MIX))],
                  out_shape=[_sds((T, 4 * MIX), BF16), _sds((4, POOL_GROUP, POOL_GROUP), F32), _sds((1, MIX), F32)],
                  scratch=[pltpu.VMEM((TM + _PH, MIX), F32), pltpu.VMEM((TM + _PH, MIX), F32)])(
                      dout, dout, h, h, dq, dk, dv, pw, scale)


def _att_bias_table(rel_bias):
    span = LEFT_CHUNKS * CHUNK
    assert ATT_TQ - 1 <= MAX_REL
    lo = MAX_REL - (ATT_TQ - 1)
    near = rel_bias[:, lo:2 * MAX_REL + 1]
    far = jnp.broadcast_to(rel_bias[:, 2 * MAX_REL:], (HEADS, span + ATT_TQ - 1 - MAX_REL))
    by_dist = jnp.concatenate([near, far], axis=1)
    rev = by_dist[:, ::-1]
    lv = rev.shape[1]
    skew = jnp.tile(rev, (1, ATT_TQ + 1))[:, :ATT_TQ * (lv + 1)].reshape(HEADS, ATT_TQ, lv + 1)[:, :, :ATT_W]
    bias = skew[:, ::-1, :]
    r = jnp.arange(ATT_TQ)[:, None]
    col = jnp.arange(ATT_W)[None, :]
    dchunk = (LEFT_CHUNKS + r // CHUNK) - col // CHUNK
    visible = (dchunk >= 0) & (dchunk <= LEFT_CHUNKS)
    return jnp.where(visible[None], bias, NEG_INF)


def _qrows(n, s=0):
    return pl.BlockSpec((ATT_TQ, n), lambda i: (i, s))


def _att_views_back(d, s):
    return pl.BlockSpec((ATT_TQ, MIX), lambda i: (jnp.maximum(i - (ATT_NV - 1) + d, 0), s))


_PAIR = 2 * HEAD_DIM


def _att_pair(refs, pp):
    sl = slice(pp * _PAIR, (pp + 1) * _PAIR)
    if isinstance(refs, (tuple, list)):
        return jnp.concatenate([r[:, sl] for r in refs], axis=0)
    return refs[:, sl]


def _att_fwd(h, table, name):
    T = h.shape[0]

    def body(q_ref, *refs):
        k_refs = refs[:ATT_NV]
        v_refs = refs[ATT_NV:2 * ATT_NV]
        tb_ref = refs[2 * ATT_NV]
        o_ref, oh_ref, lse_ref = refs[2 * ATT_NV + 1:]
        i = pl.program_id(0)
        col = lax.broadcasted_iota(jnp.int32, (1, ATT_W), 1)
        kvalid = (col + (i - (ATT_NV - 1)) * ATT_TQ) >= 0
        first = lax.broadcasted_iota(jnp.int32, (1, _PAIR), 1) < HEAD_DIM
        lses = []
        for pp in range(HEADS // 2):
            qp = _att_pair(q_ref, pp) * (HEAD_DIM ** -0.5)
            kp = _att_pair(k_refs, pp).astype(BF16)
            vp = _att_pair(v_refs, pp).astype(BF16)
            outs = []
            for e in range(2):
                half = first if e == 0 else jnp.logical_not(first)
                s = _dot_nt(jnp.where(half, qp, 0.0).astype(BF16), kp) + tb_ref[2 * pp + e]
                s = jnp.where(kvalid, s, NEG_INF)
                m = jnp.max(s, axis=-1, keepdims=True)
                p = jnp.exp(s - m)
                l = jnp.sum(p, axis=-1, keepdims=True)
                outs.append(_dot(p.astype(BF16), vp) / l)
                lses.append(m + jnp.log(l))
            o = jnp.where(first, outs[0], outs[1])
            o_ref[:, pp * _PAIR:(pp + 1) * _PAIR] = o
            oh_ref[:, pp * _PAIR:(pp + 1) * _PAIR] = o.astype(BF16)
        lse_ref[...] = jnp.concatenate(lses, axis=1)

    kviews = [_att_views_back(d, 1) for d in range(ATT_NV)]
    vviews = [_att_views_back(d, 2) for d in range(ATT_NV)]
    return _pcall(body, name=name, grid=(T // ATT_TQ,),
                  in_specs=[_qrows(MIX)] + kviews + vviews + [_full((HEADS, ATT_TQ, ATT_W))],
                  out_specs=[_qrows(MIX), _qrows(MIX), _qrows(HEADS)],
                  out_shape=[_sds((T, MIX), F32), _sds((T, MIX), BF16), _sds((T, HEADS), F32)],
                  sem=("parallel",))(h, *([h] * (2 * ATT_NV)), table)


def _att_bwd_q(h, do, o, lse, table, name):
    T = h.shape[0]

    def body(q_ref, *refs):
        k_refs = refs[:ATT_NV]
        v_refs = refs[ATT_NV:2 * ATT_NV]
        do_ref, o_ref, lse_ref, tb_ref, dq_ref, dl_ref, dtb_ref = refs[2 * ATT_NV:]
        i = pl.program_id(0)
        col = lax.broadcasted_iota(jnp.int32, (1, ATT_W), 1)
        kvalid = (col + (i - (ATT_NV - 1)) * ATT_TQ) >= 0

        first = lax.broadcasted_iota(jnp.int32, (1, _PAIR), 1) < HEAD_DIM

        @pl.when(i == 0)
        def _():
            dtb_ref[...] = jnp.zeros_like(dtb_ref)

        deltas = []
        for pp in range(HEADS // 2):
            qp = _att_pair(q_ref, pp) * (HEAD_DIM ** -0.5)
            kp = _att_pair(k_refs, pp).astype(BF16)
            vp = _att_pair(v_refs, pp).astype(BF16)
            dop = _att_pair(do_ref, pp)
            doo = dop * _att_pair(o_ref, pp)
            outs = []
            for e in range(2):
                hd = 2 * pp + e
                half = first if e == 0 else jnp.logical_not(first)
                s = _dot_nt(jnp.where(half, qp, 0.0).astype(BF16), kp) + tb_ref[hd]
                s = jnp.where(kvalid, s, NEG_INF)
                p = jnp.exp(s - lse_ref[:, hd:hd + 1])
                delta = jnp.sum(jnp.where(half, doo, 0.0), axis=-1, keepdims=True)
                dp = _dot_nt(jnp.where(half, dop, 0.0).astype(BF16), vp)
                ds = p * (dp - delta)
                dtb_ref[hd] += ds
                outs.append(_dot(ds.astype(BF16), kp))
                deltas.append(delta)
            dq = jnp.where(first, outs[0], outs[1]) * (HEAD_DIM ** -0.5)
            dq_ref[:, pp * _PAIR:(pp + 1) * _PAIR] = dq.astype(BF16)
        dl_ref[...] = jnp.concatenate(deltas, axis=1)

    kviews = [_att_views_back(d, 1) for d in range(ATT_NV)]
    vviews = [_att_views_back(d, 2) for d in range(ATT_NV)]
    tb = _full((HEADS, ATT_TQ, ATT_W))
    return _pcall(body, name=name, grid=(T // ATT_TQ,),
                  in_specs=[_qrows(MIX)] + kviews + vviews + [_qrows(MIX), _qrows(MIX), _qrows(HEADS), tb],
                  out_specs=[_qrows(MIX), _qrows(HEADS), tb],
                  out_shape=[_sds((T, MIX), BF16), _sds((T, HEADS), F32), _sds((HEADS, ATT_TQ, ATT_W), F32)])(
                      h, *([h] * (2 * ATT_NV)), do, o, lse, table)


def _att_table_by_key(table):
    t = table.reshape(HEADS, ATT_TQ, ATT_NV, ATT_TQ)[:, :, ::-1, :]
    return jnp.transpose(t, (0, 3, 2, 1)).reshape(HEADS, ATT_TQ, ATT_W)


def _att_bwd_kv(h, do, lse_t, delta_t, table_k, name):
    T = h.shape[0]
    nb = T // ATT_TQ

    def fwd_view(d, s=0):
        return pl.BlockSpec((ATT_TQ, MIX), lambda j: (jnp.minimum(j + d, nb - 1), s))

    def row_view(d):
        return pl.BlockSpec((HEADS, ATT_TQ), lambda j: (0, jnp.minimum(j + d, nb - 1)))

    def body(k_ref, v_ref, *refs):
        q_refs = refs[:ATT_NV]
        do_refs = refs[ATT_NV:2 * ATT_NV]
        lse_refs = refs[2 * ATT_NV:3 * ATT_NV]
        dl_refs = refs[3 * ATT_NV:4 * ATT_NV]
        tb_ref, dk_ref, dv_ref = refs[4 * ATT_NV:]
        j = pl.program_id(0)
        view = lax.broadcasted_iota(jnp.int32, (1, ATT_W), 1) // ATT_TQ
        valid = (j + view) <= nb - 1
        first = lax.broadcasted_iota(jnp.int32, (1, _PAIR), 1) < HEAD_DIM
        for pp in range(HEADS // 2):
            kp = _att_pair(k_ref, pp)
            vp = _att_pair(v_ref, pp)
            qs = (_att_pair(q_refs, pp) * (HEAD_DIM ** -0.5)).astype(BF16)
            dos = _att_pair(do_refs, pp).astype(BF16)
            dks, dvs = [], []
            for e in range(2):
                hd = 2 * pp + e
                half = first if e == 0 else jnp.logical_not(first)
                lses = jnp.concatenate([r[hd:hd + 1, :] for r in lse_refs], axis=1)
                dls = jnp.concatenate([r[hd:hd + 1, :] for r in dl_refs], axis=1)
                st = _dot_nt(jnp.where(half, kp, 0.0).astype(BF16), qs) + tb_ref[hd]
                pt = jnp.where(valid, jnp.exp(st - lses), 0.0)
                dvs.append(_dot(pt.astype(BF16), dos))
                dst = pt * (_dot_nt(jnp.where(half, vp, 0.0).astype(BF16), dos) - dls)
                dks.append(_dot(dst.astype(BF16), qs))
            dk_ref[:, pp * _PAIR:(pp + 1) * _PAIR] = jnp.where(first, dks[0], dks[1]).astype(BF16)
            dv_ref[:, pp * _PAIR:(pp + 1) * _PAIR] = jnp.where(first, dvs[0], dvs[1]).astype(BF16)

    qv = [fwd_view(d, 0) for d in range(ATT_NV)]
    dov = [fwd_view(d) for d in range(ATT_NV)]
    rows = [row_view(d) for d in range(ATT_NV)]
    return _pcall(body, name=name, grid=(nb,),
                  in_specs=[_qrows(MIX, 1), _qrows(MIX, 2)] + qv + dov + rows + rows
                  + [_full((HEADS, ATT_TQ, ATT_W))],
                  out_specs=[_qrows(MIX), _qrows(MIX)], out_shape=[_sds((T, MIX), BF16), _sds((T, MIX), BF16)],
                  sem=("parallel",))(h, h, *([h] * ATT_NV), *([do] * ATT_NV), *([lse_t] * ATT_NV),
                                     *([delta_t] * ATT_NV), table_k)


def _all_gather(x, name):
    R, C = x.shape

    def body(x_ref, out_ref, send_sems, recv_sems, local_sem):
        xi, yi, ci = lax.axis_index("x"), lax.axis_index("y"), lax.axis_index("c")
        me, sibling = (xi, yi, ci), (xi, yi, 1 - ci)
        chips = [(1 - xi, yi), (xi, 1 - yi), (1 - xi, 1 - yi)]

        def slot(px, py, pc):
            return out_ref.at[4 * px + 2 * py + pc]

        def copy(k, block, to, src=None):
            return pltpu.make_async_remote_copy(
                src_ref=slot(*block) if src is None else src, dst_ref=slot(*block),
                send_sem=send_sems.at[k], recv_sem=recv_sems.at[k], device_id=to, device_id_type=_MESH)

        mine = pltpu.make_async_copy(x_ref, slot(*me), local_sem)
        mine.start()
        first = [copy(0, me, sibling, src=x_ref)]
        first += [copy(1 + j, me, (*chip, ci), src=x_ref) for j, chip in enumerate(chips)]
        for cp in first:
            cp.start()
        passed = [copy(4 + j, (*chip, ci), sibling) for j, chip in enumerate(chips)]
        for j, chip in enumerate(chips):
            copy(1 + j, (*chip, ci), me).wait_recv()
            passed[j].start()
        copy(0, sibling, me).wait_recv()
        for j, chip in enumerate(chips):
            copy(4 + j, (*chip, 1 - ci), me).wait_recv()
        for cp in first + passed:
            cp.wait_send()
        mine.wait()

    return pl.pallas_call(
        body, name=name, out_shape=_sds((N_DEV, R, C), x.dtype), in_specs=[_ANY], out_specs=_ANY,
        scratch_shapes=[pltpu.SemaphoreType.DMA((7,)), pltpu.SemaphoreType.DMA((7,)), pltpu.SemaphoreType.DMA(())],
    )(x)


def _a2a_copies(s_ref, r_ref, send_sems, recv_sems, local_sem, gather=False):
    xi, yi, ci = lax.axis_index("x"), lax.axis_index("y"), lax.axis_index("c")
    me = 4 * xi + 2 * yi + ci

    def mine():
        return pltpu.make_async_copy(s_ref if gather else s_ref.at[me], r_ref.at[me], local_sem)

    def remote(m, sending):
        px = 1 - xi if m & 4 else xi
        py = 1 - yi if m & 2 else yi
        pc = 1 - ci if m & 1 else ci
        peer = 4 * px + 2 * py + pc
        src, dst = (s_ref.at[peer], r_ref.at[me]) if sending else (s_ref.at[me], r_ref.at[peer])
        if gather:
            src = s_ref
        return pltpu.make_async_remote_copy(src_ref=src, dst_ref=dst, send_sem=send_sems.at[m - 1],
                                            recv_sem=recv_sems.at[m - 1], device_id=(px, py, pc), device_id_type=_MESH)

    def start():
        mine().start()
        for m in range(1, N_DEV):
            remote(m, True).start()

    def wait():
        for m in range(1, N_DEV):
            remote(m, False).wait_recv()
        for m in range(1, N_DEV):
            remote(m, True).wait_send()
        mine().wait()

    return start, wait


_A2A_SEMS = [pltpu.SemaphoreType.DMA((7,)), pltpu.SemaphoreType.DMA((7,)), pltpu.SemaphoreType.DMA(())]


def _all_to_all(s, name):
    def body(s_ref, r_ref, send_sems, recv_sems, local_sem):
        start, wait = _a2a_copies(s_ref, r_ref, send_sems, recv_sems, local_sem)
        start()
        wait()

    return pl.pallas_call(
        body, name=name, out_shape=_sds(s.shape, s.dtype), in_specs=[_ANY], out_specs=_ANY,
        scratch_shapes=list(_A2A_SEMS))(s)


def _riding(body, n_in, n_out, n_ex, first, last, gather=False):
    def wrapped(*refs):
        ins = refs[:n_in]
        sends = refs[n_in:n_in + n_ex]
        outs = refs[n_in + n_ex:n_in + n_ex + n_out]
        recvs = refs[n_in + n_ex + n_out:n_in + 2 * n_ex + n_out]
        rest = refs[n_in + 2 * n_ex + n_out:]
        scratch, sems = rest[:len(rest) - 3 * n_ex], rest[len(rest) - 3 * n_ex:]

        @pl.when(first())
        def _():
            for k in range(n_ex):
                _a2a_copies(sends[k], recvs[k], *sems[3 * k:3 * k + 3], gather=gather)[0]()

        body(*ins, *outs, *scratch)

        @pl.when(last())
        def _():
            for k in range(n_ex):
                _a2a_copies(sends[k], recvs[k], *sems[3 * k:3 * k + 3], gather=gather)[1]()

    return wrapped


_ADAMW_BLOCK_BYTES = 6 * 1024 * 1024


def _adamw(parts, row_off, w, m, v, name):
    R, C = w.shape
    tr = None
    for cand in (512, 256, 128, 64, 32, 16):
        step_bytes = cand * C * (N_DEV * parts.dtype.itemsize + 7 * 4)
        if R % cand == 0 and row_off % cand == 0 and step_bytes <= _ADAMW_BLOCK_BYTES:
            tr = cand
            break
    assert tr is not None, (R, C, row_off)
    off = row_off // tr
    c1 = 1.0 - ADAM_B1 ** ADAM_STEP
    c2 = 1.0 - ADAM_B2 ** ADAM_STEP

    def body(p_ref, w_ref, m_ref, v_ref, g_ref, d_ref, mo_ref, vo_ref):
        g = p_ref[0].astype(F32)
        for j in range(1, N_DEV):
            g = g + p_ref[j].astype(F32)
        mn = ADAM_B1 * m_ref[...] + (1.0 - ADAM_B1) * g
        vn = ADAM_B2 * v_ref[...] + (1.0 - ADAM_B2) * (g * g)
        m_hat = mn / c1
        v_hat = vn / c2
        g_ref[...] = g
        d_ref[...] = -ADAM_LR * (m_hat / (jnp.sqrt(v_hat) + ADAM_EPS) + ADAM_WD * w_ref[...])
        mo_ref[...] = mn
        vo_ref[...] = vn

    blk = _rows(tr, C)
    return _pcall(body, name=name, grid=(R // tr,),
                  in_specs=[pl.BlockSpec((N_DEV, tr, C), lambda i: (0, off + i, 0)), blk, blk, blk],
                  out_specs=[blk] * 4, out_shape=[_sds((R, C), F32)] * 4,
                  sem=("parallel",))(parts, w, m, v)


def _piece_rows(size):
    return -(-size // PIECE) * 16


def _pack_rows(arrays, lead=()):
    parts = []
    total = 0
    for a in arrays:
        flat = a.reshape(lead + (-1,))
        size = flat.shape[-1]
        rows = _piece_rows(size)
        pad = [(0, 0)] * len(lead) + [(0, rows * FLAT_COLS - size)]
        parts.append(jnp.pad(flat, pad).reshape(lead + (rows, FLAT_COLS)))
        total += rows
    tail = -total % FLAT_ROWS
    if tail:
        parts.append(jnp.zeros(lead + (tail, FLAT_COLS), parts[0].dtype))
    return jnp.concatenate(parts, axis=len(lead))


def _unpack_rows(buf, shapes, lead=()):
    out = []
    row = 0
    nl = len(lead)
    for shape in shapes:
        size = math.prod(shape)
        rows = _piece_rows(size)
        piece = lax.slice_in_dim(buf, row, row + rows, axis=nl).reshape(lead + (rows * FLAT_COLS,))
        out.append(lax.slice_in_dim(piece, 0, size, axis=nl).reshape(lead + tuple(shape)))
        row += rows
    return out


def _to_blocks(full, axis):
    shp = full.shape
    split = full.reshape(shp[:axis] + (N_DEV, shp[axis] // N_DEV) + shp[axis + 1:])
    return jnp.moveaxis(split, axis, 0)


def _from_blocks(blocks, axis):
    shp = blocks.shape[1:]
    moved = jnp.moveaxis(blocks, 0, axis)
    return moved.reshape(shp[:axis] + (N_DEV * shp[axis],) + shp[axis + 1:])


N_LAYERS = {n: (DEPTH // 2 if n.startswith(('ev_', 'od_')) else DEPTH) for n in WEIGHT_NAMES}
R_OUT = D_MODEL // N_DEV
R_DOWN = D_FF // N_DEV


def _layer_shards(W, i):
    j = i // 2
    ev = i % 2 == 0
    rows = [W['ev_w_out' if ev else 'od_w_out'][j], W['ffn_w_down'][i], W['ple_w_gate'][i]]
    return dict(up=W['ffn_w_up'][i].astype(BF16), inn=W['ev_w_in' if ev else 'od_w_in'][j].astype(BF16),
                rows=jnp.concatenate(rows, axis=0).astype(BF16))


def _layer_weights(up, inn, rows):
    return dict(up=up.reshape(2, FF_J, 1, D_MODEL, FF_NB), win=inn.reshape(N_DEV, 1, D_MODEL, IN_NB),
                wo=lax.slice_in_dim(rows, 0, R_OUT, axis=1).reshape(1, D_MODEL, D_MODEL),
                wd4=lax.slice_in_dim(rows, R_OUT, R_OUT + R_DOWN, axis=1).reshape(1, FF_J, FF_NB, D_MODEL),
                gate=lax.slice_in_dim(rows, R_OUT + R_DOWN, 2 * R_OUT + R_DOWN, axis=1).reshape(1, D_MODEL, D_MODEL))


def _layer_weights_of(W, i):
    j = i // 2
    ev = i % 2 == 0
    return dict(up=W['ffn_w_up'][:, i:i + 1].reshape(2, FF_J, 1, D_MODEL, FF_NB),
                win=W['ev_w_in' if ev else 'od_w_in'][:, j:j + 1],
                wo=W['ev_w_out' if ev else 'od_w_out'][j:j + 1],
                wd4=W['ffn_w_down'][i:i + 1].reshape(1, FF_J, FF_NB, D_MODEL), gate=W['ple_w_gate'][i:i + 1])


def _row_blocks(g):
    return g.reshape(N_DEV, g.shape[0] // N_DEV, g.shape[1])


def _early_sends(G, i):
    ffn = jnp.concatenate([_row_blocks(G['ffn_w_down'][i]), _row_blocks(G['ple_w_gate'][i])], axis=1)
    return dict(up=G['ffn_w_up'][i], ffn=ffn)


def _late_sends(G, i):
    j = i // 2
    ev = i % 2 == 0
    return dict(out=_row_blocks(G['ev_w_out' if ev else 'od_w_out'][j]), inn=G['ev_w_in' if ev else 'od_w_in'][j])


def _local_step(x, p, tgt, W, inn0=None, shards=None):
    overlap = shards is not None
    big = [] if overlap else [_layer_weights_of(W, i) for i in range(DEPTH)]
    ln = {n: _vecs(W[n]) for n in ('ln_mix_g', 'ln_mix_b', 'ln_ffn_g', 'ln_ffn_b', 'ple_b_gate')}
    bglu = _vecs(W['ev_b_glu'])
    pscale = _vecs(W['od_pool_scale'])
    saved = []
    x0 = x
    x0h = x.astype(BF16)
    for i in range(DEPTH):
        L = f"L{i}_"
        j = i // 2
        if overlap and i == 0:
            h, got = _mm_in(x0h, inn0.reshape(N_DEV, 1, D_MODEL, IN_NB), 0, L + "mm_in",
                            ride=(shards[0]['up'], shards[0]['rows']))
            big.append(_layer_weights(got[0], inn0, got[1]))
        else:
            h = _mm_in(x0h, big[i]['win'], 0, L + "mm_in")
        B = big[i]
        s = dict(x0=x0, x0h=x0h, B=B)
        if i % 2 == 0:
            params = tuple(W[n][j] for n in ('ev_lambda_re', 'ev_lambda_im', 'ev_log_dt', 'ev_b_re', 'ev_b_im',
                                             'ev_c_re', 'ev_c_im', 'ev_d'))
            (r0, qt, pm, a1, a2), op_vjp = jax.vjp(_s5_operators, *params)
            umat = _to_chunks(h, L + "to_chunks")
            st = jnp.transpose(_s5_chunk_state(umat, pm, L + "s5_state"), (1, 0, 2))
            xp_t = _s5_scan_fwd(st, a1, a2, L + "s5_scan")
            xprev = jnp.transpose(xp_t, (1, 0, 2))
            y = _from_chunks(_s5_output(umat, xprev, r0, qt, L + "s5_out"), L + "from_chunks")
            ya, gh = _glu_fwd(y, W['ev_w_glu'], bglu, j, L + "glu")
            yb = _conv_fwd(h, W['ev_conv_w'], j, L + "conv")
            s.update(op_vjp=op_vjp, r0=r0, qt=qt, pm=pm, a1=a1, a2=a2, umat=umat, xp_t=xp_t, xprev=xprev, y=y, gh=gh)
        else:
            table, tb_vjp = jax.vjp(_att_bias_table, W['od_rel_bias'][j])
            of, ya, lse = _att_fwd(h, table, L + "att")
            yb = _pool_fwd(h, W['od_pool_w'], pscale, j, L + "pool")
            s.update(table=table, tb_vjp=tb_vjp, of=of, lse=lse)
        g1, b1 = (ln['ln_mix_g'], i), (ln['ln_mix_b'], i)
        g2, b2 = (ln['ln_ffn_g'], i), (ln['ln_ffn_b'], i)
        xhat1, rstd1, x1h = _mm_out_ln(ya, yb, B['wo'], 0, x0, g1, b1, L + "mm_out_ln")
        nxt = shards[i + 1] if overlap and i + 1 < DEPTH else None
        (hf, a3), got_a = _mm_up(x1h, B['up'], 0, L + "mm_up", ride=(nxt['up'], nxt['inn']) if nxt else ())
        (xhat2, rstd2, x2h), got_b = _mm_down_ln(a3, B['wd4'], 0, xhat1, g1, b1, g2, b2, L + "mm_down_ln",
                                                 ride=(nxt['rows'],) if nxt else ())
        if nxt:
            big.append(_layer_weights(got_a[0], got_a[1], got_b[0]))
        x3, x3h = _mm_ple(x2h, xhat2, ln['ln_ffn_g'], ln['ln_ffn_b'], p, B['gate'], ln['ple_b_gate'],
                          W['ple_w_proj'], i, L + "mm_ple", lw=0)
        s.update(h=h, ya=ya, yb=yb, xhat1=xhat1, rstd1=rstd1, x1h=x1h, hf=hf, a3=a3, xhat2=xhat2,
                 rstd2=rstd2, x2h=x2h, g1=g1)
        saved.append(s)
        x0, x0h = x3, x3h

    dx, loss = _loss_head(x0, tgt, "loss_head")

    G = {n: [None] * N_LAYERS[n] for n in WEIGHT_NAMES}
    landed = {i: {} for i in range(DEPTH)}
    pending = None
    for i in reversed(range(DEPTH)):
        L = f"L{i}_"
        j = i // 2
        s = saved[i]
        B = s['B']
        dr2, dr2h, dpreh, dpph, dbg, dg2, db2 = _ple_bwd(
            dx, s['x2h'], p, B['gate'], ln['ple_b_gate'], W['ple_w_proj'], i, s['xhat2'], s['rstd2'],
            ln['ln_ffn_g'], L + "ple_bwd", lw=0)
        G['ple_w_gate'][i] = _mm_tn(s['x2h'], dpreh, L + "dw_gate")
        G['ple_w_proj'][i] = _mm_tn(p, dpph, L + "dw_proj", a_layer=i)
        G['ple_b_gate'][i] = dbg[0]
        G['ln_ffn_g'][i] = dg2[0]
        G['ln_ffn_b'][i] = db2[0]
        dhf, got = _ffn_bwd1(dr2h, B['wd4'], 0, s['hf'], L + "ffn_bwd1",
                             ride=(pending['out'], pending['inn']) if pending else ())
        if pending:
            landed[i + 1].update(out=got[0], inn=got[1])
        G['ffn_w_down'][i] = _mm_tn_ablk(s['a3'], dr2h, L + "dw_down").reshape(D_FF, D_MODEL)
        T = dhf.shape[2]
        G['ffn_w_up'][i] = _mm_tn_bblk(s['x1h'], dhf.reshape(N_DEV, T, FF_NB), L + "dw_up")
        early = _early_sends(G, i) if overlap else None
        (dr1, dr1h, dg1, db1), got = _ffn_bwd2(dhf, B['up'], 0, dr2, s['xhat1'], s['rstd1'], s['g1'], L + "ffn_bwd2",
                                               ride=(early['up'], early['ffn']) if overlap else ())
        if overlap:
            landed[i].update(up=got[0], ffn=got[1])
        G['ln_mix_g'][i] = dg1[0]
        G['ln_mix_b'][i] = db1[0]
        dya, dyb = _out_bwd(dr1h, B['wo'], 0, L + "out_bwd")
        dwo = jnp.concatenate([_mm_tn(s['ya'], dr1h, L + "dw_out_a"), _mm_tn(s['yb'], dr1h, L + "dw_out_b")], axis=0)
        if i % 2 == 0:
            G['ev_w_out'][j] = dwo
            dy, dzh, dbglu = _glu_bwd(s['y'], dya, W['ev_w_glu'], bglu, j, L + "glu_bwd")
            G['ev_w_glu'][j] = _mm_tn(s['gh'], dzh, L + "dw_glu")
            G['ev_b_glu'][j] = dbglu[0]
            dymat = _to_chunks(dy, L + "to_chunks_dy")
            dxp_t = jnp.transpose(_s5_bwd_state(dymat, s['qt'], L + "s5_bwd_state"), (1, 0, 2))
            ds_t, da1, da2 = _s5_scan_bwd(dxp_t, s['xp_t'], s['a1'], s['a2'], L + "s5_scan_bwd")
            dumat, dr0, dqt, dpm = _s5_bwd_main(s['umat'], s['xprev'], dymat, jnp.transpose(ds_t, (1, 0, 2)),
                                                s['r0'], s['pm'], L + "s5_bwd")
            dparams = s['op_vjp']((dr0, dqt, dpm, da1, da2))
            for n, dpar in zip(('ev_lambda_re', 'ev_lambda_im', 'ev_log_dt', 'ev_b_re', 'ev_b_im', 'ev_c_re',
                                'ev_c_im', 'ev_d'), dparams):
                G[n][j] = dpar
            dua = _from_chunks(dumat, L + "from_chunks_du")
            dh, dcw = _conv_bwd(dyb, s['h'], dua, W['ev_conv_w'], j, L + "conv_bwd")
            G['ev_conv_w'][j] = dcw[:3]
            wname = 'ev_w_in'
        else:
            G['od_w_out'][j] = dwo
            dq, delta, dtable = _att_bwd_q(s['h'], dya, s['of'], s['lse'], s['table'], L + "att_bwd_q")
            dk, dv = _att_bwd_kv(s['h'], dya, s['lse'].T, delta.T, _att_table_by_key(s['table']), L + "att_bwd_kv")
            G['od_rel_bias'][j] = s['tb_vjp'](dtable)[0]
            dh, dpw, dsc = _pool_bwd(dyb, s['h'], dq, dk, dv, W['od_pool_w'], pscale, j, L + "pool_bwd")
            G['od_pool_w'][j] = dpw
            G['od_pool_scale'][j] = dsc[0]
            wname = 'od_w_in'
        G[wname][j] = _mm_tn(s['x0h'], dh, L + "dw_in", blocked_n=IN_NB)
        dx = _in_bwd(dh, B['win'], 0, dr1, L + "in_bwd")
        pending = _late_sends(G, i) if overlap else None

    if overlap:
        return loss, dx, G, landed, pending
    return loss, dx, G


def _slab(a):
    return a.reshape(-1, a.shape[-1])


def _gather_small(W):
    full = {n: W[n] for n in REPLICATED}
    got = _all_gather(_pack_rows([W[n].astype(BF16) for n in SMALL_SHARDED]), "gather_w_small")
    shapes = [W[n].shape for n in SMALL_SHARDED]
    for n, blocks in zip(SMALL_SHARDED, _unpack_rows(got, shapes, lead=(N_DEV,))):
        full[n] = _from_blocks(blocks, SHARD_AXIS[n])
    for n in ('ev_conv_w', 'od_pool_scale'):
        full[n] = full[n].astype(F32)
    return full


def _step(x, p, tgt, W, M, V):
    shards = [_layer_shards(W, i) for i in range(DEPTH)]
    inn0 = _all_gather(shards[0]['inn'], "gather_w0_inn")
    loss, dx, G, landed, tail = _local_step(x[0], p[:, 0], tgt[0], _gather_small(W), inn0=inn0, shards=shards)
    landed[0].update({k: _all_to_all(v, "scatter_g_" + k) for k, v in tail.items()})
    res = {}

    def update(parts, n):
        shape = W[n].shape
        outs = _adamw(parts, 0, _slab(W[n]), _slab(M[n]), _slab(V[n]), "adamw_" + n)
        for kind, a in zip(('grad', 'delta', 'm', 'v'), outs):
            res[kind, n] = a.reshape(shape)

    def over_layers(layers, key, lo=None, hi=None):
        got = [landed[i][key] for i in layers]
        if lo is not None:
            got = [lax.slice_in_dim(g, lo, hi, axis=1) for g in got]
        return jnp.concatenate(got, axis=1)

    even, odd, every = (0, 2), (1, 3), (0, 1, 2, 3)
    update(over_layers(even, 'out'), 'ev_w_out')
    update(over_layers(odd, 'out'), 'od_w_out')
    update(over_layers(every, 'ffn', 0, R_DOWN), 'ffn_w_down')
    update(over_layers(every, 'ffn', R_DOWN, R_DOWN + R_OUT), 'ple_w_gate')
    update(over_layers(every, 'up'), 'ffn_w_up')
    update(over_layers(even, 'inn'), 'ev_w_in')
    update(over_layers(odd, 'inn'), 'od_w_in')

    shapes = [W[n].shape for n in SMALL_SHARDED]
    send = _pack_rows([_to_blocks(jnp.stack(G[n], axis=0), SHARD_AXIS[n]).astype(BF16) for n in SMALL_SHARDED],
                      lead=(N_DEV,))
    parts = _all_to_all(send, "scatter_g_small")
    outs = _adamw(parts, 0, _pack_rows([W[n] for n in SMALL_SHARDED]), _pack_rows([M[n] for n in SMALL_SHARDED]),
                  _pack_rows([V[n] for n in SMALL_SHARDED]), "adamw_small")
    for kind, buf in zip(('grad', 'delta', 'm', 'v'), outs):
        for n, a in zip(SMALL_SHARDED, _unpack_rows(buf, shapes)):
            res[kind, n] = a

    repl_shapes = [W[n].shape for n in REPLICATED]
    small = _pack_rows([jnp.stack(G[n], axis=0) for n in REPLICATED] + [loss])
    parts = _all_gather(small, "gather_g_replicated")
    zero = jnp.zeros((1, 1), F32)
    outs = _adamw(parts, 0, _pack_rows([W[n] for n in REPLICATED] + [zero]),
                  _pack_rows([M[n] for n in REPLICATED] + [zero]),
                  _pack_rows([V[n] for n in REPLICATED] + [zero]), "adamw_replicated")
    for kind, buf in zip(('grad', 'delta', 'm', 'v'), outs):
        arrays = _unpack_rows(buf, repl_shapes + [(1, 1)])
        for n, a in zip(REPLICATED, arrays):
            res[kind, n] = a
        if kind == 'grad':
            total_loss = arrays[-1].reshape(())

    out = [total_loss, dx[None]]
    for kind in ('grad', 'delta', 'm', 'v'):
        out += [res[kind, n] for n in WEIGHT_NAMES]
    return tuple(out)


def kernel(x, p, ev_w_in, ev_lambda_re, ev_lambda_im, ev_log_dt, ev_b_re, ev_b_im, ev_c_re, ev_c_im, ev_d, ev_w_glu, ev_b_glu, ev_conv_w, ev_w_out, od_w_in, od_rel_bias, od_pool_w, od_pool_scale, od_w_out, ln_mix_g, ln_mix_b, ln_ffn_g, ln_ffn_b, ffn_w_up, ffn_w_down, ple_w_proj, ple_w_gate, ple_b_gate, loss_target, m_ev_w_in, m_ev_lambda_re, m_ev_lambda_im, m_ev_log_dt, m_ev_b_re, m_ev_b_im, m_ev_c_re, m_ev_c_im, m_ev_d, m_ev_w_glu, m_ev_b_glu, m_ev_conv_w, m_ev_w_out, m_od_w_in, m_od_rel_bias, m_od_pool_w, m_od_pool_scale, m_od_w_out, m_ln_mix_g, m_ln_mix_b, m_ln_ffn_g, m_ln_ffn_b, m_ffn_w_up, m_ffn_w_down, m_ple_w_proj, m_ple_w_gate, m_ple_b_gate, v_ev_w_in, v_ev_lambda_re, v_ev_lambda_im, v_ev_log_dt, v_ev_b_re, v_ev_b_im, v_ev_c_re, v_ev_c_im, v_ev_d, v_ev_w_glu, v_ev_b_glu, v_ev_conv_w, v_ev_w_out, v_od_w_in, v_od_rel_bias, v_od_pool_w, v_od_pool_scale, v_od_w_out, v_ln_mix_g, v_ln_mix_b, v_ln_ffn_g, v_ln_ffn_b, v_ffn_w_up, v_ffn_w_down, v_ple_w_proj, v_ple_w_gate, v_ple_b_gate):
    given = dict(locals())
    W = {n: given[n] for n in WEIGHT_NAMES}
    M = {n: given["m_" + n] for n in WEIGHT_NAMES}
    V = {n: given["v_" + n] for n in WEIGHT_NAMES}
    return _step(x, p, loss_target, W, M, V)
```

```python
import math

import jax
import jax.numpy as jnp
from jax import lax
from jax.experimental import pallas as pl
from jax.experimental.pallas import tpu as pltpu

F32 = jnp.float32
BF16 = jnp.bfloat16
HI = lax.Precision.HIGHEST

D_MODEL = 1024
DEPTH = 4
CHUNK = 64
MIX = 512
S5_GROUP = 16
S5_GROUPS = 32
S5_STATE = 64
HEADS = 8
HEAD_DIM = 64
LEFT_CHUNKS = 8
MAX_REL = 128
POOL_WINDOWS = (2, 4, 8, 16)
POOL_GROUP = 128
D_FF = 2816
D_PLE = 256
ALPHA = (2 * DEPTH) ** 0.25
LN_EPS = 1e-5
NEG_INF = -1e30
ADAM_LR = 0.001
ADAM_B1 = 0.9
ADAM_B2 = 0.999
ADAM_EPS = 1e-08
ADAM_WD = 0.01
ADAM_STEP = 10
N_DEV = 8

WEIGHT_NAMES = ['ev_w_in', 'ev_lambda_re', 'ev_lambda_im', 'ev_log_dt', 'ev_b_re', 'ev_b_im', 'ev_c_re', 'ev_c_im',
                'ev_d', 'ev_w_glu', 'ev_b_glu', 'ev_conv_w', 'ev_w_out', 'od_w_in', 'od_rel_bias', 'od_pool_w',
                'od_pool_scale', 'od_w_out', 'ln_mix_g', 'ln_mix_b', 'ln_ffn_g', 'ln_ffn_b', 'ffn_w_up', 'ffn_w_down',
                'ple_w_proj', 'ple_w_gate', 'ple_b_gate']
SHARD_AXIS = {'ev_w_in': 2, 'ev_w_glu': 1, 'ev_conv_w': 2, 'ev_w_out': 1, 'od_w_in': 2, 'od_pool_scale': 1,
              'od_w_out': 1, 'ffn_w_up': 2, 'ffn_w_down': 1, 'ple_w_proj': 2, 'ple_w_gate': 1}
REPLICATED = [n for n in WEIGHT_NAMES if n not in SHARD_AXIS]
SMALL_SHARDED = ['ev_w_glu', 'ev_conv_w', 'od_pool_scale', 'ple_w_proj']

VMEM_LIMIT = 48 * 1024 * 1024
TM = 512
TM_FF = 1024
TK_DW = 2048
IN_NB = 4 * MIX // N_DEV
FF_NB = 2 * D_FF // N_DEV
FF_J = N_DEV // 2
S5_LC = 32
S5_LW = S5_LC * S5_GROUP
ATT_TQ = 128
ATT_NV = LEFT_CHUNKS * CHUNK // ATT_TQ + 1
ATT_W = ATT_NV * ATT_TQ
FLAT_COLS = 1024
FLAT_ROWS = 256
PIECE = 16 * FLAT_COLS


_MESH = pl.DeviceIdType.MESH
_ANY = pl.BlockSpec(memory_space=pl.ANY)


def _sds(shape, dt):
    return jax.ShapeDtypeStruct(shape, dt)


def _pcall(body, *, name, grid, in_specs, out_specs, out_shape, scratch=(), sem=None):
    sem = sem or ("arbitrary",) * len(grid)
    return pl.pallas_call(
        body, name=name, grid=grid, in_specs=in_specs, out_specs=out_specs, out_shape=out_shape,
        scratch_shapes=scratch,
        compiler_params=pltpu.CompilerParams(dimension_semantics=sem, vmem_limit_bytes=VMEM_LIMIT))


def _rows(tm, n):
    return pl.BlockSpec((tm, n), lambda i: (i, 0))


def _cols(tm, s):
    return pl.BlockSpec((tm, MIX), lambda i: (i, s))


def _full(shape):
    nd = len(shape)
    return pl.BlockSpec(shape, lambda *_: (0,) * nd)


def _layer(shape, l):
    nd = len(shape)
    return pl.BlockSpec((None,) + tuple(shape), lambda *_: (l,) + (0,) * nd)


def _vecs(a):
    return a.reshape(a.shape[0], 1, a.shape[1])


def _dot(a, b, precision=None):
    return jnp.dot(a, b, preferred_element_type=F32, precision=precision)


def _dot_nt(a, b, precision=None):
    return lax.dot_general(a, b, (((1,), (1,)), ((), ())), preferred_element_type=F32, precision=precision)


def _dot_tn(a, b, precision=None):
    return lax.dot_general(a, b, (((0,), (0,)), ((), ())), preferred_element_type=F32, precision=precision)


def _split(a):
    hi = a.astype(BF16)
    return hi, (a - hi.astype(F32)).astype(BF16)


def _mm3(dot, a2, b2):
    return dot(a2[0], b2[0]) + (dot(a2[0], b2[1]) + dot(a2[1], b2[0]))


def _sigmoid(x):
    return 0.5 * jnp.tanh(0.5 * x) + 0.5


_GELU_C = math.sqrt(2.0 / math.pi)


def _gelu(x):
    return 0.5 * x * (1.0 + jnp.tanh(_GELU_C * (x + 0.044715 * x * x * x)))


def _gelu_grad(x):
    t = jnp.tanh(_GELU_C * (x + 0.044715 * x * x * x))
    return 0.5 * (1.0 + t) + 0.5 * x * (1.0 - t * t) * _GELU_C * (1.0 + 3.0 * 0.044715 * x * x)


def _ln_fwd(r, g, b):
    mu = jnp.mean(r, axis=-1, keepdims=True)
    xc = r - mu
    var = jnp.mean(xc * xc, axis=-1, keepdims=True)
    rstd = lax.rsqrt(var + LN_EPS)
    xhat = xc * rstd
    return xhat, rstd, xhat * g + b


def _ln_bwd(dx, xhat, rstd, g):
    dxh = dx * g
    m1 = jnp.mean(dxh, axis=-1, keepdims=True)
    m2 = jnp.mean(dxh * xhat, axis=-1, keepdims=True)
    return rstd * (dxh - m1 - xhat * m2)


def _colsum(x):
    return jnp.sum(x, axis=0, keepdims=True)


def _mm_in(xh, win8, l, name, ride=()):
    T = xh.shape[0]
    ni = T // TM

    def body(x_ref, w_ref, h_ref):
        x = x_ref[...]
        for b in range(N_DEV):
            h_ref[:, b * IN_NB:(b + 1) * IN_NB] = _dot(x, w_ref[b])

    in_specs = [_rows(TM, D_MODEL), pl.BlockSpec((N_DEV, None, D_MODEL, IN_NB), lambda i: (0, l, 0, 0))]
    out_shape = _sds((T, 4 * MIX), F32)
    if not ride:
        return _pcall(body, name=name, grid=(ni,), in_specs=in_specs, out_specs=_rows(TM, 4 * MIX),
                      out_shape=out_shape, sem=("parallel",))(xh, win8)
    first = lambda: pl.program_id(0) == 0
    last = lambda: pl.program_id(0) == ni - 1
    outs = _pcall(_riding(body, 2, 1, len(ride), first, last, gather=True), name=name, grid=(ni,),
                  in_specs=in_specs + [_ANY] * len(ride), out_specs=[_rows(TM, 4 * MIX)] + [_ANY] * len(ride),
                  out_shape=[out_shape] + [_sds((N_DEV,) + r.shape, r.dtype) for r in ride],
                  scratch=list(_A2A_SEMS) * len(ride))(xh, win8, *ride)
    return outs[0], outs[1:]


def _tile_of(n, cap):
    best = None
    for t in range(128, min(n, cap) + 1, 128):
        if n % t == 0:
            best = t
    assert best is not None, n
    return best


def _mm_tn(a, b, name, a_layer=None, blocked_n=None):
    T, M = a.shape[-2:]
    N = b.shape[1]
    tm = _tile_of(M, 1408)
    nbs = max(1, 512 // blocked_n) if blocked_n else 1
    tn = blocked_n * nbs if blocked_n else _tile_of(N, 1024 if tm <= 512 else 512)
    tk = min(T, TK_DW)
    nk = T // tk

    def body(a_ref, b_ref, o_ref, acc):
        k = pl.program_id(2)

        @pl.when(k == 0)
        def _():
            acc[...] = jnp.zeros_like(acc)

        acc[...] += _dot_tn(a_ref[...].astype(BF16), b_ref[...].astype(BF16))

        @pl.when(k == nk - 1)
        def _():
            if blocked_n:
                for sb in range(nbs):
                    o_ref[sb] = acc[:, sb * blocked_n:(sb + 1) * blocked_n].astype(BF16)
            else:
                o_ref[...] = acc[...].astype(BF16)

    if a_layer is None:
        a_spec = pl.BlockSpec((tk, tm), lambda i, j, k: (k, i))
    else:
        a_spec = pl.BlockSpec((None, tk, tm), lambda i, j, k: (a_layer, k, i))
    if blocked_n:
        o_spec = pl.BlockSpec((nbs, tm, blocked_n), lambda i, j, k: (j, i, 0))
        o_shape = _sds((N // blocked_n, M, blocked_n), BF16)
    else:
        o_spec = pl.BlockSpec((tm, tn), lambda i, j, k: (i, j))
        o_shape = _sds((M, N), BF16)
    return _pcall(body, name=name, grid=(M // tm, N // tn, nk),
                  in_specs=[a_spec, pl.BlockSpec((tk, tn), lambda i, j, k: (k, j))],
                  out_specs=o_spec, out_shape=o_shape, scratch=[pltpu.VMEM((tm, tn), F32)],
                  sem=("parallel", "parallel", "arbitrary"))(a, b)


def _mm_tn_bblk(a, b3, name, per_step=2):
    T, M = a.shape
    NB, _, n = b3.shape
    tk = min(T, TK_DW)
    nk = T // tk

    def body(a_ref, b_ref, o_ref, acc):
        k = pl.program_id(1)

        @pl.when(k == 0)
        def _():
            acc[...] = jnp.zeros_like(acc)

        av = a_ref[...]
        for s in range(per_step):
            acc[s] += _dot_tn(av, b_ref[s])

        @pl.when(k == nk - 1)
        def _():
            o_ref[...] = acc[...].astype(BF16)

    return _pcall(body, name=name, grid=(NB // per_step, nk),
                  in_specs=[pl.BlockSpec((tk, M), lambda j, k: (k, 0)),
                            pl.BlockSpec((per_step, tk, n), lambda j, k: (j, k, 0))],
                  out_specs=pl.BlockSpec((per_step, M, n), lambda j, k: (j, 0, 0)),
                  out_shape=_sds((NB, M, n), BF16), scratch=[pltpu.VMEM((per_step, M, n), F32)],
                  sem=("parallel", "arbitrary"))(a, b3)


def _mm_tn_ablk(a3, b, name):
    NA, T, m = a3.shape
    N = b.shape[1]
    tn = _tile_of(N, 1024)
    tk = min(T, TK_DW)
    nk = T // tk

    def body(a_ref, b_ref, o_ref, acc):
        k = pl.program_id(2)

        @pl.when(k == 0)
        def _():
            acc[...] = jnp.zeros_like(acc)

        acc[...] += _dot_tn(a_ref[...], b_ref[...])

        @pl.when(k == nk - 1)
        def _():
            o_ref[...] = acc[...].astype(BF16)

    return _pcall(body, name=name, grid=(NA, N // tn, nk),
                  in_specs=[pl.BlockSpec((None, tk, m), lambda j, n, k: (j, k, 0)),
                            pl.BlockSpec((tk, tn), lambda j, n, k: (k, n))],
                  out_specs=pl.BlockSpec((None, m, tn), lambda j, n, k: (j, 0, n)),
                  out_shape=_sds((NA, m, N), BF16), scratch=[pltpu.VMEM((m, tn), F32)],
                  sem=("parallel", "parallel", "arbitrary"))(a3, b)


def _mm_out_ln(ya, yb, wo, l, x0, g, b, name):
    T = x0.shape[0]

    def body(ya_ref, yb_ref, w_ref, x0_ref, g_ref, b_ref, xh_ref, rs_ref, x1_ref):
        r = ALPHA * x0_ref[...] + _dot(ya_ref[...], w_ref[0:MIX, :]) + _dot(yb_ref[...], w_ref[MIX:, :])
        xhat, rstd, x1 = _ln_fwd(r, g_ref[...], b_ref[...])
        xh_ref[...] = xhat
        rs_ref[...] = rstd
        x1_ref[...] = x1.astype(BF16)

    vec = _layer((1, D_MODEL), g[1])
    return _pcall(body, name=name, grid=(T // TM,),
                  in_specs=[_rows(TM, MIX), _rows(TM, MIX), _layer((D_MODEL, D_MODEL), l), _rows(TM, D_MODEL),
                            vec, vec],
                  out_specs=[_rows(TM, D_MODEL), _rows(TM, 1), _rows(TM, D_MODEL)],
                  out_shape=[_sds((T, D_MODEL), F32), _sds((T, 1), F32), _sds((T, D_MODEL), BF16)],
                  sem=("parallel",))(ya, yb, wo, x0, g[0], b[0])


def _mm_up(x1h, wup, l, name, ride=()):
    T = x1h.shape[0]
    tm = min(T, TM_FF)
    ni = T // tm

    def body(x_ref, w_ref, hf_ref, a_ref):
        x = x_ref[...]
        g = _dot(x, w_ref[0])
        u = _dot(x, w_ref[1])
        hf_ref[0] = g.astype(BF16)
        hf_ref[1] = u.astype(BF16)
        a_ref[...] = (g * _sigmoid(g) * u).astype(BF16)

    in_specs = [pl.BlockSpec((tm, D_MODEL), lambda j, i: (i, 0)),
                pl.BlockSpec((2, None, None, D_MODEL, FF_NB), lambda j, i: (0, j, l, 0, 0))]
    out_specs = [pl.BlockSpec((2, None, tm, FF_NB), lambda j, i: (0, j, i, 0)),
                 pl.BlockSpec((None, tm, FF_NB), lambda j, i: (j, i, 0))]
    out_shape = [_sds((2, FF_J, T, FF_NB), BF16), _sds((FF_J, T, FF_NB), BF16)]
    if not ride:
        return _pcall(body, name=name, grid=(FF_J, ni), in_specs=in_specs, out_specs=out_specs,
                      out_shape=out_shape, sem=("parallel", "parallel"))(x1h, wup), ()
    first = lambda: (pl.program_id(0) == 0) & (pl.program_id(1) == 0)
    last = lambda: (pl.program_id(0) == FF_J - 1) & (pl.program_id(1) == ni - 1)
    outs = _pcall(_riding(body, 2, 2, len(ride), first, last, gather=True), name=name, grid=(FF_J, ni),
                  in_specs=in_specs + [_ANY] * len(ride), out_specs=out_specs + [_ANY] * len(ride),
                  out_shape=out_shape + [_sds((N_DEV,) + r.shape, r.dtype) for r in ride],
                  scratch=list(_A2A_SEMS) * len(ride))(x1h, wup, *ride)
    return outs[:2], outs[2:]


def _mm_down_ln(a3, wd4, l, xhat1, g1, b1, g2, b2, name, ride=()):
    T = a3.shape[1]
    ni = T // TM

    def body(a_ref, w_ref, xh1_ref, g1_ref, b1_ref, g2_ref, b2_ref, xh_ref, rs_ref, x2_ref):
        x1 = xh1_ref[...] * g1_ref[...] + b1_ref[...]
        r = ALPHA * x1
        for j in range(FF_J):
            r = r + _dot(a_ref[j], w_ref[j])
        xhat, rstd, x2 = _ln_fwd(r, g2_ref[...], b2_ref[...])
        xh_ref[...] = xhat
        rs_ref[...] = rstd
        x2_ref[...] = x2.astype(BF16)

    vec = _layer((1, D_MODEL), g1[1])
    in_specs = [pl.BlockSpec((FF_J, TM, FF_NB), lambda i: (0, i, 0)),
                _layer((FF_J, FF_NB, D_MODEL), l), _rows(TM, D_MODEL), vec, vec, vec, vec]
    out_specs = [_rows(TM, D_MODEL), _rows(TM, 1), _rows(TM, D_MODEL)]
    out_shape = [_sds((T, D_MODEL), F32), _sds((T, 1), F32), _sds((T, D_MODEL), BF16)]
    operands = (a3, wd4, xhat1, g1[0], b1[0], g2[0], b2[0])
    if not ride:
        return _pcall(body, name=name, grid=(ni,), in_specs=in_specs, out_specs=out_specs, out_shape=out_shape,
                      sem=("parallel",))(*operands), ()
    first = lambda: pl.program_id(0) == 0
    last = lambda: pl.program_id(0) == ni - 1
    outs = _pcall(_riding(body, 7, 3, len(ride), first, last, gather=True), name=name, grid=(ni,),
                  in_specs=in_specs + [_ANY] * len(ride), out_specs=out_specs + [_ANY] * len(ride),
                  out_shape=out_shape + [_sds((N_DEV,) + r.shape, r.dtype) for r in ride],
                  scratch=list(_A2A_SEMS) * len(ride))(*operands, *ride)
    return outs[:3], outs[3:]


def _mm_ple(x2h, xhat2, g2, b2, p, wg, bg, wp, l, name, lw=None):
    lw = l if lw is None else lw
    T = x2h.shape[0]

    def body(x2h_ref, xh_ref, g2_ref, b2_ref, p_ref, wg_ref, bg_ref, wp_ref, o_ref, oh_ref):
        x2 = xh_ref[...] * g2_ref[...] + b2_ref[...]
        gate = _sigmoid(_dot(x2h_ref[...], wg_ref[...]) + bg_ref[...])
        pp = _dot(p_ref[...].astype(BF16), wp_ref[...])
        x3 = x2 + gate * pp
        o_ref[...] = x3
        oh_ref[...] = x3.astype(BF16)

    vec = _layer((1, D_MODEL), l)
    return _pcall(body, name=name, grid=(T // TM,),
                  in_specs=[_rows(TM, D_MODEL), _rows(TM, D_MODEL), vec, vec,
                            pl.BlockSpec((None, TM, D_PLE), lambda i: (l, i, 0)),
                            _layer((D_MODEL, D_MODEL), lw), vec, _layer((D_PLE, D_MODEL), l)],
                  out_specs=[_rows(TM, D_MODEL), _rows(TM, D_MODEL)],
                  out_shape=[_sds((T, D_MODEL), F32), _sds((T, D_MODEL), BF16)],
                  sem=("parallel",))(x2h, xhat2, g2, b2, p, wg, bg, wp)


def _loss_head(x3, tgt, name):
    T = x3.shape[0]

    def body(x_ref, t_ref, dx_ref, l_ref):
        e = x_ref[...] - t_ref[...]
        dx_ref[...] = e * (1.0 / D_MODEL)

        @pl.when(pl.program_id(0) == 0)
        def _():
            l_ref[...] = jnp.zeros_like(l_ref)

        l_ref[...] += (0.5 / D_MODEL) * jnp.sum(e * e).reshape(1, 1)

    return _pcall(body, name=name, grid=(T // TM,),
                  in_specs=[_rows(TM, D_MODEL), _rows(TM, D_MODEL)],
                  out_specs=[_rows(TM, D_MODEL), _full((1, 1))],
                  out_shape=[_sds((T, D_MODEL), F32), _sds((1, 1), F32)])(x3, tgt)


def _ple_bwd(dx3, x2h, p, wg, bg, wp, l, xhat2, rstd2, g2, name, lw=None):
    T = dx3.shape[0]
    lw = l if lw is None else lw

    def body(dx3_ref, x2h_ref, p_ref, wg_ref, bg_ref, wp_ref, xh_ref, rs_ref, g2_ref,
             dr_ref, drh_ref, dpre_ref, dpp_ref, dbg_ref, dg_ref, db_ref):
        dx3 = dx3_ref[...]
        gate = _sigmoid(_dot(x2h_ref[...], wg_ref[...]) + bg_ref[...])
        pp = _dot(p_ref[...].astype(BF16), wp_ref[...])
        dpre = dx3 * pp * gate * (1.0 - gate)
        dpreh = dpre.astype(BF16)
        dpre_ref[...] = dpreh
        dpp_ref[...] = (dx3 * gate).astype(BF16)
        dx2 = dx3 + _dot_nt(dpreh, wg_ref[...])
        xhat = xh_ref[...]
        dr = _ln_bwd(dx2, xhat, rs_ref[...], g2_ref[...])
        dr_ref[...] = dr
        drh_ref[...] = dr.astype(BF16)

        @pl.when(pl.program_id(0) == 0)
        def _():
            dbg_ref[...] = jnp.zeros_like(dbg_ref)
            dg_ref[...] = jnp.zeros_like(dg_ref)
            db_ref[...] = jnp.zeros_like(db_ref)

        dbg_ref[...] += _colsum(dpre)
        dg_ref[...] += _colsum(dx2 * xhat)
        db_ref[...] += _colsum(dx2)

    vec = _layer((1, D_MODEL), l)
    acc = _full((1, D_MODEL))
    big = _rows(TM, D_MODEL)
    return _pcall(body, name=name, grid=(T // TM,),
                  in_specs=[big, big, pl.BlockSpec((None, TM, D_PLE), lambda i: (l, i, 0)),
                            _layer((D_MODEL, D_MODEL), lw), vec, _layer((D_PLE, D_MODEL), l),
                            big, _rows(TM, 1), vec],
                  out_specs=[big, big, big, big, acc, acc, acc],
                  out_shape=[_sds((T, D_MODEL), F32), _sds((T, D_MODEL), BF16), _sds((T, D_MODEL), BF16),
                             _sds((T, D_MODEL), BF16), _sds((1, D_MODEL), F32), _sds((1, D_MODEL), F32),
                             _sds((1, D_MODEL), F32)])(dx3, x2h, p, wg, bg, wp, xhat2, rstd2, g2)


def _ffn_bwd1(dr2h, wd4, l, hf, name, ride=()):
    T = dr2h.shape[0]
    tm = min(T, TM_FF)
    ni = T // tm

    def body(d_ref, w_ref, hf_ref, o_ref):
        da = _dot_nt(d_ref[...], w_ref[...])
        g = hf_ref[0].astype(F32)
        u = hf_ref[1].astype(F32)
        sg = _sigmoid(g)
        o_ref[0] = (da * u * (sg * (1.0 + g * (1.0 - sg)))).astype(BF16)
        o_ref[1] = (da * (g * sg)).astype(BF16)

    blk = pl.BlockSpec((2, None, tm, FF_NB), lambda j, i: (0, j, i, 0))
    in_specs = [pl.BlockSpec((tm, D_MODEL), lambda j, i: (i, 0)),
                pl.BlockSpec((None, None, FF_NB, D_MODEL), lambda j, i: (l, j, 0, 0)), blk]
    out_shape = _sds((2, FF_J, T, FF_NB), BF16)
    if not ride:
        return _pcall(body, name=name, grid=(FF_J, ni), in_specs=in_specs, out_specs=blk, out_shape=out_shape,
                      sem=("parallel", "parallel"))(dr2h, wd4, hf), ()
    first = lambda: (pl.program_id(0) == 0) & (pl.program_id(1) == 0)
    last = lambda: (pl.program_id(0) == FF_J - 1) & (pl.program_id(1) == ni - 1)
    outs = _pcall(_riding(body, 3, 1, len(ride), first, last), name=name, grid=(FF_J, ni),
                  in_specs=in_specs + [_ANY] * len(ride), out_specs=[blk] + [_ANY] * len(ride),
                  out_shape=[out_shape] + [_sds(r.shape, r.dtype) for r in ride],
                  scratch=list(_A2A_SEMS) * len(ride))(dr2h, wd4, hf, *ride)
    return outs[0], outs[1:]


_TM_B2 = 256


def _ffn_bwd2(dhf, wup, l, dr2, xhat1, rstd1, g1, name, ride=()):
    T = dr2.shape[0]
    tm = min(T, _TM_B2)
    ni = T // tm

    def body(dh_ref, w_ref, dr2_ref, xh_ref, rs_ref, g_ref, dr_ref, drh_ref, dg_ref, db_ref):
        dx1 = ALPHA * dr2_ref[...]
        for s in range(2):
            for j in range(FF_J):
                dx1 = dx1 + _dot_nt(dh_ref[s, j], w_ref[s, j])
        xhat = xh_ref[...]
        dr = _ln_bwd(dx1, xhat, rs_ref[...], g_ref[...])
        dr_ref[...] = dr
        drh_ref[...] = dr.astype(BF16)

        @pl.when(pl.program_id(0) == 0)
        def _():
            dg_ref[...] = jnp.zeros_like(dg_ref)
            db_ref[...] = jnp.zeros_like(db_ref)

        dg_ref[...] += _colsum(dx1 * xhat)
        db_ref[...] += _colsum(dx1)

    big = _rows(tm, D_MODEL)
    acc = _full((1, D_MODEL))
    in_specs = [pl.BlockSpec((2, FF_J, tm, FF_NB), lambda i: (0, 0, i, 0)),
                pl.BlockSpec((2, FF_J, None, D_MODEL, FF_NB), lambda i: (0, 0, l, 0, 0)),
                big, big, _rows(tm, 1), _layer((1, D_MODEL), g1[1])]
    out_specs = [big, big, acc, acc]
    out_shape = [_sds((T, D_MODEL), F32), _sds((T, D_MODEL), BF16), _sds((1, D_MODEL), F32),
                 _sds((1, D_MODEL), F32)]
    operands = (dhf, wup, dr2, xhat1, rstd1, g1[0])
    if not ride:
        return _pcall(body, name=name, grid=(ni,), in_specs=in_specs, out_specs=out_specs,
                      out_shape=out_shape)(*operands), ()
    first = lambda: pl.program_id(0) == 0
    last = lambda: pl.program_id(0) == ni - 1
    outs = _pcall(_riding(body, 6, 4, len(ride), first, last), name=name, grid=(ni,),
                  in_specs=in_specs + [_ANY] * len(ride), out_specs=out_specs + [_ANY] * len(ride),
                  out_shape=out_shape + [_sds(r.shape, r.dtype) for r in ride],
                  scratch=list(_A2A_SEMS) * len(ride))(*operands, *ride)
    return outs[:4], outs[4:]


def _out_bwd(dr1h, wo, l, name):
    T = dr1h.shape[0]

    def body(d_ref, w_ref, da_ref, db_ref):
        d = d_ref[...]
        da_ref[...] = _dot_nt(d, w_ref[0:MIX, :])
        db_ref[...] = _dot_nt(d, w_ref[MIX:, :])

    return _pcall(body, name=name, grid=(T // TM,),
                  in_specs=[_rows(TM, D_MODEL), _layer((D_MODEL, D_MODEL), l)],
                  out_specs=[_rows(TM, MIX), _rows(TM, MIX)],
                  out_shape=[_sds((T, MIX), F32), _sds((T, MIX), F32)],
                  sem=("parallel",))(dr1h, wo)


def _in_bwd(dh, win8, l, dr1, name):
    T = dr1.shape[0]

    def body(dh_ref, w_ref, dr_ref, o_ref):
        acc = ALPHA * dr_ref[...]
        for b in range(N_DEV):
            acc = acc + _dot_nt(dh_ref[:, b * IN_NB:(b + 1) * IN_NB], w_ref[b])
        o_ref[...] = acc

    return _pcall(body, name=name, grid=(T // TM,),
                  in_specs=[_rows(TM, 4 * MIX), pl.BlockSpec((N_DEV, None, D_MODEL, IN_NB), lambda i: (0, l, 0, 0)),
                            _rows(TM, D_MODEL)],
                  out_specs=_rows(TM, D_MODEL), out_shape=_sds((T, D_MODEL), F32),
                  sem=("parallel",))(dh, win8, dr1)


def _s5_operators(lre, lim, log_dt, bre, bim, cre, cim, dskip):
    G, P, H, LC = S5_GROUPS, S5_STATE, S5_GROUP, S5_LC
    dt = jnp.exp(log_dt)[:, None]
    mag = jnp.exp(lre * dt)
    ang = lim * dt
    lb_re = mag * jnp.cos(ang)
    lb_im = mag * jnp.sin(ang)
    den = lre * lre + lim * lim
    nr = lb_re - 1.0
    ni = lb_im
    r_re = (nr * lre + ni * lim) / den
    r_im = (ni * lre - nr * lim) / den
    bb_re = r_re[..., None] * bre - r_im[..., None] * bim
    bb_im = r_re[..., None] * bim + r_im[..., None] * bre
    k = jnp.arange(LC + 1, dtype=F32)[:, None, None]
    pmag = jnp.exp(k * (lre * dt)[None])
    pang = k * ang[None]
    pw_re = pmag * jnp.cos(pang)
    pw_im = pmag * jnp.sin(pang)
    cp_re = cre[None] * pw_re[:, :, None, :] - cim[None] * pw_im[:, :, None, :]
    cp_im = cre[None] * pw_im[:, :, None, :] + cim[None] * pw_re[:, :, None, :]
    kk = (jnp.einsum('kghp,gpj->kghj', cp_re[:LC], bb_re, precision=HI)
          - jnp.einsum('kghp,gpj->kghj', cp_im[:LC], bb_im, precision=HI))
    dmat = dskip.reshape(G, H)[:, :, None] * jnp.eye(H, dtype=F32)[None]
    kk = jnp.concatenate([kk[:1] + dmat[None], kk[1:]], axis=0)
    r0 = jnp.transpose(kk, (1, 3, 0, 2)).reshape(G, H, LC * H)
    qt = jnp.stack([cp_re[1:], -cp_im[1:]], axis=0)
    qt = jnp.transpose(qt, (2, 0, 4, 1, 3)).reshape(G, 2 * P, LC * H)
    pb_re = pw_re[:LC, :, :, None] * bb_re[None] - pw_im[:LC, :, :, None] * bb_im[None]
    pb_im = pw_re[:LC, :, :, None] * bb_im[None] + pw_im[:LC, :, :, None] * bb_re[None]
    pm = jnp.stack([pb_re[::-1], pb_im[::-1]], axis=0)
    pm = jnp.transpose(pm, (2, 1, 4, 0, 3)).reshape(G, LC * H, 2 * P)
    a_re = pw_re[LC]
    a_im = pw_im[LC]
    a1 = jnp.concatenate([a_re, a_re], axis=-1)
    a2 = jnp.concatenate([-a_im, a_im], axis=-1)
    return r0, qt, pm, a1, a2


_CHUNK_BLOCK = 128
_LANE_GROUPS = 128 // S5_GROUP


def _to_chunks(u, name):
    T = u.shape[0]
    nc = T // S5_LC
    cb = min(nc, _CHUNK_BLOCK)

    def body(x_ref, o_ref, scr):
        for s in range(S5_LC):
            xt = x_ref[pl.ds(s, cb, stride=S5_LC), :].T
            for g in range(_LANE_GROUPS):
                scr[g, s * S5_GROUP:(s + 1) * S5_GROUP, :] = xt[g * S5_GROUP:(g + 1) * S5_GROUP, :]
        for g in range(_LANE_GROUPS):
            o_ref[g] = scr[g].T

    return _pcall(body, name=name, grid=(nc // cb, S5_GROUPS // _LANE_GROUPS),
                  in_specs=[pl.BlockSpec((cb * S5_LC, 128), lambda i, k: (i, k))],
                  out_specs=pl.BlockSpec((_LANE_GROUPS, cb, S5_LW), lambda i, k: (k, i, 0)),
                  out_shape=_sds((S5_GROUPS, nc, S5_LW), F32),
                  scratch=[pltpu.VMEM((_LANE_GROUPS, S5_LW, cb), F32)], sem=("parallel", "parallel"))(u)


def _from_chunks(m, name):
    nc = m.shape[1]
    cb = min(nc, _CHUNK_BLOCK)

    def body(m_ref, y_ref, scr, rows):
        for g in range(_LANE_GROUPS):
            scr[g] = m_ref[g].T
        for s in range(S5_LC):
            for g in range(_LANE_GROUPS):
                rows[g * S5_GROUP:(g + 1) * S5_GROUP, :] = scr[g, s * S5_GROUP:(s + 1) * S5_GROUP, :]
            y_ref[pl.ds(s, cb, stride=S5_LC), :] = rows[...].T

    return _pcall(body, name=name, grid=(nc // cb, S5_GROUPS // _LANE_GROUPS),
                  in_specs=[pl.BlockSpec((_LANE_GROUPS, cb, S5_LW), lambda i, k: (k, i, 0))],
                  out_specs=pl.BlockSpec((cb * S5_LC, 128), lambda i, k: (i, k)),
                  out_shape=_sds((nc * S5_LC, MIX), F32),
                  scratch=[pltpu.VMEM((_LANE_GROUPS, S5_LW, cb), F32), pltpu.VMEM((128, cb), F32)],
                  sem=("parallel", "parallel"))(m)


def _gspec(r, c):
    return pl.BlockSpec((None, r, c), lambda g: (g, 0, 0))


def _s5_chunk_state(umat, pm, name):
    G, nc, _ = umat.shape

    def body(u_ref, p_ref, o_ref):
        o_ref[...] = _mm3(_dot, _split(u_ref[...]), _split(p_ref[...]))

    return _pcall(body, name=name, grid=(G,), in_specs=[_gspec(nc, S5_LW), _gspec(S5_LW, 128)],
                  out_specs=_gspec(nc, 128), out_shape=_sds((G, nc, 128), F32), sem=("parallel",))(umat, pm)


_SCAN_G = 8
_SCAN_UNROLL = 8


def _s5_scan_fwd(s_t, a1, a2, name):
    nc, G, _ = s_t.shape

    def body(s_ref, a1_ref, a2_ref, o_ref, sb_ref):
        sb_ref[...] = pltpu.roll(s_ref[...], 64, 2)
        a1v = a1_ref[...]
        a2v = a2_ref[...]

        def step(c, carry):
            x, xb = carry
            o_ref[c] = x
            return a1v * x + a2v * xb + s_ref[c], a1v * xb - a2v * x + sb_ref[c]

        zero = jnp.zeros((_SCAN_G, 128), F32)
        lax.fori_loop(0, nc, step, (zero, zero), unroll=_SCAN_UNROLL)

    blk = pl.BlockSpec((nc, _SCAN_G, 128), lambda g: (0, g, 0))
    vec = pl.BlockSpec((_SCAN_G, 128), lambda g: (g, 0))
    return _pcall(body, name=name, grid=(G // _SCAN_G,), in_specs=[blk, vec, vec], out_specs=blk,
                  out_shape=_sds((nc, G, 128), F32), scratch=[pltpu.VMEM((nc, _SCAN_G, 128), F32)],
                  sem=("parallel",))(s_t, a1, a2)


def _s5_scan_bwd(dxp_t, xp_t, a1, a2, name):
    nc, G, _ = dxp_t.shape

    def body(dx_ref, x_ref, a1_ref, a2_ref, ds_ref, da1_ref, da2_ref, dxb_ref, xb_ref):
        dxb_ref[...] = pltpu.roll(dx_ref[...], 64, 2)
        xb_ref[...] = pltpu.roll(x_ref[...], 64, 2)
        a1v = a1_ref[...]
        a2v = a2_ref[...]
        zero = jnp.zeros((_SCAN_G, 128), F32)

        def step(n, carry):
            gc, gb, d1, d2 = carry
            c = nc - 1 - n
            ds_ref[c] = gc
            d1 = d1 + gc * x_ref[c]
            d2 = d2 + gc * xb_ref[c]
            return dx_ref[c] + a1v * gc - a2v * gb, dxb_ref[c] + a1v * gb + a2v * gc, d1, d2

        _, _, d1, d2 = lax.fori_loop(0, nc, step, (zero, zero, zero, zero), unroll=_SCAN_UNROLL)
        da1_ref[...] = d1
        da2_ref[...] = d2

    blk = pl.BlockSpec((nc, _SCAN_G, 128), lambda g: (0, g, 0))
    vec = pl.BlockSpec((_SCAN_G, 128), lambda g: (g, 0))
    return _pcall(body, name=name, grid=(G // _SCAN_G,), in_specs=[blk, blk, vec, vec],
                  out_specs=[blk, vec, vec],
                  out_shape=[_sds((nc, G, 128), F32), _sds((G, 128), F32), _sds((G, 128), F32)],
                  scratch=[pltpu.VMEM((nc, _SCAN_G, 128), F32)] * 2, sem=("parallel",))(dxp_t, xp_t, a1, a2)


def _toeplitz_rows(r0, mt):
    lane = lax.broadcasted_iota(jnp.int32, (S5_GROUP, S5_LW), 1)
    mt[0:S5_GROUP, :] = r0
    for s in range(1, S5_LC):
        mt[s * S5_GROUP:(s + 1) * S5_GROUP, :] = jnp.where(lane >= s * S5_GROUP, pltpu.roll(r0, s * S5_GROUP, 1), 0.0)


def _toeplitz_rows_t(dmt):
    lane = lax.broadcasted_iota(jnp.int32, (S5_GROUP, S5_LW), 1)
    acc = dmt[0:S5_GROUP, :]
    for s in range(1, S5_LC):
        blk = dmt[s * S5_GROUP:(s + 1) * S5_GROUP, :]
        acc = acc + jnp.where(lane < S5_LW - s * S5_GROUP, pltpu.roll(blk, S5_LW - s * S5_GROUP, 1), 0.0)
    return acc


def _s5_output(umat, xprev, r0, qt, name):
    G, nc, _ = umat.shape

    def body(u_ref, x_ref, r_ref, q_ref, o_ref, mt):
        _toeplitz_rows(r_ref[...], mt)
        o_ref[...] = (_mm3(_dot, _split(u_ref[...]), _split(mt[...]))
                      + _mm3(_dot, _split(x_ref[...]), _split(q_ref[...])))

    return _pcall(body, name=name, grid=(G,),
                  in_specs=[_gspec(nc, S5_LW), _gspec(nc, 128), _gspec(S5_GROUP, S5_LW), _gspec(128, S5_LW)],
                  out_specs=_gspec(nc, S5_LW), out_shape=_sds((G, nc, S5_LW), F32),
                  scratch=[pltpu.VMEM((S5_LW, S5_LW), F32)], sem=("parallel",))(umat, xprev, r0, qt)


def _s5_bwd_state(dymat, qt, name):
    G, nc, _ = dymat.shape

    def body(d_ref, q_ref, o_ref):
        o_ref[...] = _mm3(_dot_nt, _split(d_ref[...]), _split(q_ref[...]))

    return _pcall(body, name=name, grid=(G,), in_specs=[_gspec(nc, S5_LW), _gspec(128, S5_LW)],
                  out_specs=_gspec(nc, 128), out_shape=_sds((G, nc, 128), F32), sem=("parallel",))(dymat, qt)


def _s5_bwd_main(umat, xprev, dymat, ds, r0, pm, name):
    G, nc, _ = umat.shape

    def body(u_ref, x_ref, dy_ref, ds_ref, r_ref, p_ref, du_ref, dr_ref, dq_ref, dp_ref, mt):
        u = _split(u_ref[...])
        dy = _split(dy_ref[...])
        dsv = _split(ds_ref[...])
        _toeplitz_rows(r_ref[...], mt)
        du_ref[...] = _mm3(_dot_nt, dy, _split(mt[...])) + _mm3(_dot_nt, dsv, _split(p_ref[...]))
        dq_ref[...] = _mm3(_dot_tn, _split(x_ref[...]), dy)
        dp_ref[...] = _mm3(_dot_tn, u, dsv)
        mt[...] = _mm3(_dot_tn, u, dy)
        dr_ref[...] = _toeplitz_rows_t(mt)

    return _pcall(body, name=name, grid=(G,),
                  in_specs=[_gspec(nc, S5_LW), _gspec(nc, 128), _gspec(nc, S5_LW), _gspec(nc, 128),
                            _gspec(S5_GROUP, S5_LW), _gspec(S5_LW, 128)],
                  out_specs=[_gspec(nc, S5_LW), _gspec(S5_GROUP, S5_LW), _gspec(128, S5_LW), _gspec(S5_LW, 128)],
                  out_shape=[_sds((G, nc, S5_LW), F32), _sds((G, S5_GROUP, S5_LW), F32), _sds((G, 128, S5_LW), F32),
                             _sds((G, S5_LW, 128), F32)],
                  scratch=[pltpu.VMEM((S5_LW, S5_LW), F32)], sem=("parallel",))(umat, xprev, dymat, ds, r0, pm)


def _glu_fwd(y, wglu, bglu, l, name):
    T = y.shape[0]

    def body(y_ref, w_ref, b_ref, o_ref, g_ref):
        g = _gelu(y_ref[...])
        gh = g.astype(BF16)
        z = _dot(gh, w_ref[...]) + b_ref[...]
        o_ref[...] = (g * _sigmoid(z)).astype(BF16)
        g_ref[...] = gh

    return _pcall(body, name=name, grid=(T // TM,),
                  in_specs=[_rows(TM, MIX), _layer((MIX, MIX), l), _layer((1, MIX), l)],
                  out_specs=[_rows(TM, MIX), _rows(TM, MIX)],
                  out_shape=[_sds((T, MIX), BF16), _sds((T, MIX), BF16)], sem=("parallel",))(y, wglu, bglu)


def _glu_bwd(y, dout, wglu, bglu, l, name):
    T = y.shape[0]

    def body(y_ref, do_ref, w_ref, b_ref, dy_ref, dz_ref, db_ref):
        yv = y_ref[...]
        do = do_ref[...]
        g = _gelu(yv)
        s = _sigmoid(_dot(g.astype(BF16), w_ref[...]) + b_ref[...])
        dz = do * g * s * (1.0 - s)
        dzh = dz.astype(BF16)
        dz_ref[...] = dzh
        dg = do * s + _dot_nt(dzh, w_ref[...])
        dy_ref[...] = dg * _gelu_grad(yv)

        @pl.when(pl.program_id(0) == 0)
        def _():
            db_ref[...] = jnp.zeros_like(db_ref)

        db_ref[...] += _colsum(dz)

    return _pcall(body, name=name, grid=(T // TM,),
                  in_specs=[_rows(TM, MIX), _rows(TM, MIX), _layer((MIX, MIX), l), _layer((1, MIX), l)],
                  out_specs=[_rows(TM, MIX), _rows(TM, MIX), _full((1, MIX))],
                  out_shape=[_sds((T, MIX), F32), _sds((T, MIX), BF16), _sds((1, MIX), F32)])(y, dout, wglu, bglu)


def _prev_rows(T, h, s=0):
    return pl.BlockSpec((h, MIX), lambda i: (jnp.maximum(i * (TM // h) - 1, 0), s))


def _next_rows(T, h, s=0):
    return pl.BlockSpec((h, MIX), lambda i: (jnp.minimum((i + 1) * (TM // h), T // h - 1), s))


def _conv_fwd(h, w, l, name):
    T = h.shape[0]

    def body(b_ref, c_ref, x_ref, ch_ref, xh_ref, w_ref, o_ref, ext):
        i = pl.program_id(0)
        z = c_ref[...] * x_ref[...]
        ext[0:8, :] = jnp.where(i > 0, ch_ref[...] * xh_ref[...], 0.0)
        ext[8:, :] = z
        y = (w_ref[0:1, :] * ext[pl.ds(6, TM), :] + w_ref[1:2, :] * ext[pl.ds(7, TM), :] + w_ref[2:3, :] * z)
        o_ref[...] = (b_ref[...] * y).astype(BF16)

    return _pcall(body, name=name, grid=(T // TM,),
                  in_specs=[_cols(TM, 1), _cols(TM, 2), _cols(TM, 3), _prev_rows(T, 8, 2), _prev_rows(T, 8, 3),
                            _layer((3, MIX), l)],
                  out_specs=_rows(TM, MIX), out_shape=_sds((T, MIX), BF16),
                  scratch=[pltpu.VMEM((TM + 8, MIX), F32)], sem=("parallel",))(h, h, h, h, h, w)


def _conv_bwd(dout, h, dua, w, l, name):
    T = h.shape[0]
    nb = T // TM

    def body(do_ref, b_ref, c_ref, x_ref, ch_ref, xh_ref, don_ref, bn_ref, du_ref, w_ref,
             dh_ref, dw_ref, ext, ext2):
        i = pl.program_id(0)
        c = c_ref[...]
        x = x_ref[...]
        z = c * x
        ext[0:8, :] = jnp.where(i > 0, ch_ref[...] * xh_ref[...], 0.0)
        ext[8:, :] = z
        zm2 = ext[pl.ds(6, TM), :]
        zm1 = ext[pl.ds(7, TM), :]
        w0 = w_ref[0:1, :]
        w1 = w_ref[1:2, :]
        w2 = w_ref[2:3, :]
        y = w0 * zm2 + w1 * zm1 + w2 * z
        do = do_ref[...]
        dy = do * b_ref[...]
        ext2[0:TM, :] = dy
        ext2[TM:, :] = jnp.where(i < nb - 1, don_ref[...] * bn_ref[...], 0.0)
        dz = w2 * dy + w1 * ext2[pl.ds(1, TM), :] + w0 * ext2[pl.ds(2, TM), :]
        dh_ref[:, 0:MIX] = du_ref[...].astype(BF16)
        dh_ref[:, MIX:2 * MIX] = (do * y).astype(BF16)
        dh_ref[:, 2 * MIX:3 * MIX] = (dz * x).astype(BF16)
        dh_ref[:, 3 * MIX:] = (dz * c).astype(BF16)

        @pl.when(i == 0)
        def _():
            dw_ref[...] = jnp.zeros_like(dw_ref)

        dw_ref[0:1, :] += _colsum(dy * zm2)
        dw_ref[1:2, :] += _colsum(dy * zm1)
        dw_ref[2:3, :] += _colsum(dy * z)

    return _pcall(body, name=name, grid=(nb,),
                  in_specs=[_rows(TM, MIX), _cols(TM, 1), _cols(TM, 2), _cols(TM, 3), _prev_rows(T, 8, 2),
                            _prev_rows(T, 8, 3), _next_rows(T, 8), _next_rows(T, 8, 1), _rows(TM, MIX),
                            _layer((3, MIX), l)],
                  out_specs=[_rows(TM, 4 * MIX), _full((8, MIX))],
                  out_shape=[_sds((T, 4 * MIX), BF16), _sds((8, MIX), F32)],
                  scratch=[pltpu.VMEM((TM + 8, MIX), F32), pltpu.VMEM((TM + 8, MIX), F32)])(
                      dout, h, h, h, h, h, dout, h, dua, w)


_PH = 16


def _pooled(ext, t, gi, w):
    lo = gi * POOL_GROUP
    cur = ext[pl.ds(_PH, TM), lo:lo + POOL_GROUP]
    acc = cur
    for k in range(1, w):
        acc = acc + ext[pl.ds(_PH - k, TM), lo:lo + POOL_GROUP]
    cnt = jnp.minimum(t + 1, w).astype(F32)
    return acc / cnt - cur, cnt


def _pool_fwd(h, pw, scale, l, name):
    T = h.shape[0]

    def body(z_ref, zh_ref, pw_ref, sc_ref, o_ref, ext):
        i = pl.program_id(0)
        ext[0:_PH, :] = jnp.where(i > 0, zh_ref[...], 0.0)
        ext[_PH:, :] = z_ref[...]
        t = i * TM + lax.broadcasted_iota(jnp.int32, (TM, 1), 0)
        for gi, w in enumerate(POOL_WINDOWS):
            lo = gi * POOL_GROUP
            pooled, _ = _pooled(ext, t, gi, w)
            mixed = _dot(pooled.astype(BF16), pw_ref[gi].astype(BF16))
            o_ref[:, lo:lo + POOL_GROUP] = (mixed * sc_ref[:, lo:lo + POOL_GROUP]).astype(BF16)

    return _pcall(body, name=name, grid=(T // TM,),
                  in_specs=[_cols(TM, 3), _prev_rows(T, _PH, 3), _layer((4, POOL_GROUP, POOL_GROUP), l),
                            _layer((1, MIX), l)],
                  out_specs=_rows(TM, MIX), out_shape=_sds((T, MIX), BF16),
                  scratch=[pltpu.VMEM((TM + _PH, MIX), F32)], sem=("parallel",))(h, h, pw, scale)


def _pool_bwd(dout, h, dq, dk, dv, pw, scale, l, name):
    T = h.shape[0]
    nb = T // TM

    def body(do_ref, don_ref, z_ref, zh_ref, dq_ref, dk_ref, dv_ref, pw_ref, sc_ref,
             dh_ref, dpw_ref, dsc_ref, ext, ext2):
        i = pl.program_id(0)
        ext[0:_PH, :] = jnp.where(i > 0, zh_ref[...], 0.0)
        ext[_PH:, :] = z_ref[...]
        t = i * TM + lax.broadcasted_iota(jnp.int32, (TM, 1), 0)
        dh_ref[:, 0:MIX] = dq_ref[...]
        dh_ref[:, MIX:2 * MIX] = dk_ref[...]
        dh_ref[:, 2 * MIX:3 * MIX] = dv_ref[...]

        @pl.when(i == 0)
        def _():
            dpw_ref[...] = jnp.zeros_like(dpw_ref)
            dsc_ref[...] = jnp.zeros_like(dsc_ref)

        for gi, w in enumerate(POOL_WINDOWS):
            lo = gi * POOL_GROUP
            pwb = pw_ref[gi].astype(BF16)
            sc = sc_ref[:, lo:lo + POOL_GROUP]
            pooled, cnt = _pooled(ext, t, gi, w)
            pb = pooled.astype(BF16)
            mixed = _dot(pb, pwb)
            dog = do_ref[:, lo:lo + POOL_GROUP]
            dsc_ref[:, lo:lo + POOL_GROUP] += _colsum(dog * mixed)
            dmix = (dog * sc).astype(BF16)
            dpw_ref[gi] += _dot_tn(pb, dmix)
            dpool = _dot_nt(dmix, pwb)
            dmix_n = (jnp.where(i < nb - 1, don_ref[:, lo:lo + POOL_GROUP], 0.0) * sc).astype(BF16)
            dpool_n = _dot_nt(dmix_n, pwb)
            e = dpool / cnt
            ext2[0:TM, lo:lo + POOL_GROUP] = e
            ext2[TM:, lo:lo + POOL_GROUP] = dpool_n * (1.0 / w)
            s = e
            for k in range(1, w):
                s = s + ext2[pl.ds(k, TM), lo:lo + POOL_GROUP]
            dh_ref[:, 3 * MIX + lo:3 * MIX + lo + POOL_GROUP] = (s - dpool).astype(BF16)

    blk = _rows(TM, MIX)
    return _pcall(body, name=name, grid=(nb,),
                  in_specs=[blk, _next_rows(T, _PH), _cols(TM, 3), _prev_rows(T, _PH, 3), blk, blk, blk,
                            _layer((4, POOL_GROUP, POOL_GROUP), l), _layer((1, MIX), l)],
                  out_specs=[_rows(TM, 4 * MIX), _full((4, POOL_GROUP, POOL_GROUP)), _full((1, MIX))],
                  out_shape=[_sds((T, 4 * MIX), BF16), _sds((4, POOL_GROUP, POOL_GROUP), F32), _sds((1, MIX), F32)],
                  scratch=[pltpu.VMEM((TM + _PH, MIX), F32), pltpu.VMEM((TM + _PH, MIX), F32)])(
                      dout, dout, h, h, dq, dk, dv, pw, scale)


def _att_bias_table(rel_bias):
    span = LEFT_CHUNKS * CHUNK
    assert ATT_TQ - 1 <= MAX_REL
    lo = MAX_REL - (ATT_TQ - 1)
    near = rel_bias[:, lo:2 * MAX_REL + 1]
    far = jnp.broadcast_to(rel_bias[:, 2 * MAX_REL:], (HEADS, span + ATT_TQ - 1 - MAX_REL))
    by_dist = jnp.concatenate([near, far], axis=1)
    rev = by_dist[:, ::-1]
    lv = rev.shape[1]
    skew = jnp.tile(rev, (1, ATT_TQ + 1))[:, :ATT_TQ * (lv + 1)].reshape(HEADS, ATT_TQ, lv + 1)[:, :, :ATT_W]
    bias = skew[:, ::-1, :]
    r = jnp.arange(ATT_TQ)[:, None]
    col = jnp.arange(ATT_W)[None, :]
    dchunk = (LEFT_CHUNKS + r // CHUNK) - col // CHUNK
    visible = (dchunk >= 0) & (dchunk <= LEFT_CHUNKS)
    return jnp.where(visible[None], bias, NEG_INF)


def _qrows(n, s=0):
    return pl.BlockSpec((ATT_TQ, n), lambda i: (i, s))


def _att_views_back(d, s):
    return pl.BlockSpec((ATT_TQ, MIX), lambda i: (jnp.maximum(i - (ATT_NV - 1) + d, 0), s))


_PAIR = 2 * HEAD_DIM


def _att_pair(refs, pp):
    sl = slice(pp * _PAIR, (pp + 1) * _PAIR)
    if isinstance(refs, (tuple, list)):
        return jnp.concatenate([r[:, sl] for r in refs], axis=0)
    return refs[:, sl]


def _att_fwd(h, table, name):
    T = h.shape[0]

    def body(q_ref, *refs):
        k_refs = refs[:ATT_NV]
        v_refs = refs[ATT_NV:2 * ATT_NV]
        tb_ref = refs[2 * ATT_NV]
        o_ref, oh_ref, lse_ref = refs[2 * ATT_NV + 1:]
        i = pl.program_id(0)
        col = lax.broadcasted_iota(jnp.int32, (1, ATT_W), 1)
        kvalid = (col + (i - (ATT_NV - 1)) * ATT_TQ) >= 0
        first = lax.broadcasted_iota(jnp.int32, (1, _PAIR), 1) < HEAD_DIM
        lses = []
        for pp in range(HEADS // 2):
            qp = _att_pair(q_ref, pp) * (HEAD_DIM ** -0.5)
            kp = _att_pair(k_refs, pp).astype(BF16)
            vp = _att_pair(v_refs, pp).astype(BF16)
            outs = []
            for e in range(2):
                half = first if e == 0 else jnp.logical_not(first)
                s = _dot_nt(jnp.where(half, qp, 0.0).astype(BF16), kp) + tb_ref[2 * pp + e]
                s = jnp.where(kvalid, s, NEG_INF)
                m = jnp.max(s, axis=-1, keepdims=True)
                p = jnp.exp(s - m)
                l = jnp.sum(p, axis=-1, keepdims=True)
                outs.append(_dot(p.astype(BF16), vp) / l)
                lses.append(m + jnp.log(l))
            o = jnp.where(first, outs[0], outs[1])
            o_ref[:, pp * _PAIR:(pp + 1) * _PAIR] = o
            oh_ref[:, pp * _PAIR:(pp + 1) * _PAIR] = o.astype(BF16)
        lse_ref[...] = jnp.concatenate(lses, axis=1)

    kviews = [_att_views_back(d, 1) for d in range(ATT_NV)]
    vviews = [_att_views_back(d, 2) for d in range(ATT_NV)]
    return _pcall(body, name=name, grid=(T // ATT_TQ,),
                  in_specs=[_qrows(MIX)] + kviews + vviews + [_full((HEADS, ATT_TQ, ATT_W))],
                  out_specs=[_qrows(MIX), _qrows(MIX), _qrows(HEADS)],
                  out_shape=[_sds((T, MIX), F32), _sds((T, MIX), BF16), _sds((T, HEADS), F32)],
                  sem=("parallel",))(h, *([h] * (2 * ATT_NV)), table)


def _att_bwd_q(h, do, o, lse, table, name):
    T = h.shape[0]

    def body(q_ref, *refs):
        k_refs = refs[:ATT_NV]
        v_refs = refs[ATT_NV:2 * ATT_NV]
        do_ref, o_ref, lse_ref, tb_ref, dq_ref, dl_ref, dtb_ref = refs[2 * ATT_NV:]
        i = pl.program_id(0)
        col = lax.broadcasted_iota(jnp.int32, (1, ATT_W), 1)
        kvalid = (col + (i - (ATT_NV - 1)) * ATT_TQ) >= 0

        first = lax.broadcasted_iota(jnp.int32, (1, _PAIR), 1) < HEAD_DIM

        @pl.when(i == 0)
        def _():
            dtb_ref[...] = jnp.zeros_like(dtb_ref)

        deltas = []
        for pp in range(HEADS // 2):
            qp = _att_pair(q_ref, pp) * (HEAD_DIM ** -0.5)
            kp = _att_pair(k_refs, pp).astype(BF16)
            vp = _att_pair(v_refs, pp).astype(BF16)
            dop = _att_pair(do_ref, pp)
            doo = dop * _att_pair(o_ref, pp)
            outs = []
            for e in range(2):
                hd = 2 * pp + e
                half = first if e == 0 else jnp.logical_not(first)
                s = _dot_nt(jnp.where(half, qp, 0.0).astype(BF16), kp) + tb_ref[hd]
                s = jnp.where(kvalid, s, NEG_INF)
                p = jnp.exp(s - lse_ref[:, hd:hd + 1])
                delta = jnp.sum(jnp.where(half, doo, 0.0), axis=-1, keepdims=True)
                dp = _dot_nt(jnp.where(half, dop, 0.0).astype(BF16), vp)
                ds = p * (dp - delta)
                dtb_ref[hd] += ds
                outs.append(_dot(ds.astype(BF16), kp))
                deltas.append(delta)
            dq = jnp.where(first, outs[0], outs[1]) * (HEAD_DIM ** -0.5)
            dq_ref[:, pp * _PAIR:(pp + 1) * _PAIR] = dq.astype(BF16)
        dl_ref[...] = jnp.concatenate(deltas, axis=1)

    kviews = [_att_views_back(d, 1) for d in range(ATT_NV)]
    vviews = [_att_views_back(d, 2) for d in range(ATT_NV)]
    tb = _full((HEADS, ATT_TQ, ATT_W))
    return _pcall(body, name=name, grid=(T // ATT_TQ,),
                  in_specs=[_qrows(MIX)] + kviews + vviews + [_qrows(MIX), _qrows(MIX), _qrows(HEADS), tb],
                  out_specs=[_qrows(MIX), _qrows(HEADS), tb],
                  out_shape=[_sds((T, MIX), BF16), _sds((T, HEADS), F32), _sds((HEADS, ATT_TQ, ATT_W), F32)])(
                      h, *([h] * (2 * ATT_NV)), do, o, lse, table)


def _att_table_by_key(table):
    t = table.reshape(HEADS, ATT_TQ, ATT_NV, ATT_TQ)[:, :, ::-1, :]
    return jnp.transpose(t, (0, 3, 2, 1)).reshape(HEADS, ATT_TQ, ATT_W)


def _att_bwd_kv(h, do, lse_t, delta_t, table_k, name):
    T = h.shape[0]
    nb = T // ATT_TQ

    def fwd_view(d, s=0):
        return pl.BlockSpec((ATT_TQ, MIX), lambda j: (jnp.minimum(j + d, nb - 1), s))

    def row_view(d):
        return pl.BlockSpec((HEADS, ATT_TQ), lambda j: (0, jnp.minimum(j + d, nb - 1)))

    def body(k_ref, v_ref, *refs):
        q_refs = refs[:ATT_NV]
        do_refs = refs[ATT_NV:2 * ATT_NV]
        lse_refs = refs[2 * ATT_NV:3 * ATT_NV]
        dl_refs = refs[3 * ATT_NV:4 * ATT_NV]
        tb_ref, dk_ref, dv_ref = refs[4 * ATT_NV:]
        j = pl.program_id(0)
        view = lax.broadcasted_iota(jnp.int32, (1, ATT_W), 1) // ATT_TQ
        valid = (j + view) <= nb - 1
        first = lax.broadcasted_iota(jnp.int32, (1, _PAIR), 1) < HEAD_DIM
        for pp in range(HEADS // 2):
            kp = _att_pair(k_ref, pp)
            vp = _att_pair(v_ref, pp)
            qs = (_att_pair(q_refs, pp) * (HEAD_DIM ** -0.5)).astype(BF16)
            dos = _att_pair(do_refs, pp).astype(BF16)
            dks, dvs = [], []
            for e in range(2):
                hd = 2 * pp + e
                half = first if e == 0 else jnp.logical_not(first)
                lses = jnp.concatenate([r[hd:hd + 1, :] for r in lse_refs], axis=1)
                dls = jnp.concatenate([r[hd:hd + 1, :] for r in dl_refs], axis=1)
                st = _dot_nt(jnp.where(half, kp, 0.0).astype(BF16), qs) + tb_ref[hd]
                pt = jnp.where(valid, jnp.exp(st - lses), 0.0)
                dvs.append(_dot(pt.astype(BF16), dos))
                dst = pt * (_dot_nt(jnp.where(half, vp, 0.0).astype(BF16), dos) - dls)
                dks.append(_dot(dst.astype(BF16), qs))
            dk_ref[:, pp * _PAIR:(pp + 1) * _PAIR] = jnp.where(first, dks[0], dks[1]).astype(BF16)
            dv_ref[:, pp * _PAIR:(pp + 1) * _PAIR] = jnp.where(first, dvs[0], dvs[1]).astype(BF16)

    qv = [fwd_view(d, 0) for d in range(ATT_NV)]
    dov = [fwd_view(d) for d in range(ATT_NV)]
    rows = [row_view(d) for d in range(ATT_NV)]
    return _pcall(body, name=name, grid=(nb,),
                  in_specs=[_qrows(MIX, 1), _qrows(MIX, 2)] + qv + dov + rows + rows
                  + [_full((HEADS, ATT_TQ, ATT_W))],
                  out_specs=[_qrows(MIX), _qrows(MIX)], out_shape=[_sds((T, MIX), BF16), _sds((T, MIX), BF16)],
                  sem=("parallel",))(h, h, *([h] * ATT_NV), *([do] * ATT_NV), *([lse_t] * ATT_NV),
                                     *([delta_t] * ATT_NV), table_k)


def _all_gather(x, name):
    R, C = x.shape

    def body(x_ref, out_ref, send_sems, recv_sems, local_sem):
        xi, yi, ci = lax.axis_index("x"), lax.axis_index("y"), lax.axis_index("c")
        me, sibling = (xi, yi, ci), (xi, yi, 1 - ci)
        chips = [(1 - xi, yi), (xi, 1 - yi), (1 - xi, 1 - yi)]

        def slot(px, py, pc):
            return out_ref.at[4 * px + 2 * py + pc]

        def copy(k, block, to, src=None):
            return pltpu.make_async_remote_copy(
                src_ref=slot(*block) if src is None else src, dst_ref=slot(*block),
                send_sem=send_sems.at[k], recv_sem=recv_sems.at[k], device_id=to, device_id_type=_MESH)

        mine = pltpu.make_async_copy(x_ref, slot(*me), local_sem)
        mine.start()
        first = [copy(0, me, sibling, src=x_ref)]
        first += [copy(1 + j, me, (*chip, ci), src=x_ref) for j, chip in enumerate(chips)]
        for cp in first:
            cp.start()
        passed = [copy(4 + j, (*chip, ci), sibling) for j, chip in enumerate(chips)]
        for j, chip in enumerate(chips):
            copy(1 + j, (*chip, ci), me).wait_recv()
            passed[j].start()
        copy(0, sibling, me).wait_recv()
        for j, chip in enumerate(chips):
            copy(4 + j, (*chip, 1 - ci), me).wait_recv()
        for cp in first + passed:
            cp.wait_send()
        mine.wait()

    return pl.pallas_call(
        body, name=name, out_shape=_sds((N_DEV, R, C), x.dtype), in_specs=[_ANY], out_specs=_ANY,
        scratch_shapes=[pltpu.SemaphoreType.DMA((7,)), pltpu.SemaphoreType.DMA((7,)), pltpu.SemaphoreType.DMA(())],
    )(x)


def _a2a_copies(s_ref, r_ref, send_sems, recv_sems, local_sem, gather=False):
    xi, yi, ci = lax.axis_index("x"), lax.axis_index("y"), lax.axis_index("c")
    me = 4 * xi + 2 * yi + ci

    def mine():
        return pltpu.make_async_copy(s_ref if gather else s_ref.at[me], r_ref.at[me], local_sem)

    def remote(m, sending):
        px = 1 - xi if m & 4 else xi
        py = 1 - yi if m & 2 else yi
        pc = 1 - ci if m & 1 else ci
        peer = 4 * px + 2 * py + pc
        src, dst = (s_ref.at[peer], r_ref.at[me]) if sending else (s_ref.at[me], r_ref.at[peer])
        if gather:
            src = s_ref
        return pltpu.make_async_remote_copy(src_ref=src, dst_ref=dst, send_sem=send_sems.at[m - 1],
                                            recv_sem=recv_sems.at[m - 1], device_id=(px, py, pc), device_id_type=_MESH)

    def start():
        mine().start()
        for m in range(1, N_DEV):
            remote(m, True).start()

    def wait():
        for m in range(1, N_DEV):
            remote(m, False).wait_recv()
        for m in range(1, N_DEV):
            remote(m, True).wait_send()
        mine().wait()

    return start, wait


_A2A_SEMS = [pltpu.SemaphoreType.DMA((7,)), pltpu.SemaphoreType.DMA((7,)), pltpu.SemaphoreType.DMA(())]


def _all_to_all(s, name):
    def body(s_ref, r_ref, send_sems, recv_sems, local_sem):
        start, wait = _a2a_copies(s_ref, r_ref, send_sems, recv_sems, local_sem)
        start()
        wait()

    return pl.pallas_call(
        body, name=name, out_shape=_sds(s.shape, s.dtype), in_specs=[_ANY], out_specs=_ANY,
        scratch_shapes=list(_A2A_SEMS))(s)


def _riding(body, n_in, n_out, n_ex, first, last, gather=False):
    def wrapped(*refs):
        ins = refs[:n_in]
        sends = refs[n_in:n_in + n_ex]
        outs = refs[n_in + n_ex:n_in + n_ex + n_out]
        recvs = refs[n_in + n_ex + n_out:n_in + 2 * n_ex + n_out]
        rest = refs[n_in + 2 * n_ex + n_out:]
        scratch, sems = rest[:len(rest) - 3 * n_ex], rest[len(rest) - 3 * n_ex:]

        @pl.when(first())
        def _():
            for k in range(n_ex):
                _a2a_copies(sends[k], recvs[k], *sems[3 * k:3 * k + 3], gather=gather)[0]()

        body(*ins, *outs, *scratch)

        @pl.when(last())
        def _():
            for k in range(n_ex):
                _a2a_copies(sends[k], recvs[k], *sems[3 * k:3 * k + 3], gather=gather)[1]()

    return wrapped


_ADAMW_BLOCK_BYTES = 6 * 1024 * 1024


def _adamw(parts, row_off, w, m, v, name):
    R, C = w.shape
    tr = None
    for cand in (512, 256, 128, 64, 32, 16):
        step_bytes = cand * C * (N_DEV * parts.dtype.itemsize + 7 * 4)
        if R % cand == 0 and row_off % cand == 0 and step_bytes <= _ADAMW_BLOCK_BYTES:
            tr = cand
            break
    assert tr is not None, (R, C, row_off)
    off = row_off // tr
    c1 = 1.0 - ADAM_B1 ** ADAM_STEP
    c2 = 1.0 - ADAM_B2 ** ADAM_STEP

    def body(p_ref, w_ref, m_ref, v_ref, g_ref, d_ref, mo_ref, vo_ref):
        g = p_ref[0].astype(F32)
        for j in range(1, N_DEV):
            g = g + p_ref[j].astype(F32)
        mn = ADAM_B1 * m_ref[...] + (1.0 - ADAM_B1) * g
        vn = ADAM_B2 * v_ref[...] + (1.0 - ADAM_B2) * (g * g)
        m_hat = mn / c1
        v_hat = vn / c2
        g_ref[...] = g
        d_ref[...] = -ADAM_LR * (m_hat / (jnp.sqrt(v_hat) + ADAM_EPS) + ADAM_WD * w_ref[...])
        mo_ref[...] = mn
        vo_ref[...] = vn

    blk = _rows(tr, C)
    return _pcall(body, name=name, grid=(R // tr,),
                  in_specs=[pl.BlockSpec((N_DEV, tr, C), lambda i: (0, off + i, 0)), blk, blk, blk],
                  out_specs=[blk] * 4, out_shape=[_sds((R, C), F32)] * 4,
                  sem=("parallel",))(parts, w, m, v)


def _piece_rows(size):
    return -(-size // PIECE) * 16


def _pack_rows(arrays, lead=()):
    parts = []
    total = 0
    for a in arrays:
        flat = a.reshape(lead + (-1,))
        size = flat.shape[-1]
        rows = _piece_rows(size)
        pad = [(0, 0)] * len(lead) + [(0, rows * FLAT_COLS - size)]
        parts.append(jnp.pad(flat, pad).reshape(lead + (rows, FLAT_COLS)))
        total += rows
    tail = -total % FLAT_ROWS
    if tail:
        parts.append(jnp.zeros(lead + (tail, FLAT_COLS), parts[0].dtype))
    return jnp.concatenate(parts, axis=len(lead))


def _unpack_rows(buf, shapes, lead=()):
    out = []
    row = 0
    nl = len(lead)
    for shape in shapes:
        size = math.prod(shape)
        rows = _piece_rows(size)
        piece = lax.slice_in_dim(buf, row, row + rows, axis=nl).reshape(lead + (rows * FLAT_COLS,))
        out.append(lax.slice_in_dim(piece, 0, size, axis=nl).reshape(lead + tuple(shape)))
        row += rows
    return out


def _to_blocks(full, axis):
    shp = full.shape
    split = full.reshape(shp[:axis] + (N_DEV, shp[axis] // N_DEV) + shp[axis + 1:])
    return jnp.moveaxis(split, axis, 0)


def _from_blocks(blocks, axis):
    shp = blocks.shape[1:]
    moved = jnp.moveaxis(blocks, 0, axis)
    return moved.reshape(shp[:axis] + (N_DEV * shp[axis],) + shp[axis + 1:])


N_LAYERS = {n: (DEPTH // 2 if n.startswith(('ev_', 'od_')) else DEPTH) for n in WEIGHT_NAMES}
R_OUT = D_MODEL // N_DEV
R_DOWN = D_FF // N_DEV


def _layer_shards(W, i):
    j = i // 2
    ev = i % 2 == 0
    rows = [W['ev_w_out' if ev else 'od_w_out'][j], W['ffn_w_down'][i], W['ple_w_gate'][i]]
    return dict(up=W['ffn_w_up'][i].astype(BF16), inn=W['ev_w_in' if ev else 'od_w_in'][j].astype(BF16),
                rows=jnp.concatenate(rows, axis=0).astype(BF16))


def _layer_weights(up, inn, rows):
    return dict(up=up.reshape(2, FF_J, 1, D_MODEL, FF_NB), win=inn.reshape(N_DEV, 1, D_MODEL, IN_NB),
                wo=lax.slice_in_dim(rows, 0, R_OUT, axis=1).reshape(1, D_MODEL, D_MODEL),
                wd4=lax.slice_in_dim(rows, R_OUT, R_OUT + R_DOWN, axis=1).reshape(1, FF_J, FF_NB, D_MODEL),
                gate=lax.slice_in_dim(rows, R_OUT + R_DOWN, 2 * R_OUT + R_DOWN, axis=1).reshape(1, D_MODEL, D_MODEL))


def _layer_weights_of(W, i):
    j = i // 2
    ev = i % 2 == 0
    return dict(up=W['ffn_w_up'][:, i:i + 1].reshape(2, FF_J, 1, D_MODEL, FF_NB),
                win=W['ev_w_in' if ev else 'od_w_in'][:, j:j + 1],
                wo=W['ev_w_out' if ev else 'od_w_out'][j:j + 1],
                wd4=W['ffn_w_down'][i:i + 1].reshape(1, FF_J, FF_NB, D_MODEL), gate=W['ple_w_gate'][i:i + 1])


def _row_blocks(g):
    return g.reshape(N_DEV, g.shape[0] // N_DEV, g.shape[1])


def _early_sends(G, i):
    ffn = jnp.concatenate([_row_blocks(G['ffn_w_down'][i]), _row_blocks(G['ple_w_gate'][i])], axis=1)
    return dict(up=G['ffn_w_up'][i], ffn=ffn)


def _late_sends(G, i):
    j = i // 2
    ev = i % 2 == 0
    return dict(out=_row_blocks(G['ev_w_out' if ev else 'od_w_out'][j]), inn=G['ev_w_in' if ev else 'od_w_in'][j])


def _local_step(x, p, tgt, W, inn0=None, shards=None):
    overlap = shards is not None
    big = [] if overlap else [_layer_weights_of(W, i) for i in range(DEPTH)]
    ln = {n: _vecs(W[n]) for n in ('ln_mix_g', 'ln_mix_b', 'ln_ffn_g', 'ln_ffn_b', 'ple_b_gate')}
    bglu = _vecs(W['ev_b_glu'])
    pscale = _vecs(W['od_pool_scale'])
    saved = []
    x0 = x
    x0h = x.astype(BF16)
    for i in range(DEPTH):
        L = f"L{i}_"
        j = i // 2
        if overlap and i == 0:
            h, got = _mm_in(x0h, inn0.reshape(N_DEV, 1, D_MODEL, IN_NB), 0, L + "mm_in",
                            ride=(shards[0]['up'], shards[0]['rows']))
            big.append(_layer_weights(got[0], inn0, got[1]))
        else:
            h = _mm_in(x0h, big[i]['win'], 0, L + "mm_in")
        B = big[i]
        s = dict(x0=x0, x0h=x0h, B=B)
        if i % 2 == 0:
            params = tuple(W[n][j] for n in ('ev_lambda_re', 'ev_lambda_im', 'ev_log_dt', 'ev_b_re', 'ev_b_im',
                                             'ev_c_re', 'ev_c_im', 'ev_d'))
            (r0, qt, pm, a1, a2), op_vjp = jax.vjp(_s5_operators, *params)
            umat = _to_chunks(h, L + "to_chunks")
            st = jnp.transpose(_s5_chunk_state(umat, pm, L + "s5_state"), (1, 0, 2))
            xp_t = _s5_scan_fwd(st, a1, a2, L + "s5_scan")
            xprev = jnp.transpose(xp_t, (1, 0, 2))
            y = _from_chunks(_s5_output(umat, xprev, r0, qt, L + "s5_out"), L + "from_chunks")
            ya, gh = _glu_fwd(y, W['ev_w_glu'], bglu, j, L + "glu")
            yb = _conv_fwd(h, W['ev_conv_w'], j, L + "conv")
            s.update(op_vjp=op_vjp, r0=r0, qt=qt, pm=pm, a1=a1, a2=a2, umat=umat, xp_t=xp_t, xprev=xprev, y=y, gh=gh)
        else:
            table, tb_vjp = jax.vjp(_att_bias_table, W['od_rel_bias'][j])
            of, ya, lse = _att_fwd(h, table, L + "att")
            yb = _pool_fwd(h, W['od_pool_w'], pscale, j, L + "pool")
            s.update(table=table, tb_vjp=tb_vjp, of=of, lse=lse)
        g1, b1 = (ln['ln_mix_g'], i), (ln['ln_mix_b'], i)
        g2, b2 = (ln['ln_ffn_g'], i), (ln['ln_ffn_b'], i)
        xhat1, rstd1, x1h = _mm_out_ln(ya, yb, B['wo'], 0, x0, g1, b1, L + "mm_out_ln")
        nxt = shards[i + 1] if overlap and i + 1 < DEPTH else None
        (hf, a3), got_a = _mm_up(x1h, B['up'], 0, L + "mm_up", ride=(nxt['up'], nxt['inn']) if nxt else ())
        (xhat2, rstd2, x2h), got_b = _mm_down_ln(a3, B['wd4'], 0, xhat1, g1, b1, g2, b2, L + "mm_down_ln",
                                                 ride=(nxt['rows'],) if nxt else ())
        if nxt:
            big.append(_layer_weights(got_a[0], got_a[1], got_b[0]))
        x3, x3h = _mm_ple(x2h, xhat2, ln['ln_ffn_g'], ln['ln_ffn_b'], p, B['gate'], ln['ple_b_gate'],
                          W['ple_w_proj'], i, L + "mm_ple", lw=0)
        s.update(h=h, ya=ya, yb=yb, xhat1=xhat1, rstd1=rstd1, x1h=x1h, hf=hf, a3=a3, xhat2=xhat2,
                 rstd2=rstd2, x2h=x2h, g1=g1)
        saved.append(s)
        x0, x0h = x3, x3h

    dx, loss = _loss_head(x0, tgt, "loss_head")

    G = {n: [None] * N_LAYERS[n] for n in WEIGHT_NAMES}
    landed = {i: {} for i in range(DEPTH)}
    pending = None
    for i in reversed(range(DEPTH)):
        L = f"L{i}_"
        j = i // 2
        s = saved[i]
        B = s['B']
        dr2, dr2h, dpreh, dpph, dbg, dg2, db2 = _ple_bwd(
            dx, s['x2h'], p, B['gate'], ln['ple_b_gate'], W['ple_w_proj'], i, s['xhat2'], s['rstd2'],
            ln['ln_ffn_g'], L + "ple_bwd", lw=0)
        G['ple_w_gate'][i] = _mm_tn(s['x2h'], dpreh, L + "dw_gate")
        G['ple_w_proj'][i] = _mm_tn(p, dpph, L + "dw_proj", a_layer=i)
        G['ple_b_gate'][i] = dbg[0]
        G['ln_ffn_g'][i] = dg2[0]
        G['ln_ffn_b'][i] = db2[0]
        dhf, got = _ffn_bwd1(dr2h, B['wd4'], 0, s['hf'], L + "ffn_bwd1",
                             ride=(pending['out'], pending['inn']) if pending else ())
        if pending:
            landed[i + 1].update(out=got[0], inn=got[1])
        G['ffn_w_down'][i] = _mm_tn_ablk(s['a3'], dr2h, L + "dw_down").reshape(D_FF, D_MODEL)
        T = dhf.shape[2]
        G['ffn_w_up'][i] = _mm_tn_bblk(s['x1h'], dhf.reshape(N_DEV, T, FF_NB), L + "dw_up")
        early = _early_sends(G, i) if overlap else None
        (dr1, dr1h, dg1, db1), got = _ffn_bwd2(dhf, B['up'], 0, dr2, s['xhat1'], s['rstd1'], s['g1'], L + "ffn_bwd2",
                                               ride=(early['up'], early['ffn']) if overlap else ())
        if overlap:
            landed[i].update(up=got[0], ffn=got[1])
        G['ln_mix_g'][i] = dg1[0]
        G['ln_mix_b'][i] = db1[0]
        dya, dyb = _out_bwd(dr1h, B['wo'], 0, L + "out_bwd")
        dwo = jnp.concatenate([_mm_tn(s['ya'], dr1h, L + "dw_out_a"), _mm_tn(s['yb'], dr1h, L + "dw_out_b")], axis=0)
        if i % 2 == 0:
            G['ev_w_out'][j] = dwo
            dy, dzh, dbglu = _glu_bwd(s['y'], dya, W['ev_w_glu'], bglu, j, L + "glu_bwd")
            G['ev_w_glu'][j] = _mm_tn(s['gh'], dzh, L + "dw_glu")
            G['ev_b_glu'][j] = dbglu[0]
            dymat = _to_chunks(dy, L + "to_chunks_dy")
            dxp_t = jnp.transpose(_s5_bwd_state(dymat, s['qt'], L + "s5_bwd_state"), (1, 0, 2))
            ds_t, da1, da2 = _s5_scan_bwd(dxp_t, s['xp_t'], s['a1'], s['a2'], L + "s5_scan_bwd")
            dumat, dr0, dqt, dpm = _s5_bwd_main(s['umat'], s['xprev'], dymat, jnp.transpose(ds_t, (1, 0, 2)),
                                                s['r0'], s['pm'], L + "s5_bwd")
            dparams = s['op_vjp']((dr0, dqt, dpm, da1, da2))
            for n, dpar in zip(('ev_lambda_re', 'ev_lambda_im', 'ev_log_dt', 'ev_b_re', 'ev_b_im', 'ev_c_re',
                                'ev_c_im', 'ev_d'), dparams):
                G[n][j] = dpar
            dua = _from_chunks(dumat, L + "from_chunks_du")
            dh, dcw = _conv_bwd(dyb, s['h'], dua, W['ev_conv_w'], j, L + "conv_bwd")
            G['ev_conv_w'][j] = dcw[:3]
            wname = 'ev_w_in'
        else:
            G['od_w_out'][j] = dwo
            dq, delta, dtable = _att_bwd_q(s['h'], dya, s['of'], s['lse'], s['table'], L + "att_bwd_q")
            dk, dv = _att_bwd_kv(s['h'], dya, s['lse'].T, delta.T, _att_table_by_key(s['table']), L + "att_bwd_kv")
            G['od_rel_bias'][j] = s['tb_vjp'](dtable)[0]
            dh, dpw, dsc = _pool_bwd(dyb, s['h'], dq, dk, dv, W['od_pool_w'], pscale, j, L + "pool_bwd")
            G['od_pool_w'][j] = dpw
            G['od_pool_scale'][j] = dsc[0]
            wname = 'od_w_in'
        G[wname][j] = _mm_tn(s['x0h'], dh, L + "dw_in", blocked_n=IN_NB)
        dx = _in_bwd(dh, B['win'], 0, dr1, L + "in_bwd")
        pending = _late_sends(G, i) if overlap else None

    if overlap:
        return loss, dx, G, landed, pending
    return loss, dx, G


def _slab(a):
    return a.reshape(-1, a.shape[-1])


def _gather_small(W):
    full = {n: W[n] for n in REPLICATED}
    got = _all_gather(_pack_rows([W[n].astype(BF16) for n in SMALL_SHARDED]), "gather_w_small")
    shapes = [W[n].shape for n in SMALL_SHARDED]
    for n, blocks in zip(SMALL_SHARDED, _unpack_rows(got, shapes, lead=(N_DEV,))):
        full[n] = _from_blocks(blocks, SHARD_AXIS[n])
    for n in ('ev_conv_w', 'od_pool_scale'):
        full[n] = full[n].astype(F32)
    return full


def _step(x, p, tgt, W, M, V):
    shards = [_layer_shards(W, i) for i in range(DEPTH)]
    inn0 = _all_gather(shards[0]['inn'], "gather_w0_inn")
    loss, dx, G, landed, tail = _local_step(x[0], p[:, 0], tgt[0], _gather_small(W), inn0=inn0, shards=shards)
    landed[0].update({k: _all_to_all(v, "scatter_g_" + k) for k, v in tail.items()})
    res = {}

    def update(parts, n):
        shape = W[n].shape
        outs = _adamw(parts, 0, _slab(W[n]), _slab(M[n]), _slab(V[n]), "adamw_" + n)
        for kind, a in zip(('grad', 'delta', 'm', 'v'), outs):
            res[kind, n] = a.reshape(shape)

    def over_layers(layers, key, lo=None, hi=None):
        got = [landed[i][key] for i in layers]
        if lo is not None:
            got = [lax.slice_in_dim(g, lo, hi, axis=1) for g in got]
        return jnp.concatenate(got, axis=1)

    even, odd, every = (0, 2), (1, 3), (0, 1, 2, 3)
    update(over_layers(even, 'out'), 'ev_w_out')
    update(over_layers(odd, 'out'), 'od_w_out')
    update(over_layers(every, 'ffn', 0, R_DOWN), 'ffn_w_down')
    update(over_layers(every, 'ffn', R_DOWN, R_DOWN + R_OUT), 'ple_w_gate')
    update(over_layers(every, 'up'), 'ffn_w_up')
    update(over_layers(even, 'inn'), 'ev_w_in')
    update(over_layers(odd, 'inn'), 'od_w_in')

    shapes = [W[n].shape for n in SMALL_SHARDED]
    send = _pack_rows([_to_blocks(jnp.stack(G[n], axis=0), SHARD_AXIS[n]).astype(BF16) for n in SMALL_SHARDED],
                      lead=(N_DEV,))
    parts = _all_to_all(send, "scatter_g_small")
    outs = _adamw(parts, 0, _pack_rows([W[n] for n in SMALL_SHARDED]), _pack_rows([M[n] for n in SMALL_SHARDED]),
                  _pack_rows([V[n] for n in SMALL_SHARDED]), "adamw_small")
    for kind, buf in zip(('grad', 'delta', 'm', 'v'), outs):
        for n, a in zip(SMALL_SHARDED, _unpack_rows(buf, shapes)):
            res[kind, n] = a

    repl_shapes = [W[n].shape for n in REPLICATED]
    small = _pack_rows([jnp.stack(G[n], axis=0) for n in REPLICATED] + [loss])
    parts = _all_gather(small, "gather_g_replicated")
    zero = jnp.zeros((1, 1), F32)
    outs = _adamw(parts, 0, _pack_rows([W[n] for n in REPLICATED] + [zero]),
                  _pack_rows([M[n] for n in REPLICATED] + [zero]),
                  _pack_rows([V[n] for n in REPLICATED] + [zero]), "adamw_replicated")
    for kind, buf in zip(('grad', 'delta', 'm', 'v'), outs):
        arrays = _unpack_rows(buf, repl_shapes + [(1, 1)])
        for n, a in zip(REPLICATED, arrays):
            res[kind, n] = a
        if kind == 'grad':
            total_loss = arrays[-1].reshape(())

    out = [total_loss, dx[None]]
    for kind in ('grad', 'delta', 'm', 'v'):
        out += [res[kind, n] for n in WEIGHT_NAMES]
    return tuple(out)


def kernel(x, p, ev_w_in, ev_lambda_re, ev_lambda_im, ev_log_dt, ev_b_re, ev_b_im, ev_c_re, ev_c_im, ev_d, ev_w_glu, ev_b_glu, ev_conv_w, ev_w_out, od_w_in, od_rel_bias, od_pool_w, od_pool_scale, od_w_out, ln_mix_g, ln_mix_b, ln_ffn_g, ln_ffn_b, ffn_w_up, ffn_w_down, ple_w_proj, ple_w_gate, ple_b_gate, loss_target, m_ev_w_in, m_ev_lambda_re, m_ev_lambda_im, m_ev_log_dt, m_ev_b_re, m_ev_b_im, m_ev_c_re, m_ev_c_im, m_ev_d, m_ev_w_glu, m_ev_b_glu, m_ev_conv_w, m_ev_w_out, m_od_w_in, m_od_rel_bias, m_od_pool_w, m_od_pool_scale, m_od_w_out, m_ln_mix_g, m_ln_mix_b, m_ln_ffn_g, m_ln_ffn_b, m_ffn_w_up, m_ffn_w_down, m_ple_w_proj, m_ple_w_gate, m_ple_b_gate, v_ev_w_in, v_ev_lambda_re, v_ev_lambda_im, v_ev_log_dt, v_ev_b_re, v_ev_b_im, v_ev_c_re, v_ev_c_im, v_ev_d, v_ev_w_glu, v_ev_b_glu, v_ev_conv_w, v_ev_w_out, v_od_w_in, v_od_rel_bias, v_od_pool_w, v_od_pool_scale, v_od_w_out, v_ln_mix_g, v_ln_mix_b, v_ln_ffn_g, v_ln_ffn_b, v_ffn_w_up, v_ffn_w_down, v_ple_w_proj, v_ple_w_gate, v_ple_b_gate):
    given = dict(locals())
    W = {n: given[n] for n in WEIGHT_NAMES}
    M = {n: given["m_" + n] for n in WEIGHT_NAMES}
    V = {n: given["v_" + n] for n in WEIGHT_NAMES}
    return _step(x, p, loss_target, W, M, V)
```

```python
import math

import jax
import jax.numpy as jnp
from jax import lax
from jax.experimental import pallas as pl
from jax.experimental.pallas import tpu as pltpu

F32 = jnp.float32
BF16 = jnp.bfloat16
HI = lax.Precision.HIGHEST

D_MODEL = 1024
DEPTH = 4
CHUNK = 64
MIX = 512
S5_GROUP = 16
S5_GROUPS = 32
S5_STATE = 64
HEADS = 8
HEAD_DIM = 64
LEFT_CHUNKS = 8
MAX_REL = 128
POOL_WINDOWS = (2, 4, 8, 16)
POOL_GROUP = 128
D_FF = 2816
D_PLE = 256
ALPHA = (2 * DEPTH) ** 0.25
LN_EPS = 1e-5
NEG_INF = -1e30
ADAM_LR = 0.001
ADAM_B1 = 0.9
ADAM_B2 = 0.999
ADAM_EPS = 1e-08
ADAM_WD = 0.01
ADAM_STEP = 10
N_DEV = 8

WEIGHT_NAMES = ['ev_w_in', 'ev_lambda_re', 'ev_lambda_im', 'ev_log_dt', 'ev_b_re', 'ev_b_im', 'ev_c_re', 'ev_c_im',
                'ev_d', 'ev_w_glu', 'ev_b_glu', 'ev_conv_w', 'ev_w_out', 'od_w_in', 'od_rel_bias', 'od_pool_w',
                'od_pool_scale', 'od_w_out', 'ln_mix_g', 'ln_mix_b', 'ln_ffn_g', 'ln_ffn_b', 'ffn_w_up', 'ffn_w_down',
                'ple_w_proj', 'ple_w_gate', 'ple_b_gate']
SHARD_AXIS = {'ev_w_in': 2, 'ev_w_glu': 1, 'ev_conv_w': 2, 'ev_w_out': 1, 'od_w_in': 2, 'od_pool_scale': 1,
              'od_w_out': 1, 'ffn_w_up': 2, 'ffn_w_down': 1, 'ple_w_proj': 2, 'ple_w_gate': 1}
REPLICATED = [n for n in WEIGHT_NAMES if n not in SHARD_AXIS]
SMALL_SHARDED = ['ev_w_glu', 'ev_conv_w', 'od_pool_scale', 'ple_w_proj']

VMEM_LIMIT = 48 * 1024 * 1024
TM = 512
TM_FF = 1024
TK_DW = 2048
IN_NB = 4 * MIX // N_DEV
FF_NB = 2 * D_FF // N_DEV
FF_J = N_DEV // 2
S5_LC = 32
S5_LW = S5_LC * S5_GROUP
ATT_TQ = 128
ATT_NV = LEFT_CHUNKS * CHUNK // ATT_TQ + 1
ATT_W = ATT_NV * ATT_TQ
FLAT_COLS = 1024
FLAT_ROWS = 256
PIECE = 16 * FLAT_COLS


_MESH = pl.DeviceIdType.MESH
_ANY = pl.BlockSpec(memory_space=pl.ANY)


def _sds(shape, dt):
    return jax.ShapeDtypeStruct(shape, dt)


def _pcall(body, *, name, grid, in_specs, out_specs, out_shape, scratch=(), sem=None):
    sem = sem or ("arbitrary",) * len(grid)
    return pl.pallas_call(
        body, name=name, grid=grid, in_specs=in_specs, out_specs=out_specs, out_shape=out_shape,
        scratch_shapes=scratch,
        compiler_params=pltpu.CompilerParams(dimension_semantics=sem, vmem_limit_bytes=VMEM_LIMIT))


def _rows(tm, n):
    return pl.BlockSpec((tm, n), lambda i: (i, 0))


def _cols(tm, s):
    return pl.BlockSpec((tm, MIX), lambda i: (i, s))


def _full(shape):
    nd = len(shape)
    return pl.BlockSpec(shape, lambda *_: (0,) * nd)


def _layer(shape, l):
    nd = len(shape)
    return pl.BlockSpec((None,) + tuple(shape), lambda *_: (l,) + (0,) * nd)


def _vecs(a):
    return a.reshape(a.shape[0], 1, a.shape[1])


def _dot(a, b, precision=None):
    return jnp.dot(a, b, preferred_element_type=F32, precision=precision)


def _dot_nt(a, b, precision=None):
    return lax.dot_general(a, b, (((1,), (1,)), ((), ())), preferred_element_type=F32, precision=precision)


def _dot_tn(a, b, precision=None):
    return lax.dot_general(a, b, (((0,), (0,)), ((), ())), preferred_element_type=F32, precision=precision)


def _split(a):
    hi = a.astype(BF16)
    return hi, (a - hi.astype(F32)).astype(BF16)


def _mm3(dot, a2, b2):
    return dot(a2[0], b2[0]) + (dot(a2[0], b2[1]) + dot(a2[1], b2[0]))


def _sigmoid(x):
    return 0.5 * jnp.tanh(0.5 * x) + 0.5


_GELU_C = math.sqrt(2.0 / math.pi)


def _gelu(x):
    return 0.5 * x * (1.0 + jnp.tanh(_GELU_C * (x + 0.044715 * x * x * x)))


def _gelu_grad(x):
    t = jnp.tanh(_GELU_C * (x + 0.044715 * x * x * x))
    return 0.5 * (1.0 + t) + 0.5 * x * (1.0 - t * t) * _GELU_C * (1.0 + 3.0 * 0.044715 * x * x)


def _ln_fwd(r, g, b):
    mu = jnp.mean(r, axis=-1, keepdims=True)
    xc = r - mu
    var = jnp.mean(xc * xc, axis=-1, keepdims=True)
    rstd = lax.rsqrt(var + LN_EPS)
    xhat = xc * rstd
    return xhat, rstd, xhat * g + b


def _ln_bwd(dx, xhat, rstd, g):
    dxh = dx * g
    m1 = jnp.mean(dxh, axis=-1, keepdims=True)
    m2 = jnp.mean(dxh * xhat, axis=-1, keepdims=True)
    return rstd * (dxh - m1 - xhat * m2)


def _colsum(x):
    return jnp.sum(x, axis=0, keepdims=True)


def _mm_in(xh, win8, l, name, ride=()):
    T = xh.shape[0]
    ni = T // TM

    def body(x_ref, w_ref, h_ref):
        x = x_ref[...]
        for b in range(N_DEV):
            h_ref[:, b * IN_NB:(b + 1) * IN_NB] = _dot(x, w_ref[b])

    in_specs = [_rows(TM, D_MODEL), pl.BlockSpec((N_DEV, None, D_MODEL, IN_NB), lambda i: (0, l, 0, 0))]
    out_shape = _sds((T, 4 * MIX), F32)
    if not ride:
        return _pcall(body, name=name, grid=(ni,), in_specs=in_specs, out_specs=_rows(TM, 4 * MIX),
                      out_shape=out_shape, sem=("parallel",))(xh, win8)
    first = lambda: pl.program_id(0) == 0
    last = lambda: pl.program_id(0) == ni - 1
    outs = _pcall(_riding(body, 2, 1, len(ride), first, last, gather=True), name=name, grid=(ni,),
                  in_specs=in_specs + [_ANY] * len(ride), out_specs=[_rows(TM, 4 * MIX)] + [_ANY] * len(ride),
                  out_shape=[out_shape] + [_sds((N_DEV,) + r.shape, r.dtype) for r in ride],
                  scratch=list(_A2A_SEMS) * len(ride))(xh, win8, *ride)
    return outs[0], outs[1:]


def _tile_of(n, cap):
    best = None
    for t in range(128, min(n, cap) + 1, 128):
        if n % t == 0:
            best = t
    assert best is not None, n
    return best


def _mm_tn(a, b, name, a_layer=None, blocked_n=None):
    T, M = a.shape[-2:]
    N = b.shape[1]
    tm = _tile_of(M, 1408)
    nbs = max(1, 512 // blocked_n) if blocked_n else 1
    tn = blocked_n * nbs if blocked_n else _tile_of(N, 1024 if tm <= 512 else 512)
    tk = min(T, TK_DW)
    nk = T // tk

    def body(a_ref, b_ref, o_ref, acc):
        k = pl.program_id(2)

        @pl.when(k == 0)
        def _():
            acc[...] = jnp.zeros_like(acc)

        acc[...] += _dot_tn(a_ref[...].astype(BF16), b_ref[...].astype(BF16))

        @pl.when(k == nk - 1)
        def _():
            if blocked_n:
                for sb in range(nbs):
                    o_ref[sb] = acc[:, sb * blocked_n:(sb + 1) * blocked_n].astype(BF16)
            else:
                o_ref[...] = acc[...].astype(BF16)

    if a_layer is None:
        a_spec = pl.BlockSpec((tk, tm), lambda i, j, k: (k, i))
    else:
        a_spec = pl.BlockSpec((None, tk, tm), lambda i, j, k: (a_layer, k, i))
    if blocked_n:
        o_spec = pl.BlockSpec((nbs, tm, blocked_n), lambda i, j, k: (j, i, 0))
        o_shape = _sds((N // blocked_n, M, blocked_n), BF16)
    else:
        o_spec = pl.BlockSpec((tm, tn), lambda i, j, k: (i, j))
        o_shape = _sds((M, N), BF16)
    return _pcall(body, name=name, grid=(M // tm, N // tn, nk),
                  in_specs=[a_spec, pl.BlockSpec((tk, tn), lambda i, j, k: (k, j))],
                  out_specs=o_spec, out_shape=o_shape, scratch=[pltpu.VMEM((tm, tn), F32)],
                  sem=("parallel", "parallel", "arbitrary"))(a, b)


def _mm_tn_bblk(a, b3, name, per_step=2):
    T, M = a.shape
    NB, _, n = b3.shape
    tk = min(T, TK_DW)
    nk = T // tk

    def body(a_ref, b_ref, o_ref, acc):
        k = pl.program_id(1)

        @pl.when(k == 0)
        def _():
            acc[...] = jnp.zeros_like(acc)

        av = a_ref[...]
        for s in range(per_step):
            acc[s] += _dot_tn(av, b_ref[s])

        @pl.when(k == nk - 1)
        def _():
            o_ref[...] = acc[...].astype(BF16)

    return _pcall(body, name=name, grid=(NB // per_step, nk),
                  in_specs=[pl.BlockSpec((tk, M), lambda j, k: (k, 0)),
                            pl.BlockSpec((per_step, tk, n), lambda j, k: (j, k, 0))],
                  out_specs=pl.BlockSpec((per_step, M, n), lambda j, k: (j, 0, 0)),
                  out_shape=_sds((NB, M, n), BF16), scratch=[pltpu.VMEM((per_step, M, n), F32)],
                  sem=("parallel", "arbitrary"))(a, b3)


def _mm_tn_ablk(a3, b, name):
    NA, T, m = a3.shape
    N = b.shape[1]
    tn = _tile_of(N, 1024)
    tk = min(T, TK_DW)
    nk = T // tk

    def body(a_ref, b_ref, o_ref, acc):
        k = pl.program_id(2)

        @pl.when(k == 0)
        def _():
            acc[...] = jnp.zeros_like(acc)

        acc[...] += _dot_tn(a_ref[...], b_ref[...])

        @pl.when(k == nk - 1)
        def _():
            o_ref[...] = acc[...].astype(BF16)

    return _pcall(body, name=name, grid=(NA, N // tn, nk),
                  in_specs=[pl.BlockSpec((None, tk, m), lambda j, n, k: (j, k, 0)),
                            pl.BlockSpec((tk, tn), lambda j, n, k: (k, n))],
                  out_specs=pl.BlockSpec((None, m, tn), lambda j, n, k: (j, 0, n)),
                  out_shape=_sds((NA, m, N), BF16), scratch=[pltpu.VMEM((m, tn), F32)],
                  sem=("parallel", "parallel", "arbitrary"))(a3, b)


def _mm_out_ln(ya, yb, wo, l, x0, g, b, name):
    T = x0.shape[0]

    def body(ya_ref, yb_ref, w_ref, x0_ref, g_ref, b_ref, xh_ref, rs_ref, x1_ref):
        r = ALPHA * x0_ref[...] + _dot(ya_ref[...], w_ref[0:MIX, :]) + _dot(yb_ref[...], w_ref[MIX:, :])
        xhat, rstd, x1 = _ln_fwd(r, g_ref[...], b_ref[...])
        xh_ref[...] = xhat
        rs_ref[...] = rstd
        x1_ref[...] = x1.astype(BF16)

    vec = _layer((1, D_MODEL), g[1])
    return _pcall(body, name=name, grid=(T // TM,),
                  in_specs=[_rows(TM, MIX), _rows(TM, MIX), _layer((D_MODEL, D_MODEL), l), _rows(TM, D_MODEL),
                            vec, vec],
                  out_specs=[_rows(TM, D_MODEL), _rows(TM, 1), _rows(TM, D_MODEL)],
                  out_shape=[_sds((T, D_MODEL), F32), _sds((T, 1), F32), _sds((T, D_MODEL), BF16)],
                  sem=("parallel",))(ya, yb, wo, x0, g[0], b[0])


def _mm_up(x1h, wup, l, name, ride=()):
    T = x1h.shape[0]
    tm = min(T, TM_FF)
    ni = T // tm

    def body(x_ref, w_ref, hf_ref, a_ref):
        x = x_ref[...]
        g = _dot(x, w_ref[0])
        u = _dot(x, w_ref[1])
        hf_ref[0] = g.astype(BF16)
        hf_ref[1] = u.astype(BF16)
        a_ref[...] = (g * _sigmoid(g) * u).astype(BF16)

    in_specs = [pl.BlockSpec((tm, D_MODEL), lambda j, i: (i, 0)),
                pl.BlockSpec((2, None, None, D_MODEL, FF_NB), lambda j, i: (0, j, l, 0, 0))]
    out_specs = [pl.BlockSpec((2, None, tm, FF_NB), lambda j, i: (0, j, i, 0)),
                 pl.BlockSpec((None, tm, FF_NB), lambda j, i: (j, i, 0))]
    out_shape = [_sds((2, FF_J, T, FF_NB), BF16), _sds((FF_J, T, FF_NB), BF16)]
    if not ride:
        return _pcall(body, name=name, grid=(FF_J, ni), in_specs=in_specs, out_specs=out_specs,
                      out_shape=out_shape, sem=("parallel", "parallel"))(x1h, wup), ()
    first = lambda: (pl.program_id(0) == 0) & (pl.program_id(1) == 0)
    last = lambda: (pl.program_id(0) == FF_J - 1) & (pl.program_id(1) == ni - 1)
    outs = _pcall(_riding(body, 2, 2, len(ride), first, last, gather=True), name=name, grid=(FF_J, ni),
                  in_specs=in_specs + [_ANY] * len(ride), out_specs=out_specs + [_ANY] * len(ride),
                  out_shape=out_shape + [_sds((N_DEV,) + r.shape, r.dtype) for r in ride],
                  scratch=list(_A2A_SEMS) * len(ride))(x1h, wup, *ride)
    return outs[:2], outs[2:]


def _mm_down_ln(a3, wd4, l, xhat1, g1, b1, g2, b2, name, ride=()):
    T = a3.shape[1]
    ni = T // TM

    def body(a_ref, w_ref, xh1_ref, g1_ref, b1_ref, g2_ref, b2_ref, xh_ref, rs_ref, x2_ref):
        x1 = xh1_ref[...] * g1_ref[...] + b1_ref[...]
        r = ALPHA * x1
        for j in range(FF_J):
            r = r + _dot(a_ref[j], w_ref[j])
        xhat, rstd, x2 = _ln_fwd(r, g2_ref[...], b2_ref[...])
        xh_ref[...] = xhat
        rs_ref[...] = rstd
        x2_ref[...] = x2.astype(BF16)

    vec = _layer((1, D_MODEL), g1[1])
    in_specs = [pl.BlockSpec((FF_J, TM, FF_NB), lambda i: (0, i, 0)),
                _layer((FF_J, FF_NB, D_MODEL), l), _rows(TM, D_MODEL), vec, vec, vec, vec]
    out_specs = [_rows(TM, D_MODEL), _rows(TM, 1), _rows(TM, D_MODEL)]
    out_shape = [_sds((T, D_MODEL), F32), _sds((T, 1), F32), _sds((T, D_MODEL), BF16)]
    operands = (a3, wd4, xhat1, g1[0], b1[0], g2[0], b2[0])
    if not ride:
        return _pcall(body, name=name, grid=(ni,), in_specs=in_specs, out_specs=out_specs, out_shape=out_shape,
                      sem=("parallel",))(*operands), ()
    first = lambda: pl.program_id(0) == 0
    last = lambda: pl.program_id(0) == ni - 1
    outs = _pcall(_riding(body, 7, 3, len(ride), first, last, gather=True), name=name, grid=(ni,),
                  in_specs=in_specs + [_ANY] * len(ride), out_specs=out_specs + [_ANY] * len(ride),
                  out_shape=out_shape + [_sds((N_DEV,) + r.shape, r.dtype) for r in ride],
                  scratch=list(_A2A_SEMS) * len(ride))(*operands, *ride)
    return outs[:3], outs[3:]


def _mm_ple(x2h, xhat2, g2, b2, p, wg, bg, wp, l, name, lw=None):
    lw = l if lw is None else lw
    T = x2h.shape[0]

    def body(x2h_ref, xh_ref, g2_ref, b2_ref, p_ref, wg_ref, bg_ref, wp_ref, o_ref, oh_ref):
        x2 = xh_ref[...] * g2_ref[...] + b2_ref[...]
        gate = _sigmoid(_dot(x2h_ref[...], wg_ref[...]) + bg_ref[...])
        pp = _dot(p_ref[...].astype(BF16), wp_ref[...])
        x3 = x2 + gate * pp
        o_ref[...] = x3
        oh_ref[...] = x3.astype(BF16)

    vec = _layer((1, D_MODEL), l)
    return _pcall(body, name=name, grid=(T // TM,),
                  in_specs=[_rows(TM, D_MODEL), _rows(TM, D_MODEL), vec, vec,
                            pl.BlockSpec((None, TM, D_PLE), lambda i: (l, i, 0)),
                            _layer((D_MODEL, D_MODEL), lw), vec, _layer((D_PLE, D_MODEL), l)],
                  out_specs=[_rows(TM, D_MODEL), _rows(TM, D_MODEL)],
                  out_shape=[_sds((T, D_MODEL), F32), _sds((T, D_MODEL), BF16)],
                  sem=("parallel",))(x2h, xhat2, g2, b2, p, wg, bg, wp)


def _loss_head(x3, tgt, name):
    T = x3.shape[0]

    def body(x_ref, t_ref, dx_ref, l_ref):
        e = x_ref[...] - t_ref[...]
        dx_ref[...] = e * (1.0 / D_MODEL)

        @pl.when(pl.program_id(0) == 0)
        def _():
            l_ref[...] = jnp.zeros_like(l_ref)

        l_ref[...] += (0.5 / D_MODEL) * jnp.sum(e * e).reshape(1, 1)

    return _pcall(body, name=name, grid=(T // TM,),
                  in_specs=[_rows(TM, D_MODEL), _rows(TM, D_MODEL)],
                  out_specs=[_rows(TM, D_MODEL), _full((1, 1))],
                  out_shape=[_sds((T, D_MODEL), F32), _sds((1, 1), F32)])(x3, tgt)


def _ple_bwd(dx3, x2h, p, wg, bg, wp, l, xhat2, rstd2, g2, name, lw=None):
    T = dx3.shape[0]
    lw = l if lw is None else lw

    def body(dx3_ref, x2h_ref, p_ref, wg_ref, bg_ref, wp_ref, xh_ref, rs_ref, g2_ref,
             dr_ref, drh_ref, dpre_ref, dpp_ref, dbg_ref, dg_ref, db_ref):
        dx3 = dx3_ref[...]
        gate = _sigmoid(_dot(x2h_ref[...], wg_ref[...]) + bg_ref[...])
        pp = _dot(p_ref[...].astype(BF16), wp_ref[...])
        dpre = dx3 * pp * gate * (1.0 - gate)
        dpreh = dpre.astype(BF16)
        dpre_ref[...] = dpreh
        dpp_ref[...] = (dx3 * gate).astype(BF16)
        dx2 = dx3 + _dot_nt(dpreh, wg_ref[...])
        xhat = xh_ref[...]
        dr = _ln_bwd(dx2, xhat, rs_ref[...], g2_ref[...])
        dr_ref[...] = dr
        drh_ref[...] = dr.astype(BF16)

        @pl.when(pl.program_id(0) == 0)
        def _():
            dbg_ref[...] = jnp.zeros_like(dbg_ref)
            dg_ref[...] = jnp.zeros_like(dg_ref)
            db_ref[...] = jnp.zeros_like(db_ref)

        dbg_ref[...] += _colsum(dpre)
        dg_ref[...] += _colsum(dx2 * xhat)
        db_ref[...] += _colsum(dx2)

    vec = _layer((1, D_MODEL), l)
    acc = _full((1, D_MODEL))
    big = _rows(TM, D_MODEL)
    return _pcall(body, name=name, grid=(T // TM,),
                  in_specs=[big, big, pl.BlockSpec((None, TM, D_PLE), lambda i: (l, i, 0)),
                            _layer((D_MODEL, D_MODEL), lw), vec, _layer((D_PLE, D_MODEL), l),
                            big, _rows(TM, 1), vec],
                  out_specs=[big, big, big, big, acc, acc, acc],
                  out_shape=[_sds((T, D_MODEL), F32), _sds((T, D_MODEL), BF16), _sds((T, D_MODEL), BF16),
                             _sds((T, D_MODEL), BF16), _sds((1, D_MODEL), F32), _sds((1, D_MODEL), F32),
                             _sds((1, D_MODEL), F32)])(dx3, x2h, p, wg, bg, wp, xhat2, rstd2, g2)


def _ffn_bwd1(dr2h, wd4, l, hf, name, ride=()):
    T = dr2h.shape[0]
    tm = min(T, TM_FF)
    ni = T // tm

    def body(d_ref, w_ref, hf_ref, o_ref):
        da = _dot_nt(d_ref[...], w_ref[...])
        g = hf_ref[0].astype(F32)
        u = hf_ref[1].astype(F32)
        sg = _sigmoid(g)
        o_ref[0] = (da * u * (sg * (1.0 + g * (1.0 - sg)))).astype(BF16)
        o_ref[1] = (da * (g * sg)).astype(BF16)

    blk = pl.BlockSpec((2, None, tm, FF_NB), lambda j, i: (0, j, i, 0))
    in_specs = [pl.BlockSpec((tm, D_MODEL), lambda j, i: (i, 0)),
                pl.BlockSpec((None, None, FF_NB, D_MODEL), lambda j, i: (l, j, 0, 0)), blk]
    out_shape = _sds((2, FF_J, T, FF_NB), BF16)
    if not ride:
        return _pcall(body, name=name, grid=(FF_J, ni), in_specs=in_specs, out_specs=blk, out_shape=out_shape,
                      sem=("parallel", "parallel"))(dr2h, wd4, hf), ()
    first = lambda: (pl.program_id(0) == 0) & (pl.program_id(1) == 0)
    last = lambda: (pl.program_id(0) == FF_J - 1) & (pl.program_id(1) == ni - 1)
    outs = _pcall(_riding(body, 3, 1, len(ride), first, last), name=name, grid=(FF_J, ni),
                  in_specs=in_specs + [_ANY] * len(ride), out_specs=[blk] + [_ANY] * len(ride),
                  out_shape=[out_shape] + [_sds(r.shape, r.dtype) for r in ride],
                  scratch=list(_A2A_SEMS) * len(ride))(dr2h, wd4, hf, *ride)
    return outs[0], outs[1:]


_TM_B2 = 512


def _ffn_bwd2(dhf, wup, l, dr2, xhat1, rstd1, g1, name, ride=()):
    T = dr2.shape[0]
    tm = min(T, _TM_B2)
    ni = T // tm

    def body(dh_ref, w_ref, dr2_ref, xh_ref, rs_ref, g_ref, dr_ref, drh_ref, dg_ref, db_ref):
        dx1 = ALPHA * dr2_ref[...]
        for s in range(2):
            for j in range(FF_J):
                dx1 = dx1 + _dot_nt(dh_ref[s, j], w_ref[s, j])
        xhat = xh_ref[...]
        dr = _ln_bwd(dx1, xhat, rs_ref[...], g_ref[...])
        dr_ref[...] = dr
        drh_ref[...] = dr.astype(BF16)

        @pl.when(pl.program_id(0) == 0)
        def _():
            dg_ref[...] = jnp.zeros_like(dg_ref)
            db_ref[...] = jnp.zeros_like(db_ref)

        dg_ref[...] += _colsum(dx1 * xhat)
        db_ref[...] += _colsum(dx1)

    big = _rows(tm, D_MODEL)
    acc = _full((1, D_MODEL))
    in_specs = [pl.BlockSpec((2, FF_J, tm, FF_NB), lambda i: (0, 0, i, 0)),
                pl.BlockSpec((2, FF_J, None, D_MODEL, FF_NB), lambda i: (0, 0, l, 0, 0),
                             pipeline_mode=pl.Buffered(1)),
                big, big, _rows(tm, 1), _layer((1, D_MODEL), g1[1])]
    out_specs = [big, big, acc, acc]
    out_shape = [_sds((T, D_MODEL), F32), _sds((T, D_MODEL), BF16), _sds((1, D_MODEL), F32),
                 _sds((1, D_MODEL), F32)]
    operands = (dhf, wup, dr2, xhat1, rstd1, g1[0])
    if not ride:
        return _pcall(body, name=name, grid=(ni,), in_specs=in_specs, out_specs=out_specs,
                      out_shape=out_shape)(*operands), ()
    first = lambda: pl.program_id(0) == 0
    last = lambda: pl.program_id(0) == ni - 1
    outs = _pcall(_riding(body, 6, 4, len(ride), first, last), name=name, grid=(ni,),
                  in_specs=in_specs + [_ANY] * len(ride), out_specs=out_specs + [_ANY] * len(ride),
                  out_shape=out_shape + [_sds(r.shape, r.dtype) for r in ride],
                  scratch=list(_A2A_SEMS) * len(ride))(*operands, *ride)
    return outs[:4], outs[4:]


def _out_bwd(dr1h, wo, l, name):
    T = dr1h.shape[0]

    def body(d_ref, w_ref, da_ref, db_ref):
        d = d_ref[...]
        da_ref[...] = _dot_nt(d, w_ref[0:MIX, :])
        db_ref[...] = _dot_nt(d, w_ref[MIX:, :])

    return _pcall(body, name=name, grid=(T // TM,),
                  in_specs=[_rows(TM, D_MODEL), _layer((D_MODEL, D_MODEL), l)],
                  out_specs=[_rows(TM, MIX), _rows(TM, MIX)],
                  out_shape=[_sds((T, MIX), F32), _sds((T, MIX), F32)],
                  sem=("parallel",))(dr1h, wo)


def _in_bwd(dh, win8, l, dr1, name):
    T = dr1.shape[0]

    def body(dh_ref, w_ref, dr_ref, o_ref):
        acc = ALPHA * dr_ref[...]
        for b in range(N_DEV):
            acc = acc + _dot_nt(dh_ref[:, b * IN_NB:(b + 1) * IN_NB], w_ref[b])
        o_ref[...] = acc

    return _pcall(body, name=name, grid=(T // TM,),
                  in_specs=[_rows(TM, 4 * MIX), pl.BlockSpec((N_DEV, None, D_MODEL, IN_NB), lambda i: (0, l, 0, 0)),
                            _rows(TM, D_MODEL)],
                  out_specs=_rows(TM, D_MODEL), out_shape=_sds((T, D_MODEL), F32),
                  sem=("parallel",))(dh, win8, dr1)


def _s5_operators(lre, lim, log_dt, bre, bim, cre, cim, dskip):
    G, P, H, LC = S5_GROUPS, S5_STATE, S5_GROUP, S5_LC
    dt = jnp.exp(log_dt)[:, None]
    mag = jnp.exp(lre * dt)
    ang = lim * dt
    lb_re = mag * jnp.cos(ang)
    lb_im = mag * jnp.sin(ang)
    den = lre * lre + lim * lim
    nr = lb_re - 1.0
    ni = lb_im
    r_re = (nr * lre + ni * lim) / den
    r_im = (ni * lre - nr * lim) / den
    bb_re = r_re[..., None] * bre - r_im[..., None] * bim
    bb_im = r_re[..., None] * bim + r_im[..., None] * bre
    k = jnp.arange(LC + 1, dtype=F32)[:, None, None]
    pmag = jnp.exp(k * (lre * dt)[None])
    pang = k * ang[None]
    pw_re = pmag * jnp.cos(pang)
    pw_im = pmag * jnp.sin(pang)
    cp_re = cre[None] * pw_re[:, :, None, :] - cim[None] * pw_im[:, :, None, :]
    cp_im = cre[None] * pw_im[:, :, None, :] + cim[None] * pw_re[:, :, None, :]
    kk = (jnp.einsum('kghp,gpj->kghj', cp_re[:LC], bb_re, precision=HI)
          - jnp.einsum('kghp,gpj->kghj', cp_im[:LC], bb_im, precision=HI))
    dmat = dskip.reshape(G, H)[:, :, None] * jnp.eye(H, dtype=F32)[None]
    kk = jnp.concatenate([kk[:1] + dmat[None], kk[1:]], axis=0)
    r0 = jnp.transpose(kk, (1, 3, 0, 2)).reshape(G, H, LC * H)
    qt = jnp.stack([cp_re[1:], -cp_im[1:]], axis=0)
    qt = jnp.transpose(qt, (2, 0, 4, 1, 3)).reshape(G, 2 * P, LC * H)
    pb_re = pw_re[:LC, :, :, None] * bb_re[None] - pw_im[:LC, :, :, None] * bb_im[None]
    pb_im = pw_re[:LC, :, :, None] * bb_im[None] + pw_im[:LC, :, :, None] * bb_re[None]
    pm = jnp.stack([pb_re[::-1], pb_im[::-1]], axis=0)
    pm = jnp.transpose(pm, (2, 1, 4, 0, 3)).reshape(G, LC * H, 2 * P)
    a_re = pw_re[LC]
    a_im = pw_im[LC]
    a1 = jnp.concatenate([a_re, a_re], axis=-1)
    a2 = jnp.concatenate([-a_im, a_im], axis=-1)
    return r0, qt, pm, a1, a2


_CHUNK_BLOCK = 128
_LANE_GROUPS = 128 // S5_GROUP


def _to_chunks(u, name):
    T = u.shape[0]
    nc = T // S5_LC
    cb = min(nc, _CHUNK_BLOCK)

    def body(x_ref, o_ref, scr):
        for s in range(S5_LC):
            xt = x_ref[pl.ds(s, cb, stride=S5_LC), :].T
            for g in range(_LANE_GROUPS):
                scr[g, s * S5_GROUP:(s + 1) * S5_GROUP, :] = xt[g * S5_GROUP:(g + 1) * S5_GROUP, :]
        for g in range(_LANE_GROUPS):
            o_ref[g] = scr[g].T

    return _pcall(body, name=name, grid=(nc // cb, S5_GROUPS // _LANE_GROUPS),
                  in_specs=[pl.BlockSpec((cb * S5_LC, 128), lambda i, k: (i, k))],
                  out_specs=pl.BlockSpec((_LANE_GROUPS, cb, S5_LW), lambda i, k: (k, i, 0)),
                  out_shape=_sds((S5_GROUPS, nc, S5_LW), F32),
                  scratch=[pltpu.VMEM((_LANE_GROUPS, S5_LW, cb), F32)], sem=("parallel", "parallel"))(u)


def _from_chunks(m, name):
    nc = m.shape[1]
    cb = min(nc, _CHUNK_BLOCK)

    def body(m_ref, y_ref, scr, rows):
        for g in range(_LANE_GROUPS):
            scr[g] = m_ref[g].T
        for s in range(S5_LC):
            for g in range(_LANE_GROUPS):
                rows[g * S5_GROUP:(g + 1) * S5_GROUP, :] = scr[g, s * S5_GROUP:(s + 1) * S5_GROUP, :]
            y_ref[pl.ds(s, cb, stride=S5_LC), :] = rows[...].T

    return _pcall(body, name=name, grid=(nc // cb, S5_GROUPS // _LANE_GROUPS),
                  in_specs=[pl.BlockSpec((_LANE_GROUPS, cb, S5_LW), lambda i, k: (k, i, 0))],
                  out_specs=pl.BlockSpec((cb * S5_LC, 128), lambda i, k: (i, k)),
                  out_shape=_sds((nc * S5_LC, MIX), F32),
                  scratch=[pltpu.VMEM((_LANE_GROUPS, S5_LW, cb), F32), pltpu.VMEM((128, cb), F32)],
                  sem=("parallel", "parallel"))(m)


def _gspec(r, c):
    return pl.BlockSpec((None, r, c), lambda g: (g, 0, 0))


def _s5_chunk_state(umat, pm, name):
    G, nc, _ = umat.shape

    def body(u_ref, p_ref, o_ref):
        o_ref[...] = _mm3(_dot, _split(u_ref[...]), _split(p_ref[...]))

    return _pcall(body, name=name, grid=(G,), in_specs=[_gspec(nc, S5_LW), _gspec(S5_LW, 128)],
                  out_specs=_gspec(nc, 128), out_shape=_sds((G, nc, 128), F32), sem=("parallel",))(umat, pm)


_SCAN_G = 8
_SCAN_UNROLL = 8


def _s5_scan_fwd(s_t, a1, a2, name):
    nc, G, _ = s_t.shape

    def body(s_ref, a1_ref, a2_ref, o_ref, sb_ref):
        sb_ref[...] = pltpu.roll(s_ref[...], 64, 2)
        a1v = a1_ref[...]
        a2v = a2_ref[...]

        def step(c, carry):
            x, xb = carry
            o_ref[c] = x
            return a1v * x + a2v * xb + s_ref[c], a1v * xb - a2v * x + sb_ref[c]

        zero = jnp.zeros((_SCAN_G, 128), F32)
        lax.fori_loop(0, nc, step, (zero, zero), unroll=_SCAN_UNROLL)

    blk = pl.BlockSpec((nc, _SCAN_G, 128), lambda g: (0, g, 0))
    vec = pl.BlockSpec((_SCAN_G, 128), lambda g: (g, 0))
    return _pcall(body, name=name, grid=(G // _SCAN_G,), in_specs=[blk, vec, vec], out_specs=blk,
                  out_shape=_sds((nc, G, 128), F32), scratch=[pltpu.VMEM((nc, _SCAN_G, 128), F32)],
                  sem=("parallel",))(s_t, a1, a2)


def _s5_scan_bwd(dxp_t, xp_t, a1, a2, name):
    nc, G, _ = dxp_t.shape

    def body(dx_ref, x_ref, a1_ref, a2_ref, ds_ref, da1_ref, da2_ref, dxb_ref, xb_ref):
        dxb_ref[...] = pltpu.roll(dx_ref[...], 64, 2)
        xb_ref[...] = pltpu.roll(x_ref[...], 64, 2)
        a1v = a1_ref[...]
        a2v = a2_ref[...]
        zero = jnp.zeros((_SCAN_G, 128), F32)

        def step(n, carry):
            gc, gb, d1, d2 = carry
            c = nc - 1 - n
            ds_ref[c] = gc
            d1 = d1 + gc * x_ref[c]
            d2 = d2 + gc * xb_ref[c]
            return dx_ref[c] + a1v * gc - a2v * gb, dxb_ref[c] + a1v * gb + a2v * gc, d1, d2

        _, _, d1, d2 = lax.fori_loop(0, nc, step, (zero, zero, zero, zero), unroll=_SCAN_UNROLL)
        da1_ref[...] = d1
        da2_ref[...] = d2

    blk = pl.BlockSpec((nc, _SCAN_G, 128), lambda g: (0, g, 0))
    vec = pl.BlockSpec((_SCAN_G, 128), lambda g: (g, 0))
    return _pcall(body, name=name, grid=(G // _SCAN_G,), in_specs=[blk, blk, vec, vec],
                  out_specs=[blk, vec, vec],
                  out_shape=[_sds((nc, G, 128), F32), _sds((G, 128), F32), _sds((G, 128), F32)],
                  scratch=[pltpu.VMEM((nc, _SCAN_G, 128), F32)] * 2, sem=("parallel",))(dxp_t, xp_t, a1, a2)


def _toeplitz_rows(r0, mt):
    lane = lax.broadcasted_iota(jnp.int32, (S5_GROUP, S5_LW), 1)
    mt[0:S5_GROUP, :] = r0
    for s in range(1, S5_LC):
        mt[s * S5_GROUP:(s + 1) * S5_GROUP, :] = jnp.where(lane >= s * S5_GROUP, pltpu.roll(r0, s * S5_GROUP, 1), 0.0)


def _toeplitz_rows_t(dmt):
    lane = lax.broadcasted_iota(jnp.int32, (S5_GROUP, S5_LW), 1)
    acc = dmt[0:S5_GROUP, :]
    for s in range(1, S5_LC):
        blk = dmt[s * S5_GROUP:(s + 1) * S5_GROUP, :]
        acc = acc + jnp.where(lane < S5_LW - s * S5_GROUP, pltpu.roll(blk, S5_LW - s * S5_GROUP, 1), 0.0)
    return acc


def _s5_output(umat, xprev, r0, qt, name):
    G, nc, _ = umat.shape

    def body(u_ref, x_ref, r_ref, q_ref, o_ref, mt):
        _toeplitz_rows(r_ref[...], mt)
        o_ref[...] = (_mm3(_dot, _split(u_ref[...]), _split(mt[...]))
                      + _mm3(_dot, _split(x_ref[...]), _split(q_ref[...])))

    return _pcall(body, name=name, grid=(G,),
                  in_specs=[_gspec(nc, S5_LW), _gspec(nc, 128), _gspec(S5_GROUP, S5_LW), _gspec(128, S5_LW)],
                  out_specs=_gspec(nc, S5_LW), out_shape=_sds((G, nc, S5_LW), F32),
                  scratch=[pltpu.VMEM((S5_LW, S5_LW), F32)], sem=("parallel",))(umat, xprev, r0, qt)


def _s5_bwd_state(dymat, qt, name):
    G, nc, _ = dymat.shape

    def body(d_ref, q_ref, o_ref):
        o_ref[...] = _mm3(_dot_nt, _split(d_ref[...]), _split(q_ref[...]))

    return _pcall(body, name=name, grid=(G,), in_specs=[_gspec(nc, S5_LW), _gspec(128, S5_LW)],
                  out_specs=_gspec(nc, 128), out_shape=_sds((G, nc, 128), F32), sem=("parallel",))(dymat, qt)


def _s5_bwd_main(umat, xprev, dymat, ds, r0, pm, name):
    G, nc, _ = umat.shape

    def body(u_ref, x_ref, dy_ref, ds_ref, r_ref, p_ref, du_ref, dr_ref, dq_ref, dp_ref, mt):
        u = _split(u_ref[...])
        dy = _split(dy_ref[...])
        dsv = _split(ds_ref[...])
        _toeplitz_rows(r_ref[...], mt)
        du_ref[...] = _mm3(_dot_nt, dy, _split(mt[...])) + _mm3(_dot_nt, dsv, _split(p_ref[...]))
        dq_ref[...] = _mm3(_dot_tn, _split(x_ref[...]), dy)
        dp_ref[...] = _mm3(_dot_tn, u, dsv)
        mt[...] = _mm3(_dot_tn, u, dy)
        dr_ref[...] = _toeplitz_rows_t(mt)

    return _pcall(body, name=name, grid=(G,),
                  in_specs=[_gspec(nc, S5_LW), _gspec(nc, 128), _gspec(nc, S5_LW), _gspec(nc, 128),
                            _gspec(S5_GROUP, S5_LW), _gspec(S5_LW, 128)],
                  out_specs=[_gspec(nc, S5_LW), _gspec(S5_GROUP, S5_LW), _gspec(128, S5_LW), _gspec(S5_LW, 128)],
                  out_shape=[_sds((G, nc, S5_LW), F32), _sds((G, S5_GROUP, S5_LW), F32), _sds((G, 128, S5_LW), F32),
                             _sds((G, S5_LW, 128), F32)],
                  scratch=[pltpu.VMEM((S5_LW, S5_LW), F32)], sem=("parallel",))(umat, xprev, dymat, ds, r0, pm)


def _glu_fwd(y, wglu, bglu, l, name):
    T = y.shape[0]

    def body(y_ref, w_ref, b_ref, o_ref, g_ref):
        g = _gelu(y_ref[...])
        gh = g.astype(BF16)
        z = _dot(gh, w_ref[...]) + b_ref[...]
        o_ref[...] = (g * _sigmoid(z)).astype(BF16)
        g_ref[...] = gh

    return _pcall(body, name=name, grid=(T // TM,),
                  in_specs=[_rows(TM, MIX), _layer((MIX, MIX), l), _layer((1, MIX), l)],
                  out_specs=[_rows(TM, MIX), _rows(TM, MIX)],
                  out_shape=[_sds((T, MIX), BF16), _sds((T, MIX), BF16)], sem=("parallel",))(y, wglu, bglu)


def _glu_bwd(y, dout, wglu, bglu, l, name):
    T = y.shape[0]

    def body(y_ref, do_ref, w_ref, b_ref, dy_ref, dz_ref, db_ref):
        yv = y_ref[...]
        do = do_ref[...]
        g = _gelu(yv)
        s = _sigmoid(_dot(g.astype(BF16), w_ref[...]) + b_ref[...])
        dz = do * g * s * (1.0 - s)
        dzh = dz.astype(BF16)
        dz_ref[...] = dzh
        dg = do * s + _dot_nt(dzh, w_ref[...])
        dy_ref[...] = dg * _gelu_grad(yv)

        @pl.when(pl.program_id(0) == 0)
        def _():
            db_ref[...] = jnp.zeros_like(db_ref)

        db_ref[...] += _colsum(dz)

    return _pcall(body, name=name, grid=(T // TM,),
                  in_specs=[_rows(TM, MIX), _rows(TM, MIX), _layer((MIX, MIX), l), _layer((1, MIX), l)],
                  out_specs=[_rows(TM, MIX), _rows(TM, MIX), _full((1, MIX))],
                  out_shape=[_sds((T, MIX), F32), _sds((T, MIX), BF16), _sds((1, MIX), F32)])(y, dout, wglu, bglu)


def _prev_rows(T, h, s=0):
    return pl.BlockSpec((h, MIX), lambda i: (jnp.maximum(i * (TM // h) - 1, 0), s))


def _next_rows(T, h, s=0):
    return pl.BlockSpec((h, MIX), lambda i: (jnp.minimum((i + 1) * (TM // h), T // h - 1), s))


def _conv_fwd(h, w, l, name):
    T = h.shape[0]

    def body(b_ref, c_ref, x_ref, ch_ref, xh_ref, w_ref, o_ref, ext):
        i = pl.program_id(0)
        z = c_ref[...] * x_ref[...]
        ext[0:8, :] = jnp.where(i > 0, ch_ref[...] * xh_ref[...], 0.0)
        ext[8:, :] = z
        y = (w_ref[0:1, :] * ext[pl.ds(6, TM), :] + w_ref[1:2, :] * ext[pl.ds(7, TM), :] + w_ref[2:3, :] * z)
        o_ref[...] = (b_ref[...] * y).astype(BF16)

    return _pcall(body, name=name, grid=(T // TM,),
                  in_specs=[_cols(TM, 1), _cols(TM, 2), _cols(TM, 3), _prev_rows(T, 8, 2), _prev_rows(T, 8, 3),
                            _layer((3, MIX), l)],
                  out_specs=_rows(TM, MIX), out_shape=_sds((T, MIX), BF16),
                  scratch=[pltpu.VMEM((TM + 8, MIX), F32)], sem=("parallel",))(h, h, h, h, h, w)


def _conv_bwd(dout, h, dua, w, l, name):
    T = h.shape[0]
    nb = T // TM

    def body(do_ref, b_ref, c_ref, x_ref, ch_ref, xh_ref, don_ref, bn_ref, du_ref, w_ref,
             dh_ref, dw_ref, ext, ext2):
        i = pl.program_id(0)
        c = c_ref[...]
        x = x_ref[...]
        z = c * x
        ext[0:8, :] = jnp.where(i > 0, ch_ref[...] * xh_ref[...], 0.0)
        ext[8:, :] = z
        zm2 = ext[pl.ds(6, TM), :]
        zm1 = ext[pl.ds(7, TM), :]
        w0 = w_ref[0:1, :]
        w1 = w_ref[1:2, :]
        w2 = w_ref[2:3, :]
        y = w0 * zm2 + w1 * zm1 + w2 * z
        do = do_ref[...]
        dy = do * b_ref[...]
        ext2[0:TM, :] = dy
        ext2[TM:, :] = jnp.where(i < nb - 1, don_ref[...] * bn_ref[...], 0.0)
        dz = w2 * dy + w1 * ext2[pl.ds(1, TM), :] + w0 * ext2[pl.ds(2, TM), :]
        dh_ref[:, 0:MIX] = du_ref[...].astype(BF16)
        dh_ref[:, MIX:2 * MIX] = (do * y).astype(BF16)
        dh_ref[:, 2 * MIX:3 * MIX] = (dz * x).astype(BF16)
        dh_ref[:, 3 * MIX:] = (dz * c).astype(BF16)

        @pl.when(i == 0)
        def _():
            dw_ref[...] = jnp.zeros_like(dw_ref)

        dw_ref[0:1, :] += _colsum(dy * zm2)
        dw_ref[1:2, :] += _colsum(dy * zm1)
        dw_ref[2:3, :] += _colsum(dy * z)

    return _pcall(body, name=name, grid=(nb,),
                  in_specs=[_rows(TM, MIX), _cols(TM, 1), _cols(TM, 2), _cols(TM, 3), _prev_rows(T, 8, 2),
                            _prev_rows(T, 8, 3), _next_rows(T, 8), _next_rows(T, 8, 1), _rows(TM, MIX),
                            _layer((3, MIX), l)],
                  out_specs=[_rows(TM, 4 * MIX), _full((8, MIX))],
                  out_shape=[_sds((T, 4 * MIX), BF16), _sds((8, MIX), F32)],
                  scratch=[pltpu.VMEM((TM + 8, MIX), F32), pltpu.VMEM((TM + 8, MIX), F32)])(
                      dout, h, h, h, h, h, dout, h, dua, w)


_PH = 16


def _pooled(ext, t, gi, w):
    lo = gi * POOL_GROUP
    cur = ext[pl.ds(_PH, TM), lo:lo + POOL_GROUP]
    acc = cur
    for k in range(1, w):
        acc = acc + ext[pl.ds(_PH - k, TM), lo:lo + POOL_GROUP]
    cnt = jnp.minimum(t + 1, w).astype(F32)
    return acc / cnt - cur, cnt


def _pool_fwd(h, pw, scale, l, name):
    T = h.shape[0]

    def body(z_ref, zh_ref, pw_ref, sc_ref, o_ref, ext):
        i = pl.program_id(0)
        ext[0:_PH, :] = jnp.where(i > 0, zh_ref[...], 0.0)
        ext[_PH:, :] = z_ref[...]
        t = i * TM + lax.broadcasted_iota(jnp.int32, (TM, 1), 0)
        for gi, w in enumerate(POOL_WINDOWS):
            lo = gi * POOL_GROUP
            pooled, _ = _pooled(ext, t, gi, w)
            mixed = _dot(pooled.astype(BF16), pw_ref[gi].astype(BF16))
            o_ref[:, lo:lo + POOL_GROUP] = (mixed * sc_ref[:, lo:lo + POOL_GROUP]).astype(BF16)

    return _pcall(body, name=name, grid=(T // TM,),
                  in_specs=[_cols(TM, 3), _prev_rows(T, _PH, 3), _layer((4, POOL_GROUP, POOL_GROUP), l),
                            _layer((1, MIX), l)],
                  out_specs=_rows(TM, MIX), out_shape=_sds((T, MIX), BF16),
                  scratch=[pltpu.VMEM((TM + _PH, MIX), F32)], sem=("parallel",))(h, h, pw, scale)


def _pool_bwd(dout, h, dq, dk, dv, pw, scale, l, name):
    T = h.shape[0]
    nb = T // TM

    def body(do_ref, don_ref, z_ref, zh_ref, dq_ref, dk_ref, dv_ref, pw_ref, sc_ref,
             dh_ref, dpw_ref, dsc_ref, ext, ext2):
        i = pl.program_id(0)
        ext[0:_PH, :] = jnp.where(i > 0, zh_ref[...], 0.0)
        ext[_PH:, :] = z_ref[...]
        t = i * TM + lax.broadcasted_iota(jnp.int32, (TM, 1), 0)
        dh_ref[:, 0:MIX] = dq_ref[...]
        dh_ref[:, MIX:2 * MIX] = dk_ref[...]
        dh_ref[:, 2 * MIX:3 * MIX] = dv_ref[...]

        @pl.when(i == 0)
        def _():
            dpw_ref[...] = jnp.zeros_like(dpw_ref)
            dsc_ref[...] = jnp.zeros_like(dsc_ref)

        for gi, w in enumerate(POOL_WINDOWS):
            lo = gi * POOL_GROUP
            pwb = pw_ref[gi].astype(BF16)
            sc = sc_ref[:, lo:lo + POOL_GROUP]
            pooled, cnt = _pooled(ext, t, gi, w)
            pb = pooled.astype(BF16)
            mixed = _dot(pb, pwb)
            dog = do_ref[:, lo:lo + POOL_GROUP]
            dsc_ref[:, lo:lo + POOL_GROUP] += _colsum(dog * mixed)
            dmix = (dog * sc).astype(BF16)
            dpw_ref[gi] += _dot_tn(pb, dmix)
            dpool = _dot_nt(dmix, pwb)
            dmix_n = (jnp.where(i < nb - 1, don_ref[:, lo:lo + POOL_GROUP], 0.0) * sc).astype(BF16)
            dpool_n = _dot_nt(dmix_n, pwb)
            e = dpool / cnt
            ext2[0:TM, lo:lo + POOL_GROUP] = e
            ext2[TM:, lo:lo + POOL_GROUP] = dpool_n * (1.0 / w)
            s = e
            for k in range(1, w):
                s = s + ext2[pl.ds(k, TM), lo:lo + POOL_GROUP]
            dh_ref[:, 3 * MIX + lo:3 * MIX + lo + POOL_GROUP] = (s - dpool).astype(BF16)

    blk = _rows(TM, MIX)
    return _pcall(body, name=name, grid=(nb,),
                  in_specs=[blk, _next_rows(T, _PH), _cols(TM, 3), _prev_rows(T, _PH, 3), blk, blk, blk,
                            _layer((4, POOL_GROUP, POOL_GROUP), l), _layer((1, MIX), l)],
                  out_specs=[_rows(TM, 4 * MIX), _full((4, POOL_GROUP, POOL_GROUP)), _full((1, MIX))],
                  out_shape=[_sds((T, 4 * MIX), BF16), _sds((4, POOL_GROUP, POOL_GROUP), F32), _sds((1, MIX), F32)],
                  scratch=[pltpu.VMEM((TM + _PH, MIX), F32), pltpu.VMEM((TM + _PH, MIX), F32)])(
                      dout, dout, h, h, dq, dk, dv, pw, scale)


def _att_bias_table(rel_bias):
    span = LEFT_CHUNKS * CHUNK
    assert ATT_TQ - 1 <= MAX_REL
    lo = MAX_REL - (ATT_TQ - 1)
    near = rel_bias[:, lo:2 * MAX_REL + 1]
    far = jnp.broadcast_to(rel_bias[:, 2 * MAX_REL:], (HEADS, span + ATT_TQ - 1 - MAX_REL))
    by_dist = jnp.concatenate([near, far], axis=1)
    rev = by_dist[:, ::-1]
    lv = rev.shape[1]
    skew = jnp.tile(rev, (1, ATT_TQ + 1))[:, :ATT_TQ * (lv + 1)].reshape(HEADS, ATT_TQ, lv + 1)[:, :, :ATT_W]
    bias = skew[:, ::-1, :]
    r = jnp.arange(ATT_TQ)[:, None]
    col = jnp.arange(ATT_W)[None, :]
    dchunk = (LEFT_CHUNKS + r // CHUNK) - col // CHUNK
    visible = (dchunk >= 0) & (dchunk <= LEFT_CHUNKS)
    return jnp.where(visible[None], bias, NEG_INF)


def _qrows(n, s=0):
    return pl.BlockSpec((ATT_TQ, n), lambda i: (i, s))


def _att_views_back(d, s):
    return pl.BlockSpec((ATT_TQ, MIX), lambda i: (jnp.maximum(i - (ATT_NV - 1) + d, 0), s))


_PAIR = 2 * HEAD_DIM


def _att_pair(refs, pp):
    sl = slice(pp * _PAIR, (pp + 1) * _PAIR)
    if isinstance(refs, (tuple, list)):
        return jnp.concatenate([r[:, sl] for r in refs], axis=0)
    return refs[:, sl]


def _att_fwd(h, table, name):
    T = h.shape[0]

    def body(q_ref, *refs):
        k_refs = refs[:ATT_NV]
        v_refs = refs[ATT_NV:2 * ATT_NV]
        tb_ref = refs[2 * ATT_NV]
        o_ref, oh_ref, lse_ref = refs[2 * ATT_NV + 1:]
        i = pl.program_id(0)
        col = lax.broadcasted_iota(jnp.int32, (1, ATT_W), 1)
        kvalid = (col + (i - (ATT_NV - 1)) * ATT_TQ) >= 0
        first = lax.broadcasted_iota(jnp.int32, (1, _PAIR), 1) < HEAD_DIM
        lses = []
        for pp in range(HEADS // 2):
            qp = _att_pair(q_ref, pp) * (HEAD_DIM ** -0.5)
            kp = _att_pair(k_refs, pp).astype(BF16)
            vp = _att_pair(v_refs, pp).astype(BF16)
            outs = []
            for e in range(2):
                half = first if e == 0 else jnp.logical_not(first)
                s = _dot_nt(jnp.where(half, qp, 0.0).astype(BF16), kp) + tb_ref[2 * pp + e]
                s = jnp.where(kvalid, s, NEG_INF)
                m = jnp.max(s, axis=-1, keepdims=True)
                p = jnp.exp(s - m)
                l = jnp.sum(p, axis=-1, keepdims=True)
                outs.append(_dot(p.astype(BF16), vp) / l)
                lses.append(m + jnp.log(l))
            o = jnp.where(first, outs[0], outs[1])
            o_ref[:, pp * _PAIR:(pp + 1) * _PAIR] = o
            oh_ref[:, pp * _PAIR:(pp + 1) * _PAIR] = o.astype(BF16)
        lse_ref[...] = jnp.concatenate(lses, axis=1)

    kviews = [_att_views_back(d, 1) for d in range(ATT_NV)]
    vviews = [_att_views_back(d, 2) for d in range(ATT_NV)]
    return _pcall(body, name=name, grid=(T // ATT_TQ,),
                  in_specs=[_qrows(MIX)] + kviews + vviews + [_full((HEADS, ATT_TQ, ATT_W))],
                  out_specs=[_qrows(MIX), _qrows(MIX), _qrows(HEADS)],
                  out_shape=[_sds((T, MIX), F32), _sds((T, MIX), BF16), _sds((T, HEADS), F32)],
                  sem=("parallel",))(h, *([h] * (2 * ATT_NV)), table)


def _att_bwd_q(h, do, o, lse, table, name):
    T = h.shape[0]

    def body(q_ref, *refs):
        k_refs = refs[:ATT_NV]
        v_refs = refs[ATT_NV:2 * ATT_NV]
        do_ref, o_ref, lse_ref, tb_ref, dq_ref, dl_ref, dtb_ref = refs[2 * ATT_NV:]
        i = pl.program_id(0)
        col = lax.broadcasted_iota(jnp.int32, (1, ATT_W), 1)
        kvalid = (col + (i - (ATT_NV - 1)) * ATT_TQ) >= 0

        first = lax.broadcasted_iota(jnp.int32, (1, _PAIR), 1) < HEAD_DIM

        @pl.when(i == 0)
        def _():
            dtb_ref[...] = jnp.zeros_like(dtb_ref)

        deltas = []
        for pp in range(HEADS // 2):
            qp = _att_pair(q_ref, pp) * (HEAD_DIM ** -0.5)
            kp = _att_pair(k_refs, pp).astype(BF16)
            vp = _att_pair(v_refs, pp).astype(BF16)
            dop = _att_pair(do_ref, pp)
            doo = dop * _att_pair(o_ref, pp)
            outs = []
            for e in range(2):
                hd = 2 * pp + e
                half = first if e == 0 else jnp.logical_not(first)
                s = _dot_nt(jnp.where(half, qp, 0.0).astype(BF16), kp) + tb_ref[hd]
                s = jnp.where(kvalid, s, NEG_INF)
                p = jnp.exp(s - lse_ref[:, hd:hd + 1])
                delta = jnp.sum(jnp.where(half, doo, 0.0), axis=-1, keepdims=True)
                dp = _dot_nt(jnp.where(half, dop, 0.0).astype(BF16), vp)
                ds = p * (dp - delta)
                dtb_ref[hd] += ds
                outs.append(_dot(ds.astype(BF16), kp))
                deltas.append(delta)
            dq = jnp.where(first, outs[0], outs[1]) * (HEAD_DIM ** -0.5)
            dq_ref[:, pp * _PAIR:(pp + 1) * _PAIR] = dq.astype(BF16)
        dl_ref[...] = jnp.concatenate(deltas, axis=1)

    kviews = [_att_views_back(d, 1) for d in range(ATT_NV)]
    vviews = [_att_views_back(d, 2) for d in range(ATT_NV)]
    tb = _full((HEADS, ATT_TQ, ATT_W))
    return _pcall(body, name=name, grid=(T // ATT_TQ,),
                  in_specs=[_qrows(MIX)] + kviews + vviews + [_qrows(MIX), _qrows(MIX), _qrows(HEADS), tb],
                  out_specs=[_qrows(MIX), _qrows(HEADS), tb],
                  out_shape=[_sds((T, MIX), BF16), _sds((T, HEADS), F32), _sds((HEADS, ATT_TQ, ATT_W), F32)])(
                      h, *([h] * (2 * ATT_NV)), do, o, lse, table)


def _att_table_by_key(table):
    t = table.reshape(HEADS, ATT_TQ, ATT_NV, ATT_TQ)[:, :, ::-1, :]
    return jnp.transpose(t, (0, 3, 2, 1)).reshape(HEADS, ATT_TQ, ATT_W)


def _att_bwd_kv(h, do, lse_t, delta_t, table_k, name):
    T = h.shape[0]
    nb = T // ATT_TQ

    def fwd_view(d, s=0):
        return pl.BlockSpec((ATT_TQ, MIX), lambda j: (jnp.minimum(j + d, nb - 1), s))

    def row_view(d):
        return pl.BlockSpec((HEADS, ATT_TQ), lambda j: (0, jnp.minimum(j + d, nb - 1)))

    def body(k_ref, v_ref, *refs):
        q_refs = refs[:ATT_NV]
        do_refs = refs[ATT_NV:2 * ATT_NV]
        lse_refs = refs[2 * ATT_NV:3 * ATT_NV]
        dl_refs = refs[3 * ATT_NV:4 * ATT_NV]
        tb_ref, dk_ref, dv_ref = refs[4 * ATT_NV:]
        j = pl.program_id(0)
        view = lax.broadcasted_iota(jnp.int32, (1, ATT_W), 1) // ATT_TQ
        valid = (j + view) <= nb - 1
        first = lax.broadcasted_iota(jnp.int32, (1, _PAIR), 1) < HEAD_DIM
        for pp in range(HEADS // 2):
            kp = _att_pair(k_ref, pp)
            vp = _att_pair(v_ref, pp)
            qs = (_att_pair(q_refs, pp) * (HEAD_DIM ** -0.5)).astype(BF16)
            dos = _att_pair(do_refs, pp).astype(BF16)
            dks, dvs = [], []
            for e in range(2):
                hd = 2 * pp + e
                half = first if e == 0 else jnp.logical_not(first)
                lses = jnp.concatenate([r[hd:hd + 1, :] for r in lse_refs], axis=1)
                dls = jnp.concatenate([r[hd:hd + 1, :] for r in dl_refs], axis=1)
                st = _dot_nt(jnp.where(half, kp, 0.0).astype(BF16), qs) + tb_ref[hd]
                pt = jnp.where(valid, jnp.exp(st - lses), 0.0)
                dvs.append(_dot(pt.astype(BF16), dos))
                dst = pt * (_dot_nt(jnp.where(half, vp, 0.0).astype(BF16), dos) - dls)
                dks.append(_dot(dst.astype(BF16), qs))
            dk_ref[:, pp * _PAIR:(pp + 1) * _PAIR] = jnp.where(first, dks[0], dks[1]).astype(BF16)
            dv_ref[:, pp * _PAIR:(pp + 1) * _PAIR] = jnp.where(first, dvs[0], dvs[1]).astype(BF16)

    qv = [fwd_view(d, 0) for d in range(ATT_NV)]
    dov = [fwd_view(d) for d in range(ATT_NV)]
    rows = [row_view(d) for d in range(ATT_NV)]
    return _pcall(body, name=name, grid=(nb,),
                  in_specs=[_qrows(MIX, 1), _qrows(MIX, 2)] + qv + dov + rows + rows
                  + [_full((HEADS, ATT_TQ, ATT_W))],
                  out_specs=[_qrows(MIX), _qrows(MIX)], out_shape=[_sds((T, MIX), BF16), _sds((T, MIX), BF16)],
                  sem=("parallel",))(h, h, *([h] * ATT_NV), *([do] * ATT_NV), *([lse_t] * ATT_NV),
                                     *([delta_t] * ATT_NV), table_k)


def _all_gather(x, name):
    R, C = x.shape

    def body(x_ref, out_ref, send_sems, recv_sems, local_sem):
        xi, yi, ci = lax.axis_index("x"), lax.axis_index("y"), lax.axis_index("c")
        me, sibling = (xi, yi, ci), (xi, yi, 1 - ci)
        chips = [(1 - xi, yi), (xi, 1 - yi), (1 - xi, 1 - yi)]

        def slot(px, py, pc):
            return out_ref.at[4 * px + 2 * py + pc]

        def copy(k, block, to, src=None):
            return pltpu.make_async_remote_copy(
                src_ref=slot(*block) if src is None else src, dst_ref=slot(*block),
                send_sem=send_sems.at[k], recv_sem=recv_sems.at[k], device_id=to, device_id_type=_MESH)

        mine = pltpu.make_async_copy(x_ref, slot(*me), local_sem)
        mine.start()
        first = [copy(0, me, sibling, src=x_ref)]
        first += [copy(1 + j, me, (*chip, ci), src=x_ref) for j, chip in enumerate(chips)]
        for cp in first:
            cp.start()
        passed = [copy(4 + j, (*chip, ci), sibling) for j, chip in enumerate(chips)]
        for j, chip in enumerate(chips):
            copy(1 + j, (*chip, ci), me).wait_recv()
            passed[j].start()
        copy(0, sibling, me).wait_recv()
        for j, chip in enumerate(chips):
            copy(4 + j, (*chip, 1 - ci), me).wait_recv()
        for cp in first + passed:
            cp.wait_send()
        mine.wait()

    return pl.pallas_call(
        body, name=name, out_shape=_sds((N_DEV, R, C), x.dtype), in_specs=[_ANY], out_specs=_ANY,
        scratch_shapes=[pltpu.SemaphoreType.DMA((7,)), pltpu.SemaphoreType.DMA((7,)), pltpu.SemaphoreType.DMA(())],
    )(x)


def _a2a_copies(s_ref, r_ref, send_sems, recv_sems, local_sem, gather=False):
    xi, yi, ci = lax.axis_index("x"), lax.axis_index("y"), lax.axis_index("c")
    me = 4 * xi + 2 * yi + ci

    def mine():
        return pltpu.make_async_copy(s_ref if gather else s_ref.at[me], r_ref.at[me], local_sem)

    def remote(m, sending):
        px = 1 - xi if m & 4 else xi
        py = 1 - yi if m & 2 else yi
        pc = 1 - ci if m & 1 else ci
        peer = 4 * px + 2 * py + pc
        src, dst = (s_ref.at[peer], r_ref.at[me]) if sending else (s_ref.at[me], r_ref.at[peer])
        if gather:
            src = s_ref
        return pltpu.make_async_remote_copy(src_ref=src, dst_ref=dst, send_sem=send_sems.at[m - 1],
                                            recv_sem=recv_sems.at[m - 1], device_id=(px, py, pc), device_id_type=_MESH)

    def start():
        mine().start()
        for m in range(1, N_DEV):
            remote(m, True).start()

    def wait():
        for m in range(1, N_DEV):
            remote(m, False).wait_recv()
        for m in range(1, N_DEV):
            remote(m, True).wait_send()
        mine().wait()

    return start, wait


_A2A_SEMS = [pltpu.SemaphoreType.DMA((7,)), pltpu.SemaphoreType.DMA((7,)), pltpu.SemaphoreType.DMA(())]


def _all_to_all(s, name):
    def body(s_ref, r_ref, send_sems, recv_sems, local_sem):
        start, wait = _a2a_copies(s_ref, r_ref, send_sems, recv_sems, local_sem)
        start()
        wait()

    return pl.pallas_call(
        body, name=name, out_shape=_sds(s.shape, s.dtype), in_specs=[_ANY], out_specs=_ANY,
        scratch_shapes=list(_A2A_SEMS))(s)


def _riding(body, n_in, n_out, n_ex, first, last, gather=False):
    def wrapped(*refs):
        ins = refs[:n_in]
        sends = refs[n_in:n_in + n_ex]
        outs = refs[n_in + n_ex:n_in + n_ex + n_out]
        recvs = refs[n_in + n_ex + n_out:n_in + 2 * n_ex + n_out]
        rest = refs[n_in + 2 * n_ex + n_out:]
        scratch, sems = rest[:len(rest) - 3 * n_ex], rest[len(rest) - 3 * n_ex:]

        @pl.when(first())
        def _():
            for k in range(n_ex):
                _a2a_copies(sends[k], recvs[k], *sems[3 * k:3 * k + 3], gather=gather)[0]()

        body(*ins, *outs, *scratch)

        @pl.when(last())
        def _():
            for k in range(n_ex):
                _a2a_copies(sends[k], recvs[k], *sems[3 * k:3 * k + 3], gather=gather)[1]()

    return wrapped


_ADAMW_BLOCK_BYTES = 6 * 1024 * 1024


def _adamw(parts, row_off, w, m, v, name):
    R, C = w.shape
    tr = None
    for cand in (512, 256, 128, 64, 32, 16):
        step_bytes = cand * C * (N_DEV * parts.dtype.itemsize + 7 * 4)
        if R % cand == 0 and row_off % cand == 0 and step_bytes <= _ADAMW_BLOCK_BYTES:
            tr = cand
            break
    assert tr is not None, (R, C, row_off)
    off = row_off // tr
    c1 = 1.0 - ADAM_B1 ** ADAM_STEP
    c2 = 1.0 - ADAM_B2 ** ADAM_STEP

    def body(p_ref, w_ref, m_ref, v_ref, g_ref, d_ref, mo_ref, vo_ref):
        g = p_ref[0].astype(F32)
        for j in range(1, N_DEV):
            g = g + p_ref[j].astype(F32)
        mn = ADAM_B1 * m_ref[...] + (1.0 - ADAM_B1) * g
        vn = ADAM_B2 * v_ref[...] + (1.0 - ADAM_B2) * (g * g)
        m_hat = mn / c1
        v_hat = vn / c2
        g_ref[...] = g
        d_ref[...] = -ADAM_LR * (m_hat / (jnp.sqrt(v_hat) + ADAM_EPS) + ADAM_WD * w_ref[...])
        mo_ref[...] = mn
        vo_ref[...] = vn

    blk = _rows(tr, C)
    return _pcall(body, name=name, grid=(R // tr,),
                  in_specs=[pl.BlockSpec((N_DEV, tr, C), lambda i: (0, off + i, 0)), blk, blk, blk],
                  out_specs=[blk] * 4, out_shape=[_sds((R, C), F32)] * 4,
                  sem=("parallel",))(parts, w, m, v)


def _piece_rows(size):
    return -(-size // PIECE) * 16


def _pack_rows(arrays, lead=()):
    parts = []
    total = 0
    for a in arrays:
        flat = a.reshape(lead + (-1,))
        size = flat.shape[-1]
        rows = _piece_rows(size)
        pad = [(0, 0)] * len(lead) + [(0, rows * FLAT_COLS - size)]
        parts.append(jnp.pad(flat, pad).reshape(lead + (rows, FLAT_COLS)))
        total += rows
    tail = -total % FLAT_ROWS
    if tail:
        parts.append(jnp.zeros(lead + (tail, FLAT_COLS), parts[0].dtype))
    return jnp.concatenate(parts, axis=len(lead))


def _unpack_rows(buf, shapes, lead=()):
    out = []
    row = 0
    nl = len(lead)
    for shape in shapes:
        size = math.prod(shape)
        rows = _piece_rows(size)
        piece = lax.slice_in_dim(buf, row, row + rows, axis=nl).reshape(lead + (rows * FLAT_COLS,))
        out.append(lax.slice_in_dim(piece, 0, size, axis=nl).reshape(lead + tuple(shape)))
        row += rows
    return out


def _to_blocks(full, axis):
    shp = full.shape
    split = full.reshape(shp[:axis] + (N_DEV, shp[axis] // N_DEV) + shp[axis + 1:])
    return jnp.moveaxis(split, axis, 0)


def _from_blocks(blocks, axis):
    shp = blocks.shape[1:]
    moved = jnp.moveaxis(blocks, 0, axis)
    return moved.reshape(shp[:axis] + (N_DEV * shp[axis],) + shp[axis + 1:])


N_LAYERS = {n: (DEPTH // 2 if n.startswith(('ev_', 'od_')) else DEPTH) for n in WEIGHT_NAMES}
R_OUT = D_MODEL // N_DEV
R_DOWN = D_FF // N_DEV


def _layer_shards(W, i):
    j = i // 2
    ev = i % 2 == 0
    rows = [W['ev_w_out' if ev else 'od_w_out'][j], W['ffn_w_down'][i], W['ple_w_gate'][i]]
    return dict(up=W['ffn_w_up'][i].astype(BF16), inn=W['ev_w_in' if ev else 'od_w_in'][j].astype(BF16),
                rows=jnp.concatenate(rows, axis=0).astype(BF16))


def _layer_weights(up, inn, rows):
    return dict(up=up.reshape(2, FF_J, 1, D_MODEL, FF_NB), win=inn.reshape(N_DEV, 1, D_MODEL, IN_NB),
                wo=lax.slice_in_dim(rows, 0, R_OUT, axis=1).reshape(1, D_MODEL, D_MODEL),
                wd4=lax.slice_in_dim(rows, R_OUT, R_OUT + R_DOWN, axis=1).reshape(1, FF_J, FF_NB, D_MODEL),
                gate=lax.slice_in_dim(rows, R_OUT + R_DOWN, 2 * R_OUT + R_DOWN, axis=1).reshape(1, D_MODEL, D_MODEL))


def _layer_weights_of(W, i):
    j = i // 2
    ev = i % 2 == 0
    return dict(up=W['ffn_w_up'][:, i:i + 1].reshape(2, FF_J, 1, D_MODEL, FF_NB),
                win=W['ev_w_in' if ev else 'od_w_in'][:, j:j + 1],
                wo=W['ev_w_out' if ev else 'od_w_out'][j:j + 1],
                wd4=W['ffn_w_down'][i:i + 1].reshape(1, FF_J, FF_NB, D_MODEL), gate=W['ple_w_gate'][i:i + 1])


def _row_blocks(g):
    return g.reshape(N_DEV, g.shape[0] // N_DEV, g.shape[1])


def _early_sends(G, i):
    ffn = jnp.concatenate([_row_blocks(G['ffn_w_down'][i]), _row_blocks(G['ple_w_gate'][i])], axis=1)
    return dict(up=G['ffn_w_up'][i], ffn=ffn)


def _late_sends(G, i):
    j = i // 2
    ev = i % 2 == 0
    return dict(out=_row_blocks(G['ev_w_out' if ev else 'od_w_out'][j]), inn=G['ev_w_in' if ev else 'od_w_in'][j])


def _local_step(x, p, tgt, W, inn0=None, shards=None):
    overlap = shards is not None
    big = [] if overlap else [_layer_weights_of(W, i) for i in range(DEPTH)]
    ln = {n: _vecs(W[n]) for n in ('ln_mix_g', 'ln_mix_b', 'ln_ffn_g', 'ln_ffn_b', 'ple_b_gate')}
    bglu = _vecs(W['ev_b_glu'])
    pscale = _vecs(W['od_pool_scale'])
    saved = []
    x0 = x
    x0h = x.astype(BF16)
    for i in range(DEPTH):
        L = f"L{i}_"
        j = i // 2
        if overlap and i == 0:
            h, got = _mm_in(x0h, inn0.reshape(N_DEV, 1, D_MODEL, IN_NB), 0, L + "mm_in",
                            ride=(shards[0]['up'], shards[0]['rows']))
            big.append(_layer_weights(got[0], inn0, got[1]))
        else:
            h = _mm_in(x0h, big[i]['win'], 0, L + "mm_in")
        B = big[i]
        s = dict(x0=x0, x0h=x0h, B=B)
        if i % 2 == 0:
            params = tuple(W[n][j] for n in ('ev_lambda_re', 'ev_lambda_im', 'ev_log_dt', 'ev_b_re', 'ev_b_im',
                                             'ev_c_re', 'ev_c_im', 'ev_d'))
            (r0, qt, pm, a1, a2), op_vjp = jax.vjp(_s5_operators, *params)
            umat = _to_chunks(h, L + "to_chunks")
            st = jnp.transpose(_s5_chunk_state(umat, pm, L + "s5_state"), (1, 0, 2))
            xp_t = _s5_scan_fwd(st, a1, a2, L + "s5_scan")
            xprev = jnp.transpose(xp_t, (1, 0, 2))
            y = _from_chunks(_s5_output(umat, xprev, r0, qt, L + "s5_out"), L + "from_chunks")
            ya, gh = _glu_fwd(y, W['ev_w_glu'], bglu, j, L + "glu")
            yb = _conv_fwd(h, W['ev_conv_w'], j, L + "conv")
            s.update(op_vjp=op_vjp, r0=r0, qt=qt, pm=pm, a1=a1, a2=a2, umat=umat, xp_t=xp_t, xprev=xprev, y=y, gh=gh)
        else:
            table, tb_vjp = jax.vjp(_att_bias_table, W['od_rel_bias'][j])
            of, ya, lse = _att_fwd(h, table, L + "att")
            yb = _pool_fwd(h, W['od_pool_w'], pscale, j, L + "pool")
            s.update(table=table, tb_vjp=tb_vjp, of=of, lse=lse)
        g1, b1 = (ln['ln_mix_g'], i), (ln['ln_mix_b'], i)
        g2, b2 = (ln['ln_ffn_g'], i), (ln['ln_ffn_b'], i)
        xhat1, rstd1, x1h = _mm_out_ln(ya, yb, B['wo'], 0, x0, g1, b1, L + "mm_out_ln")
        nxt = shards[i + 1] if overlap and i + 1 < DEPTH else None
        (hf, a3), got_a = _mm_up(x1h, B['up'], 0, L + "mm_up", ride=(nxt['up'], nxt['inn']) if nxt else ())
        (xhat2, rstd2, x2h), got_b = _mm_down_ln(a3, B['wd4'], 0, xhat1, g1, b1, g2, b2, L + "mm_down_ln",
                                                 ride=(nxt['rows'],) if nxt else ())
        if nxt:
            big.append(_layer_weights(got_a[0], got_a[1], got_b[0]))
        x3, x3h = _mm_ple(x2h, xhat2, ln['ln_ffn_g'], ln['ln_ffn_b'], p, B['gate'], ln['ple_b_gate'],
                          W['ple_w_proj'], i, L + "mm_ple", lw=0)
        s.update(h=h, ya=ya, yb=yb, xhat1=xhat1, rstd1=rstd1, x1h=x1h, hf=hf, a3=a3, xhat2=xhat2,
                 rstd2=rstd2, x2h=x2h, g1=g1)
        saved.append(s)
        x0, x0h = x3, x3h

    dx, loss = _loss_head(x0, tgt, "loss_head")

    G = {n: [None] * N_LAYERS[n] for n in WEIGHT_NAMES}
    landed = {i: {} for i in range(DEPTH)}
    pending = None
    for i in reversed(range(DEPTH)):
        L = f"L{i}_"
        j = i // 2
        s = saved[i]
        B = s['B']
        dr2, dr2h, dpreh, dpph, dbg, dg2, db2 = _ple_bwd(
            dx, s['x2h'], p, B['gate'], ln['ple_b_gate'], W['ple_w_proj'], i, s['xhat2'], s['rstd2'],
            ln['ln_ffn_g'], L + "ple_bwd", lw=0)
        G['ple_w_gate'][i] = _mm_tn(s['x2h'], dpreh, L + "dw_gate")
        G['ple_w_proj'][i] = _mm_tn(p, dpph, L + "dw_proj", a_layer=i)
        G['ple_b_gate'][i] = dbg[0]
        G['ln_ffn_g'][i] = dg2[0]
        G['ln_ffn_b'][i] = db2[0]
        dhf, got = _ffn_bwd1(dr2h, B['wd4'], 0, s['hf'], L + "ffn_bwd1",
                             ride=(pending['out'], pending['inn']) if pending else ())
        if pending:
            landed[i + 1].update(out=got[0], inn=got[1])
        G['ffn_w_down'][i] = _mm_tn_ablk(s['a3'], dr2h, L + "dw_down").reshape(D_FF, D_MODEL)
        T = dhf.shape[2]
        G['ffn_w_up'][i] = _mm_tn_bblk(s['x1h'], dhf.reshape(N_DEV, T, FF_NB), L + "dw_up")
        early = _early_sends(G, i) if overlap else None
        (dr1, dr1h, dg1, db1), got = _ffn_bwd2(dhf, B['up'], 0, dr2, s['xhat1'], s['rstd1'], s['g1'], L + "ffn_bwd2",
                                               ride=(early['up'], early['ffn']) if overlap else ())
        if overlap:
            landed[i].update(up=got[0], ffn=got[1])
        G['ln_mix_g'][i] = dg1[0]
        G['ln_mix_b'][i] = db1[0]
        dya, dyb = _out_bwd(dr1h, B['wo'], 0, L + "out_bwd")
        dwo = jnp.concatenate([_mm_tn(s['ya'], dr1h, L + "dw_out_a"), _mm_tn(s['yb'], dr1h, L + "dw_out_b")], axis=0)
        if i % 2 == 0:
            G['ev_w_out'][j] = dwo
            dy, dzh, dbglu = _glu_bwd(s['y'], dya, W['ev_w_glu'], bglu, j, L + "glu_bwd")
            G['ev_w_glu'][j] = _mm_tn(s['gh'], dzh, L + "dw_glu")
            G['ev_b_glu'][j] = dbglu[0]
            dymat = _to_chunks(dy, L + "to_chunks_dy")
            dxp_t = jnp.transpose(_s5_bwd_state(dymat, s['qt'], L + "s5_bwd_state"), (1, 0, 2))
            ds_t, da1, da2 = _s5_scan_bwd(dxp_t, s['xp_t'], s['a1'], s['a2'], L + "s5_scan_bwd")
            dumat, dr0, dqt, dpm = _s5_bwd_main(s['umat'], s['xprev'], dymat, jnp.transpose(ds_t, (1, 0, 2)),
                                                s['r0'], s['pm'], L + "s5_bwd")
            dparams = s['op_vjp']((dr0, dqt, dpm, da1, da2))
            for n, dpar in zip(('ev_lambda_re', 'ev_lambda_im', 'ev_log_dt', 'ev_b_re', 'ev_b_im', 'ev_c_re',
                                'ev_c_im', 'ev_d'), dparams):
                G[n][j] = dpar
            dua = _from_chunks(dumat, L + "from_chunks_du")
            dh, dcw = _conv_bwd(dyb, s['h'], dua, W['ev_conv_w'], j, L + "conv_bwd")
            G['ev_conv_w'][j] = dcw[:3]
            wname = 'ev_w_in'
        else:
            G['od_w_out'][j] = dwo
            dq, delta, dtable = _att_bwd_q(s['h'], dya, s['of'], s['lse'], s['table'], L + "att_bwd_q")
            dk, dv = _att_bwd_kv(s['h'], dya, s['lse'].T, delta.T, _att_table_by_key(s['table']), L + "att_bwd_kv")
            G['od_rel_bias'][j] = s['tb_vjp'](dtable)[0]
            dh, dpw, dsc = _pool_bwd(dyb, s['h'], dq, dk, dv, W['od_pool_w'], pscale, j, L + "pool_bwd")
            G['od_pool_w'][j] = dpw
            G['od_pool_scale'][j] = dsc[0]
            wname = 'od_w_in'
        G[wname][j] = _mm_tn(s['x0h'], dh, L + "dw_in", blocked_n=IN_NB)
        dx = _in_bwd(dh, B['win'], 0, dr1, L + "in_bwd")
        pending = _late_sends(G, i) if overlap else None

    if overlap:
        return loss, dx, G, landed, pending
    return loss, dx, G


def _slab(a):
    return a.reshape(-1, a.shape[-1])


def _gather_small(W):
    full = {n: W[n] for n in REPLICATED}
    got = _all_gather(_pack_rows([W[n].astype(BF16) for n in SMALL_SHARDED]), "gather_w_small")
    shapes = [W[n].shape for n in SMALL_SHARDED]
    for n, blocks in zip(SMALL_SHARDED, _unpack_rows(got, shapes, lead=(N_DEV,))):
        full[n] = _from_blocks(blocks, SHARD_AXIS[n])
    for n in ('ev_conv_w', 'od_pool_scale'):
        full[n] = full[n].astype(F32)
    return full


def _step(x, p, tgt, W, M, V):
    shards = [_layer_shards(W, i) for i in range(DEPTH)]
    inn0 = _all_gather(shards[0]['inn'], "gather_w0_inn")
    loss, dx, G, landed, tail = _local_step(x[0], p[:, 0], tgt[0], _gather_small(W), inn0=inn0, shards=shards)
    landed[0].update({k: _all_to_all(v, "scatter_g_" + k) for k, v in tail.items()})
    res = {}

    def update(parts, n):
        shape = W[n].shape
        outs = _adamw(parts, 0, _slab(W[n]), _slab(M[n]), _slab(V[n]), "adamw_" + n)
        for kind, a in zip(('grad', 'delta', 'm', 'v'), outs):
            res[kind, n] = a.reshape(shape)

    def over_layers(layers, key, lo=None, hi=None):
        got = [landed[i][key] for i in layers]
        if lo is not None:
            got = [lax.slice_in_dim(g, lo, hi, axis=1) for g in got]
        return jnp.concatenate(got, axis=1)

    even, odd, every = (0, 2), (1, 3), (0, 1, 2, 3)
    update(over_layers(even, 'out'), 'ev_w_out')
    update(over_layers(odd, 'out'), 'od_w_out')
    update(over_layers(every, 'ffn', 0, R_DOWN), 'ffn_w_down')
    update(over_layers(every, 'ffn', R_DOWN, R_DOWN + R_OUT), 'ple_w_gate')
    update(over_layers(every, 'up'), 'ffn_w_up')
    update(over_layers(even, 'inn'), 'ev_w_in')
    update(over_layers(odd, 'inn'), 'od_w_in')

    shapes = [W[n].shape for n in SMALL_SHARDED]
    send = _pack_rows([_to_blocks(jnp.stack(G[n], axis=0), SHARD_AXIS[n]).astype(BF16) for n in SMALL_SHARDED],
                      lead=(N_DEV,))
    parts = _all_to_all(send, "scatter_g_small")
    outs = _adamw(parts, 0, _pack_rows([W[n] for n in SMALL_SHARDED]), _pack_rows([M[n] for n in SMALL_SHARDED]),
                  _pack_rows([V[n] for n in SMALL_SHARDED]), "adamw_small")
    for kind, buf in zip(('grad', 'delta', 'm', 'v'), outs):
        for n, a in zip(SMALL_SHARDED, _unpack_rows(buf, shapes)):
            res[kind, n] = a

    repl_shapes = [W[n].shape for n in REPLICATED]
    small = _pack_rows([jnp.stack(G[n], axis=0) for n in REPLICATED] + [loss])
    parts = _all_gather(small, "gather_g_replicated")
    zero = jnp.zeros((1, 1), F32)
    outs = _adamw(parts, 0, _pack_rows([W[n] for n in REPLICATED] + [zero]),
                  _pack_rows([M[n] for n in REPLICATED] + [zero]),
                  _pack_rows([V[n] for n in REPLICATED] + [zero]), "adamw_replicated")
    for kind, buf in zip(('grad', 'delta', 'm', 'v'), outs):
        arrays = _unpack_rows(buf, repl_shapes + [(1, 1)])
        for n, a in zip(REPLICATED, arrays):
            res[kind, n] = a
        if kind == 'grad':
            total_loss = arrays[-1].reshape(())

    out = [total_loss, dx[None]]
    for kind in ('grad', 'delta', 'm', 'v'):
        out += [res[kind, n] for n in WEIGHT_NAMES]
    return tuple(out)


def kernel(x, p, ev_w_in, ev_lambda_re, ev_lambda_im, ev_log_dt, ev_b_re, ev_b_im, ev_c_re, ev_c_im, ev_d, ev_w_glu, ev_b_glu, ev_conv_w, ev_w_out, od_w_in, od_rel_bias, od_pool_w, od_pool_scale, od_w_out, ln_mix_g, ln_mix_b, ln_ffn_g, ln_ffn_b, ffn_w_up, ffn_w_down, ple_w_proj, ple_w_gate, ple_b_gate, loss_target, m_ev_w_in, m_ev_lambda_re, m_ev_lambda_im, m_ev_log_dt, m_ev_b_re, m_ev_b_im, m_ev_c_re, m_ev_c_im, m_ev_d, m_ev_w_glu, m_ev_b_glu, m_ev_conv_w, m_ev_w_out, m_od_w_in, m_od_rel_bias, m_od_pool_w, m_od_pool_scale, m_od_w_out, m_ln_mix_g, m_ln_mix_b, m_ln_ffn_g, m_ln_ffn_b, m_ffn_w_up, m_ffn_w_down, m_ple_w_proj, m_ple_w_gate, m_ple_b_gate, v_ev_w_in, v_ev_lambda_re, v_ev_lambda_im, v_ev_log_dt, v_ev_b_re, v_ev_b_im, v_ev_c_re, v_ev_c_im, v_ev_d, v_ev_w_glu, v_ev_b_glu, v_ev_conv_w, v_ev_w_out, v_od_w_in, v_od_rel_bias, v_od_pool_w, v_od_pool_scale, v_od_w_out, v_ln_mix_g, v_ln_mix_b, v_ln_ffn_g, v_ln_ffn_b, v_ffn_w_up, v_ffn_w_down, v_ple_w_proj, v_ple_w_gate, v_ple_b_gate):
    given = dict(locals())
    W = {n: given[n] for n in WEIGHT_NAMES}
    M = {n: given["m_" + n] for n in WEIGHT_NAMES}
    V = {n: given["v_" + n] for n in WEIGHT_NAMES}
    return _step(x, p, loss_target, W, M, V)
```
